```python
import math
import jax, jax.numpy as jnp
from jax import lax
import numpy as np

D_MODEL = 1024
BATCH = 32
SEQ = 256
DEPTH = 1
DEC_BATCH = 4
DEC_SEQ = 2048
PAST_LEN = 256

GRID_W = 64
D_MIX = D_MODEL
D_A = D_MIX // 2
D_P = D_MIX - D_A
H_A = 4
DK = D_A // H_A
DV = D_A // H_A
CONV_K = 5
CHUNK = 64
POOL_WINDOWS = (2, 4, 8, 16)
N_PG = len(POOL_WINDOWS)
PG = D_P // N_PG
IN_W = 4 * D_A + 4 * H_A + D_P
N_EXPERTS = 256
TOP_K = 8
N_GROUPS = 8
TOPK_GROUP = 4
D_EXPERT = 256
D_SHARED = 256
ROUTED_SCALE = 2.5
MOE_BLOCK = 128
EPS = 1e-6

kernel_name = 'hybrid_deltanet_pool_moe_diffusion_step'


def rmsnorm(x, g):
    x32 = x.astype(jnp.float32)
    y = x32 * lax.rsqrt(jnp.mean(x32 * x32, axis=-1, keepdims=True) + EPS)
    return (y * g.astype(jnp.float32)).astype(x.dtype)


def l2norm(x):
    return x * lax.rsqrt(jnp.sum(x * x, axis=-1, keepdims=True) + EPS)


def short_conv(u, w):
    C = u.shape[-1]
    y = lax.conv_general_dilated(u, w[:, None, :].astype(u.dtype), window_strides=(1,),
                                 padding=[(CONV_K // 2, CONV_K // 2)],
                                 dimension_numbers=('NWC', 'WIO', 'NWC'), feature_group_count=C)
    return jax.nn.silu(y)


def gated_delta_chunked(q, k, v, g, beta, s0):
    B, H, L, _ = q.shape
    N = L // CHUNK
    def blk(t):
        return t.reshape((B, H, N, CHUNK) + t.shape[3:])
    q, k, v, g, beta = blk(q), blk(k), blk(v), blk(g), blk(beta)
    g = jnp.cumsum(g, axis=-1)
    idx = jnp.arange(CHUNK)
    tril = idx[:, None] >= idx[None, :]
    strict = idx[:, None] > idx[None, :]
    diff = g[..., :, None] - g[..., None, :]
    decay = jnp.where(tril, jnp.exp(jnp.where(tril, diff, 0.0)), 0.0)
    kb = k * beta[..., None]
    a = jnp.where(strict, jnp.einsum('bhnik,bhnjk->bhnij', kb, k) * decay, 0.0)
    rhs = jnp.concatenate([v * beta[..., None], kb * jnp.exp(g)[..., None]], axis=-1)
    sol = lax.linalg.triangular_solve(a, rhs, left_side=True, lower=True, unit_diagonal=True)
    u, w = sol[..., :DV], sol[..., DV:]
    attn = jnp.where(tril, jnp.einsum('bhnik,bhnjk->bhnij', q, k) * decay, 0.0)
    q_dec = q * jnp.exp(g)[..., None]
    g_last = g[..., -1]
    k_dec = k * jnp.exp(g_last[..., None] - g)[..., None]

    def step(s, xs):
        u_n, w_n, attn_n, qd_n, kd_n, gl_n = xs
        v_new = u_n - jnp.einsum('bhck,bhkv->bhcv', w_n, s)
        o_n = jnp.einsum('bhck,bhkv->bhcv', qd_n, s) + jnp.einsum('bhij,bhjv->bhiv', attn_n, v_new)
        s = s * jnp.exp(gl_n)[..., None, None] + jnp.einsum('bhck,bhcv->bhkv', kd_n, v_new)
        return s, o_n

    xs = (jnp.moveaxis(u, 2, 0), jnp.moveaxis(w, 2, 0), jnp.moveaxis(attn, 2, 0),
          jnp.moveaxis(q_dec, 2, 0), jnp.moveaxis(k_dec, 2, 0), jnp.moveaxis(g_last, 2, 0))
    s_fin, o = lax.scan(step, s0, xs)
    o = jnp.moveaxis(o, 0, 2).reshape(B, H, L, DV)
    return o, s_fin


def deltanet_mixer(qkv, z, b, a, conv_w, a_log, dt_bias, onorm_g, s0):
    f32 = jnp.float32
    B, L, _ = qkv.shape
    qkv = short_conv(qkv, conv_w).astype(f32)
    q, k, v = jnp.split(qkv, 3, axis=-1)
    def heads(t):
        return t.reshape(B, L, H_A, -1).transpose(0, 2, 1, 3)
    q = l2norm(heads(q)) * (DK ** -0.5)
    k = l2norm(heads(k))
    v = heads(v)
    beta = jax.nn.sigmoid(b.astype(f32)).reshape(B, L, 2, H_A).transpose(2, 0, 3, 1)
    a_in = a.astype(f32).reshape(B, L, 2, H_A).transpose(2, 0, 3, 1)
    g = -jnp.exp(a_log.astype(f32))[:, None, :, None] * jax.nn.softplus(
        a_in + dt_bias.astype(f32)[:, None, :, None])
    s0 = s0.astype(f32)
    o_f, s_f = gated_delta_chunked(q, k, v, g[0], beta[0], s0[:, 0])
    def flip(t):
        return jnp.flip(t, axis=2)
    o_b, s_b = gated_delta_chunked(flip(q), flip(k), flip(v), flip(g[1]), flip(beta[1]), s0[:, 1])
    o = (o_f + flip(o_b)).transpose(0, 2, 1, 3)
    o = rmsnorm(o, onorm_g) * jax.nn.silu(z.astype(f32)).reshape(B, L, H_A, DV)
    return o.reshape(B, L, D_A), jnp.stack([s_f, s_b], axis=1)


def box_sum(u, w, axis):
    L = u.shape[axis]
    cs = jnp.cumsum(u, axis=axis)
    cs = jnp.concatenate([jnp.zeros_like(lax.slice_in_dim(cs, 0, 1, axis=axis)), cs], axis=axis)
    t = jnp.arange(L)
    lo = jnp.clip(t - w // 2, 0, L)
    hi = jnp.clip(t + w - w // 2, 0, L)
    s = jnp.take(cs, hi, axis=axis) - jnp.take(cs, lo, axis=axis)
    return s, (hi - lo).astype(u.dtype)


def pool_mixer(u, pool_w, pool_scale, grid):
    f32 = jnp.float32
    B, L, _ = u.shape
    u32 = u.astype(f32)
    outs = []
    for i, w in enumerate(POOL_WINDOWS):
        ui = u32[..., i * PG:(i + 1) * PG]
        if grid:
            rows = L // GRID_W
            ug = ui.reshape(B, rows, GRID_W, PG)
            s, cr = box_sum(ug, w, 1)
            s, cc = box_sum(s, w, 2)
            mean = (s / (cr[None, :, None, None] * cc[None, None, :, None])).reshape(B, L, PG)
        else:
            s, cnt = box_sum(ui, w, 1)
            mean = s / cnt[None, :, None]
        outs.append(jnp.einsum('blc,cd->bld', mean - ui, pool_w[i].astype(f32)))
    return jnp.concatenate(outs, axis=-1) * pool_scale.astype(f32)


def moe_ffn(h, router_w, router_bias, w_gate, w_up, w_down, sh_gate, sh_up, sh_down):
    f32 = jnp.float32
    B, L, D = h.shape
    T = B * L
    hf = h.reshape(T, D)
    scores = jax.nn.sigmoid(hf.astype(f32) @ router_w.astype(f32))
    sel = scores + router_bias.astype(f32)
    grp = sel.reshape(T, N_GROUPS, N_EXPERTS // N_GROUPS)
    grp_score = lax.top_k(grp, 2)[0].sum(-1)
    _, top_g = lax.top_k(grp_score, TOPK_GROUP)
    gmask = jnp.any(top_g[:, :, None] == jnp.arange(N_GROUPS)[None, None, :], axis=1)
    emask = jnp.repeat(gmask, N_EXPERTS // N_GROUPS, axis=1)
    _, idx = lax.top_k(jnp.where(emask, sel, -jnp.inf), TOP_K)
    wts = jnp.take_along_axis(scores, idx, axis=1)
    wts = wts / jnp.sum(wts, axis=-1, keepdims=True) * ROUTED_SCALE
    TK = T * TOP_K
    flat_e = idx.reshape(-1)
    order = jnp.argsort(flat_e)
    sorted_e = flat_e[order]
    tok = (order // TOP_K).astype(jnp.int32)
    counts = jnp.bincount(flat_e, length=N_EXPERTS)
    padded = (counts + MOE_BLOCK - 1) // MOE_BLOCK * MOE_BLOCK
    pad_end = jnp.cumsum(padded)
    pad_start = pad_end - padded
    seg_start = jnp.cumsum(counts) - counts
    dest = pad_start[sorted_e] + jnp.arange(TK) - seg_start[sorted_e]
    n_pad = -(-TK // MOE_BLOCK) * MOE_BLOCK + N_EXPERTS * MOE_BLOCK
    n_blk = n_pad // MOE_BLOCK
    buf_tok = jnp.full((n_pad,), T, jnp.int32).at[dest].set(tok)
    buf_w = jnp.zeros((n_pad,), f32).at[dest].set(wts.reshape(-1)[order])
    blk_e = jnp.minimum(jnp.searchsorted(pad_end, jnp.arange(n_blk) * MOE_BLOCK, side='right'),
                        N_EXPERTS - 1)
    h_pad = jnp.concatenate([hf, jnp.zeros((1, D), hf.dtype)], axis=0)

    def expert_block(args):
        t_b, e = args
        xb = h_pad[t_b]
        return (jax.nn.silu(xb @ w_gate[e]) * (xb @ w_up[e])) @ w_down[e]

    y = lax.map(expert_block, (buf_tok.reshape(n_blk, MOE_BLOCK), blk_e)).reshape(n_pad, D)
    routed = jax.ops.segment_sum(y * buf_w[:, None].astype(y.dtype), buf_tok, num_segments=T + 1)[:T]
    shared = (jax.nn.silu(hf @ sh_gate) * (hf @ sh_up)) @ sh_down
    return (routed + shared).reshape(B, L, D)


def trunk_layer(x, mod, s0, grid, norm1_g, w_in, conv_w, a_log, dt_bias, onorm_g, pool_w, pool_scale,
                w_out, norm2_g, router_w, router_bias, w_gate, w_up, w_down, sh_gate, sh_up, sh_down):
    shift1, scale1, gate1, shift2, scale2, gate2 = jnp.split(mod, 6, axis=-1)
    h = rmsnorm(x, norm1_g) * (1 + scale1) + shift1
    proj = h @ w_in
    qkv, z, b, a, u = jnp.split(proj, [3 * D_A, 4 * D_A, 4 * D_A + 2 * H_A, 4 * D_A + 4 * H_A], axis=-1)
    o_a, state = deltanet_mixer(qkv, z, b, a, conv_w, a_log, dt_bias, onorm_g, s0)
    o_p = pool_mixer(u, pool_w, pool_scale, grid)
    mix = jnp.concatenate([o_a.astype(x.dtype), o_p.astype(x.dtype)], axis=-1) @ w_out
    x = x + gate1 * mix
    h = rmsnorm(x, norm2_g) * (1 + scale2) + shift2
    x = x + gate2 * moe_ffn(h, router_w, router_bias, w_gate, w_up, w_down, sh_gate, sh_up, sh_down)
    return x, state


def setup_inputs(seed: int = 0) -> dict:
    key = jax.random.key(seed)
    ks = jax.random.split(key, 26)
    f32 = jnp.float32
    def nrm(k, shape, s):
        return jax.random.normal(k, shape, f32) * s
    dt = jnp.exp(jax.random.uniform(ks[10], (DEPTH, 2, H_A), f32, math.log(1e-3), math.log(1e-1)))
    return {
        'x_prompt': nrm(ks[0], (BATCH, SEQ, D_MODEL), 1.0),
        'x_sample': nrm(ks[1], (DEC_BATCH, DEC_SEQ, D_MODEL), 1.0),
        'state_delta': nrm(ks[2], (DEC_BATCH, DEPTH, 2, H_A, DK, DV), 0.1),
        'c': nrm(ks[3], (DEC_BATCH, D_MODEL), 1.0),
        'c_ctx': nrm(ks[4], (D_MODEL,), 1.0),
        'w_ada': nrm(ks[5], (DEPTH, D_MODEL, 6 * D_MODEL), 0.5 * D_MODEL ** -0.5),
        'b_ada': nrm(ks[6], (DEPTH, 6 * D_MODEL), 0.01),
        'norm1_g': 1.0 + nrm(ks[7], (DEPTH, D_MODEL), 0.02),
        'w_in': nrm(ks[8], (DEPTH, D_MODEL, IN_W), D_MODEL ** -0.5),
        'conv_w': nrm(ks[9], (DEPTH, CONV_K, 3 * D_A), CONV_K ** -0.5),
        'a_log': jnp.log(jax.random.uniform(ks[11], (DEPTH, 2, H_A), f32, 1.0, 16.0)),
        'dt_bias': dt + jnp.log(-jnp.expm1(-dt)),
        'onorm_g': 1.0 + nrm(ks[12], (DEPTH, DV), 0.02),
        'pool_w': nrm(ks[13], (DEPTH, N_PG, PG, PG), PG ** -0.5),
        'pool_scale': 1.0 + nrm(ks[14], (DEPTH, D_P), 0.02),
        'w_out': nrm(ks[15], (DEPTH, D_MIX, D_MODEL), D_MIX ** -0.5),
        'norm2_g': 1.0 + nrm(ks[16], (DEPTH, D_MODEL), 0.02),
        'router_w': nrm(ks[17], (DEPTH, D_MODEL, N_EXPERTS), D_MODEL ** -0.5),
        'router_bias': nrm(ks[18], (DEPTH, N_EXPERTS), 0.01),
        'exp_w_gate': nrm(ks[19], (DEPTH, N_EXPERTS, D_MODEL, D_EXPERT), D_MODEL ** -0.5),
        'exp_w_up': nrm(ks[20], (DEPTH, N_EXPERTS, D_MODEL, D_EXPERT), D_MODEL ** -0.5),
        'exp_w_down': nrm(ks[21], (DEPTH, N_EXPERTS, D_EXPERT, D_MODEL), D_EXPERT ** -0.5),
        'sh_w_gate': nrm(ks[22], (DEPTH, D_MODEL, D_SHARED), D_MODEL ** -0.5),
        'sh_w_up': nrm(ks[23], (DEPTH, D_MODEL, D_SHARED), D_MODEL ** -0.5),
        'sh_w_down': nrm(ks[24], (DEPTH, D_SHARED, D_MODEL), D_SHARED ** -0.5),
        'final_g': 1.0 + nrm(ks[25], (D_MODEL,), 0.02),
    }


def reference(x_prompt, x_sample, state_delta, c, c_ctx, w_ada, b_ada, norm1_g, w_in, conv_w, a_log,
              dt_bias, onorm_g, pool_w, pool_scale, w_out, norm2_g, router_w, router_bias, exp_w_gate,
              exp_w_up, exp_w_down, sh_w_gate, sh_w_up, sh_w_down, final_g):
    xp = x_prompt
    xs = x_sample
    new_states = []
    for l in range(DEPTH):
        lw = (norm1_g[l], w_in[l], conv_w[l], a_log[l], dt_bias[l], onorm_g[l], pool_w[l], pool_scale[l],
              w_out[l], norm2_g[l], router_w[l], router_bias[l], exp_w_gate[l], exp_w_up[l], exp_w_down[l],
              sh_w_gate[l], sh_w_up[l], sh_w_down[l])
        mod_ctx = (jax.nn.silu(c_ctx) @ w_ada[l] + b_ada[l])[None, None, :]
        mod_lat = (jax.nn.silu(c) @ w_ada[l] + b_ada[l])[:, None, :]
        s0_ctx = jnp.zeros((xp.shape[0], 2, H_A, DK, DV), jnp.float32)
        xp, st_ctx = trunk_layer(xp, mod_ctx, s0_ctx, False, *lw)
        xs, _ = trunk_layer(xs, mod_lat, state_delta[:, l], True, *lw)
        new_states.append(st_ctx.astype(x_prompt.dtype))
    y_prompt = rmsnorm(xp, final_g)
    y_sample = rmsnorm(xs, final_g)
    new_state_delta = jnp.stack(new_states, axis=1)
    return (y_prompt, y_sample, new_state_delta)
```

```python
import math
import jax, jax.numpy as jnp
from jax import lax
import numpy as np
from jax.experimental import pallas as pl
from jax.experimental.pallas import tpu as pltpu

D_MODEL = 1024
DEPTH = 1
GRID_W = 64
D_MIX = D_MODEL
D_A = D_MIX // 2
D_P = D_MIX - D_A
H_A = 4
DK = D_A // H_A
DV = D_A // H_A
CONV_K = 5
CHUNK = 64
POOL_WINDOWS = (2, 4, 8, 16)
N_PG = len(POOL_WINDOWS)
PG = D_P // N_PG
N_EXPERTS = 256
TOP_K = 8
N_GROUPS = 8
TOPK_GROUP = 4
ROUTED_SCALE = 2.5
MOE_BLOCK = 128
EPS = 1e-6


def rmsnorm(x, g):
    x32 = x.astype(jnp.float32)
    y = x32 * lax.rsqrt(jnp.mean(x32 * x32, axis=-1, keepdims=True) + EPS)
    return (y * g.astype(jnp.float32)).astype(x.dtype)


def l2norm(x):
    return x * lax.rsqrt(jnp.sum(x * x, axis=-1, keepdims=True) + EPS)


def short_conv(u, w):
    C = u.shape[-1]
    y = lax.conv_general_dilated(u, w[:, None, :].astype(u.dtype), window_strides=(1,),
                                 padding=[(CONV_K // 2, CONV_K // 2)],
                                 dimension_numbers=('NWC', 'WIO', 'NWC'), feature_group_count=C)
    return jax.nn.silu(y)


def gated_delta_chunked(q, k, v, g, beta, s0):
    B, H, L, _ = q.shape
    N = L // CHUNK
    def blk(t):
        return t.reshape((B, H, N, CHUNK) + t.shape[3:])
    q, k, v, g, beta = blk(q), blk(k), blk(v), blk(g), blk(beta)
    g = jnp.cumsum(g, axis=-1)
    idx = jnp.arange(CHUNK)
    tril = idx[:, None] >= idx[None, :]
    strict = idx[:, None] > idx[None, :]
    diff = g[..., :, None] - g[..., None, :]
    decay = jnp.where(tril, jnp.exp(jnp.where(tril, diff, 0.0)), 0.0)
    kb = k * beta[..., None]
    a = jnp.where(strict, jnp.einsum('bhnik,bhnjk->bhnij', kb, k) * decay, 0.0)
    rhs = jnp.concatenate([v * beta[..., None], kb * jnp.exp(g)[..., None]], axis=-1)
    sol = lax.linalg.triangular_solve(a, rhs, left_side=True, lower=True, unit_diagonal=True)
    u, w = sol[..., :DV], sol[..., DV:]
    attn = jnp.where(tril, jnp.einsum('bhnik,bhnjk->bhnij', q, k) * decay, 0.0)
    q_dec = q * jnp.exp(g)[..., None]
    g_last = g[..., -1]
    k_dec = k * jnp.exp(g_last[..., None] - g)[..., None]

    def step(s, xs):
        u_n, w_n, attn_n, qd_n, kd_n, gl_n = xs
        v_new = u_n - jnp.einsum('bhck,bhkv->bhcv', w_n, s)
        o_n = jnp.einsum('bhck,bhkv->bhcv', qd_n, s) + jnp.einsum('bhij,bhjv->bhiv', attn_n, v_new)
        s = s * jnp.exp(gl_n)[..., None, None] + jnp.einsum('bhck,bhcv->bhkv', kd_n, v_new)
        return s, o_n

    xs = (jnp.moveaxis(u, 2, 0), jnp.moveaxis(w, 2, 0), jnp.moveaxis(attn, 2, 0),
          jnp.moveaxis(q_dec, 2, 0), jnp.moveaxis(k_dec, 2, 0), jnp.moveaxis(g_last, 2, 0))
    s_fin, o = lax.scan(step, s0, xs)
    o = jnp.moveaxis(o, 0, 2).reshape(B, H, L, DV)
    return o, s_fin


def deltanet_mixer(qkv, z, b, a, conv_w, a_log, dt_bias, onorm_g, s0):
    f32 = jnp.float32
    B, L, _ = qkv.shape
    qkv = short_conv(qkv, conv_w).astype(f32)
    q, k, v = jnp.split(qkv, 3, axis=-1)
    def heads(t):
        return t.reshape(B, L, H_A, -1).transpose(0, 2, 1, 3)
    q = l2norm(heads(q)) * (DK ** -0.5)
    k = l2norm(heads(k))
    v = heads(v)
    beta = jax.nn.sigmoid(b.astype(f32)).reshape(B, L, 2, H_A).transpose(2, 0, 3, 1)
    a_in = a.astype(f32).reshape(B, L, 2, H_A).transpose(2, 0, 3, 1)
    g = -jnp.exp(a_log.astype(f32))[:, None, :, None] * jax.nn.softplus(
        a_in + dt_bias.astype(f32)[:, None, :, None])
    s0 = s0.astype(f32)
    o_f, s_f = gated_delta_chunked(q, k, v, g[0], beta[0], s0[:, 0])
    def flip(t):
        return jnp.flip(t, axis=2)
    o_b, s_b = gated_delta_chunked(flip(q), flip(k), flip(v), flip(g[1]), flip(beta[1]), s0[:, 1])
    o = (o_f + flip(o_b)).transpose(0, 2, 1, 3)
    o = rmsnorm(o, onorm_g) * jax.nn.silu(z.astype(f32)).reshape(B, L, H_A, DV)
    return o.reshape(B, L, D_A), jnp.stack([s_f, s_b], axis=1)


def box_sum(u, w, axis):
    L = u.shape[axis]
    cs = jnp.cumsum(u, axis=axis)
    cs = jnp.concatenate([jnp.zeros_like(lax.slice_in_dim(cs, 0, 1, axis=axis)), cs], axis=axis)
    t = jnp.arange(L)
    lo = jnp.clip(t - w // 2, 0, L)
    hi = jnp.clip(t + w - w // 2, 0, L)
    s = jnp.take(cs, hi, axis=axis) - jnp.take(cs, lo, axis=axis)
    return s, (hi - lo).astype(u.dtype)


def pool_mixer(u, pool_w, pool_scale, grid):
    f32 = jnp.float32
    B, L, _ = u.shape
    u32 = u.astype(f32)
    outs = []
    for i, w in enumerate(POOL_WINDOWS):
        ui = u32[..., i * PG:(i + 1) * PG]
        if grid:
            rows = L // GRID_W
            ug = ui.reshape(B, rows, GRID_W, PG)
            s, cr = box_sum(ug, w, 1)
            s, cc = box_sum(s, w, 2)
            mean = (s / (cr[None, :, None, None] * cc[None, None, :, None])).reshape(B, L, PG)
        else:
            s, cnt = box_sum(ui, w, 1)
            mean = s / cnt[None, :, None]
        outs.append(jnp.einsum('blc,cd->bld', mean - ui, pool_w[i].astype(f32)))
    return jnp.concatenate(outs, axis=-1) * pool_scale.astype(f32)


BM = 256


def _expert_kernel(blk_e_ref, nused_ref, x_ref, wt_ref, wg_ref, wu_ref, wd_ref, y_ref, wg_s, wu_s, wd_s):
    i = pl.program_id(0)

    @pl.when(i < nused_ref[0])
    def _():
        e = blk_e_ref[i]
        prev = blk_e_ref[jnp.maximum(i - 1, 0)]

        @pl.when((i == 0) | (e != prev))
        def _():
            wg_s[...] = wg_ref[0].astype(jnp.bfloat16)
            wu_s[...] = wu_ref[0].astype(jnp.bfloat16)
            wd_s[...] = wd_ref[0].astype(jnp.bfloat16)

        x = x_ref[...]
        g = jnp.dot(x, wg_s[...], preferred_element_type=jnp.float32)
        u = jnp.dot(x, wu_s[...], preferred_element_type=jnp.float32)
        a = (g * jax.nn.sigmoid(g)) * u
        y = jnp.dot(a.astype(jnp.bfloat16), wd_s[...], preferred_element_type=jnp.float32)
        y_ref[...] = y * wt_ref[...]


def _expert_call(x_sorted, w_sorted, blk_e, n_used, w_gate, w_up, w_down):
    n_pad, D = x_sorted.shape
    n_blk = n_pad // BM
    E, _, F = w_gate.shape

    def row_map(i, be, nu):
        return (jnp.minimum(i, nu[0] - 1), 0)

    def w_map(i, be, nu):
        return (be[jnp.minimum(i, nu[0] - 1)], 0, 0)

    return pl.pallas_call(
        _expert_kernel,
        grid_spec=pltpu.PrefetchScalarGridSpec(
            num_scalar_prefetch=2,
            grid=(n_blk,),
            in_specs=[pl.BlockSpec((BM, D), row_map),
                      pl.BlockSpec((BM, 1), row_map),
                      pl.BlockSpec((1, D, F), w_map),
                      pl.BlockSpec((1, D, F), w_map),
                      pl.BlockSpec((1, F, D), w_map)],
            out_specs=pl.BlockSpec((BM, D), row_map),
            scratch_shapes=[pltpu.VMEM((D, F), jnp.bfloat16), pltpu.VMEM((D, F), jnp.bfloat16),
                            pltpu.VMEM((F, D), jnp.bfloat16)]),
        out_shape=jax.ShapeDtypeStruct((n_pad, D), jnp.float32),
        compiler_params=pltpu.CompilerParams(dimension_semantics=("arbitrary",),
                                             vmem_limit_bytes=48 * 1024 * 1024),
    )(blk_e, n_used, x_sorted, w_sorted, w_gate, w_up, w_down)


def moe_ffn(hf, router_w, router_bias, w_gate, w_up, w_down, sh_gate, sh_up, sh_down):
    f32 = jnp.float32
    T, D = hf.shape
    scores = jax.nn.sigmoid(jnp.dot(hf, router_w, precision=lax.Precision.HIGHEST))
    sel = scores + router_bias.astype(f32)
    grp = sel.reshape(T, N_GROUPS, N_EXPERTS // N_GROUPS)
    grp_score = lax.top_k(grp, 2)[0].sum(-1)
    _, top_g = lax.top_k(grp_score, TOPK_GROUP)
    gmask = jnp.any(top_g[:, :, None] == jnp.arange(N_GROUPS)[None, None, :], axis=1)
    emask = jnp.repeat(gmask, N_EXPERTS // N_GROUPS, axis=1)
    _, idx = lax.top_k(jnp.where(emask, sel, -jnp.inf), TOP_K)
    wts = jnp.take_along_axis(scores, idx, axis=1)
    wts = wts / jnp.sum(wts, axis=-1, keepdims=True) * ROUTED_SCALE
    TK = T * TOP_K
    flat_e = idx.reshape(-1)
    order = jnp.argsort(flat_e)
    sorted_e = flat_e[order]
    tok = (order // TOP_K).astype(jnp.int32)
    counts = jnp.bincount(flat_e, length=N_EXPERTS)
    padded = (counts + BM - 1) // BM * BM
    pad_end = jnp.cumsum(padded)
    pad_start = pad_end - padded
    seg_start = jnp.cumsum(counts) - counts
    dest = (pad_start[sorted_e] + jnp.arange(TK) - seg_start[sorted_e]).astype(jnp.int32)
    n_pad = TK + N_EXPERTS * BM
    n_blk = n_pad // BM
    buf_tok = jnp.full((n_pad,), T, jnp.int32).at[dest].set(tok)
    buf_w = jnp.zeros((n_pad,), f32).at[dest].set(wts.reshape(-1)[order])
    blk_e = jnp.minimum(jnp.searchsorted(pad_end, jnp.arange(n_blk) * BM, side='right'),
                        N_EXPERTS - 1).astype(jnp.int32)
    n_used = (pad_end[-1] // BM).astype(jnp.int32).reshape(1)
    h_pad = jnp.concatenate([hf.astype(jnp.bfloat16), jnp.zeros((1, D), jnp.bfloat16)], axis=0)
    x_sorted = h_pad[buf_tok]
    y = _expert_call(x_sorted, buf_w[:, None], blk_e, n_used, w_gate, w_up, w_down)
    pos = jnp.zeros((TK,), jnp.int32).at[order].set(dest)
    routed = y[pos].reshape(T, TOP_K, D).sum(axis=1)
    shared = (jax.nn.silu(hf @ sh_gate) * (hf @ sh_up)) @ sh_down
    return routed + shared


def trunk_mix(x, mod, s0, grid, norm1_g, w_in, conv_w, a_log, dt_bias, onorm_g, pool_w, pool_scale,
              w_out, norm2_g):
    shift1, scale1, gate1, shift2, scale2, gate2 = jnp.split(mod, 6, axis=-1)
    h = rmsnorm(x, norm1_g) * (1 + scale1) + shift1
    proj = h @ w_in
    qkv, z, b, a, u = jnp.split(proj, [3 * D_A, 4 * D_A, 4 * D_A + 2 * H_A, 4 * D_A + 4 * H_A], axis=-1)
    o_a, state = deltanet_mixer(qkv, z, b, a, conv_w, a_log, dt_bias, onorm_g, s0)
    o_p = pool_mixer(u, pool_w, pool_scale, grid)
    mix = jnp.concatenate([o_a.astype(x.dtype), o_p.astype(x.dtype)], axis=-1) @ w_out
    x = x + gate1 * mix
    h = rmsnorm(x, norm2_g) * (1 + scale2) + shift2
    return x, h, jnp.broadcast_to(gate2, x.shape), state


def _final_norm_kernel(x_ref, g_ref, o_ref):
    x = x_ref[...]
    y = x * lax.rsqrt(jnp.mean(x * x, axis=-1, keepdims=True) + EPS)
    o_ref[...] = y * g_ref[...]


def _final_norm(x, g):
    B, L, D = x.shape
    xf = x.reshape(B * L, D)
    tm = 512
    out = pl.pallas_call(
        _final_norm_kernel,
        grid=(B * L // tm,),
        in_specs=[pl.BlockSpec((tm, D), lambda i: (i, 0)), pl.BlockSpec((1, D), lambda i: (0, 0))],
        out_specs=pl.BlockSpec((tm, D), lambda i: (i, 0)),
        out_shape=jax.ShapeDtypeStruct((B * L, D), x.dtype),
    )(xf, g.reshape(1, D))
    return out.reshape(B, L, D)


def kernel(x_prompt, x_sample, state_delta, c, c_ctx, w_ada, b_ada, norm1_g, w_in, conv_w, a_log,
           dt_bias, onorm_g, pool_w, pool_scale, w_out, norm2_g, router_w, router_bias, exp_w_gate,
           exp_w_up, exp_w_down, sh_w_gate, sh_w_up, sh_w_down, final_g):
    xp = x_prompt
    xs = x_sample
    new_states = []
    for l in range(DEPTH):
        lw = (norm1_g[l], w_in[l], conv_w[l], a_log[l], dt_bias[l], onorm_g[l], pool_w[l], pool_scale[l],
              w_out[l], norm2_g[l])
        mod_ctx = (jax.nn.silu(c_ctx) @ w_ada[l] + b_ada[l])[None, None, :]
        mod_lat = (jax.nn.silu(c) @ w_ada[l] + b_ada[l])[:, None, :]
        s0_ctx = jnp.zeros((xp.shape[0], 2, H_A, DK, DV), jnp.float32)
        xp, hp, gp, st_ctx = trunk_mix(xp, mod_ctx, s0_ctx, False, *lw)
        xs, hs, gs, _ = trunk_mix(xs, mod_lat, state_delta[:, l], True, *lw)
        Tp = xp.shape[0] * xp.shape[1]
        hf = jnp.concatenate([hp.reshape(Tp, D_MODEL), hs.reshape(-1, D_MODEL)], axis=0)
        m = moe_ffn(hf, router_w[l], router_bias[l], exp_w_gate[l], exp_w_up[l], exp_w_down[l],
                    sh_w_gate[l], sh_w_up[l], sh_w_down[l])
        xp = xp + gp * m[:Tp].reshape(xp.shape)
        xs = xs + gs * m[Tp:].reshape(xs.shape)
        new_states.append(st_ctx.astype(x_prompt.dtype))
    y_prompt = _final_norm(xp, final_g)
    y_sample = _final_norm(xs, final_g)
    new_state_delta = jnp.stack(new_states, axis=1)
    return (y_prompt, y_sample, new_state_delta)
```

```python
import math
import jax, jax.numpy as jnp
from jax import lax
import numpy as np
from jax.experimental import pallas as pl
from jax.experimental.pallas import tpu as pltpu

D_MODEL = 1024
DEPTH = 1
GRID_W = 64
D_MIX = D_MODEL
D_A = D_MIX // 2
D_P = D_MIX - D_A
H_A = 4
DK = D_A // H_A
DV = D_A // H_A
CONV_K = 5
CHUNK = 64
POOL_WINDOWS = (2, 4, 8, 16)
N_PG = len(POOL_WINDOWS)
PG = D_P // N_PG
N_EXPERTS = 256
TOP_K = 8
N_GROUPS = 8
TOPK_GROUP = 4
ROUTED_SCALE = 2.5
MOE_BLOCK = 128
EPS = 1e-6


def rmsnorm(x, g):
    x32 = x.astype(jnp.float32)
    y = x32 * lax.rsqrt(jnp.mean(x32 * x32, axis=-1, keepdims=True) + EPS)
    return (y * g.astype(jnp.float32)).astype(x.dtype)


def l2norm(x):
    return x * lax.rsqrt(jnp.sum(x * x, axis=-1, keepdims=True) + EPS)


def short_conv(u, w):
    C = u.shape[-1]
    y = lax.conv_general_dilated(u, w[:, None, :].astype(u.dtype), window_strides=(1,),
                                 padding=[(CONV_K // 2, CONV_K // 2)],
                                 dimension_numbers=('NWC', 'WIO', 'NWC'), feature_group_count=C)
    return jax.nn.silu(y)


def gated_delta_chunked(q, k, v, g, beta, s0):
    B, H, L, _ = q.shape
    N = L // CHUNK
    def blk(t):
        return t.reshape((B, H, N, CHUNK) + t.shape[3:])
    q, k, v, g, beta = blk(q), blk(k), blk(v), blk(g), blk(beta)
    g = jnp.cumsum(g, axis=-1)
    idx = jnp.arange(CHUNK)
    tril = idx[:, None] >= idx[None, :]
    strict = idx[:, None] > idx[None, :]
    diff = g[..., :, None] - g[..., None, :]
    decay = jnp.where(tril, jnp.exp(jnp.where(tril, diff, 0.0)), 0.0)
    kb = k * beta[..., None]
    a = jnp.where(strict, jnp.einsum('bhnik,bhnjk->bhnij', kb, k) * decay, 0.0)
    rhs = jnp.concatenate([v * beta[..., None], kb * jnp.exp(g)[..., None]], axis=-1)
    sol = lax.linalg.triangular_solve(a, rhs, left_side=True, lower=True, unit_diagonal=True)
    u, w = sol[..., :DV], sol[..., DV:]
    attn = jnp.where(tril, jnp.einsum('bhnik,bhnjk->bhnij', q, k) * decay, 0.0)
    q_dec = q * jnp.exp(g)[..., None]
    g_last = g[..., -1]
    k_dec = k * jnp.exp(g_last[..., None] - g)[..., None]

    def step(s, xs):
        u_n, w_n, attn_n, qd_n, kd_n, gl_n = xs
        v_new = u_n - jnp.einsum('bhck,bhkv->bhcv', w_n, s)
        o_n = jnp.einsum('bhck,bhkv->bhcv', qd_n, s) + jnp.einsum('bhij,bhjv->bhiv', attn_n, v_new)
        s = s * jnp.exp(gl_n)[..., None, None] + jnp.einsum('bhck,bhcv->bhkv', kd_n, v_new)
        return s, o_n

    xs = (jnp.moveaxis(u, 2, 0), jnp.moveaxis(w, 2, 0), jnp.moveaxis(attn, 2, 0),
          jnp.moveaxis(q_dec, 2, 0), jnp.moveaxis(k_dec, 2, 0), jnp.moveaxis(g_last, 2, 0))
    s_fin, o = lax.scan(step, s0, xs)
    o = jnp.moveaxis(o, 0, 2).reshape(B, H, L, DV)
    return o, s_fin


def deltanet_mixer(qkv, z, b, a, conv_w, a_log, dt_bias, onorm_g, s0):
    f32 = jnp.float32
    B, L, _ = qkv.shape
    qkv = short_conv(qkv, conv_w).astype(f32)
    q, k, v = jnp.split(qkv, 3, axis=-1)
    def heads(t):
        return t.reshape(B, L, H_A, -1).transpose(0, 2, 1, 3)
    q = l2norm(heads(q)) * (DK ** -0.5)
    k = l2norm(heads(k))
    v = heads(v)
    beta = jax.nn.sigmoid(b.astype(f32)).reshape(B, L, 2, H_A).transpose(2, 0, 3, 1)
    a_in = a.astype(f32).reshape(B, L, 2, H_A).transpose(2, 0, 3, 1)
    g = -jnp.exp(a_log.astype(f32))[:, None, :, None] * jax.nn.softplus(
        a_in + dt_bias.astype(f32)[:, None, :, None])
    s0 = s0.astype(f32)
    o_f, s_f = gated_delta_chunked(q, k, v, g[0], beta[0], s0[:, 0])
    def flip(t):
        return jnp.flip(t, axis=2)
    o_b, s_b = gated_delta_chunked(flip(q), flip(k), flip(v), flip(g[1]), flip(beta[1]), s0[:, 1])
    o = (o_f + flip(o_b)).transpose(0, 2, 1, 3)
    o = rmsnorm(o, onorm_g) * jax.nn.silu(z.astype(f32)).reshape(B, L, H_A, DV)
    return o.reshape(B, L, D_A), jnp.stack([s_f, s_b], axis=1)


def box_sum(u, w, axis):
    L = u.shape[axis]
    cs = jnp.cumsum(u, axis=axis)
    cs = jnp.concatenate([jnp.zeros_like(lax.slice_in_dim(cs, 0, 1, axis=axis)), cs], axis=axis)
    t = jnp.arange(L)
    lo = jnp.clip(t - w // 2, 0, L)
    hi = jnp.clip(t + w - w // 2, 0, L)
    s = jnp.take(cs, hi, axis=axis) - jnp.take(cs, lo, axis=axis)
    return s, (hi - lo).astype(u.dtype)


def pool_mixer(u, pool_w, pool_scale, grid):
    f32 = jnp.float32
    B, L, _ = u.shape
    u32 = u.astype(f32)
    outs = []
    for i, w in enumerate(POOL_WINDOWS):
        ui = u32[..., i * PG:(i + 1) * PG]
        if grid:
            rows = L // GRID_W
            ug = ui.reshape(B, rows, GRID_W, PG)
            s, cr = box_sum(ug, w, 1)
            s, cc = box_sum(s, w, 2)
            mean = (s / (cr[None, :, None, None] * cc[None, None, :, None])).reshape(B, L, PG)
        else:
            s, cnt = box_sum(ui, w, 1)
            mean = s / cnt[None, :, None]
        outs.append(jnp.einsum('blc,cd->bld', mean - ui, pool_w[i].astype(f32)))
    return jnp.concatenate(outs, axis=-1) * pool_scale.astype(f32)


BM = 256


def _expert_kernel(blk_e_ref, nused_ref, tok_hbm, h_hbm, wg_ref, wu_ref, wd_ref, y_ref,
                   xbuf, tok_s, wg_s, wu_s, wd_s, gsem, tsem):
    i = pl.program_id(0)
    n_used = nused_ref[0]

    def tok_copy(b, slot):
        return pltpu.make_async_copy(tok_hbm.at[b], tok_s.at[slot], tsem.at[slot])

    def start_gather(slot):
        for r in range(BM):
            pltpu.make_async_copy(h_hbm.at[tok_s[slot, r]], xbuf.at[slot, r], gsem.at[slot]).start()

    def wait_gather(slot):
        pltpu.make_async_copy(xbuf.at[slot], xbuf.at[slot], gsem.at[slot]).wait()

    @pl.when(i == 0)
    def _():
        tok_copy(0, 0).start()
        tok_copy(0, 0).wait()
        start_gather(0)
        tok_copy(jnp.minimum(1, n_used - 1), 1).start()

    @pl.when(i < n_used)
    def _():
        slot = i % 2
        nslot = 1 - slot
        nxt = jnp.minimum(i + 1, n_used - 1)
        tok_copy(nxt, nslot).wait()
        start_gather(nslot)
        tok_copy(jnp.minimum(i + 2, n_used - 1), slot).start()

        e = blk_e_ref[i]
        prev = blk_e_ref[jnp.maximum(i - 1, 0)]

        @pl.when((i == 0) | (e != prev))
        def _():
            wg_s[...] = wg_ref[0].astype(jnp.bfloat16)
            wu_s[...] = wu_ref[0].astype(jnp.bfloat16)
            wd_s[...] = wd_ref[0].astype(jnp.bfloat16)

        wait_gather(slot)
        x = xbuf[slot].astype(jnp.bfloat16)
        g = jnp.dot(x, wg_s[...], preferred_element_type=jnp.float32)
        u = jnp.dot(x, wu_s[...], preferred_element_type=jnp.float32)
        a = (g * jax.nn.sigmoid(g)) * u
        y = jnp.dot(a.astype(jnp.bfloat16), wd_s[...], preferred_element_type=jnp.float32)
        y_ref[...] = y

        @pl.when(i == n_used - 1)
        def _():
            wait_gather(nslot)
            tok_copy(0, slot).wait()

    @pl.when(i >= n_used)
    def _():
        y_ref[...] = jnp.zeros_like(y_ref)


def _expert_call(hf, tok_sorted, blk_e, n_used, w_gate, w_up, w_down):
    n_blk = tok_sorted.shape[0]
    n_pad = n_blk * BM
    T, D = hf.shape
    E, _, F = w_gate.shape

    def row_map(i, be, nu):
        return (jnp.minimum(i, nu[0] - 1), 0)

    def w_map(i, be, nu):
        return (be[jnp.minimum(i, nu[0] - 1)], 0, 0)

    return pl.pallas_call(
        _expert_kernel,
        grid_spec=pltpu.PrefetchScalarGridSpec(
            num_scalar_prefetch=2,
            grid=(n_blk,),
            in_specs=[pl.BlockSpec(memory_space=pl.ANY),
                      pl.BlockSpec(memory_space=pl.ANY),
                      pl.BlockSpec((1, D, F), w_map),
                      pl.BlockSpec((1, D, F), w_map),
                      pl.BlockSpec((1, F, D), w_map)],
            out_specs=pl.BlockSpec((BM, D), lambda i, be, nu: (i, 0)),
            scratch_shapes=[pltpu.VMEM((2, BM, D), jnp.float32),
                            pltpu.SMEM((2, BM), jnp.int32),
                            pltpu.VMEM((D, F), jnp.bfloat16), pltpu.VMEM((D, F), jnp.bfloat16),
                            pltpu.VMEM((F, D), jnp.bfloat16),
                            pltpu.SemaphoreType.DMA((2,)), pltpu.SemaphoreType.DMA((2,))]),
        out_shape=jax.ShapeDtypeStruct((n_pad, D), jnp.float32),
        compiler_params=pltpu.CompilerParams(dimension_semantics=("arbitrary",),
                                             vmem_limit_bytes=48 * 1024 * 1024),
    )(blk_e, n_used, tok_sorted, hf, w_gate, w_up, w_down)


TC = 64


def _combine_kernel(pos_hbm, y_hbm, w_ref, out_ref, ybuf, pos_s, gsem, psem):
    j = pl.program_id(0)
    last = pl.num_programs(0) - 1

    def pos_copy(b, slot):
        return pltpu.make_async_copy(pos_hbm.at[b], pos_s.at[slot], psem.at[slot])

    def start_gather(slot):
        for k in range(TOP_K):
            for t in range(TC):
                pltpu.make_async_copy(y_hbm.at[pos_s[slot, k * TC + t]], ybuf.at[slot, k, t],
                                      gsem.at[slot]).start()

    def wait_gather(slot):
        pltpu.make_async_copy(ybuf.at[slot], ybuf.at[slot], gsem.at[slot]).wait()

    @pl.when(j == 0)
    def _():
        pos_copy(0, 0).start()
        pos_copy(0, 0).wait()
        start_gather(0)
        pos_copy(jnp.minimum(1, last), 1).start()

    slot = j % 2
    nslot = 1 - slot
    pos_copy(0, nslot).wait()
    start_gather(nslot)
    pos_copy(jnp.minimum(j + 2, last), slot).start()
    wait_gather(slot)
    w = w_ref[...]
    acc = w[:, 0:1] * ybuf[slot, 0]
    for k in range(1, TOP_K):
        acc = acc + w[:, k:k + 1] * ybuf[slot, k]
    out_ref[...] = acc

    @pl.when(j == last)
    def _():
        wait_gather(nslot)
        pos_copy(0, slot).wait()


def _combine_call(y_sorted, pos, wts):
    T, K = pos.shape
    D = y_sorted.shape[1]
    n_tiles = T // TC
    pos_t = pos.reshape(n_tiles, TC, K).transpose(0, 2, 1).reshape(n_tiles, K * TC)
    return pl.pallas_call(
        _combine_kernel,
        grid=(n_tiles,),
        in_specs=[pl.BlockSpec(memory_space=pl.ANY),
                  pl.BlockSpec(memory_space=pl.ANY),
                  pl.BlockSpec((TC, K), lambda j: (j, 0))],
        out_specs=pl.BlockSpec((TC, D), lambda j: (j, 0)),
        scratch_shapes=[pltpu.VMEM((2, K, TC, D), jnp.float32),
                        pltpu.SMEM((2, K * TC), jnp.int32),
                        pltpu.SemaphoreType.DMA((2,)), pltpu.SemaphoreType.DMA((2,))],
        out_shape=jax.ShapeDtypeStruct((T, D), jnp.float32),
        compiler_params=pltpu.CompilerParams(dimension_semantics=("arbitrary",)),
    )(pos_t, y_sorted, wts)


def moe_ffn(hf, router_w, router_bias, w_gate, w_up, w_down, sh_gate, sh_up, sh_down):
    f32 = jnp.float32
    T, D = hf.shape
    scores = jax.nn.sigmoid(jnp.dot(hf, router_w, precision=lax.Precision.HIGHEST))
    sel = scores + router_bias.astype(f32)
    grp = sel.reshape(T, N_GROUPS, N_EXPERTS // N_GROUPS)
    grp_score = lax.top_k(grp, 2)[0].sum(-1)
    _, top_g = lax.top_k(grp_score, TOPK_GROUP)
    gmask = jnp.any(top_g[:, :, None] == jnp.arange(N_GROUPS)[None, None, :], axis=1)
    emask = jnp.repeat(gmask, N_EXPERTS // N_GROUPS, axis=1)
    _, idx = lax.top_k(jnp.where(emask, sel, -jnp.inf), TOP_K)
    wts = jnp.take_along_axis(scores, idx, axis=1)
    wts = wts / jnp.sum(wts, axis=-1, keepdims=True) * ROUTED_SCALE
    TK = T * TOP_K
    flat_e = idx.reshape(-1)
    order = jnp.argsort(flat_e)
    sorted_e = flat_e[order]
    tok = (order // TOP_K).astype(jnp.int32)
    counts = jnp.bincount(flat_e, length=N_EXPERTS)
    padded = (counts + BM - 1) // BM * BM
    pad_end = jnp.cumsum(padded)
    pad_start = pad_end - padded
    seg_start = jnp.cumsum(counts) - counts
    dest = (pad_start[sorted_e] + jnp.arange(TK) - seg_start[sorted_e]).astype(jnp.int32)
    n_pad = TK + N_EXPERTS * BM
    n_blk = n_pad // BM
    buf_tok = jnp.zeros((n_pad,), jnp.int32).at[dest].set(tok)
    blk_e = jnp.minimum(jnp.searchsorted(pad_end, jnp.arange(n_blk) * BM, side='right'),
                        N_EXPERTS - 1).astype(jnp.int32)
    n_used = (pad_end[-1] // BM).astype(jnp.int32).reshape(1)
    y = _expert_call(hf, buf_tok.reshape(n_blk, BM), blk_e, n_used, w_gate, w_up, w_down)
    pos = jnp.zeros((TK,), jnp.int32).at[order].set(dest).reshape(T, TOP_K)
    routed = _combine_call(y, pos, wts)
    shared = (jax.nn.silu(hf @ sh_gate) * (hf @ sh_up)) @ sh_down
    return routed + shared


def trunk_mix(x, mod, s0, grid, norm1_g, w_in, conv_w, a_log, dt_bias, onorm_g, pool_w, pool_scale,
              w_out, norm2_g):
    shift1, scale1, gate1, shift2, scale2, gate2 = jnp.split(mod, 6, axis=-1)
    h = rmsnorm(x, norm1_g) * (1 + scale1) + shift1
    proj = h @ w_in
    qkv, z, b, a, u = jnp.split(proj, [3 * D_A, 4 * D_A, 4 * D_A + 2 * H_A, 4 * D_A + 4 * H_A], axis=-1)
    o_a, state = deltanet_mixer(qkv, z, b, a, conv_w, a_log, dt_bias, onorm_g, s0)
    o_p = pool_mixer(u, pool_w, pool_scale, grid)
    mix = jnp.concatenate([o_a.astype(x.dtype), o_p.astype(x.dtype)], axis=-1) @ w_out
    x = x + gate1 * mix
    h = rmsnorm(x, norm2_g) * (1 + scale2) + shift2
    return x, h, jnp.broadcast_to(gate2, x.shape), state


def _final_norm_kernel(x_ref, g_ref, o_ref):
    x = x_ref[...]
    y = x * lax.rsqrt(jnp.mean(x * x, axis=-1, keepdims=True) + EPS)
    o_ref[...] = y * g_ref[...]


def _final_norm(x, g):
    B, L, D = x.shape
    xf = x.reshape(B * L, D)
    tm = 512
    out = pl.pallas_call(
        _final_norm_kernel,
        grid=(B * L // tm,),
        in_specs=[pl.BlockSpec((tm, D), lambda i: (i, 0)), pl.BlockSpec((1, D), lambda i: (0, 0))],
        out_specs=pl.BlockSpec((tm, D), lambda i: (i, 0)),
        out_shape=jax.ShapeDtypeStruct((B * L, D), x.dtype),
    )(xf, g.reshape(1, D))
    return out.reshape(B, L, D)


def kernel(x_prompt, x_sample, state_delta, c, c_ctx, w_ada, b_ada, norm1_g, w_in, conv_w, a_log,
           dt_bias, onorm_g, pool_w, pool_scale, w_out, norm2_g, router_w, router_bias, exp_w_gate,
           exp_w_up, exp_w_down, sh_w_gate, sh_w_up, sh_w_down, final_g):
    xp = x_prompt
    xs = x_sample
    new_states = []
    for l in range(DEPTH):
        lw = (norm1_g[l], w_in[l], conv_w[l], a_log[l], dt_bias[l], onorm_g[l], pool_w[l], pool_scale[l],
              w_out[l], norm2_g[l])
        mod_ctx = (jax.nn.silu(c_ctx) @ w_ada[l] + b_ada[l])[None, None, :]
        mod_lat = (jax.nn.silu(c) @ w_ada[l] + b_ada[l])[:, None, :]
        s0_ctx = jnp.zeros((xp.shape[0], 2, H_A, DK, DV), jnp.float32)
        xp, hp, gp, st_ctx = trunk_mix(xp, mod_ctx, s0_ctx, False, *lw)
        xs, hs, gs, _ = trunk_mix(xs, mod_lat, state_delta[:, l], True, *lw)
        Tp = xp.shape[0] * xp.shape[1]
        hf = jnp.concatenate([hp.reshape(Tp, D_MODEL), hs.reshape(-1, D_MODEL)], axis=0)
        m = moe_ffn(hf, router_w[l], router_bias[l], exp_w_gate[l], exp_w_up[l], exp_w_down[l],
                    sh_w_gate[l], sh_w_up[l], sh_w_down[l])
        xp = xp + gp * m[:Tp].reshape(xp.shape)
        xs = xs + gs * m[Tp:].reshape(xs.shape)
        new_states.append(st_ctx.astype(x_prompt.dtype))
    y_prompt = _final_norm(xp, final_g)
    y_sample = _final_norm(xs, final_g)
    new_state_delta = jnp.stack(new_states, axis=1)
    return (y_prompt, y_sample, new_state_delta)
```

```python
import functools
import math
import jax, jax.numpy as jnp
from jax import lax
import numpy as np
from jax.experimental import pallas as pl
from jax.experimental.pallas import tpu as pltpu

D_MODEL = 1024
DEPTH = 1
GRID_W = 64
D_MIX = D_MODEL
D_A = D_MIX // 2
D_P = D_MIX - D_A
H_A = 4
DK = D_A // H_A
DV = D_A // H_A
CONV_K = 5
CHUNK = 64
POOL_WINDOWS = (2, 4, 8, 16)
N_PG = len(POOL_WINDOWS)
PG = D_P // N_PG
N_EXPERTS = 256
TOP_K = 8
N_GROUPS = 8
TOPK_GROUP = 4
ROUTED_SCALE = 2.5
MOE_BLOCK = 128
EPS = 1e-6


def rmsnorm(x, g):
    x32 = x.astype(jnp.float32)
    y = x32 * lax.rsqrt(jnp.mean(x32 * x32, axis=-1, keepdims=True) + EPS)
    return (y * g.astype(jnp.float32)).astype(x.dtype)


def l2norm(x):
    return x * lax.rsqrt(jnp.sum(x * x, axis=-1, keepdims=True) + EPS)


def short_conv(u, w):
    C = u.shape[-1]
    y = lax.conv_general_dilated(u, w[:, None, :].astype(u.dtype), window_strides=(1,),
                                 padding=[(CONV_K // 2, CONV_K // 2)],
                                 dimension_numbers=('NWC', 'WIO', 'NWC'), feature_group_count=C)
    return jax.nn.silu(y)


def gated_delta_chunked(q, k, v, g, beta, s0):
    B, H, L, _ = q.shape
    N = L // CHUNK
    def blk(t):
        return t.reshape((B, H, N, CHUNK) + t.shape[3:])
    q, k, v, g, beta = blk(q), blk(k), blk(v), blk(g), blk(beta)
    g = jnp.cumsum(g, axis=-1)
    idx = jnp.arange(CHUNK)
    tril = idx[:, None] >= idx[None, :]
    strict = idx[:, None] > idx[None, :]
    diff = g[..., :, None] - g[..., None, :]
    decay = jnp.where(tril, jnp.exp(jnp.where(tril, diff, 0.0)), 0.0)
    kb = k * beta[..., None]
    a = jnp.where(strict, jnp.einsum('bhnik,bhnjk->bhnij', kb, k) * decay, 0.0)
    rhs = jnp.concatenate([v * beta[..., None], kb * jnp.exp(g)[..., None]], axis=-1)
    sol = lax.linalg.triangular_solve(a, rhs, left_side=True, lower=True, unit_diagonal=True)
    u, w = sol[..., :DV], sol[..., DV:]
    attn = jnp.where(tril, jnp.einsum('bhnik,bhnjk->bhnij', q, k) * decay, 0.0)
    q_dec = q * jnp.exp(g)[..., None]
    g_last = g[..., -1]
    k_dec = k * jnp.exp(g_last[..., None] - g)[..., None]

    def step(s, xs):
        u_n, w_n, attn_n, qd_n, kd_n, gl_n = xs
        v_new = u_n - jnp.einsum('bhck,bhkv->bhcv', w_n, s)
        o_n = jnp.einsum('bhck,bhkv->bhcv', qd_n, s) + jnp.einsum('bhij,bhjv->bhiv', attn_n, v_new)
        s = s * jnp.exp(gl_n)[..., None, None] + jnp.einsum('bhck,bhcv->bhkv', kd_n, v_new)
        return s, o_n

    xs = (jnp.moveaxis(u, 2, 0), jnp.moveaxis(w, 2, 0), jnp.moveaxis(attn, 2, 0),
          jnp.moveaxis(q_dec, 2, 0), jnp.moveaxis(k_dec, 2, 0), jnp.moveaxis(g_last, 2, 0))
    s_fin, o = lax.scan(step, s0, xs)
    o = jnp.moveaxis(o, 0, 2).reshape(B, H, L, DV)
    return o, s_fin


def deltanet_mixer(qkv, z, b, a, conv_w, a_log, dt_bias, onorm_g, s0):
    f32 = jnp.float32
    B, L, _ = qkv.shape
    qkv = short_conv(qkv, conv_w).astype(f32)
    q, k, v = jnp.split(qkv, 3, axis=-1)
    def heads(t):
        return t.reshape(B, L, H_A, -1).transpose(0, 2, 1, 3)
    q = l2norm(heads(q)) * (DK ** -0.5)
    k = l2norm(heads(k))
    v = heads(v)
    beta = jax.nn.sigmoid(b.astype(f32)).reshape(B, L, 2, H_A).transpose(2, 0, 3, 1)
    a_in = a.astype(f32).reshape(B, L, 2, H_A).transpose(2, 0, 3, 1)
    g = -jnp.exp(a_log.astype(f32))[:, None, :, None] * jax.nn.softplus(
        a_in + dt_bias.astype(f32)[:, None, :, None])
    s0 = s0.astype(f32)
    o_f, s_f = gated_delta_chunked(q, k, v, g[0], beta[0], s0[:, 0])
    def flip(t):
        return jnp.flip(t, axis=2)
    o_b, s_b = gated_delta_chunked(flip(q), flip(k), flip(v), flip(g[1]), flip(beta[1]), s0[:, 1])
    o = (o_f + flip(o_b)).transpose(0, 2, 1, 3)
    o = rmsnorm(o, onorm_g) * jax.nn.silu(z.astype(f32)).reshape(B, L, H_A, DV)
    return o.reshape(B, L, D_A), jnp.stack([s_f, s_b], axis=1)


SC = 256
CPS = SC // CHUNK
BASE = 16


def _mm(a, b):
    return jnp.dot(a.astype(jnp.bfloat16), b.astype(jnp.bfloat16), preferred_element_type=jnp.float32)


def _mm_nt(a, b):
    return lax.dot_general(a.astype(jnp.bfloat16), b.astype(jnp.bfloat16), (((1,), (1,)), ((), ())),
                           preferred_element_type=jnp.float32)


def _softplus(x):
    return jnp.maximum(x, 0.0) + jnp.log(1.0 + jnp.exp(-jnp.abs(x)))


def _delta_kernel(sc_ref, xq_ref, xk_ref, xv_ref, z_ref, bac_ref, bar_ref, cwq_ref, cwk_ref, cwv_ref,
                  og_ref, s0_ref, o_ref, st_ref, q_s, k_s, v_s, o_s, vn_s, *, n_sc, zero_init):
    h = pl.program_id(1)
    L = q_s.shape[0]

    def conv(x_ref, w_ref):
        x = x_ref[0]
        row = lax.broadcasted_iota(jnp.int32, x.shape, 0)
        acc = x * w_ref[CONV_K // 2:CONV_K // 2 + 1, :]
        for j in range(CONV_K):
            d = j - CONV_K // 2
            if d == 0:
                continue
            xs = pltpu.roll(x, (-d) % L, 0)
            ok = (row + d >= 0) & (row + d < L)
            acc = acc + jnp.where(ok, xs, 0.0) * w_ref[j:j + 1, :]
        return acc * jax.nn.sigmoid(acc)

    q = conv(xq_ref, cwq_ref)
    q_s[...] = q * lax.rsqrt(jnp.sum(q * q, axis=-1, keepdims=True) + EPS) * (DK ** -0.5)
    k = conv(xk_ref, cwk_ref)
    k_s[...] = k * lax.rsqrt(jnp.sum(k * k, axis=-1, keepdims=True) + EPS)
    v_s[...] = conv(xv_ref, cwv_ref)
    o_s[...] = jnp.zeros_like(o_s)

    ri = lax.broadcasted_iota(jnp.int32, (SC, SC), 0)
    ci = lax.broadcasted_iota(jnp.int32, (SC, SC), 1)
    same = (ri // CHUNK) == (ci // CHUNK)
    same_base = (ri // BASE) == (ci // BASE)
    merge_masks = [(ri // w) == (ci // w) for w in (2 * BASE, CHUNK)]
    eye = (ri == ci).astype(jnp.float32)
    rowi = lax.broadcasted_iota(jnp.int32, (SC, DV), 0)

    def unit(m, d, s):
        r0 = pl.multiple_of(m * SC, SC)
        q = q_s[pl.ds(r0, SC), :]
        k = k_s[pl.ds(r0, SC), :]
        v = v_s[pl.ds(r0, SC), :]
        bc = bac_ref[0, 0, pl.ds(r0, SC), :]
        br = bar_ref[0, 0, m]
        a_l = sc_ref[d * H_A + h]
        dtb = sc_ref[2 * H_A + d * H_A + h]
        neg_ea = -jnp.exp(jnp.full((1, 1), a_l, jnp.float32))
        beta = jax.nn.sigmoid(bc[:, d:d + 1])
        g_col = neg_ea * _softplus(bc[:, 2 + d:3 + d] + dtb)
        g_row = neg_ea * _softplus(br[2 + d:3 + d, :] + dtb)
        if d == 0:
            tri, strict = same & (ci <= ri), same & (ci < ri)
        else:
            tri, strict = same & (ci >= ri), same & (ci > ri)
        tri_t = same & (ri <= ci) if d == 0 else same & (ri >= ci)
        gc_col = jnp.sum(jnp.where(tri, g_row, 0.0), axis=1, keepdims=True)
        gc_row = jnp.sum(jnp.where(tri_t, g_col, 0.0), axis=0, keepdims=True)
        gl_col = jnp.sum(jnp.where(same, g_row, 0.0), axis=1, keepdims=True)
        decay = jnp.where(tri, jnp.exp(jnp.where(tri, gc_col - gc_row, 0.0)), 0.0)
        kb = k * beta
        a = jnp.where(strict, _mm_nt(kb, k) * decay, 0.0)
        attn = jnp.where(tri, _mm_nt(q, k) * decay, 0.0)
        eg = jnp.exp(gc_col)
        x = jnp.concatenate([v * beta, kb * eg], axis=1)
        a0 = jnp.where(same_base, a, 0.0)
        t = eye - a0
        p = a0
        for _ in range(BASE.bit_length() - 2):
            p = _mm(p, p)
            t = t + _mm(t, p)
        inner = same_base
        for outer in merge_masks:
            off = jnp.where(outer & ~inner, a, 0.0)
            t = t - _mm(t, _mm(off, t))
            inner = outer
        x = _mm(t, x)
        u = x[:, :DV]
        w = x[:, DV:]
        qd = q * eg
        kdt = (k * jnp.exp(gl_col - gc_col)).T
        egl = jnp.exp(gl_col)
        vn_s[d] = jnp.zeros((SC, DV), jnp.float32)
        order = range(CPS) if d == 0 else range(CPS - 1, -1, -1)
        for c in order:
            lo, hi = c * CHUNK, (c + 1) * CHUNK
            ws_qs = _mm(jnp.concatenate([w[lo:hi], qd[lo:hi]], axis=0), s)
            v_new = u[lo:hi] - ws_qs[:CHUNK]
            vn_s[d, lo:hi, :] = v_new
            vn = vn_s[d]
            o_c = ws_qs[CHUNK:] + _mm(attn[lo:hi, :], vn)
            o_s[pl.ds(r0 + lo, CHUNK), :] += o_c
            v_only = jnp.where((rowi >= lo) & (rowi < hi), vn, 0.0)
            s = s * egl[lo:lo + 1, :] + _mm(kdt, v_only)
        return s

    if zero_init:
        s_f = jnp.zeros((DK, DV), jnp.float32)
        s_b = jnp.zeros((DK, DV), jnp.float32)
    else:
        s_f = s0_ref[0, 0, 0]
        s_b = s0_ref[0, 1, 0]

    def body(m, carry):
        s_f, s_b = carry
        return unit(m, 0, s_f), unit(n_sc - 1 - m, 1, s_b)

    if n_sc == 1:
        s_f, s_b = body(0, (s_f, s_b))
    else:
        s_f, s_b = lax.fori_loop(0, n_sc, body, (s_f, s_b))

    st_ref[0, 0, 0] = s_f
    st_ref[0, 1, 0] = s_b
    o = o_s[...]
    o = o * lax.rsqrt(jnp.mean(o * o, axis=-1, keepdims=True) + EPS) * og_ref[...]
    zz = z_ref[0]
    o_ref[0] = o * (zz * jax.nn.sigmoid(zz))


def _delta_call(qkv, z, ba, conv_w, a_log, dt_bias, onorm_g, s0):
    B, L, _ = qkv.shape
    n_sc = L // SC
    bah = ba.reshape(B, L, 4, H_A).transpose(0, 3, 1, 2)
    bar = bah.reshape(B, H_A, n_sc, SC, 4).transpose(0, 1, 2, 4, 3)
    scal = jnp.concatenate([a_log.reshape(-1), dt_bias.reshape(-1)]).astype(jnp.float32)
    zero_init = s0 is None
    if zero_init:
        s0 = jnp.zeros((1, 2, 1, DK, DV), jnp.float32)
        s0_spec = pl.BlockSpec((1, 2, 1, DK, DV), lambda b, h, sc: (0, 0, 0, 0, 0))
    else:
        s0_spec = pl.BlockSpec((1, 2, 1, DK, DV), lambda b, h, sc: (b, 0, h, 0, 0))

    def col(off):
        return pl.BlockSpec((1, L, DK), lambda b, h, sc: (b, 0, off + h))

    def cw(off):
        return pl.BlockSpec((CONV_K, DK), lambda b, h, sc: (0, off + h))

    kern = functools.partial(_delta_kernel, n_sc=n_sc, zero_init=zero_init)
    return pl.pallas_call(
        kern,
        grid_spec=pltpu.PrefetchScalarGridSpec(
            num_scalar_prefetch=1,
            grid=(B, H_A),
            in_specs=[col(0), col(H_A), col(2 * H_A),
                      pl.BlockSpec((1, L, DV), lambda b, h, sc: (b, 0, h)),
                      pl.BlockSpec((1, 1, L, 4), lambda b, h, sc: (b, h, 0, 0)),
                      pl.BlockSpec((1, 1, n_sc, 4, SC), lambda b, h, sc: (b, h, 0, 0, 0)),
                      cw(0), cw(H_A), cw(2 * H_A),
                      pl.BlockSpec((1, DV), lambda b, h, sc: (0, 0)),
                      s0_spec],
            out_specs=[pl.BlockSpec((1, L, DV), lambda b, h, sc: (b, 0, h)),
                       pl.BlockSpec((1, 2, 1, DK, DV), lambda b, h, sc: (b, 0, h, 0, 0))],
            scratch_shapes=[pltpu.VMEM((L, DK), jnp.float32), pltpu.VMEM((L, DK), jnp.float32),
                            pltpu.VMEM((L, DV), jnp.float32), pltpu.VMEM((L, DV), jnp.float32),
                            pltpu.VMEM((2, SC, DV), jnp.float32)]),
        out_shape=[jax.ShapeDtypeStruct((B, L, D_A), jnp.float32),
                   jax.ShapeDtypeStruct((B, 2, H_A, DK, DV), jnp.float32)],
        compiler_params=pltpu.CompilerParams(dimension_semantics=("arbitrary", "arbitrary"),
                                             vmem_limit_bytes=48 * 1024 * 1024),
    )(scal, qkv, qkv, qkv, z, bah, bar, conv_w, conv_w, conv_w, onorm_g.reshape(1, DV), s0)


def box_sum(u, w, axis):
    L = u.shape[axis]
    cs = jnp.cumsum(u, axis=axis)
    cs = jnp.concatenate([jnp.zeros_like(lax.slice_in_dim(cs, 0, 1, axis=axis)), cs], axis=axis)
    t = jnp.arange(L)
    lo = jnp.clip(t - w // 2, 0, L)
    hi = jnp.clip(t + w - w // 2, 0, L)
    s = jnp.take(cs, hi, axis=axis) - jnp.take(cs, lo, axis=axis)
    return s, (hi - lo).astype(u.dtype)


def pool_mixer(u, pool_w, pool_scale, grid):
    f32 = jnp.float32
    B, L, _ = u.shape
    u32 = u.astype(f32)
    outs = []
    for i, w in enumerate(POOL_WINDOWS):
        ui = u32[..., i * PG:(i + 1) * PG]
        if grid:
            rows = L // GRID_W
            ug = ui.reshape(B, rows, GRID_W, PG)
            s, cr = box_sum(ug, w, 1)
            s, cc = box_sum(s, w, 2)
            mean = (s / (cr[None, :, None, None] * cc[None, None, :, None])).reshape(B, L, PG)
        else:
            s, cnt = box_sum(ui, w, 1)
            mean = s / cnt[None, :, None]
        outs.append(jnp.einsum('blc,cd->bld', mean - ui, pool_w[i].astype(f32)))
    return jnp.concatenate(outs, axis=-1) * pool_scale.astype(f32)


BM = 256


def _expert_kernel(blk_e_ref, nused_ref, tok_hbm, h_hbm, wg_ref, wu_ref, wd_ref, y_ref,
                   xbuf, tok_s, wg_s, wu_s, wd_s, gsem, tsem):
    i = pl.program_id(0)
    n_used = nused_ref[0]

    def tok_copy(b, slot):
        return pltpu.make_async_copy(tok_hbm.at[b], tok_s.at[slot], tsem.at[slot])

    def start_gather(slot):
        for r in range(BM):
            pltpu.make_async_copy(h_hbm.at[tok_s[slot, r]], xbuf.at[slot, r], gsem.at[slot]).start()

    def wait_gather(slot):
        pltpu.make_async_copy(xbuf.at[slot], xbuf.at[slot], gsem.at[slot]).wait()

    @pl.when(i == 0)
    def _():
        tok_copy(0, 0).start()
        tok_copy(0, 0).wait()
        start_gather(0)
        tok_copy(jnp.minimum(1, n_used - 1), 1).start()

    @pl.when(i < n_used)
    def _():
        slot = i % 2
        nslot = 1 - slot
        nxt = jnp.minimum(i + 1, n_used - 1)
        tok_copy(nxt, nslot).wait()
        start_gather(nslot)
        tok_copy(jnp.minimum(i + 2, n_used - 1), slot).start()

        e = blk_e_ref[i]
        prev = blk_e_ref[jnp.maximum(i - 1, 0)]

        @pl.when((i == 0) | (e != prev))
        def _():
            wg_s[...] = wg_ref[0].astype(jnp.bfloat16)
            wu_s[...] = wu_ref[0].astype(jnp.bfloat16)
            wd_s[...] = wd_ref[0].astype(jnp.bfloat16)

        wait_gather(slot)
        x = xbuf[slot].astype(jnp.bfloat16)
        g = jnp.dot(x, wg_s[...], preferred_element_type=jnp.float32)
        u = jnp.dot(x, wu_s[...], preferred_element_type=jnp.float32)
        a = (g * jax.nn.sigmoid(g)) * u
        y = jnp.dot(a.astype(jnp.bfloat16), wd_s[...], preferred_element_type=jnp.float32)
        y_ref[...] = y

        @pl.when(i == n_used - 1)
        def _():
            wait_gather(nslot)
            tok_copy(0, slot).wait()

    @pl.when(i >= n_used)
    def _():
        y_ref[...] = jnp.zeros_like(y_ref)


def _expert_call(hf, tok_sorted, blk_e, n_used, w_gate, w_up, w_down):
    n_blk = tok_sorted.shape[0]
    n_pad = n_blk * BM
    T, D = hf.shape
    E, _, F = w_gate.shape

    def row_map(i, be, nu):
        return (jnp.minimum(i, nu[0] - 1), 0)

    def w_map(i, be, nu):
        return (be[jnp.minimum(i, nu[0] - 1)], 0, 0)

    return pl.pallas_call(
        _expert_kernel,
        grid_spec=pltpu.PrefetchScalarGridSpec(
            num_scalar_prefetch=2,
            grid=(n_blk,),
            in_specs=[pl.BlockSpec(memory_space=pl.ANY),
                      pl.BlockSpec(memory_space=pl.ANY),
                      pl.BlockSpec((1, D, F), w_map),
                      pl.BlockSpec((1, D, F), w_map),
                      pl.BlockSpec((1, F, D), w_map)],
            out_specs=pl.BlockSpec((BM, D), lambda i, be, nu: (i, 0)),
            scratch_shapes=[pltpu.VMEM((2, BM, D), jnp.float32),
                            pltpu.SMEM((2, BM), jnp.int32),
                            pltpu.VMEM((D, F), jnp.bfloat16), pltpu.VMEM((D, F), jnp.bfloat16),
                            pltpu.VMEM((F, D), jnp.bfloat16),
                            pltpu.SemaphoreType.DMA((2,)), pltpu.SemaphoreType.DMA((2,))]),
        out_shape=jax.ShapeDtypeStruct((n_pad, D), jnp.float32),
        compiler_params=pltpu.CompilerParams(dimension_semantics=("arbitrary",),
                                             vmem_limit_bytes=48 * 1024 * 1024),
    )(blk_e, n_used, tok_sorted, hf, w_gate, w_up, w_down)


TC = 64


def _combine_kernel(pos_hbm, y_hbm, w_ref, out_ref, ybuf, pos_s, gsem, psem):
    j = pl.program_id(0)
    last = pl.num_programs(0) - 1

    def pos_copy(b, slot):
        return pltpu.make_async_copy(pos_hbm.at[b], pos_s.at[slot], psem.at[slot])

    def start_gather(slot):
        for k in range(TOP_K):
            for t in range(TC):
                pltpu.make_async_copy(y_hbm.at[pos_s[slot, k * TC + t]], ybuf.at[slot, k, t],
                                      gsem.at[slot]).start()

    def wait_gather(slot):
        pltpu.make_async_copy(ybuf.at[slot], ybuf.at[slot], gsem.at[slot]).wait()

    @pl.when(j == 0)
    def _():
        pos_copy(0, 0).start()
        pos_copy(0, 0).wait()
        start_gather(0)
        pos_copy(jnp.minimum(1, last), 1).start()

    slot = j % 2
    nslot = 1 - slot
    pos_copy(0, nslot).wait()
    start_gather(nslot)
    pos_copy(jnp.minimum(j + 2, last), slot).start()
    wait_gather(slot)
    w = w_ref[...]
    acc = w[:, 0:1] * ybuf[slot, 0]
    for k in range(1, TOP_K):
        acc = acc + w[:, k:k + 1] * ybuf[slot, k]
    out_ref[...] = acc

    @pl.when(j == last)
    def _():
        wait_gather(nslot)
        pos_copy(0, slot).wait()


def _combine_call(y_sorted, pos, wts):
    T, K = pos.shape
    D = y_sorted.shape[1]
    n_tiles = T // TC
    pos_t = pos.reshape(n_tiles, TC, K).transpose(0, 2, 1).reshape(n_tiles, K * TC)
    return pl.pallas_call(
        _combine_kernel,
        grid=(n_tiles,),
        in_specs=[pl.BlockSpec(memory_space=pl.ANY),
                  pl.BlockSpec(memory_space=pl.ANY),
                  pl.BlockSpec((TC, K), lambda j: (j, 0))],
        out_specs=pl.BlockSpec((TC, D), lambda j: (j, 0)),
        scratch_shapes=[pltpu.VMEM((2, K, TC, D), jnp.float32),
                        pltpu.SMEM((2, K * TC), jnp.int32),
                        pltpu.SemaphoreType.DMA((2,)), pltpu.SemaphoreType.DMA((2,))],
        out_shape=jax.ShapeDtypeStruct((T, D), jnp.float32),
        compiler_params=pltpu.CompilerParams(dimension_semantics=("arbitrary",)),
    )(pos_t, y_sorted, wts)


def moe_ffn(hf, router_w, router_bias, w_gate, w_up, w_down, sh_gate, sh_up, sh_down):
    f32 = jnp.float32
    T, D = hf.shape
    scores = jax.nn.sigmoid(jnp.dot(hf, router_w, precision=lax.Precision.HIGHEST))
    sel = scores + router_bias.astype(f32)
    grp = sel.reshape(T, N_GROUPS, N_EXPERTS // N_GROUPS)
    grp_score = lax.top_k(grp, 2)[0].sum(-1)
    _, top_g = lax.top_k(grp_score, TOPK_GROUP)
    gmask = jnp.any(top_g[:, :, None] == jnp.arange(N_GROUPS)[None, None, :], axis=1)
    emask = jnp.repeat(gmask, N_EXPERTS // N_GROUPS, axis=1)
    _, idx = lax.top_k(jnp.where(emask, sel, -jnp.inf), TOP_K)
    wts = jnp.take_along_axis(scores, idx, axis=1)
    wts = wts / jnp.sum(wts, axis=-1, keepdims=True) * ROUTED_SCALE
    TK = T * TOP_K
    flat_e = idx.reshape(-1)
    order = jnp.argsort(flat_e)
    sorted_e = flat_e[order]
    tok = (order // TOP_K).astype(jnp.int32)
    counts = jnp.bincount(flat_e, length=N_EXPERTS)
    padded = (counts + BM - 1) // BM * BM
    pad_end = jnp.cumsum(padded)
    pad_start = pad_end - padded
    seg_start = jnp.cumsum(counts) - counts
    dest = (pad_start[sorted_e] + jnp.arange(TK) - seg_start[sorted_e]).astype(jnp.int32)
    n_pad = TK + N_EXPERTS * BM
    n_blk = n_pad // BM
    buf_tok = jnp.zeros((n_pad,), jnp.int32).at[dest].set(tok)
    blk_e = jnp.minimum(jnp.searchsorted(pad_end, jnp.arange(n_blk) * BM, side='right'),
                        N_EXPERTS - 1).astype(jnp.int32)
    n_used = (pad_end[-1] // BM).astype(jnp.int32).reshape(1)
    y = _expert_call(hf, buf_tok.reshape(n_blk, BM), blk_e, n_used, w_gate, w_up, w_down)
    pos = jnp.zeros((TK,), jnp.int32).at[order].set(dest).reshape(T, TOP_K)
    routed = _combine_call(y, pos, wts)
    shared = (jax.nn.silu(hf @ sh_gate) * (hf @ sh_up)) @ sh_down
    return routed + shared


def trunk_mix(x, mod, s0, grid, norm1_g, w_in, conv_w, a_log, dt_bias, onorm_g, pool_w, pool_scale,
              w_out, norm2_g):
    shift1, scale1, gate1, shift2, scale2, gate2 = jnp.split(mod, 6, axis=-1)
    h = rmsnorm(x, norm1_g) * (1 + scale1) + shift1
    proj = h @ w_in
    qkv, z, ba, u = jnp.split(proj, [3 * D_A, 4 * D_A, 4 * D_A + 4 * H_A], axis=-1)
    o_a, state = _delta_call(qkv, z, ba, conv_w, a_log, dt_bias, onorm_g, s0)
    o_p = pool_mixer(u, pool_w, pool_scale, grid)
    mix = jnp.concatenate([o_a.astype(x.dtype), o_p.astype(x.dtype)], axis=-1) @ w_out
    x = x + gate1 * mix
    h = rmsnorm(x, norm2_g) * (1 + scale2) + shift2
    return x, h, jnp.broadcast_to(gate2, x.shape), state


def _final_norm_kernel(x_ref, g_ref, o_ref):
    x = x_ref[...]
    y = x * lax.rsqrt(jnp.mean(x * x, axis=-1, keepdims=True) + EPS)
    o_ref[...] = y * g_ref[...]


def _final_norm(x, g):
    B, L, D = x.shape
    xf = x.reshape(B * L, D)
    tm = 512
    out = pl.pallas_call(
        _final_norm_kernel,
        grid=(B * L // tm,),
        in_specs=[pl.BlockSpec((tm, D), lambda i: (i, 0)), pl.BlockSpec((1, D), lambda i: (0, 0))],
        out_specs=pl.BlockSpec((tm, D), lambda i: (i, 0)),
        out_shape=jax.ShapeDtypeStruct((B * L, D), x.dtype),
    )(xf, g.reshape(1, D))
    return out.reshape(B, L, D)


def kernel(x_prompt, x_sample, state_delta, c, c_ctx, w_ada, b_ada, norm1_g, w_in, conv_w, a_log,
           dt_bias, onorm_g, pool_w, pool_scale, w_out, norm2_g, router_w, router_bias, exp_w_gate,
           exp_w_up, exp_w_down, sh_w_gate, sh_w_up, sh_w_down, final_g):
    xp = x_prompt
    xs = x_sample
    new_states = []
    for l in range(DEPTH):
        lw = (norm1_g[l], w_in[l], conv_w[l], a_log[l], dt_bias[l], onorm_g[l], pool_w[l], pool_scale[l],
              w_out[l], norm2_g[l])
        mod_ctx = (jax.nn.silu(c_ctx) @ w_ada[l] + b_ada[l])[None, None, :]
        mod_lat = (jax.nn.silu(c) @ w_ada[l] + b_ada[l])[:, None, :]
        xp, hp, gp, st_ctx = trunk_mix(xp, mod_ctx, None, False, *lw)
        xs, hs, gs, _ = trunk_mix(xs, mod_lat, state_delta[:, l], True, *lw)
        Tp = xp.shape[0] * xp.shape[1]
        hf = jnp.concatenate([hp.reshape(Tp, D_MODEL), hs.reshape(-1, D_MODEL)], axis=0)
        m = moe_ffn(hf, router_w[l], router_bias[l], exp_w_gate[l], exp_w_up[l], exp_w_down[l],
                    sh_w_gate[l], sh_w_up[l], sh_w_down[l])
        xp = xp + gp * m[:Tp].reshape(xp.shape)
        xs = xs + gs * m[Tp:].reshape(xs.shape)
        new_states.append(st_ctx.astype(x_prompt.dtype))
    y_prompt = _final_norm(xp, final_g)
    y_sample = _final_norm(xs, final_g)
    new_state_delta = jnp.stack(new_states, axis=1)
    return (y_prompt, y_sample, new_state_delta)
```

```python
import functools
import jax, jax.numpy as jnp
from jax import lax
from jax.experimental import pallas as pl
from jax.experimental.pallas import tpu as pltpu

D_MODEL = 1024
DEPTH = 1
GRID_W = 64
D_MIX = D_MODEL
D_A = D_MIX // 2
D_P = D_MIX - D_A
H_A = 4
DK = D_A // H_A
DV = D_A // H_A
CONV_K = 5
CHUNK = 64
POOL_WINDOWS = (2, 4, 8, 16)
N_PG = len(POOL_WINDOWS)
PG = D_P // N_PG
N_EXPERTS = 256
TOP_K = 8
N_GROUPS = 8
TOPK_GROUP = 4
ROUTED_SCALE = 2.5
EPS = 1e-6


def rmsnorm(x, g):
    x32 = x.astype(jnp.float32)
    y = x32 * lax.rsqrt(jnp.mean(x32 * x32, axis=-1, keepdims=True) + EPS)
    return (y * g.astype(jnp.float32)).astype(x.dtype)


SC = 256
CPS = SC // CHUNK
BASE = 16


def _mm(a, b):
    return jnp.dot(a.astype(jnp.bfloat16), b.astype(jnp.bfloat16), preferred_element_type=jnp.float32)


def _mm_nt(a, b):
    return lax.dot_general(a.astype(jnp.bfloat16), b.astype(jnp.bfloat16), (((1,), (1,)), ((), ())),
                           preferred_element_type=jnp.float32)


def _softplus(x):
    return jnp.maximum(x, 0.0) + jnp.log(1.0 + jnp.exp(-jnp.abs(x)))


def _delta_kernel(sc_ref, xq_ref, xk_ref, xv_ref, z_ref, bac_ref, bar_ref, cwq_ref, cwk_ref, cwv_ref,
                  og_ref, s0_ref, o_ref, st_ref, q_s, k_s, v_s, o_s, vn_s, *, n_sc, zero_init):
    h = pl.program_id(1)
    L = q_s.shape[0]

    def conv(x_ref, w_ref):
        x = x_ref[0]
        row = lax.broadcasted_iota(jnp.int32, x.shape, 0)
        acc = x * w_ref[CONV_K // 2:CONV_K // 2 + 1, :]
        for j in range(CONV_K):
            d = j - CONV_K // 2
            if d == 0:
                continue
            xs = pltpu.roll(x, (-d) % L, 0)
            ok = (row + d >= 0) & (row + d < L)
            acc = acc + jnp.where(ok, xs, 0.0) * w_ref[j:j + 1, :]
        return acc * jax.nn.sigmoid(acc)

    q = conv(xq_ref, cwq_ref)
    q_s[...] = q * lax.rsqrt(jnp.sum(q * q, axis=-1, keepdims=True) + EPS) * (DK ** -0.5)
    k = conv(xk_ref, cwk_ref)
    k_s[...] = k * lax.rsqrt(jnp.sum(k * k, axis=-1, keepdims=True) + EPS)
    v_s[...] = conv(xv_ref, cwv_ref)
    o_s[...] = jnp.zeros_like(o_s)

    ri = lax.broadcasted_iota(jnp.int32, (SC, SC), 0)
    ci = lax.broadcasted_iota(jnp.int32, (SC, SC), 1)
    same = (ri // CHUNK) == (ci // CHUNK)
    same_base = (ri // BASE) == (ci // BASE)
    merge_masks = [(ri // w) == (ci // w) for w in (2 * BASE, CHUNK)]
    eye = (ri == ci).astype(jnp.float32)
    rowi = lax.broadcasted_iota(jnp.int32, (SC, DV), 0)

    def unit(m, d, s):
        r0 = pl.multiple_of(m * SC, SC)
        q = q_s[pl.ds(r0, SC), :]
        k = k_s[pl.ds(r0, SC), :]
        v = v_s[pl.ds(r0, SC), :]
        bc = bac_ref[0, 0, pl.ds(r0, SC), :]
        br = bar_ref[0, 0, m]
        a_l = sc_ref[d * H_A + h]
        dtb = sc_ref[2 * H_A + d * H_A + h]
        neg_ea = -jnp.exp(jnp.full((1, 1), a_l, jnp.float32))
        beta = jax.nn.sigmoid(bc[:, d:d + 1])
        g_col = neg_ea * _softplus(bc[:, 2 + d:3 + d] + dtb)
        g_row = neg_ea * _softplus(br[2 + d:3 + d, :] + dtb)
        if d == 0:
            tri, strict = same & (ci <= ri), same & (ci < ri)
        else:
            tri, strict = same & (ci >= ri), same & (ci > ri)
        tri_t = same & (ri <= ci) if d == 0 else same & (ri >= ci)
        gc_col = jnp.sum(jnp.where(tri, g_row, 0.0), axis=1, keepdims=True)
        gc_row = jnp.sum(jnp.where(tri_t, g_col, 0.0), axis=0, keepdims=True)
        gl_col = jnp.sum(jnp.where(same, g_row, 0.0), axis=1, keepdims=True)
        decay = jnp.where(tri, jnp.exp(jnp.where(tri, gc_col - gc_row, 0.0)), 0.0)
        kb = k * beta
        a = jnp.where(strict, _mm_nt(kb, k) * decay, 0.0)
        attn = jnp.where(tri, _mm_nt(q, k) * decay, 0.0)
        eg = jnp.exp(gc_col)
        x = jnp.concatenate([v * beta, kb * eg], axis=1)
        a0 = jnp.where(same_base, a, 0.0)
        t = eye - a0
        p = a0
        for _ in range(BASE.bit_length() - 2):
            p = _mm(p, p)
            t = t + _mm(t, p)
        inner = same_base
        for outer in merge_masks:
            off = jnp.where(outer & ~inner, a, 0.0)
            t = t - _mm(t, _mm(off, t))
            inner = outer
        x = _mm(t, x)
        u = x[:, :DV]
        w = x[:, DV:]
        qd = q * eg
        kdt = (k * jnp.exp(gl_col - gc_col)).T
        egl = jnp.exp(gl_col)
        vn_s[d] = jnp.zeros((SC, DV), jnp.float32)
        order = range(CPS) if d == 0 else range(CPS - 1, -1, -1)
        for c in order:
            lo, hi = c * CHUNK, (c + 1) * CHUNK
            ws_qs = _mm(jnp.concatenate([w[lo:hi], qd[lo:hi]], axis=0), s)
            v_new = u[lo:hi] - ws_qs[:CHUNK]
            vn_s[d, lo:hi, :] = v_new
            vn = vn_s[d]
            o_c = ws_qs[CHUNK:] + _mm(attn[lo:hi, :], vn)
            o_s[pl.ds(r0 + lo, CHUNK), :] += o_c
            v_only = jnp.where((rowi >= lo) & (rowi < hi), vn, 0.0)
            s = s * egl[lo:lo + 1, :] + _mm(kdt, v_only)
        return s

    if zero_init:
        s_f = jnp.zeros((DK, DV), jnp.float32)
        s_b = jnp.zeros((DK, DV), jnp.float32)
    else:
        s_f = s0_ref[0, 0, 0]
        s_b = s0_ref[0, 1, 0]

    def body(m, carry):
        s_f, s_b = carry
        return unit(m, 0, s_f), unit(n_sc - 1 - m, 1, s_b)

    if n_sc == 1:
        s_f, s_b = body(0, (s_f, s_b))
    else:
        s_f, s_b = lax.fori_loop(0, n_sc, body, (s_f, s_b))

    st_ref[0, 0, 0] = s_f
    st_ref[0, 1, 0] = s_b
    o = o_s[...]
    o = o * lax.rsqrt(jnp.mean(o * o, axis=-1, keepdims=True) + EPS) * og_ref[...]
    zz = z_ref[0]
    o_ref[0] = o * (zz * jax.nn.sigmoid(zz))


def _delta_call(qkv, z, ba, conv_w, a_log, dt_bias, onorm_g, s0):
    B, L, _ = qkv.shape
    n_sc = L // SC
    bah = ba.reshape(B, L, 4, H_A).transpose(0, 3, 1, 2)
    bar = bah.reshape(B, H_A, n_sc, SC, 4).transpose(0, 1, 2, 4, 3)
    scal = jnp.concatenate([a_log.reshape(-1), dt_bias.reshape(-1)]).astype(jnp.float32)
    zero_init = s0 is None
    if zero_init:
        s0 = jnp.zeros((1, 2, 1, DK, DV), jnp.float32)
        s0_spec = pl.BlockSpec((1, 2, 1, DK, DV), lambda b, h, sc: (0, 0, 0, 0, 0))
    else:
        s0_spec = pl.BlockSpec((1, 2, 1, DK, DV), lambda b, h, sc: (b, 0, h, 0, 0))

    def col(off):
        return pl.BlockSpec((1, L, DK), lambda b, h, sc: (b, 0, off + h))

    def cw(off):
        return pl.BlockSpec((CONV_K, DK), lambda b, h, sc: (0, off + h))

    kern = functools.partial(_delta_kernel, n_sc=n_sc, zero_init=zero_init)
    return pl.pallas_call(
        kern,
        grid_spec=pltpu.PrefetchScalarGridSpec(
            num_scalar_prefetch=1,
            grid=(B, H_A),
            in_specs=[col(0), col(H_A), col(2 * H_A),
                      pl.BlockSpec((1, L, DV), lambda b, h, sc: (b, 0, h)),
                      pl.BlockSpec((1, 1, L, 4), lambda b, h, sc: (b, h, 0, 0)),
                      pl.BlockSpec((1, 1, n_sc, 4, SC), lambda b, h, sc: (b, h, 0, 0, 0)),
                      cw(0), cw(H_A), cw(2 * H_A),
                      pl.BlockSpec((1, DV), lambda b, h, sc: (0, 0)),
                      s0_spec],
            out_specs=[pl.BlockSpec((1, L, DV), lambda b, h, sc: (b, 0, h)),
                       pl.BlockSpec((1, 2, 1, DK, DV), lambda b, h, sc: (b, 0, h, 0, 0))],
            scratch_shapes=[pltpu.VMEM((L, DK), jnp.float32), pltpu.VMEM((L, DK), jnp.float32),
                            pltpu.VMEM((L, DV), jnp.float32), pltpu.VMEM((L, DV), jnp.float32),
                            pltpu.VMEM((2, SC, DV), jnp.float32)]),
        out_shape=[jax.ShapeDtypeStruct((B, L, D_A), jnp.float32),
                   jax.ShapeDtypeStruct((B, 2, H_A, DK, DV), jnp.float32)],
        compiler_params=pltpu.CompilerParams(dimension_semantics=("arbitrary", "arbitrary"),
                                             vmem_limit_bytes=48 * 1024 * 1024),
    )(scal, qkv, qkv, qkv, z, bah, bar, conv_w, conv_w, conv_w, onorm_g.reshape(1, DV), s0)


def box_sum(u, w, axis):
    L = u.shape[axis]
    cs = jnp.cumsum(u, axis=axis)
    cs = jnp.concatenate([jnp.zeros_like(lax.slice_in_dim(cs, 0, 1, axis=axis)), cs], axis=axis)
    t = jnp.arange(L)
    lo = jnp.clip(t - w // 2, 0, L)
    hi = jnp.clip(t + w - w // 2, 0, L)
    s = jnp.take(cs, hi, axis=axis) - jnp.take(cs, lo, axis=axis)
    return s, (hi - lo).astype(u.dtype)


def pool_mixer(u, pool_w, pool_scale, grid):
    f32 = jnp.float32
    B, L, _ = u.shape
    u32 = u.astype(f32)
    outs = []
    for i, w in enumerate(POOL_WINDOWS):
        ui = u32[..., i * PG:(i + 1) * PG]
        if grid:
            rows = L // GRID_W
            ug = ui.reshape(B, rows, GRID_W, PG)
            s, cr = box_sum(ug, w, 1)
            s, cc = box_sum(s, w, 2)
            mean = (s / (cr[None, :, None, None] * cc[None, None, :, None])).reshape(B, L, PG)
        else:
            s, cnt = box_sum(ui, w, 1)
            mean = s / cnt[None, :, None]
        outs.append(jnp.einsum('blc,cd->bld', mean - ui, pool_w[i].astype(f32)))
    return jnp.concatenate(outs, axis=-1) * pool_scale.astype(f32)


TR = 256
GSZ = N_EXPERTS // N_GROUPS
NEG = -jnp.inf


def _split_bf16(a):
    hi = a.astype(jnp.bfloat16)
    return hi, (a - hi.astype(jnp.float32)).astype(jnp.bfloat16)


def _col_to_row(col, eye_mask):
    return jnp.sum(jnp.where(eye_mask, col, jnp.zeros_like(col)), axis=0, keepdims=True)


def _route_kernel(h_ref, rwh_ref, rwl_ref, rb_ref, idx_ref, rank_ref, w_ref, cnt_ref, cnt_s):
    i = pl.program_id(0)

    @pl.when(i == 0)
    def _():
        cnt_s[...] = jnp.zeros_like(cnt_s)

    h = h_ref[...]
    hh, hl = _split_bf16(h)

    def f(x, y):
        return jnp.dot(x, y, preferred_element_type=jnp.float32)

    logits = f(hh, rwh_ref[...]) + (f(hh, rwl_ref[...]) + f(hl, rwh_ref[...]))
    scores = jax.nn.sigmoid(logits)
    sel = scores + rb_ref[...]
    lane = lax.broadcasted_iota(jnp.int32, sel.shape, 1)
    gid = lane // GSZ

    def first_argmax(v):
        m = jnp.max(v, axis=1, keepdims=True)
        first = jnp.min(jnp.where(v == m, lane, N_EXPERTS), axis=1, keepdims=True)
        return m, first

    gscore = []
    for g in range(N_GROUPS):
        vg = jnp.where(gid == g, sel, NEG)
        m1, i1 = first_argmax(vg)
        m2 = jnp.max(jnp.where(lane == i1, NEG, vg), axis=1, keepdims=True)
        gscore.append(m1 + m2)
    emask = jnp.zeros(sel.shape, jnp.bool_)
    for g in range(N_GROUPS):
        beat = jnp.zeros(gscore[g].shape, jnp.int32)
        for o in range(N_GROUPS):
            if o == g:
                continue
            wins = (gscore[o] > gscore[g]) | ((gscore[o] == gscore[g]) & (o < g))
            beat = beat + wins.astype(jnp.int32)
        emask = emask | ((gid == g) & (beat < TOPK_GROUP))
    cand = jnp.where(emask, sel, NEG)
    chosen = []
    picked = jnp.zeros(sel.shape, jnp.bool_)
    for _ in range(TOP_K):
        _, ik = first_argmax(cand)
        hit = lane == ik
        chosen.append((ik, hit))
        picked = picked | hit
        cand = jnp.where(hit, NEG, cand)
    wraw = jnp.where(picked, scores, 0.0)
    wmat = wraw / jnp.sum(wraw, axis=1, keepdims=True) * ROUTED_SCALE

    pm = picked.astype(jnp.bfloat16)
    ri = lax.broadcasted_iota(jnp.int32, (TR, TR), 0)
    ci = lax.broadcasted_iota(jnp.int32, (TR, TR), 1)
    earlier = (ci < ri).astype(jnp.bfloat16)
    rank_mat = jnp.dot(earlier, pm, preferred_element_type=jnp.float32) + cnt_s[...]
    cnt_s[...] = cnt_s[...] + jnp.sum(picked.astype(jnp.float32), axis=0, keepdims=True)
    cnt_ref[...] = cnt_s[...].astype(jnp.int32)

    eye = ri == ci
    lane8 = lax.broadcasted_iota(jnp.int32, (TR, TOP_K), 1)
    wcols = jnp.zeros((TR, TOP_K), jnp.float32)
    for k, (ik, hit) in enumerate(chosen):
        rk = jnp.sum(jnp.where(hit, rank_mat, 0.0), axis=1, keepdims=True)
        wk = jnp.sum(jnp.where(hit, wmat, 0.0), axis=1, keepdims=True)
        idx_ref[0, k:k + 1, :] = _col_to_row(ik, eye)
        rank_ref[0, k:k + 1, :] = _col_to_row(rk, eye).astype(jnp.int32)
        wcols = jnp.where(lane8 == k, wk, wcols)
    w_ref[...] = wcols


def _route_call(hf, router_w, router_bias):
    T, D = hf.shape
    n_tiles = T // TR
    rwh, rwl = _split_bf16(router_w)
    return pl.pallas_call(
        _route_kernel,
        grid=(n_tiles,),
        in_specs=[pl.BlockSpec((TR, D), lambda i: (i, 0)),
                  pl.BlockSpec((D, N_EXPERTS), lambda i: (0, 0)),
                  pl.BlockSpec((D, N_EXPERTS), lambda i: (0, 0)),
                  pl.BlockSpec((1, N_EXPERTS), lambda i: (0, 0))],
        out_specs=[pl.BlockSpec((1, TOP_K, TR), lambda i: (i, 0, 0)),
                   pl.BlockSpec((1, TOP_K, TR), lambda i: (i, 0, 0)),
                   pl.BlockSpec((TR, TOP_K), lambda i: (i, 0)),
                   pl.BlockSpec((1, N_EXPERTS), lambda i: (0, 0))],
        scratch_shapes=[pltpu.VMEM((1, N_EXPERTS), jnp.float32)],
        out_shape=[jax.ShapeDtypeStruct((n_tiles, TOP_K, TR), jnp.int32),
                   jax.ShapeDtypeStruct((n_tiles, TOP_K, TR), jnp.int32),
                   jax.ShapeDtypeStruct((T, TOP_K), jnp.float32),
                   jax.ShapeDtypeStruct((1, N_EXPERTS), jnp.int32)],
        compiler_params=pltpu.CompilerParams(dimension_semantics=("arbitrary",)),
    )(hf, rwh, rwl, router_bias.reshape(1, N_EXPERTS).astype(jnp.float32))


def _dispatch_kernel(idx_ref, rank_ref, pstart_ref, h_hbm, xs_init, pos_ref, xs_hbm, pos_v, pos_s, ssem, psem):
    del xs_init
    i = pl.program_id(0)
    erow = lax.broadcasted_iota(jnp.int32, (N_EXPERTS, TR), 0)
    pstart = pstart_ref[...]
    for k in range(TOP_K):
        hit = erow == idx_ref[0, k:k + 1, :]
        seg = jnp.sum(jnp.where(hit, pstart, 0), axis=0, keepdims=True)
        pos_v[k:k + 1, :] = seg + rank_ref[0, k:k + 1, :]
    pos_ref[0] = pos_v[...]
    cp = pltpu.make_async_copy(pos_v, pos_s, psem)
    cp.start()
    cp.wait()

    def body(t, carry):
        for k in range(TOP_K):
            pltpu.make_async_copy(h_hbm.at[i * TR + t], xs_hbm.at[pos_s[k, t]], ssem).start()
        return carry

    lax.fori_loop(0, TR, body, 0, unroll=8)
    n_rows = TR * TOP_K
    pltpu.make_async_copy(xs_hbm.at[pl.ds(0, n_rows)], xs_hbm.at[pl.ds(0, n_rows)], ssem).wait()


def _dispatch_call(hf, idx, rank, pad_start, n_pad):
    T, D = hf.shape
    n_tiles = T // TR
    return pl.pallas_call(
        _dispatch_kernel,
        grid=(n_tiles,),
        in_specs=[pl.BlockSpec((1, TOP_K, TR), lambda i: (i, 0, 0)),
                  pl.BlockSpec((1, TOP_K, TR), lambda i: (i, 0, 0)),
                  pl.BlockSpec((N_EXPERTS, 1), lambda i: (0, 0)),
                  pl.BlockSpec(memory_space=pl.ANY),
                  pl.BlockSpec(memory_space=pl.ANY)],
        out_specs=[pl.BlockSpec((1, TOP_K, TR), lambda i: (i, 0, 0)),
                   pl.BlockSpec(memory_space=pl.ANY)],
        scratch_shapes=[pltpu.VMEM((TOP_K, TR), jnp.int32), pltpu.SMEM((TOP_K, TR), jnp.int32),
                        pltpu.SemaphoreType.DMA, pltpu.SemaphoreType.DMA],
        out_shape=[jax.ShapeDtypeStruct((n_tiles, TOP_K, TR), jnp.int32),
                   jax.ShapeDtypeStruct((n_pad, D), hf.dtype)],
        input_output_aliases={4: 1},
        compiler_params=pltpu.CompilerParams(dimension_semantics=("arbitrary",)),
    )(idx, rank, pad_start.reshape(N_EXPERTS, 1), hf, jnp.zeros((n_pad, D), hf.dtype))


BM = 256


def _expert_kernel(blk_e_ref, nvalid_ref, nused_ref, x_ref, wg_ref, wu_ref, wd_ref, y_ref, wg_s, wu_s, wd_s):
    i = pl.program_id(0)

    @pl.when(i < nused_ref[0])
    def _():
        e = blk_e_ref[i]
        prev = blk_e_ref[jnp.maximum(i - 1, 0)]

        @pl.when((i == 0) | (e != prev))
        def _():
            wg_s[...] = wg_ref[0].astype(jnp.bfloat16)
            wu_s[...] = wu_ref[0].astype(jnp.bfloat16)
            wd_s[...] = wd_ref[0].astype(jnp.bfloat16)

        row = lax.broadcasted_iota(jnp.int32, (BM, 1), 0)
        x = jnp.where(row < nvalid_ref[i], x_ref[...], 0.0).astype(jnp.bfloat16)
        g = jnp.dot(x, wg_s[...], preferred_element_type=jnp.float32)
        u = jnp.dot(x, wu_s[...], preferred_element_type=jnp.float32)
        a = (g * jax.nn.sigmoid(g)) * u
        y_ref[...] = jnp.dot(a.astype(jnp.bfloat16), wd_s[...], preferred_element_type=jnp.float32)

    @pl.when(i >= nused_ref[0])
    def _():
        y_ref[...] = jnp.zeros_like(y_ref)


def _expert_call(x_sorted, blk_e, n_valid, n_used, w_gate, w_up, w_down):
    n_pad, D = x_sorted.shape
    n_blk = n_pad // BM
    E, _, F = w_gate.shape

    def row_map(i, be, nv, nu):
        return (jnp.minimum(i, nu[0] - 1), 0)

    def w_map(i, be, nv, nu):
        return (be[jnp.minimum(i, nu[0] - 1)], 0, 0)

    return pl.pallas_call(
        _expert_kernel,
        grid_spec=pltpu.PrefetchScalarGridSpec(
            num_scalar_prefetch=3,
            grid=(n_blk,),
            in_specs=[pl.BlockSpec((BM, D), row_map),
                      pl.BlockSpec((1, D, F), w_map),
                      pl.BlockSpec((1, D, F), w_map),
                      pl.BlockSpec((1, F, D), w_map)],
            out_specs=pl.BlockSpec((BM, D), lambda i, be, nv, nu: (i, 0)),
            scratch_shapes=[pltpu.VMEM((D, F), jnp.bfloat16), pltpu.VMEM((D, F), jnp.bfloat16),
                            pltpu.VMEM((F, D), jnp.bfloat16)]),
        out_shape=jax.ShapeDtypeStruct((n_pad, D), jnp.float32),
        compiler_params=pltpu.CompilerParams(dimension_semantics=("arbitrary",),
                                             vmem_limit_bytes=48 * 1024 * 1024),
    )(blk_e, n_valid, n_used, x_sorted, w_gate, w_up, w_down)


TC = 64


def _combine_kernel(pos_hbm, y_hbm, w_ref, out_ref, ybuf, pos_s, gsem, psem):
    j = pl.program_id(0)
    last = pl.num_programs(0) - 1

    def pos_copy(b, slot):
        return pltpu.make_async_copy(pos_hbm.at[b], pos_s.at[slot], psem.at[slot])

    def start_gather(slot):
        for k in range(TOP_K):
            for t in range(TC):
                pltpu.make_async_copy(y_hbm.at[pos_s[slot, k * TC + t]], ybuf.at[slot, k, t],
                                      gsem.at[slot]).start()

    def wait_gather(slot):
        pltpu.make_async_copy(ybuf.at[slot], ybuf.at[slot], gsem.at[slot]).wait()

    @pl.when(j == 0)
    def _():
        pos_copy(0, 0).start()
        pos_copy(0, 0).wait()
        start_gather(0)
        pos_copy(jnp.minimum(1, last), 1).start()

    slot = j % 2
    nslot = 1 - slot
    pos_copy(0, nslot).wait()
    start_gather(nslot)
    pos_copy(jnp.minimum(j + 2, last), slot).start()
    wait_gather(slot)
    w = w_ref[...]
    acc = w[:, 0:1] * ybuf[slot, 0]
    for k in range(1, TOP_K):
        acc = acc + w[:, k:k + 1] * ybuf[slot, k]
    out_ref[...] = acc

    @pl.when(j == last)
    def _():
        wait_gather(nslot)
        pos_copy(0, slot).wait()


def _combine_call(y_sorted, pos_t, wts):
    T, K = wts.shape
    D = y_sorted.shape[1]
    n_tiles = T // TC
    return pl.pallas_call(
        _combine_kernel,
        grid=(n_tiles,),
        in_specs=[pl.BlockSpec(memory_space=pl.ANY),
                  pl.BlockSpec(memory_space=pl.ANY),
                  pl.BlockSpec((TC, K), lambda j: (j, 0))],
        out_specs=pl.BlockSpec((TC, D), lambda j: (j, 0)),
        scratch_shapes=[pltpu.VMEM((2, K, TC, D), jnp.float32),
                        pltpu.SMEM((2, K * TC), jnp.int32),
                        pltpu.SemaphoreType.DMA((2,)), pltpu.SemaphoreType.DMA((2,))],
        out_shape=jax.ShapeDtypeStruct((T, D), jnp.float32),
        compiler_params=pltpu.CompilerParams(dimension_semantics=("arbitrary",)),
    )(pos_t, y_sorted, wts)


def moe_ffn(hf, router_w, router_bias, w_gate, w_up, w_down, sh_gate, sh_up, sh_down):
    T, D = hf.shape
    idx, rank, wts, cnt = _route_call(hf, router_w, router_bias)
    counts = cnt[0]
    padded = (counts + BM - 1) // BM * BM
    pad_end = jnp.cumsum(padded)
    pad_start = (pad_end - padded).astype(jnp.int32)
    n_pad = T * TOP_K + N_EXPERTS * BM
    n_blk = n_pad // BM
    pos, x_sorted = _dispatch_call(hf, idx, rank, pad_start, n_pad)
    blk_row0 = jnp.arange(n_blk, dtype=jnp.int32) * BM
    blk_e = jnp.minimum(jnp.searchsorted(pad_end, blk_row0, side='right'), N_EXPERTS - 1).astype(jnp.int32)
    n_valid = jnp.clip(pad_start[blk_e] + counts[blk_e] - blk_row0, 0, BM).astype(jnp.int32)
    n_used = (pad_end[-1] // BM).astype(jnp.int32).reshape(1)
    y = _expert_call(x_sorted, blk_e, n_valid, n_used, w_gate, w_up, w_down)
    n_tiles = T // TR
    pos_t = pos.reshape(n_tiles, TOP_K, TR // TC, TC).transpose(0, 2, 1, 3).reshape(T // TC, TOP_K * TC)
    routed = _combine_call(y, pos_t, wts)
    shared = (jax.nn.silu(hf @ sh_gate) * (hf @ sh_up)) @ sh_down
    return routed + shared


def trunk_mix(x, mod, s0, grid, norm1_g, w_in, conv_w, a_log, dt_bias, onorm_g, pool_w, pool_scale,
              w_out, norm2_g):
    shift1, scale1, gate1, shift2, scale2, gate2 = jnp.split(mod, 6, axis=-1)
    h = rmsnorm(x, norm1_g) * (1 + scale1) + shift1
    proj = h @ w_in
    qkv, z, ba, u = jnp.split(proj, [3 * D_A, 4 * D_A, 4 * D_A + 4 * H_A], axis=-1)
    o_a, state = _delta_call(qkv, z, ba, conv_w, a_log, dt_bias, onorm_g, s0)
    o_p = pool_mixer(u, pool_w, pool_scale, grid)
    mix = jnp.concatenate([o_a.astype(x.dtype), o_p.astype(x.dtype)], axis=-1) @ w_out
    x = x + gate1 * mix
    h = rmsnorm(x, norm2_g) * (1 + scale2) + shift2
    return x, h, jnp.broadcast_to(gate2, x.shape), state


def _final_norm_kernel(x_ref, g_ref, o_ref):
    x = x_ref[...]
    y = x * lax.rsqrt(jnp.mean(x * x, axis=-1, keepdims=True) + EPS)
    o_ref[...] = y * g_ref[...]


def _final_norm(x, g):
    B, L, D = x.shape
    xf = x.reshape(B * L, D)
    tm = 512
    out = pl.pallas_call(
        _final_norm_kernel,
        grid=(B * L // tm,),
        in_specs=[pl.BlockSpec((tm, D), lambda i: (i, 0)), pl.BlockSpec((1, D), lambda i: (0, 0))],
        out_specs=pl.BlockSpec((tm, D), lambda i: (i, 0)),
        out_shape=jax.ShapeDtypeStruct((B * L, D), x.dtype),
    )(xf, g.reshape(1, D))
    return out.reshape(B, L, D)


def kernel(x_prompt, x_sample, state_delta, c, c_ctx, w_ada, b_ada, norm1_g, w_in, conv_w, a_log,
           dt_bias, onorm_g, pool_w, pool_scale, w_out, norm2_g, router_w, router_bias, exp_w_gate,
           exp_w_up, exp_w_down, sh_w_gate, sh_w_up, sh_w_down, final_g):
    xp = x_prompt
    xs = x_sample
    new_states = []
    for l in range(DEPTH):
        lw = (norm1_g[l], w_in[l], conv_w[l], a_log[l], dt_bias[l], onorm_g[l], pool_w[l], pool_scale[l],
              w_out[l], norm2_g[l])
        mod_ctx = (jax.nn.silu(c_ctx) @ w_ada[l] + b_ada[l])[None, None, :]
        mod_lat = (jax.nn.silu(c) @ w_ada[l] + b_ada[l])[:, None, :]
        xp, hp, gp, st_ctx = trunk_mix(xp, mod_ctx, None, False, *lw)
        xs, hs, gs, _ = trunk_mix(xs, mod_lat, state_delta[:, l], True, *lw)
        Tp = xp.shape[0] * xp.shape[1]
        hf = jnp.concatenate([hp.reshape(Tp, D_MODEL), hs.reshape(-1, D_MODEL)], axis=0)
        m = moe_ffn(hf, router_w[l], router_bias[l], exp_w_gate[l], exp_w_up[l], exp_w_down[l],
                    sh_w_gate[l], sh_w_up[l], sh_w_down[l])
        xp = xp + gp * m[:Tp].reshape(xp.shape)
        xs = xs + gs * m[Tp:].reshape(xs.shape)
        new_states.append(st_ctx.astype(x_prompt.dtype))
    y_prompt = _final_norm(xp, final_g)
    y_sample = _final_norm(xs, final_g)
    new_state_delta = jnp.stack(new_states, axis=1)
    return (y_prompt, y_sample, new_state_delta)
```

```python
import functools
import jax, jax.numpy as jnp
from jax import lax
from jax.experimental import pallas as pl
from jax.experimental.pallas import tpu as pltpu

D_MODEL = 1024
DEPTH = 1
GRID_W = 64
D_MIX = D_MODEL
D_A = D_MIX // 2
D_P = D_MIX - D_A
H_A = 4
DK = D_A // H_A
DV = D_A // H_A
CONV_K = 5
CHUNK = 64
POOL_WINDOWS = (2, 4, 8, 16)
N_PG = len(POOL_WINDOWS)
PG = D_P // N_PG
N_EXPERTS = 256
TOP_K = 8
N_GROUPS = 8
TOPK_GROUP = 4
ROUTED_SCALE = 2.5
EPS = 1e-6


def rmsnorm(x, g):
    x32 = x.astype(jnp.float32)
    y = x32 * lax.rsqrt(jnp.mean(x32 * x32, axis=-1, keepdims=True) + EPS)
    return (y * g.astype(jnp.float32)).astype(x.dtype)


SC = 256
CPS = SC // CHUNK
BASE = 16


def _mm(a, b):
    return jnp.dot(a.astype(jnp.bfloat16), b.astype(jnp.bfloat16), preferred_element_type=jnp.float32)


def _mm_nt(a, b):
    return lax.dot_general(a.astype(jnp.bfloat16), b.astype(jnp.bfloat16), (((1,), (1,)), ((), ())),
                           preferred_element_type=jnp.float32)


def _softplus(x):
    return jnp.maximum(x, 0.0) + jnp.log(1.0 + jnp.exp(-jnp.abs(x)))


def _delta_kernel(sc_ref, xq_ref, xk_ref, xv_ref, z_ref, bac_ref, bar_ref, cwq_ref, cwk_ref, cwv_ref,
                  og_ref, s0_ref, o_ref, st_ref, q_s, k_s, v_s, o_s, vn_s, *, n_sc, zero_init):
    h = pl.program_id(1)
    L = q_s.shape[0]

    def conv(x_ref, w_ref):
        x = x_ref[0]
        row = lax.broadcasted_iota(jnp.int32, x.shape, 0)
        acc = x * w_ref[CONV_K // 2:CONV_K // 2 + 1, :]
        for j in range(CONV_K):
            d = j - CONV_K // 2
            if d == 0:
                continue
            xs = pltpu.roll(x, (-d) % L, 0)
            ok = (row + d >= 0) & (row + d < L)
            acc = acc + jnp.where(ok, xs, 0.0) * w_ref[j:j + 1, :]
        return acc * jax.nn.sigmoid(acc)

    q = conv(xq_ref, cwq_ref)
    q_s[...] = q * lax.rsqrt(jnp.sum(q * q, axis=-1, keepdims=True) + EPS) * (DK ** -0.5)
    k = conv(xk_ref, cwk_ref)
    k_s[...] = k * lax.rsqrt(jnp.sum(k * k, axis=-1, keepdims=True) + EPS)
    v_s[...] = conv(xv_ref, cwv_ref)
    o_s[...] = jnp.zeros_like(o_s)

    ri = lax.broadcasted_iota(jnp.int32, (SC, SC), 0)
    ci = lax.broadcasted_iota(jnp.int32, (SC, SC), 1)
    same = (ri // CHUNK) == (ci // CHUNK)
    same_base = (ri // BASE) == (ci // BASE)
    merge_masks = [(ri // w) == (ci // w) for w in (2 * BASE, CHUNK)]
    eye = (ri == ci).astype(jnp.float32)
    rowi = lax.broadcasted_iota(jnp.int32, (SC, DV), 0)

    def unit(m, d, s):
        r0 = pl.multiple_of(m * SC, SC)
        q = q_s[pl.ds(r0, SC), :]
        k = k_s[pl.ds(r0, SC), :]
        v = v_s[pl.ds(r0, SC), :]
        bc = bac_ref[0, 0, pl.ds(r0, SC), :]
        br = bar_ref[0, 0, m]
        a_l = sc_ref[d * H_A + h]
        dtb = sc_ref[2 * H_A + d * H_A + h]
        neg_ea = -jnp.exp(jnp.full((1, 1), a_l, jnp.float32))
        beta = jax.nn.sigmoid(bc[:, d:d + 1])
        g_col = neg_ea * _softplus(bc[:, 2 + d:3 + d] + dtb)
        g_row = neg_ea * _softplus(br[2 + d:3 + d, :] + dtb)
        if d == 0:
            tri, strict = same & (ci <= ri), same & (ci < ri)
        else:
            tri, strict = same & (ci >= ri), same & (ci > ri)
        tri_t = same & (ri <= ci) if d == 0 else same & (ri >= ci)
        gc_col = jnp.sum(jnp.where(tri, g_row, 0.0), axis=1, keepdims=True)
        gc_row = jnp.sum(jnp.where(tri_t, g_col, 0.0), axis=0, keepdims=True)
        gl_col = jnp.sum(jnp.where(same, g_row, 0.0), axis=1, keepdims=True)
        decay = jnp.where(tri, jnp.exp(jnp.where(tri, gc_col - gc_row, 0.0)), 0.0)
        kb = k * beta
        a = jnp.where(strict, _mm_nt(kb, k) * decay, 0.0)
        attn = jnp.where(tri, _mm_nt(q, k) * decay, 0.0)
        eg = jnp.exp(gc_col)
        x = jnp.concatenate([v * beta, kb * eg], axis=1)
        a0 = jnp.where(same_base, a, 0.0)
        t = eye - a0
        p = a0
        for _ in range(BASE.bit_length() - 2):
            p = _mm(p, p)
            t = t + _mm(t, p)
        inner = same_base
        for outer in merge_masks:
            off = jnp.where(outer & ~inner, a, 0.0)
            t = t - _mm(t, _mm(off, t))
            inner = outer
        x = _mm(t, x)
        u = x[:, :DV]
        w = x[:, DV:]
        qd = q * eg
        kdt = (k * jnp.exp(gl_col - gc_col)).T
        egl = jnp.exp(gl_col)
        vn_s[d] = jnp.zeros((SC, DV), jnp.float32)
        order = range(CPS) if d == 0 else range(CPS - 1, -1, -1)
        for c in order:
            lo, hi = c * CHUNK, (c + 1) * CHUNK
            ws_qs = _mm(jnp.concatenate([w[lo:hi], qd[lo:hi]], axis=0), s)
            v_new = u[lo:hi] - ws_qs[:CHUNK]
            vn_s[d, lo:hi, :] = v_new
            vn = vn_s[d]
            o_c = ws_qs[CHUNK:] + _mm(attn[lo:hi, :], vn)
            o_s[pl.ds(r0 + lo, CHUNK), :] += o_c
            v_only = jnp.where((rowi >= lo) & (rowi < hi), vn, 0.0)
            s = s * egl[lo:lo + 1, :] + _mm(kdt, v_only)
        return s

    if zero_init:
        s_f = jnp.zeros((DK, DV), jnp.float32)
        s_b = jnp.zeros((DK, DV), jnp.float32)
    else:
        s_f = s0_ref[0, 0, 0]
        s_b = s0_ref[0, 1, 0]

    def body(m, carry):
        s_f, s_b = carry
        return unit(m, 0, s_f), unit(n_sc - 1 - m, 1, s_b)

    if n_sc == 1:
        s_f, s_b = body(0, (s_f, s_b))
    else:
        s_f, s_b = lax.fori_loop(0, n_sc, body, (s_f, s_b))

    st_ref[0, 0, 0] = s_f
    st_ref[0, 1, 0] = s_b
    o = o_s[...]
    o = o * lax.rsqrt(jnp.mean(o * o, axis=-1, keepdims=True) + EPS) * og_ref[...]
    zz = z_ref[0]
    o_ref[0] = o * (zz * jax.nn.sigmoid(zz))


def _delta_call(qkv, z, ba, conv_w, a_log, dt_bias, onorm_g, s0):
    B, L, _ = qkv.shape
    n_sc = L // SC
    bah = ba.reshape(B, L, 4, H_A).transpose(0, 3, 1, 2)
    bar = bah.reshape(B, H_A, n_sc, SC, 4).transpose(0, 1, 2, 4, 3)
    scal = jnp.concatenate([a_log.reshape(-1), dt_bias.reshape(-1)]).astype(jnp.float32)
    zero_init = s0 is None
    if zero_init:
        s0 = jnp.zeros((1, 2, 1, DK, DV), jnp.float32)
        s0_spec = pl.BlockSpec((1, 2, 1, DK, DV), lambda b, h, sc: (0, 0, 0, 0, 0))
    else:
        s0_spec = pl.BlockSpec((1, 2, 1, DK, DV), lambda b, h, sc: (b, 0, h, 0, 0))

    def col(off):
        return pl.BlockSpec((1, L, DK), lambda b, h, sc: (b, 0, off + h))

    def cw(off):
        return pl.BlockSpec((CONV_K, DK), lambda b, h, sc: (0, off + h))

    kern = functools.partial(_delta_kernel, n_sc=n_sc, zero_init=zero_init)
    return pl.pallas_call(
        kern,
        grid_spec=pltpu.PrefetchScalarGridSpec(
            num_scalar_prefetch=1,
            grid=(B, H_A),
            in_specs=[col(0), col(H_A), col(2 * H_A),
                      pl.BlockSpec((1, L, DV), lambda b, h, sc: (b, 0, h)),
                      pl.BlockSpec((1, 1, L, 4), lambda b, h, sc: (b, h, 0, 0)),
                      pl.BlockSpec((1, 1, n_sc, 4, SC), lambda b, h, sc: (b, h, 0, 0, 0)),
                      cw(0), cw(H_A), cw(2 * H_A),
                      pl.BlockSpec((1, DV), lambda b, h, sc: (0, 0)),
                      s0_spec],
            out_specs=[pl.BlockSpec((1, L, DV), lambda b, h, sc: (b, 0, h)),
                       pl.BlockSpec((1, 2, 1, DK, DV), lambda b, h, sc: (b, 0, h, 0, 0))],
            scratch_shapes=[pltpu.VMEM((L, DK), jnp.float32), pltpu.VMEM((L, DK), jnp.float32),
                            pltpu.VMEM((L, DV), jnp.float32), pltpu.VMEM((L, DV), jnp.float32),
                            pltpu.VMEM((2, SC, DV), jnp.float32)]),
        out_shape=[jax.ShapeDtypeStruct((B, L, D_A), jnp.float32),
                   jax.ShapeDtypeStruct((B, 2, H_A, DK, DV), jnp.float32)],
        compiler_params=pltpu.CompilerParams(dimension_semantics=("arbitrary", "arbitrary"),
                                             vmem_limit_bytes=48 * 1024 * 1024),
    )(scal, qkv, qkv, qkv, z, bah, bar, conv_w, conv_w, conv_w, onorm_g.reshape(1, DV), s0)


def box_sum(u, w, axis):
    L = u.shape[axis]
    cs = jnp.cumsum(u, axis=axis)
    cs = jnp.concatenate([jnp.zeros_like(lax.slice_in_dim(cs, 0, 1, axis=axis)), cs], axis=axis)
    t = jnp.arange(L)
    lo = jnp.clip(t - w // 2, 0, L)
    hi = jnp.clip(t + w - w // 2, 0, L)
    s = jnp.take(cs, hi, axis=axis) - jnp.take(cs, lo, axis=axis)
    return s, (hi - lo).astype(u.dtype)


def pool_mixer(u, pool_w, pool_scale, grid):
    f32 = jnp.float32
    B, L, _ = u.shape
    u32 = u.astype(f32)
    outs = []
    for i, w in enumerate(POOL_WINDOWS):
        ui = u32[..., i * PG:(i + 1) * PG]
        if grid:
            rows = L // GRID_W
            ug = ui.reshape(B, rows, GRID_W, PG)
            s, cr = box_sum(ug, w, 1)
            s, cc = box_sum(s, w, 2)
            mean = (s / (cr[None, :, None, None] * cc[None, None, :, None])).reshape(B, L, PG)
        else:
            s, cnt = box_sum(ui, w, 1)
            mean = s / cnt[None, :, None]
        outs.append(jnp.einsum('blc,cd->bld', mean - ui, pool_w[i].astype(f32)))
    return jnp.concatenate(outs, axis=-1) * pool_scale.astype(f32)


TR = 256
GSZ = N_EXPERTS // N_GROUPS
NEG = -jnp.inf


def _split_bf16(a):
    hi = a.astype(jnp.bfloat16)
    return hi, (a - hi.astype(jnp.float32)).astype(jnp.bfloat16)


def _col_to_row(col, eye_mask):
    return jnp.sum(jnp.where(eye_mask, col, jnp.zeros_like(col)), axis=0, keepdims=True)


def _route_kernel(h_ref, rwh_ref, rwl_ref, rb_ref, idx_ref, rank_ref, w_ref, cnt_ref, cnt_s):
    i = pl.program_id(0)

    @pl.when(i == 0)
    def _():
        cnt_s[...] = jnp.zeros_like(cnt_s)

    h = h_ref[...]
    hh, hl = _split_bf16(h)

    def f(x, y):
        return jnp.dot(x, y, preferred_element_type=jnp.float32)

    logits = f(hh, rwh_ref[...]) + (f(hh, rwl_ref[...]) + f(hl, rwh_ref[...]))
    scores = jax.nn.sigmoid(logits)
    sel = scores + rb_ref[...]
    lane = lax.broadcasted_iota(jnp.int32, sel.shape, 1)
    gid = lane // GSZ

    def first_argmax(v):
        m = jnp.max(v, axis=1, keepdims=True)
        first = jnp.min(jnp.where(v == m, lane, N_EXPERTS), axis=1, keepdims=True)
        return m, first

    gscore = []
    for g in range(N_GROUPS):
        vg = jnp.where(gid == g, sel, NEG)
        m1, i1 = first_argmax(vg)
        m2 = jnp.max(jnp.where(lane == i1, NEG, vg), axis=1, keepdims=True)
        gscore.append(m1 + m2)
    emask = jnp.zeros(sel.shape, jnp.bool_)
    for g in range(N_GROUPS):
        beat = jnp.zeros(gscore[g].shape, jnp.int32)
        for o in range(N_GROUPS):
            if o == g:
                continue
            wins = (gscore[o] > gscore[g]) | ((gscore[o] == gscore[g]) & (o < g))
            beat = beat + wins.astype(jnp.int32)
        emask = emask | ((gid == g) & (beat < TOPK_GROUP))
    cand = jnp.where(emask, sel, NEG)
    chosen = []
    picked = jnp.zeros(sel.shape, jnp.bool_)
    for _ in range(TOP_K):
        _, ik = first_argmax(cand)
        hit = lane == ik
        chosen.append((ik, hit))
        picked = picked | hit
        cand = jnp.where(hit, NEG, cand)
    wraw = jnp.where(picked, scores, 0.0)
    wmat = wraw / jnp.sum(wraw, axis=1, keepdims=True) * ROUTED_SCALE

    pm = picked.astype(jnp.bfloat16)
    ri = lax.broadcasted_iota(jnp.int32, (TR, TR), 0)
    ci = lax.broadcasted_iota(jnp.int32, (TR, TR), 1)
    earlier = (ci < ri).astype(jnp.bfloat16)
    rank_mat = jnp.dot(earlier, pm, preferred_element_type=jnp.float32) + cnt_s[...]
    cnt_s[...] = cnt_s[...] + jnp.sum(picked.astype(jnp.float32), axis=0, keepdims=True)
    cnt_ref[...] = cnt_s[...].astype(jnp.int32)

    eye = ri == ci
    lane8 = lax.broadcasted_iota(jnp.int32, (TR, TOP_K), 1)
    wcols = jnp.zeros((TR, TOP_K), jnp.float32)
    for k, (ik, hit) in enumerate(chosen):
        rk = jnp.sum(jnp.where(hit, rank_mat, 0.0), axis=1, keepdims=True)
        wk = jnp.sum(jnp.where(hit, wmat, 0.0), axis=1, keepdims=True)
        idx_ref[0, k:k + 1, :] = _col_to_row(ik, eye)
        rank_ref[0, k:k + 1, :] = _col_to_row(rk, eye).astype(jnp.int32)
        wcols = jnp.where(lane8 == k, wk, wcols)
    w_ref[...] = wcols


def _route_call(hf, router_w, router_bias):
    T, D = hf.shape
    n_tiles = T // TR
    rwh, rwl = _split_bf16(router_w)
    return pl.pallas_call(
        _route_kernel,
        grid=(n_tiles,),
        in_specs=[pl.BlockSpec((TR, D), lambda i: (i, 0)),
                  pl.BlockSpec((D, N_EXPERTS), lambda i: (0, 0)),
                  pl.BlockSpec((D, N_EXPERTS), lambda i: (0, 0)),
                  pl.BlockSpec((1, N_EXPERTS), lambda i: (0, 0))],
        out_specs=[pl.BlockSpec((1, TOP_K, TR), lambda i: (i, 0, 0)),
                   pl.BlockSpec((1, TOP_K, TR), lambda i: (i, 0, 0)),
                   pl.BlockSpec((TR, TOP_K), lambda i: (i, 0)),
                   pl.BlockSpec((1, N_EXPERTS), lambda i: (0, 0))],
        scratch_shapes=[pltpu.VMEM((1, N_EXPERTS), jnp.float32)],
        out_shape=[jax.ShapeDtypeStruct((n_tiles, TOP_K, TR), jnp.int32),
                   jax.ShapeDtypeStruct((n_tiles, TOP_K, TR), jnp.int32),
                   jax.ShapeDtypeStruct((T, TOP_K), jnp.float32),
                   jax.ShapeDtypeStruct((1, N_EXPERTS), jnp.int32)],
        compiler_params=pltpu.CompilerParams(dimension_semantics=("arbitrary",)),
    )(hf, rwh, rwl, router_bias.reshape(1, N_EXPERTS).astype(jnp.float32))


def _dispatch_kernel(idx_ref, rank_ref, pstart_ref, h_ref, xs_init, pos_ref, xs_hbm, pos_v, pos_s, ssem, psem):
    del xs_init
    erow = lax.broadcasted_iota(jnp.int32, (N_EXPERTS, TR), 0)
    pstart = pstart_ref[...]
    for k in range(TOP_K):
        hit = erow == idx_ref[0, k:k + 1, :]
        seg = jnp.sum(jnp.where(hit, pstart, 0), axis=0, keepdims=True)
        pos_v[k:k + 1, :] = seg + rank_ref[0, k:k + 1, :]
    pos_ref[0] = pos_v[...]
    cp = pltpu.make_async_copy(pos_v, pos_s, psem)
    cp.start()
    cp.wait()

    def body(t, carry):
        for k in range(TOP_K):
            pltpu.make_async_copy(h_ref.at[t], xs_hbm.at[pos_s[k, t]], ssem).start()
        return carry

    lax.fori_loop(0, TR, body, 0, unroll=8)
    n_rows = TR * TOP_K
    pltpu.make_async_copy(xs_hbm.at[pl.ds(0, n_rows)], xs_hbm.at[pl.ds(0, n_rows)], ssem).wait()


def _dispatch_call(hf, idx, rank, pad_start, n_pad):
    T, D = hf.shape
    n_tiles = T // TR
    return pl.pallas_call(
        _dispatch_kernel,
        grid=(n_tiles,),
        in_specs=[pl.BlockSpec((1, TOP_K, TR), lambda i: (i, 0, 0)),
                  pl.BlockSpec((1, TOP_K, TR), lambda i: (i, 0, 0)),
                  pl.BlockSpec((N_EXPERTS, 1), lambda i: (0, 0)),
                  pl.BlockSpec((TR, D), lambda i: (i, 0)),
                  pl.BlockSpec(memory_space=pl.ANY)],
        out_specs=[pl.BlockSpec((1, TOP_K, TR), lambda i: (i, 0, 0)),
                   pl.BlockSpec(memory_space=pl.ANY)],
        scratch_shapes=[pltpu.VMEM((TOP_K, TR), jnp.int32), pltpu.SMEM((TOP_K, TR), jnp.int32),
                        pltpu.SemaphoreType.DMA, pltpu.SemaphoreType.DMA],
        out_shape=[jax.ShapeDtypeStruct((n_tiles, TOP_K, TR), jnp.int32),
                   jax.ShapeDtypeStruct((n_pad, D), hf.dtype)],
        input_output_aliases={4: 1},
        compiler_params=pltpu.CompilerParams(dimension_semantics=("arbitrary",)),
    )(idx, rank, pad_start.reshape(N_EXPERTS, 1), hf, jnp.zeros((n_pad, D), hf.dtype))


BM = 256


def _expert_kernel(blk_e_ref, nvalid_ref, nused_ref, x_ref, wg_ref, wu_ref, wd_ref, y_ref, wg_s, wu_s, wd_s):
    i = pl.program_id(0)

    @pl.when(i < nused_ref[0])
    def _():
        e = blk_e_ref[i]
        prev = blk_e_ref[jnp.maximum(i - 1, 0)]

        @pl.when((i == 0) | (e != prev))
        def _():
            wg_s[...] = wg_ref[0].astype(jnp.bfloat16)
            wu_s[...] = wu_ref[0].astype(jnp.bfloat16)
            wd_s[...] = wd_ref[0].astype(jnp.bfloat16)

        row = lax.broadcasted_iota(jnp.int32, (BM, 1), 0)
        x = jnp.where(row < nvalid_ref[i], x_ref[...], 0.0).astype(jnp.bfloat16)
        g = jnp.dot(x, wg_s[...], preferred_element_type=jnp.float32)
        u = jnp.dot(x, wu_s[...], preferred_element_type=jnp.float32)
        a = (g * jax.nn.sigmoid(g)) * u
        y_ref[...] = jnp.dot(a.astype(jnp.bfloat16), wd_s[...], preferred_element_type=jnp.float32)

    @pl.when(i >= nused_ref[0])
    def _():
        y_ref[...] = jnp.zeros_like(y_ref)


def _expert_call(x_sorted, blk_e, n_valid, n_used, w_gate, w_up, w_down):
    n_pad, D = x_sorted.shape
    n_blk = n_pad // BM
    E, _, F = w_gate.shape

    def row_map(i, be, nv, nu):
        return (jnp.minimum(i, nu[0] - 1), 0)

    def w_map(i, be, nv, nu):
        return (be[jnp.minimum(i, nu[0] - 1)], 0, 0)

    return pl.pallas_call(
        _expert_kernel,
        grid_spec=pltpu.PrefetchScalarGridSpec(
            num_scalar_prefetch=3,
            grid=(n_blk,),
            in_specs=[pl.BlockSpec((BM, D), row_map),
                      pl.BlockSpec((1, D, F), w_map),
                      pl.BlockSpec((1, D, F), w_map),
                      pl.BlockSpec((1, F, D), w_map)],
            out_specs=pl.BlockSpec((BM, D), lambda i, be, nv, nu: (i, 0)),
            scratch_shapes=[pltpu.VMEM((D, F), jnp.bfloat16), pltpu.VMEM((D, F), jnp.bfloat16),
                            pltpu.VMEM((F, D), jnp.bfloat16)]),
        out_shape=jax.ShapeDtypeStruct((n_pad, D), jnp.float32),
        compiler_params=pltpu.CompilerParams(dimension_semantics=("arbitrary",),
                                             vmem_limit_bytes=48 * 1024 * 1024),
    )(blk_e, n_valid, n_used, x_sorted, w_gate, w_up, w_down)


TC = 64


def _combine_kernel(pos_hbm, y_hbm, w_ref, out_ref, ybuf, pos_s, gsem, psem):
    j = pl.program_id(0)
    last = pl.num_programs(0) - 1

    def pos_copy(b, slot):
        return pltpu.make_async_copy(pos_hbm.at[b], pos_s.at[slot], psem.at[slot])

    def start_gather(slot):
        for k in range(TOP_K):
            for t in range(TC):
                pltpu.make_async_copy(y_hbm.at[pos_s[slot, k * TC + t]], ybuf.at[slot, k, t],
                                      gsem.at[slot]).start()

    def wait_gather(slot):
        pltpu.make_async_copy(ybuf.at[slot], ybuf.at[slot], gsem.at[slot]).wait()

    @pl.when(j == 0)
    def _():
        pos_copy(0, 0).start()
        pos_copy(0, 0).wait()
        start_gather(0)
        pos_copy(jnp.minimum(1, last), 1).start()

    slot = j % 2
    nslot = 1 - slot
    pos_copy(0, nslot).wait()
    start_gather(nslot)
    pos_copy(jnp.minimum(j + 2, last), slot).start()
    wait_gather(slot)
    w = w_ref[...]
    acc = w[:, 0:1] * ybuf[slot, 0]
    for k in range(1, TOP_K):
        acc = acc + w[:, k:k + 1] * ybuf[slot, k]
    out_ref[...] = acc

    @pl.when(j == last)
    def _():
        wait_gather(nslot)
        pos_copy(0, slot).wait()


def _combine_call(y_sorted, pos_t, wts):
    T, K = wts.shape
    D = y_sorted.shape[1]
    n_tiles = T // TC
    return pl.pallas_call(
        _combine_kernel,
        grid=(n_tiles,),
        in_specs=[pl.BlockSpec(memory_space=pl.ANY),
                  pl.BlockSpec(memory_space=pl.ANY),
                  pl.BlockSpec((TC, K), lambda j: (j, 0))],
        out_specs=pl.BlockSpec((TC, D), lambda j: (j, 0)),
        scratch_shapes=[pltpu.VMEM((2, K, TC, D), jnp.float32),
                        pltpu.SMEM((2, K * TC), jnp.int32),
                        pltpu.SemaphoreType.DMA((2,)), pltpu.SemaphoreType.DMA((2,))],
        out_shape=jax.ShapeDtypeStruct((T, D), jnp.float32),
        compiler_params=pltpu.CompilerParams(dimension_semantics=("arbitrary",)),
    )(pos_t, y_sorted, wts)


def moe_ffn(hf, router_w, router_bias, w_gate, w_up, w_down, sh_gate, sh_up, sh_down):
    T, D = hf.shape
    idx, rank, wts, cnt = _route_call(hf, router_w, router_bias)
    counts = cnt[0]
    padded = (counts + BM - 1) // BM * BM
    pad_end = jnp.cumsum(padded)
    pad_start = (pad_end - padded).astype(jnp.int32)
    n_pad = T * TOP_K + N_EXPERTS * BM
    n_blk = n_pad // BM
    pos, x_sorted = _dispatch_call(hf, idx, rank, pad_start, n_pad)
    blk_row0 = jnp.arange(n_blk, dtype=jnp.int32) * BM
    blk_e = jnp.minimum(jnp.searchsorted(pad_end, blk_row0, side='right'), N_EXPERTS - 1).astype(jnp.int32)
    n_valid = jnp.clip(pad_start[blk_e] + counts[blk_e] - blk_row0, 0, BM).astype(jnp.int32)
    n_used = (pad_end[-1] // BM).astype(jnp.int32).reshape(1)
    y = _expert_call(x_sorted, blk_e, n_valid, n_used, w_gate, w_up, w_down)
    n_tiles = T // TR
    pos_t = pos.reshape(n_tiles, TOP_K, TR // TC, TC).transpose(0, 2, 1, 3).reshape(T // TC, TOP_K * TC)
    routed = _combine_call(y, pos_t, wts)
    shared = (jax.nn.silu(hf @ sh_gate) * (hf @ sh_up)) @ sh_down
    return routed + shared


def trunk_mix(x, mod, s0, grid, norm1_g, w_in, conv_w, a_log, dt_bias, onorm_g, pool_w, pool_scale,
              w_out, norm2_g):
    shift1, scale1, gate1, shift2, scale2, gate2 = jnp.split(mod, 6, axis=-1)
    h = rmsnorm(x, norm1_g) * (1 + scale1) + shift1
    proj = h @ w_in
    qkv, z, ba, u = jnp.split(proj, [3 * D_A, 4 * D_A, 4 * D_A + 4 * H_A], axis=-1)
    o_a, state = _delta_call(qkv, z, ba, conv_w, a_log, dt_bias, onorm_g, s0)
    o_p = pool_mixer(u, pool_w, pool_scale, grid)
    mix = jnp.concatenate([o_a.astype(x.dtype), o_p.astype(x.dtype)], axis=-1) @ w_out
    x = x + gate1 * mix
    h = rmsnorm(x, norm2_g) * (1 + scale2) + shift2
    return x, h, jnp.broadcast_to(gate2, x.shape), state


def _final_norm_kernel(x_ref, g_ref, o_ref):
    x = x_ref[...]
    y = x * lax.rsqrt(jnp.mean(x * x, axis=-1, keepdims=True) + EPS)
    o_ref[...] = y * g_ref[...]


def _final_norm(x, g):
    B, L, D = x.shape
    xf = x.reshape(B * L, D)
    tm = 512
    out = pl.pallas_call(
        _final_norm_kernel,
        grid=(B * L // tm,),
        in_specs=[pl.BlockSpec((tm, D), lambda i: (i, 0)), pl.BlockSpec((1, D), lambda i: (0, 0))],
        out_specs=pl.BlockSpec((tm, D), lambda i: (i, 0)),
        out_shape=jax.ShapeDtypeStruct((B * L, D), x.dtype),
    )(xf, g.reshape(1, D))
    return out.reshape(B, L, D)


def kernel(x_prompt, x_sample, state_delta, c, c_ctx, w_ada, b_ada, norm1_g, w_in, conv_w, a_log,
           dt_bias, onorm_g, pool_w, pool_scale, w_out, norm2_g, router_w, router_bias, exp_w_gate,
           exp_w_up, exp_w_down, sh_w_gate, sh_w_up, sh_w_down, final_g):
    xp = x_prompt
    xs = x_sample
    new_states = []
    for l in range(DEPTH):
        lw = (norm1_g[l], w_in[l], conv_w[l], a_log[l], dt_bias[l], onorm_g[l], pool_w[l], pool_scale[l],
              w_out[l], norm2_g[l])
        mod_ctx = (jax.nn.silu(c_ctx) @ w_ada[l] + b_ada[l])[None, None, :]
        mod_lat = (jax.nn.silu(c) @ w_ada[l] + b_ada[l])[:, None, :]
        xp, hp, gp, st_ctx = trunk_mix(xp, mod_ctx, None, False, *lw)
        xs, hs, gs, _ = trunk_mix(xs, mod_lat, state_delta[:, l], True, *lw)
        Tp = xp.shape[0] * xp.shape[1]
        hf = jnp.concatenate([hp.reshape(Tp, D_MODEL), hs.reshape(-1, D_MODEL)], axis=0)
        m = moe_ffn(hf, router_w[l], router_bias[l], exp_w_gate[l], exp_w_up[l], exp_w_down[l],
                    sh_w_gate[l], sh_w_up[l], sh_w_down[l])
        xp = xp + gp * m[:Tp].reshape(xp.shape)
        xs = xs + gs * m[Tp:].reshape(xs.shape)
        new_states.append(st_ctx.astype(x_prompt.dtype))
    y_prompt = _final_norm(xp, final_g)
    y_sample = _final_norm(xs, final_g)
    new_state_delta = jnp.stack(new_states, axis=1)
    return (y_prompt, y_sample, new_state_delta)
```

```python
import functools
import jax, jax.numpy as jnp
from jax import lax
from jax.experimental import pallas as pl
from jax.experimental.pallas import tpu as pltpu

D_MODEL = 1024
DEPTH = 1
GRID_W = 64
D_MIX = D_MODEL
D_A = D_MIX // 2
D_P = D_MIX - D_A
H_A = 4
DK = D_A // H_A
DV = D_A // H_A
CONV_K = 5
CHUNK = 64
POOL_WINDOWS = (2, 4, 8, 16)
N_PG = len(POOL_WINDOWS)
PG = D_P // N_PG
N_EXPERTS = 256
TOP_K = 8
N_GROUPS = 8
TOPK_GROUP = 4
ROUTED_SCALE = 2.5
EPS = 1e-6
VMEM_LIMIT = 48 * 1024 * 1024


def _split_bf16(a):
    hi = a.astype(jnp.bfloat16)
    return hi, (a - hi.astype(jnp.float32)).astype(jnp.bfloat16)


def _bdot(a, b):
    return jnp.dot(a, b, preferred_element_type=jnp.float32)


N_MOD = 6
MOD_ROWS = 8
TM = 512


def _ada_kernel(c_ref, w_ref, b_ref, o_ref):
    c = c_ref[...]
    s = c * jax.nn.sigmoid(c)
    sh, sl = _split_bf16(s)
    wh, wl = _split_bf16(w_ref[...])
    o_ref[...] = _bdot(sh, wh) + (_bdot(sh, wl) + _bdot(sl, wh)) + b_ref[...]


def _ada_call(cvec, w_ada, b_ada):
    R, D = cvec.shape
    N = w_ada.shape[1]
    tn = 1024
    return pl.pallas_call(
        _ada_kernel,
        grid=(N // tn,),
        in_specs=[pl.BlockSpec((R, D), lambda j: (0, 0)),
                  pl.BlockSpec((D, tn), lambda j: (0, j)),
                  pl.BlockSpec((1, tn), lambda j: (0, j))],
        out_specs=pl.BlockSpec((R, tn), lambda j: (0, j)),
        out_shape=jax.ShapeDtypeStruct((R, N), jnp.float32),
    )(cvec, w_ada, b_ada.reshape(1, N))


def _mod_row(tile, tokens_per_tile, n_ctx, lat_len):
    t0 = tile * tokens_per_tile
    return jnp.where(t0 < n_ctx, 0, 1 + (t0 - n_ctx) // lat_len)


def _inproj_kernel(x_ref, mod_ref, g_ref, wq_ref, wz_ref, wb_ref, wu_ref, q_ref, z_ref, b_ref, u_ref):
    x = x_ref[...]
    y = x * lax.rsqrt(jnp.mean(x * x, axis=-1, keepdims=True) + EPS) * g_ref[...]
    h = (y * (1.0 + mod_ref[0, 1:2, :]) + mod_ref[0, 0:1, :]).astype(jnp.bfloat16)
    q_ref[...] = _bdot(h, wq_ref[...])
    z_ref[...] = _bdot(h, wz_ref[...])
    b_ref[...] = _bdot(h, wb_ref[...])
    u_ref[...] = _bdot(h, wu_ref[...])


def _inproj_call(x, mod, norm1_g, w_in, n_ctx, lat_len):
    T, D = x.shape
    bf = jnp.bfloat16
    nq, nz, nb = 3 * D_A, D_A, 4 * H_A
    wq = w_in[:, :nq].astype(bf)
    wz = w_in[:, nq:nq + nz].astype(bf)
    wb = jnp.pad(w_in[:, nq + nz:nq + nz + nb], ((0, 0), (0, 128 - nb))).astype(bf)
    wu = w_in[:, nq + nz + nb:].astype(bf)
    row = functools.partial(_mod_row, tokens_per_tile=TM, n_ctx=n_ctx, lat_len=lat_len)

    def full(a):
        return pl.BlockSpec(a.shape, lambda i: (0, 0))

    def rows(n):
        return pl.BlockSpec((TM, n), lambda i: (i, 0))

    return pl.pallas_call(
        _inproj_kernel,
        grid=(T // TM,),
        in_specs=[rows(D), pl.BlockSpec((1, N_MOD, D), lambda i: (row(i), 0, 0)),
                  pl.BlockSpec((1, D), lambda i: (0, 0)), full(wq), full(wz), full(wb), full(wu)],
        out_specs=[rows(nq), rows(nz), rows(128), rows(D_P)],
        out_shape=[jax.ShapeDtypeStruct((T, nq), jnp.float32), jax.ShapeDtypeStruct((T, nz), jnp.float32),
                   jax.ShapeDtypeStruct((T, 128), jnp.float32), jax.ShapeDtypeStruct((T, D_P), jnp.float32)],
        compiler_params=pltpu.CompilerParams(dimension_semantics=("arbitrary",),
                                             vmem_limit_bytes=VMEM_LIMIT),
    )(x, mod, norm1_g.reshape(1, D), wq, wz, wb, wu)


PT = 256


def _window_bounds(pos, w, n):
    return jnp.maximum(pos - w // 2, 0), jnp.minimum(pos + w - w // 2, n)


def _band_sum(band, x):
    xh, xl = _split_bf16(x)
    return _bdot(band, xh) + _bdot(band, xl)


def _pool_seq_kernel(u_ref, pw_ref, ps_ref, o_ref):
    L = u_ref.shape[1]
    ti = lax.broadcasted_iota(jnp.int32, (L, L), 0)
    ji = lax.broadcasted_iota(jnp.int32, (L, L), 1)
    tcol = lax.broadcasted_iota(jnp.int32, (L, 1), 0)
    for i, w in enumerate(POOL_WINDOWS):
        lo, hi = _window_bounds(ti, w, L)
        band = ((ji >= lo) & (ji < hi)).astype(jnp.bfloat16)
        clo, chi = _window_bounds(tcol, w, L)
        ug = u_ref[0, :, i * PG:(i + 1) * PG]
        mean = _band_sum(band, ug) / (chi - clo).astype(jnp.float32)
        d = (mean - ug).astype(jnp.bfloat16)
        o_ref[0, :, i * PG:(i + 1) * PG] = _bdot(d, pw_ref[i]) * ps_ref[:, i * PG:(i + 1) * PG]


def _pool_grid_kernel(u_ref, pw_ref, ps_ref, o_ref, pad_s, r_s):
    L = u_ref.shape[1]
    rows = L // GRID_W
    halo = (max(POOL_WINDOWS) // 2) * GRID_W
    pad_s[0:halo, :] = jnp.zeros((halo, D_P), jnp.float32)
    pad_s[halo + L:, :] = jnp.zeros((halo, D_P), jnp.float32)
    pad_s[halo:halo + L, :] = u_ref[0]
    ti = lax.broadcasted_iota(jnp.int32, (PT, PT), 0)
    ji = lax.broadcasted_iota(jnp.int32, (PT, PT), 1)
    tcol = lax.broadcasted_iota(jnp.int32, (PT, 1), 0)
    for i, w in enumerate(POOL_WINDOWS):
        cs = slice(i * PG, (i + 1) * PG)
        acc = None
        for dr in range(-(w // 2), w - w // 2):
            part = pad_s[halo + dr * GRID_W:halo + dr * GRID_W + L, cs]
            acc = part if acc is None else acc + part
        r_s[...] = acc
        lo, hi = _window_bounds(ti % GRID_W, w, GRID_W)
        band = ((ji // GRID_W == ti // GRID_W) & (ji % GRID_W >= lo) & (ji % GRID_W < hi)).astype(jnp.bfloat16)
        clo, chi = _window_bounds(tcol % GRID_W, w, GRID_W)
        ccnt = (chi - clo).astype(jnp.float32)
        for tile in range(L // PT):
            ts = slice(tile * PT, (tile + 1) * PT)
            rlo, rhi = _window_bounds(tile * (PT // GRID_W) + tcol // GRID_W, w, rows)
            mean = _band_sum(band, r_s[ts, :]) / ((rhi - rlo).astype(jnp.float32) * ccnt)
            d = (mean - u_ref[0, ts, cs]).astype(jnp.bfloat16)
            o_ref[0, ts, cs] = _bdot(d, pw_ref[i]) * ps_ref[:, cs]


def _pool_call(u, pool_w, pool_scale, grid):
    B, L, _ = u.shape
    pw = pool_w.astype(jnp.bfloat16)
    ps = pool_scale.reshape(1, D_P)
    specs = dict(
        grid=(B,),
        in_specs=[pl.BlockSpec((1, L, D_P), lambda b: (b, 0, 0)),
                  pl.BlockSpec((N_PG, PG, PG), lambda b: (0, 0, 0)),
                  pl.BlockSpec((1, D_P), lambda b: (0, 0))],
        out_specs=pl.BlockSpec((1, L, D_P), lambda b: (b, 0, 0)),
        out_shape=jax.ShapeDtypeStruct((B, L, D_P), jnp.float32),
        compiler_params=pltpu.CompilerParams(dimension_semantics=("arbitrary",),
                                             vmem_limit_bytes=VMEM_LIMIT))
    if not grid:
        return pl.pallas_call(_pool_seq_kernel, **specs)(u, pw, ps)
    halo = (max(POOL_WINDOWS) // 2) * GRID_W
    return pl.pallas_call(
        _pool_grid_kernel,
        scratch_shapes=[pltpu.VMEM((L + 2 * halo, D_P), jnp.float32), pltpu.VMEM((L, PG), jnp.float32)],
        **specs)(u, pw, ps)


def _outproj_kernel(x_ref, oa_ref, op_ref, mod_ref, g2_ref, wo_ref, sg_ref, su_ref, sd_ref,
                    x1_ref, h2_ref, sh_ref):
    mix = (_bdot(oa_ref[...].astype(jnp.bfloat16), wo_ref[:D_A, :])
           + _bdot(op_ref[...].astype(jnp.bfloat16), wo_ref[D_A:, :]))
    x1 = x_ref[...] + mod_ref[0, 2:3, :] * mix
    x1_ref[...] = x1
    y = x1 * lax.rsqrt(jnp.mean(x1 * x1, axis=-1, keepdims=True) + EPS) * g2_ref[...]
    h2 = y * (1.0 + mod_ref[0, 4:5, :]) + mod_ref[0, 3:4, :]
    h2_ref[...] = h2
    hb = h2.astype(jnp.bfloat16)
    g = _bdot(hb, sg_ref[...])
    a = (g * jax.nn.sigmoid(g)) * _bdot(hb, su_ref[...])
    sh_ref[...] = _bdot(a.astype(jnp.bfloat16), sd_ref[...])


def _outproj_call(x, o_a, o_p, mod, norm2_g, w_out, sh_gate, sh_up, sh_down, n_ctx, lat_len):
    T, D = x.shape
    bf = jnp.bfloat16
    row = functools.partial(_mod_row, tokens_per_tile=TM, n_ctx=n_ctx, lat_len=lat_len)
    ws = [w_out.astype(bf), sh_gate.astype(bf), sh_up.astype(bf), sh_down.astype(bf)]

    def rows(n):
        return pl.BlockSpec((TM, n), lambda i: (i, 0))

    return pl.pallas_call(
        _outproj_kernel,
        grid=(T // TM,),
        in_specs=[rows(D), rows(D_A), rows(D_P), pl.BlockSpec((1, N_MOD, D), lambda i: (row(i), 0, 0)),
                  pl.BlockSpec((1, D), lambda i: (0, 0))] + [pl.BlockSpec(w.shape, lambda i: (0, 0)) for w in ws],
        out_specs=[rows(D), rows(D), rows(D)],
        out_shape=[jax.ShapeDtypeStruct((T, D), jnp.float32)] * 3,
        compiler_params=pltpu.CompilerParams(dimension_semantics=("arbitrary",),
                                             vmem_limit_bytes=VMEM_LIMIT),
    )(x, o_a, o_p, mod, norm2_g.reshape(1, D), *ws)


SC = 256
CPS = SC // CHUNK
BASE = 16


def _mm(a, b):
    return jnp.dot(a.astype(jnp.bfloat16), b.astype(jnp.bfloat16), preferred_element_type=jnp.float32)


def _mm_nt(a, b):
    return lax.dot_general(a.astype(jnp.bfloat16), b.astype(jnp.bfloat16), (((1,), (1,)), ((), ())),
                           preferred_element_type=jnp.float32)


def _softplus(x):
    return jnp.maximum(x, 0.0) + jnp.log(1.0 + jnp.exp(-jnp.abs(x)))


def _delta_kernel(sc_ref, xq_ref, xk_ref, xv_ref, z_ref, bac_ref, bar_ref, cwq_ref, cwk_ref, cwv_ref,
                  og_ref, s0_ref, o_ref, st_ref, q_s, k_s, v_s, o_s, vn_s, *, n_sc, zero_init):
    h = pl.program_id(1)
    L = q_s.shape[0]

    def conv(x_ref, w_ref):
        x = x_ref[...]
        row = lax.broadcasted_iota(jnp.int32, x.shape, 0)
        acc = x * w_ref[CONV_K // 2:CONV_K // 2 + 1, :]
        for j in range(CONV_K):
            d = j - CONV_K // 2
            if d == 0:
                continue
            xs = pltpu.roll(x, (-d) % L, 0)
            ok = (row + d >= 0) & (row + d < L)
            acc = acc + jnp.where(ok, xs, 0.0) * w_ref[j:j + 1, :]
        return acc * jax.nn.sigmoid(acc)

    q = conv(xq_ref, cwq_ref)
    q_s[...] = q * lax.rsqrt(jnp.sum(q * q, axis=-1, keepdims=True) + EPS) * (DK ** -0.5)
    k = conv(xk_ref, cwk_ref)
    k_s[...] = k * lax.rsqrt(jnp.sum(k * k, axis=-1, keepdims=True) + EPS)
    v_s[...] = conv(xv_ref, cwv_ref)
    o_s[...] = jnp.zeros_like(o_s)

    ri = lax.broadcasted_iota(jnp.int32, (SC, SC), 0)
    ci = lax.broadcasted_iota(jnp.int32, (SC, SC), 1)
    same = (ri // CHUNK) == (ci // CHUNK)
    same_base = (ri // BASE) == (ci // BASE)
    merge_masks = [(ri // w) == (ci // w) for w in (2 * BASE, CHUNK)]
    eye = (ri == ci).astype(jnp.float32)
    rowi = lax.broadcasted_iota(jnp.int32, (SC, DV), 0)

    def unit(m, d, s):
        r0 = pl.multiple_of(m * SC, SC)
        q = q_s[pl.ds(r0, SC), :]
        k = k_s[pl.ds(r0, SC), :]
        v = v_s[pl.ds(r0, SC), :]
        bc = bac_ref[0, 0, pl.ds(r0, SC), :]
        br = bar_ref[0, 0, m]
        a_l = sc_ref[d * H_A + h]
        dtb = sc_ref[2 * H_A + d * H_A + h]
        neg_ea = -jnp.exp(jnp.full((1, 1), a_l, jnp.float32))
        beta = jax.nn.sigmoid(bc[:, d:d + 1])
        g_col = neg_ea * _softplus(bc[:, 2 + d:3 + d] + dtb)
        g_row = neg_ea * _softplus(br[2 + d:3 + d, :] + dtb)
        if d == 0:
            tri, strict = same & (ci <= ri), same & (ci < ri)
        else:
            tri, strict = same & (ci >= ri), same & (ci > ri)
        tri_t = same & (ri <= ci) if d == 0 else same & (ri >= ci)
        gc_col = jnp.sum(jnp.where(tri, g_row, 0.0), axis=1, keepdims=True)
        gc_row = jnp.sum(jnp.where(tri_t, g_col, 0.0), axis=0, keepdims=True)
        gl_col = jnp.sum(jnp.where(same, g_row, 0.0), axis=1, keepdims=True)
        decay = jnp.where(tri, jnp.exp(jnp.where(tri, gc_col - gc_row, 0.0)), 0.0)
        kb = k * beta
        a = jnp.where(strict, _mm_nt(kb, k) * decay, 0.0)
        attn = jnp.where(tri, _mm_nt(q, k) * decay, 0.0)
        eg = jnp.exp(gc_col)
        x = jnp.concatenate([v * beta, kb * eg], axis=1)
        a0 = jnp.where(same_base, a, 0.0)
        t = eye - a0
        p = a0
        for _ in range(BASE.bit_length() - 2):
            p = _mm(p, p)
            t = t + _mm(t, p)
        inner = same_base
        for outer in merge_masks:
            off = jnp.where(outer & ~inner, a, 0.0)
            t = t - _mm(t, _mm(off, t))
            inner = outer
        x = _mm(t, x)
        u = x[:, :DV]
        w = x[:, DV:]
        qd = q * eg
        kdt = (k * jnp.exp(gl_col - gc_col)).T
        egl = jnp.exp(gl_col)
        vn_s[d] = jnp.zeros((SC, DV), jnp.float32)
        order = range(CPS) if d == 0 else range(CPS - 1, -1, -1)
        for c in order:
            lo, hi = c * CHUNK, (c + 1) * CHUNK
            ws_qs = _mm(jnp.concatenate([w[lo:hi], qd[lo:hi]], axis=0), s)
            v_new = u[lo:hi] - ws_qs[:CHUNK]
            vn_s[d, lo:hi, :] = v_new
            vn = vn_s[d]
            o_c = ws_qs[CHUNK:] + _mm(attn[lo:hi, :], vn)
            o_s[pl.ds(r0 + lo, CHUNK), :] += o_c
            v_only = jnp.where((rowi >= lo) & (rowi < hi), vn, 0.0)
            s = s * egl[lo:lo + 1, :] + _mm(kdt, v_only)
        return s

    if zero_init:
        s_f = jnp.zeros((DK, DV), jnp.float32)
        s_b = jnp.zeros((DK, DV), jnp.float32)
    else:
        s_f = s0_ref[0, 0, 0]
        s_b = s0_ref[0, 1, 0]

    def body(m, carry):
        s_f, s_b = carry
        return unit(m, 0, s_f), unit(n_sc - 1 - m, 1, s_b)

    if n_sc == 1:
        s_f, s_b = body(0, (s_f, s_b))
    else:
        s_f, s_b = lax.fori_loop(0, n_sc, body, (s_f, s_b))

    st_ref[0, 0, 0] = s_f
    st_ref[0, 1, 0] = s_b
    o = o_s[...]
    o = o * lax.rsqrt(jnp.mean(o * o, axis=-1, keepdims=True) + EPS) * og_ref[...]
    zz = z_ref[...]
    o_ref[...] = o * (zz * jax.nn.sigmoid(zz))


def _delta_call(qkv, z, ba, conv_w, a_log, dt_bias, onorm_g, s0, B, L, row_blk0):
    n_sc = L // SC
    t0 = row_blk0 * L
    bah = ba[t0:t0 + B * L, :4 * H_A].reshape(B, L, 4, H_A).transpose(0, 3, 1, 2)
    bar = bah.reshape(B, H_A, n_sc, SC, 4).transpose(0, 1, 2, 4, 3)
    scal = jnp.concatenate([a_log.reshape(-1), dt_bias.reshape(-1)]).astype(jnp.float32)
    zero_init = s0 is None
    if zero_init:
        s0 = jnp.zeros((1, 2, 1, DK, DV), jnp.float32)
        s0_spec = pl.BlockSpec((1, 2, 1, DK, DV), lambda b, h, sc: (0, 0, 0, 0, 0))
    else:
        s0_spec = pl.BlockSpec((1, 2, 1, DK, DV), lambda b, h, sc: (b, 0, h, 0, 0))

    def col(off):
        return pl.BlockSpec((L, DK), lambda b, h, sc: (row_blk0 + b, off + h))

    def cw(off):
        return pl.BlockSpec((CONV_K, DK), lambda b, h, sc: (0, off + h))

    kern = functools.partial(_delta_kernel, n_sc=n_sc, zero_init=zero_init)
    return pl.pallas_call(
        kern,
        grid_spec=pltpu.PrefetchScalarGridSpec(
            num_scalar_prefetch=1,
            grid=(B, H_A),
            in_specs=[col(0), col(H_A), col(2 * H_A),
                      pl.BlockSpec((L, DV), lambda b, h, sc: (row_blk0 + b, h)),
                      pl.BlockSpec((1, 1, L, 4), lambda b, h, sc: (b, h, 0, 0)),
                      pl.BlockSpec((1, 1, n_sc, 4, SC), lambda b, h, sc: (b, h, 0, 0, 0)),
                      cw(0), cw(H_A), cw(2 * H_A),
                      pl.BlockSpec((1, DV), lambda b, h, sc: (0, 0)),
                      s0_spec],
            out_specs=[pl.BlockSpec((L, DV), lambda b, h, sc: (b, h)),
                       pl.BlockSpec((1, 2, 1, DK, DV), lambda b, h, sc: (b, 0, h, 0, 0))],
            scratch_shapes=[pltpu.VMEM((L, DK), jnp.float32), pltpu.VMEM((L, DK), jnp.float32),
                            pltpu.VMEM((L, DV), jnp.float32), pltpu.VMEM((L, DV), jnp.float32),
                            pltpu.VMEM((2, SC, DV), jnp.float32)]),
        out_shape=[jax.ShapeDtypeStruct((B * L, D_A), jnp.float32),
                   jax.ShapeDtypeStruct((B, 2, H_A, DK, DV), jnp.float32)],
        compiler_params=pltpu.CompilerParams(dimension_semantics=("arbitrary", "arbitrary"),
                                             vmem_limit_bytes=VMEM_LIMIT),
    )(scal, qkv, qkv, qkv, z, bah, bar, conv_w, conv_w, conv_w, onorm_g.reshape(1, DV), s0)


TR = 256
GSZ = N_EXPERTS // N_GROUPS
NEG = -jnp.inf


def _col_to_row(col, eye_mask):
    return jnp.sum(jnp.where(eye_mask, col, jnp.zeros_like(col)), axis=0, keepdims=True)


def _route_kernel(h_ref, rwh_ref, rwl_ref, rb_ref, idx_ref, rank_ref, w_ref, cnt_ref, cnt_s):
    i = pl.program_id(0)

    @pl.when(i == 0)
    def _():
        cnt_s[...] = jnp.zeros_like(cnt_s)

    h = h_ref[...]
    hh, hl = _split_bf16(h)
    logits = _bdot(hh, rwh_ref[...]) + (_bdot(hh, rwl_ref[...]) + _bdot(hl, rwh_ref[...]))
    scores = jax.nn.sigmoid(logits)
    sel = scores + rb_ref[...]
    lane = lax.broadcasted_iota(jnp.int32, sel.shape, 1)
    gid = lane // GSZ

    def first_argmax(v):
        m = jnp.max(v, axis=1, keepdims=True)
        first = jnp.min(jnp.where(v == m, lane, N_EXPERTS), axis=1, keepdims=True)
        return m, first

    gscore = []
    for g in range(N_GROUPS):
        vg = jnp.where(gid == g, sel, NEG)
        m1, i1 = first_argmax(vg)
        m2 = jnp.max(jnp.where(lane == i1, NEG, vg), axis=1, keepdims=True)
        gscore.append(m1 + m2)
    emask = jnp.zeros(sel.shape, jnp.bool_)
    for g in range(N_GROUPS):
        beat = jnp.zeros(gscore[g].shape, jnp.int32)
        for o in range(N_GROUPS):
            if o == g:
                continue
            wins = (gscore[o] > gscore[g]) | ((gscore[o] == gscore[g]) & (o < g))
            beat = beat + wins.astype(jnp.int32)
        emask = emask | ((gid == g) & (beat < TOPK_GROUP))
    cand = jnp.where(emask, sel, NEG)
    chosen = []
    picked = jnp.zeros(sel.shape, jnp.bool_)
    for _ in range(TOP_K):
        _, ik = first_argmax(cand)
        hit = lane == ik
        chosen.append((ik, hit))
        picked = picked | hit
        cand = jnp.where(hit, NEG, cand)
    wraw = jnp.where(picked, scores, 0.0)
    wmat = wraw / jnp.sum(wraw, axis=1, keepdims=True) * ROUTED_SCALE

    pm = picked.astype(jnp.bfloat16)
    ri = lax.broadcasted_iota(jnp.int32, (TR, TR), 0)
    ci = lax.broadcasted_iota(jnp.int32, (TR, TR), 1)
    earlier = (ci < ri).astype(jnp.bfloat16)
    rank_mat = _bdot(earlier, pm) + cnt_s[...]
    cnt_s[...] = cnt_s[...] + jnp.sum(picked.astype(jnp.float32), axis=0, keepdims=True)
    cnt_ref[...] = cnt_s[...].astype(jnp.int32)

    eye = ri == ci
    lane8 = lax.broadcasted_iota(jnp.int32, (TR, TOP_K), 1)
    wcols = jnp.zeros((TR, TOP_K), jnp.float32)
    for k, (ik, hit) in enumerate(chosen):
        rk = jnp.sum(jnp.where(hit, rank_mat, 0.0), axis=1, keepdims=True)
        wk = jnp.sum(jnp.where(hit, wmat, 0.0), axis=1, keepdims=True)
        idx_ref[0, k:k + 1, :] = _col_to_row(ik, eye)
        rank_ref[0, k:k + 1, :] = _col_to_row(rk, eye).astype(jnp.int32)
        wcols = jnp.where(lane8 == k, wk, wcols)
    w_ref[...] = wcols


def _route_call(hf, router_w, router_bias):
    T, D = hf.shape
    n_tiles = T // TR
    rwh, rwl = _split_bf16(router_w)
    return pl.pallas_call(
        _route_kernel,
        grid=(n_tiles,),
        in_specs=[pl.BlockSpec((TR, D), lambda i: (i, 0)),
                  pl.BlockSpec((D, N_EXPERTS), lambda i: (0, 0)),
                  pl.BlockSpec((D, N_EXPERTS), lambda i: (0, 0)),
                  pl.BlockSpec((1, N_EXPERTS), lambda i: (0, 0))],
        out_specs=[pl.BlockSpec((1, TOP_K, TR), lambda i: (i, 0, 0)),
                   pl.BlockSpec((1, TOP_K, TR), lambda i: (i, 0, 0)),
                   pl.BlockSpec((TR, TOP_K), lambda i: (i, 0)),
                   pl.BlockSpec((1, N_EXPERTS), lambda i: (0, 0))],
        scratch_shapes=[pltpu.VMEM((1, N_EXPERTS), jnp.float32)],
        out_shape=[jax.ShapeDtypeStruct((n_tiles, TOP_K, TR), jnp.int32),
                   jax.ShapeDtypeStruct((n_tiles, TOP_K, TR), jnp.int32),
                   jax.ShapeDtypeStruct((T, TOP_K), jnp.float32),
                   jax.ShapeDtypeStruct((1, N_EXPERTS), jnp.int32)],
        compiler_params=pltpu.CompilerParams(dimension_semantics=("arbitrary",)),
    )(hf, rwh, rwl, router_bias.reshape(1, N_EXPERTS).astype(jnp.float32))


def _dispatch_kernel(idx_ref, rank_ref, pstart_ref, h_ref, xs_init, pos_ref, xs_hbm, pos_v, pos_s, ssem, psem):
    del xs_init
    erow = lax.broadcasted_iota(jnp.int32, (N_EXPERTS, TR), 0)
    pstart = pstart_ref[...]
    for k in range(TOP_K):
        hit = erow == idx_ref[0, k:k + 1, :]
        seg = jnp.sum(jnp.where(hit, pstart, 0), axis=0, keepdims=True)
        pos_v[k:k + 1, :] = seg + rank_ref[0, k:k + 1, :]
    pos_ref[0] = pos_v[...]
    cp = pltpu.make_async_copy(pos_v, pos_s, psem)
    cp.start()
    cp.wait()

    def body(t, carry):
        for k in range(TOP_K):
            pltpu.make_async_copy(h_ref.at[t], xs_hbm.at[pos_s[k, t]], ssem).start()
        return carry

    lax.fori_loop(0, TR, body, 0, unroll=8)
    n_rows = TR * TOP_K
    pltpu.make_async_copy(xs_hbm.at[pl.ds(0, n_rows)], xs_hbm.at[pl.ds(0, n_rows)], ssem).wait()


def _dispatch_call(hf, idx, rank, pad_start, n_pad):
    T, D = hf.shape
    n_tiles = T // TR
    return pl.pallas_call(
        _dispatch_kernel,
        grid=(n_tiles,),
        in_specs=[pl.BlockSpec((1, TOP_K, TR), lambda i: (i, 0, 0)),
                  pl.BlockSpec((1, TOP_K, TR), lambda i: (i, 0, 0)),
                  pl.BlockSpec((N_EXPERTS, 1), lambda i: (0, 0)),
                  pl.BlockSpec((TR, D), lambda i: (i, 0)),
                  pl.BlockSpec(memory_space=pl.ANY)],
        out_specs=[pl.BlockSpec((1, TOP_K, TR), lambda i: (i, 0, 0)),
                   pl.BlockSpec(memory_space=pl.ANY)],
        scratch_shapes=[pltpu.VMEM((TOP_K, TR), jnp.int32), pltpu.SMEM((TOP_K, TR), jnp.int32),
                        pltpu.SemaphoreType.DMA, pltpu.SemaphoreType.DMA],
        out_shape=[jax.ShapeDtypeStruct((n_tiles, TOP_K, TR), jnp.int32),
                   jax.ShapeDtypeStruct((n_pad, D), hf.dtype)],
        input_output_aliases={4: 1},
        compiler_params=pltpu.CompilerParams(dimension_semantics=("arbitrary",)),
    )(idx, rank, pad_start.reshape(N_EXPERTS, 1), hf, jnp.zeros((n_pad, D), hf.dtype))


BM = 256


def _expert_kernel(blk_e_ref, nvalid_ref, nused_ref, x_ref, wg_ref, wu_ref, wd_ref, y_ref, wg_s, wu_s, wd_s):
    i = pl.program_id(0)

    @pl.when(i < nused_ref[0])
    def _():
        e = blk_e_ref[i]
        prev = blk_e_ref[jnp.maximum(i - 1, 0)]

        @pl.when((i == 0) | (e != prev))
        def _():
            wg_s[...] = wg_ref[0].astype(jnp.bfloat16)
            wu_s[...] = wu_ref[0].astype(jnp.bfloat16)
            wd_s[...] = wd_ref[0].astype(jnp.bfloat16)

        row = lax.broadcasted_iota(jnp.int32, (BM, 1), 0)
        x = jnp.where(row < nvalid_ref[i], x_ref[...], 0.0).astype(jnp.bfloat16)
        g = _bdot(x, wg_s[...])
        u = _bdot(x, wu_s[...])
        a = (g * jax.nn.sigmoid(g)) * u
        y_ref[...] = _bdot(a.astype(jnp.bfloat16), wd_s[...])

    @pl.when(i >= nused_ref[0])
    def _():
        y_ref[...] = jnp.zeros_like(y_ref)


def _expert_call(x_sorted, blk_e, n_valid, n_used, w_gate, w_up, w_down):
    n_pad, D = x_sorted.shape
    n_blk = n_pad // BM
    E, _, F = w_gate.shape

    def row_map(i, be, nv, nu):
        return (jnp.minimum(i, nu[0] - 1), 0)

    def w_map(i, be, nv, nu):
        return (be[jnp.minimum(i, nu[0] - 1)], 0, 0)

    return pl.pallas_call(
        _expert_kernel,
        grid_spec=pltpu.PrefetchScalarGridSpec(
            num_scalar_prefetch=3,
            grid=(n_blk,),
            in_specs=[pl.BlockSpec((BM, D), row_map),
                      pl.BlockSpec((1, D, F), w_map),
                      pl.BlockSpec((1, D, F), w_map),
                      pl.BlockSpec((1, F, D), w_map)],
            out_specs=pl.BlockSpec((BM, D), lambda i, be, nv, nu: (i, 0)),
            scratch_shapes=[pltpu.VMEM((D, F), jnp.bfloat16), pltpu.VMEM((D, F), jnp.bfloat16),
                            pltpu.VMEM((F, D), jnp.bfloat16)]),
        out_shape=jax.ShapeDtypeStruct((n_pad, D), jnp.float32),
        compiler_params=pltpu.CompilerParams(dimension_semantics=("arbitrary",),
                                             vmem_limit_bytes=VMEM_LIMIT),
    )(blk_e, n_valid, n_used, x_sorted, w_gate, w_up, w_down)


TC = 64


def _combine_kernel(pos_hbm, y_hbm, w_ref, x1_ref, sh_ref, mod_ref, fg_ref, out_ref, ybuf, pos_s, gsem, psem):
    j = pl.program_id(0)
    last = pl.num_programs(0) - 1

    def pos_copy(b, slot):
        return pltpu.make_async_copy(pos_hbm.at[b], pos_s.at[slot], psem.at[slot])

    def start_gather(slot):
        for k in range(TOP_K):
            for t in range(TC):
                pltpu.make_async_copy(y_hbm.at[pos_s[slot, k * TC + t]], ybuf.at[slot, k, t],
                                      gsem.at[slot]).start()

    def wait_gather(slot):
        pltpu.make_async_copy(ybuf.at[slot], ybuf.at[slot], gsem.at[slot]).wait()

    @pl.when(j == 0)
    def _():
        pos_copy(0, 0).start()
        pos_copy(0, 0).wait()
        start_gather(0)
        pos_copy(jnp.minimum(1, last), 1).start()

    slot = j % 2
    nslot = 1 - slot
    pos_copy(0, nslot).wait()
    start_gather(nslot)
    pos_copy(jnp.minimum(j + 2, last), slot).start()
    wait_gather(slot)
    w = w_ref[...]
    acc = w[:, 0:1] * ybuf[slot, 0]
    for k in range(1, TOP_K):
        acc = acc + w[:, k:k + 1] * ybuf[slot, k]
    x2 = x1_ref[...] + mod_ref[0, 5:6, :] * (acc + sh_ref[...])
    out_ref[...] = x2 * lax.rsqrt(jnp.mean(x2 * x2, axis=-1, keepdims=True) + EPS) * fg_ref[...]

    @pl.when(j == last)
    def _():
        wait_gather(nslot)
        pos_copy(0, slot).wait()


def _combine_call(y_sorted, pos_t, wts, x1, shared, mod, final_g, n_ctx, lat_len):
    T, K = wts.shape
    D = y_sorted.shape[1]
    n_tiles = T // TC
    row = functools.partial(_mod_row, tokens_per_tile=TC, n_ctx=n_ctx, lat_len=lat_len)
    return pl.pallas_call(
        _combine_kernel,
        grid=(n_tiles,),
        in_specs=[pl.BlockSpec(memory_space=pl.ANY),
                  pl.BlockSpec(memory_space=pl.ANY),
                  pl.BlockSpec((TC, K), lambda j: (j, 0)),
                  pl.BlockSpec((TC, D), lambda j: (j, 0)),
                  pl.BlockSpec((TC, D), lambda j: (j, 0)),
                  pl.BlockSpec((1, N_MOD, D), lambda j: (row(j), 0, 0)),
                  pl.BlockSpec((1, D), lambda j: (0, 0))],
        out_specs=pl.BlockSpec((TC, D), lambda j: (j, 0)),
        scratch_shapes=[pltpu.VMEM((2, K, TC, D), jnp.float32),
                        pltpu.SMEM((2, K * TC), jnp.int32),
                        pltpu.SemaphoreType.DMA((2,)), pltpu.SemaphoreType.DMA((2,))],
        out_shape=jax.ShapeDtypeStruct((T, D), jnp.float32),
        compiler_params=pltpu.CompilerParams(dimension_semantics=("arbitrary",)),
    )(pos_t, y_sorted, wts, x1, shared, mod, final_g.reshape(1, D))


def _moe_routed(h2, router_w, router_bias, w_gate, w_up, w_down):
    T, D = h2.shape
    idx, rank, wts, cnt = _route_call(h2, router_w, router_bias)
    counts = cnt[0]
    padded = (counts + BM - 1) // BM * BM
    pad_end = jnp.cumsum(padded)
    pad_start = (pad_end - padded).astype(jnp.int32)
    n_pad = T * TOP_K + N_EXPERTS * BM
    n_blk = n_pad // BM
    pos, x_sorted = _dispatch_call(h2, idx, rank, pad_start, n_pad)
    blk_row0 = jnp.arange(n_blk, dtype=jnp.int32) * BM
    blk_e = jnp.minimum(jnp.searchsorted(pad_end, blk_row0, side='right'), N_EXPERTS - 1).astype(jnp.int32)
    n_valid = jnp.clip(pad_start[blk_e] + counts[blk_e] - blk_row0, 0, BM).astype(jnp.int32)
    n_used = (pad_end[-1] // BM).astype(jnp.int32).reshape(1)
    y = _expert_call(x_sorted, blk_e, n_valid, n_used, w_gate, w_up, w_down)
    pos_t = pos.reshape(T // TR, TOP_K, TR // TC, TC).transpose(0, 2, 1, 3).reshape(T // TC, TOP_K * TC)
    return y, pos_t, wts


def kernel(x_prompt, x_sample, state_delta, c, c_ctx, w_ada, b_ada, norm1_g, w_in, conv_w, a_log,
           dt_bias, onorm_g, pool_w, pool_scale, w_out, norm2_g, router_w, router_bias, exp_w_gate,
           exp_w_up, exp_w_down, sh_w_gate, sh_w_up, sh_w_down, final_g):
    Bc, Lc, D = x_prompt.shape
    Bl, Ll, _ = x_sample.shape
    n_ctx = Bc * Lc
    assert DEPTH == 1 and 1 + Bl <= MOD_ROWS and n_ctx % Ll == 0
    x = jnp.concatenate([x_prompt.reshape(n_ctx, D), x_sample.reshape(Bl * Ll, D)], axis=0)
    cvec = jnp.concatenate([c_ctx[None], c, jnp.zeros((MOD_ROWS - 1 - Bl, D), c.dtype)], axis=0)
    l = 0
    mod = _ada_call(cvec, w_ada[l], b_ada[l]).reshape(MOD_ROWS, N_MOD, D)
    qkv, z, ba, u = _inproj_call(x, mod, norm1_g[l], w_in[l], n_ctx, Ll)
    dn = (conv_w[l], a_log[l], dt_bias[l], onorm_g[l])
    oa_c, st_ctx = _delta_call(qkv, z, ba, *dn, None, Bc, Lc, 0)
    oa_l, _ = _delta_call(qkv, z, ba, *dn, state_delta[:, l], Bl, Ll, n_ctx // Ll)
    op_c = _pool_call(u[:n_ctx].reshape(Bc, Lc, D_P), pool_w[l], pool_scale[l], False)
    op_l = _pool_call(u[n_ctx:].reshape(Bl, Ll, D_P), pool_w[l], pool_scale[l], True)
    o_a = jnp.concatenate([oa_c, oa_l], axis=0)
    o_p = jnp.concatenate([op_c.reshape(n_ctx, D_P), op_l.reshape(Bl * Ll, D_P)], axis=0)
    x1, h2, shared = _outproj_call(x, o_a, o_p, mod, norm2_g[l], w_out[l], sh_w_gate[l], sh_w_up[l],
                                   sh_w_down[l], n_ctx, Ll)
    y, pos_t, wts = _moe_routed(h2, router_w[l], router_bias[l], exp_w_gate[l], exp_w_up[l], exp_w_down[l])
    out = _combine_call(y, pos_t, wts, x1, shared, mod, final_g, n_ctx, Ll)
    y_prompt = out[:n_ctx].reshape(Bc, Lc, D)
    y_sample = out[n_ctx:].reshape(Bl, Ll, D)
    new_state_delta = st_ctx[:, None].astype(x_prompt.dtype)
    return (y_prompt, y_sample, new_state_delta)
```

```python
import functools
import jax, jax.numpy as jnp
from jax import lax
from jax.experimental import pallas as pl
from jax.experimental.pallas import tpu as pltpu

D_MODEL = 1024
DEPTH = 1
GRID_W = 64
D_MIX = D_MODEL
D_A = D_MIX // 2
D_P = D_MIX - D_A
H_A = 4
DK = D_A // H_A
DV = D_A // H_A
CONV_K = 5
CHUNK = 64
POOL_WINDOWS = (2, 4, 8, 16)
N_PG = len(POOL_WINDOWS)
PG = D_P // N_PG
N_EXPERTS = 256
TOP_K = 8
N_GROUPS = 8
TOPK_GROUP = 4
ROUTED_SCALE = 2.5
EPS = 1e-6
VMEM_LIMIT = 48 * 1024 * 1024


def _split_bf16(a):
    hi = a.astype(jnp.bfloat16)
    return hi, (a - hi.astype(jnp.float32)).astype(jnp.bfloat16)


def _bdot(a, b):
    return jnp.dot(a, b, preferred_element_type=jnp.float32)


N_MOD = 6
MOD_ROWS = 8
TM = 512


def _ada_kernel(c_ref, w_ref, b_ref, o_ref):
    c = c_ref[...]
    s = c * jax.nn.sigmoid(c)
    sh, sl = _split_bf16(s)
    wh, wl = _split_bf16(w_ref[...])
    o_ref[...] = _bdot(sh, wh) + (_bdot(sh, wl) + _bdot(sl, wh)) + b_ref[...]


def _ada_call(cvec, w_ada, b_ada):
    R, D = cvec.shape
    N = w_ada.shape[1]
    tn = 1024
    return pl.pallas_call(
        _ada_kernel,
        grid=(N // tn,),
        in_specs=[pl.BlockSpec((R, D), lambda j: (0, 0)),
                  pl.BlockSpec((D, tn), lambda j: (0, j)),
                  pl.BlockSpec((1, tn), lambda j: (0, j))],
        out_specs=pl.BlockSpec((R, tn), lambda j: (0, j)),
        out_shape=jax.ShapeDtypeStruct((R, N), jnp.float32),
    )(cvec, w_ada, b_ada.reshape(1, N))


def _mod_row(tile, tokens_per_tile, n_ctx, lat_len):
    t0 = tile * tokens_per_tile
    return jnp.where(t0 < n_ctx, 0, 1 + (t0 - n_ctx) // lat_len)


def _inproj_kernel(x_ref, mod_ref, g_ref, wq_ref, wz_ref, wb_ref, wu_ref, q_ref, z_ref, b_ref, u_ref):
    x = x_ref[...]
    y = x * lax.rsqrt(jnp.mean(x * x, axis=-1, keepdims=True) + EPS) * g_ref[...]
    h = (y * (1.0 + mod_ref[0, 1:2, :]) + mod_ref[0, 0:1, :]).astype(jnp.bfloat16)
    q_ref[...] = _bdot(h, wq_ref[...])
    z_ref[...] = _bdot(h, wz_ref[...])
    b_ref[...] = _bdot(h, wb_ref[...])
    u_ref[...] = _bdot(h, wu_ref[...])


def _inproj_call(x, mod, norm1_g, w_in, n_ctx, lat_len):
    T, D = x.shape
    bf = jnp.bfloat16
    nq, nz, nb = 3 * D_A, D_A, 4 * H_A
    wq = w_in[:, :nq].astype(bf)
    wz = w_in[:, nq:nq + nz].astype(bf)
    wb = jnp.pad(w_in[:, nq + nz:nq + nz + nb], ((0, 0), (0, 128 - nb))).astype(bf)
    wu = w_in[:, nq + nz + nb:].astype(bf)
    row = functools.partial(_mod_row, tokens_per_tile=TM, n_ctx=n_ctx, lat_len=lat_len)

    def full(a):
        return pl.BlockSpec(a.shape, lambda i: (0, 0))

    def rows(n):
        return pl.BlockSpec((TM, n), lambda i: (i, 0))

    return pl.pallas_call(
        _inproj_kernel,
        grid=(T // TM,),
        in_specs=[rows(D), pl.BlockSpec((1, N_MOD, D), lambda i: (row(i), 0, 0)),
                  pl.BlockSpec((1, D), lambda i: (0, 0)), full(wq), full(wz), full(wb), full(wu)],
        out_specs=[rows(nq), rows(nz), rows(128), rows(D_P)],
        out_shape=[jax.ShapeDtypeStruct((T, nq), jnp.float32), jax.ShapeDtypeStruct((T, nz), jnp.float32),
                   jax.ShapeDtypeStruct((T, 128), jnp.float32), jax.ShapeDtypeStruct((T, D_P), jnp.float32)],
        compiler_params=pltpu.CompilerParams(dimension_semantics=("arbitrary",),
                                             vmem_limit_bytes=VMEM_LIMIT),
    )(x, mod, norm1_g.reshape(1, D), wq, wz, wb, wu)


PT = 256


def _window_bounds(pos, w, n):
    return jnp.maximum(pos - w // 2, 0), jnp.minimum(pos + w - w // 2, n)


def _band_sum(band, x):
    xh, xl = _split_bf16(x)
    return _bdot(band, xh) + _bdot(band, xl)


def _pool_seq_kernel(u_ref, pw_ref, ps_ref, o_ref):
    L = u_ref.shape[1]
    ti = lax.broadcasted_iota(jnp.int32, (L, L), 0)
    ji = lax.broadcasted_iota(jnp.int32, (L, L), 1)
    tcol = lax.broadcasted_iota(jnp.int32, (L, 1), 0)
    for i, w in enumerate(POOL_WINDOWS):
        lo, hi = _window_bounds(ti, w, L)
        band = ((ji >= lo) & (ji < hi)).astype(jnp.bfloat16)
        clo, chi = _window_bounds(tcol, w, L)
        ug = u_ref[0, :, i * PG:(i + 1) * PG]
        mean = _band_sum(band, ug) / (chi - clo).astype(jnp.float32)
        d = (mean - ug).astype(jnp.bfloat16)
        o_ref[0, :, i * PG:(i + 1) * PG] = _bdot(d, pw_ref[i]) * ps_ref[:, i * PG:(i + 1) * PG]


def _pool_grid_kernel(u_ref, pw_ref, ps_ref, o_ref, pad_s, r_s):
    L = u_ref.shape[1]
    rows = L // GRID_W
    halo = (max(POOL_WINDOWS) // 2) * GRID_W
    pad_s[0:halo, :] = jnp.zeros((halo, D_P), jnp.float32)
    pad_s[halo + L:, :] = jnp.zeros((halo, D_P), jnp.float32)
    pad_s[halo:halo + L, :] = u_ref[0]
    ti = lax.broadcasted_iota(jnp.int32, (PT, PT), 0)
    ji = lax.broadcasted_iota(jnp.int32, (PT, PT), 1)
    tcol = lax.broadcasted_iota(jnp.int32, (PT, 1), 0)
    for i, w in enumerate(POOL_WINDOWS):
        cs = slice(i * PG, (i + 1) * PG)
        acc = None
        for dr in range(-(w // 2), w - w // 2):
            part = pad_s[halo + dr * GRID_W:halo + dr * GRID_W + L, cs]
            acc = part if acc is None else acc + part
        r_s[...] = acc
        lo, hi = _window_bounds(ti % GRID_W, w, GRID_W)
        band = ((ji // GRID_W == ti // GRID_W) & (ji % GRID_W >= lo) & (ji % GRID_W < hi)).astype(jnp.bfloat16)
        clo, chi = _window_bounds(tcol % GRID_W, w, GRID_W)
        ccnt = (chi - clo).astype(jnp.float32)
        for tile in range(L // PT):
            ts = slice(tile * PT, (tile + 1) * PT)
            rlo, rhi = _window_bounds(tile * (PT // GRID_W) + tcol // GRID_W, w, rows)
            mean = _band_sum(band, r_s[ts, :]) / ((rhi - rlo).astype(jnp.float32) * ccnt)
            d = (mean - u_ref[0, ts, cs]).astype(jnp.bfloat16)
            o_ref[0, ts, cs] = _bdot(d, pw_ref[i]) * ps_ref[:, cs]


def _pool_call(u, pool_w, pool_scale, grid):
    B, L, _ = u.shape
    pw = pool_w.astype(jnp.bfloat16)
    ps = pool_scale.reshape(1, D_P)
    specs = dict(
        grid=(B,),
        in_specs=[pl.BlockSpec((1, L, D_P), lambda b: (b, 0, 0)),
                  pl.BlockSpec((N_PG, PG, PG), lambda b: (0, 0, 0)),
                  pl.BlockSpec((1, D_P), lambda b: (0, 0))],
        out_specs=pl.BlockSpec((1, L, D_P), lambda b: (b, 0, 0)),
        out_shape=jax.ShapeDtypeStruct((B, L, D_P), jnp.float32),
        compiler_params=pltpu.CompilerParams(dimension_semantics=("arbitrary",),
                                             vmem_limit_bytes=VMEM_LIMIT))
    if not grid:
        return pl.pallas_call(_pool_seq_kernel, **specs)(u, pw, ps)
    halo = (max(POOL_WINDOWS) // 2) * GRID_W
    return pl.pallas_call(
        _pool_grid_kernel,
        scratch_shapes=[pltpu.VMEM((L + 2 * halo, D_P), jnp.float32), pltpu.VMEM((L, PG), jnp.float32)],
        **specs)(u, pw, ps)


def _outproj_kernel(x_ref, oa_ref, op_ref, mod_ref, g2_ref, wo_ref, sg_ref, su_ref, sd_ref,
                    x1_ref, h2_ref, sh_ref):
    mix = (_bdot(oa_ref[...].astype(jnp.bfloat16), wo_ref[:D_A, :])
           + _bdot(op_ref[...].astype(jnp.bfloat16), wo_ref[D_A:, :]))
    x1 = x_ref[...] + mod_ref[0, 2:3, :] * mix
    x1_ref[...] = x1
    y = x1 * lax.rsqrt(jnp.mean(x1 * x1, axis=-1, keepdims=True) + EPS) * g2_ref[...]
    h2 = y * (1.0 + mod_ref[0, 4:5, :]) + mod_ref[0, 3:4, :]
    h2_ref[...] = h2
    hb = h2.astype(jnp.bfloat16)
    g = _bdot(hb, sg_ref[...])
    a = (g * jax.nn.sigmoid(g)) * _bdot(hb, su_ref[...])
    sh_ref[...] = _bdot(a.astype(jnp.bfloat16), sd_ref[...])


def _outproj_call(x, o_a, o_p, mod, norm2_g, w_out, sh_gate, sh_up, sh_down, n_ctx, lat_len):
    T, D = x.shape
    bf = jnp.bfloat16
    row = functools.partial(_mod_row, tokens_per_tile=TM, n_ctx=n_ctx, lat_len=lat_len)
    ws = [w_out.astype(bf), sh_gate.astype(bf), sh_up.astype(bf), sh_down.astype(bf)]

    def rows(n):
        return pl.BlockSpec((TM, n), lambda i: (i, 0))

    return pl.pallas_call(
        _outproj_kernel,
        grid=(T // TM,),
        in_specs=[rows(D), rows(D_A), rows(D_P), pl.BlockSpec((1, N_MOD, D), lambda i: (row(i), 0, 0)),
                  pl.BlockSpec((1, D), lambda i: (0, 0))] + [pl.BlockSpec(w.shape, lambda i: (0, 0)) for w in ws],
        out_specs=[rows(D), rows(D), rows(D)],
        out_shape=[jax.ShapeDtypeStruct((T, D), jnp.float32)] * 3,
        compiler_params=pltpu.CompilerParams(dimension_semantics=("arbitrary",),
                                             vmem_limit_bytes=VMEM_LIMIT),
    )(x, o_a, o_p, mod, norm2_g.reshape(1, D), *ws)


SC = 256
CPS = SC // CHUNK
BASE = 16
DELTA_HEAD_ROWS = 4096


def _mm(a, b):
    return jnp.dot(a.astype(jnp.bfloat16), b.astype(jnp.bfloat16), preferred_element_type=jnp.float32)


def _mm_nt(a, b):
    return lax.dot_general(a.astype(jnp.bfloat16), b.astype(jnp.bfloat16), (((1,), (1,)), ((), ())),
                           preferred_element_type=jnp.float32)


def _softplus(x):
    return jnp.maximum(x, 0.0) + jnp.log(1.0 + jnp.exp(-jnp.abs(x)))


def _delta_kernel(sc_ref, xq_ref, xk_ref, xv_ref, z_ref, bac_ref, bar_ref, cwq_ref, cwk_ref, cwv_ref,
                  og_ref, s0_ref, o_ref, st_ref, q_s, k_s, v_s, o_s, vn_s, *, n_sc, zero_init, hpb):
    hb = pl.program_id(1)
    L = q_s.shape[1]

    def conv(x_ref, w_ref, cs):
        x = x_ref[:, cs]
        row = lax.broadcasted_iota(jnp.int32, x.shape, 0)
        acc = x * w_ref[CONV_K // 2:CONV_K // 2 + 1, cs]
        for j in range(CONV_K):
            d = j - CONV_K // 2
            if d == 0:
                continue
            xs = pltpu.roll(x, (-d) % L, 0)
            ok = (row + d >= 0) & (row + d < L)
            acc = acc + jnp.where(ok, xs, 0.0) * w_ref[j:j + 1, cs]
        return acc * jax.nn.sigmoid(acc)

    for hh in range(hpb):
        cs = slice(hh * DK, (hh + 1) * DK)
        q = conv(xq_ref, cwq_ref, cs)
        q_s[hh] = q * lax.rsqrt(jnp.sum(q * q, axis=-1, keepdims=True) + EPS) * (DK ** -0.5)
        k = conv(xk_ref, cwk_ref, cs)
        k_s[hh] = k * lax.rsqrt(jnp.sum(k * k, axis=-1, keepdims=True) + EPS)
        v_s[hh] = conv(xv_ref, cwv_ref, cs)
    o_s[...] = jnp.zeros_like(o_s)

    ri = lax.broadcasted_iota(jnp.int32, (SC, SC), 0)
    ci = lax.broadcasted_iota(jnp.int32, (SC, SC), 1)
    same = (ri // CHUNK) == (ci // CHUNK)
    same_base = (ri // BASE) == (ci // BASE)
    merge_masks = [(ri // w) == (ci // w) for w in (2 * BASE, CHUNK)]
    eye = (ri == ci).astype(jnp.float32)
    rowi = lax.broadcasted_iota(jnp.int32, (SC, DV), 0)

    def unit(m, d, hh, s):
        r0 = pl.multiple_of(m * SC, SC)
        h = hb * hpb + hh
        vn_d = vn_s.at[2 * hh + d]
        q = q_s[hh, pl.ds(r0, SC), :]
        k = k_s[hh, pl.ds(r0, SC), :]
        v = v_s[hh, pl.ds(r0, SC), :]
        bc = bac_ref[0, hh, pl.ds(r0, SC), :]
        br = bar_ref[0, hh, m]
        a_l = sc_ref[d * H_A + h]
        dtb = sc_ref[2 * H_A + d * H_A + h]
        neg_ea = -jnp.exp(jnp.full((1, 1), a_l, jnp.float32))
        beta = jax.nn.sigmoid(bc[:, d:d + 1])
        g_col = neg_ea * _softplus(bc[:, 2 + d:3 + d] + dtb)
        g_row = neg_ea * _softplus(br[2 + d:3 + d, :] + dtb)
        if d == 0:
            tri, strict = same & (ci <= ri), same & (ci < ri)
        else:
            tri, strict = same & (ci >= ri), same & (ci > ri)
        tri_t = same & (ri <= ci) if d == 0 else same & (ri >= ci)
        gc_col = jnp.sum(jnp.where(tri, g_row, 0.0), axis=1, keepdims=True)
        gc_row = jnp.sum(jnp.where(tri_t, g_col, 0.0), axis=0, keepdims=True)
        gl_col = jnp.sum(jnp.where(same, g_row, 0.0), axis=1, keepdims=True)
        decay = jnp.where(tri, jnp.exp(jnp.where(tri, gc_col - gc_row, 0.0)), 0.0)
        kb = k * beta
        a = jnp.where(strict, _mm_nt(kb, k) * decay, 0.0)
        attn = jnp.where(tri, _mm_nt(q, k) * decay, 0.0)
        eg = jnp.exp(gc_col)
        x = jnp.concatenate([v * beta, kb * eg], axis=1)
        a0 = jnp.where(same_base, a, 0.0)
        t = eye - a0
        p = a0
        for _ in range(BASE.bit_length() - 2):
            p = _mm(p, p)
            t = t + _mm(t, p)
        inner = same_base
        for outer in merge_masks:
            off = jnp.where(outer & ~inner, a, 0.0)
            t = t - _mm(t, _mm(off, t))
            inner = outer
        x = _mm(t, x)
        u = x[:, :DV]
        w = x[:, DV:]
        qd = q * eg
        kdt = (k * jnp.exp(gl_col - gc_col)).T
        egl = jnp.exp(gl_col)
        vn_d[...] = jnp.zeros((SC, DV), jnp.float32)
        order = range(CPS) if d == 0 else range(CPS - 1, -1, -1)
        for c in order:
            lo, hi = c * CHUNK, (c + 1) * CHUNK
            ws_qs = _mm(jnp.concatenate([w[lo:hi], qd[lo:hi]], axis=0), s)
            v_new = u[lo:hi] - ws_qs[:CHUNK]
            vn_d[lo:hi, :] = v_new
            vn = vn_d[...]
            o_c = ws_qs[CHUNK:] + _mm(attn[lo:hi, :], vn)
            o_s[hh, pl.ds(r0 + lo, CHUNK), :] += o_c
            v_only = jnp.where((rowi >= lo) & (rowi < hi), vn, 0.0)
            s = s * egl[lo:lo + 1, :] + _mm(kdt, v_only)
        return s

    if zero_init:
        states = tuple(jnp.zeros((DK, DV), jnp.float32) for _ in range(2 * hpb))
    else:
        states = tuple(s0_ref[0, d, hh] for hh in range(hpb) for d in range(2))

    def body(m, carry):
        return tuple(unit(m if d == 0 else n_sc - 1 - m, d, hh, carry[2 * hh + d])
                     for hh in range(hpb) for d in range(2))

    if n_sc == 1:
        states = body(0, states)
    else:
        states = lax.fori_loop(0, n_sc, body, states)

    for hh in range(hpb):
        for d in range(2):
            st_ref[0, d, hh] = states[2 * hh + d]
        o = o_s[hh]
        o = o * lax.rsqrt(jnp.mean(o * o, axis=-1, keepdims=True) + EPS) * og_ref[...]
        zz = z_ref[:, hh * DV:(hh + 1) * DV]
        o_ref[:, hh * DV:(hh + 1) * DV] = o * (zz * jax.nn.sigmoid(zz))


def _delta_call(qkv, z, ba, conv_w, a_log, dt_bias, onorm_g, s0, B, L, row_blk0):
    n_sc = L // SC
    t0 = row_blk0 * L
    bah = ba[t0:t0 + B * L, :4 * H_A].reshape(B, L, 4, H_A).transpose(0, 3, 1, 2)
    bar = bah.reshape(B, H_A, n_sc, SC, 4).transpose(0, 1, 2, 4, 3)
    scal = jnp.concatenate([a_log.reshape(-1), dt_bias.reshape(-1)]).astype(jnp.float32)
    hpb = max(1, min(H_A, DELTA_HEAD_ROWS // L))
    n_hb = H_A // hpb
    zero_init = s0 is None
    if zero_init:
        s0 = jnp.zeros((1, 2, hpb, DK, DV), jnp.float32)
        s0_spec = pl.BlockSpec((1, 2, hpb, DK, DV), lambda b, h, sc: (0, 0, 0, 0, 0))
    else:
        s0_spec = pl.BlockSpec((1, 2, hpb, DK, DV), lambda b, h, sc: (b, 0, h, 0, 0))

    def col(off):
        return pl.BlockSpec((L, hpb * DK), lambda b, h, sc: (row_blk0 + b, off * n_hb + h))

    def cw(off):
        return pl.BlockSpec((CONV_K, hpb * DK), lambda b, h, sc: (0, off * n_hb + h))

    kern = functools.partial(_delta_kernel, n_sc=n_sc, zero_init=zero_init, hpb=hpb)
    return pl.pallas_call(
        kern,
        grid_spec=pltpu.PrefetchScalarGridSpec(
            num_scalar_prefetch=1,
            grid=(B, n_hb),
            in_specs=[col(0), col(1), col(2),
                      pl.BlockSpec((L, hpb * DV), lambda b, h, sc: (row_blk0 + b, h)),
                      pl.BlockSpec((1, hpb, L, 4), lambda b, h, sc: (b, h, 0, 0)),
                      pl.BlockSpec((1, hpb, n_sc, 4, SC), lambda b, h, sc: (b, h, 0, 0, 0)),
                      cw(0), cw(1), cw(2),
                      pl.BlockSpec((1, DV), lambda b, h, sc: (0, 0)),
                      s0_spec],
            out_specs=[pl.BlockSpec((L, hpb * DV), lambda b, h, sc: (b, h)),
                       pl.BlockSpec((1, 2, hpb, DK, DV), lambda b, h, sc: (b, 0, h, 0, 0))],
            scratch_shapes=[pltpu.VMEM((hpb, L, DK), jnp.float32), pltpu.VMEM((hpb, L, DK), jnp.float32),
                            pltpu.VMEM((hpb, L, DV), jnp.float32), pltpu.VMEM((hpb, L, DV), jnp.float32),
                            pltpu.VMEM((2 * hpb, SC, DV), jnp.float32)]),
        out_shape=[jax.ShapeDtypeStruct((B * L, D_A), jnp.float32),
                   jax.ShapeDtypeStruct((B, 2, H_A, DK, DV), jnp.float32)],
        compiler_params=pltpu.CompilerParams(dimension_semantics=("arbitrary", "arbitrary"),
                                             vmem_limit_bytes=VMEM_LIMIT),
    )(scal, qkv, qkv, qkv, z, bah, bar, conv_w, conv_w, conv_w, onorm_g.reshape(1, DV), s0)


TR = 256
GSZ = N_EXPERTS // N_GROUPS
NEG = -jnp.inf


def _col_to_row(col, eye_mask):
    return jnp.sum(jnp.where(eye_mask, col, jnp.zeros_like(col)), axis=0, keepdims=True)


def _route_kernel(h_ref, rwh_ref, rwl_ref, rb_ref, idx_ref, rank_ref, w_ref, cnt_ref, cnt_s):
    i = pl.program_id(0)

    @pl.when(i == 0)
    def _():
        cnt_s[...] = jnp.zeros_like(cnt_s)

    h = h_ref[...]
    hh, hl = _split_bf16(h)
    logits = _bdot(hh, rwh_ref[...]) + (_bdot(hh, rwl_ref[...]) + _bdot(hl, rwh_ref[...]))
    scores = jax.nn.sigmoid(logits)
    sel = scores + rb_ref[...]
    lane = lax.broadcasted_iota(jnp.int32, sel.shape, 1)
    gid = lane // GSZ

    def first_argmax(v):
        m = jnp.max(v, axis=1, keepdims=True)
        first = jnp.min(jnp.where(v == m, lane, N_EXPERTS), axis=1, keepdims=True)
        return m, first

    gscore = []
    for g in range(N_GROUPS):
        vg = jnp.where(gid == g, sel, NEG)
        m1, i1 = first_argmax(vg)
        m2 = jnp.max(jnp.where(lane == i1, NEG, vg), axis=1, keepdims=True)
        gscore.append(m1 + m2)
    emask = jnp.zeros(sel.shape, jnp.bool_)
    for g in range(N_GROUPS):
        beat = jnp.zeros(gscore[g].shape, jnp.int32)
        for o in range(N_GROUPS):
            if o == g:
                continue
            wins = (gscore[o] > gscore[g]) | ((gscore[o] == gscore[g]) & (o < g))
            beat = beat + wins.astype(jnp.int32)
        emask = emask | ((gid == g) & (beat < TOPK_GROUP))
    cand = jnp.where(emask, sel, NEG)
    chosen = []
    picked = jnp.zeros(sel.shape, jnp.bool_)
    for _ in range(TOP_K):
        _, ik = first_argmax(cand)
        hit = lane == ik
        chosen.append((ik, hit))
        picked = picked | hit
        cand = jnp.where(hit, NEG, cand)
    wraw = jnp.where(picked, scores, 0.0)
    wmat = wraw / jnp.sum(wraw, axis=1, keepdims=True) * ROUTED_SCALE

    pm = picked.astype(jnp.bfloat16)
    ri = lax.broadcasted_iota(jnp.int32, (TR, TR), 0)
    ci = lax.broadcasted_iota(jnp.int32, (TR, TR), 1)
    earlier = (ci < ri).astype(jnp.bfloat16)
    rank_mat = _bdot(earlier, pm) + cnt_s[...]
    cnt_s[...] = cnt_s[...] + jnp.sum(picked.astype(jnp.float32), axis=0, keepdims=True)
    cnt_ref[...] = cnt_s[...].astype(jnp.int32)

    eye = ri == ci
    lane8 = lax.broadcasted_iota(jnp.int32, (TR, TOP_K), 1)
    wcols = jnp.zeros((TR, TOP_K), jnp.float32)
    for k, (ik, hit) in enumerate(chosen):
        rk = jnp.sum(jnp.where(hit, rank_mat, 0.0), axis=1, keepdims=True)
        wk = jnp.sum(jnp.where(hit, wmat, 0.0), axis=1, keepdims=True)
        idx_ref[0, k:k + 1, :] = _col_to_row(ik, eye)
        rank_ref[0, k:k + 1, :] = _col_to_row(rk, eye).astype(jnp.int32)
        wcols = jnp.where(lane8 == k, wk, wcols)
    w_ref[...] = wcols


def _route_call(hf, router_w, router_bias):
    T, D = hf.shape
    n_tiles = T // TR
    rwh, rwl = _split_bf16(router_w)
    return pl.pallas_call(
        _route_kernel,
        grid=(n_tiles,),
        in_specs=[pl.BlockSpec((TR, D), lambda i: (i, 0)),
                  pl.BlockSpec((D, N_EXPERTS), lambda i: (0, 0)),
                  pl.BlockSpec((D, N_EXPERTS), lambda i: (0, 0)),
                  pl.BlockSpec((1, N_EXPERTS), lambda i: (0, 0))],
        out_specs=[pl.BlockSpec((1, TOP_K, TR), lambda i: (i, 0, 0)),
                   pl.BlockSpec((1, TOP_K, TR), lambda i: (i, 0, 0)),
                   pl.BlockSpec((TR, TOP_K), lambda i: (i, 0)),
                   pl.BlockSpec((1, N_EXPERTS), lambda i: (0, 0))],
        scratch_shapes=[pltpu.VMEM((1, N_EXPERTS), jnp.float32)],
        out_shape=[jax.ShapeDtypeStruct((n_tiles, TOP_K, TR), jnp.int32),
                   jax.ShapeDtypeStruct((n_tiles, TOP_K, TR), jnp.int32),
                   jax.ShapeDtypeStruct((T, TOP_K), jnp.float32),
                   jax.ShapeDtypeStruct((1, N_EXPERTS), jnp.int32)],
        compiler_params=pltpu.CompilerParams(dimension_semantics=("arbitrary",)),
    )(hf, rwh, rwl, router_bias.reshape(1, N_EXPERTS).astype(jnp.float32))


def _dispatch_kernel(idx_ref, rank_ref, pstart_ref, h_ref, xs_init, pos_ref, xs_hbm, pos_v, pos_s, ssem, psem):
    del xs_init
    erow = lax.broadcasted_iota(jnp.int32, (N_EXPERTS, TR), 0)
    pstart = pstart_ref[...]
    for k in range(TOP_K):
        hit = erow == idx_ref[0, k:k + 1, :]
        seg = jnp.sum(jnp.where(hit, pstart, 0), axis=0, keepdims=True)
        pos_v[k:k + 1, :] = seg + rank_ref[0, k:k + 1, :]
    pos_ref[0] = pos_v[...]
    cp = pltpu.make_async_copy(pos_v, pos_s, psem)
    cp.start()
    cp.wait()

    def body(t, carry):
        for k in range(TOP_K):
            pltpu.make_async_copy(h_ref.at[t], xs_hbm.at[pos_s[k, t]], ssem).start()
        return carry

    lax.fori_loop(0, TR, body, 0, unroll=8)
    n_rows = TR * TOP_K
    pltpu.make_async_copy(xs_hbm.at[pl.ds(0, n_rows)], xs_hbm.at[pl.ds(0, n_rows)], ssem).wait()


def _dispatch_call(hf, idx, rank, pad_start, n_pad):
    T, D = hf.shape
    n_tiles = T // TR
    return pl.pallas_call(
        _dispatch_kernel,
        grid=(n_tiles,),
        in_specs=[pl.BlockSpec((1, TOP_K, TR), lambda i: (i, 0, 0)),
                  pl.BlockSpec((1, TOP_K, TR), lambda i: (i, 0, 0)),
                  pl.BlockSpec((N_EXPERTS, 1), lambda i: (0, 0)),
                  pl.BlockSpec((TR, D), lambda i: (i, 0)),
                  pl.BlockSpec(memory_space=pl.ANY)],
        out_specs=[pl.BlockSpec((1, TOP_K, TR), lambda i: (i, 0, 0)),
                   pl.BlockSpec(memory_space=pl.ANY)],
        scratch_shapes=[pltpu.VMEM((TOP_K, TR), jnp.int32), pltpu.SMEM((TOP_K, TR), jnp.int32),
                        pltpu.SemaphoreType.DMA, pltpu.SemaphoreType.DMA],
        out_shape=[jax.ShapeDtypeStruct((n_tiles, TOP_K, TR), jnp.int32),
                   jax.ShapeDtypeStruct((n_pad, D), hf.dtype)],
        input_output_aliases={4: 1},
        compiler_params=pltpu.CompilerParams(dimension_semantics=("arbitrary",)),
    )(idx, rank, pad_start.reshape(N_EXPERTS, 1), hf, jnp.zeros((n_pad, D), hf.dtype))


BM = 256


def _expert_kernel(blk_e_ref, nvalid_ref, nused_ref, x_ref, wg_ref, wu_ref, wd_ref, y_ref, wg_s, wu_s, wd_s):
    i = pl.program_id(0)

    @pl.when(i < nused_ref[0])
    def _():
        e = blk_e_ref[i]
        prev = blk_e_ref[jnp.maximum(i - 1, 0)]

        @pl.when((i == 0) | (e != prev))
        def _():
            wg_s[...] = wg_ref[0].astype(jnp.bfloat16)
            wu_s[...] = wu_ref[0].astype(jnp.bfloat16)
            wd_s[...] = wd_ref[0].astype(jnp.bfloat16)

        row = lax.broadcasted_iota(jnp.int32, (BM, 1), 0)
        x = jnp.where(row < nvalid_ref[i], x_ref[...], 0.0).astype(jnp.bfloat16)
        g = _bdot(x, wg_s[...])
        u = _bdot(x, wu_s[...])
        a = (g * jax.nn.sigmoid(g)) * u
        y_ref[...] = _bdot(a.astype(jnp.bfloat16), wd_s[...])

    @pl.when(i >= nused_ref[0])
    def _():
        y_ref[...] = jnp.zeros_like(y_ref)


def _expert_call(x_sorted, blk_e, n_valid, n_used, w_gate, w_up, w_down):
    n_pad, D = x_sorted.shape
    n_blk = n_pad // BM
    E, _, F = w_gate.shape

    def row_map(i, be, nv, nu):
        return (jnp.minimum(i, nu[0] - 1), 0)

    def w_map(i, be, nv, nu):
        return (be[jnp.minimum(i, nu[0] - 1)], 0, 0)

    return pl.pallas_call(
        _expert_kernel,
        grid_spec=pltpu.PrefetchScalarGridSpec(
            num_scalar_prefetch=3,
            grid=(n_blk,),
            in_specs=[pl.BlockSpec((BM, D), row_map),
                      pl.BlockSpec((1, D, F), w_map),
                      pl.BlockSpec((1, D, F), w_map),
                      pl.BlockSpec((1, F, D), w_map)],
            out_specs=pl.BlockSpec((BM, D), lambda i, be, nv, nu: (i, 0)),
            scratch_shapes=[pltpu.VMEM((D, F), jnp.bfloat16), pltpu.VMEM((D, F), jnp.bfloat16),
                            pltpu.VMEM((F, D), jnp.bfloat16)]),
        out_shape=jax.ShapeDtypeStruct((n_pad, D), jnp.float32),
        compiler_params=pltpu.CompilerParams(dimension_semantics=("arbitrary",),
                                             vmem_limit_bytes=VMEM_LIMIT),
    )(blk_e, n_valid, n_used, x_sorted, w_gate, w_up, w_down)


TC = 64


def _combine_kernel(pos_hbm, y_hbm, w_ref, x1_ref, sh_ref, mod_ref, fg_ref, out_ref, ybuf, pos_s, gsem, psem):
    j = pl.program_id(0)
    last = pl.num_programs(0) - 1

    def pos_copy(b, slot):
        return pltpu.make_async_copy(pos_hbm.at[b], pos_s.at[slot], psem.at[slot])

    def start_gather(slot):
        for k in range(TOP_K):
            for t in range(TC):
                pltpu.make_async_copy(y_hbm.at[pos_s[slot, k * TC + t]], ybuf.at[slot, k, t],
                                      gsem.at[slot]).start()

    def wait_gather(slot):
        pltpu.make_async_copy(ybuf.at[slot], ybuf.at[slot], gsem.at[slot]).wait()

    @pl.when(j == 0)
    def _():
        pos_copy(0, 0).start()
        pos_copy(0, 0).wait()
        start_gather(0)
        pos_copy(jnp.minimum(1, last), 1).start()

    slot = j % 2
    nslot = 1 - slot
    pos_copy(0, nslot).wait()
    start_gather(nslot)
    pos_copy(jnp.minimum(j + 2, last), slot).start()
    wait_gather(slot)
    w = w_ref[...]
    acc = w[:, 0:1] * ybuf[slot, 0]
    for k in range(1, TOP_K):
        acc = acc + w[:, k:k + 1] * ybuf[slot, k]
    x2 = x1_ref[...] + mod_ref[0, 5:6, :] * (acc + sh_ref[...])
    out_ref[...] = x2 * lax.rsqrt(jnp.mean(x2 * x2, axis=-1, keepdims=True) + EPS) * fg_ref[...]

    @pl.when(j == last)
    def _():
        wait_gather(nslot)
        pos_copy(0, slot).wait()


def _combine_call(y_sorted, pos_t, wts, x1, shared, mod, final_g, n_ctx, lat_len):
    T, K = wts.shape
    D = y_sorted.shape[1]
    n_tiles = T // TC
    row = functools.partial(_mod_row, tokens_per_tile=TC, n_ctx=n_ctx, lat_len=lat_len)
    return pl.pallas_call(
        _combine_kernel,
        grid=(n_tiles,),
        in_specs=[pl.BlockSpec(memory_space=pl.ANY),
                  pl.BlockSpec(memory_space=pl.ANY),
                  pl.BlockSpec((TC, K), lambda j: (j, 0)),
                  pl.BlockSpec((TC, D), lambda j: (j, 0)),
                  pl.BlockSpec((TC, D), lambda j: (j, 0)),
                  pl.BlockSpec((1, N_MOD, D), lambda j: (row(j), 0, 0)),
                  pl.BlockSpec((1, D), lambda j: (0, 0))],
        out_specs=pl.BlockSpec((TC, D), lambda j: (j, 0)),
        scratch_shapes=[pltpu.VMEM((2, K, TC, D), jnp.float32),
                        pltpu.SMEM((2, K * TC), jnp.int32),
                        pltpu.SemaphoreType.DMA((2,)), pltpu.SemaphoreType.DMA((2,))],
        out_shape=jax.ShapeDtypeStruct((T, D), jnp.float32),
        compiler_params=pltpu.CompilerParams(dimension_semantics=("arbitrary",)),
    )(pos_t, y_sorted, wts, x1, shared, mod, final_g.reshape(1, D))


def _moe_routed(h2, router_w, router_bias, w_gate, w_up, w_down):
    T, D = h2.shape
    idx, rank, wts, cnt = _route_call(h2, router_w, router_bias)
    counts = cnt[0]
    padded = (counts + BM - 1) // BM * BM
    pad_end = jnp.cumsum(padded)
    pad_start = (pad_end - padded).astype(jnp.int32)
    n_pad = T * TOP_K + N_EXPERTS * BM
    n_blk = n_pad // BM
    pos, x_sorted = _dispatch_call(h2, idx, rank, pad_start, n_pad)
    blk_row0 = jnp.arange(n_blk, dtype=jnp.int32) * BM
    blk_e = jnp.minimum(jnp.sum((pad_end[None, :] <= blk_row0[:, None]).astype(jnp.int32), axis=1), N_EXPERTS - 1)
    own = blk_e[:, None] == jnp.arange(N_EXPERTS, dtype=jnp.int32)[None, :]
    seg_end = jnp.sum(jnp.where(own, (pad_start + counts)[None, :], 0), axis=1)
    n_valid = jnp.clip(seg_end - blk_row0, 0, BM).astype(jnp.int32)
    n_used = (pad_end[-1] // BM).astype(jnp.int32).reshape(1)
    y = _expert_call(x_sorted, blk_e, n_valid, n_used, w_gate, w_up, w_down)
    pos_t = pos.reshape(T // TR, TOP_K, TR // TC, TC).transpose(0, 2, 1, 3).reshape(T // TC, TOP_K * TC)
    return y, pos_t, wts


def kernel(x_prompt, x_sample, state_delta, c, c_ctx, w_ada, b_ada, norm1_g, w_in, conv_w, a_log,
           dt_bias, onorm_g, pool_w, pool_scale, w_out, norm2_g, router_w, router_bias, exp_w_gate,
           exp_w_up, exp_w_down, sh_w_gate, sh_w_up, sh_w_down, final_g):
    Bc, Lc, D = x_prompt.shape
    Bl, Ll, _ = x_sample.shape
    n_ctx = Bc * Lc
    assert DEPTH == 1 and 1 + Bl <= MOD_ROWS and n_ctx % Ll == 0
    x = jnp.concatenate([x_prompt.reshape(n_ctx, D), x_sample.reshape(Bl * Ll, D)], axis=0)
    cvec = jnp.concatenate([c_ctx[None], c, jnp.zeros((MOD_ROWS - 1 - Bl, D), c.dtype)], axis=0)
    l = 0
    mod = _ada_call(cvec, w_ada[l], b_ada[l]).reshape(MOD_ROWS, N_MOD, D)
    qkv, z, ba, u = _inproj_call(x, mod, norm1_g[l], w_in[l], n_ctx, Ll)
    dn = (conv_w[l], a_log[l], dt_bias[l], onorm_g[l])
    oa_c, st_ctx = _delta_call(qkv, z, ba, *dn, None, Bc, Lc, 0)
    oa_l, _ = _delta_call(qkv, z, ba, *dn, state_delta[:, l], Bl, Ll, n_ctx // Ll)
    op_c = _pool_call(u[:n_ctx].reshape(Bc, Lc, D_P), pool_w[l], pool_scale[l], False)
    op_l = _pool_call(u[n_ctx:].reshape(Bl, Ll, D_P), pool_w[l], pool_scale[l], True)
    o_a = jnp.concatenate([oa_c, oa_l], axis=0)
    o_p = jnp.concatenate([op_c.reshape(n_ctx, D_P), op_l.reshape(Bl * Ll, D_P)], axis=0)
    x1, h2, shared = _outproj_call(x, o_a, o_p, mod, norm2_g[l], w_out[l], sh_w_gate[l], sh_w_up[l],
                                   sh_w_down[l], n_ctx, Ll)
    y, pos_t, wts = _moe_routed(h2, router_w[l], router_bias[l], exp_w_gate[l], exp_w_up[l], exp_w_down[l])
    out = _combine_call(y, pos_t, wts, x1, shared, mod, final_g, n_ctx, Ll)
    y_prompt = out[:n_ctx].reshape(Bc, Lc, D)
    y_sample = out[n_ctx:].reshape(Bl, Ll, D)
    new_state_delta = st_ctx[:, None].astype(x_prompt.dtype)
    return (y_prompt, y_sample, new_state_delta)
```

```python
import functools
import jax, jax.numpy as jnp
from jax import lax
from jax.experimental import pallas as pl
from jax.experimental.pallas import tpu as pltpu

D_MODEL = 1024
DEPTH = 1
GRID_W = 64
D_MIX = D_MODEL
D_A = D_MIX // 2
D_P = D_MIX - D_A
H_A = 4
DK = D_A // H_A
DV = D_A // H_A
CONV_K = 5
CHUNK = 64
POOL_WINDOWS = (2, 4, 8, 16)
N_PG = len(POOL_WINDOWS)
PG = D_P // N_PG
N_EXPERTS = 256
TOP_K = 8
N_GROUPS = 8
TOPK_GROUP = 4
ROUTED_SCALE = 2.5
EPS = 1e-6
VMEM_LIMIT = 48 * 1024 * 1024


def _split_bf16(a):
    hi = a.astype(jnp.bfloat16)
    return hi, (a - hi.astype(jnp.float32)).astype(jnp.bfloat16)


def _bdot(a, b):
    return jnp.dot(a, b, preferred_element_type=jnp.float32)


N_MOD = 6
MOD_ROWS = 8
TM = 512


def _ada_kernel(c_ref, w_ref, b_ref, o_ref):
    c = c_ref[...]
    s = c * jax.nn.sigmoid(c)
    sh, sl = _split_bf16(s)
    wh, wl = _split_bf16(w_ref[...])
    o_ref[...] = _bdot(sh, wh) + (_bdot(sh, wl) + _bdot(sl, wh)) + b_ref[...]


def _ada_call(cvec, w_ada, b_ada):
    R, D = cvec.shape
    N = w_ada.shape[1]
    tn = 1024
    return pl.pallas_call(
        _ada_kernel,
        grid=(N // tn,),
        in_specs=[pl.BlockSpec((R, D), lambda j: (0, 0)),
                  pl.BlockSpec((D, tn), lambda j: (0, j)),
                  pl.BlockSpec((1, tn), lambda j: (0, j))],
        out_specs=pl.BlockSpec((R, tn), lambda j: (0, j)),
        out_shape=jax.ShapeDtypeStruct((R, N), jnp.float32),
    )(cvec, w_ada, b_ada.reshape(1, N))


def _mod_row(tile, tokens_per_tile, n_ctx, lat_len):
    t0 = tile * tokens_per_tile
    return jnp.where(t0 < n_ctx, 0, 1 + (t0 - n_ctx) // lat_len)


def _inproj_kernel(x_ref, mod_ref, g_ref, wq_ref, wz_ref, wb_ref, wu_ref, q_ref, z_ref, b_ref, u_ref):
    x = x_ref[...]
    y = x * lax.rsqrt(jnp.mean(x * x, axis=-1, keepdims=True) + EPS) * g_ref[...]
    h = (y * (1.0 + mod_ref[0, 1:2, :]) + mod_ref[0, 0:1, :]).astype(jnp.bfloat16)
    q_ref[...] = _bdot(h, wq_ref[...])
    z_ref[...] = _bdot(h, wz_ref[...])
    b_ref[...] = _bdot(h, wb_ref[...])
    u_ref[...] = _bdot(h, wu_ref[...])


def _inproj_call(x, mod, norm1_g, w_in, n_ctx, lat_len):
    T, D = x.shape
    bf = jnp.bfloat16
    nq, nz, nb = 3 * D_A, D_A, 4 * H_A
    wq = w_in[:, :nq].astype(bf)
    wz = w_in[:, nq:nq + nz].astype(bf)
    wb = jnp.pad(w_in[:, nq + nz:nq + nz + nb], ((0, 0), (0, 128 - nb))).astype(bf)
    wu = w_in[:, nq + nz + nb:].astype(bf)
    row = functools.partial(_mod_row, tokens_per_tile=TM, n_ctx=n_ctx, lat_len=lat_len)

    def full(a):
        return pl.BlockSpec(a.shape, lambda i: (0, 0))

    def rows(n):
        return pl.BlockSpec((TM, n), lambda i: (i, 0))

    return pl.pallas_call(
        _inproj_kernel,
        grid=(T // TM,),
        in_specs=[rows(D), pl.BlockSpec((1, N_MOD, D), lambda i: (row(i), 0, 0)),
                  pl.BlockSpec((1, D), lambda i: (0, 0)), full(wq), full(wz), full(wb), full(wu)],
        out_specs=[rows(nq), rows(nz), rows(128), rows(D_P)],
        out_shape=[jax.ShapeDtypeStruct((T, nq), jnp.float32), jax.ShapeDtypeStruct((T, nz), jnp.float32),
                   jax.ShapeDtypeStruct((T, 128), jnp.float32), jax.ShapeDtypeStruct((T, D_P), jnp.float32)],
        compiler_params=pltpu.CompilerParams(dimension_semantics=("arbitrary",),
                                             vmem_limit_bytes=VMEM_LIMIT),
    )(x, mod, norm1_g.reshape(1, D), wq, wz, wb, wu)


PT = 256


def _window_bounds(pos, w, n):
    return jnp.maximum(pos - w // 2, 0), jnp.minimum(pos + w - w // 2, n)


def _band_sum(band, x):
    xh, xl = _split_bf16(x)
    return _bdot(band, xh) + _bdot(band, xl)


def _pool_seq_kernel(u_ref, pw_ref, ps_ref, o_ref):
    L = u_ref.shape[1]
    ti = lax.broadcasted_iota(jnp.int32, (L, L), 0)
    ji = lax.broadcasted_iota(jnp.int32, (L, L), 1)
    tcol = lax.broadcasted_iota(jnp.int32, (L, 1), 0)
    for i, w in enumerate(POOL_WINDOWS):
        lo, hi = _window_bounds(ti, w, L)
        band = ((ji >= lo) & (ji < hi)).astype(jnp.bfloat16)
        clo, chi = _window_bounds(tcol, w, L)
        ug = u_ref[0, :, i * PG:(i + 1) * PG]
        mean = _band_sum(band, ug) / (chi - clo).astype(jnp.float32)
        d = (mean - ug).astype(jnp.bfloat16)
        o_ref[0, :, i * PG:(i + 1) * PG] = _bdot(d, pw_ref[i]) * ps_ref[:, i * PG:(i + 1) * PG]


def _pool_grid_kernel(u_ref, pw_ref, ps_ref, o_ref, pad_s, r_s):
    L = u_ref.shape[1]
    rows = L // GRID_W
    halo = (max(POOL_WINDOWS) // 2) * GRID_W
    pad_s[0:halo, :] = jnp.zeros((halo, D_P), jnp.float32)
    pad_s[halo + L:, :] = jnp.zeros((halo, D_P), jnp.float32)
    pad_s[halo:halo + L, :] = u_ref[0]
    ti = lax.broadcasted_iota(jnp.int32, (PT, PT), 0)
    ji = lax.broadcasted_iota(jnp.int32, (PT, PT), 1)
    tcol = lax.broadcasted_iota(jnp.int32, (PT, 1), 0)
    for i, w in enumerate(POOL_WINDOWS):
        cs = slice(i * PG, (i + 1) * PG)
        acc = None
        for dr in range(-(w // 2), w - w // 2):
            part = pad_s[halo + dr * GRID_W:halo + dr * GRID_W + L, cs]
            acc = part if acc is None else acc + part
        r_s[...] = acc
        lo, hi = _window_bounds(ti % GRID_W, w, GRID_W)
        band = ((ji // GRID_W == ti // GRID_W) & (ji % GRID_W >= lo) & (ji % GRID_W < hi)).astype(jnp.bfloat16)
        clo, chi = _window_bounds(tcol % GRID_W, w, GRID_W)
        ccnt = (chi - clo).astype(jnp.float32)
        for tile in range(L // PT):
            ts = slice(tile * PT, (tile + 1) * PT)
            rlo, rhi = _window_bounds(tile * (PT // GRID_W) + tcol // GRID_W, w, rows)
            mean = _band_sum(band, r_s[ts, :]) / ((rhi - rlo).astype(jnp.float32) * ccnt)
            d = (mean - u_ref[0, ts, cs]).astype(jnp.bfloat16)
            o_ref[0, ts, cs] = _bdot(d, pw_ref[i]) * ps_ref[:, cs]


def _pool_call(u, pool_w, pool_scale, grid):
    B, L, _ = u.shape
    pw = pool_w.astype(jnp.bfloat16)
    ps = pool_scale.reshape(1, D_P)
    specs = dict(
        grid=(B,),
        in_specs=[pl.BlockSpec((1, L, D_P), lambda b: (b, 0, 0)),
                  pl.BlockSpec((N_PG, PG, PG), lambda b: (0, 0, 0)),
                  pl.BlockSpec((1, D_P), lambda b: (0, 0))],
        out_specs=pl.BlockSpec((1, L, D_P), lambda b: (b, 0, 0)),
        out_shape=jax.ShapeDtypeStruct((B, L, D_P), jnp.float32),
        compiler_params=pltpu.CompilerParams(dimension_semantics=("arbitrary",),
                                             vmem_limit_bytes=VMEM_LIMIT))
    if not grid:
        return pl.pallas_call(_pool_seq_kernel, **specs)(u, pw, ps)
    halo = (max(POOL_WINDOWS) // 2) * GRID_W
    return pl.pallas_call(
        _pool_grid_kernel,
        scratch_shapes=[pltpu.VMEM((L + 2 * halo, D_P), jnp.float32), pltpu.VMEM((L, PG), jnp.float32)],
        **specs)(u, pw, ps)


def _outproj_kernel(x_ref, oa_ref, op_ref, mod_ref, g2_ref, wo_ref, sg_ref, su_ref, sd_ref,
                    x1_ref, h2_ref, sh_ref):
    mix = (_bdot(oa_ref[...].astype(jnp.bfloat16), wo_ref[:D_A, :])
           + _bdot(op_ref[...].astype(jnp.bfloat16), wo_ref[D_A:, :]))
    x1 = x_ref[...] + mod_ref[0, 2:3, :] * mix
    x1_ref[...] = x1
    y = x1 * lax.rsqrt(jnp.mean(x1 * x1, axis=-1, keepdims=True) + EPS) * g2_ref[...]
    h2 = y * (1.0 + mod_ref[0, 4:5, :]) + mod_ref[0, 3:4, :]
    h2_ref[...] = h2
    hb = h2.astype(jnp.bfloat16)
    g = _bdot(hb, sg_ref[...])
    a = (g * jax.nn.sigmoid(g)) * _bdot(hb, su_ref[...])
    sh_ref[...] = _bdot(a.astype(jnp.bfloat16), sd_ref[...])


def _outproj_call(x, o_a, o_p, mod, norm2_g, w_out, sh_gate, sh_up, sh_down, n_ctx, lat_len):
    T, D = x.shape
    bf = jnp.bfloat16
    row = functools.partial(_mod_row, tokens_per_tile=TM, n_ctx=n_ctx, lat_len=lat_len)
    ws = [w_out.astype(bf), sh_gate.astype(bf), sh_up.astype(bf), sh_down.astype(bf)]

    def rows(n):
        return pl.BlockSpec((TM, n), lambda i: (i, 0))

    return pl.pallas_call(
        _outproj_kernel,
        grid=(T // TM,),
        in_specs=[rows(D), rows(D_A), rows(D_P), pl.BlockSpec((1, N_MOD, D), lambda i: (row(i), 0, 0)),
                  pl.BlockSpec((1, D), lambda i: (0, 0))] + [pl.BlockSpec(w.shape, lambda i: (0, 0)) for w in ws],
        out_specs=[rows(D), rows(D), rows(D)],
        out_shape=[jax.ShapeDtypeStruct((T, D), jnp.float32)] * 3,
        compiler_params=pltpu.CompilerParams(dimension_semantics=("arbitrary",),
                                             vmem_limit_bytes=VMEM_LIMIT),
    )(x, o_a, o_p, mod, norm2_g.reshape(1, D), *ws)


SC = 256
CPS = SC // CHUNK
BASE = 16
DELTA_HEAD_ROWS = 4096


def _mm(a, b):
    return jnp.dot(a.astype(jnp.bfloat16), b.astype(jnp.bfloat16), preferred_element_type=jnp.float32)


def _mm_nt(a, b):
    return lax.dot_general(a.astype(jnp.bfloat16), b.astype(jnp.bfloat16), (((1,), (1,)), ((), ())),
                           preferred_element_type=jnp.float32)


def _softplus(x):
    return jnp.maximum(x, 0.0) + jnp.log(1.0 + jnp.exp(-jnp.abs(x)))


def _delta_kernel(sc_ref, xq_ref, xk_ref, xv_ref, z_ref, bac_ref, bar_ref, cwq_ref, cwk_ref, cwv_ref,
                  og_ref, s0_ref, o_ref, st_ref, q_s, k_s, v_s, o_s, vn_s, *, n_sc, zero_init, hpb):
    hb = pl.program_id(1)
    L = q_s.shape[1]

    def conv(x_ref, w_ref, cs):
        x = x_ref[:, cs]
        row = lax.broadcasted_iota(jnp.int32, x.shape, 0)
        acc = x * w_ref[CONV_K // 2:CONV_K // 2 + 1, cs]
        for j in range(CONV_K):
            d = j - CONV_K // 2
            if d == 0:
                continue
            xs = pltpu.roll(x, (-d) % L, 0)
            ok = (row + d >= 0) & (row + d < L)
            acc = acc + jnp.where(ok, xs, 0.0) * w_ref[j:j + 1, cs]
        return acc * jax.nn.sigmoid(acc)

    for hh in range(hpb):
        cs = slice(hh * DK, (hh + 1) * DK)
        q = conv(xq_ref, cwq_ref, cs)
        q_s[hh] = q * lax.rsqrt(jnp.sum(q * q, axis=-1, keepdims=True) + EPS) * (DK ** -0.5)
        k = conv(xk_ref, cwk_ref, cs)
        k_s[hh] = k * lax.rsqrt(jnp.sum(k * k, axis=-1, keepdims=True) + EPS)
        v_s[hh] = conv(xv_ref, cwv_ref, cs)
    o_s[...] = jnp.zeros_like(o_s)

    ri = lax.broadcasted_iota(jnp.int32, (SC, SC), 0)
    ci = lax.broadcasted_iota(jnp.int32, (SC, SC), 1)
    same = (ri // CHUNK) == (ci // CHUNK)
    same_base = (ri // BASE) == (ci // BASE)
    merge_masks = [(ri // w) == (ci // w) for w in (2 * BASE, CHUNK)]
    eye = (ri == ci).astype(jnp.float32)
    rowi = lax.broadcasted_iota(jnp.int32, (SC, DV), 0)

    def prep(m, d, hh):
        r0 = pl.multiple_of(m * SC, SC)
        h = hb * hpb + hh
        q = q_s[hh, pl.ds(r0, SC), :]
        k = k_s[hh, pl.ds(r0, SC), :]
        v = v_s[hh, pl.ds(r0, SC), :]
        bc = bac_ref[0, hh, pl.ds(r0, SC), :]
        br = bar_ref[0, hh, m]
        a_l = sc_ref[d * H_A + h]
        dtb = sc_ref[2 * H_A + d * H_A + h]
        neg_ea = -jnp.exp(jnp.full((1, 1), a_l, jnp.float32))
        beta = jax.nn.sigmoid(bc[:, d:d + 1])
        g_col = neg_ea * _softplus(bc[:, 2 + d:3 + d] + dtb)
        g_row = neg_ea * _softplus(br[2 + d:3 + d, :] + dtb)
        if d == 0:
            tri, strict = same & (ci <= ri), same & (ci < ri)
        else:
            tri, strict = same & (ci >= ri), same & (ci > ri)
        tri_t = same & (ri <= ci) if d == 0 else same & (ri >= ci)
        gc_col = jnp.sum(jnp.where(tri, g_row, 0.0), axis=1, keepdims=True)
        gc_row = jnp.sum(jnp.where(tri_t, g_col, 0.0), axis=0, keepdims=True)
        gl_col = jnp.sum(jnp.where(same, g_row, 0.0), axis=1, keepdims=True)
        decay = jnp.where(tri, jnp.exp(jnp.where(tri, gc_col - gc_row, 0.0)), 0.0)
        kb = k * beta
        a = jnp.where(strict, _mm_nt(kb, k) * decay, 0.0)
        attn = jnp.where(tri, _mm_nt(q, k) * decay, 0.0)
        eg = jnp.exp(gc_col)
        x = jnp.concatenate([v * beta, kb * eg], axis=1)
        qd = q * eg
        kdt = (k * jnp.exp(gl_col - gc_col)).T
        return dict(r0=r0, a=a, attn=attn, x=x, qd=qd, kdt=kdt, egl=jnp.exp(gl_col))

    def run_chains(ms, states):
        n = len(chains)
        ops = [prep(ms[i], d, hh) for i, (hh, d) in enumerate(chains)]
        ps = [jnp.where(same_base, o["a"], 0.0) for o in ops]
        ts = [eye - p for p in ps]
        for _ in range(BASE.bit_length() - 2):
            ps = [_mm(p, p) for p in ps]
            ts = [t + _mm(t, p) for t, p in zip(ts, ps)]
        inner = same_base
        for outer in merge_masks:
            lows = [_mm(jnp.where(outer & ~inner, o["a"], 0.0), t) for o, t in zip(ops, ts)]
            ts = [t - _mm(t, low) for t, low in zip(ts, lows)]
            inner = outer
        xs = [_mm(t, o["x"]) for t, o in zip(ts, ops)]
        for i in range(n):
            vn_s[i] = jnp.zeros((SC, DV), jnp.float32)
        states = list(states)
        for step in range(CPS):
            cs = [step if d == 0 else CPS - 1 - step for _, d in chains]
            los = [c * CHUNK for c in cs]
            ws_qs = [_mm(jnp.concatenate([x[lo:lo + CHUNK, DV:], o["qd"][lo:lo + CHUNK]], axis=0), s)
                     for x, o, lo, s in zip(xs, ops, los, states)]
            for i in range(n):
                vn_s[i, los[i]:los[i] + CHUNK, :] = xs[i][los[i]:los[i] + CHUNK, :DV] - ws_qs[i][:CHUNK]
            vns = [vn_s[i] for i in range(n)]
            o_cs = [wq[CHUNK:] + _mm(o["attn"][lo:lo + CHUNK, :], vn)
                    for wq, o, lo, vn in zip(ws_qs, ops, los, vns)]
            for i, (hh, _) in enumerate(chains):
                o_s[hh, pl.ds(ops[i]["r0"] + los[i], CHUNK), :] += o_cs[i]
            states = [s * o["egl"][lo:lo + 1, :]
                      + _mm(o["kdt"], jnp.where((rowi >= lo) & (rowi < lo + CHUNK), vn, 0.0))
                      for s, o, lo, vn in zip(states, ops, los, vns)]
        return tuple(states)

    if zero_init:
        states = tuple(jnp.zeros((DK, DV), jnp.float32) for _ in range(2 * hpb))
    else:
        states = tuple(s0_ref[0, d, hh] for hh in range(hpb) for d in range(2))

    chains = [(hh, d) for hh in range(hpb) for d in range(2)]

    def body(m, carry):
        return run_chains([m if d == 0 else n_sc - 1 - m for _, d in chains], carry)

    if n_sc == 1:
        states = body(0, states)
    else:
        states = lax.fori_loop(0, n_sc, body, states)

    for hh in range(hpb):
        for d in range(2):
            st_ref[0, d, hh] = states[2 * hh + d]
        o = o_s[hh]
        o = o * lax.rsqrt(jnp.mean(o * o, axis=-1, keepdims=True) + EPS) * og_ref[...]
        zz = z_ref[:, hh * DV:(hh + 1) * DV]
        o_ref[:, hh * DV:(hh + 1) * DV] = o * (zz * jax.nn.sigmoid(zz))


def _delta_call(qkv, z, ba, conv_w, a_log, dt_bias, onorm_g, s0, B, L, row_blk0):
    n_sc = L // SC
    t0 = row_blk0 * L
    bah = ba[t0:t0 + B * L, :4 * H_A].reshape(B, L, 4, H_A).transpose(0, 3, 1, 2)
    bar = bah.reshape(B, H_A, n_sc, SC, 4).transpose(0, 1, 2, 4, 3)
    scal = jnp.concatenate([a_log.reshape(-1), dt_bias.reshape(-1)]).astype(jnp.float32)
    hpb = max(1, min(H_A, DELTA_HEAD_ROWS // L))
    n_hb = H_A // hpb
    zero_init = s0 is None
    if zero_init:
        s0 = jnp.zeros((1, 2, hpb, DK, DV), jnp.float32)
        s0_spec = pl.BlockSpec((1, 2, hpb, DK, DV), lambda b, h, sc: (0, 0, 0, 0, 0))
    else:
        s0_spec = pl.BlockSpec((1, 2, hpb, DK, DV), lambda b, h, sc: (b, 0, h, 0, 0))

    def col(off):
        return pl.BlockSpec((L, hpb * DK), lambda b, h, sc: (row_blk0 + b, off * n_hb + h))

    def cw(off):
        return pl.BlockSpec((CONV_K, hpb * DK), lambda b, h, sc: (0, off * n_hb + h))

    kern = functools.partial(_delta_kernel, n_sc=n_sc, zero_init=zero_init, hpb=hpb)
    return pl.pallas_call(
        kern,
        grid_spec=pltpu.PrefetchScalarGridSpec(
            num_scalar_prefetch=1,
            grid=(B, n_hb),
            in_specs=[col(0), col(1), col(2),
                      pl.BlockSpec((L, hpb * DV), lambda b, h, sc: (row_blk0 + b, h)),
                      pl.BlockSpec((1, hpb, L, 4), lambda b, h, sc: (b, h, 0, 0)),
                      pl.BlockSpec((1, hpb, n_sc, 4, SC), lambda b, h, sc: (b, h, 0, 0, 0)),
                      cw(0), cw(1), cw(2),
                      pl.BlockSpec((1, DV), lambda b, h, sc: (0, 0)),
                      s0_spec],
            out_specs=[pl.BlockSpec((L, hpb * DV), lambda b, h, sc: (b, h)),
                       pl.BlockSpec((1, 2, hpb, DK, DV), lambda b, h, sc: (b, 0, h, 0, 0))],
            scratch_shapes=[pltpu.VMEM((hpb, L, DK), jnp.float32), pltpu.VMEM((hpb, L, DK), jnp.float32),
                            pltpu.VMEM((hpb, L, DV), jnp.float32), pltpu.VMEM((hpb, L, DV), jnp.float32),
                            pltpu.VMEM((2 * hpb, SC, DV), jnp.float32)]),
        out_shape=[jax.ShapeDtypeStruct((B * L, D_A), jnp.float32),
                   jax.ShapeDtypeStruct((B, 2, H_A, DK, DV), jnp.float32)],
        compiler_params=pltpu.CompilerParams(dimension_semantics=("arbitrary", "arbitrary"),
                                             vmem_limit_bytes=VMEM_LIMIT),
    )(scal, qkv, qkv, qkv, z, bah, bar, conv_w, conv_w, conv_w, onorm_g.reshape(1, DV), s0)


TR = 256
GSZ = N_EXPERTS // N_GROUPS
NEG = -jnp.inf


def _col_to_row(col, eye_mask):
    return jnp.sum(jnp.where(eye_mask, col, jnp.zeros_like(col)), axis=0, keepdims=True)


def _route_kernel(h_ref, rwh_ref, rwl_ref, rb_ref, idx_ref, rank_ref, w_ref, cnt_ref, cnt_s):
    i = pl.program_id(0)

    @pl.when(i == 0)
    def _():
        cnt_s[...] = jnp.zeros_like(cnt_s)

    h = h_ref[...]
    hh, hl = _split_bf16(h)
    logits = _bdot(hh, rwh_ref[...]) + (_bdot(hh, rwl_ref[...]) + _bdot(hl, rwh_ref[...]))
    scores = jax.nn.sigmoid(logits)
    sel = scores + rb_ref[...]
    lane = lax.broadcasted_iota(jnp.int32, sel.shape, 1)
    gid = lane // GSZ

    def first_argmax(v):
        m = jnp.max(v, axis=1, keepdims=True)
        first = jnp.min(jnp.where(v == m, lane, N_EXPERTS), axis=1, keepdims=True)
        return m, first

    gscore = []
    for g in range(N_GROUPS):
        vg = jnp.where(gid == g, sel, NEG)
        m1, i1 = first_argmax(vg)
        m2 = jnp.max(jnp.where(lane == i1, NEG, vg), axis=1, keepdims=True)
        gscore.append(m1 + m2)
    emask = jnp.zeros(sel.shape, jnp.bool_)
    for g in range(N_GROUPS):
        beat = jnp.zeros(gscore[g].shape, jnp.int32)
        for o in range(N_GROUPS):
            if o == g:
                continue
            wins = (gscore[o] > gscore[g]) | ((gscore[o] == gscore[g]) & (o < g))
            beat = beat + wins.astype(jnp.int32)
        emask = emask | ((gid == g) & (beat < TOPK_GROUP))
    cand = jnp.where(emask, sel, NEG)
    chosen = []
    picked = jnp.zeros(sel.shape, jnp.bool_)
    for _ in range(TOP_K):
        _, ik = first_argmax(cand)
        hit = lane == ik
        chosen.append((ik, hit))
        picked = picked | hit
        cand = jnp.where(hit, NEG, cand)
    wraw = jnp.where(picked, scores, 0.0)
    wmat = wraw / jnp.sum(wraw, axis=1, keepdims=True) * ROUTED_SCALE

    pm = picked.astype(jnp.bfloat16)
    ri = lax.broadcasted_iota(jnp.int32, (TR, TR), 0)
    ci = lax.broadcasted_iota(jnp.int32, (TR, TR), 1)
    earlier = (ci < ri).astype(jnp.bfloat16)
    rank_mat = _bdot(earlier, pm) + cnt_s[...]
    cnt_s[...] = cnt_s[...] + jnp.sum(picked.astype(jnp.float32), axis=0, keepdims=True)
    cnt_ref[...] = cnt_s[...].astype(jnp.int32)

    eye = ri == ci
    lane8 = lax.broadcasted_iota(jnp.int32, (TR, TOP_K), 1)
    wcols = jnp.zeros((TR, TOP_K), jnp.float32)
    for k, (ik, hit) in enumerate(chosen):
        rk = jnp.sum(jnp.where(hit, rank_mat, 0.0), axis=1, keepdims=True)
        wk = jnp.sum(jnp.where(hit, wmat, 0.0), axis=1, keepdims=True)
        idx_ref[0, k:k + 1, :] = _col_to_row(ik, eye)
        rank_ref[0, k:k + 1, :] = _col_to_row(rk, eye).astype(jnp.int32)
        wcols = jnp.where(lane8 == k, wk, wcols)
    w_ref[...] = wcols


def _route_call(hf, router_w, router_bias):
    T, D = hf.shape
    n_tiles = T // TR
    rwh, rwl = _split_bf16(router_w)
    return pl.pallas_call(
        _route_kernel,
        grid=(n_tiles,),
        in_specs=[pl.BlockSpec((TR, D), lambda i: (i, 0)),
                  pl.BlockSpec((D, N_EXPERTS), lambda i: (0, 0)),
                  pl.BlockSpec((D, N_EXPERTS), lambda i: (0, 0)),
                  pl.BlockSpec((1, N_EXPERTS), lambda i: (0, 0))],
        out_specs=[pl.BlockSpec((1, TOP_K, TR), lambda i: (i, 0, 0)),
                   pl.BlockSpec((1, TOP_K, TR), lambda i: (i, 0, 0)),
                   pl.BlockSpec((TR, TOP_K), lambda i: (i, 0)),
                   pl.BlockSpec((1, N_EXPERTS), lambda i: (0, 0))],
        scratch_shapes=[pltpu.VMEM((1, N_EXPERTS), jnp.float32)],
        out_shape=[jax.ShapeDtypeStruct((n_tiles, TOP_K, TR), jnp.int32),
                   jax.ShapeDtypeStruct((n_tiles, TOP_K, TR), jnp.int32),
                   jax.ShapeDtypeStruct((T, TOP_K), jnp.float32),
                   jax.ShapeDtypeStruct((1, N_EXPERTS), jnp.int32)],
        compiler_params=pltpu.CompilerParams(dimension_semantics=("arbitrary",)),
    )(hf, rwh, rwl, router_bias.reshape(1, N_EXPERTS).astype(jnp.float32))


def _dispatch_kernel(idx_ref, rank_ref, pstart_ref, h_ref, xs_init, pos_ref, xs_hbm, pos_v, pos_s, ssem, psem):
    del xs_init
    erow = lax.broadcasted_iota(jnp.int32, (N_EXPERTS, TR), 0)
    pstart = pstart_ref[...]
    for k in range(TOP_K):
        hit = erow == idx_ref[0, k:k + 1, :]
        seg = jnp.sum(jnp.where(hit, pstart, 0), axis=0, keepdims=True)
        pos_v[k:k + 1, :] = seg + rank_ref[0, k:k + 1, :]
    pos_ref[0] = pos_v[...]
    cp = pltpu.make_async_copy(pos_v, pos_s, psem)
    cp.start()
    cp.wait()

    def body(t, carry):
        for k in range(TOP_K):
            pltpu.make_async_copy(h_ref.at[t], xs_hbm.at[pos_s[k, t]], ssem).start()
        return carry

    lax.fori_loop(0, TR, body, 0, unroll=8)
    n_rows = TR * TOP_K
    pltpu.make_async_copy(xs_hbm.at[pl.ds(0, n_rows)], xs_hbm.at[pl.ds(0, n_rows)], ssem).wait()


def _dispatch_call(hf, idx, rank, pad_start, n_pad):
    T, D = hf.shape
    n_tiles = T // TR
    return pl.pallas_call(
        _dispatch_kernel,
        grid=(n_tiles,),
        in_specs=[pl.BlockSpec((1, TOP_K, TR), lambda i: (i, 0, 0)),
                  pl.BlockSpec((1, TOP_K, TR), lambda i: (i, 0, 0)),
                  pl.BlockSpec((N_EXPERTS, 1), lambda i: (0, 0)),
                  pl.BlockSpec((TR, D), lambda i: (i, 0)),
                  pl.BlockSpec(memory_space=pl.ANY)],
        out_specs=[pl.BlockSpec((1, TOP_K, TR), lambda i: (i, 0, 0)),
                   pl.BlockSpec(memory_space=pl.ANY)],
        scratch_shapes=[pltpu.VMEM((TOP_K, TR), jnp.int32), pltpu.SMEM((TOP_K, TR), jnp.int32),
                        pltpu.SemaphoreType.DMA, pltpu.SemaphoreType.DMA],
        out_shape=[jax.ShapeDtypeStruct((n_tiles, TOP_K, TR), jnp.int32),
                   jax.ShapeDtypeStruct((n_pad, D), hf.dtype)],
        input_output_aliases={4: 1},
        compiler_params=pltpu.CompilerParams(dimension_semantics=("arbitrary",)),
    )(idx, rank, pad_start.reshape(N_EXPERTS, 1), hf, jnp.zeros((n_pad, D), hf.dtype))


BM = 256


def _expert_kernel(blk_e_ref, nvalid_ref, nused_ref, x_ref, wg_ref, wu_ref, wd_ref, y_ref, wg_s, wu_s, wd_s):
    i = pl.program_id(0)

    @pl.when(i < nused_ref[0])
    def _():
        e = blk_e_ref[i]
        prev = blk_e_ref[jnp.maximum(i - 1, 0)]

        @pl.when((i == 0) | (e != prev))
        def _():
            wg_s[...] = wg_ref[0].astype(jnp.bfloat16)
            wu_s[...] = wu_ref[0].astype(jnp.bfloat16)
            wd_s[...] = wd_ref[0].astype(jnp.bfloat16)

        row = lax.broadcasted_iota(jnp.int32, (BM, 1), 0)
        x = jnp.where(row < nvalid_ref[i], x_ref[...], 0.0).astype(jnp.bfloat16)
        g = _bdot(x, wg_s[...])
        u = _bdot(x, wu_s[...])
        a = (g * jax.nn.sigmoid(g)) * u
        y_ref[...] = _bdot(a.astype(jnp.bfloat16), wd_s[...])

    @pl.when(i >= nused_ref[0])
    def _():
        y_ref[...] = jnp.zeros_like(y_ref)


def _expert_call(x_sorted, blk_e, n_valid, n_used, w_gate, w_up, w_down):
    n_pad, D = x_sorted.shape
    n_blk = n_pad // BM
    E, _, F = w_gate.shape

    def row_map(i, be, nv, nu):
        return (jnp.minimum(i, nu[0] - 1), 0)

    def w_map(i, be, nv, nu):
        return (be[jnp.minimum(i, nu[0] - 1)], 0, 0)

    return pl.pallas_call(
        _expert_kernel,
        grid_spec=pltpu.PrefetchScalarGridSpec(
            num_scalar_prefetch=3,
            grid=(n_blk,),
            in_specs=[pl.BlockSpec((BM, D), row_map),
                      pl.BlockSpec((1, D, F), w_map),
                      pl.BlockSpec((1, D, F), w_map),
                      pl.BlockSpec((1, F, D), w_map)],
            out_specs=pl.BlockSpec((BM, D), lambda i, be, nv, nu: (i, 0)),
            scratch_shapes=[pltpu.VMEM((D, F), jnp.bfloat16), pltpu.VMEM((D, F), jnp.bfloat16),
                            pltpu.VMEM((F, D), jnp.bfloat16)]),
        out_shape=jax.ShapeDtypeStruct((n_pad, D), jnp.float32),
        compiler_params=pltpu.CompilerParams(dimension_semantics=("arbitrary",),
                                             vmem_limit_bytes=VMEM_LIMIT),
    )(blk_e, n_valid, n_used, x_sorted, w_gate, w_up, w_down)


TC = 64


def _combine_kernel(pos_hbm, y_hbm, w_ref, x1_ref, sh_ref, mod_ref, fg_ref, out_ref, ybuf, pos_s, gsem, psem):
    j = pl.program_id(0)
    last = pl.num_programs(0) - 1

    def pos_copy(b, slot):
        return pltpu.make_async_copy(pos_hbm.at[b], pos_s.at[slot], psem.at[slot])

    def start_gather(slot):
        for k in range(TOP_K):
            for t in range(TC):
                pltpu.make_async_copy(y_hbm.at[pos_s[slot, k * TC + t]], ybuf.at[slot, k, t],
                                      gsem.at[slot]).start()

    def wait_gather(slot):
        pltpu.make_async_copy(ybuf.at[slot], ybuf.at[slot], gsem.at[slot]).wait()

    @pl.when(j == 0)
    def _():
        pos_copy(0, 0).start()
        pos_copy(0, 0).wait()
        start_gather(0)
        pos_copy(jnp.minimum(1, last), 1).start()

    slot = j % 2
    nslot = 1 - slot
    pos_copy(0, nslot).wait()
    start_gather(nslot)
    pos_copy(jnp.minimum(j + 2, last), slot).start()
    wait_gather(slot)
    w = w_ref[...]
    acc = w[:, 0:1] * ybuf[slot, 0]
    for k in range(1, TOP_K):
        acc = acc + w[:, k:k + 1] * ybuf[slot, k]
    x2 = x1_ref[...] + mod_ref[0, 5:6, :] * (acc + sh_ref[...])
    out_ref[...] = x2 * lax.rsqrt(jnp.mean(x2 * x2, axis=-1, keepdims=True) + EPS) * fg_ref[...]

    @pl.when(j == last)
    def _():
        wait_gather(nslot)
        pos_copy(0, slot).wait()


def _combine_call(y_sorted, pos_t, wts, x1, shared, mod, final_g, n_ctx, lat_len):
    T, K = wts.shape
    D = y_sorted.shape[1]
    n_tiles = T // TC
    row = functools.partial(_mod_row, tokens_per_tile=TC, n_ctx=n_ctx, lat_len=lat_len)
    return pl.pallas_call(
        _combine_kernel,
        grid=(n_tiles,),
        in_specs=[pl.BlockSpec(memory_space=pl.ANY),
                  pl.BlockSpec(memory_space=pl.ANY),
                  pl.BlockSpec((TC, K), lambda j: (j, 0)),
                  pl.BlockSpec((TC, D), lambda j: (j, 0)),
                  pl.BlockSpec((TC, D), lambda j: (j, 0)),
                  pl.BlockSpec((1, N_MOD, D), lambda j: (row(j), 0, 0)),
                  pl.BlockSpec((1, D), lambda j: (0, 0))],
        out_specs=pl.BlockSpec((TC, D), lambda j: (j, 0)),
        scratch_shapes=[pltpu.VMEM((2, K, TC, D), jnp.float32),
                        pltpu.SMEM((2, K * TC), jnp.int32),
                        pltpu.SemaphoreType.DMA((2,)), pltpu.SemaphoreType.DMA((2,))],
        out_shape=jax.ShapeDtypeStruct((T, D), jnp.float32),
        compiler_params=pltpu.CompilerParams(dimension_semantics=("arbitrary",)),
    )(pos_t, y_sorted, wts, x1, shared, mod, final_g.reshape(1, D))


def _moe_routed(h2, router_w, router_bias, w_gate, w_up, w_down):
    T, D = h2.shape
    idx, rank, wts, cnt = _route_call(h2, router_w, router_bias)
    counts = cnt[0]
    padded = (counts + BM - 1) // BM * BM
    pad_end = jnp.cumsum(padded)
    pad_start = (pad_end - padded).astype(jnp.int32)
    n_pad = T * TOP_K + N_EXPERTS * BM
    n_blk = n_pad // BM
    pos, x_sorted = _dispatch_call(h2, idx, rank, pad_start, n_pad)
    blk_row0 = jnp.arange(n_blk, dtype=jnp.int32) * BM
    blk_e = jnp.minimum(jnp.sum((pad_end[None, :] <= blk_row0[:, None]).astype(jnp.int32), axis=1), N_EXPERTS - 1)
    own = blk_e[:, None] == jnp.arange(N_EXPERTS, dtype=jnp.int32)[None, :]
    seg_end = jnp.sum(jnp.where(own, (pad_start + counts)[None, :], 0), axis=1)
    n_valid = jnp.clip(seg_end - blk_row0, 0, BM).astype(jnp.int32)
    n_used = (pad_end[-1] // BM).astype(jnp.int32).reshape(1)
    y = _expert_call(x_sorted, blk_e, n_valid, n_used, w_gate, w_up, w_down)
    pos_t = pos.reshape(T // TR, TOP_K, TR // TC, TC).transpose(0, 2, 1, 3).reshape(T // TC, TOP_K * TC)
    return y, pos_t, wts


def kernel(x_prompt, x_sample, state_delta, c, c_ctx, w_ada, b_ada, norm1_g, w_in, conv_w, a_log,
           dt_bias, onorm_g, pool_w, pool_scale, w_out, norm2_g, router_w, router_bias, exp_w_gate,
           exp_w_up, exp_w_down, sh_w_gate, sh_w_up, sh_w_down, final_g):
    Bc, Lc, D = x_prompt.shape
    Bl, Ll, _ = x_sample.shape
    n_ctx = Bc * Lc
    assert DEPTH == 1 and 1 + Bl <= MOD_ROWS and n_ctx % Ll == 0
    x = jnp.concatenate([x_prompt.reshape(n_ctx, D), x_sample.reshape(Bl * Ll, D)], axis=0)
    cvec = jnp.concatenate([c_ctx[None], c, jnp.zeros((MOD_ROWS - 1 - Bl, D), c.dtype)], axis=0)
    l = 0
    mod = _ada_call(cvec, w_ada[l], b_ada[l]).reshape(MOD_ROWS, N_MOD, D)
    qkv, z, ba, u = _inproj_call(x, mod, norm1_g[l], w_in[l], n_ctx, Ll)
    dn = (conv_w[l], a_log[l], dt_bias[l], onorm_g[l])
    oa_c, st_ctx = _delta_call(qkv, z, ba, *dn, None, Bc, Lc, 0)
    oa_l, _ = _delta_call(qkv, z, ba, *dn, state_delta[:, l], Bl, Ll, n_ctx // Ll)
    op_c = _pool_call(u[:n_ctx].reshape(Bc, Lc, D_P), pool_w[l], pool_scale[l], False)
    op_l = _pool_call(u[n_ctx:].reshape(Bl, Ll, D_P), pool_w[l], pool_scale[l], True)
    o_a = jnp.concatenate([oa_c, oa_l], axis=0)
    o_p = jnp.concatenate([op_c.reshape(n_ctx, D_P), op_l.reshape(Bl * Ll, D_P)], axis=0)
    x1, h2, shared = _outproj_call(x, o_a, o_p, mod, norm2_g[l], w_out[l], sh_w_gate[l], sh_w_up[l],
                                   sh_w_down[l], n_ctx, Ll)
    y, pos_t, wts = _moe_routed(h2, router_w[l], router_bias[l], exp_w_gate[l], exp_w_up[l], exp_w_down[l])
    out = _combine_call(y, pos_t, wts, x1, shared, mod, final_g, n_ctx, Ll)
    y_prompt = out[:n_ctx].reshape(Bc, Lc, D)
    y_sample = out[n_ctx:].reshape(Bl, Ll, D)
    new_state_delta = st_ctx[:, None].astype(x_prompt.dtype)
    return (y_prompt, y_sample, new_state_delta)
```

```python
import functools
import jax, jax.numpy as jnp
from jax import lax
from jax.experimental import pallas as pl
from jax.experimental.pallas import tpu as pltpu

D_MODEL = 1024
DEPTH = 1
GRID_W = 64
D_MIX = D_MODEL
D_A = D_MIX // 2
D_P = D_MIX - D_A
H_A = 4
DK = D_A // H_A
DV = D_A // H_A
CONV_K = 5
CHUNK = 64
POOL_WINDOWS = (2, 4, 8, 16)
N_PG = len(POOL_WINDOWS)
PG = D_P // N_PG
N_EXPERTS = 256
TOP_K = 8
N_GROUPS = 8
TOPK_GROUP = 4
ROUTED_SCALE = 2.5
EPS = 1e-6
VMEM_LIMIT = 48 * 1024 * 1024


def _split_bf16(a):
    hi = a.astype(jnp.bfloat16)
    return hi, (a - hi.astype(jnp.float32)).astype(jnp.bfloat16)


def _bdot(a, b):
    return jnp.dot(a, b, preferred_element_type=jnp.float32)


def _pack_rows(x):
    m = x.shape[1] // 2
    hi = lax.bitcast_convert_type(x[:, :m].astype(jnp.bfloat16).astype(jnp.float32), jnp.uint32)
    lo = lax.bitcast_convert_type(x[:, m:].astype(jnp.bfloat16).astype(jnp.float32), jnp.uint32)
    return hi | (lo >> 16)


def _unpack_rows(p):
    hi = lax.bitcast_convert_type(p & jnp.uint32(0xFFFF0000), jnp.float32)
    lo = lax.bitcast_convert_type(p << 16, jnp.float32)
    return hi, lo


N_MOD = 6
MOD_ROWS = 8
TM = 512


def _ada_kernel(c_ref, w_ref, b_ref, o_ref):
    c = c_ref[...]
    s = c * jax.nn.sigmoid(c)
    sh, sl = _split_bf16(s)
    wh, wl = _split_bf16(w_ref[...])
    o_ref[...] = _bdot(sh, wh) + (_bdot(sh, wl) + _bdot(sl, wh)) + b_ref[...]


def _ada_call(cvec, w_ada, b_ada):
    R, D = cvec.shape
    N = w_ada.shape[1]
    tn = 1024
    return pl.pallas_call(
        _ada_kernel,
        grid=(N // tn,),
        in_specs=[pl.BlockSpec((R, D), lambda j: (0, 0)),
                  pl.BlockSpec((D, tn), lambda j: (0, j)),
                  pl.BlockSpec((1, tn), lambda j: (0, j))],
        out_specs=pl.BlockSpec((R, tn), lambda j: (0, j)),
        out_shape=jax.ShapeDtypeStruct((R, N), jnp.float32),
    )(cvec, w_ada, b_ada.reshape(1, N))


def _mod_row(tile, tokens_per_tile, n_ctx, lat_len):
    t0 = tile * tokens_per_tile
    return jnp.where(t0 < n_ctx, 0, 1 + (t0 - n_ctx) // lat_len)


def _inproj_kernel(x_ref, mod_ref, g_ref, wq_ref, wz_ref, wb_ref, wu_ref, q_ref, z_ref, b_ref, u_ref):
    x = x_ref[...]
    y = x * lax.rsqrt(jnp.mean(x * x, axis=-1, keepdims=True) + EPS) * g_ref[...]
    h = (y * (1.0 + mod_ref[0, 1:2, :]) + mod_ref[0, 0:1, :]).astype(jnp.bfloat16)
    q_ref[...] = _bdot(h, wq_ref[...])
    z_ref[...] = _bdot(h, wz_ref[...])
    b_ref[...] = _bdot(h, wb_ref[...])
    u_ref[...] = _bdot(h, wu_ref[...])


def _inproj_call(x, mod, norm1_g, w_in, n_ctx, lat_len):
    T, D = x.shape
    bf = jnp.bfloat16
    nq, nz, nb = 3 * D_A, D_A, 4 * H_A
    wq = w_in[:, :nq].astype(bf)
    wz = w_in[:, nq:nq + nz].astype(bf)
    wb = jnp.pad(w_in[:, nq + nz:nq + nz + nb], ((0, 0), (0, 128 - nb))).astype(bf)
    wu = w_in[:, nq + nz + nb:].astype(bf)
    row = functools.partial(_mod_row, tokens_per_tile=TM, n_ctx=n_ctx, lat_len=lat_len)

    def full(a):
        return pl.BlockSpec(a.shape, lambda i: (0, 0))

    def rows(n):
        return pl.BlockSpec((TM, n), lambda i: (i, 0))

    return pl.pallas_call(
        _inproj_kernel,
        grid=(T // TM,),
        in_specs=[rows(D), pl.BlockSpec((1, N_MOD, D), lambda i: (row(i), 0, 0)),
                  pl.BlockSpec((1, D), lambda i: (0, 0)), full(wq), full(wz), full(wb), full(wu)],
        out_specs=[rows(nq), rows(nz), rows(128), rows(D_P)],
        out_shape=[jax.ShapeDtypeStruct((T, nq), jnp.float32), jax.ShapeDtypeStruct((T, nz), jnp.float32),
                   jax.ShapeDtypeStruct((T, 128), jnp.float32), jax.ShapeDtypeStruct((T, D_P), jnp.float32)],
        compiler_params=pltpu.CompilerParams(dimension_semantics=("arbitrary",),
                                             vmem_limit_bytes=VMEM_LIMIT),
    )(x, mod, norm1_g.reshape(1, D), wq, wz, wb, wu)


PT = 256


def _window_bounds(pos, w, n):
    return jnp.maximum(pos - w // 2, 0), jnp.minimum(pos + w - w // 2, n)


def _band_sum(band, x):
    xh, xl = _split_bf16(x)
    return _bdot(band, xh) + _bdot(band, xl)


def _pool_seq_kernel(u_ref, pw_ref, ps_ref, o_ref):
    L = u_ref.shape[1]
    ti = lax.broadcasted_iota(jnp.int32, (L, L), 0)
    ji = lax.broadcasted_iota(jnp.int32, (L, L), 1)
    tcol = lax.broadcasted_iota(jnp.int32, (L, 1), 0)
    for i, w in enumerate(POOL_WINDOWS):
        lo, hi = _window_bounds(ti, w, L)
        band = ((ji >= lo) & (ji < hi)).astype(jnp.bfloat16)
        clo, chi = _window_bounds(tcol, w, L)
        ug = u_ref[0, :, i * PG:(i + 1) * PG]
        mean = _band_sum(band, ug) / (chi - clo).astype(jnp.float32)
        d = (mean - ug).astype(jnp.bfloat16)
        o_ref[0, :, i * PG:(i + 1) * PG] = _bdot(d, pw_ref[i]) * ps_ref[:, i * PG:(i + 1) * PG]


def _pool_grid_kernel(u_ref, pw_ref, ps_ref, o_ref, pad_s, r_s):
    L = u_ref.shape[1]
    rows = L // GRID_W
    halo = (max(POOL_WINDOWS) // 2) * GRID_W
    pad_s[0:halo, :] = jnp.zeros((halo, D_P), jnp.float32)
    pad_s[halo + L:, :] = jnp.zeros((halo, D_P), jnp.float32)
    pad_s[halo:halo + L, :] = u_ref[0]
    ti = lax.broadcasted_iota(jnp.int32, (PT, PT), 0)
    ji = lax.broadcasted_iota(jnp.int32, (PT, PT), 1)
    tcol = lax.broadcasted_iota(jnp.int32, (PT, 1), 0)
    for i, w in enumerate(POOL_WINDOWS):
        cs = slice(i * PG, (i + 1) * PG)
        acc = None
        for dr in range(-(w // 2), w - w // 2):
            part = pad_s[halo + dr * GRID_W:halo + dr * GRID_W + L, cs]
            acc = part if acc is None else acc + part
        r_s[...] = acc
        lo, hi = _window_bounds(ti % GRID_W, w, GRID_W)
        band = ((ji // GRID_W == ti // GRID_W) & (ji % GRID_W >= lo) & (ji % GRID_W < hi)).astype(jnp.bfloat16)
        clo, chi = _window_bounds(tcol % GRID_W, w, GRID_W)
        ccnt = (chi - clo).astype(jnp.float32)
        for tile in range(L // PT):
            ts = slice(tile * PT, (tile + 1) * PT)
            rlo, rhi = _window_bounds(tile * (PT // GRID_W) + tcol // GRID_W, w, rows)
            mean = _band_sum(band, r_s[ts, :]) / ((rhi - rlo).astype(jnp.float32) * ccnt)
            d = (mean - u_ref[0, ts, cs]).astype(jnp.bfloat16)
            o_ref[0, ts, cs] = _bdot(d, pw_ref[i]) * ps_ref[:, cs]


def _pool_call(u, pool_w, pool_scale, grid):
    B, L, _ = u.shape
    pw = pool_w.astype(jnp.bfloat16)
    ps = pool_scale.reshape(1, D_P)
    specs = dict(
        grid=(B,),
        in_specs=[pl.BlockSpec((1, L, D_P), lambda b: (b, 0, 0)),
                  pl.BlockSpec((N_PG, PG, PG), lambda b: (0, 0, 0)),
                  pl.BlockSpec((1, D_P), lambda b: (0, 0))],
        out_specs=pl.BlockSpec((1, L, D_P), lambda b: (b, 0, 0)),
        out_shape=jax.ShapeDtypeStruct((B, L, D_P), jnp.float32),
        compiler_params=pltpu.CompilerParams(dimension_semantics=("arbitrary",),
                                             vmem_limit_bytes=VMEM_LIMIT))
    if not grid:
        return pl.pallas_call(_pool_seq_kernel, **specs)(u, pw, ps)
    halo = (max(POOL_WINDOWS) // 2) * GRID_W
    return pl.pallas_call(
        _pool_grid_kernel,
        scratch_shapes=[pltpu.VMEM((L + 2 * halo, D_P), jnp.float32), pltpu.VMEM((L, PG), jnp.float32)],
        **specs)(u, pw, ps)


def _outproj_kernel(x_ref, oa_ref, op_ref, mod_ref, g2_ref, wo_ref, sg_ref, su_ref, sd_ref,
                    x1_ref, h2_ref, h2p_ref, sh_ref):
    mix = (_bdot(oa_ref[...].astype(jnp.bfloat16), wo_ref[:D_A, :])
           + _bdot(op_ref[...].astype(jnp.bfloat16), wo_ref[D_A:, :]))
    x1 = x_ref[...] + mod_ref[0, 2:3, :] * mix
    x1_ref[...] = x1
    y = x1 * lax.rsqrt(jnp.mean(x1 * x1, axis=-1, keepdims=True) + EPS) * g2_ref[...]
    h2 = y * (1.0 + mod_ref[0, 4:5, :]) + mod_ref[0, 3:4, :]
    h2_ref[...] = h2
    h2p_ref[...] = _pack_rows(h2)
    hb = h2.astype(jnp.bfloat16)
    g = _bdot(hb, sg_ref[...])
    a = (g * jax.nn.sigmoid(g)) * _bdot(hb, su_ref[...])
    sh_ref[...] = _bdot(a.astype(jnp.bfloat16), sd_ref[...])


def _outproj_call(x, o_a, o_p, mod, norm2_g, w_out, sh_gate, sh_up, sh_down, n_ctx, lat_len):
    T, D = x.shape
    bf = jnp.bfloat16
    row = functools.partial(_mod_row, tokens_per_tile=TM, n_ctx=n_ctx, lat_len=lat_len)
    ws = [w_out.astype(bf), sh_gate.astype(bf), sh_up.astype(bf), sh_down.astype(bf)]

    def rows(n):
        return pl.BlockSpec((TM, n), lambda i: (i, 0))

    return pl.pallas_call(
        _outproj_kernel,
        grid=(T // TM,),
        in_specs=[rows(D), rows(D_A), rows(D_P), pl.BlockSpec((1, N_MOD, D), lambda i: (row(i), 0, 0)),
                  pl.BlockSpec((1, D), lambda i: (0, 0))] + [pl.BlockSpec(w.shape, lambda i: (0, 0)) for w in ws],
        out_specs=[rows(D), rows(D), rows(D // 2), rows(D)],
        out_shape=[jax.ShapeDtypeStruct((T, D), jnp.float32), jax.ShapeDtypeStruct((T, D), jnp.float32),
                   jax.ShapeDtypeStruct((T, D // 2), jnp.uint32), jax.ShapeDtypeStruct((T, D), jnp.float32)],
        compiler_params=pltpu.CompilerParams(dimension_semantics=("arbitrary",),
                                             vmem_limit_bytes=VMEM_LIMIT),
    )(x, o_a, o_p, mod, norm2_g.reshape(1, D), *ws)


SC = 256
CPS = SC // CHUNK
BASE = 16
DELTA_HEAD_ROWS = 4096


def _mm(a, b):
    return jnp.dot(a.astype(jnp.bfloat16), b.astype(jnp.bfloat16), preferred_element_type=jnp.float32)


def _mm_nt(a, b):
    return lax.dot_general(a.astype(jnp.bfloat16), b.astype(jnp.bfloat16), (((1,), (1,)), ((), ())),
                           preferred_element_type=jnp.float32)


def _softplus(x):
    return jnp.maximum(x, 0.0) + jnp.log(1.0 + jnp.exp(-jnp.abs(x)))


def _delta_kernel(sc_ref, xq_ref, xk_ref, xv_ref, z_ref, bac_ref, bar_ref, cwq_ref, cwk_ref, cwv_ref,
                  og_ref, s0_ref, o_ref, st_ref, q_s, k_s, v_s, o_s, vn_s, *, n_sc, zero_init, hpb):
    hb = pl.program_id(1)
    L = q_s.shape[1]

    def conv(x_ref, w_ref, cs):
        x = x_ref[:, cs]
        row = lax.broadcasted_iota(jnp.int32, x.shape, 0)
        acc = x * w_ref[CONV_K // 2:CONV_K // 2 + 1, cs]
        for j in range(CONV_K):
            d = j - CONV_K // 2
            if d == 0:
                continue
            xs = pltpu.roll(x, (-d) % L, 0)
            ok = (row + d >= 0) & (row + d < L)
            acc = acc + jnp.where(ok, xs, 0.0) * w_ref[j:j + 1, cs]
        return acc * jax.nn.sigmoid(acc)

    for hh in range(hpb):
        cs = slice(hh * DK, (hh + 1) * DK)
        q = conv(xq_ref, cwq_ref, cs)
        q_s[hh] = q * lax.rsqrt(jnp.sum(q * q, axis=-1, keepdims=True) + EPS) * (DK ** -0.5)
        k = conv(xk_ref, cwk_ref, cs)
        k_s[hh] = k * lax.rsqrt(jnp.sum(k * k, axis=-1, keepdims=True) + EPS)
        v_s[hh] = conv(xv_ref, cwv_ref, cs)
    o_s[...] = jnp.zeros_like(o_s)

    ri = lax.broadcasted_iota(jnp.int32, (SC, SC), 0)
    ci = lax.broadcasted_iota(jnp.int32, (SC, SC), 1)
    same = (ri // CHUNK) == (ci // CHUNK)
    same_base = (ri // BASE) == (ci // BASE)
    merge_masks = [(ri // w) == (ci // w) for w in (2 * BASE, CHUNK)]
    eye = (ri == ci).astype(jnp.float32)
    rowi = lax.broadcasted_iota(jnp.int32, (SC, DV), 0)

    def prep(m, d, hh):
        r0 = pl.multiple_of(m * SC, SC)
        h = hb * hpb + hh
        q = q_s[hh, pl.ds(r0, SC), :]
        k = k_s[hh, pl.ds(r0, SC), :]
        v = v_s[hh, pl.ds(r0, SC), :]
        bc = bac_ref[0, hh, pl.ds(r0, SC), :]
        br = bar_ref[0, hh, m]
        a_l = sc_ref[d * H_A + h]
        dtb = sc_ref[2 * H_A + d * H_A + h]
        neg_ea = -jnp.exp(jnp.full((1, 1), a_l, jnp.float32))
        beta = jax.nn.sigmoid(bc[:, d:d + 1])
        g_col = neg_ea * _softplus(bc[:, 2 + d:3 + d] + dtb)
        g_row = neg_ea * _softplus(br[2 + d:3 + d, :] + dtb)
        if d == 0:
            tri, strict = same & (ci <= ri), same & (ci < ri)
        else:
            tri, strict = same & (ci >= ri), same & (ci > ri)
        tri_t = same & (ri <= ci) if d == 0 else same & (ri >= ci)
        gc_col = jnp.sum(jnp.where(tri, g_row, 0.0), axis=1, keepdims=True)
        gc_row = jnp.sum(jnp.where(tri_t, g_col, 0.0), axis=0, keepdims=True)
        gl_col = jnp.sum(jnp.where(same, g_row, 0.0), axis=1, keepdims=True)
        decay = jnp.where(tri, jnp.exp(jnp.where(tri, gc_col - gc_row, 0.0)), 0.0)
        kb = k * beta
        a = jnp.where(strict, _mm_nt(kb, k) * decay, 0.0)
        attn = jnp.where(tri, _mm_nt(q, k) * decay, 0.0)
        eg = jnp.exp(gc_col)
        x = jnp.concatenate([v * beta, kb * eg], axis=1)
        qd = q * eg
        kdt = (k * jnp.exp(gl_col - gc_col)).T
        return dict(r0=r0, a=a, attn=attn, x=x, qd=qd, kdt=kdt, egl=jnp.exp(gl_col))

    def run_chains(ms, states):
        n = len(chains)
        ops = [prep(ms[i], d, hh) for i, (hh, d) in enumerate(chains)]
        ps = [jnp.where(same_base, o["a"], 0.0) for o in ops]
        ts = [eye - p for p in ps]
        for _ in range(BASE.bit_length() - 2):
            ps = [_mm(p, p) for p in ps]
            ts = [t + _mm(t, p) for t, p in zip(ts, ps)]
        inner = same_base
        for outer in merge_masks:
            lows = [_mm(jnp.where(outer & ~inner, o["a"], 0.0), t) for o, t in zip(ops, ts)]
            ts = [t - _mm(t, low) for t, low in zip(ts, lows)]
            inner = outer
        xs = [_mm(t, o["x"]) for t, o in zip(ts, ops)]
        for i in range(n):
            vn_s[i] = jnp.zeros((SC, DV), jnp.float32)
        states = list(states)
        for step in range(CPS):
            cs = [step if d == 0 else CPS - 1 - step for _, d in chains]
            los = [c * CHUNK for c in cs]
            ws_qs = [_mm(jnp.concatenate([x[lo:lo + CHUNK, DV:], o["qd"][lo:lo + CHUNK]], axis=0), s)
                     for x, o, lo, s in zip(xs, ops, los, states)]
            for i in range(n):
                vn_s[i, los[i]:los[i] + CHUNK, :] = xs[i][los[i]:los[i] + CHUNK, :DV] - ws_qs[i][:CHUNK]
            vns = [vn_s[i] for i in range(n)]
            o_cs = [wq[CHUNK:] + _mm(o["attn"][lo:lo + CHUNK, :], vn)
                    for wq, o, lo, vn in zip(ws_qs, ops, los, vns)]
            for i, (hh, _) in enumerate(chains):
                o_s[hh, pl.ds(ops[i]["r0"] + los[i], CHUNK), :] += o_cs[i]
            states = [s * o["egl"][lo:lo + 1, :]
                      + _mm(o["kdt"], jnp.where((rowi >= lo) & (rowi < lo + CHUNK), vn, 0.0))
                      for s, o, lo, vn in zip(states, ops, los, vns)]
        return tuple(states)

    if zero_init:
        states = tuple(jnp.zeros((DK, DV), jnp.float32) for _ in range(2 * hpb))
    else:
        states = tuple(s0_ref[0, d, hh] for hh in range(hpb) for d in range(2))

    chains = [(hh, d) for hh in range(hpb) for d in range(2)]

    def body(m, carry):
        return run_chains([m if d == 0 else n_sc - 1 - m for _, d in chains], carry)

    if n_sc == 1:
        states = body(0, states)
    else:
        states = lax.fori_loop(0, n_sc, body, states)

    for hh in range(hpb):
        for d in range(2):
            st_ref[0, d, hh] = states[2 * hh + d]
        o = o_s[hh]
        o = o * lax.rsqrt(jnp.mean(o * o, axis=-1, keepdims=True) + EPS) * og_ref[...]
        zz = z_ref[:, hh * DV:(hh + 1) * DV]
        o_ref[:, hh * DV:(hh + 1) * DV] = o * (zz * jax.nn.sigmoid(zz))


def _delta_call(qkv, z, ba, conv_w, a_log, dt_bias, onorm_g, s0, B, L, row_blk0):
    n_sc = L // SC
    t0 = row_blk0 * L
    bah = ba[t0:t0 + B * L, :4 * H_A].reshape(B, L, 4, H_A).transpose(0, 3, 1, 2)
    bar = bah.reshape(B, H_A, n_sc, SC, 4).transpose(0, 1, 2, 4, 3)
    scal = jnp.concatenate([a_log.reshape(-1), dt_bias.reshape(-1)]).astype(jnp.float32)
    hpb = max(1, min(H_A, DELTA_HEAD_ROWS // L))
    n_hb = H_A // hpb
    zero_init = s0 is None
    if zero_init:
        s0 = jnp.zeros((1, 2, hpb, DK, DV), jnp.float32)
        s0_spec = pl.BlockSpec((1, 2, hpb, DK, DV), lambda b, h, sc: (0, 0, 0, 0, 0))
    else:
        s0_spec = pl.BlockSpec((1, 2, hpb, DK, DV), lambda b, h, sc: (b, 0, h, 0, 0))

    def col(off):
        return pl.BlockSpec((L, hpb * DK), lambda b, h, sc: (row_blk0 + b, off * n_hb + h))

    def cw(off):
        return pl.BlockSpec((CONV_K, hpb * DK), lambda b, h, sc: (0, off * n_hb + h))

    kern = functools.partial(_delta_kernel, n_sc=n_sc, zero_init=zero_init, hpb=hpb)
    return pl.pallas_call(
        kern,
        grid_spec=pltpu.PrefetchScalarGridSpec(
            num_scalar_prefetch=1,
            grid=(B, n_hb),
            in_specs=[col(0), col(1), col(2),
                      pl.BlockSpec((L, hpb * DV), lambda b, h, sc: (row_blk0 + b, h)),
                      pl.BlockSpec((1, hpb, L, 4), lambda b, h, sc: (b, h, 0, 0)),
                      pl.BlockSpec((1, hpb, n_sc, 4, SC), lambda b, h, sc: (b, h, 0, 0, 0)),
                      cw(0), cw(1), cw(2),
                      pl.BlockSpec((1, DV), lambda b, h, sc: (0, 0)),
                      s0_spec],
            out_specs=[pl.BlockSpec((L, hpb * DV), lambda b, h, sc: (b, h)),
                       pl.BlockSpec((1, 2, hpb, DK, DV), lambda b, h, sc: (b, 0, h, 0, 0))],
            scratch_shapes=[pltpu.VMEM((hpb, L, DK), jnp.float32), pltpu.VMEM((hpb, L, DK), jnp.float32),
                            pltpu.VMEM((hpb, L, DV), jnp.float32), pltpu.VMEM((hpb, L, DV), jnp.float32),
                            pltpu.VMEM((2 * hpb, SC, DV), jnp.float32)]),
        out_shape=[jax.ShapeDtypeStruct((B * L, D_A), jnp.float32),
                   jax.ShapeDtypeStruct((B, 2, H_A, DK, DV), jnp.float32)],
        compiler_params=pltpu.CompilerParams(dimension_semantics=("arbitrary", "arbitrary"),
                                             vmem_limit_bytes=VMEM_LIMIT),
    )(scal, qkv, qkv, qkv, z, bah, bar, conv_w, conv_w, conv_w, onorm_g.reshape(1, DV), s0)


TR = 256
GSZ = N_EXPERTS // N_GROUPS
NEG = -jnp.inf


def _col_to_row(col, eye_mask):
    return jnp.sum(jnp.where(eye_mask, col, jnp.zeros_like(col)), axis=0, keepdims=True)


def _route_kernel(h_ref, rwh_ref, rwl_ref, rb_ref, idx_ref, rank_ref, w_ref, cnt_ref, cnt_s):
    i = pl.program_id(0)

    @pl.when(i == 0)
    def _():
        cnt_s[...] = jnp.zeros_like(cnt_s)

    h = h_ref[...]
    hh, hl = _split_bf16(h)
    logits = _bdot(hh, rwh_ref[...]) + (_bdot(hh, rwl_ref[...]) + _bdot(hl, rwh_ref[...]))
    scores = jax.nn.sigmoid(logits)
    sel = scores + rb_ref[...]
    lane = lax.broadcasted_iota(jnp.int32, sel.shape, 1)
    gid = lane // GSZ

    def first_argmax(v):
        m = jnp.max(v, axis=1, keepdims=True)
        first = jnp.min(jnp.where(v == m, lane, N_EXPERTS), axis=1, keepdims=True)
        return m, first

    gscore = []
    for g in range(N_GROUPS):
        vg = jnp.where(gid == g, sel, NEG)
        m1, i1 = first_argmax(vg)
        m2 = jnp.max(jnp.where(lane == i1, NEG, vg), axis=1, keepdims=True)
        gscore.append(m1 + m2)
    emask = jnp.zeros(sel.shape, jnp.bool_)
    for g in range(N_GROUPS):
        beat = jnp.zeros(gscore[g].shape, jnp.int32)
        for o in range(N_GROUPS):
            if o == g:
                continue
            wins = (gscore[o] > gscore[g]) | ((gscore[o] == gscore[g]) & (o < g))
            beat = beat + wins.astype(jnp.int32)
        emask = emask | ((gid == g) & (beat < TOPK_GROUP))
    cand = jnp.where(emask, sel, NEG)
    chosen = []
    picked = jnp.zeros(sel.shape, jnp.bool_)
    for _ in range(TOP_K):
        _, ik = first_argmax(cand)
        hit = lane == ik
        chosen.append((ik, hit))
        picked = picked | hit
        cand = jnp.where(hit, NEG, cand)
    wraw = jnp.where(picked, scores, 0.0)
    wmat = wraw / jnp.sum(wraw, axis=1, keepdims=True) * ROUTED_SCALE

    pm = picked.astype(jnp.bfloat16)
    ri = lax.broadcasted_iota(jnp.int32, (TR, TR), 0)
    ci = lax.broadcasted_iota(jnp.int32, (TR, TR), 1)
    earlier = (ci < ri).astype(jnp.bfloat16)
    rank_mat = _bdot(earlier, pm) + cnt_s[...]
    cnt_s[...] = cnt_s[...] + jnp.sum(picked.astype(jnp.float32), axis=0, keepdims=True)
    cnt_ref[...] = cnt_s[...].astype(jnp.int32)

    eye = ri == ci
    lane8 = lax.broadcasted_iota(jnp.int32, (TR, TOP_K), 1)
    wcols = jnp.zeros((TR, TOP_K), jnp.float32)
    for k, (ik, hit) in enumerate(chosen):
        rk = jnp.sum(jnp.where(hit, rank_mat, 0.0), axis=1, keepdims=True)
        wk = jnp.sum(jnp.where(hit, wmat, 0.0), axis=1, keepdims=True)
        idx_ref[0, k:k + 1, :] = _col_to_row(ik, eye)
        rank_ref[0, k:k + 1, :] = _col_to_row(rk, eye).astype(jnp.int32)
        wcols = jnp.where(lane8 == k, wk, wcols)
    w_ref[...] = wcols


def _route_call(hf, router_w, router_bias):
    T, D = hf.shape
    n_tiles = T // TR
    rwh, rwl = _split_bf16(router_w)
    return pl.pallas_call(
        _route_kernel,
        grid=(n_tiles,),
        in_specs=[pl.BlockSpec((TR, D), lambda i: (i, 0)),
                  pl.BlockSpec((D, N_EXPERTS), lambda i: (0, 0)),
                  pl.BlockSpec((D, N_EXPERTS), lambda i: (0, 0)),
                  pl.BlockSpec((1, N_EXPERTS), lambda i: (0, 0))],
        out_specs=[pl.BlockSpec((1, TOP_K, TR), lambda i: (i, 0, 0)),
                   pl.BlockSpec((1, TOP_K, TR), lambda i: (i, 0, 0)),
                   pl.BlockSpec((TR, TOP_K), lambda i: (i, 0)),
                   pl.BlockSpec((1, N_EXPERTS), lambda i: (0, 0))],
        scratch_shapes=[pltpu.VMEM((1, N_EXPERTS), jnp.float32)],
        out_shape=[jax.ShapeDtypeStruct((n_tiles, TOP_K, TR), jnp.int32),
                   jax.ShapeDtypeStruct((n_tiles, TOP_K, TR), jnp.int32),
                   jax.ShapeDtypeStruct((T, TOP_K), jnp.float32),
                   jax.ShapeDtypeStruct((1, N_EXPERTS), jnp.int32)],
        compiler_params=pltpu.CompilerParams(dimension_semantics=("arbitrary",)),
    )(hf, rwh, rwl, router_bias.reshape(1, N_EXPERTS).astype(jnp.float32))


def _dispatch_kernel(idx_ref, rank_ref, pstart_ref, h_ref, xs_init, pos_ref, xs_hbm, pos_v, pos_s, ssem, psem):
    del xs_init
    erow = lax.broadcasted_iota(jnp.int32, (N_EXPERTS, TR), 0)
    pstart = pstart_ref[...]
    for k in range(TOP_K):
        hit = erow == idx_ref[0, k:k + 1, :]
        seg = jnp.sum(jnp.where(hit, pstart, 0), axis=0, keepdims=True)
        pos_v[k:k + 1, :] = seg + rank_ref[0, k:k + 1, :]
    pos_ref[0] = pos_v[...]
    cp = pltpu.make_async_copy(pos_v, pos_s, psem)
    cp.start()
    cp.wait()

    def body(t, carry):
        for k in range(TOP_K):
            pltpu.make_async_copy(h_ref.at[t], xs_hbm.at[pos_s[k, t]], ssem).start()
        return carry

    lax.fori_loop(0, TR, body, 0, unroll=8)
    n_rows = TR * TOP_K
    pltpu.make_async_copy(xs_hbm.at[pl.ds(0, n_rows)], xs_hbm.at[pl.ds(0, n_rows)], ssem).wait()


def _dispatch_call(hf, idx, rank, pad_start, n_pad):
    T, D = hf.shape
    n_tiles = T // TR
    return pl.pallas_call(
        _dispatch_kernel,
        grid=(n_tiles,),
        in_specs=[pl.BlockSpec((1, TOP_K, TR), lambda i: (i, 0, 0)),
                  pl.BlockSpec((1, TOP_K, TR), lambda i: (i, 0, 0)),
                  pl.BlockSpec((N_EXPERTS, 1), lambda i: (0, 0)),
                  pl.BlockSpec((TR, D), lambda i: (i, 0)),
                  pl.BlockSpec(memory_space=pl.ANY)],
        out_specs=[pl.BlockSpec((1, TOP_K, TR), lambda i: (i, 0, 0)),
                   pl.BlockSpec(memory_space=pl.ANY)],
        scratch_shapes=[pltpu.VMEM((TOP_K, TR), jnp.int32), pltpu.SMEM((TOP_K, TR), jnp.int32),
                        pltpu.SemaphoreType.DMA, pltpu.SemaphoreType.DMA],
        out_shape=[jax.ShapeDtypeStruct((n_tiles, TOP_K, TR), jnp.int32),
                   jax.ShapeDtypeStruct((n_pad, D), hf.dtype)],
        input_output_aliases={4: 1},
        compiler_params=pltpu.CompilerParams(dimension_semantics=("arbitrary",)),
    )(idx, rank, pad_start.reshape(N_EXPERTS, 1), hf, jnp.zeros((n_pad, D), hf.dtype))


BM = 256


def _expert_kernel(blk_e_ref, nvalid_ref, nused_ref, x_ref, wg_ref, wu_ref, wd_ref, y_ref, wg_s, wu_s, wd_s):
    i = pl.program_id(0)

    @pl.when(i < nused_ref[0])
    def _():
        e = blk_e_ref[i]
        prev = blk_e_ref[jnp.maximum(i - 1, 0)]

        @pl.when((i == 0) | (e != prev))
        def _():
            wg_s[...] = wg_ref[0].astype(jnp.bfloat16)
            wu_s[...] = wu_ref[0].astype(jnp.bfloat16)
            wd_s[...] = wd_ref[0].astype(jnp.bfloat16)

        row = lax.broadcasted_iota(jnp.int32, (BM, 1), 0)
        xa, xb = _unpack_rows(jnp.where(row < nvalid_ref[i], x_ref[...], jnp.uint32(0)))
        xa = xa.astype(jnp.bfloat16)
        xb = xb.astype(jnp.bfloat16)
        half = xa.shape[1]
        g = _bdot(xa, wg_s[:half, :]) + _bdot(xb, wg_s[half:, :])
        u = _bdot(xa, wu_s[:half, :]) + _bdot(xb, wu_s[half:, :])
        a = (g * jax.nn.sigmoid(g)) * u
        y_ref[...] = _pack_rows(_bdot(a.astype(jnp.bfloat16), wd_s[...]))

    @pl.when(i >= nused_ref[0])
    def _():
        y_ref[...] = jnp.zeros_like(y_ref)


def _expert_call(x_sorted, blk_e, n_valid, n_used, w_gate, w_up, w_down):
    n_pad, DH = x_sorted.shape
    n_blk = n_pad // BM
    E, D, F = w_gate.shape

    def row_map(i, be, nv, nu):
        return (jnp.minimum(i, nu[0] - 1), 0)

    def w_map(i, be, nv, nu):
        return (be[jnp.minimum(i, nu[0] - 1)], 0, 0)

    return pl.pallas_call(
        _expert_kernel,
        grid_spec=pltpu.PrefetchScalarGridSpec(
            num_scalar_prefetch=3,
            grid=(n_blk,),
            in_specs=[pl.BlockSpec((BM, DH), row_map),
                      pl.BlockSpec((1, D, F), w_map),
                      pl.BlockSpec((1, D, F), w_map),
                      pl.BlockSpec((1, F, D), w_map)],
            out_specs=pl.BlockSpec((BM, DH), lambda i, be, nv, nu: (i, 0)),
            scratch_shapes=[pltpu.VMEM((D, F), jnp.bfloat16), pltpu.VMEM((D, F), jnp.bfloat16),
                            pltpu.VMEM((F, D), jnp.bfloat16)]),
        out_shape=jax.ShapeDtypeStruct((n_pad, DH), jnp.uint32),
        compiler_params=pltpu.CompilerParams(dimension_semantics=("arbitrary",),
                                             vmem_limit_bytes=VMEM_LIMIT),
    )(blk_e, n_valid, n_used, x_sorted, w_gate, w_up, w_down)


TC = 64


def _combine_kernel(pos_hbm, y_hbm, w_ref, x1_ref, sh_ref, mod_ref, fg_ref, out_ref, ybuf, pos_s, gsem, psem):
    j = pl.program_id(0)
    last = pl.num_programs(0) - 1

    def pos_copy(b, slot):
        return pltpu.make_async_copy(pos_hbm.at[b], pos_s.at[slot], psem.at[slot])

    def start_gather(slot):
        for k in range(TOP_K):
            for t in range(TC):
                pltpu.make_async_copy(y_hbm.at[pos_s[slot, k * TC + t]], ybuf.at[slot, k, t],
                                      gsem.at[slot]).start()

    def wait_gather(slot):
        pltpu.make_async_copy(ybuf.at[slot], ybuf.at[slot], gsem.at[slot]).wait()

    @pl.when(j == 0)
    def _():
        pos_copy(0, 0).start()
        pos_copy(0, 0).wait()
        start_gather(0)
        pos_copy(jnp.minimum(1, last), 1).start()

    slot = j % 2
    nslot = 1 - slot
    pos_copy(0, nslot).wait()
    start_gather(nslot)
    pos_copy(jnp.minimum(j + 2, last), slot).start()
    wait_gather(slot)
    w = w_ref[...]
    acc_a = acc_b = None
    for k in range(TOP_K):
        ya, yb = _unpack_rows(ybuf[slot, k])
        acc_a = w[:, k:k + 1] * ya if k == 0 else acc_a + w[:, k:k + 1] * ya
        acc_b = w[:, k:k + 1] * yb if k == 0 else acc_b + w[:, k:k + 1] * yb
    acc = jnp.concatenate([acc_a, acc_b], axis=1)
    x2 = x1_ref[...] + mod_ref[0, 5:6, :] * (acc + sh_ref[...])
    out_ref[...] = x2 * lax.rsqrt(jnp.mean(x2 * x2, axis=-1, keepdims=True) + EPS) * fg_ref[...]

    @pl.when(j == last)
    def _():
        wait_gather(nslot)
        pos_copy(0, slot).wait()


def _combine_call(y_sorted, pos_t, wts, x1, shared, mod, final_g, n_ctx, lat_len):
    T, K = wts.shape
    DH = y_sorted.shape[1]
    D = 2 * DH
    n_tiles = T // TC
    row = functools.partial(_mod_row, tokens_per_tile=TC, n_ctx=n_ctx, lat_len=lat_len)
    return pl.pallas_call(
        _combine_kernel,
        grid=(n_tiles,),
        in_specs=[pl.BlockSpec(memory_space=pl.ANY),
                  pl.BlockSpec(memory_space=pl.ANY),
                  pl.BlockSpec((TC, K), lambda j: (j, 0)),
                  pl.BlockSpec((TC, D), lambda j: (j, 0)),
                  pl.BlockSpec((TC, D), lambda j: (j, 0)),
                  pl.BlockSpec((1, N_MOD, D), lambda j: (row(j), 0, 0)),
                  pl.BlockSpec((1, D), lambda j: (0, 0))],
        out_specs=pl.BlockSpec((TC, D), lambda j: (j, 0)),
        scratch_shapes=[pltpu.VMEM((2, K, TC, DH), jnp.uint32),
                        pltpu.SMEM((2, K * TC), jnp.int32),
                        pltpu.SemaphoreType.DMA((2,)), pltpu.SemaphoreType.DMA((2,))],
        out_shape=jax.ShapeDtypeStruct((T, D), jnp.float32),
        compiler_params=pltpu.CompilerParams(dimension_semantics=("arbitrary",)),
    )(pos_t, y_sorted, wts, x1, shared, mod, final_g.reshape(1, D))


def _moe_routed(h2, h2p, router_w, router_bias, w_gate, w_up, w_down):
    T, D = h2.shape
    idx, rank, wts, cnt = _route_call(h2, router_w, router_bias)
    counts = cnt[0]
    padded = (counts + BM - 1) // BM * BM
    pad_end = jnp.cumsum(padded)
    pad_start = (pad_end - padded).astype(jnp.int32)
    n_pad = T * TOP_K + N_EXPERTS * BM
    n_blk = n_pad // BM
    pos, x_sorted = _dispatch_call(h2p, idx, rank, pad_start, n_pad)
    blk_row0 = jnp.arange(n_blk, dtype=jnp.int32) * BM
    blk_e = jnp.minimum(jnp.sum((pad_end[None, :] <= blk_row0[:, None]).astype(jnp.int32), axis=1), N_EXPERTS - 1)
    own = blk_e[:, None] == jnp.arange(N_EXPERTS, dtype=jnp.int32)[None, :]
    seg_end = jnp.sum(jnp.where(own, (pad_start + counts)[None, :], 0), axis=1)
    n_valid = jnp.clip(seg_end - blk_row0, 0, BM).astype(jnp.int32)
    n_used = (pad_end[-1] // BM).astype(jnp.int32).reshape(1)
    y = _expert_call(x_sorted, blk_e, n_valid, n_used, w_gate, w_up, w_down)
    pos_t = pos.reshape(T // TR, TOP_K, TR // TC, TC).transpose(0, 2, 1, 3).reshape(T // TC, TOP_K * TC)
    return y, pos_t, wts


def kernel(x_prompt, x_sample, state_delta, c, c_ctx, w_ada, b_ada, norm1_g, w_in, conv_w, a_log,
           dt_bias, onorm_g, pool_w, pool_scale, w_out, norm2_g, router_w, router_bias, exp_w_gate,
           exp_w_up, exp_w_down, sh_w_gate, sh_w_up, sh_w_down, final_g):
    Bc, Lc, D = x_prompt.shape
    Bl, Ll, _ = x_sample.shape
    n_ctx = Bc * Lc
    assert DEPTH == 1 and 1 + Bl <= MOD_ROWS and n_ctx % Ll == 0
    x = jnp.concatenate([x_prompt.reshape(n_ctx, D), x_sample.reshape(Bl * Ll, D)], axis=0)
    cvec = jnp.concatenate([c_ctx[None], c, jnp.zeros((MOD_ROWS - 1 - Bl, D), c.dtype)], axis=0)
    l = 0
    mod = _ada_call(cvec, w_ada[l], b_ada[l]).reshape(MOD_ROWS, N_MOD, D)
    qkv, z, ba, u = _inproj_call(x, mod, norm1_g[l], w_in[l], n_ctx, Ll)
    dn = (conv_w[l], a_log[l], dt_bias[l], onorm_g[l])
    oa_c, st_ctx = _delta_call(qkv, z, ba, *dn, None, Bc, Lc, 0)
    oa_l, _ = _delta_call(qkv, z, ba, *dn, state_delta[:, l], Bl, Ll, n_ctx // Ll)
    op_c = _pool_call(u[:n_ctx].reshape(Bc, Lc, D_P), pool_w[l], pool_scale[l], False)
    op_l = _pool_call(u[n_ctx:].reshape(Bl, Ll, D_P), pool_w[l], pool_scale[l], True)
    o_a = jnp.concatenate([oa_c, oa_l], axis=0)
    o_p = jnp.concatenate([op_c.reshape(n_ctx, D_P), op_l.reshape(Bl * Ll, D_P)], axis=0)
    x1, h2, h2p, shared = _outproj_call(x, o_a, o_p, mod, norm2_g[l], w_out[l], sh_w_gate[l], sh_w_up[l],
                                        sh_w_down[l], n_ctx, Ll)
    y, pos_t, wts = _moe_routed(h2, h2p, router_w[l], router_bias[l], exp_w_gate[l], exp_w_up[l],
                                exp_w_down[l])
    out = _combine_call(y, pos_t, wts, x1, shared, mod, final_g, n_ctx, Ll)
    y_prompt = out[:n_ctx].reshape(Bc, Lc, D)
    y_sample = out[n_ctx:].reshape(Bl, Ll, D)
    new_state_delta = st_ctx[:, None].astype(x_prompt.dtype)
    return (y_prompt, y_sample, new_state_delta)
```

```python
import functools
import jax, jax.numpy as jnp
from jax import lax
from jax.experimental import pallas as pl
from jax.experimental.pallas import tpu as pltpu

D_MODEL = 1024
DEPTH = 1
GRID_W = 64
D_MIX = D_MODEL
D_A = D_MIX // 2
D_P = D_MIX - D_A
H_A = 4
DK = D_A // H_A
DV = D_A // H_A
CONV_K = 5
CHUNK = 64
POOL_WINDOWS = (2, 4, 8, 16)
N_PG = len(POOL_WINDOWS)
PG = D_P // N_PG
N_EXPERTS = 256
TOP_K = 8
N_GROUPS = 8
TOPK_GROUP = 4
ROUTED_SCALE = 2.5
EPS = 1e-6
VMEM_LIMIT = 48 * 1024 * 1024


def _split_bf16(a):
    hi = a.astype(jnp.bfloat16)
    return hi, (a - hi.astype(jnp.float32)).astype(jnp.bfloat16)


def _bdot(a, b):
    return jnp.dot(a, b, preferred_element_type=jnp.float32)


def _pack_rows(x):
    m = x.shape[1] // 2
    hi = lax.bitcast_convert_type(x[:, :m].astype(jnp.bfloat16).astype(jnp.float32), jnp.uint32)
    lo = lax.bitcast_convert_type(x[:, m:].astype(jnp.bfloat16).astype(jnp.float32), jnp.uint32)
    return hi | (lo >> 16)


def _unpack_rows(p):
    hi = lax.bitcast_convert_type(p & jnp.uint32(0xFFFF0000), jnp.float32)
    lo = lax.bitcast_convert_type(p << 16, jnp.float32)
    return hi, lo


N_MOD = 6
MOD_ROWS = 8
TM = 512


def _ada_kernel(c_ref, w_ref, b_ref, o_ref):
    c = c_ref[...]
    s = c * jax.nn.sigmoid(c)
    sh, sl = _split_bf16(s)
    wh, wl = _split_bf16(w_ref[...])
    o_ref[...] = _bdot(sh, wh) + (_bdot(sh, wl) + _bdot(sl, wh)) + b_ref[...]


def _ada_call(cvec, w_ada, b_ada):
    R, D = cvec.shape
    N = w_ada.shape[1]
    tn = 1024
    return pl.pallas_call(
        _ada_kernel,
        grid=(N // tn,),
        in_specs=[pl.BlockSpec((R, D), lambda j: (0, 0)),
                  pl.BlockSpec((D, tn), lambda j: (0, j)),
                  pl.BlockSpec((1, tn), lambda j: (0, j))],
        out_specs=pl.BlockSpec((R, tn), lambda j: (0, j)),
        out_shape=jax.ShapeDtypeStruct((R, N), jnp.float32),
    )(cvec, w_ada, b_ada.reshape(1, N))


def _mod_row(tile, tokens_per_tile, n_ctx, lat_len):
    t0 = tile * tokens_per_tile
    return jnp.where(t0 < n_ctx, 0, 1 + (t0 - n_ctx) // lat_len)


def _two_part_specs(n_ctx_tiles, width):
    return (pl.BlockSpec((TM, width), lambda i: (jnp.minimum(i, n_ctx_tiles - 1), 0)),
            pl.BlockSpec((TM, width), lambda i: (jnp.maximum(i - n_ctx_tiles, 0), 0)))


def _pick(n_ctx_tiles, ctx_ref, lat_ref):
    return jnp.where(pl.program_id(0) < n_ctx_tiles, ctx_ref[...], lat_ref[...])


def _inproj_kernel(xc_ref, xl_ref, mod_ref, g_ref, wq_ref, wz_ref, wb_ref, wu_ref, q_ref, z_ref, b_ref, u_ref,
                   *, n_ctx_tiles):
    x = _pick(n_ctx_tiles, xc_ref, xl_ref)
    y = x * lax.rsqrt(jnp.mean(x * x, axis=-1, keepdims=True) + EPS) * g_ref[...]
    h = (y * (1.0 + mod_ref[0, 1:2, :]) + mod_ref[0, 0:1, :]).astype(jnp.bfloat16)
    q_ref[...] = _bdot(h, wq_ref[...])
    z_ref[...] = _bdot(h, wz_ref[...])
    b_ref[...] = _bdot(h, wb_ref[...])
    u_ref[...] = _bdot(h, wu_ref[...])


def _inproj_call(x_ctx, x_lat, mod, norm1_g, w_in, lat_len):
    n_ctx, D = x_ctx.shape
    T = n_ctx + x_lat.shape[0]
    bf = jnp.bfloat16
    nq, nz, nb = 3 * D_A, D_A, 4 * H_A
    wq = w_in[:, :nq].astype(bf)
    wz = w_in[:, nq:nq + nz].astype(bf)
    wb = jnp.pad(w_in[:, nq + nz:nq + nz + nb], ((0, 0), (0, 128 - nb))).astype(bf)
    wu = w_in[:, nq + nz + nb:].astype(bf)
    row = functools.partial(_mod_row, tokens_per_tile=TM, n_ctx=n_ctx, lat_len=lat_len)

    def full(a):
        return pl.BlockSpec(a.shape, lambda i: (0, 0))

    def rows(n):
        return pl.BlockSpec((TM, n), lambda i: (i, 0))

    return pl.pallas_call(
        functools.partial(_inproj_kernel, n_ctx_tiles=n_ctx // TM),
        grid=(T // TM,),
        in_specs=[*_two_part_specs(n_ctx // TM, D), pl.BlockSpec((1, N_MOD, D), lambda i: (row(i), 0, 0)),
                  pl.BlockSpec((1, D), lambda i: (0, 0)), full(wq), full(wz), full(wb), full(wu)],
        out_specs=[rows(nq), rows(nz), rows(128), rows(D_P)],
        out_shape=[jax.ShapeDtypeStruct((T, nq), jnp.float32), jax.ShapeDtypeStruct((T, nz), jnp.float32),
                   jax.ShapeDtypeStruct((T, 128), jnp.float32), jax.ShapeDtypeStruct((T, D_P), jnp.float32)],
        compiler_params=pltpu.CompilerParams(dimension_semantics=("arbitrary",),
                                             vmem_limit_bytes=VMEM_LIMIT),
    )(x_ctx, x_lat, mod, norm1_g.reshape(1, D), wq, wz, wb, wu)


PT = 256


def _window_bounds(pos, w, n):
    return jnp.maximum(pos - w // 2, 0), jnp.minimum(pos + w - w // 2, n)


def _band_sum(band, x):
    xh, xl = _split_bf16(x)
    return _bdot(band, xh) + _bdot(band, xl)


def _pool_seq_kernel(u_ref, pw_ref, ps_ref, o_ref):
    L = u_ref.shape[0]
    ti = lax.broadcasted_iota(jnp.int32, (L, L), 0)
    ji = lax.broadcasted_iota(jnp.int32, (L, L), 1)
    tcol = lax.broadcasted_iota(jnp.int32, (L, 1), 0)
    for i, w in enumerate(POOL_WINDOWS):
        lo, hi = _window_bounds(ti, w, L)
        band = ((ji >= lo) & (ji < hi)).astype(jnp.bfloat16)
        clo, chi = _window_bounds(tcol, w, L)
        ug = u_ref[:, i * PG:(i + 1) * PG]
        mean = _band_sum(band, ug) / (chi - clo).astype(jnp.float32)
        d = (mean - ug).astype(jnp.bfloat16)
        o_ref[:, i * PG:(i + 1) * PG] = _bdot(d, pw_ref[i]) * ps_ref[:, i * PG:(i + 1) * PG]


def _pool_grid_kernel(u_ref, pw_ref, ps_ref, o_ref, pad_s, r_s):
    L = u_ref.shape[0]
    rows = L // GRID_W
    halo = (max(POOL_WINDOWS) // 2) * GRID_W
    pad_s[0:halo, :] = jnp.zeros((halo, D_P), jnp.float32)
    pad_s[halo + L:, :] = jnp.zeros((halo, D_P), jnp.float32)
    pad_s[halo:halo + L, :] = u_ref[...]
    ti = lax.broadcasted_iota(jnp.int32, (PT, PT), 0)
    ji = lax.broadcasted_iota(jnp.int32, (PT, PT), 1)
    tcol = lax.broadcasted_iota(jnp.int32, (PT, 1), 0)
    for i, w in enumerate(POOL_WINDOWS):
        cs = slice(i * PG, (i + 1) * PG)
        acc = None
        for dr in range(-(w // 2), w - w // 2):
            part = pad_s[halo + dr * GRID_W:halo + dr * GRID_W + L, cs]
            acc = part if acc is None else acc + part
        r_s[...] = acc
        lo, hi = _window_bounds(ti % GRID_W, w, GRID_W)
        band = ((ji // GRID_W == ti // GRID_W) & (ji % GRID_W >= lo) & (ji % GRID_W < hi)).astype(jnp.bfloat16)
        clo, chi = _window_bounds(tcol % GRID_W, w, GRID_W)
        ccnt = (chi - clo).astype(jnp.float32)
        for tile in range(L // PT):
            ts = slice(tile * PT, (tile + 1) * PT)
            rlo, rhi = _window_bounds(tile * (PT // GRID_W) + tcol // GRID_W, w, rows)
            mean = _band_sum(band, r_s[ts, :]) / ((rhi - rlo).astype(jnp.float32) * ccnt)
            d = (mean - u_ref[ts, cs]).astype(jnp.bfloat16)
            o_ref[ts, cs] = _bdot(d, pw_ref[i]) * ps_ref[:, cs]


def _pool_call(u, pool_w, pool_scale, grid, B, L, row_blk0):
    pw = pool_w.astype(jnp.bfloat16)
    ps = pool_scale.reshape(1, D_P)
    specs = dict(
        grid=(B,),
        in_specs=[pl.BlockSpec((L, D_P), lambda b: (row_blk0 + b, 0)),
                  pl.BlockSpec((N_PG, PG, PG), lambda b: (0, 0, 0)),
                  pl.BlockSpec((1, D_P), lambda b: (0, 0))],
        out_specs=pl.BlockSpec((L, D_P), lambda b: (b, 0)),
        out_shape=jax.ShapeDtypeStruct((B * L, D_P), jnp.float32),
        compiler_params=pltpu.CompilerParams(dimension_semantics=("arbitrary",),
                                             vmem_limit_bytes=VMEM_LIMIT))
    if not grid:
        return pl.pallas_call(_pool_seq_kernel, **specs)(u, pw, ps)
    halo = (max(POOL_WINDOWS) // 2) * GRID_W
    return pl.pallas_call(
        _pool_grid_kernel,
        scratch_shapes=[pltpu.VMEM((L + 2 * halo, D_P), jnp.float32), pltpu.VMEM((L, PG), jnp.float32)],
        **specs)(u, pw, ps)


def _outproj_kernel(xc_ref, xl_ref, oac_ref, oal_ref, opc_ref, opl_ref, mod_ref, g2_ref, wo_ref, sg_ref, su_ref,
                    sd_ref, x1_ref, h2_ref, h2p_ref, sh_ref, *, n_ctx_tiles):
    o_a = _pick(n_ctx_tiles, oac_ref, oal_ref)
    o_p = _pick(n_ctx_tiles, opc_ref, opl_ref)
    mix = (_bdot(o_a.astype(jnp.bfloat16), wo_ref[:D_A, :])
           + _bdot(o_p.astype(jnp.bfloat16), wo_ref[D_A:, :]))
    x1 = _pick(n_ctx_tiles, xc_ref, xl_ref) + mod_ref[0, 2:3, :] * mix
    x1_ref[...] = x1
    y = x1 * lax.rsqrt(jnp.mean(x1 * x1, axis=-1, keepdims=True) + EPS) * g2_ref[...]
    h2 = y * (1.0 + mod_ref[0, 4:5, :]) + mod_ref[0, 3:4, :]
    h2_ref[...] = h2
    h2p_ref[...] = _pack_rows(h2)
    hb = h2.astype(jnp.bfloat16)
    g = _bdot(hb, sg_ref[...])
    a = (g * jax.nn.sigmoid(g)) * _bdot(hb, su_ref[...])
    sh_ref[...] = _bdot(a.astype(jnp.bfloat16), sd_ref[...])


def _outproj_call(x_parts, oa_parts, op_parts, mod, norm2_g, w_out, sh_gate, sh_up, sh_down, lat_len):
    n_ctx, D = x_parts[0].shape
    T = n_ctx + x_parts[1].shape[0]
    nct = n_ctx // TM
    bf = jnp.bfloat16
    row = functools.partial(_mod_row, tokens_per_tile=TM, n_ctx=n_ctx, lat_len=lat_len)
    ws = [w_out.astype(bf), sh_gate.astype(bf), sh_up.astype(bf), sh_down.astype(bf)]

    def rows(n):
        return pl.BlockSpec((TM, n), lambda i: (i, 0))

    return pl.pallas_call(
        functools.partial(_outproj_kernel, n_ctx_tiles=nct),
        grid=(T // TM,),
        in_specs=[*_two_part_specs(nct, D), *_two_part_specs(nct, D_A), *_two_part_specs(nct, D_P),
                  pl.BlockSpec((1, N_MOD, D), lambda i: (row(i), 0, 0)),
                  pl.BlockSpec((1, D), lambda i: (0, 0))] + [pl.BlockSpec(w.shape, lambda i: (0, 0)) for w in ws],
        out_specs=[rows(D), rows(D), rows(D // 2), rows(D)],
        out_shape=[jax.ShapeDtypeStruct((T, D), jnp.float32), jax.ShapeDtypeStruct((T, D), jnp.float32),
                   jax.ShapeDtypeStruct((T, D // 2), jnp.uint32), jax.ShapeDtypeStruct((T, D), jnp.float32)],
        compiler_params=pltpu.CompilerParams(dimension_semantics=("arbitrary",),
                                             vmem_limit_bytes=VMEM_LIMIT),
    )(*x_parts, *oa_parts, *op_parts, mod, norm2_g.reshape(1, D), *ws)


SC = 256
CPS = SC // CHUNK
BASE = 16
DELTA_HEAD_ROWS = 4096


def _mm(a, b):
    return jnp.dot(a.astype(jnp.bfloat16), b.astype(jnp.bfloat16), preferred_element_type=jnp.float32)


def _mm_nt(a, b):
    return lax.dot_general(a.astype(jnp.bfloat16), b.astype(jnp.bfloat16), (((1,), (1,)), ((), ())),
                           preferred_element_type=jnp.float32)


def _softplus(x):
    return jnp.maximum(x, 0.0) + jnp.log(1.0 + jnp.exp(-jnp.abs(x)))


def _delta_kernel(sc_ref, xq_ref, xk_ref, xv_ref, z_ref, bac_ref, bar_ref, cwq_ref, cwk_ref, cwv_ref,
                  og_ref, s0_ref, o_ref, st_ref, q_s, k_s, v_s, o_s, vn_s, *, n_sc, zero_init, hpb):
    hb = pl.program_id(1)
    L = q_s.shape[1]

    def conv(x_ref, w_ref, cs):
        x = x_ref[:, cs]
        row = lax.broadcasted_iota(jnp.int32, x.shape, 0)
        acc = x * w_ref[CONV_K // 2:CONV_K // 2 + 1, cs]
        for j in range(CONV_K):
            d = j - CONV_K // 2
            if d == 0:
                continue
            xs = pltpu.roll(x, (-d) % L, 0)
            ok = (row + d >= 0) & (row + d < L)
            acc = acc + jnp.where(ok, xs, 0.0) * w_ref[j:j + 1, cs]
        return acc * jax.nn.sigmoid(acc)

    for hh in range(hpb):
        cs = slice(hh * DK, (hh + 1) * DK)
        q = conv(xq_ref, cwq_ref, cs)
        q_s[hh] = q * lax.rsqrt(jnp.sum(q * q, axis=-1, keepdims=True) + EPS) * (DK ** -0.5)
        k = conv(xk_ref, cwk_ref, cs)
        k_s[hh] = k * lax.rsqrt(jnp.sum(k * k, axis=-1, keepdims=True) + EPS)
        v_s[hh] = conv(xv_ref, cwv_ref, cs)
    o_s[...] = jnp.zeros_like(o_s)

    ri = lax.broadcasted_iota(jnp.int32, (SC, SC), 0)
    ci = lax.broadcasted_iota(jnp.int32, (SC, SC), 1)
    same = (ri // CHUNK) == (ci // CHUNK)
    same_base = (ri // BASE) == (ci // BASE)
    merge_masks = [(ri // w) == (ci // w) for w in (2 * BASE, CHUNK)]
    eye = (ri == ci).astype(jnp.float32)
    rowi = lax.broadcasted_iota(jnp.int32, (SC, DV), 0)

    def prep(m, d, hh):
        r0 = pl.multiple_of(m * SC, SC)
        h = hb * hpb + hh
        q = q_s[hh, pl.ds(r0, SC), :]
        k = k_s[hh, pl.ds(r0, SC), :]
        v = v_s[hh, pl.ds(r0, SC), :]
        bc = bac_ref[0, hh, pl.ds(r0, SC), :]
        br = bar_ref[0, hh, m]
        a_l = sc_ref[d * H_A + h]
        dtb = sc_ref[2 * H_A + d * H_A + h]
        neg_ea = -jnp.exp(jnp.full((1, 1), a_l, jnp.float32))
        beta = jax.nn.sigmoid(bc[:, d:d + 1])
        g_col = neg_ea * _softplus(bc[:, 2 + d:3 + d] + dtb)
        g_row = neg_ea * _softplus(br[2 + d:3 + d, :] + dtb)
        if d == 0:
            tri, strict = same & (ci <= ri), same & (ci < ri)
        else:
            tri, strict = same & (ci >= ri), same & (ci > ri)
        tri_t = same & (ri <= ci) if d == 0 else same & (ri >= ci)
        gc_col = jnp.sum(jnp.where(tri, g_row, 0.0), axis=1, keepdims=True)
        gc_row = jnp.sum(jnp.where(tri_t, g_col, 0.0), axis=0, keepdims=True)
        gl_col = jnp.sum(jnp.where(same, g_row, 0.0), axis=1, keepdims=True)
        decay = jnp.where(tri, jnp.exp(jnp.where(tri, gc_col - gc_row, 0.0)), 0.0)
        kb = k * beta
        a = jnp.where(strict, _mm_nt(kb, k) * decay, 0.0)
        attn = jnp.where(tri, _mm_nt(q, k) * decay, 0.0)
        eg = jnp.exp(gc_col)
        x = jnp.concatenate([v * beta, kb * eg], axis=1)
        qd = q * eg
        kdt = (k * jnp.exp(gl_col - gc_col)).T
        return dict(r0=r0, a=a, attn=attn, x=x, qd=qd, kdt=kdt, egl=jnp.exp(gl_col))

    def run_chains(ms, states):
        n = len(chains)
        ops = [prep(ms[i], d, hh) for i, (hh, d) in enumerate(chains)]
        ps = [jnp.where(same_base, o["a"], 0.0) for o in ops]
        ts = [eye - p for p in ps]
        for _ in range(BASE.bit_length() - 2):
            ps = [_mm(p, p) for p in ps]
            ts = [t + _mm(t, p) for t, p in zip(ts, ps)]
        inner = same_base
        for outer in merge_masks:
            lows = [_mm(jnp.where(outer & ~inner, o["a"], 0.0), t) for o, t in zip(ops, ts)]
            ts = [t - _mm(t, low) for t, low in zip(ts, lows)]
            inner = outer
        xs = [_mm(t, o["x"]) for t, o in zip(ts, ops)]
        for i in range(n):
            vn_s[i] = jnp.zeros((SC, DV), jnp.float32)
        states = list(states)
        for step in range(CPS):
            cs = [step if d == 0 else CPS - 1 - step for _, d in chains]
            los = [c * CHUNK for c in cs]
            ws_qs = [_mm(jnp.concatenate([x[lo:lo + CHUNK, DV:], o["qd"][lo:lo + CHUNK]], axis=0), s)
                     for x, o, lo, s in zip(xs, ops, los, states)]
            for i in range(n):
                vn_s[i, los[i]:los[i] + CHUNK, :] = xs[i][los[i]:los[i] + CHUNK, :DV] - ws_qs[i][:CHUNK]
            vns = [vn_s[i] for i in range(n)]
            o_cs = [wq[CHUNK:] + _mm(o["attn"][lo:lo + CHUNK, :], vn)
                    for wq, o, lo, vn in zip(ws_qs, ops, los, vns)]
            for i, (hh, _) in enumerate(chains):
                o_s[hh, pl.ds(ops[i]["r0"] + los[i], CHUNK), :] += o_cs[i]
            states = [s * o["egl"][lo:lo + 1, :]
                      + _mm(o["kdt"], jnp.where((rowi >= lo) & (rowi < lo + CHUNK), vn, 0.0))
                      for s, o, lo, vn in zip(states, ops, los, vns)]
        return tuple(states)

    if zero_init:
        states = tuple(jnp.zeros((DK, DV), jnp.float32) for _ in range(2 * hpb))
    else:
        states = tuple(s0_ref[0, d, hh] for hh in range(hpb) for d in range(2))

    chains = [(hh, d) for hh in range(hpb) for d in range(2)]

    def body(m, carry):
        return run_chains([m if d == 0 else n_sc - 1 - m for _, d in chains], carry)

    if n_sc == 1:
        states = body(0, states)
    else:
        states = lax.fori_loop(0, n_sc, body, states)

    for hh in range(hpb):
        for d in range(2):
            st_ref[0, d, hh] = states[2 * hh + d]
        o = o_s[hh]
        o = o * lax.rsqrt(jnp.mean(o * o, axis=-1, keepdims=True) + EPS) * og_ref[...]
        zz = z_ref[:, hh * DV:(hh + 1) * DV]
        o_ref[:, hh * DV:(hh + 1) * DV] = o * (zz * jax.nn.sigmoid(zz))


def _delta_call(qkv, z, ba, conv_w, a_log, dt_bias, onorm_g, s0, B, L, row_blk0):
    n_sc = L // SC
    t0 = row_blk0 * L
    bah = ba[t0:t0 + B * L, :4 * H_A].reshape(B, L, 4, H_A).transpose(0, 3, 1, 2)
    bar = bah.reshape(B, H_A, n_sc, SC, 4).transpose(0, 1, 2, 4, 3)
    scal = jnp.concatenate([a_log.reshape(-1), dt_bias.reshape(-1)]).astype(jnp.float32)
    hpb = max(1, min(H_A, DELTA_HEAD_ROWS // L))
    n_hb = H_A // hpb
    zero_init = s0 is None
    if zero_init:
        s0 = jnp.zeros((1, 2, hpb, DK, DV), jnp.float32)
        s0_spec = pl.BlockSpec((1, 2, hpb, DK, DV), lambda b, h, sc: (0, 0, 0, 0, 0))
    else:
        s0_spec = pl.BlockSpec((1, 2, hpb, DK, DV), lambda b, h, sc: (b, 0, h, 0, 0))

    def col(off):
        return pl.BlockSpec((L, hpb * DK), lambda b, h, sc: (row_blk0 + b, off * n_hb + h))

    def cw(off):
        return pl.BlockSpec((CONV_K, hpb * DK), lambda b, h, sc: (0, off * n_hb + h))

    kern = functools.partial(_delta_kernel, n_sc=n_sc, zero_init=zero_init, hpb=hpb)
    return pl.pallas_call(
        kern,
        grid_spec=pltpu.PrefetchScalarGridSpec(
            num_scalar_prefetch=1,
            grid=(B, n_hb),
            in_specs=[col(0), col(1), col(2),
                      pl.BlockSpec((L, hpb * DV), lambda b, h, sc: (row_blk0 + b, h)),
                      pl.BlockSpec((1, hpb, L, 4), lambda b, h, sc: (b, h, 0, 0)),
                      pl.BlockSpec((1, hpb, n_sc, 4, SC), lambda b, h, sc: (b, h, 0, 0, 0)),
                      cw(0), cw(1), cw(2),
                      pl.BlockSpec((1, DV), lambda b, h, sc: (0, 0)),
                      s0_spec],
            out_specs=[pl.BlockSpec((L, hpb * DV), lambda b, h, sc: (b, h)),
                       pl.BlockSpec((1, 2, hpb, DK, DV), lambda b, h, sc: (b, 0, h, 0, 0))],
            scratch_shapes=[pltpu.VMEM((hpb, L, DK), jnp.float32), pltpu.VMEM((hpb, L, DK), jnp.float32),
                            pltpu.VMEM((hpb, L, DV), jnp.float32), pltpu.VMEM((hpb, L, DV), jnp.float32),
                            pltpu.VMEM((2 * hpb, SC, DV), jnp.float32)]),
        out_shape=[jax.ShapeDtypeStruct((B * L, D_A), jnp.float32),
                   jax.ShapeDtypeStruct((B, 2, H_A, DK, DV), jnp.float32)],
        compiler_params=pltpu.CompilerParams(dimension_semantics=("arbitrary", "arbitrary"),
                                             vmem_limit_bytes=VMEM_LIMIT),
    )(scal, qkv, qkv, qkv, z, bah, bar, conv_w, conv_w, conv_w, onorm_g.reshape(1, DV), s0)


TR = 256
GSZ = N_EXPERTS // N_GROUPS
NEG = -jnp.inf


def _col_to_row(col, eye_mask):
    return jnp.sum(jnp.where(eye_mask, col, jnp.zeros_like(col)), axis=0, keepdims=True)


def _route_kernel(h_ref, rwh_ref, rwl_ref, rb_ref, idx_ref, rank_ref, w_ref, cnt_ref, cnt_s):
    i = pl.program_id(0)

    @pl.when(i == 0)
    def _():
        cnt_s[...] = jnp.zeros_like(cnt_s)

    h = h_ref[...]
    hh, hl = _split_bf16(h)
    logits = _bdot(hh, rwh_ref[...]) + (_bdot(hh, rwl_ref[...]) + _bdot(hl, rwh_ref[...]))
    scores = jax.nn.sigmoid(logits)
    sel = scores + rb_ref[...]
    lane = lax.broadcasted_iota(jnp.int32, sel.shape, 1)
    gid = lane // GSZ

    def first_argmax(v):
        m = jnp.max(v, axis=1, keepdims=True)
        first = jnp.min(jnp.where(v == m, lane, N_EXPERTS), axis=1, keepdims=True)
        return m, first

    gscore = []
    for g in range(N_GROUPS):
        vg = jnp.where(gid == g, sel, NEG)
        m1, i1 = first_argmax(vg)
        m2 = jnp.max(jnp.where(lane == i1, NEG, vg), axis=1, keepdims=True)
        gscore.append(m1 + m2)
    emask = jnp.zeros(sel.shape, jnp.bool_)
    for g in range(N_GROUPS):
        beat = jnp.zeros(gscore[g].shape, jnp.int32)
        for o in range(N_GROUPS):
            if o == g:
                continue
            wins = (gscore[o] > gscore[g]) | ((gscore[o] == gscore[g]) & (o < g))
            beat = beat + wins.astype(jnp.int32)
        emask = emask | ((gid == g) & (beat < TOPK_GROUP))
    cand = jnp.where(emask, sel, NEG)
    chosen = []
    picked = jnp.zeros(sel.shape, jnp.bool_)
    for _ in range(TOP_K):
        _, ik = first_argmax(cand)
        hit = lane == ik
        chosen.append((ik, hit))
        picked = picked | hit
        cand = jnp.where(hit, NEG, cand)
    wraw = jnp.where(picked, scores, 0.0)
    wmat = wraw / jnp.sum(wraw, axis=1, keepdims=True) * ROUTED_SCALE

    pm = picked.astype(jnp.bfloat16)
    ri = lax.broadcasted_iota(jnp.int32, (TR, TR), 0)
    ci = lax.broadcasted_iota(jnp.int32, (TR, TR), 1)
    earlier = (ci < ri).astype(jnp.bfloat16)
    rank_mat = _bdot(earlier, pm) + cnt_s[...]
    cnt_s[...] = cnt_s[...] + jnp.sum(picked.astype(jnp.float32), axis=0, keepdims=True)
    cnt_ref[...] = cnt_s[...].astype(jnp.int32)

    eye = ri == ci
    lane8 = lax.broadcasted_iota(jnp.int32, (TR, TOP_K), 1)
    wcols = jnp.zeros((TR, TOP_K), jnp.float32)
    for k, (ik, hit) in enumerate(chosen):
        rk = jnp.sum(jnp.where(hit, rank_mat, 0.0), axis=1, keepdims=True)
        wk = jnp.sum(jnp.where(hit, wmat, 0.0), axis=1, keepdims=True)
        idx_ref[0, k:k + 1, :] = _col_to_row(ik, eye)
        rank_ref[0, k:k + 1, :] = _col_to_row(rk, eye).astype(jnp.int32)
        wcols = jnp.where(lane8 == k, wk, wcols)
    w_ref[...] = wcols


def _route_call(hf, router_w, router_bias):
    T, D = hf.shape
    n_tiles = T // TR
    rwh, rwl = _split_bf16(router_w)
    return pl.pallas_call(
        _route_kernel,
        grid=(n_tiles,),
        in_specs=[pl.BlockSpec((TR, D), lambda i: (i, 0)),
                  pl.BlockSpec((D, N_EXPERTS), lambda i: (0, 0)),
                  pl.BlockSpec((D, N_EXPERTS), lambda i: (0, 0)),
                  pl.BlockSpec((1, N_EXPERTS), lambda i: (0, 0))],
        out_specs=[pl.BlockSpec((1, TOP_K, TR), lambda i: (i, 0, 0)),
                   pl.BlockSpec((1, TOP_K, TR), lambda i: (i, 0, 0)),
                   pl.BlockSpec((TR, TOP_K), lambda i: (i, 0)),
                   pl.BlockSpec((1, N_EXPERTS), lambda i: (0, 0))],
        scratch_shapes=[pltpu.VMEM((1, N_EXPERTS), jnp.float32)],
        out_shape=[jax.ShapeDtypeStruct((n_tiles, TOP_K, TR), jnp.int32),
                   jax.ShapeDtypeStruct((n_tiles, TOP_K, TR), jnp.int32),
                   jax.ShapeDtypeStruct((T, TOP_K), jnp.float32),
                   jax.ShapeDtypeStruct((1, N_EXPERTS), jnp.int32)],
        compiler_params=pltpu.CompilerParams(dimension_semantics=("arbitrary",)),
    )(hf, rwh, rwl, router_bias.reshape(1, N_EXPERTS).astype(jnp.float32))


def _dispatch_kernel(idx_ref, rank_ref, pstart_ref, h_ref, xs_init, pos_ref, xs_hbm, pos_v, pos_s, ssem, psem):
    del xs_init
    erow = lax.broadcasted_iota(jnp.int32, (N_EXPERTS, TR), 0)
    pstart = pstart_ref[...]
    for k in range(TOP_K):
        hit = erow == idx_ref[0, k:k + 1, :]
        seg = jnp.sum(jnp.where(hit, pstart, 0), axis=0, keepdims=True)
        pos_v[k:k + 1, :] = seg + rank_ref[0, k:k + 1, :]
    pos_ref[0] = pos_v[...]
    cp = pltpu.make_async_copy(pos_v, pos_s, psem)
    cp.start()
    cp.wait()

    def body(t, carry):
        for k in range(TOP_K):
            pltpu.make_async_copy(h_ref.at[t], xs_hbm.at[pos_s[k, t]], ssem).start()
        return carry

    lax.fori_loop(0, TR, body, 0, unroll=8)
    n_rows = TR * TOP_K
    pltpu.make_async_copy(xs_hbm.at[pl.ds(0, n_rows)], xs_hbm.at[pl.ds(0, n_rows)], ssem).wait()


def _dispatch_call(hf, idx, rank, pad_start, n_pad):
    T, D = hf.shape
    n_tiles = T // TR
    return pl.pallas_call(
        _dispatch_kernel,
        grid=(n_tiles,),
        in_specs=[pl.BlockSpec((1, TOP_K, TR), lambda i: (i, 0, 0)),
                  pl.BlockSpec((1, TOP_K, TR), lambda i: (i, 0, 0)),
                  pl.BlockSpec((N_EXPERTS, 1), lambda i: (0, 0)),
                  pl.BlockSpec((TR, D), lambda i: (i, 0)),
                  pl.BlockSpec(memory_space=pl.ANY)],
        out_specs=[pl.BlockSpec((1, TOP_K, TR), lambda i: (i, 0, 0)),
                   pl.BlockSpec(memory_space=pl.ANY)],
        scratch_shapes=[pltpu.VMEM((TOP_K, TR), jnp.int32), pltpu.SMEM((TOP_K, TR), jnp.int32),
                        pltpu.SemaphoreType.DMA, pltpu.SemaphoreType.DMA],
        out_shape=[jax.ShapeDtypeStruct((n_tiles, TOP_K, TR), jnp.int32),
                   jax.ShapeDtypeStruct((n_pad, D), hf.dtype)],
        input_output_aliases={4: 1},
        compiler_params=pltpu.CompilerParams(dimension_semantics=("arbitrary",)),
    )(idx, rank, pad_start.reshape(N_EXPERTS, 1), hf, jnp.zeros((n_pad, D), hf.dtype))


BM = 256


def _expert_kernel(blk_e_ref, nvalid_ref, nused_ref, x_ref, wg_ref, wu_ref, wd_ref, y_ref, wg_s, wu_s, wd_s):
    i = pl.program_id(0)

    @pl.when(i < nused_ref[0])
    def _():
        e = blk_e_ref[i]
        prev = blk_e_ref[jnp.maximum(i - 1, 0)]

        @pl.when((i == 0) | (e != prev))
        def _():
            wg_s[...] = wg_ref[0].astype(jnp.bfloat16)
            wu_s[...] = wu_ref[0].astype(jnp.bfloat16)
            wd_s[...] = wd_ref[0].astype(jnp.bfloat16)

        row = lax.broadcasted_iota(jnp.int32, (BM, 1), 0)
        xa, xb = _unpack_rows(jnp.where(row < nvalid_ref[i], x_ref[...], jnp.uint32(0)))
        xa = xa.astype(jnp.bfloat16)
        xb = xb.astype(jnp.bfloat16)
        half = xa.shape[1]
        g = _bdot(xa, wg_s[:half, :]) + _bdot(xb, wg_s[half:, :])
        u = _bdot(xa, wu_s[:half, :]) + _bdot(xb, wu_s[half:, :])
        a = (g * jax.nn.sigmoid(g)) * u
        y_ref[...] = _pack_rows(_bdot(a.astype(jnp.bfloat16), wd_s[...]))

    @pl.when(i >= nused_ref[0])
    def _():
        y_ref[...] = jnp.zeros_like(y_ref)


def _expert_call(x_sorted, blk_e, n_valid, n_used, w_gate, w_up, w_down):
    n_pad, DH = x_sorted.shape
    n_blk = n_pad // BM
    E, D, F = w_gate.shape

    def row_map(i, be, nv, nu):
        return (jnp.minimum(i, nu[0] - 1), 0)

    def w_map(i, be, nv, nu):
        return (be[jnp.minimum(i, nu[0] - 1)], 0, 0)

    return pl.pallas_call(
        _expert_kernel,
        grid_spec=pltpu.PrefetchScalarGridSpec(
            num_scalar_prefetch=3,
            grid=(n_blk,),
            in_specs=[pl.BlockSpec((BM, DH), row_map),
                      pl.BlockSpec((1, D, F), w_map),
                      pl.BlockSpec((1, D, F), w_map),
                      pl.BlockSpec((1, F, D), w_map)],
            out_specs=pl.BlockSpec((BM, DH), lambda i, be, nv, nu: (i, 0)),
            scratch_shapes=[pltpu.VMEM((D, F), jnp.bfloat16), pltpu.VMEM((D, F), jnp.bfloat16),
                            pltpu.VMEM((F, D), jnp.bfloat16)]),
        out_shape=jax.ShapeDtypeStruct((n_pad, DH), jnp.uint32),
        compiler_params=pltpu.CompilerParams(dimension_semantics=("arbitrary",),
                                             vmem_limit_bytes=VMEM_LIMIT),
    )(blk_e, n_valid, n_used, x_sorted, w_gate, w_up, w_down)


TC = 128


def _combine_kernel(pos_hbm, y_hbm, w_ref, x1_ref, sh_ref, mod_ref, fg_ref, out_ref, ybuf, pos_s, gsem, psem):
    j = pl.program_id(0)
    last = pl.num_programs(0) - 1

    def pos_copy(b, slot):
        return pltpu.make_async_copy(pos_hbm.at[b], pos_s.at[slot], psem.at[slot])

    def start_gather(slot):
        for k in range(TOP_K):
            for t in range(TC):
                pltpu.make_async_copy(y_hbm.at[pos_s[slot, k * TC + t]], ybuf.at[slot, k, t],
                                      gsem.at[slot]).start()

    def wait_gather(slot):
        pltpu.make_async_copy(ybuf.at[slot], ybuf.at[slot], gsem.at[slot]).wait()

    @pl.when(j == 0)
    def _():
        pos_copy(0, 0).start()
        pos_copy(0, 0).wait()
        start_gather(0)
        pos_copy(jnp.minimum(1, last), 1).start()

    slot = j % 2
    nslot = 1 - slot
    pos_copy(0, nslot).wait()
    start_gather(nslot)
    pos_copy(jnp.minimum(j + 2, last), slot).start()
    wait_gather(slot)
    w = w_ref[...]
    acc_a = acc_b = None
    for k in range(TOP_K):
        ya, yb = _unpack_rows(ybuf[slot, k])
        acc_a = w[:, k:k + 1] * ya if k == 0 else acc_a + w[:, k:k + 1] * ya
        acc_b = w[:, k:k + 1] * yb if k == 0 else acc_b + w[:, k:k + 1] * yb
    acc = jnp.concatenate([acc_a, acc_b], axis=1)
    x2 = x1_ref[...] + mod_ref[0, 5:6, :] * (acc + sh_ref[...])
    out_ref[...] = x2 * lax.rsqrt(jnp.mean(x2 * x2, axis=-1, keepdims=True) + EPS) * fg_ref[...]

    @pl.when(j == last)
    def _():
        wait_gather(nslot)
        pos_copy(0, slot).wait()


def _combine_call(y_sorted, pos_t, wts, x1, shared, mod, final_g, n_ctx, lat_len):
    T, K = wts.shape
    DH = y_sorted.shape[1]
    D = 2 * DH
    n_tiles = T // TC
    row = functools.partial(_mod_row, tokens_per_tile=TC, n_ctx=n_ctx, lat_len=lat_len)
    return pl.pallas_call(
        _combine_kernel,
        grid=(n_tiles,),
        in_specs=[pl.BlockSpec(memory_space=pl.ANY),
                  pl.BlockSpec(memory_space=pl.ANY),
                  pl.BlockSpec((TC, K), lambda j: (j, 0)),
                  pl.BlockSpec((TC, D), lambda j: (j, 0)),
                  pl.BlockSpec((TC, D), lambda j: (j, 0)),
                  pl.BlockSpec((1, N_MOD, D), lambda j: (row(j), 0, 0)),
                  pl.BlockSpec((1, D), lambda j: (0, 0))],
        out_specs=pl.BlockSpec((TC, D), lambda j: (j, 0)),
        scratch_shapes=[pltpu.VMEM((2, K, TC, DH), jnp.uint32),
                        pltpu.SMEM((2, K * TC), jnp.int32),
                        pltpu.SemaphoreType.DMA((2,)), pltpu.SemaphoreType.DMA((2,))],
        out_shape=jax.ShapeDtypeStruct((T, D), jnp.float32),
        compiler_params=pltpu.CompilerParams(dimension_semantics=("arbitrary",)),
    )(pos_t, y_sorted, wts, x1, shared, mod, final_g.reshape(1, D))


def _moe_routed(h2, h2p, router_w, router_bias, w_gate, w_up, w_down):
    T, D = h2.shape
    idx, rank, wts, cnt = _route_call(h2, router_w, router_bias)
    counts = cnt[0]
    padded = (counts + BM - 1) // BM * BM
    pad_end = jnp.cumsum(padded)
    pad_start = (pad_end - padded).astype(jnp.int32)
    n_pad = T * TOP_K + N_EXPERTS * BM
    n_blk = n_pad // BM
    pos, x_sorted = _dispatch_call(h2p, idx, rank, pad_start, n_pad)
    blk_row0 = jnp.arange(n_blk, dtype=jnp.int32) * BM
    blk_e = jnp.minimum(jnp.sum((pad_end[None, :] <= blk_row0[:, None]).astype(jnp.int32), axis=1), N_EXPERTS - 1)
    own = blk_e[:, None] == jnp.arange(N_EXPERTS, dtype=jnp.int32)[None, :]
    seg_end = jnp.sum(jnp.where(own, (pad_start + counts)[None, :], 0), axis=1)
    n_valid = jnp.clip(seg_end - blk_row0, 0, BM).astype(jnp.int32)
    n_used = (pad_end[-1] // BM).astype(jnp.int32).reshape(1)
    y = _expert_call(x_sorted, blk_e, n_valid, n_used, w_gate, w_up, w_down)
    pos_t = pos.reshape(T // TR, TOP_K, TR // TC, TC).transpose(0, 2, 1, 3).reshape(T // TC, TOP_K * TC)
    return y, pos_t, wts


def kernel(x_prompt, x_sample, state_delta, c, c_ctx, w_ada, b_ada, norm1_g, w_in, conv_w, a_log,
           dt_bias, onorm_g, pool_w, pool_scale, w_out, norm2_g, router_w, router_bias, exp_w_gate,
           exp_w_up, exp_w_down, sh_w_gate, sh_w_up, sh_w_down, final_g):
    Bc, Lc, D = x_prompt.shape
    Bl, Ll, _ = x_sample.shape
    n_ctx = Bc * Lc
    assert DEPTH == 1 and 1 + Bl <= MOD_ROWS and n_ctx % Ll == 0
    x_parts = (x_prompt.reshape(n_ctx, D), x_sample.reshape(Bl * Ll, D))
    cvec = jnp.concatenate([c_ctx[None], c, jnp.zeros((MOD_ROWS - 1 - Bl, D), c.dtype)], axis=0)
    l = 0
    mod = _ada_call(cvec, w_ada[l], b_ada[l]).reshape(MOD_ROWS, N_MOD, D)
    qkv, z, ba, u = _inproj_call(*x_parts, mod, norm1_g[l], w_in[l], Ll)
    dn = (conv_w[l], a_log[l], dt_bias[l], onorm_g[l])
    oa_c, st_ctx = _delta_call(qkv, z, ba, *dn, None, Bc, Lc, 0)
    oa_l, _ = _delta_call(qkv, z, ba, *dn, state_delta[:, l], Bl, Ll, n_ctx // Ll)
    op_c = _pool_call(u, pool_w[l], pool_scale[l], False, Bc, Lc, 0)
    op_l = _pool_call(u, pool_w[l], pool_scale[l], True, Bl, Ll, n_ctx // Ll)
    x1, h2, h2p, shared = _outproj_call(x_parts, (oa_c, oa_l), (op_c, op_l), mod, norm2_g[l], w_out[l],
                                        sh_w_gate[l], sh_w_up[l], sh_w_down[l], Ll)
    y, pos_t, wts = _moe_routed(h2, h2p, router_w[l], router_bias[l], exp_w_gate[l], exp_w_up[l],
                                exp_w_down[l])
    out = _combine_call(y, pos_t, wts, x1, shared, mod, final_g, n_ctx, Ll)
    y_prompt = out[:n_ctx].reshape(Bc, Lc, D)
    y_sample = out[n_ctx:].reshape(Bl, Ll, D)
    new_state_delta = st_ctx[:, None].astype(x_prompt.dtype)
    return (y_prompt, y_sample, new_state_delta)
```

```python
import functools
import jax, jax.numpy as jnp
from jax import lax
from jax.experimental import pallas as pl
from jax.experimental.pallas import tpu as pltpu

D_MODEL = 1024
DEPTH = 1
GRID_W = 64
D_MIX = D_MODEL
D_A = D_MIX // 2
D_P = D_MIX - D_A
H_A = 4
DK = D_A // H_A
DV = D_A // H_A
CONV_K = 5
CHUNK = 64
POOL_WINDOWS = (2, 4, 8, 16)
N_PG = len(POOL_WINDOWS)
PG = D_P // N_PG
N_EXPERTS = 256
TOP_K = 8
N_GROUPS = 8
TOPK_GROUP = 4
ROUTED_SCALE = 2.5
EPS = 1e-6
VMEM_LIMIT = 48 * 1024 * 1024


def _split_bf16(a):
    hi = a.astype(jnp.bfloat16)
    return hi, (a - hi.astype(jnp.float32)).astype(jnp.bfloat16)


def _bdot(a, b):
    return jnp.dot(a, b, preferred_element_type=jnp.float32)


def _pack_rows(x):
    m = x.shape[1] // 2
    hi = lax.bitcast_convert_type(x[:, :m].astype(jnp.bfloat16).astype(jnp.float32), jnp.uint32)
    lo = lax.bitcast_convert_type(x[:, m:].astype(jnp.bfloat16).astype(jnp.float32), jnp.uint32)
    return hi | (lo >> 16)


def _unpack_rows(p):
    hi = lax.bitcast_convert_type(p & jnp.uint32(0xFFFF0000), jnp.float32)
    lo = lax.bitcast_convert_type(p << 16, jnp.float32)
    return hi, lo


N_MOD = 6
MOD_ROWS = 8
TM = 512


def _ada_kernel(c_ref, w_ref, b_ref, o_ref):
    c = c_ref[...]
    s = c * jax.nn.sigmoid(c)
    sh, sl = _split_bf16(s)
    wh, wl = _split_bf16(w_ref[...])
    o_ref[...] = _bdot(sh, wh) + (_bdot(sh, wl) + _bdot(sl, wh)) + b_ref[...]


def _ada_call(cvec, w_ada, b_ada):
    R, D = cvec.shape
    N = w_ada.shape[1]
    tn = 1024
    return pl.pallas_call(
        _ada_kernel,
        grid=(N // tn,),
        in_specs=[pl.BlockSpec((R, D), lambda j: (0, 0)),
                  pl.BlockSpec((D, tn), lambda j: (0, j)),
                  pl.BlockSpec((1, tn), lambda j: (0, j))],
        out_specs=pl.BlockSpec((R, tn), lambda j: (0, j)),
        out_shape=jax.ShapeDtypeStruct((R, N), jnp.float32),
    )(cvec, w_ada, b_ada.reshape(1, N))


def _mod_row(tile, tokens_per_tile, n_ctx, lat_len):
    t0 = tile * tokens_per_tile
    return jnp.where(t0 < n_ctx, 0, 1 + (t0 - n_ctx) // lat_len)


def _two_part_specs(n_ctx_tiles, width):
    return (pl.BlockSpec((TM, width), lambda i: (jnp.minimum(i, n_ctx_tiles - 1), 0)),
            pl.BlockSpec((TM, width), lambda i: (jnp.maximum(i - n_ctx_tiles, 0), 0)))


def _pick(n_ctx_tiles, ctx_ref, lat_ref):
    return jnp.where(pl.program_id(0) < n_ctx_tiles, ctx_ref[...], lat_ref[...])


def _inproj_kernel(xc_ref, xl_ref, mod_ref, g_ref, wq_ref, wz_ref, wb_ref, wu_ref, q_ref, z_ref, b_ref, u_ref,
                   *, n_ctx_tiles):
    x = _pick(n_ctx_tiles, xc_ref, xl_ref)
    y = x * lax.rsqrt(jnp.mean(x * x, axis=-1, keepdims=True) + EPS) * g_ref[...]
    h = (y * (1.0 + mod_ref[0, 1:2, :]) + mod_ref[0, 0:1, :]).astype(jnp.bfloat16)
    q_ref[...] = _bdot(h, wq_ref[...])
    z_ref[...] = _bdot(h, wz_ref[...])
    b_ref[...] = _bdot(h, wb_ref[...])
    u_ref[...] = _bdot(h, wu_ref[...])


def _inproj_call(x_ctx, x_lat, mod, norm1_g, w_in, lat_len):
    n_ctx, D = x_ctx.shape
    T = n_ctx + x_lat.shape[0]
    bf = jnp.bfloat16
    nq, nz, nb = 3 * D_A, D_A, 4 * H_A
    wq = w_in[:, :nq].astype(bf)
    wz = w_in[:, nq:nq + nz].astype(bf)
    wb = jnp.pad(w_in[:, nq + nz:nq + nz + nb], ((0, 0), (0, 128 - nb))).astype(bf)
    wu = w_in[:, nq + nz + nb:].astype(bf)
    row = functools.partial(_mod_row, tokens_per_tile=TM, n_ctx=n_ctx, lat_len=lat_len)

    def full(a):
        return pl.BlockSpec(a.shape, lambda i: (0, 0))

    def rows(n):
        return pl.BlockSpec((TM, n), lambda i: (i, 0))

    return pl.pallas_call(
        functools.partial(_inproj_kernel, n_ctx_tiles=n_ctx // TM),
        grid=(T // TM,),
        in_specs=[*_two_part_specs(n_ctx // TM, D), pl.BlockSpec((1, N_MOD, D), lambda i: (row(i), 0, 0)),
                  pl.BlockSpec((1, D), lambda i: (0, 0)), full(wq), full(wz), full(wb), full(wu)],
        out_specs=[rows(nq), rows(nz), rows(128), rows(D_P)],
        out_shape=[jax.ShapeDtypeStruct((T, nq), jnp.float32), jax.ShapeDtypeStruct((T, nz), jnp.float32),
                   jax.ShapeDtypeStruct((T, 128), jnp.float32), jax.ShapeDtypeStruct((T, D_P), jnp.float32)],
        compiler_params=pltpu.CompilerParams(dimension_semantics=("arbitrary",),
                                             vmem_limit_bytes=VMEM_LIMIT),
    )(x_ctx, x_lat, mod, norm1_g.reshape(1, D), wq, wz, wb, wu)


PT = 256


def _window_bounds(pos, w, n):
    return jnp.maximum(pos - w // 2, 0), jnp.minimum(pos + w - w // 2, n)


def _band_sum(band, x):
    xh, xl = _split_bf16(x)
    return _bdot(band, xh) + _bdot(band, xl)


def _pool_seq_kernel(u_ref, pw_ref, ps_ref, o_ref):
    L = u_ref.shape[0]
    ti = lax.broadcasted_iota(jnp.int32, (L, L), 0)
    ji = lax.broadcasted_iota(jnp.int32, (L, L), 1)
    tcol = lax.broadcasted_iota(jnp.int32, (L, 1), 0)
    for i, w in enumerate(POOL_WINDOWS):
        lo, hi = _window_bounds(ti, w, L)
        band = ((ji >= lo) & (ji < hi)).astype(jnp.bfloat16)
        clo, chi = _window_bounds(tcol, w, L)
        ug = u_ref[:, i * PG:(i + 1) * PG]
        mean = _band_sum(band, ug) / (chi - clo).astype(jnp.float32)
        d = (mean - ug).astype(jnp.bfloat16)
        o_ref[:, i * PG:(i + 1) * PG] = _bdot(d, pw_ref[i]) * ps_ref[:, i * PG:(i + 1) * PG]


def _pool_grid_kernel(u_ref, pw_ref, ps_ref, o_ref, pad_s, r_s):
    L = u_ref.shape[0]
    rows = L // GRID_W
    halo = (max(POOL_WINDOWS) // 2) * GRID_W
    pad_s[0:halo, :] = jnp.zeros((halo, D_P), jnp.float32)
    pad_s[halo + L:, :] = jnp.zeros((halo, D_P), jnp.float32)
    pad_s[halo:halo + L, :] = u_ref[...]
    ti = lax.broadcasted_iota(jnp.int32, (PT, PT), 0)
    ji = lax.broadcasted_iota(jnp.int32, (PT, PT), 1)
    tcol = lax.broadcasted_iota(jnp.int32, (PT, 1), 0)
    for i, w in enumerate(POOL_WINDOWS):
        cs = slice(i * PG, (i + 1) * PG)
        acc = None
        for dr in range(-(w // 2), w - w // 2):
            part = pad_s[halo + dr * GRID_W:halo + dr * GRID_W + L, cs]
            acc = part if acc is None else acc + part
        r_s[...] = acc
        lo, hi = _window_bounds(ti % GRID_W, w, GRID_W)
        band = ((ji // GRID_W == ti // GRID_W) & (ji % GRID_W >= lo) & (ji % GRID_W < hi)).astype(jnp.bfloat16)
        clo, chi = _window_bounds(tcol % GRID_W, w, GRID_W)
        ccnt = (chi - clo).astype(jnp.float32)
        for tile in range(L // PT):
            ts = slice(tile * PT, (tile + 1) * PT)
            rlo, rhi = _window_bounds(tile * (PT // GRID_W) + tcol // GRID_W, w, rows)
            mean = _band_sum(band, r_s[ts, :]) / ((rhi - rlo).astype(jnp.float32) * ccnt)
            d = (mean - u_ref[ts, cs]).astype(jnp.bfloat16)
            o_ref[ts, cs] = _bdot(d, pw_ref[i]) * ps_ref[:, cs]


def _pool_call(u, pool_w, pool_scale, grid, B, L, row_blk0):
    pw = pool_w.astype(jnp.bfloat16)
    ps = pool_scale.reshape(1, D_P)
    specs = dict(
        grid=(B,),
        in_specs=[pl.BlockSpec((L, D_P), lambda b: (row_blk0 + b, 0)),
                  pl.BlockSpec((N_PG, PG, PG), lambda b: (0, 0, 0)),
                  pl.BlockSpec((1, D_P), lambda b: (0, 0))],
        out_specs=pl.BlockSpec((L, D_P), lambda b: (b, 0)),
        out_shape=jax.ShapeDtypeStruct((B * L, D_P), jnp.float32),
        compiler_params=pltpu.CompilerParams(dimension_semantics=("arbitrary",),
                                             vmem_limit_bytes=VMEM_LIMIT))
    if not grid:
        return pl.pallas_call(_pool_seq_kernel, **specs)(u, pw, ps)
    halo = (max(POOL_WINDOWS) // 2) * GRID_W
    return pl.pallas_call(
        _pool_grid_kernel,
        scratch_shapes=[pltpu.VMEM((L + 2 * halo, D_P), jnp.float32), pltpu.VMEM((L, PG), jnp.float32)],
        **specs)(u, pw, ps)


def _outproj_kernel(xc_ref, xl_ref, oac_ref, oal_ref, opc_ref, opl_ref, mod_ref, g2_ref, wo_ref, sg_ref, su_ref,
                    sd_ref, x1_ref, h2_ref, h2p_ref, sh_ref, *, n_ctx_tiles):
    o_a = _pick(n_ctx_tiles, oac_ref, oal_ref)
    o_p = _pick(n_ctx_tiles, opc_ref, opl_ref)
    mix = (_bdot(o_a.astype(jnp.bfloat16), wo_ref[:D_A, :])
           + _bdot(o_p.astype(jnp.bfloat16), wo_ref[D_A:, :]))
    x1 = _pick(n_ctx_tiles, xc_ref, xl_ref) + mod_ref[0, 2:3, :] * mix
    x1_ref[...] = x1
    y = x1 * lax.rsqrt(jnp.mean(x1 * x1, axis=-1, keepdims=True) + EPS) * g2_ref[...]
    h2 = y * (1.0 + mod_ref[0, 4:5, :]) + mod_ref[0, 3:4, :]
    h2_ref[...] = h2
    h2p_ref[...] = _pack_rows(h2)
    hb = h2.astype(jnp.bfloat16)
    g = _bdot(hb, sg_ref[...])
    a = (g * jax.nn.sigmoid(g)) * _bdot(hb, su_ref[...])
    sh_ref[...] = _bdot(a.astype(jnp.bfloat16), sd_ref[...])


def _outproj_call(x_parts, oa_parts, op_parts, mod, norm2_g, w_out, sh_gate, sh_up, sh_down, lat_len):
    n_ctx, D = x_parts[0].shape
    T = n_ctx + x_parts[1].shape[0]
    nct = n_ctx // TM
    bf = jnp.bfloat16
    row = functools.partial(_mod_row, tokens_per_tile=TM, n_ctx=n_ctx, lat_len=lat_len)
    ws = [w_out.astype(bf), sh_gate.astype(bf), sh_up.astype(bf), sh_down.astype(bf)]

    def rows(n):
        return pl.BlockSpec((TM, n), lambda i: (i, 0))

    return pl.pallas_call(
        functools.partial(_outproj_kernel, n_ctx_tiles=nct),
        grid=(T // TM,),
        in_specs=[*_two_part_specs(nct, D), *_two_part_specs(nct, D_A), *_two_part_specs(nct, D_P),
                  pl.BlockSpec((1, N_MOD, D), lambda i: (row(i), 0, 0)),
                  pl.BlockSpec((1, D), lambda i: (0, 0))] + [pl.BlockSpec(w.shape, lambda i: (0, 0)) for w in ws],
        out_specs=[rows(D), rows(D), rows(D // 2), rows(D)],
        out_shape=[jax.ShapeDtypeStruct((T, D), jnp.float32), jax.ShapeDtypeStruct((T, D), jnp.float32),
                   jax.ShapeDtypeStruct((T, D // 2), jnp.uint32), jax.ShapeDtypeStruct((T, D), jnp.float32)],
        compiler_params=pltpu.CompilerParams(dimension_semantics=("arbitrary",),
                                             vmem_limit_bytes=VMEM_LIMIT),
    )(*x_parts, *oa_parts, *op_parts, mod, norm2_g.reshape(1, D), *ws)


SC = 256
CPS = SC // CHUNK
BASE = 16
DELTA_HEAD_ROWS = 4096


def _mm(a, b):
    return jnp.dot(a.astype(jnp.bfloat16), b.astype(jnp.bfloat16), preferred_element_type=jnp.float32)


def _mm_nt(a, b):
    return lax.dot_general(a.astype(jnp.bfloat16), b.astype(jnp.bfloat16), (((1,), (1,)), ((), ())),
                           preferred_element_type=jnp.float32)


def _softplus(x):
    return jnp.maximum(x, 0.0) + jnp.log(1.0 + jnp.exp(-jnp.abs(x)))


def _delta_kernel(sc_ref, xq_ref, xk_ref, xv_ref, z_ref, bac_ref, bar_ref, cwq_ref, cwk_ref, cwv_ref,
                  og_ref, s0_ref, o_ref, st_ref, q_s, k_s, v_s, o_s, vn_s, *, n_sc, zero_init, hpb):
    hb = pl.program_id(1)
    L = q_s.shape[1]

    def conv(x_ref, w_ref, cs):
        x = x_ref[:, cs]
        row = lax.broadcasted_iota(jnp.int32, x.shape, 0)
        acc = x * w_ref[CONV_K // 2:CONV_K // 2 + 1, cs]
        for j in range(CONV_K):
            d = j - CONV_K // 2
            if d == 0:
                continue
            xs = pltpu.roll(x, (-d) % L, 0)
            ok = (row + d >= 0) & (row + d < L)
            acc = acc + jnp.where(ok, xs, 0.0) * w_ref[j:j + 1, cs]
        return acc * jax.nn.sigmoid(acc)

    for hh in range(hpb):
        cs = slice(hh * DK, (hh + 1) * DK)
        q = conv(xq_ref, cwq_ref, cs)
        q_s[hh] = q * lax.rsqrt(jnp.sum(q * q, axis=-1, keepdims=True) + EPS) * (DK ** -0.5)
        k = conv(xk_ref, cwk_ref, cs)
        k_s[hh] = k * lax.rsqrt(jnp.sum(k * k, axis=-1, keepdims=True) + EPS)
        v_s[hh] = conv(xv_ref, cwv_ref, cs)
    o_s[...] = jnp.zeros_like(o_s)

    ri = lax.broadcasted_iota(jnp.int32, (SC, SC), 0)
    ci = lax.broadcasted_iota(jnp.int32, (SC, SC), 1)
    same = (ri // CHUNK) == (ci // CHUNK)
    same_base = (ri // BASE) == (ci // BASE)
    merge_masks = [(ri // w) == (ci // w) for w in (2 * BASE, CHUNK)]
    eye = (ri == ci).astype(jnp.float32)
    rowi = lax.broadcasted_iota(jnp.int32, (SC, DV), 0)

    def prep(m, d, hh):
        r0 = pl.multiple_of(m * SC, SC)
        h = hb * hpb + hh
        q = q_s[hh, pl.ds(r0, SC), :]
        k = k_s[hh, pl.ds(r0, SC), :]
        v = v_s[hh, pl.ds(r0, SC), :]
        bc = bac_ref[0, hh, pl.ds(r0, SC), :]
        br = bar_ref[0, hh, m]
        a_l = sc_ref[d * H_A + h]
        dtb = sc_ref[2 * H_A + d * H_A + h]
        neg_ea = -jnp.exp(jnp.full((1, 1), a_l, jnp.float32))
        beta = jax.nn.sigmoid(bc[:, d:d + 1])
        g_col = neg_ea * _softplus(bc[:, 2 + d:3 + d] + dtb)
        g_row = neg_ea * _softplus(br[2 + d:3 + d, :] + dtb)
        if d == 0:
            tri, strict = same & (ci <= ri), same & (ci < ri)
        else:
            tri, strict = same & (ci >= ri), same & (ci > ri)
        tri_t = same & (ri <= ci) if d == 0 else same & (ri >= ci)
        gc_col = jnp.sum(jnp.where(tri, g_row, 0.0), axis=1, keepdims=True)
        gc_row = jnp.sum(jnp.where(tri_t, g_col, 0.0), axis=0, keepdims=True)
        gl_col = jnp.sum(jnp.where(same, g_row, 0.0), axis=1, keepdims=True)
        decay = jnp.where(tri, jnp.exp(jnp.where(tri, gc_col - gc_row, 0.0)), 0.0)
        kb = k * beta
        a = jnp.where(strict, _mm_nt(kb, k) * decay, 0.0)
        attn = jnp.where(tri, _mm_nt(q, k) * decay, 0.0)
        eg = jnp.exp(gc_col)
        x = jnp.concatenate([v * beta, kb * eg], axis=1)
        qd = q * eg
        kdt = (k * jnp.exp(gl_col - gc_col)).T
        return dict(r0=r0, a=a, attn=attn, x=x, qd=qd, kdt=kdt, egl=jnp.exp(gl_col))

    def run_chains(ms, states):
        n = len(chains)
        ops = [prep(ms[i], d, hh) for i, (hh, d) in enumerate(chains)]
        ps = [jnp.where(same_base, o["a"], 0.0) for o in ops]
        ts = [eye - p for p in ps]
        for _ in range(BASE.bit_length() - 2):
            ps = [_mm(p, p) for p in ps]
            ts = [t + _mm(t, p) for t, p in zip(ts, ps)]
        inner = same_base
        for outer in merge_masks:
            lows = [_mm(jnp.where(outer & ~inner, o["a"], 0.0), t) for o, t in zip(ops, ts)]
            ts = [t - _mm(t, low) for t, low in zip(ts, lows)]
            inner = outer
        xs = [_mm(t, o["x"]) for t, o in zip(ts, ops)]
        for i in range(n):
            vn_s[i] = jnp.zeros((SC, DV), jnp.float32)
        states = list(states)
        for step in range(CPS):
            cs = [step if d == 0 else CPS - 1 - step for _, d in chains]
            los = [c * CHUNK for c in cs]
            ws_qs = [_mm(jnp.concatenate([x[lo:lo + CHUNK, DV:], o["qd"][lo:lo + CHUNK]], axis=0), s)
                     for x, o, lo, s in zip(xs, ops, los, states)]
            for i in range(n):
                vn_s[i, los[i]:los[i] + CHUNK, :] = xs[i][los[i]:los[i] + CHUNK, :DV] - ws_qs[i][:CHUNK]
            vns = [vn_s[i] for i in range(n)]
            o_cs = [wq[CHUNK:] + _mm(o["attn"][lo:lo + CHUNK, :], vn)
                    for wq, o, lo, vn in zip(ws_qs, ops, los, vns)]
            for i, (hh, _) in enumerate(chains):
                o_s[hh, pl.ds(ops[i]["r0"] + los[i], CHUNK), :] += o_cs[i]
            states = [s * o["egl"][lo:lo + 1, :]
                      + _mm(o["kdt"], jnp.where((rowi >= lo) & (rowi < lo + CHUNK), vn, 0.0))
                      for s, o, lo, vn in zip(states, ops, los, vns)]
        return tuple(states)

    if zero_init:
        states = tuple(jnp.zeros((DK, DV), jnp.float32) for _ in range(2 * hpb))
    else:
        states = tuple(s0_ref[0, d, hh] for hh in range(hpb) for d in range(2))

    chains = [(hh, d) for hh in range(hpb) for d in range(2)]

    def body(m, carry):
        return run_chains([m if d == 0 else n_sc - 1 - m for _, d in chains], carry)

    if n_sc == 1:
        states = body(0, states)
    else:
        states = lax.fori_loop(0, n_sc, body, states)

    for hh in range(hpb):
        for d in range(2):
            st_ref[0, d, hh] = states[2 * hh + d]
        o = o_s[hh]
        o = o * lax.rsqrt(jnp.mean(o * o, axis=-1, keepdims=True) + EPS) * og_ref[...]
        zz = z_ref[:, hh * DV:(hh + 1) * DV]
        o_ref[:, hh * DV:(hh + 1) * DV] = o * (zz * jax.nn.sigmoid(zz))


def _delta_call(qkv, z, ba, conv_w, a_log, dt_bias, onorm_g, s0, B, L, row_blk0):
    n_sc = L // SC
    t0 = row_blk0 * L
    bah = ba[t0:t0 + B * L, :4 * H_A].reshape(B, L, 4, H_A).transpose(0, 3, 1, 2)
    bar = bah.reshape(B, H_A, n_sc, SC, 4).transpose(0, 1, 2, 4, 3)
    scal = jnp.concatenate([a_log.reshape(-1), dt_bias.reshape(-1)]).astype(jnp.float32)
    hpb = max(1, min(H_A, DELTA_HEAD_ROWS // L))
    n_hb = H_A // hpb
    zero_init = s0 is None
    if zero_init:
        s0 = jnp.zeros((1, 2, hpb, DK, DV), jnp.float32)
        s0_spec = pl.BlockSpec((1, 2, hpb, DK, DV), lambda b, h, sc: (0, 0, 0, 0, 0))
    else:
        s0_spec = pl.BlockSpec((1, 2, hpb, DK, DV), lambda b, h, sc: (b, 0, h, 0, 0))

    def col(off):
        return pl.BlockSpec((L, hpb * DK), lambda b, h, sc: (row_blk0 + b, off * n_hb + h))

    def cw(off):
        return pl.BlockSpec((CONV_K, hpb * DK), lambda b, h, sc: (0, off * n_hb + h))

    kern = functools.partial(_delta_kernel, n_sc=n_sc, zero_init=zero_init, hpb=hpb)
    return pl.pallas_call(
        kern,
        grid_spec=pltpu.PrefetchScalarGridSpec(
            num_scalar_prefetch=1,
            grid=(B, n_hb),
            in_specs=[col(0), col(1), col(2),
                      pl.BlockSpec((L, hpb * DV), lambda b, h, sc: (row_blk0 + b, h)),
                      pl.BlockSpec((1, hpb, L, 4), lambda b, h, sc: (b, h, 0, 0)),
                      pl.BlockSpec((1, hpb, n_sc, 4, SC), lambda b, h, sc: (b, h, 0, 0, 0)),
                      cw(0), cw(1), cw(2),
                      pl.BlockSpec((1, DV), lambda b, h, sc: (0, 0)),
                      s0_spec],
            out_specs=[pl.BlockSpec((L, hpb * DV), lambda b, h, sc: (b, h)),
                       pl.BlockSpec((1, 2, hpb, DK, DV), lambda b, h, sc: (b, 0, h, 0, 0))],
            scratch_shapes=[pltpu.VMEM((hpb, L, DK), jnp.float32), pltpu.VMEM((hpb, L, DK), jnp.float32),
                            pltpu.VMEM((hpb, L, DV), jnp.float32), pltpu.VMEM((hpb, L, DV), jnp.float32),
                            pltpu.VMEM((2 * hpb, SC, DV), jnp.float32)]),
        out_shape=[jax.ShapeDtypeStruct((B * L, D_A), jnp.float32),
                   jax.ShapeDtypeStruct((B, 2, H_A, DK, DV), jnp.float32)],
        compiler_params=pltpu.CompilerParams(dimension_semantics=("arbitrary", "arbitrary"),
                                             vmem_limit_bytes=VMEM_LIMIT),
    )(scal, qkv, qkv, qkv, z, bah, bar, conv_w, conv_w, conv_w, onorm_g.reshape(1, DV), s0)


TR = 256
GSZ = N_EXPERTS // N_GROUPS
NEG = -jnp.inf
BM = 256
SUBLANES = 8
FILL_PIECES = tuple(p for p in (BM >> s for s in range(1, BM.bit_length())) if p >= SUBLANES)


def _route_kernel(h_ref, rwh_ref, rwl_ref, rb_ref, idx_ref, rank_ref, w_ref, cnt_ref, cnt_s):
    i = pl.program_id(0)

    @pl.when(i == 0)
    def _():
        cnt_s[...] = jnp.zeros_like(cnt_s)

    h = h_ref[...]
    hh, hl = _split_bf16(h)
    logits = _bdot(hh, rwh_ref[...]) + (_bdot(hh, rwl_ref[...]) + _bdot(hl, rwh_ref[...]))
    scores = jax.nn.sigmoid(logits.T)
    sel = scores + rb_ref[...]
    erow = lax.broadcasted_iota(jnp.int32, sel.shape, 0)
    grow = lax.broadcasted_iota(jnp.int32, (GSZ, TR), 0)

    def first_argmax(v, rows):
        m = jnp.max(v, axis=0, keepdims=True)
        first = jnp.min(jnp.where(v == m, rows, N_EXPERTS), axis=0, keepdims=True)
        return m, first

    gs = []
    for g in range(N_GROUPS):
        vg = sel[g * GSZ:(g + 1) * GSZ, :]
        m1, i1 = first_argmax(vg, grow)
        m2 = jnp.max(jnp.where(grow == i1, NEG, vg), axis=0, keepdims=True)
        gs.append(m1 + m2)
    cand = []
    for g in range(N_GROUPS):
        beat = jnp.zeros(gs[g].shape, jnp.int32)
        for o in range(N_GROUPS):
            if o == g:
                continue
            wins = (gs[o] > gs[g]) | ((gs[o] == gs[g]) & (o < g))
            beat = beat + wins.astype(jnp.int32)
        cand.append(jnp.where(beat < TOPK_GROUP, sel[g * GSZ:(g + 1) * GSZ, :], NEG))
    cand = jnp.concatenate(cand, axis=0)
    chosen = []
    picked = jnp.zeros(sel.shape, jnp.bool_)
    for _ in range(TOP_K):
        _, ik = first_argmax(cand, erow)
        hit = erow == ik
        chosen.append((ik, hit))
        picked = picked | hit
        cand = jnp.where(hit, NEG, cand)
    wsum = jnp.sum(jnp.where(picked, scores, 0.0), axis=0, keepdims=True)

    ri = lax.broadcasted_iota(jnp.int32, (TR, TR), 0)
    ci = lax.broadcasted_iota(jnp.int32, (TR, TR), 1)
    earlier = (ri < ci).astype(jnp.bfloat16)
    rank_mat = _bdot(picked.astype(jnp.bfloat16), earlier) + cnt_s[...]
    cnt_s[...] = cnt_s[...] + jnp.sum(picked.astype(jnp.float32), axis=1, keepdims=True)
    cnt_ref[...] = cnt_s[...].astype(jnp.int32)

    for k, (ik, hit) in enumerate(chosen):
        idx_ref[0, k:k + 1, :] = ik
        rank_ref[0, k:k + 1, :] = jnp.sum(jnp.where(hit, rank_mat, 0.0), axis=0, keepdims=True).astype(jnp.int32)
        w_ref[0, k:k + 1, :] = jnp.sum(jnp.where(hit, scores, 0.0), axis=0, keepdims=True) / wsum * ROUTED_SCALE


def _route_call(hf, router_w, router_bias):
    T, D = hf.shape
    n_tiles = T // TR
    rwh, rwl = _split_bf16(router_w)
    row_spec = pl.BlockSpec((1, TOP_K, TR), lambda i: (i, 0, 0))
    return pl.pallas_call(
        _route_kernel,
        grid=(n_tiles,),
        in_specs=[pl.BlockSpec((TR, D), lambda i: (i, 0)),
                  pl.BlockSpec((D, N_EXPERTS), lambda i: (0, 0)),
                  pl.BlockSpec((D, N_EXPERTS), lambda i: (0, 0)),
                  pl.BlockSpec((N_EXPERTS, 1), lambda i: (0, 0))],
        out_specs=[row_spec, row_spec, row_spec, pl.BlockSpec((N_EXPERTS, 1), lambda i: (0, 0))],
        scratch_shapes=[pltpu.VMEM((N_EXPERTS, 1), jnp.float32)],
        out_shape=[jax.ShapeDtypeStruct((n_tiles, TOP_K, TR), jnp.int32),
                   jax.ShapeDtypeStruct((n_tiles, TOP_K, TR), jnp.int32),
                   jax.ShapeDtypeStruct((n_tiles, TOP_K, TR), jnp.float32),
                   jax.ShapeDtypeStruct((N_EXPERTS, 1), jnp.int32)],
        compiler_params=pltpu.CompilerParams(dimension_semantics=("arbitrary",)),
    )(hf, rwh, rwl, router_bias.reshape(N_EXPERTS, 1).astype(jnp.float32))


def _dispatch_kernel(fill_ref, idx_ref, rank_ref, pstart_ref, h_ref, pos_ref, xs_hbm, pos_v, pos_s, zbuf,
                     ssem, psem, zsem):
    n_blk = xs_hbm.shape[0] // BM

    @pl.when(pl.program_id(0) == 0)
    def _():
        zbuf[...] = jnp.zeros_like(zbuf)

        def pad_copies(e, act):
            n = fill_ref[N_EXPERTS + e]
            start = fill_ref[e]
            head = jnp.minimum((-start) & (SUBLANES - 1), n)
            for j in range(SUBLANES - 1):
                @pl.when(j < head)
                def _(j=j):
                    act(pltpu.make_async_copy(zbuf.at[0], xs_hbm.at[start + j], zsem))
            off = start + head
            rest = n - head
            for piece in FILL_PIECES:
                @pl.when((rest & piece) != 0)
                def _(off=off, piece=piece):
                    dst = xs_hbm.at[pl.ds(pl.multiple_of(off, SUBLANES), piece)]
                    act(pltpu.make_async_copy(zbuf.at[pl.ds(0, piece)], dst, zsem))
                off = off + (rest & piece)

        def tail_copy(b, act):
            act(pltpu.make_async_copy(zbuf, xs_hbm.at[pl.ds(pl.multiple_of(b * BM, BM), BM)], zsem))

        for act in (lambda c: c.start(), lambda c: c.wait()):
            def per_expert(e, carry, act=act):
                pad_copies(e, act)
                return carry

            def per_block(b, carry, act=act):
                tail_copy(b, act)
                return carry

            lax.fori_loop(0, N_EXPERTS, per_expert, 0)
            lax.fori_loop(fill_ref[2 * N_EXPERTS], n_blk, per_block, 0)

    erow = lax.broadcasted_iota(jnp.int32, (N_EXPERTS, TR), 0)
    pstart = pstart_ref[...]
    for k in range(TOP_K):
        hit = erow == idx_ref[0, k:k + 1, :]
        seg = jnp.sum(jnp.where(hit, pstart, 0), axis=0, keepdims=True)
        pos_v[k:k + 1, :] = seg + rank_ref[0, k:k + 1, :]
    pos_ref[0] = pos_v[...]
    cp = pltpu.make_async_copy(pos_v, pos_s, psem)
    cp.start()
    cp.wait()

    def body(t, carry):
        for k in range(TOP_K):
            pltpu.make_async_copy(h_ref.at[t], xs_hbm.at[pos_s[k, t]], ssem).start()
        return carry

    lax.fori_loop(0, TR, body, 0, unroll=8)
    n_rows = TR * TOP_K
    pltpu.make_async_copy(xs_hbm.at[pl.ds(0, n_rows)], xs_hbm.at[pl.ds(0, n_rows)], ssem).wait()


def _dispatch_call(hf, idx, rank, pad_start, fill_tab, n_pad):
    T, D = hf.shape
    n_tiles = T // TR
    row_spec = pl.BlockSpec((1, TOP_K, TR), lambda i, ft: (i, 0, 0))
    return pl.pallas_call(
        _dispatch_kernel,
        grid_spec=pltpu.PrefetchScalarGridSpec(
            num_scalar_prefetch=1,
            grid=(n_tiles,),
            in_specs=[row_spec, row_spec,
                      pl.BlockSpec((N_EXPERTS, 1), lambda i, ft: (0, 0)),
                      pl.BlockSpec((TR, D), lambda i, ft: (i, 0))],
            out_specs=[row_spec, pl.BlockSpec(memory_space=pl.ANY)],
            scratch_shapes=[pltpu.VMEM((TOP_K, TR), jnp.int32), pltpu.SMEM((TOP_K, TR), jnp.int32),
                            pltpu.VMEM((BM, D), hf.dtype),
                            pltpu.SemaphoreType.DMA, pltpu.SemaphoreType.DMA, pltpu.SemaphoreType.DMA]),
        out_shape=[jax.ShapeDtypeStruct((n_tiles, TOP_K, TR), jnp.int32),
                   jax.ShapeDtypeStruct((n_pad, D), hf.dtype)],
        compiler_params=pltpu.CompilerParams(dimension_semantics=("arbitrary",)),
    )(fill_tab, idx, rank, pad_start.reshape(N_EXPERTS, 1), hf)


def _expert_kernel(blk_e_ref, nvalid_ref, nused_ref, x_ref, wg_ref, wu_ref, wd_ref, y_ref, wg_s, wu_s, wd_s):
    i = pl.program_id(0)

    @pl.when(i < nused_ref[0])
    def _():
        e = blk_e_ref[i]
        prev = blk_e_ref[jnp.maximum(i - 1, 0)]

        @pl.when((i == 0) | (e != prev))
        def _():
            wg_s[...] = wg_ref[0].astype(jnp.bfloat16)
            wu_s[...] = wu_ref[0].astype(jnp.bfloat16)
            wd_s[...] = wd_ref[0].astype(jnp.bfloat16)

        row = lax.broadcasted_iota(jnp.int32, (BM, 1), 0)
        xa, xb = _unpack_rows(jnp.where(row < nvalid_ref[i], x_ref[...], jnp.uint32(0)))
        xa = xa.astype(jnp.bfloat16)
        xb = xb.astype(jnp.bfloat16)
        half = xa.shape[1]
        g = _bdot(xa, wg_s[:half, :]) + _bdot(xb, wg_s[half:, :])
        u = _bdot(xa, wu_s[:half, :]) + _bdot(xb, wu_s[half:, :])
        a = (g * jax.nn.sigmoid(g)) * u
        y_ref[...] = _pack_rows(_bdot(a.astype(jnp.bfloat16), wd_s[...]))

    @pl.when(i >= nused_ref[0])
    def _():
        y_ref[...] = jnp.zeros_like(y_ref)


def _expert_call(x_sorted, blk_e, n_valid, n_used, w_gate, w_up, w_down):
    n_pad, DH = x_sorted.shape
    n_blk = n_pad // BM
    E, D, F = w_gate.shape

    def row_map(i, be, nv, nu):
        return (jnp.minimum(i, nu[0] - 1), 0)

    def w_map(i, be, nv, nu):
        return (be[jnp.minimum(i, nu[0] - 1)], 0, 0)

    return pl.pallas_call(
        _expert_kernel,
        grid_spec=pltpu.PrefetchScalarGridSpec(
            num_scalar_prefetch=3,
            grid=(n_blk,),
            in_specs=[pl.BlockSpec((BM, DH), row_map),
                      pl.BlockSpec((1, D, F), w_map),
                      pl.BlockSpec((1, D, F), w_map),
                      pl.BlockSpec((1, F, D), w_map)],
            out_specs=pl.BlockSpec((BM, DH), lambda i, be, nv, nu: (i, 0)),
            scratch_shapes=[pltpu.VMEM((D, F), jnp.bfloat16), pltpu.VMEM((D, F), jnp.bfloat16),
                            pltpu.VMEM((F, D), jnp.bfloat16)]),
        out_shape=jax.ShapeDtypeStruct((n_pad, DH), jnp.uint32),
        compiler_params=pltpu.CompilerParams(dimension_semantics=("arbitrary",),
                                             vmem_limit_bytes=VMEM_LIMIT),
    )(blk_e, n_valid, n_used, x_sorted, w_gate, w_up, w_down)


TC = 128


def _combine_kernel(pos_hbm, y_hbm, w_ref, x1_ref, sh_ref, mod_ref, fg_ref, out_ref, ybuf, pos_s, gsem, psem):
    j = pl.program_id(0)
    last = pl.num_programs(0) - 1

    def pos_copy(b, slot):
        return pltpu.make_async_copy(pos_hbm.at[b], pos_s.at[slot], psem.at[slot])

    def start_gather(slot):
        for k in range(TOP_K):
            for t in range(TC):
                pltpu.make_async_copy(y_hbm.at[pos_s[slot, k * TC + t]], ybuf.at[slot, k, t],
                                      gsem.at[slot]).start()

    def wait_gather(slot):
        pltpu.make_async_copy(ybuf.at[slot], ybuf.at[slot], gsem.at[slot]).wait()

    @pl.when(j == 0)
    def _():
        pos_copy(0, 0).start()
        pos_copy(0, 0).wait()
        start_gather(0)
        pos_copy(jnp.minimum(1, last), 1).start()

    slot = j % 2
    nslot = 1 - slot
    pos_copy(0, nslot).wait()
    start_gather(nslot)
    pos_copy(jnp.minimum(j + 2, last), slot).start()
    wait_gather(slot)
    w = w_ref[...]
    acc_a = acc_b = None
    for k in range(TOP_K):
        ya, yb = _unpack_rows(ybuf[slot, k])
        acc_a = w[:, k:k + 1] * ya if k == 0 else acc_a + w[:, k:k + 1] * ya
        acc_b = w[:, k:k + 1] * yb if k == 0 else acc_b + w[:, k:k + 1] * yb
    acc = jnp.concatenate([acc_a, acc_b], axis=1)
    x2 = x1_ref[...] + mod_ref[0, 5:6, :] * (acc + sh_ref[...])
    out_ref[...] = x2 * lax.rsqrt(jnp.mean(x2 * x2, axis=-1, keepdims=True) + EPS) * fg_ref[...]

    @pl.when(j == last)
    def _():
        wait_gather(nslot)
        pos_copy(0, slot).wait()


def _combine_call(y_sorted, pos_t, wts, x1, shared, mod, final_g, n_ctx, lat_len):
    T, K = wts.shape
    DH = y_sorted.shape[1]
    D = 2 * DH
    n_tiles = T // TC
    row = functools.partial(_mod_row, tokens_per_tile=TC, n_ctx=n_ctx, lat_len=lat_len)
    return pl.pallas_call(
        _combine_kernel,
        grid=(n_tiles,),
        in_specs=[pl.BlockSpec(memory_space=pl.ANY),
                  pl.BlockSpec(memory_space=pl.ANY),
                  pl.BlockSpec((TC, K), lambda j: (j, 0)),
                  pl.BlockSpec((TC, D), lambda j: (j, 0)),
                  pl.BlockSpec((TC, D), lambda j: (j, 0)),
                  pl.BlockSpec((1, N_MOD, D), lambda j: (row(j), 0, 0)),
                  pl.BlockSpec((1, D), lambda j: (0, 0))],
        out_specs=pl.BlockSpec((TC, D), lambda j: (j, 0)),
        scratch_shapes=[pltpu.VMEM((2, K, TC, DH), jnp.uint32),
                        pltpu.SMEM((2, K * TC), jnp.int32),
                        pltpu.SemaphoreType.DMA((2,)), pltpu.SemaphoreType.DMA((2,))],
        out_shape=jax.ShapeDtypeStruct((T, D), jnp.float32),
        compiler_params=pltpu.CompilerParams(dimension_semantics=("arbitrary",)),
    )(pos_t, y_sorted, wts, x1, shared, mod, final_g.reshape(1, D))


def _moe_routed(h2, h2p, router_w, router_bias, w_gate, w_up, w_down):
    T, D = h2.shape
    idx, rank, w_rows, cnt = _route_call(h2, router_w, router_bias)
    wts = w_rows.transpose(0, 2, 1).reshape(T, TOP_K)
    counts = cnt[:, 0]
    padded = (counts + BM - 1) // BM * BM
    pad_end = jnp.cumsum(padded)
    pad_start = (pad_end - padded).astype(jnp.int32)
    n_pad = T * TOP_K + N_EXPERTS * BM
    n_blk = n_pad // BM
    n_used = (pad_end[-1] // BM).astype(jnp.int32).reshape(1)
    fill_tab = jnp.concatenate([pad_start + counts, padded - counts, n_used]).astype(jnp.int32)
    pos, x_sorted = _dispatch_call(h2p, idx, rank, pad_start, fill_tab, n_pad)
    blk_row0 = jnp.arange(n_blk, dtype=jnp.int32) * BM
    blk_e = jnp.minimum(jnp.sum((pad_end[None, :] <= blk_row0[:, None]).astype(jnp.int32), axis=1), N_EXPERTS - 1)
    own = blk_e[:, None] == jnp.arange(N_EXPERTS, dtype=jnp.int32)[None, :]
    seg_end = jnp.sum(jnp.where(own, (pad_start + counts)[None, :], 0), axis=1)
    n_valid = jnp.clip(seg_end - blk_row0, 0, BM).astype(jnp.int32)
    y = _expert_call(x_sorted, blk_e, n_valid, n_used, w_gate, w_up, w_down)
    pos_t = pos.reshape(T // TR, TOP_K, TR // TC, TC).transpose(0, 2, 1, 3).reshape(T // TC, TOP_K * TC)
    return y, pos_t, wts


def kernel(x_prompt, x_sample, state_delta, c, c_ctx, w_ada, b_ada, norm1_g, w_in, conv_w, a_log,
           dt_bias, onorm_g, pool_w, pool_scale, w_out, norm2_g, router_w, router_bias, exp_w_gate,
           exp_w_up, exp_w_down, sh_w_gate, sh_w_up, sh_w_down, final_g):
    Bc, Lc, D = x_prompt.shape
    Bl, Ll, _ = x_sample.shape
    n_ctx = Bc * Lc
    assert DEPTH == 1 and 1 + Bl <= MOD_ROWS and n_ctx % Ll == 0
    x_parts = (x_prompt.reshape(n_ctx, D), x_sample.reshape(Bl * Ll, D))
    cvec = jnp.concatenate([c_ctx[None], c, jnp.zeros((MOD_ROWS - 1 - Bl, D), c.dtype)], axis=0)
    l = 0
    mod = _ada_call(cvec, w_ada[l], b_ada[l]).reshape(MOD_ROWS, N_MOD, D)
    qkv, z, ba, u = _inproj_call(*x_parts, mod, norm1_g[l], w_in[l], Ll)
    dn = (conv_w[l], a_log[l], dt_bias[l], onorm_g[l])
    oa_c, st_ctx = _delta_call(qkv, z, ba, *dn, None, Bc, Lc, 0)
    oa_l, _ = _delta_call(qkv, z, ba, *dn, state_delta[:, l], Bl, Ll, n_ctx // Ll)
    op_c = _pool_call(u, pool_w[l], pool_scale[l], False, Bc, Lc, 0)
    op_l = _pool_call(u, pool_w[l], pool_scale[l], True, Bl, Ll, n_ctx // Ll)
    x1, h2, h2p, shared = _outproj_call(x_parts, (oa_c, oa_l), (op_c, op_l), mod, norm2_g[l], w_out[l],
                                        sh_w_gate[l], sh_w_up[l], sh_w_down[l], Ll)
    y, pos_t, wts = _moe_routed(h2, h2p, router_w[l], router_bias[l], exp_w_gate[l], exp_w_up[l],
                                exp_w_down[l])
    out = _combine_call(y, pos_t, wts, x1, shared, mod, final_g, n_ctx, Ll)
    y_prompt = out[:n_ctx].reshape(Bc, Lc, D)
    y_sample = out[n_ctx:].reshape(Bl, Ll, D)
    new_state_delta = st_ctx[:, None].astype(x_prompt.dtype)
    return (y_prompt, y_sample, new_state_delta)
```

```python
import functools
import jax, jax.numpy as jnp
from jax import lax
from jax.experimental import pallas as pl
from jax.experimental.pallas import tpu as pltpu
from jax.experimental.pallas import tpu_sc as plsc

D_MODEL = 1024
DEPTH = 1
GRID_W = 64
D_MIX = D_MODEL
D_A = D_MIX // 2
D_P = D_MIX - D_A
H_A = 4
DK = D_A // H_A
DV = D_A // H_A
CONV_K = 5
CHUNK = 64
POOL_WINDOWS = (2, 4, 8, 16)
N_PG = len(POOL_WINDOWS)
PG = D_P // N_PG
N_EXPERTS = 256
TOP_K = 8
N_GROUPS = 8
TOPK_GROUP = 4
ROUTED_SCALE = 2.5
EPS = 1e-6
VMEM_LIMIT = 48 * 1024 * 1024


def _split_bf16(a):
    hi = a.astype(jnp.bfloat16)
    return hi, (a - hi.astype(jnp.float32)).astype(jnp.bfloat16)


def _bdot(a, b):
    return jnp.dot(a, b, preferred_element_type=jnp.float32)


def _pack_rows(x):
    m = x.shape[1] // 2
    hi = lax.bitcast_convert_type(x[:, :m].astype(jnp.bfloat16).astype(jnp.float32), jnp.uint32)
    lo = lax.bitcast_convert_type(x[:, m:].astype(jnp.bfloat16).astype(jnp.float32), jnp.uint32)
    return hi | (lo >> 16)


def _unpack_rows(p):
    hi = lax.bitcast_convert_type(p & jnp.uint32(0xFFFF0000), jnp.float32)
    lo = lax.bitcast_convert_type(p << 16, jnp.float32)
    return hi, lo


N_MOD = 6
MOD_ROWS = 8
TM = 512


def _ada_kernel(c_ref, w_ref, b_ref, o_ref):
    c = c_ref[...]
    s = c * jax.nn.sigmoid(c)
    sh, sl = _split_bf16(s)
    wh, wl = _split_bf16(w_ref[...])
    o_ref[...] = _bdot(sh, wh) + (_bdot(sh, wl) + _bdot(sl, wh)) + b_ref[...]


def _ada_call(cvec, w_ada, b_ada):
    R, D = cvec.shape
    N = w_ada.shape[1]
    tn = 1024
    return pl.pallas_call(
        _ada_kernel,
        grid=(N // tn,),
        in_specs=[pl.BlockSpec((R, D), lambda j: (0, 0)),
                  pl.BlockSpec((D, tn), lambda j: (0, j)),
                  pl.BlockSpec((1, tn), lambda j: (0, j))],
        out_specs=pl.BlockSpec((R, tn), lambda j: (0, j)),
        out_shape=jax.ShapeDtypeStruct((R, N), jnp.float32),
    )(cvec, w_ada, b_ada.reshape(1, N))


def _mod_row(tile, tokens_per_tile, n_ctx, lat_len):
    t0 = tile * tokens_per_tile
    return jnp.where(t0 < n_ctx, 0, 1 + (t0 - n_ctx) // lat_len)


def _two_part_specs(n_ctx_tiles, width):
    return (pl.BlockSpec((TM, width), lambda i: (jnp.minimum(i, n_ctx_tiles - 1), 0)),
            pl.BlockSpec((TM, width), lambda i: (jnp.maximum(i - n_ctx_tiles, 0), 0)))


def _pick(n_ctx_tiles, ctx_ref, lat_ref):
    return jnp.where(pl.program_id(0) < n_ctx_tiles, ctx_ref[...], lat_ref[...])


def _inproj_kernel(xc_ref, xl_ref, mod_ref, g_ref, wq_ref, wz_ref, wb_ref, wu_ref, q_ref, z_ref, b_ref, u_ref,
                   *, n_ctx_tiles):
    x = _pick(n_ctx_tiles, xc_ref, xl_ref)
    y = x * lax.rsqrt(jnp.mean(x * x, axis=-1, keepdims=True) + EPS) * g_ref[...]
    h = (y * (1.0 + mod_ref[0, 1:2, :]) + mod_ref[0, 0:1, :]).astype(jnp.bfloat16)
    q_ref[...] = _bdot(h, wq_ref[...])
    z_ref[...] = _bdot(h, wz_ref[...])
    b_ref[...] = _bdot(h, wb_ref[...])
    u_ref[...] = _bdot(h, wu_ref[...])


def _inproj_call(x_ctx, x_lat, mod, norm1_g, w_in, lat_len):
    n_ctx, D = x_ctx.shape
    T = n_ctx + x_lat.shape[0]
    bf = jnp.bfloat16
    nq, nz, nb = 3 * D_A, D_A, 4 * H_A
    wq = w_in[:, :nq].astype(bf)
    wz = w_in[:, nq:nq + nz].astype(bf)
    wb = jnp.pad(w_in[:, nq + nz:nq + nz + nb], ((0, 0), (0, 128 - nb))).astype(bf)
    wu = w_in[:, nq + nz + nb:].astype(bf)
    row = functools.partial(_mod_row, tokens_per_tile=TM, n_ctx=n_ctx, lat_len=lat_len)

    def full(a):
        return pl.BlockSpec(a.shape, lambda i: (0, 0))

    def rows(n):
        return pl.BlockSpec((TM, n), lambda i: (i, 0))

    return pl.pallas_call(
        functools.partial(_inproj_kernel, n_ctx_tiles=n_ctx // TM),
        grid=(T // TM,),
        in_specs=[*_two_part_specs(n_ctx // TM, D), pl.BlockSpec((1, N_MOD, D), lambda i: (row(i), 0, 0)),
                  pl.BlockSpec((1, D), lambda i: (0, 0)), full(wq), full(wz), full(wb), full(wu)],
        out_specs=[rows(nq), rows(nz), rows(128), rows(D_P)],
        out_shape=[jax.ShapeDtypeStruct((T, nq), jnp.float32), jax.ShapeDtypeStruct((T, nz), jnp.float32),
                   jax.ShapeDtypeStruct((T, 128), jnp.float32), jax.ShapeDtypeStruct((T, D_P), jnp.float32)],
        compiler_params=pltpu.CompilerParams(dimension_semantics=("arbitrary",),
                                             vmem_limit_bytes=VMEM_LIMIT),
    )(x_ctx, x_lat, mod, norm1_g.reshape(1, D), wq, wz, wb, wu)


PT = 256


def _window_bounds(pos, w, n):
    return jnp.maximum(pos - w // 2, 0), jnp.minimum(pos + w - w // 2, n)


def _band_sum(band, x):
    xh, xl = _split_bf16(x)
    return _bdot(band, xh) + _bdot(band, xl)


def _pool_seq_kernel(u_ref, pw_ref, ps_ref, o_ref):
    L = u_ref.shape[0]
    ti = lax.broadcasted_iota(jnp.int32, (L, L), 0)
    ji = lax.broadcasted_iota(jnp.int32, (L, L), 1)
    tcol = lax.broadcasted_iota(jnp.int32, (L, 1), 0)
    for i, w in enumerate(POOL_WINDOWS):
        lo, hi = _window_bounds(ti, w, L)
        band = ((ji >= lo) & (ji < hi)).astype(jnp.bfloat16)
        clo, chi = _window_bounds(tcol, w, L)
        ug = u_ref[:, i * PG:(i + 1) * PG]
        mean = _band_sum(band, ug) / (chi - clo).astype(jnp.float32)
        d = (mean - ug).astype(jnp.bfloat16)
        o_ref[:, i * PG:(i + 1) * PG] = _bdot(d, pw_ref[i]) * ps_ref[:, i * PG:(i + 1) * PG]


def _pool_grid_kernel(u_ref, pw_ref, ps_ref, o_ref, pad_s, r_s):
    L = u_ref.shape[0]
    rows = L // GRID_W
    halo = (max(POOL_WINDOWS) // 2) * GRID_W
    pad_s[0:halo, :] = jnp.zeros((halo, D_P), jnp.float32)
    pad_s[halo + L:, :] = jnp.zeros((halo, D_P), jnp.float32)
    pad_s[halo:halo + L, :] = u_ref[...]
    ti = lax.broadcasted_iota(jnp.int32, (PT, PT), 0)
    ji = lax.broadcasted_iota(jnp.int32, (PT, PT), 1)
    tcol = lax.broadcasted_iota(jnp.int32, (PT, 1), 0)
    for i, w in enumerate(POOL_WINDOWS):
        cs = slice(i * PG, (i + 1) * PG)
        acc = None
        for dr in range(-(w // 2), w - w // 2):
            part = pad_s[halo + dr * GRID_W:halo + dr * GRID_W + L, cs]
            acc = part if acc is None else acc + part
        r_s[...] = acc
        lo, hi = _window_bounds(ti % GRID_W, w, GRID_W)
        band = ((ji // GRID_W == ti // GRID_W) & (ji % GRID_W >= lo) & (ji % GRID_W < hi)).astype(jnp.bfloat16)
        clo, chi = _window_bounds(tcol % GRID_W, w, GRID_W)
        ccnt = (chi - clo).astype(jnp.float32)
        for tile in range(L // PT):
            ts = slice(tile * PT, (tile + 1) * PT)
            rlo, rhi = _window_bounds(tile * (PT // GRID_W) + tcol // GRID_W, w, rows)
            mean = _band_sum(band, r_s[ts, :]) / ((rhi - rlo).astype(jnp.float32) * ccnt)
            d = (mean - u_ref[ts, cs]).astype(jnp.bfloat16)
            o_ref[ts, cs] = _bdot(d, pw_ref[i]) * ps_ref[:, cs]


def _pool_call(u, pool_w, pool_scale, grid, B, L, row_blk0):
    pw = pool_w.astype(jnp.bfloat16)
    ps = pool_scale.reshape(1, D_P)
    specs = dict(
        grid=(B,),
        in_specs=[pl.BlockSpec((L, D_P), lambda b: (row_blk0 + b, 0)),
                  pl.BlockSpec((N_PG, PG, PG), lambda b: (0, 0, 0)),
                  pl.BlockSpec((1, D_P), lambda b: (0, 0))],
        out_specs=pl.BlockSpec((L, D_P), lambda b: (b, 0)),
        out_shape=jax.ShapeDtypeStruct((B * L, D_P), jnp.float32),
        compiler_params=pltpu.CompilerParams(dimension_semantics=("arbitrary",),
                                             vmem_limit_bytes=VMEM_LIMIT))
    if not grid:
        return pl.pallas_call(_pool_seq_kernel, **specs)(u, pw, ps)
    halo = (max(POOL_WINDOWS) // 2) * GRID_W
    return pl.pallas_call(
        _pool_grid_kernel,
        scratch_shapes=[pltpu.VMEM((L + 2 * halo, D_P), jnp.float32), pltpu.VMEM((L, PG), jnp.float32)],
        **specs)(u, pw, ps)


def _outproj_kernel(xc_ref, xl_ref, oac_ref, oal_ref, opc_ref, opl_ref, mod_ref, g2_ref, wo_ref, sg_ref, su_ref,
                    sd_ref, x1_ref, h2_ref, h2p_ref, sh_ref, *, n_ctx_tiles):
    o_a = _pick(n_ctx_tiles, oac_ref, oal_ref)
    o_p = _pick(n_ctx_tiles, opc_ref, opl_ref)
    mix = (_bdot(o_a.astype(jnp.bfloat16), wo_ref[:D_A, :])
           + _bdot(o_p.astype(jnp.bfloat16), wo_ref[D_A:, :]))
    x1 = _pick(n_ctx_tiles, xc_ref, xl_ref) + mod_ref[0, 2:3, :] * mix
    x1_ref[...] = x1
    y = x1 * lax.rsqrt(jnp.mean(x1 * x1, axis=-1, keepdims=True) + EPS) * g2_ref[...]
    h2 = y * (1.0 + mod_ref[0, 4:5, :]) + mod_ref[0, 3:4, :]
    h2_ref[...] = h2
    h2p_ref[...] = _pack_rows(h2)
    hb = h2.astype(jnp.bfloat16)
    g = _bdot(hb, sg_ref[...])
    a = (g * jax.nn.sigmoid(g)) * _bdot(hb, su_ref[...])
    sh_ref[...] = _bdot(a.astype(jnp.bfloat16), sd_ref[...])


def _outproj_call(x_parts, oa_parts, op_parts, mod, norm2_g, w_out, sh_gate, sh_up, sh_down, lat_len):
    n_ctx, D = x_parts[0].shape
    T = n_ctx + x_parts[1].shape[0]
    nct = n_ctx // TM
    bf = jnp.bfloat16
    row = functools.partial(_mod_row, tokens_per_tile=TM, n_ctx=n_ctx, lat_len=lat_len)
    ws = [w_out.astype(bf), sh_gate.astype(bf), sh_up.astype(bf), sh_down.astype(bf)]

    def rows(n):
        return pl.BlockSpec((TM, n), lambda i: (i, 0))

    return pl.pallas_call(
        functools.partial(_outproj_kernel, n_ctx_tiles=nct),
        grid=(T // TM,),
        in_specs=[*_two_part_specs(nct, D), *_two_part_specs(nct, D_A), *_two_part_specs(nct, D_P),
                  pl.BlockSpec((1, N_MOD, D), lambda i: (row(i), 0, 0)),
                  pl.BlockSpec((1, D), lambda i: (0, 0))] + [pl.BlockSpec(w.shape, lambda i: (0, 0)) for w in ws],
        out_specs=[rows(D), rows(D), rows(D // 2), rows(D)],
        out_shape=[jax.ShapeDtypeStruct((T, D), jnp.float32), jax.ShapeDtypeStruct((T, D), jnp.float32),
                   jax.ShapeDtypeStruct((T, D // 2), jnp.uint32), jax.ShapeDtypeStruct((T, D), jnp.float32)],
        compiler_params=pltpu.CompilerParams(dimension_semantics=("arbitrary",),
                                             vmem_limit_bytes=VMEM_LIMIT),
    )(*x_parts, *oa_parts, *op_parts, mod, norm2_g.reshape(1, D), *ws)


SC = 256
CPS = SC // CHUNK
BASE = 16
DELTA_HEAD_ROWS = 4096


def _mm(a, b):
    return jnp.dot(a.astype(jnp.bfloat16), b.astype(jnp.bfloat16), preferred_element_type=jnp.float32)


def _mm_nt(a, b):
    return lax.dot_general(a.astype(jnp.bfloat16), b.astype(jnp.bfloat16), (((1,), (1,)), ((), ())),
                           preferred_element_type=jnp.float32)


def _softplus(x):
    return jnp.maximum(x, 0.0) + jnp.log(1.0 + jnp.exp(-jnp.abs(x)))


def _delta_kernel(sc_ref, xq_ref, xk_ref, xv_ref, z_ref, bac_ref, bar_ref, cwq_ref, cwk_ref, cwv_ref,
                  og_ref, s0_ref, o_ref, st_ref, q_s, k_s, v_s, o_s, vn_s, *, n_sc, zero_init, hpb):
    hb = pl.program_id(1)
    L = q_s.shape[1]

    def conv(x_ref, w_ref, cs):
        x = x_ref[:, cs]
        row = lax.broadcasted_iota(jnp.int32, x.shape, 0)
        acc = x * w_ref[CONV_K // 2:CONV_K // 2 + 1, cs]
        for j in range(CONV_K):
            d = j - CONV_K // 2
            if d == 0:
                continue
            xs = pltpu.roll(x, (-d) % L, 0)
            ok = (row + d >= 0) & (row + d < L)
            acc = acc + jnp.where(ok, xs, 0.0) * w_ref[j:j + 1, cs]
        return acc * jax.nn.sigmoid(acc)

    for hh in range(hpb):
        cs = slice(hh * DK, (hh + 1) * DK)
        q = conv(xq_ref, cwq_ref, cs)
        q_s[hh] = q * lax.rsqrt(jnp.sum(q * q, axis=-1, keepdims=True) + EPS) * (DK ** -0.5)
        k = conv(xk_ref, cwk_ref, cs)
        k_s[hh] = k * lax.rsqrt(jnp.sum(k * k, axis=-1, keepdims=True) + EPS)
        v_s[hh] = conv(xv_ref, cwv_ref, cs)
    o_s[...] = jnp.zeros_like(o_s)

    ri = lax.broadcasted_iota(jnp.int32, (SC, SC), 0)
    ci = lax.broadcasted_iota(jnp.int32, (SC, SC), 1)
    same = (ri // CHUNK) == (ci // CHUNK)
    same_base = (ri // BASE) == (ci // BASE)
    merge_masks = [(ri // w) == (ci // w) for w in (2 * BASE, CHUNK)]
    eye = (ri == ci).astype(jnp.float32)
    rowi = lax.broadcasted_iota(jnp.int32, (SC, DV), 0)

    def prep(m, d, hh):
        r0 = pl.multiple_of(m * SC, SC)
        h = hb * hpb + hh
        q = q_s[hh, pl.ds(r0, SC), :]
        k = k_s[hh, pl.ds(r0, SC), :]
        v = v_s[hh, pl.ds(r0, SC), :]
        bc = bac_ref[0, hh, pl.ds(r0, SC), :]
        br = bar_ref[0, hh, m]
        a_l = sc_ref[d * H_A + h]
        dtb = sc_ref[2 * H_A + d * H_A + h]
        neg_ea = -jnp.exp(jnp.full((1, 1), a_l, jnp.float32))
        beta = jax.nn.sigmoid(bc[:, d:d + 1])
        g_col = neg_ea * _softplus(bc[:, 2 + d:3 + d] + dtb)
        g_row = neg_ea * _softplus(br[2 + d:3 + d, :] + dtb)
        if d == 0:
            tri, strict = same & (ci <= ri), same & (ci < ri)
        else:
            tri, strict = same & (ci >= ri), same & (ci > ri)
        tri_t = same & (ri <= ci) if d == 0 else same & (ri >= ci)
        gc_col = jnp.sum(jnp.where(tri, g_row, 0.0), axis=1, keepdims=True)
        gc_row = jnp.sum(jnp.where(tri_t, g_col, 0.0), axis=0, keepdims=True)
        gl_col = jnp.sum(jnp.where(same, g_row, 0.0), axis=1, keepdims=True)
        decay = jnp.where(tri, jnp.exp(jnp.where(tri, gc_col - gc_row, 0.0)), 0.0)
        kb = k * beta
        a = jnp.where(strict, _mm_nt(kb, k) * decay, 0.0)
        attn = jnp.where(tri, _mm_nt(q, k) * decay, 0.0)
        eg = jnp.exp(gc_col)
        x = jnp.concatenate([v * beta, kb * eg], axis=1)
        qd = q * eg
        kdt = (k * jnp.exp(gl_col - gc_col)).T
        return dict(r0=r0, a=a, attn=attn, x=x, qd=qd, kdt=kdt, egl=jnp.exp(gl_col))

    def run_chains(ms, states):
        n = len(chains)
        ops = [prep(ms[i], d, hh) for i, (hh, d) in enumerate(chains)]
        ps = [jnp.where(same_base, o["a"], 0.0) for o in ops]
        ts = [eye - p for p in ps]
        for _ in range(BASE.bit_length() - 2):
            ps = [_mm(p, p) for p in ps]
            ts = [t + _mm(t, p) for t, p in zip(ts, ps)]
        inner = same_base
        for outer in merge_masks:
            lows = [_mm(jnp.where(outer & ~inner, o["a"], 0.0), t) for o, t in zip(ops, ts)]
            ts = [t - _mm(t, low) for t, low in zip(ts, lows)]
            inner = outer
        xs = [_mm(t, o["x"]) for t, o in zip(ts, ops)]
        for i in range(n):
            vn_s[i] = jnp.zeros((SC, DV), jnp.float32)
        states = list(states)
        for step in range(CPS):
            cs = [step if d == 0 else CPS - 1 - step for _, d in chains]
            los = [c * CHUNK for c in cs]
            ws_qs = [_mm(jnp.concatenate([x[lo:lo + CHUNK, DV:], o["qd"][lo:lo + CHUNK]], axis=0), s)
                     for x, o, lo, s in zip(xs, ops, los, states)]
            for i in range(n):
                vn_s[i, los[i]:los[i] + CHUNK, :] = xs[i][los[i]:los[i] + CHUNK, :DV] - ws_qs[i][:CHUNK]
            vns = [vn_s[i] for i in range(n)]
            o_cs = [wq[CHUNK:] + _mm(o["attn"][lo:lo + CHUNK, :], vn)
                    for wq, o, lo, vn in zip(ws_qs, ops, los, vns)]
            for i, (hh, _) in enumerate(chains):
                o_s[hh, pl.ds(ops[i]["r0"] + los[i], CHUNK), :] += o_cs[i]
            states = [s * o["egl"][lo:lo + 1, :]
                      + _mm(o["kdt"], jnp.where((rowi >= lo) & (rowi < lo + CHUNK), vn, 0.0))
                      for s, o, lo, vn in zip(states, ops, los, vns)]
        return tuple(states)

    if zero_init:
        states = tuple(jnp.zeros((DK, DV), jnp.float32) for _ in range(2 * hpb))
    else:
        states = tuple(s0_ref[0, d, hh] for hh in range(hpb) for d in range(2))

    chains = [(hh, d) for hh in range(hpb) for d in range(2)]

    def body(m, carry):
        return run_chains([m if d == 0 else n_sc - 1 - m for _, d in chains], carry)

    if n_sc == 1:
        states = body(0, states)
    else:
        states = lax.fori_loop(0, n_sc, body, states)

    for hh in range(hpb):
        for d in range(2):
            st_ref[0, d, hh] = states[2 * hh + d]
        o = o_s[hh]
        o = o * lax.rsqrt(jnp.mean(o * o, axis=-1, keepdims=True) + EPS) * og_ref[...]
        zz = z_ref[:, hh * DV:(hh + 1) * DV]
        o_ref[:, hh * DV:(hh + 1) * DV] = o * (zz * jax.nn.sigmoid(zz))


def _delta_call(qkv, z, ba, conv_w, a_log, dt_bias, onorm_g, s0, B, L, row_blk0):
    n_sc = L // SC
    t0 = row_blk0 * L
    bah = ba[t0:t0 + B * L, :4 * H_A].reshape(B, L, 4, H_A).transpose(0, 3, 1, 2)
    bar = bah.reshape(B, H_A, n_sc, SC, 4).transpose(0, 1, 2, 4, 3)
    scal = jnp.concatenate([a_log.reshape(-1), dt_bias.reshape(-1)]).astype(jnp.float32)
    hpb = max(1, min(H_A, DELTA_HEAD_ROWS // L))
    n_hb = H_A // hpb
    zero_init = s0 is None
    if zero_init:
        s0 = jnp.zeros((1, 2, hpb, DK, DV), jnp.float32)
        s0_spec = pl.BlockSpec((1, 2, hpb, DK, DV), lambda b, h, sc: (0, 0, 0, 0, 0))
    else:
        s0_spec = pl.BlockSpec((1, 2, hpb, DK, DV), lambda b, h, sc: (b, 0, h, 0, 0))

    def col(off):
        return pl.BlockSpec((L, hpb * DK), lambda b, h, sc: (row_blk0 + b, off * n_hb + h))

    def cw(off):
        return pl.BlockSpec((CONV_K, hpb * DK), lambda b, h, sc: (0, off * n_hb + h))

    kern = functools.partial(_delta_kernel, n_sc=n_sc, zero_init=zero_init, hpb=hpb)
    return pl.pallas_call(
        kern,
        grid_spec=pltpu.PrefetchScalarGridSpec(
            num_scalar_prefetch=1,
            grid=(B, n_hb),
            in_specs=[col(0), col(1), col(2),
                      pl.BlockSpec((L, hpb * DV), lambda b, h, sc: (row_blk0 + b, h)),
                      pl.BlockSpec((1, hpb, L, 4), lambda b, h, sc: (b, h, 0, 0)),
                      pl.BlockSpec((1, hpb, n_sc, 4, SC), lambda b, h, sc: (b, h, 0, 0, 0)),
                      cw(0), cw(1), cw(2),
                      pl.BlockSpec((1, DV), lambda b, h, sc: (0, 0)),
                      s0_spec],
            out_specs=[pl.BlockSpec((L, hpb * DV), lambda b, h, sc: (b, h)),
                       pl.BlockSpec((1, 2, hpb, DK, DV), lambda b, h, sc: (b, 0, h, 0, 0))],
            scratch_shapes=[pltpu.VMEM((hpb, L, DK), jnp.float32), pltpu.VMEM((hpb, L, DK), jnp.float32),
                            pltpu.VMEM((hpb, L, DV), jnp.float32), pltpu.VMEM((hpb, L, DV), jnp.float32),
                            pltpu.VMEM((2 * hpb, SC, DV), jnp.float32)]),
        out_shape=[jax.ShapeDtypeStruct((B * L, D_A), jnp.float32),
                   jax.ShapeDtypeStruct((B, 2, H_A, DK, DV), jnp.float32)],
        compiler_params=pltpu.CompilerParams(dimension_semantics=("arbitrary", "arbitrary"),
                                             vmem_limit_bytes=VMEM_LIMIT),
    )(scal, qkv, qkv, qkv, z, bah, bar, conv_w, conv_w, conv_w, onorm_g.reshape(1, DV), s0)


TR = 256
GSZ = N_EXPERTS // N_GROUPS
NEG = -jnp.inf
BM = 256
SUBLANES = 8
FILL_PIECES = tuple(p for p in (BM >> s for s in range(1, BM.bit_length())) if p >= SUBLANES)


def _route_kernel(h_ref, rwh_ref, rwl_ref, rb_ref, idx_ref, rank_ref, w_ref, cnt_ref, cnt_s):
    i = pl.program_id(0)

    @pl.when(i == 0)
    def _():
        cnt_s[...] = jnp.zeros_like(cnt_s)

    h = h_ref[...]
    hh, hl = _split_bf16(h)
    logits = _bdot(hh, rwh_ref[...]) + (_bdot(hh, rwl_ref[...]) + _bdot(hl, rwh_ref[...]))
    scores = jax.nn.sigmoid(logits.T)
    sel = scores + rb_ref[...]
    erow = lax.broadcasted_iota(jnp.int32, sel.shape, 0)
    grow = lax.broadcasted_iota(jnp.int32, (GSZ, TR), 0)

    def first_argmax(v, rows):
        m = jnp.max(v, axis=0, keepdims=True)
        first = jnp.min(jnp.where(v == m, rows, N_EXPERTS), axis=0, keepdims=True)
        return m, first

    gs = []
    for g in range(N_GROUPS):
        vg = sel[g * GSZ:(g + 1) * GSZ, :]
        m1, i1 = first_argmax(vg, grow)
        m2 = jnp.max(jnp.where(grow == i1, NEG, vg), axis=0, keepdims=True)
        gs.append(m1 + m2)
    cand = []
    for g in range(N_GROUPS):
        beat = jnp.zeros(gs[g].shape, jnp.int32)
        for o in range(N_GROUPS):
            if o == g:
                continue
            wins = (gs[o] > gs[g]) | ((gs[o] == gs[g]) & (o < g))
            beat = beat + wins.astype(jnp.int32)
        cand.append(jnp.where(beat < TOPK_GROUP, sel[g * GSZ:(g + 1) * GSZ, :], NEG))
    cand = jnp.concatenate(cand, axis=0)
    chosen = []
    picked = jnp.zeros(sel.shape, jnp.bool_)
    for _ in range(TOP_K):
        _, ik = first_argmax(cand, erow)
        hit = erow == ik
        chosen.append((ik, hit))
        picked = picked | hit
        cand = jnp.where(hit, NEG, cand)
    wsum = jnp.sum(jnp.where(picked, scores, 0.0), axis=0, keepdims=True)

    ri = lax.broadcasted_iota(jnp.int32, (TR, TR), 0)
    ci = lax.broadcasted_iota(jnp.int32, (TR, TR), 1)
    earlier = (ri < ci).astype(jnp.bfloat16)
    rank_mat = _bdot(picked.astype(jnp.bfloat16), earlier) + cnt_s[...]
    cnt_s[...] = cnt_s[...] + jnp.sum(picked.astype(jnp.float32), axis=1, keepdims=True)
    cnt_ref[...] = cnt_s[...].astype(jnp.int32)

    for k, (ik, hit) in enumerate(chosen):
        idx_ref[0, k:k + 1, :] = ik
        rank_ref[0, k:k + 1, :] = jnp.sum(jnp.where(hit, rank_mat, 0.0), axis=0, keepdims=True).astype(jnp.int32)
        w_ref[0, k:k + 1, :] = jnp.sum(jnp.where(hit, scores, 0.0), axis=0, keepdims=True) / wsum * ROUTED_SCALE


def _route_call(hf, router_w, router_bias):
    T, D = hf.shape
    n_tiles = T // TR
    rwh, rwl = _split_bf16(router_w)
    row_spec = pl.BlockSpec((1, TOP_K, TR), lambda i: (i, 0, 0))
    return pl.pallas_call(
        _route_kernel,
        grid=(n_tiles,),
        in_specs=[pl.BlockSpec((TR, D), lambda i: (i, 0)),
                  pl.BlockSpec((D, N_EXPERTS), lambda i: (0, 0)),
                  pl.BlockSpec((D, N_EXPERTS), lambda i: (0, 0)),
                  pl.BlockSpec((N_EXPERTS, 1), lambda i: (0, 0))],
        out_specs=[row_spec, row_spec, row_spec, pl.BlockSpec((N_EXPERTS, 1), lambda i: (0, 0))],
        scratch_shapes=[pltpu.VMEM((N_EXPERTS, 1), jnp.float32)],
        out_shape=[jax.ShapeDtypeStruct((n_tiles, TOP_K, TR), jnp.int32),
                   jax.ShapeDtypeStruct((n_tiles, TOP_K, TR), jnp.int32),
                   jax.ShapeDtypeStruct((n_tiles, TOP_K, TR), jnp.float32),
                   jax.ShapeDtypeStruct((N_EXPERTS, 1), jnp.int32)],
        compiler_params=pltpu.CompilerParams(dimension_semantics=("arbitrary",)),
    )(hf, rwh, rwl, router_bias.reshape(N_EXPERTS, 1).astype(jnp.float32))


def _dispatch_kernel(fill_ref, idx_ref, rank_ref, pstart_ref, h_ref, pos_ref, xs_hbm, pos_v, pos_s, zbuf,
                     ssem, psem, zsem):
    n_blk = xs_hbm.shape[0] // BM

    @pl.when(pl.program_id(0) == 0)
    def _():
        zbuf[...] = jnp.zeros_like(zbuf)

        def pad_copies(e, act):
            n = fill_ref[N_EXPERTS + e]
            start = fill_ref[e]
            head = jnp.minimum((-start) & (SUBLANES - 1), n)
            for j in range(SUBLANES - 1):
                @pl.when(j < head)
                def _(j=j):
                    act(pltpu.make_async_copy(zbuf.at[0], xs_hbm.at[start + j], zsem))
            off = start + head
            rest = n - head
            for piece in FILL_PIECES:
                @pl.when((rest & piece) != 0)
                def _(off=off, piece=piece):
                    dst = xs_hbm.at[pl.ds(pl.multiple_of(off, SUBLANES), piece)]
                    act(pltpu.make_async_copy(zbuf.at[pl.ds(0, piece)], dst, zsem))
                off = off + (rest & piece)

        def tail_copy(b, act):
            act(pltpu.make_async_copy(zbuf, xs_hbm.at[pl.ds(pl.multiple_of(b * BM, BM), BM)], zsem))

        for act in (lambda c: c.start(), lambda c: c.wait()):
            def per_expert(e, carry, act=act):
                pad_copies(e, act)
                return carry

            def per_block(b, carry, act=act):
                tail_copy(b, act)
                return carry

            lax.fori_loop(0, N_EXPERTS, per_expert, 0)
            lax.fori_loop(fill_ref[2 * N_EXPERTS], n_blk, per_block, 0)

    erow = lax.broadcasted_iota(jnp.int32, (N_EXPERTS, TR), 0)
    pstart = pstart_ref[...]
    for k in range(TOP_K):
        hit = erow == idx_ref[0, k:k + 1, :]
        seg = jnp.sum(jnp.where(hit, pstart, 0), axis=0, keepdims=True)
        pos_v[k:k + 1, :] = seg + rank_ref[0, k:k + 1, :]
    pos_ref[0] = pos_v[...]
    cp = pltpu.make_async_copy(pos_v, pos_s, psem)
    cp.start()
    cp.wait()

    def body(t, carry):
        for k in range(TOP_K):
            pltpu.make_async_copy(h_ref.at[t], xs_hbm.at[pos_s[k, t]], ssem).start()
        return carry

    lax.fori_loop(0, TR, body, 0, unroll=8)
    n_rows = TR * TOP_K
    pltpu.make_async_copy(xs_hbm.at[pl.ds(0, n_rows)], xs_hbm.at[pl.ds(0, n_rows)], ssem).wait()


def _dispatch_call(hf, idx, rank, pad_start, fill_tab, n_pad):
    T, D = hf.shape
    n_tiles = T // TR
    row_spec = pl.BlockSpec((1, TOP_K, TR), lambda i, ft: (i, 0, 0))
    return pl.pallas_call(
        _dispatch_kernel,
        grid_spec=pltpu.PrefetchScalarGridSpec(
            num_scalar_prefetch=1,
            grid=(n_tiles,),
            in_specs=[row_spec, row_spec,
                      pl.BlockSpec((N_EXPERTS, 1), lambda i, ft: (0, 0)),
                      pl.BlockSpec((TR, D), lambda i, ft: (i, 0))],
            out_specs=[row_spec, pl.BlockSpec(memory_space=pl.ANY)],
            scratch_shapes=[pltpu.VMEM((TOP_K, TR), jnp.int32), pltpu.SMEM((TOP_K, TR), jnp.int32),
                            pltpu.VMEM((BM, D), hf.dtype),
                            pltpu.SemaphoreType.DMA, pltpu.SemaphoreType.DMA, pltpu.SemaphoreType.DMA]),
        out_shape=[jax.ShapeDtypeStruct((n_tiles, TOP_K, TR), jnp.int32),
                   jax.ShapeDtypeStruct((n_pad, D), hf.dtype)],
        compiler_params=pltpu.CompilerParams(dimension_semantics=("arbitrary",)),
    )(fill_tab, idx, rank, pad_start.reshape(N_EXPERTS, 1), hf)


def _expert_kernel(blk_e_ref, nvalid_ref, nused_ref, x_ref, wg_ref, wu_ref, wd_ref, y_ref, wg_s, wu_s, wd_s):
    i = pl.program_id(0)

    @pl.when(i < nused_ref[0])
    def _():
        e = blk_e_ref[i]
        prev = blk_e_ref[jnp.maximum(i - 1, 0)]

        @pl.when((i == 0) | (e != prev))
        def _():
            wg_s[...] = wg_ref[0].astype(jnp.bfloat16)
            wu_s[...] = wu_ref[0].astype(jnp.bfloat16)
            wd_s[...] = wd_ref[0].astype(jnp.bfloat16)

        row = lax.broadcasted_iota(jnp.int32, (BM, 1), 0)
        xa, xb = _unpack_rows(jnp.where(row < nvalid_ref[i], x_ref[...], jnp.uint32(0)))
        xa = xa.astype(jnp.bfloat16)
        xb = xb.astype(jnp.bfloat16)
        half = xa.shape[1]
        g = _bdot(xa, wg_s[:half, :]) + _bdot(xb, wg_s[half:, :])
        u = _bdot(xa, wu_s[:half, :]) + _bdot(xb, wu_s[half:, :])
        a = (g * jax.nn.sigmoid(g)) * u
        y_ref[...] = _pack_rows(_bdot(a.astype(jnp.bfloat16), wd_s[...]))

    @pl.when(i >= nused_ref[0])
    def _():
        y_ref[...] = jnp.zeros_like(y_ref)


def _expert_call(x_sorted, blk_e, n_valid, n_used, w_gate, w_up, w_down):
    n_pad, DH = x_sorted.shape
    n_blk = n_pad // BM
    E, D, F = w_gate.shape

    def row_map(i, be, nv, nu):
        return (jnp.minimum(i, nu[0] - 1), 0)

    def w_map(i, be, nv, nu):
        return (be[jnp.minimum(i, nu[0] - 1)], 0, 0)

    return pl.pallas_call(
        _expert_kernel,
        grid_spec=pltpu.PrefetchScalarGridSpec(
            num_scalar_prefetch=3,
            grid=(n_blk,),
            in_specs=[pl.BlockSpec((BM, DH), row_map),
                      pl.BlockSpec((1, D, F), w_map),
                      pl.BlockSpec((1, D, F), w_map),
                      pl.BlockSpec((1, F, D), w_map)],
            out_specs=pl.BlockSpec((BM, DH), lambda i, be, nv, nu: (i, 0)),
            scratch_shapes=[pltpu.VMEM((D, F), jnp.bfloat16), pltpu.VMEM((D, F), jnp.bfloat16),
                            pltpu.VMEM((F, D), jnp.bfloat16)]),
        out_shape=jax.ShapeDtypeStruct((n_pad, DH), jnp.uint32),
        compiler_params=pltpu.CompilerParams(dimension_semantics=("arbitrary",),
                                             vmem_limit_bytes=VMEM_LIMIT),
    )(blk_e, n_valid, n_used, x_sorted, w_gate, w_up, w_down)


TC = 128


def _combine_kernel(pos_hbm, y_hbm, w_ref, x1_ref, sh_ref, mod_ref, fg_ref, out_ref, ybuf, pos_s, gsem, psem):
    j = pl.program_id(0)
    last = pl.num_programs(0) - 1

    def pos_copy(b, slot):
        return pltpu.make_async_copy(pos_hbm.at[b], pos_s.at[slot], psem.at[slot])

    def start_gather(slot):
        for k in range(TOP_K):
            for t in range(TC):
                pltpu.make_async_copy(y_hbm.at[pos_s[slot, k * TC + t]], ybuf.at[slot, k, t],
                                      gsem.at[slot]).start()

    def wait_gather(slot):
        pltpu.make_async_copy(ybuf.at[slot], ybuf.at[slot], gsem.at[slot]).wait()

    @pl.when(j == 0)
    def _():
        pos_copy(0, 0).start()
        pos_copy(0, 0).wait()
        start_gather(0)
        pos_copy(jnp.minimum(1, last), 1).start()

    slot = j % 2
    nslot = 1 - slot
    pos_copy(0, nslot).wait()
    start_gather(nslot)
    pos_copy(jnp.minimum(j + 2, last), slot).start()
    wait_gather(slot)
    w = w_ref[...]
    acc_a = acc_b = None
    for k in range(TOP_K):
        ya, yb = _unpack_rows(ybuf[slot, k])
        acc_a = w[:, k:k + 1] * ya if k == 0 else acc_a + w[:, k:k + 1] * ya
        acc_b = w[:, k:k + 1] * yb if k == 0 else acc_b + w[:, k:k + 1] * yb
    acc = jnp.concatenate([acc_a, acc_b], axis=1)
    x2 = x1_ref[...] + mod_ref[0, 5:6, :] * (acc + sh_ref[...])
    out_ref[...] = x2 * lax.rsqrt(jnp.mean(x2 * x2, axis=-1, keepdims=True) + EPS) * fg_ref[...]

    @pl.when(j == last)
    def _():
        wait_gather(nslot)
        pos_copy(0, slot).wait()


def _combine_call(y_sorted, pos_t, wts, x1, shared, mod, final_g, n_ctx, lat_len):
    T, K = wts.shape
    DH = y_sorted.shape[1]
    D = 2 * DH
    n_tiles = T // TC
    row = functools.partial(_mod_row, tokens_per_tile=TC, n_ctx=n_ctx, lat_len=lat_len)
    return pl.pallas_call(
        _combine_kernel,
        grid=(n_tiles,),
        in_specs=[pl.BlockSpec(memory_space=pl.ANY),
                  pl.BlockSpec(memory_space=pl.ANY),
                  pl.BlockSpec((TC, K), lambda j: (j, 0)),
                  pl.BlockSpec((TC, D), lambda j: (j, 0)),
                  pl.BlockSpec((TC, D), lambda j: (j, 0)),
                  pl.BlockSpec((1, N_MOD, D), lambda j: (row(j), 0, 0)),
                  pl.BlockSpec((1, D), lambda j: (0, 0))],
        out_specs=pl.BlockSpec((TC, D), lambda j: (j, 0)),
        scratch_shapes=[pltpu.VMEM((2, K, TC, DH), jnp.uint32),
                        pltpu.SMEM((2, K * TC), jnp.int32),
                        pltpu.SemaphoreType.DMA((2,)), pltpu.SemaphoreType.DMA((2,))],
        out_shape=jax.ShapeDtypeStruct((T, D), jnp.float32),
        compiler_params=pltpu.CompilerParams(dimension_semantics=("arbitrary",)),
    )(pos_t, y_sorted, wts, x1, shared, mod, final_g.reshape(1, D))


SC_CORES = 2
SC_SUBCORES = 16
SC_CHUNK = 128


def _sc_gather_call(table, idx):
    n_idx = idx.shape[0]
    width = table.shape[1]
    n_workers = SC_CORES * SC_SUBCORES
    per_worker = n_idx // n_workers
    assert per_worker * n_workers == n_idx and per_worker % SC_CHUNK == 0
    mesh = plsc.VectorSubcoreMesh(core_axis_name="c", subcore_axis_name="s")

    def body(table_hbm, idx_hbm, out_hbm, idx_v, rows_v, sem):
        wid = lax.axis_index("s") * SC_CORES + lax.axis_index("c")
        base = wid * per_worker

        @pl.loop(0, per_worker // SC_CHUNK)
        def _(ch):
            off = base + ch * SC_CHUNK
            pltpu.sync_copy(idx_hbm.at[pl.ds(off, SC_CHUNK)], idx_v)
            pltpu.async_copy(table_hbm.at[idx_v], rows_v, sem).wait()
            pltpu.sync_copy(rows_v, out_hbm.at[pl.ds(off, SC_CHUNK)])

    return pl.kernel(
        body, out_type=jax.ShapeDtypeStruct((n_idx, width), table.dtype), mesh=mesh,
        scratch_types=[pltpu.VMEM((SC_CHUNK,), jnp.int32), pltpu.VMEM((SC_CHUNK, width), table.dtype),
                       pltpu.SemaphoreType.DMA],
    )(table, idx)


def _combine_dense_kernel(g_ref, w_ref, x1_ref, sh_ref, mod_ref, fg_ref, out_ref):
    w = w_ref[...]
    acc_a = acc_b = None
    for k in range(TOP_K):
        ya, yb = _unpack_rows(lax.bitcast_convert_type(g_ref[0, k], jnp.uint32))
        acc_a = w[:, k:k + 1] * ya if k == 0 else acc_a + w[:, k:k + 1] * ya
        acc_b = w[:, k:k + 1] * yb if k == 0 else acc_b + w[:, k:k + 1] * yb
    acc = jnp.concatenate([acc_a, acc_b], axis=1)
    x2 = x1_ref[...] + mod_ref[0, 5:6, :] * (acc + sh_ref[...])
    out_ref[...] = x2 * lax.rsqrt(jnp.mean(x2 * x2, axis=-1, keepdims=True) + EPS) * fg_ref[...]


def _combine_dense_call(gathered, wts, x1, shared, mod, final_g, n_ctx, lat_len):
    T, K = wts.shape
    DH = gathered.shape[-1]
    D = 2 * DH
    row = functools.partial(_mod_row, tokens_per_tile=TC, n_ctx=n_ctx, lat_len=lat_len)
    return pl.pallas_call(
        _combine_dense_kernel,
        grid=(T // TC,),
        in_specs=[pl.BlockSpec((1, K, TC, DH), lambda j: (j, 0, 0, 0)),
                  pl.BlockSpec((TC, K), lambda j: (j, 0)),
                  pl.BlockSpec((TC, D), lambda j: (j, 0)),
                  pl.BlockSpec((TC, D), lambda j: (j, 0)),
                  pl.BlockSpec((1, N_MOD, D), lambda j: (row(j), 0, 0)),
                  pl.BlockSpec((1, D), lambda j: (0, 0))],
        out_specs=pl.BlockSpec((TC, D), lambda j: (j, 0)),
        out_shape=jax.ShapeDtypeStruct((T, D), jnp.float32),
        compiler_params=pltpu.CompilerParams(dimension_semantics=("arbitrary",)),
    )(gathered, wts, x1, shared, mod, final_g.reshape(1, D))


def _moe_routed(h2, h2p, router_w, router_bias, w_gate, w_up, w_down):
    T, D = h2.shape
    idx, rank, w_rows, cnt = _route_call(h2, router_w, router_bias)
    wts = w_rows.transpose(0, 2, 1).reshape(T, TOP_K)
    counts = cnt[:, 0]
    padded = (counts + BM - 1) // BM * BM
    pad_end = jnp.cumsum(padded)
    pad_start = (pad_end - padded).astype(jnp.int32)
    n_pad = T * TOP_K + N_EXPERTS * BM
    n_blk = n_pad // BM
    n_used = (pad_end[-1] // BM).astype(jnp.int32).reshape(1)
    fill_tab = jnp.concatenate([pad_start + counts, padded - counts, n_used]).astype(jnp.int32)
    pos, x_sorted = _dispatch_call(h2p, idx, rank, pad_start, fill_tab, n_pad)
    blk_row0 = jnp.arange(n_blk, dtype=jnp.int32) * BM
    blk_e = jnp.minimum(jnp.sum((pad_end[None, :] <= blk_row0[:, None]).astype(jnp.int32), axis=1), N_EXPERTS - 1)
    own = blk_e[:, None] == jnp.arange(N_EXPERTS, dtype=jnp.int32)[None, :]
    seg_end = jnp.sum(jnp.where(own, (pad_start + counts)[None, :], 0), axis=1)
    n_valid = jnp.clip(seg_end - blk_row0, 0, BM).astype(jnp.int32)
    y = _expert_call(x_sorted, blk_e, n_valid, n_used, w_gate, w_up, w_down)
    pos_t = pos.reshape(T // TR, TOP_K, TR // TC, TC).transpose(0, 2, 1, 3).reshape(T // TC, TOP_K * TC)
    return y, pos_t, wts


def kernel(x_prompt, x_sample, state_delta, c, c_ctx, w_ada, b_ada, norm1_g, w_in, conv_w, a_log,
           dt_bias, onorm_g, pool_w, pool_scale, w_out, norm2_g, router_w, router_bias, exp_w_gate,
           exp_w_up, exp_w_down, sh_w_gate, sh_w_up, sh_w_down, final_g):
    Bc, Lc, D = x_prompt.shape
    Bl, Ll, _ = x_sample.shape
    n_ctx = Bc * Lc
    assert DEPTH == 1 and 1 + Bl <= MOD_ROWS and n_ctx % Ll == 0
    x_parts = (x_prompt.reshape(n_ctx, D), x_sample.reshape(Bl * Ll, D))
    cvec = jnp.concatenate([c_ctx[None], c, jnp.zeros((MOD_ROWS - 1 - Bl, D), c.dtype)], axis=0)
    l = 0
    mod = _ada_call(cvec, w_ada[l], b_ada[l]).reshape(MOD_ROWS, N_MOD, D)
    qkv, z, ba, u = _inproj_call(*x_parts, mod, norm1_g[l], w_in[l], Ll)
    dn = (conv_w[l], a_log[l], dt_bias[l], onorm_g[l])
    oa_c, st_ctx = _delta_call(qkv, z, ba, *dn, None, Bc, Lc, 0)
    oa_l, _ = _delta_call(qkv, z, ba, *dn, state_delta[:, l], Bl, Ll, n_ctx // Ll)
    op_c = _pool_call(u, pool_w[l], pool_scale[l], False, Bc, Lc, 0)
    op_l = _pool_call(u, pool_w[l], pool_scale[l], True, Bl, Ll, n_ctx // Ll)
    x1, h2, h2p, shared = _outproj_call(x_parts, (oa_c, oa_l), (op_c, op_l), mod, norm2_g[l], w_out[l],
                                        sh_w_gate[l], sh_w_up[l], sh_w_down[l], Ll)
    y, pos_t, wts = _moe_routed(h2, h2p, router_w[l], router_bias[l], exp_w_gate[l], exp_w_up[l],
                                exp_w_down[l])
    T = n_ctx + Bl * Ll
    gathered = _sc_gather_call(lax.bitcast_convert_type(y, jnp.int32), pos_t.reshape(-1))
    out = _combine_dense_call(gathered.reshape(T // TC, TOP_K, TC, D // 2), wts, x1, shared, mod, final_g,
                              n_ctx, Ll)
    y_prompt = out[:n_ctx].reshape(Bc, Lc, D)
    y_sample = out[n_ctx:].reshape(Bl, Ll, D)
    new_state_delta = st_ctx[:, None].astype(x_prompt.dtype)
    return (y_prompt, y_sample, new_state_delta)
```

```python
import functools
import jax, jax.numpy as jnp
from jax import lax
from jax.experimental import pallas as pl
from jax.experimental.pallas import tpu as pltpu
from jax.experimental.pallas import tpu_sc as plsc

D_MODEL = 1024
DEPTH = 1
GRID_W = 64
D_MIX = D_MODEL
D_A = D_MIX // 2
D_P = D_MIX - D_A
H_A = 4
DK = D_A // H_A
DV = D_A // H_A
CONV_K = 5
CHUNK = 64
POOL_WINDOWS = (2, 4, 8, 16)
N_PG = len(POOL_WINDOWS)
PG = D_P // N_PG
N_EXPERTS = 256
TOP_K = 8
N_GROUPS = 8
TOPK_GROUP = 4
ROUTED_SCALE = 2.5
EPS = 1e-6
VMEM_LIMIT = 48 * 1024 * 1024


def _split_bf16(a):
    hi = a.astype(jnp.bfloat16)
    return hi, (a - hi.astype(jnp.float32)).astype(jnp.bfloat16)


def _bdot(a, b):
    return jnp.dot(a, b, preferred_element_type=jnp.float32)


def _pack_rows(x):
    m = x.shape[1] // 2
    hi = lax.bitcast_convert_type(x[:, :m].astype(jnp.bfloat16).astype(jnp.float32), jnp.uint32)
    lo = lax.bitcast_convert_type(x[:, m:].astype(jnp.bfloat16).astype(jnp.float32), jnp.uint32)
    return lax.bitcast_convert_type(hi | (lo >> 16), jnp.int32)


def _unpack_rows(p):
    p = lax.bitcast_convert_type(p, jnp.uint32)
    hi = lax.bitcast_convert_type(p & jnp.uint32(0xFFFF0000), jnp.float32)
    lo = lax.bitcast_convert_type(p << 16, jnp.float32)
    return hi, lo


N_MOD = 6
MOD_ROWS = 8
TM = 512


def _ada_kernel(c_ref, w_ref, b_ref, o_ref):
    c = c_ref[...]
    s = c * jax.nn.sigmoid(c)
    sh, sl = _split_bf16(s)
    wh, wl = _split_bf16(w_ref[...])
    o_ref[...] = _bdot(sh, wh) + (_bdot(sh, wl) + _bdot(sl, wh)) + b_ref[...]


def _ada_call(cvec, w_ada, b_ada):
    R, D = cvec.shape
    N = w_ada.shape[1]
    tn = 1024
    return pl.pallas_call(
        _ada_kernel,
        grid=(N // tn,),
        in_specs=[pl.BlockSpec((R, D), lambda j: (0, 0)),
                  pl.BlockSpec((D, tn), lambda j: (0, j)),
                  pl.BlockSpec((1, tn), lambda j: (0, j))],
        out_specs=pl.BlockSpec((R, tn), lambda j: (0, j)),
        out_shape=jax.ShapeDtypeStruct((R, N), jnp.float32),
    )(cvec, w_ada, b_ada.reshape(1, N))


def _mod_row(tile, tokens_per_tile, n_ctx, lat_len):
    t0 = tile * tokens_per_tile
    return jnp.where(t0 < n_ctx, 0, 1 + (t0 - n_ctx) // lat_len)


def _two_part_specs(n_ctx_tiles, width):
    return (pl.BlockSpec((TM, width), lambda i: (jnp.minimum(i, n_ctx_tiles - 1), 0)),
            pl.BlockSpec((TM, width), lambda i: (jnp.maximum(i - n_ctx_tiles, 0), 0)))


def _pick(n_ctx_tiles, ctx_ref, lat_ref):
    return jnp.where(pl.program_id(0) < n_ctx_tiles, ctx_ref[...], lat_ref[...])


def _inproj_kernel(xc_ref, xl_ref, mod_ref, g_ref, wq_ref, wz_ref, wb_ref, wu_ref, q_ref, z_ref, b_ref, u_ref,
                   *, n_ctx_tiles):
    x = _pick(n_ctx_tiles, xc_ref, xl_ref)
    y = x * lax.rsqrt(jnp.mean(x * x, axis=-1, keepdims=True) + EPS) * g_ref[...]
    h = (y * (1.0 + mod_ref[0, 1:2, :]) + mod_ref[0, 0:1, :]).astype(jnp.bfloat16)
    q_ref[...] = _bdot(h, wq_ref[...])
    z_ref[...] = _bdot(h, wz_ref[...])
    b_ref[...] = _bdot(h, wb_ref[...])
    u_ref[...] = _bdot(h, wu_ref[...])


def _inproj_call(x_ctx, x_lat, mod, norm1_g, w_in, lat_len):
    n_ctx, D = x_ctx.shape
    T = n_ctx + x_lat.shape[0]
    bf = jnp.bfloat16
    nq, nz, nb = 3 * D_A, D_A, 4 * H_A
    wq = w_in[:, :nq].astype(bf)
    wz = w_in[:, nq:nq + nz].astype(bf)
    wb = jnp.pad(w_in[:, nq + nz:nq + nz + nb], ((0, 0), (0, 128 - nb))).astype(bf)
    wu = w_in[:, nq + nz + nb:].astype(bf)
    row = functools.partial(_mod_row, tokens_per_tile=TM, n_ctx=n_ctx, lat_len=lat_len)

    def full(a):
        return pl.BlockSpec(a.shape, lambda i: (0, 0))

    def rows(n):
        return pl.BlockSpec((TM, n), lambda i: (i, 0))

    return pl.pallas_call(
        functools.partial(_inproj_kernel, n_ctx_tiles=n_ctx // TM),
        grid=(T // TM,),
        in_specs=[*_two_part_specs(n_ctx // TM, D), pl.BlockSpec((1, N_MOD, D), lambda i: (row(i), 0, 0)),
                  pl.BlockSpec((1, D), lambda i: (0, 0)), full(wq), full(wz), full(wb), full(wu)],
        out_specs=[rows(nq), rows(nz), rows(128), rows(D_P)],
        out_shape=[jax.ShapeDtypeStruct((T, nq), jnp.float32), jax.ShapeDtypeStruct((T, nz), jnp.float32),
                   jax.ShapeDtypeStruct((T, 128), jnp.float32), jax.ShapeDtypeStruct((T, D_P), jnp.float32)],
        compiler_params=pltpu.CompilerParams(dimension_semantics=("arbitrary",),
                                             vmem_limit_bytes=VMEM_LIMIT),
    )(x_ctx, x_lat, mod, norm1_g.reshape(1, D), wq, wz, wb, wu)


PT = 256


def _window_bounds(pos, w, n):
    return jnp.maximum(pos - w // 2, 0), jnp.minimum(pos + w - w // 2, n)


def _band_sum(band, x):
    xh, xl = _split_bf16(x)
    return _bdot(band, xh) + _bdot(band, xl)


def _pool_seq_kernel(u_ref, pw_ref, ps_ref, o_ref):
    L = u_ref.shape[0]
    ti = lax.broadcasted_iota(jnp.int32, (L, L), 0)
    ji = lax.broadcasted_iota(jnp.int32, (L, L), 1)
    tcol = lax.broadcasted_iota(jnp.int32, (L, 1), 0)
    for i, w in enumerate(POOL_WINDOWS):
        lo, hi = _window_bounds(ti, w, L)
        band = ((ji >= lo) & (ji < hi)).astype(jnp.bfloat16)
        clo, chi = _window_bounds(tcol, w, L)
        ug = u_ref[:, i * PG:(i + 1) * PG]
        mean = _band_sum(band, ug) / (chi - clo).astype(jnp.float32)
        d = (mean - ug).astype(jnp.bfloat16)
        o_ref[:, i * PG:(i + 1) * PG] = _bdot(d, pw_ref[i]) * ps_ref[:, i * PG:(i + 1) * PG]


def _pool_grid_kernel(u_ref, pw_ref, ps_ref, o_ref, pad_s, r_s):
    L = u_ref.shape[0]
    rows = L // GRID_W
    halo = (max(POOL_WINDOWS) // 2) * GRID_W
    pad_s[0:halo, :] = jnp.zeros((halo, D_P), jnp.float32)
    pad_s[halo + L:, :] = jnp.zeros((halo, D_P), jnp.float32)
    pad_s[halo:halo + L, :] = u_ref[...]
    ti = lax.broadcasted_iota(jnp.int32, (PT, PT), 0)
    ji = lax.broadcasted_iota(jnp.int32, (PT, PT), 1)
    tcol = lax.broadcasted_iota(jnp.int32, (PT, 1), 0)
    for i, w in enumerate(POOL_WINDOWS):
        cs = slice(i * PG, (i + 1) * PG)
        acc = None
        for dr in range(-(w // 2), w - w // 2):
            part = pad_s[halo + dr * GRID_W:halo + dr * GRID_W + L, cs]
            acc = part if acc is None else acc + part
        r_s[...] = acc
        lo, hi = _window_bounds(ti % GRID_W, w, GRID_W)
        band = ((ji // GRID_W == ti // GRID_W) & (ji % GRID_W >= lo) & (ji % GRID_W < hi)).astype(jnp.bfloat16)
        clo, chi = _window_bounds(tcol % GRID_W, w, GRID_W)
        ccnt = (chi - clo).astype(jnp.float32)
        for tile in range(L // PT):
            ts = slice(tile * PT, (tile + 1) * PT)
            rlo, rhi = _window_bounds(tile * (PT // GRID_W) + tcol // GRID_W, w, rows)
            mean = _band_sum(band, r_s[ts, :]) / ((rhi - rlo).astype(jnp.float32) * ccnt)
            d = (mean - u_ref[ts, cs]).astype(jnp.bfloat16)
            o_ref[ts, cs] = _bdot(d, pw_ref[i]) * ps_ref[:, cs]


def _pool_call(u, pool_w, pool_scale, grid, B, L, row_blk0):
    pw = pool_w.astype(jnp.bfloat16)
    ps = pool_scale.reshape(1, D_P)
    specs = dict(
        grid=(B,),
        in_specs=[pl.BlockSpec((L, D_P), lambda b: (row_blk0 + b, 0)),
                  pl.BlockSpec((N_PG, PG, PG), lambda b: (0, 0, 0)),
                  pl.BlockSpec((1, D_P), lambda b: (0, 0))],
        out_specs=pl.BlockSpec((L, D_P), lambda b: (b, 0)),
        out_shape=jax.ShapeDtypeStruct((B * L, D_P), jnp.float32),
        compiler_params=pltpu.CompilerParams(dimension_semantics=("arbitrary",),
                                             vmem_limit_bytes=VMEM_LIMIT))
    if not grid:
        return pl.pallas_call(_pool_seq_kernel, **specs)(u, pw, ps)
    halo = (max(POOL_WINDOWS) // 2) * GRID_W
    return pl.pallas_call(
        _pool_grid_kernel,
        scratch_shapes=[pltpu.VMEM((L + 2 * halo, D_P), jnp.float32), pltpu.VMEM((L, PG), jnp.float32)],
        **specs)(u, pw, ps)


def _outproj_kernel(xc_ref, xl_ref, oac_ref, oal_ref, opc_ref, opl_ref, mod_ref, g2_ref, wo_ref, sg_ref, su_ref,
                    sd_ref, x1_ref, h2_ref, h2p_ref, sh_ref, *, n_ctx_tiles):
    o_a = _pick(n_ctx_tiles, oac_ref, oal_ref)
    o_p = _pick(n_ctx_tiles, opc_ref, opl_ref)
    mix = (_bdot(o_a.astype(jnp.bfloat16), wo_ref[:D_A, :])
           + _bdot(o_p.astype(jnp.bfloat16), wo_ref[D_A:, :]))
    x1 = _pick(n_ctx_tiles, xc_ref, xl_ref) + mod_ref[0, 2:3, :] * mix
    x1_ref[...] = x1
    y = x1 * lax.rsqrt(jnp.mean(x1 * x1, axis=-1, keepdims=True) + EPS) * g2_ref[...]
    h2 = y * (1.0 + mod_ref[0, 4:5, :]) + mod_ref[0, 3:4, :]
    h2_ref[...] = h2
    h2p_ref[...] = _pack_rows(h2)
    hb = h2.astype(jnp.bfloat16)
    g = _bdot(hb, sg_ref[...])
    a = (g * jax.nn.sigmoid(g)) * _bdot(hb, su_ref[...])
    sh_ref[...] = _bdot(a.astype(jnp.bfloat16), sd_ref[...])


def _outproj_call(x_parts, oa_parts, op_parts, mod, norm2_g, w_out, sh_gate, sh_up, sh_down, lat_len):
    n_ctx, D = x_parts[0].shape
    T = n_ctx + x_parts[1].shape[0]
    nct = n_ctx // TM
    bf = jnp.bfloat16
    row = functools.partial(_mod_row, tokens_per_tile=TM, n_ctx=n_ctx, lat_len=lat_len)
    ws = [w_out.astype(bf), sh_gate.astype(bf), sh_up.astype(bf), sh_down.astype(bf)]

    def rows(n):
        return pl.BlockSpec((TM, n), lambda i: (i, 0))

    return pl.pallas_call(
        functools.partial(_outproj_kernel, n_ctx_tiles=nct),
        grid=(T // TM,),
        in_specs=[*_two_part_specs(nct, D), *_two_part_specs(nct, D_A), *_two_part_specs(nct, D_P),
                  pl.BlockSpec((1, N_MOD, D), lambda i: (row(i), 0, 0)),
                  pl.BlockSpec((1, D), lambda i: (0, 0))] + [pl.BlockSpec(w.shape, lambda i: (0, 0)) for w in ws],
        out_specs=[rows(D), rows(D), rows(D // 2), rows(D)],
        out_shape=[jax.ShapeDtypeStruct((T, D), jnp.float32), jax.ShapeDtypeStruct((T, D), jnp.float32),
                   jax.ShapeDtypeStruct((T, D // 2), jnp.int32), jax.ShapeDtypeStruct((T, D), jnp.float32)],
        compiler_params=pltpu.CompilerParams(dimension_semantics=("arbitrary",),
                                             vmem_limit_bytes=VMEM_LIMIT),
    )(*x_parts, *oa_parts, *op_parts, mod, norm2_g.reshape(1, D), *ws)


SC = 256
CPS = SC // CHUNK
BASE = 16
DELTA_HEAD_ROWS = 4096


def _mm(a, b):
    return jnp.dot(a.astype(jnp.bfloat16), b.astype(jnp.bfloat16), preferred_element_type=jnp.float32)


def _mm_nt(a, b):
    return lax.dot_general(a.astype(jnp.bfloat16), b.astype(jnp.bfloat16), (((1,), (1,)), ((), ())),
                           preferred_element_type=jnp.float32)


def _softplus(x):
    return jnp.maximum(x, 0.0) + jnp.log(1.0 + jnp.exp(-jnp.abs(x)))


def _delta_kernel(sc_ref, xq_ref, xk_ref, xv_ref, z_ref, bac_ref, bar_ref, cwq_ref, cwk_ref, cwv_ref,
                  og_ref, s0_ref, o_ref, st_ref, q_s, k_s, v_s, o_s, vn_s, *, n_sc, zero_init, hpb):
    hb = pl.program_id(1)
    L = q_s.shape[1]

    def conv(x_ref, w_ref, cs):
        x = x_ref[:, cs]
        row = lax.broadcasted_iota(jnp.int32, x.shape, 0)
        acc = x * w_ref[CONV_K // 2:CONV_K // 2 + 1, cs]
        for j in range(CONV_K):
            d = j - CONV_K // 2
            if d == 0:
                continue
            xs = pltpu.roll(x, (-d) % L, 0)
            ok = (row + d >= 0) & (row + d < L)
            acc = acc + jnp.where(ok, xs, 0.0) * w_ref[j:j + 1, cs]
        return acc * jax.nn.sigmoid(acc)

    for hh in range(hpb):
        cs = slice(hh * DK, (hh + 1) * DK)
        q = conv(xq_ref, cwq_ref, cs)
        q_s[hh] = q * lax.rsqrt(jnp.sum(q * q, axis=-1, keepdims=True) + EPS) * (DK ** -0.5)
        k = conv(xk_ref, cwk_ref, cs)
        k_s[hh] = k * lax.rsqrt(jnp.sum(k * k, axis=-1, keepdims=True) + EPS)
        v_s[hh] = conv(xv_ref, cwv_ref, cs)
    o_s[...] = jnp.zeros_like(o_s)

    ri = lax.broadcasted_iota(jnp.int32, (SC, SC), 0)
    ci = lax.broadcasted_iota(jnp.int32, (SC, SC), 1)
    same = (ri // CHUNK) == (ci // CHUNK)
    same_base = (ri // BASE) == (ci // BASE)
    merge_masks = [(ri // w) == (ci // w) for w in (2 * BASE, CHUNK)]
    eye = (ri == ci).astype(jnp.float32)
    rowi = lax.broadcasted_iota(jnp.int32, (SC, DV), 0)

    def prep(m, d, hh):
        r0 = pl.multiple_of(m * SC, SC)
        h = hb * hpb + hh
        q = q_s[hh, pl.ds(r0, SC), :]
        k = k_s[hh, pl.ds(r0, SC), :]
        v = v_s[hh, pl.ds(r0, SC), :]
        bc = bac_ref[0, hh, pl.ds(r0, SC), :]
        br = bar_ref[0, hh, m]
        a_l = sc_ref[d * H_A + h]
        dtb = sc_ref[2 * H_A + d * H_A + h]
        neg_ea = -jnp.exp(jnp.full((1, 1), a_l, jnp.float32))
        beta = jax.nn.sigmoid(bc[:, d:d + 1])
        g_col = neg_ea * _softplus(bc[:, 2 + d:3 + d] + dtb)
        g_row = neg_ea * _softplus(br[2 + d:3 + d, :] + dtb)
        if d == 0:
            tri, strict = same & (ci <= ri), same & (ci < ri)
        else:
            tri, strict = same & (ci >= ri), same & (ci > ri)
        tri_t = same & (ri <= ci) if d == 0 else same & (ri >= ci)
        gc_col = jnp.sum(jnp.where(tri, g_row, 0.0), axis=1, keepdims=True)
        gc_row = jnp.sum(jnp.where(tri_t, g_col, 0.0), axis=0, keepdims=True)
        gl_col = jnp.sum(jnp.where(same, g_row, 0.0), axis=1, keepdims=True)
        decay = jnp.where(tri, jnp.exp(jnp.where(tri, gc_col - gc_row, 0.0)), 0.0)
        kb = k * beta
        a = jnp.where(strict, _mm_nt(kb, k) * decay, 0.0)
        attn = jnp.where(tri, _mm_nt(q, k) * decay, 0.0)
        eg = jnp.exp(gc_col)
        x = jnp.concatenate([v * beta, kb * eg], axis=1)
        qd = q * eg
        kdt = (k * jnp.exp(gl_col - gc_col)).T
        return dict(r0=r0, a=a, attn=attn, x=x, qd=qd, kdt=kdt, egl=jnp.exp(gl_col))

    def run_chains(ms, states):
        n = len(chains)
        ops = [prep(ms[i], d, hh) for i, (hh, d) in enumerate(chains)]
        ps = [jnp.where(same_base, o["a"], 0.0) for o in ops]
        ts = [eye - p for p in ps]
        for _ in range(BASE.bit_length() - 2):
            ps = [_mm(p, p) for p in ps]
            ts = [t + _mm(t, p) for t, p in zip(ts, ps)]
        inner = same_base
        for outer in merge_masks:
            lows = [_mm(jnp.where(outer & ~inner, o["a"], 0.0), t) for o, t in zip(ops, ts)]
            ts = [t - _mm(t, low) for t, low in zip(ts, lows)]
            inner = outer
        xs = [_mm(t, o["x"]) for t, o in zip(ts, ops)]
        for i in range(n):
            vn_s[i] = jnp.zeros((SC, DV), jnp.float32)
        states = list(states)
        for step in range(CPS):
            cs = [step if d == 0 else CPS - 1 - step for _, d in chains]
            los = [c * CHUNK for c in cs]
            ws_qs = [_mm(jnp.concatenate([x[lo:lo + CHUNK, DV:], o["qd"][lo:lo + CHUNK]], axis=0), s)
                     for x, o, lo, s in zip(xs, ops, los, states)]
            for i in range(n):
                vn_s[i, los[i]:los[i] + CHUNK, :] = xs[i][los[i]:los[i] + CHUNK, :DV] - ws_qs[i][:CHUNK]
            vns = [vn_s[i] for i in range(n)]
            o_cs = [wq[CHUNK:] + _mm(o["attn"][lo:lo + CHUNK, :], vn)
                    for wq, o, lo, vn in zip(ws_qs, ops, los, vns)]
            for i, (hh, _) in enumerate(chains):
                o_s[hh, pl.ds(ops[i]["r0"] + los[i], CHUNK), :] += o_cs[i]
            states = [s * o["egl"][lo:lo + 1, :]
                      + _mm(o["kdt"], jnp.where((rowi >= lo) & (rowi < lo + CHUNK), vn, 0.0))
                      for s, o, lo, vn in zip(states, ops, los, vns)]
        return tuple(states)

    if zero_init:
        states = tuple(jnp.zeros((DK, DV), jnp.float32) for _ in range(2 * hpb))
    else:
        states = tuple(s0_ref[0, d, hh] for hh in range(hpb) for d in range(2))

    chains = [(hh, d) for hh in range(hpb) for d in range(2)]

    def body(m, carry):
        return run_chains([m if d == 0 else n_sc - 1 - m for _, d in chains], carry)

    if n_sc == 1:
        states = body(0, states)
    else:
        states = lax.fori_loop(0, n_sc, body, states)

    for hh in range(hpb):
        for d in range(2):
            st_ref[0, d, hh] = states[2 * hh + d]
        o = o_s[hh]
        o = o * lax.rsqrt(jnp.mean(o * o, axis=-1, keepdims=True) + EPS) * og_ref[...]
        zz = z_ref[:, hh * DV:(hh + 1) * DV]
        o_ref[:, hh * DV:(hh + 1) * DV] = o * (zz * jax.nn.sigmoid(zz))


def _delta_call(qkv, z, ba, conv_w, a_log, dt_bias, onorm_g, s0, B, L, row_blk0):
    n_sc = L // SC
    t0 = row_blk0 * L
    bah = ba[t0:t0 + B * L, :4 * H_A].reshape(B, L, 4, H_A).transpose(0, 3, 1, 2)
    bar = bah.reshape(B, H_A, n_sc, SC, 4).transpose(0, 1, 2, 4, 3)
    scal = jnp.concatenate([a_log.reshape(-1), dt_bias.reshape(-1)]).astype(jnp.float32)
    hpb = max(1, min(H_A, DELTA_HEAD_ROWS // L))
    n_hb = H_A // hpb
    zero_init = s0 is None
    if zero_init:
        s0 = jnp.zeros((1, 2, hpb, DK, DV), jnp.float32)
        s0_spec = pl.BlockSpec((1, 2, hpb, DK, DV), lambda b, h, sc: (0, 0, 0, 0, 0))
    else:
        s0_spec = pl.BlockSpec((1, 2, hpb, DK, DV), lambda b, h, sc: (b, 0, h, 0, 0))

    def col(off):
        return pl.BlockSpec((L, hpb * DK), lambda b, h, sc: (row_blk0 + b, off * n_hb + h))

    def cw(off):
        return pl.BlockSpec((CONV_K, hpb * DK), lambda b, h, sc: (0, off * n_hb + h))

    kern = functools.partial(_delta_kernel, n_sc=n_sc, zero_init=zero_init, hpb=hpb)
    return pl.pallas_call(
        kern,
        grid_spec=pltpu.PrefetchScalarGridSpec(
            num_scalar_prefetch=1,
            grid=(B, n_hb),
            in_specs=[col(0), col(1), col(2),
                      pl.BlockSpec((L, hpb * DV), lambda b, h, sc: (row_blk0 + b, h)),
                      pl.BlockSpec((1, hpb, L, 4), lambda b, h, sc: (b, h, 0, 0)),
                      pl.BlockSpec((1, hpb, n_sc, 4, SC), lambda b, h, sc: (b, h, 0, 0, 0)),
                      cw(0), cw(1), cw(2),
                      pl.BlockSpec((1, DV), lambda b, h, sc: (0, 0)),
                      s0_spec],
            out_specs=[pl.BlockSpec((L, hpb * DV), lambda b, h, sc: (b, h)),
                       pl.BlockSpec((1, 2, hpb, DK, DV), lambda b, h, sc: (b, 0, h, 0, 0))],
            scratch_shapes=[pltpu.VMEM((hpb, L, DK), jnp.float32), pltpu.VMEM((hpb, L, DK), jnp.float32),
                            pltpu.VMEM((hpb, L, DV), jnp.float32), pltpu.VMEM((hpb, L, DV), jnp.float32),
                            pltpu.VMEM((2 * hpb, SC, DV), jnp.float32)]),
        out_shape=[jax.ShapeDtypeStruct((B * L, D_A), jnp.float32),
                   jax.ShapeDtypeStruct((B, 2, H_A, DK, DV), jnp.float32)],
        compiler_params=pltpu.CompilerParams(dimension_semantics=("arbitrary", "arbitrary"),
                                             vmem_limit_bytes=VMEM_LIMIT),
    )(scal, qkv, qkv, qkv, z, bah, bar, conv_w, conv_w, conv_w, onorm_g.reshape(1, DV), s0)


TR = 256
GSZ = N_EXPERTS // N_GROUPS
NEG = -jnp.inf
BM = 256
SUBLANES = 8
FILL_PIECES = tuple(p for p in (BM >> s for s in range(1, BM.bit_length())) if p >= SUBLANES)


def _route_kernel(h_ref, rwh_ref, rwl_ref, rb_ref, idx_ref, rank_ref, w_ref, cnt_ref, cnt_s):
    i = pl.program_id(0)

    @pl.when(i == 0)
    def _():
        cnt_s[...] = jnp.zeros_like(cnt_s)

    h = h_ref[...]
    hh, hl = _split_bf16(h)
    logits = _bdot(hh, rwh_ref[...]) + (_bdot(hh, rwl_ref[...]) + _bdot(hl, rwh_ref[...]))
    scores = jax.nn.sigmoid(logits.T)
    sel = scores + rb_ref[...]
    erow = lax.broadcasted_iota(jnp.int32, sel.shape, 0)
    grow = lax.broadcasted_iota(jnp.int32, (GSZ, TR), 0)

    def first_argmax(v, rows):
        m = jnp.max(v, axis=0, keepdims=True)
        first = jnp.min(jnp.where(v == m, rows, N_EXPERTS), axis=0, keepdims=True)
        return m, first

    gs = []
    for g in range(N_GROUPS):
        vg = sel[g * GSZ:(g + 1) * GSZ, :]
        m1, i1 = first_argmax(vg, grow)
        m2 = jnp.max(jnp.where(grow == i1, NEG, vg), axis=0, keepdims=True)
        gs.append(m1 + m2)
    cand = []
    for g in range(N_GROUPS):
        beat = jnp.zeros(gs[g].shape, jnp.int32)
        for o in range(N_GROUPS):
            if o == g:
                continue
            wins = (gs[o] > gs[g]) | ((gs[o] == gs[g]) & (o < g))
            beat = beat + wins.astype(jnp.int32)
        cand.append(jnp.where(beat < TOPK_GROUP, sel[g * GSZ:(g + 1) * GSZ, :], NEG))
    cand = jnp.concatenate(cand, axis=0)
    chosen = []
    picked = jnp.zeros(sel.shape, jnp.bool_)
    for _ in range(TOP_K):
        _, ik = first_argmax(cand, erow)
        hit = erow == ik
        chosen.append((ik, hit))
        picked = picked | hit
        cand = jnp.where(hit, NEG, cand)
    wsum = jnp.sum(jnp.where(picked, scores, 0.0), axis=0, keepdims=True)

    ri = lax.broadcasted_iota(jnp.int32, (TR, TR), 0)
    ci = lax.broadcasted_iota(jnp.int32, (TR, TR), 1)
    earlier = (ri < ci).astype(jnp.bfloat16)
    rank_mat = _bdot(picked.astype(jnp.bfloat16), earlier) + cnt_s[...]
    cnt_s[...] = cnt_s[...] + jnp.sum(picked.astype(jnp.float32), axis=1, keepdims=True)
    cnt_ref[...] = cnt_s[...].astype(jnp.int32)

    for k, (ik, hit) in enumerate(chosen):
        idx_ref[0, k:k + 1, :] = ik
        rank_ref[0, k:k + 1, :] = jnp.sum(jnp.where(hit, rank_mat, 0.0), axis=0, keepdims=True).astype(jnp.int32)
        w_ref[0, k:k + 1, :] = jnp.sum(jnp.where(hit, scores, 0.0), axis=0, keepdims=True) / wsum * ROUTED_SCALE


def _route_call(hf, router_w, router_bias):
    T, D = hf.shape
    n_tiles = T // TR
    rwh, rwl = _split_bf16(router_w)
    row_spec = pl.BlockSpec((1, TOP_K, TR), lambda i: (i, 0, 0))
    return pl.pallas_call(
        _route_kernel,
        grid=(n_tiles,),
        in_specs=[pl.BlockSpec((TR, D), lambda i: (i, 0)),
                  pl.BlockSpec((D, N_EXPERTS), lambda i: (0, 0)),
                  pl.BlockSpec((D, N_EXPERTS), lambda i: (0, 0)),
                  pl.BlockSpec((N_EXPERTS, 1), lambda i: (0, 0))],
        out_specs=[row_spec, row_spec, row_spec, pl.BlockSpec((N_EXPERTS, 1), lambda i: (0, 0))],
        scratch_shapes=[pltpu.VMEM((N_EXPERTS, 1), jnp.float32)],
        out_shape=[jax.ShapeDtypeStruct((n_tiles, TOP_K, TR), jnp.int32),
                   jax.ShapeDtypeStruct((n_tiles, TOP_K, TR), jnp.int32),
                   jax.ShapeDtypeStruct((n_tiles, TOP_K, TR), jnp.float32),
                   jax.ShapeDtypeStruct((N_EXPERTS, 1), jnp.int32)],
        compiler_params=pltpu.CompilerParams(dimension_semantics=("arbitrary",)),
    )(hf, rwh, rwl, router_bias.reshape(N_EXPERTS, 1).astype(jnp.float32))


def _dispatch_kernel(fill_ref, idx_ref, rank_ref, pstart_ref, h_ref, pos_ref, xs_hbm, pos_v, pos_s, zbuf,
                     ssem, psem, zsem):
    n_blk = xs_hbm.shape[0] // BM

    @pl.when(pl.program_id(0) == 0)
    def _():
        zbuf[...] = jnp.zeros_like(zbuf)

        def pad_copies(e, act):
            n = fill_ref[N_EXPERTS + e]
            start = fill_ref[e]
            head = jnp.minimum((-start) & (SUBLANES - 1), n)
            for j in range(SUBLANES - 1):
                @pl.when(j < head)
                def _(j=j):
                    act(pltpu.make_async_copy(zbuf.at[0], xs_hbm.at[start + j], zsem))
            off = start + head
            rest = n - head
            for piece in FILL_PIECES:
                @pl.when((rest & piece) != 0)
                def _(off=off, piece=piece):
                    dst = xs_hbm.at[pl.ds(pl.multiple_of(off, SUBLANES), piece)]
                    act(pltpu.make_async_copy(zbuf.at[pl.ds(0, piece)], dst, zsem))
                off = off + (rest & piece)

        def tail_copy(b, act):
            act(pltpu.make_async_copy(zbuf, xs_hbm.at[pl.ds(pl.multiple_of(b * BM, BM), BM)], zsem))

        for act in (lambda c: c.start(), lambda c: c.wait()):
            def per_expert(e, carry, act=act):
                pad_copies(e, act)
                return carry

            def per_block(b, carry, act=act):
                tail_copy(b, act)
                return carry

            lax.fori_loop(0, N_EXPERTS, per_expert, 0)
            lax.fori_loop(fill_ref[2 * N_EXPERTS], n_blk, per_block, 0)

    erow = lax.broadcasted_iota(jnp.int32, (N_EXPERTS, TR), 0)
    pstart = pstart_ref[...]
    for k in range(TOP_K):
        hit = erow == idx_ref[0, k:k + 1, :]
        seg = jnp.sum(jnp.where(hit, pstart, 0), axis=0, keepdims=True)
        pos_v[k:k + 1, :] = seg + rank_ref[0, k:k + 1, :]
    pos_ref[0] = pos_v[...]
    cp = pltpu.make_async_copy(pos_v, pos_s, psem)
    cp.start()
    cp.wait()

    def body(t, carry):
        for k in range(TOP_K):
            pltpu.make_async_copy(h_ref.at[t], xs_hbm.at[pos_s[k, t]], ssem).start()
        return carry

    lax.fori_loop(0, TR, body, 0, unroll=8)
    n_rows = TR * TOP_K
    pltpu.make_async_copy(xs_hbm.at[pl.ds(0, n_rows)], xs_hbm.at[pl.ds(0, n_rows)], ssem).wait()


def _dispatch_call(hf, idx, rank, pad_start, fill_tab, n_pad):
    T, D = hf.shape
    n_tiles = T // TR
    row_spec = pl.BlockSpec((1, TOP_K, TR), lambda i, ft: (i, 0, 0))
    return pl.pallas_call(
        _dispatch_kernel,
        grid_spec=pltpu.PrefetchScalarGridSpec(
            num_scalar_prefetch=1,
            grid=(n_tiles,),
            in_specs=[row_spec, row_spec,
                      pl.BlockSpec((N_EXPERTS, 1), lambda i, ft: (0, 0)),
                      pl.BlockSpec((TR, D), lambda i, ft: (i, 0))],
            out_specs=[row_spec, pl.BlockSpec(memory_space=pl.ANY)],
            scratch_shapes=[pltpu.VMEM((TOP_K, TR), jnp.int32), pltpu.SMEM((TOP_K, TR), jnp.int32),
                            pltpu.VMEM((BM, D), hf.dtype),
                            pltpu.SemaphoreType.DMA, pltpu.SemaphoreType.DMA, pltpu.SemaphoreType.DMA]),
        out_shape=[jax.ShapeDtypeStruct((n_tiles, TOP_K, TR), jnp.int32),
                   jax.ShapeDtypeStruct((n_pad, D), hf.dtype)],
        compiler_params=pltpu.CompilerParams(dimension_semantics=("arbitrary",)),
    )(fill_tab, idx, rank, pad_start.reshape(N_EXPERTS, 1), hf)


def _expert_kernel(blk_e_ref, nvalid_ref, nused_ref, x_ref, wg_ref, wu_ref, wd_ref, y_ref, wg_s, wu_s, wd_s):
    i = pl.program_id(0)

    @pl.when(i < nused_ref[0])
    def _():
        e = blk_e_ref[i]
        prev = blk_e_ref[jnp.maximum(i - 1, 0)]

        @pl.when((i == 0) | (e != prev))
        def _():
            wg_s[...] = wg_ref[0].astype(jnp.bfloat16)
            wu_s[...] = wu_ref[0].astype(jnp.bfloat16)
            wd_s[...] = wd_ref[0].astype(jnp.bfloat16)

        row = lax.broadcasted_iota(jnp.int32, (BM, 1), 0)
        xa, xb = _unpack_rows(jnp.where(row < nvalid_ref[i], x_ref[...], 0))
        xa = xa.astype(jnp.bfloat16)
        xb = xb.astype(jnp.bfloat16)
        half = xa.shape[1]
        g = _bdot(xa, wg_s[:half, :]) + _bdot(xb, wg_s[half:, :])
        u = _bdot(xa, wu_s[:half, :]) + _bdot(xb, wu_s[half:, :])
        a = (g * jax.nn.sigmoid(g)) * u
        y_ref[...] = _pack_rows(_bdot(a.astype(jnp.bfloat16), wd_s[...]))

    @pl.when(i >= nused_ref[0])
    def _():
        y_ref[...] = jnp.zeros_like(y_ref)


def _expert_call(x_sorted, blk_e, n_valid, n_used, w_gate, w_up, w_down):
    n_pad, DH = x_sorted.shape
    n_blk = n_pad // BM
    E, D, F = w_gate.shape

    def row_map(i, be, nv, nu):
        return (jnp.minimum(i, nu[0] - 1), 0)

    def w_map(i, be, nv, nu):
        return (be[jnp.minimum(i, nu[0] - 1)], 0, 0)

    return pl.pallas_call(
        _expert_kernel,
        grid_spec=pltpu.PrefetchScalarGridSpec(
            num_scalar_prefetch=3,
            grid=(n_blk,),
            in_specs=[pl.BlockSpec((BM, DH), row_map),
                      pl.BlockSpec((1, D, F), w_map),
                      pl.BlockSpec((1, D, F), w_map),
                      pl.BlockSpec((1, F, D), w_map)],
            out_specs=pl.BlockSpec((BM, DH), lambda i, be, nv, nu: (i, 0)),
            scratch_shapes=[pltpu.VMEM((D, F), jnp.bfloat16), pltpu.VMEM((D, F), jnp.bfloat16),
                            pltpu.VMEM((F, D), jnp.bfloat16)]),
        out_shape=jax.ShapeDtypeStruct((n_pad, DH), jnp.int32),
        compiler_params=pltpu.CompilerParams(dimension_semantics=("arbitrary",),
                                             vmem_limit_bytes=VMEM_LIMIT),
    )(blk_e, n_valid, n_used, x_sorted, w_gate, w_up, w_down)


TC = 128


def _combine_kernel(pos_hbm, y_hbm, w_ref, x1_ref, sh_ref, mod_ref, fg_ref, out_ref, ybuf, pos_s, gsem, psem):
    j = pl.program_id(0)
    last = pl.num_programs(0) - 1

    def pos_copy(b, slot):
        return pltpu.make_async_copy(pos_hbm.at[b], pos_s.at[slot], psem.at[slot])

    def start_gather(slot):
        for k in range(TOP_K):
            for t in range(TC):
                pltpu.make_async_copy(y_hbm.at[pos_s[slot, k * TC + t]], ybuf.at[slot, k, t],
                                      gsem.at[slot]).start()

    def wait_gather(slot):
        pltpu.make_async_copy(ybuf.at[slot], ybuf.at[slot], gsem.at[slot]).wait()

    @pl.when(j == 0)
    def _():
        pos_copy(0, 0).start()
        pos_copy(0, 0).wait()
        start_gather(0)
        pos_copy(jnp.minimum(1, last), 1).start()

    slot = j % 2
    nslot = 1 - slot
    pos_copy(0, nslot).wait()
    start_gather(nslot)
    pos_copy(jnp.minimum(j + 2, last), slot).start()
    wait_gather(slot)
    w = w_ref[...]
    acc_a = acc_b = None
    for k in range(TOP_K):
        ya, yb = _unpack_rows(ybuf[slot, k])
        acc_a = w[:, k:k + 1] * ya if k == 0 else acc_a + w[:, k:k + 1] * ya
        acc_b = w[:, k:k + 1] * yb if k == 0 else acc_b + w[:, k:k + 1] * yb
    acc = jnp.concatenate([acc_a, acc_b], axis=1)
    x2 = x1_ref[...] + mod_ref[0, 5:6, :] * (acc + sh_ref[...])
    out_ref[...] = x2 * lax.rsqrt(jnp.mean(x2 * x2, axis=-1, keepdims=True) + EPS) * fg_ref[...]

    @pl.when(j == last)
    def _():
        wait_gather(nslot)
        pos_copy(0, slot).wait()


def _combine_call(y_sorted, pos_t, wts, x1, shared, mod, final_g, n_ctx, lat_len):
    T, K = wts.shape
    DH = y_sorted.shape[1]
    D = 2 * DH
    n_tiles = T // TC
    row = functools.partial(_mod_row, tokens_per_tile=TC, n_ctx=n_ctx, lat_len=lat_len)
    return pl.pallas_call(
        _combine_kernel,
        grid=(n_tiles,),
        in_specs=[pl.BlockSpec(memory_space=pl.ANY),
                  pl.BlockSpec(memory_space=pl.ANY),
                  pl.BlockSpec((TC, K), lambda j: (j, 0)),
                  pl.BlockSpec((TC, D), lambda j: (j, 0)),
                  pl.BlockSpec((TC, D), lambda j: (j, 0)),
                  pl.BlockSpec((1, N_MOD, D), lambda j: (row(j), 0, 0)),
                  pl.BlockSpec((1, D), lambda j: (0, 0))],
        out_specs=pl.BlockSpec((TC, D), lambda j: (j, 0)),
        scratch_shapes=[pltpu.VMEM((2, K, TC, DH), jnp.uint32),
                        pltpu.SMEM((2, K * TC), jnp.int32),
                        pltpu.SemaphoreType.DMA((2,)), pltpu.SemaphoreType.DMA((2,))],
        out_shape=jax.ShapeDtypeStruct((T, D), jnp.float32),
        compiler_params=pltpu.CompilerParams(dimension_semantics=("arbitrary",)),
    )(pos_t, y_sorted, wts, x1, shared, mod, final_g.reshape(1, D))


SC_CORES = 2
SC_SUBCORES = 16
SC_CHUNK = 128


def _sc_gather_call(table, idx):
    n_idx = idx.shape[0]
    width = table.shape[1]
    n_workers = SC_CORES * SC_SUBCORES
    per_worker = n_idx // n_workers
    assert per_worker * n_workers == n_idx and per_worker % SC_CHUNK == 0
    mesh = plsc.VectorSubcoreMesh(core_axis_name="c", subcore_axis_name="s")

    def body(table_hbm, idx_hbm, out_hbm, idx_v, rows_v, sem):
        wid = lax.axis_index("s") * SC_CORES + lax.axis_index("c")
        base = wid * per_worker

        @pl.loop(0, per_worker // SC_CHUNK)
        def _(ch):
            off = base + ch * SC_CHUNK
            pltpu.sync_copy(idx_hbm.at[pl.ds(off, SC_CHUNK)], idx_v)
            pltpu.async_copy(table_hbm.at[idx_v], rows_v, sem).wait()
            pltpu.sync_copy(rows_v, out_hbm.at[pl.ds(off, SC_CHUNK)])

    return pl.kernel(
        body, out_type=jax.ShapeDtypeStruct((n_idx, width), table.dtype), mesh=mesh,
        scratch_types=[pltpu.VMEM((SC_CHUNK,), jnp.int32), pltpu.VMEM((SC_CHUNK, width), table.dtype),
                       pltpu.SemaphoreType.DMA],
    )(table, idx)


def _sc_scatter_call(src, pos, n_out):
    n_idx = pos.shape[0]
    width = src.shape[1]
    n_workers = SC_CORES * SC_SUBCORES
    per_worker = n_idx // n_workers
    assert per_worker * n_workers == n_idx and per_worker % SC_CHUNK == 0 and TR % SC_CHUNK == 0
    mesh = plsc.VectorSubcoreMesh(core_axis_name="c", subcore_axis_name="s")
    tile_pairs = TOP_K * TR

    def body(src_hbm, pos_hbm, out_hbm, idx_v, rows_v, sem):
        wid = lax.axis_index("s") * SC_CORES + lax.axis_index("c")
        base = wid * per_worker

        @pl.loop(0, per_worker // SC_CHUNK)
        def _(ch):
            off = base + ch * SC_CHUNK
            row0 = (off // tile_pairs) * TR + off % TR
            pltpu.sync_copy(pos_hbm.at[pl.ds(off, SC_CHUNK)], idx_v)
            pltpu.sync_copy(src_hbm.at[pl.ds(row0, SC_CHUNK)], rows_v)
            pltpu.async_copy(rows_v, out_hbm.at[idx_v], sem).wait()

    return pl.kernel(
        body, out_type=jax.ShapeDtypeStruct((n_out, width), src.dtype), mesh=mesh,
        scratch_types=[pltpu.VMEM((SC_CHUNK,), jnp.int32), pltpu.VMEM((SC_CHUNK, width), src.dtype),
                       pltpu.SemaphoreType.DMA],
    )(src, pos)


def _positions_kernel(idx_ref, rank_ref, pstart_ref, pos_ref):
    erow = lax.broadcasted_iota(jnp.int32, (N_EXPERTS, TR), 0)
    pstart = pstart_ref[...]
    for k in range(TOP_K):
        hit = erow == idx_ref[0, k:k + 1, :]
        seg = jnp.sum(jnp.where(hit, pstart, 0), axis=0, keepdims=True)
        pos_ref[0, k:k + 1, :] = seg + rank_ref[0, k:k + 1, :]


def _positions_call(idx, rank, pad_start):
    n_tiles = idx.shape[0]
    row_spec = pl.BlockSpec((1, TOP_K, TR), lambda i: (i, 0, 0))
    return pl.pallas_call(
        _positions_kernel,
        grid=(n_tiles,),
        in_specs=[row_spec, row_spec, pl.BlockSpec((N_EXPERTS, 1), lambda i: (0, 0))],
        out_specs=row_spec,
        out_shape=jax.ShapeDtypeStruct((n_tiles, TOP_K, TR), jnp.int32),
    )(idx, rank, pad_start.reshape(N_EXPERTS, 1))


def _combine_dense_kernel(g_ref, w_ref, x1_ref, sh_ref, mod_ref, fg_ref, out_ref):
    w = w_ref[...]
    acc_a = acc_b = None
    for k in range(TOP_K):
        ya, yb = _unpack_rows(g_ref[0, k])
        acc_a = w[:, k:k + 1] * ya if k == 0 else acc_a + w[:, k:k + 1] * ya
        acc_b = w[:, k:k + 1] * yb if k == 0 else acc_b + w[:, k:k + 1] * yb
    acc = jnp.concatenate([acc_a, acc_b], axis=1)
    x2 = x1_ref[...] + mod_ref[0, 5:6, :] * (acc + sh_ref[...])
    out_ref[...] = x2 * lax.rsqrt(jnp.mean(x2 * x2, axis=-1, keepdims=True) + EPS) * fg_ref[...]


def _combine_dense_call(gathered, wts, x1, shared, mod, final_g, n_ctx, lat_len):
    T, K = wts.shape
    DH = gathered.shape[-1]
    D = 2 * DH
    row = functools.partial(_mod_row, tokens_per_tile=TC, n_ctx=n_ctx, lat_len=lat_len)
    return pl.pallas_call(
        _combine_dense_kernel,
        grid=(T // TC,),
        in_specs=[pl.BlockSpec((1, K, TC, DH), lambda j: (j, 0, 0, 0)),
                  pl.BlockSpec((TC, K), lambda j: (j, 0)),
                  pl.BlockSpec((TC, D), lambda j: (j, 0)),
                  pl.BlockSpec((TC, D), lambda j: (j, 0)),
                  pl.BlockSpec((1, N_MOD, D), lambda j: (row(j), 0, 0)),
                  pl.BlockSpec((1, D), lambda j: (0, 0))],
        out_specs=pl.BlockSpec((TC, D), lambda j: (j, 0)),
        out_shape=jax.ShapeDtypeStruct((T, D), jnp.float32),
        compiler_params=pltpu.CompilerParams(dimension_semantics=("arbitrary",)),
    )(gathered, wts, x1, shared, mod, final_g.reshape(1, D))


def _moe_routed(h2, h2p, router_w, router_bias, w_gate, w_up, w_down):
    T, D = h2.shape
    idx, rank, w_rows, cnt = _route_call(h2, router_w, router_bias)
    wts = w_rows.transpose(0, 2, 1).reshape(T, TOP_K)
    counts = cnt[:, 0]
    padded = (counts + BM - 1) // BM * BM
    pad_end = jnp.cumsum(padded)
    pad_start = (pad_end - padded).astype(jnp.int32)
    n_pad = T * TOP_K + N_EXPERTS * BM
    n_blk = n_pad // BM
    n_used = (pad_end[-1] // BM).astype(jnp.int32).reshape(1)
    pos = _positions_call(idx, rank, pad_start)
    x_sorted = _sc_scatter_call(h2p, pos.reshape(-1), n_pad)
    blk_row0 = jnp.arange(n_blk, dtype=jnp.int32) * BM
    blk_e = jnp.minimum(jnp.sum((pad_end[None, :] <= blk_row0[:, None]).astype(jnp.int32), axis=1), N_EXPERTS - 1)
    own = blk_e[:, None] == jnp.arange(N_EXPERTS, dtype=jnp.int32)[None, :]
    seg_end = jnp.sum(jnp.where(own, (pad_start + counts)[None, :], 0), axis=1)
    n_valid = jnp.clip(seg_end - blk_row0, 0, BM).astype(jnp.int32)
    y = _expert_call(x_sorted, blk_e, n_valid, n_used, w_gate, w_up, w_down)
    pos_t = pos.reshape(T // TR, TOP_K, TR // TC, TC).transpose(0, 2, 1, 3).reshape(T // TC, TOP_K * TC)
    return y, pos_t, wts


def kernel(x_prompt, x_sample, state_delta, c, c_ctx, w_ada, b_ada, norm1_g, w_in, conv_w, a_log,
           dt_bias, onorm_g, pool_w, pool_scale, w_out, norm2_g, router_w, router_bias, exp_w_gate,
           exp_w_up, exp_w_down, sh_w_gate, sh_w_up, sh_w_down, final_g):
    Bc, Lc, D = x_prompt.shape
    Bl, Ll, _ = x_sample.shape
    n_ctx = Bc * Lc
    assert DEPTH == 1 and 1 + Bl <= MOD_ROWS and n_ctx % Ll == 0
    x_parts = (x_prompt.reshape(n_ctx, D), x_sample.reshape(Bl * Ll, D))
    cvec = jnp.concatenate([c_ctx[None], c, jnp.zeros((MOD_ROWS - 1 - Bl, D), c.dtype)], axis=0)
    l = 0
    mod = _ada_call(cvec, w_ada[l], b_ada[l]).reshape(MOD_ROWS, N_MOD, D)
    qkv, z, ba, u = _inproj_call(*x_parts, mod, norm1_g[l], w_in[l], Ll)
    dn = (conv_w[l], a_log[l], dt_bias[l], onorm_g[l])
    oa_c, st_ctx = _delta_call(qkv, z, ba, *dn, None, Bc, Lc, 0)
    oa_l, _ = _delta_call(qkv, z, ba, *dn, state_delta[:, l], Bl, Ll, n_ctx // Ll)
    op_c = _pool_call(u, pool_w[l], pool_scale[l], False, Bc, Lc, 0)
    op_l = _pool_call(u, pool_w[l], pool_scale[l], True, Bl, Ll, n_ctx // Ll)
    x1, h2, h2p, shared = _outproj_call(x_parts, (oa_c, oa_l), (op_c, op_l), mod, norm2_g[l], w_out[l],
                                        sh_w_gate[l], sh_w_up[l], sh_w_down[l], Ll)
    y, pos_t, wts = _moe_routed(h2, h2p, router_w[l], router_bias[l], exp_w_gate[l], exp_w_up[l],
                                exp_w_down[l])
    T = n_ctx + Bl * Ll
    gathered = _sc_gather_call(y, pos_t.reshape(-1))
    out = _combine_dense_call(gathered.reshape(T // TC, TOP_K, TC, D // 2), wts, x1, shared, mod, final_g,
                              n_ctx, Ll)
    y_prompt = out[:n_ctx].reshape(Bc, Lc, D)
    y_sample = out[n_ctx:].reshape(Bl, Ll, D)
    new_state_delta = st_ctx[:, None].astype(x_prompt.dtype)
    return (y_prompt, y_sample, new_state_delta)
```

```python
import functools
import jax, jax.numpy as jnp
from jax import lax
from jax.experimental import pallas as pl
from jax.experimental.pallas import tpu as pltpu
from jax.experimental.pallas import tpu_sc as plsc

D_MODEL = 1024
DEPTH = 1
GRID_W = 64
D_MIX = D_MODEL
D_A = D_MIX // 2
D_P = D_MIX - D_A
H_A = 4
DK = D_A // H_A
DV = D_A // H_A
CONV_K = 5
CHUNK = 64
POOL_WINDOWS = (2, 4, 8, 16)
N_PG = len(POOL_WINDOWS)
PG = D_P // N_PG
N_EXPERTS = 256
TOP_K = 8
N_GROUPS = 8
TOPK_GROUP = 4
ROUTED_SCALE = 2.5
EPS = 1e-6
VMEM_LIMIT = 48 * 1024 * 1024


def _split_bf16(a):
    hi = a.astype(jnp.bfloat16)
    return hi, (a - hi.astype(jnp.float32)).astype(jnp.bfloat16)


def _bdot(a, b):
    return jnp.dot(a, b, preferred_element_type=jnp.float32)


def _pack_rows(x):
    m = x.shape[1] // 2
    hi = lax.bitcast_convert_type(x[:, :m].astype(jnp.bfloat16).astype(jnp.float32), jnp.uint32)
    lo = lax.bitcast_convert_type(x[:, m:].astype(jnp.bfloat16).astype(jnp.float32), jnp.uint32)
    return lax.bitcast_convert_type(hi | (lo >> 16), jnp.int32)


def _unpack_rows(p):
    p = lax.bitcast_convert_type(p, jnp.uint32)
    hi = lax.bitcast_convert_type(p & jnp.uint32(0xFFFF0000), jnp.float32)
    lo = lax.bitcast_convert_type(p << 16, jnp.float32)
    return hi, lo


N_MOD = 6
MOD_ROWS = 8
TM = 512


def _ada_kernel(c_ref, w_ref, b_ref, o_ref):
    c = c_ref[...]
    s = c * jax.nn.sigmoid(c)
    sh, sl = _split_bf16(s)
    wh, wl = _split_bf16(w_ref[...])
    o_ref[...] = _bdot(sh, wh) + (_bdot(sh, wl) + _bdot(sl, wh)) + b_ref[...]


def _ada_call(cvec, w_ada, b_ada):
    R, D = cvec.shape
    N = w_ada.shape[1]
    tn = 1024
    return pl.pallas_call(
        _ada_kernel,
        grid=(N // tn,),
        in_specs=[pl.BlockSpec((R, D), lambda j: (0, 0)),
                  pl.BlockSpec((D, tn), lambda j: (0, j)),
                  pl.BlockSpec((1, tn), lambda j: (0, j))],
        out_specs=pl.BlockSpec((R, tn), lambda j: (0, j)),
        out_shape=jax.ShapeDtypeStruct((R, N), jnp.float32),
    )(cvec, w_ada, b_ada.reshape(1, N))


def _mod_row(tile, tokens_per_tile, n_ctx, lat_len):
    t0 = tile * tokens_per_tile
    return jnp.where(t0 < n_ctx, 0, 1 + (t0 - n_ctx) // lat_len)


def _two_part_specs(n_ctx_tiles, width):
    return (pl.BlockSpec((TM, width), lambda i: (jnp.minimum(i, n_ctx_tiles - 1), 0)),
            pl.BlockSpec((TM, width), lambda i: (jnp.maximum(i - n_ctx_tiles, 0), 0)))


def _pick(n_ctx_tiles, ctx_ref, lat_ref):
    return jnp.where(pl.program_id(0) < n_ctx_tiles, ctx_ref[...], lat_ref[...])


def _inproj_kernel(xc_ref, xl_ref, mod_ref, g_ref, wq_ref, wz_ref, wb_ref, wu_ref, q_ref, z_ref, b_ref, u_ref,
                   *, n_ctx_tiles):
    x = _pick(n_ctx_tiles, xc_ref, xl_ref)
    y = x * lax.rsqrt(jnp.mean(x * x, axis=-1, keepdims=True) + EPS) * g_ref[...]
    h = (y * (1.0 + mod_ref[0, 1:2, :]) + mod_ref[0, 0:1, :]).astype(jnp.bfloat16)
    q_ref[...] = _bdot(h, wq_ref[...])
    z_ref[...] = _bdot(h, wz_ref[...])
    b_ref[...] = _bdot(h, wb_ref[...])
    u_ref[...] = _bdot(h, wu_ref[...])


def _inproj_call(x_ctx, x_lat, mod, norm1_g, w_in, lat_len):
    n_ctx, D = x_ctx.shape
    T = n_ctx + x_lat.shape[0]
    bf = jnp.bfloat16
    nq, nz, nb = 3 * D_A, D_A, 4 * H_A
    wq = w_in[:, :nq].astype(bf)
    wz = w_in[:, nq:nq + nz].astype(bf)
    wb = jnp.pad(w_in[:, nq + nz:nq + nz + nb], ((0, 0), (0, 128 - nb))).astype(bf)
    wu = w_in[:, nq + nz + nb:].astype(bf)
    row = functools.partial(_mod_row, tokens_per_tile=TM, n_ctx=n_ctx, lat_len=lat_len)

    def full(a):
        return pl.BlockSpec(a.shape, lambda i: (0, 0))

    def rows(n):
        return pl.BlockSpec((TM, n), lambda i: (i, 0))

    return pl.pallas_call(
        functools.partial(_inproj_kernel, n_ctx_tiles=n_ctx // TM),
        grid=(T // TM,),
        in_specs=[*_two_part_specs(n_ctx // TM, D), pl.BlockSpec((1, N_MOD, D), lambda i: (row(i), 0, 0)),
                  pl.BlockSpec((1, D), lambda i: (0, 0)), full(wq), full(wz), full(wb), full(wu)],
        out_specs=[rows(nq), rows(nz), rows(128), rows(D_P)],
        out_shape=[jax.ShapeDtypeStruct((T, nq), jnp.float32), jax.ShapeDtypeStruct((T, nz), jnp.float32),
                   jax.ShapeDtypeStruct((T, 128), jnp.float32), jax.ShapeDtypeStruct((T, D_P), jnp.float32)],
        compiler_params=pltpu.CompilerParams(dimension_semantics=("arbitrary",),
                                             vmem_limit_bytes=VMEM_LIMIT),
    )(x_ctx, x_lat, mod, norm1_g.reshape(1, D), wq, wz, wb, wu)


PT = 256


def _window_bounds(pos, w, n):
    return jnp.maximum(pos - w // 2, 0), jnp.minimum(pos + w - w // 2, n)


def _band_sum(band, x):
    xh, xl = _split_bf16(x)
    return _bdot(band, xh) + _bdot(band, xl)


def _pool_seq_kernel(u_ref, pw_ref, ps_ref, o_ref):
    L = u_ref.shape[0]
    ti = lax.broadcasted_iota(jnp.int32, (L, L), 0)
    ji = lax.broadcasted_iota(jnp.int32, (L, L), 1)
    tcol = lax.broadcasted_iota(jnp.int32, (L, 1), 0)
    for i, w in enumerate(POOL_WINDOWS):
        lo, hi = _window_bounds(ti, w, L)
        band = ((ji >= lo) & (ji < hi)).astype(jnp.bfloat16)
        clo, chi = _window_bounds(tcol, w, L)
        ug = u_ref[:, i * PG:(i + 1) * PG]
        mean = _band_sum(band, ug) / (chi - clo).astype(jnp.float32)
        d = (mean - ug).astype(jnp.bfloat16)
        o_ref[:, i * PG:(i + 1) * PG] = _bdot(d, pw_ref[i]) * ps_ref[:, i * PG:(i + 1) * PG]


def _pool_grid_kernel(u_ref, pw_ref, ps_ref, o_ref, pad_s, r_s):
    L = u_ref.shape[0]
    rows = L // GRID_W
    halo = (max(POOL_WINDOWS) // 2) * GRID_W
    pad_s[0:halo, :] = jnp.zeros((halo, D_P), jnp.float32)
    pad_s[halo + L:, :] = jnp.zeros((halo, D_P), jnp.float32)
    pad_s[halo:halo + L, :] = u_ref[...]
    ti = lax.broadcasted_iota(jnp.int32, (PT, PT), 0)
    ji = lax.broadcasted_iota(jnp.int32, (PT, PT), 1)
    tcol = lax.broadcasted_iota(jnp.int32, (PT, 1), 0)
    for i, w in enumerate(POOL_WINDOWS):
        cs = slice(i * PG, (i + 1) * PG)
        acc = None
        for dr in range(-(w // 2), w - w // 2):
            part = pad_s[halo + dr * GRID_W:halo + dr * GRID_W + L, cs]
            acc = part if acc is None else acc + part
        r_s[...] = acc
        lo, hi = _window_bounds(ti % GRID_W, w, GRID_W)
        band = ((ji // GRID_W == ti // GRID_W) & (ji % GRID_W >= lo) & (ji % GRID_W < hi)).astype(jnp.bfloat16)
        clo, chi = _window_bounds(tcol % GRID_W, w, GRID_W)
        ccnt = (chi - clo).astype(jnp.float32)
        for tile in range(L // PT):
            ts = slice(tile * PT, (tile + 1) * PT)
            rlo, rhi = _window_bounds(tile * (PT // GRID_W) + tcol // GRID_W, w, rows)
            mean = _band_sum(band, r_s[ts, :]) / ((rhi - rlo).astype(jnp.float32) * ccnt)
            d = (mean - u_ref[ts, cs]).astype(jnp.bfloat16)
            o_ref[ts, cs] = _bdot(d, pw_ref[i]) * ps_ref[:, cs]


def _pool_call(u, pool_w, pool_scale, grid, B, L, row_blk0):
    pw = pool_w.astype(jnp.bfloat16)
    ps = pool_scale.reshape(1, D_P)
    specs = dict(
        grid=(B,),
        in_specs=[pl.BlockSpec((L, D_P), lambda b: (row_blk0 + b, 0)),
                  pl.BlockSpec((N_PG, PG, PG), lambda b: (0, 0, 0)),
                  pl.BlockSpec((1, D_P), lambda b: (0, 0))],
        out_specs=pl.BlockSpec((L, D_P), lambda b: (b, 0)),
        out_shape=jax.ShapeDtypeStruct((B * L, D_P), jnp.float32),
        compiler_params=pltpu.CompilerParams(dimension_semantics=("arbitrary",),
                                             vmem_limit_bytes=VMEM_LIMIT))
    if not grid:
        return pl.pallas_call(_pool_seq_kernel, **specs)(u, pw, ps)
    halo = (max(POOL_WINDOWS) // 2) * GRID_W
    return pl.pallas_call(
        _pool_grid_kernel,
        scratch_shapes=[pltpu.VMEM((L + 2 * halo, D_P), jnp.float32), pltpu.VMEM((L, PG), jnp.float32)],
        **specs)(u, pw, ps)


def _outproj_kernel(xc_ref, xl_ref, oac_ref, oal_ref, opc_ref, opl_ref, mod_ref, g2_ref, wo_ref, sg_ref, su_ref,
                    sd_ref, x1_ref, h2_ref, h2p_ref, sh_ref, *, n_ctx_tiles):
    o_a = _pick(n_ctx_tiles, oac_ref, oal_ref)
    o_p = _pick(n_ctx_tiles, opc_ref, opl_ref)
    mix = (_bdot(o_a.astype(jnp.bfloat16), wo_ref[:D_A, :])
           + _bdot(o_p.astype(jnp.bfloat16), wo_ref[D_A:, :]))
    x1 = _pick(n_ctx_tiles, xc_ref, xl_ref) + mod_ref[0, 2:3, :] * mix
    x1_ref[...] = x1
    y = x1 * lax.rsqrt(jnp.mean(x1 * x1, axis=-1, keepdims=True) + EPS) * g2_ref[...]
    h2 = y * (1.0 + mod_ref[0, 4:5, :]) + mod_ref[0, 3:4, :]
    h2_ref[...] = h2
    h2p_ref[...] = _pack_rows(h2)
    hb = h2.astype(jnp.bfloat16)
    g = _bdot(hb, sg_ref[...])
    a = (g * jax.nn.sigmoid(g)) * _bdot(hb, su_ref[...])
    sh_ref[...] = _bdot(a.astype(jnp.bfloat16), sd_ref[...])


def _outproj_call(x_parts, oa_parts, op_parts, mod, norm2_g, w_out, sh_gate, sh_up, sh_down, lat_len):
    n_ctx, D = x_parts[0].shape
    T = n_ctx + x_parts[1].shape[0]
    nct = n_ctx // TM
    bf = jnp.bfloat16
    row = functools.partial(_mod_row, tokens_per_tile=TM, n_ctx=n_ctx, lat_len=lat_len)
    ws = [w_out.astype(bf), sh_gate.astype(bf), sh_up.astype(bf), sh_down.astype(bf)]

    def rows(n):
        return pl.BlockSpec((TM, n), lambda i: (i, 0))

    return pl.pallas_call(
        functools.partial(_outproj_kernel, n_ctx_tiles=nct),
        grid=(T // TM,),
        in_specs=[*_two_part_specs(nct, D), *_two_part_specs(nct, D_A), *_two_part_specs(nct, D_P),
                  pl.BlockSpec((1, N_MOD, D), lambda i: (row(i), 0, 0)),
                  pl.BlockSpec((1, D), lambda i: (0, 0))] + [pl.BlockSpec(w.shape, lambda i: (0, 0)) for w in ws],
        out_specs=[rows(D), rows(D), rows(D // 2), rows(D)],
        out_shape=[jax.ShapeDtypeStruct((T, D), jnp.float32), jax.ShapeDtypeStruct((T, D), jnp.float32),
                   jax.ShapeDtypeStruct((T, D // 2), jnp.int32), jax.ShapeDtypeStruct((T, D), jnp.float32)],
        compiler_params=pltpu.CompilerParams(dimension_semantics=("arbitrary",),
                                             vmem_limit_bytes=VMEM_LIMIT),
    )(*x_parts, *oa_parts, *op_parts, mod, norm2_g.reshape(1, D), *ws)


SC = 256
CPS = SC // CHUNK
BASE = 16
DELTA_HEAD_ROWS = 4096


def _mm(a, b):
    return jnp.dot(a.astype(jnp.bfloat16), b.astype(jnp.bfloat16), preferred_element_type=jnp.float32)


def _mm_nt(a, b):
    return lax.dot_general(a.astype(jnp.bfloat16), b.astype(jnp.bfloat16), (((1,), (1,)), ((), ())),
                           preferred_element_type=jnp.float32)


def _softplus(x):
    return jnp.maximum(x, 0.0) + jnp.log(1.0 + jnp.exp(-jnp.abs(x)))


def _delta_kernel(sc_ref, xq_ref, xk_ref, xv_ref, z_ref, bac_ref, bar_ref, cwq_ref, cwk_ref, cwv_ref,
                  og_ref, s0_ref, o_ref, st_ref, q_s, k_s, v_s, o_s, vn_s, *, n_sc, zero_init, hpb):
    hb = pl.program_id(1)
    L = q_s.shape[1]

    def conv(x_ref, w_ref, cs):
        x = x_ref[:, cs]
        row = lax.broadcasted_iota(jnp.int32, x.shape, 0)
        acc = x * w_ref[CONV_K // 2:CONV_K // 2 + 1, cs]
        for j in range(CONV_K):
            d = j - CONV_K // 2
            if d == 0:
                continue
            xs = pltpu.roll(x, (-d) % L, 0)
            ok = (row + d >= 0) & (row + d < L)
            acc = acc + jnp.where(ok, xs, 0.0) * w_ref[j:j + 1, cs]
        return acc * jax.nn.sigmoid(acc)

    for hh in range(hpb):
        cs = slice(hh * DK, (hh + 1) * DK)
        q = conv(xq_ref, cwq_ref, cs)
        q_s[hh] = q * lax.rsqrt(jnp.sum(q * q, axis=-1, keepdims=True) + EPS) * (DK ** -0.5)
        k = conv(xk_ref, cwk_ref, cs)
        k_s[hh] = k * lax.rsqrt(jnp.sum(k * k, axis=-1, keepdims=True) + EPS)
        v_s[hh] = conv(xv_ref, cwv_ref, cs)
    o_s[...] = jnp.zeros_like(o_s)

    ri = lax.broadcasted_iota(jnp.int32, (SC, SC), 0)
    ci = lax.broadcasted_iota(jnp.int32, (SC, SC), 1)
    same = (ri // CHUNK) == (ci // CHUNK)
    same_base = (ri // BASE) == (ci // BASE)
    merge_masks = [(ri // w) == (ci // w) for w in (2 * BASE, CHUNK)]
    eye = (ri == ci).astype(jnp.float32)
    rowi = lax.broadcasted_iota(jnp.int32, (SC, DV), 0)

    def prep(m, d, hh):
        r0 = pl.multiple_of(m * SC, SC)
        h = hb * hpb + hh
        q = q_s[hh, pl.ds(r0, SC), :]
        k = k_s[hh, pl.ds(r0, SC), :]
        v = v_s[hh, pl.ds(r0, SC), :]
        bc = bac_ref[0, hh, pl.ds(r0, SC), :]
        br = bar_ref[0, hh, m]
        a_l = sc_ref[d * H_A + h]
        dtb = sc_ref[2 * H_A + d * H_A + h]
        neg_ea = -jnp.exp(jnp.full((1, 1), a_l, jnp.float32))
        beta = jax.nn.sigmoid(bc[:, d:d + 1])
        g_col = neg_ea * _softplus(bc[:, 2 + d:3 + d] + dtb)
        g_row = neg_ea * _softplus(br[2 + d:3 + d, :] + dtb)
        if d == 0:
            tri, strict = same & (ci <= ri), same & (ci < ri)
        else:
            tri, strict = same & (ci >= ri), same & (ci > ri)
        tri_t = same & (ri <= ci) if d == 0 else same & (ri >= ci)
        gc_col = jnp.sum(jnp.where(tri, g_row, 0.0), axis=1, keepdims=True)
        gc_row = jnp.sum(jnp.where(tri_t, g_col, 0.0), axis=0, keepdims=True)
        gl_col = jnp.sum(jnp.where(same, g_row, 0.0), axis=1, keepdims=True)
        decay = jnp.where(tri, jnp.exp(jnp.where(tri, gc_col - gc_row, 0.0)), 0.0)
        kb = k * beta
        a = jnp.where(strict, _mm_nt(kb, k) * decay, 0.0)
        attn = jnp.where(tri, _mm_nt(q, k) * decay, 0.0)
        eg = jnp.exp(gc_col)
        x = jnp.concatenate([v * beta, kb * eg], axis=1)
        qd = q * eg
        kdt = (k * jnp.exp(gl_col - gc_col)).T
        return dict(r0=r0, a=a, attn=attn, x=x, qd=qd, kdt=kdt, egl=jnp.exp(gl_col))

    def run_chains(ms, states):
        n = len(chains)
        ops = [prep(ms[i], d, hh) for i, (hh, d) in enumerate(chains)]
        ps = [jnp.where(same_base, o["a"], 0.0) for o in ops]
        ts = [eye - p for p in ps]
        for _ in range(BASE.bit_length() - 2):
            ps = [_mm(p, p) for p in ps]
            ts = [t + _mm(t, p) for t, p in zip(ts, ps)]
        inner = same_base
        for outer in merge_masks:
            lows = [_mm(jnp.where(outer & ~inner, o["a"], 0.0), t) for o, t in zip(ops, ts)]
            ts = [t - _mm(t, low) for t, low in zip(ts, lows)]
            inner = outer
        xs = [_mm(t, o["x"]) for t, o in zip(ts, ops)]
        for i in range(n):
            vn_s[i] = jnp.zeros((SC, DV), jnp.float32)
        states = list(states)
        for step in range(CPS):
            cs = [step if d == 0 else CPS - 1 - step for _, d in chains]
            los = [c * CHUNK for c in cs]
            ws_qs = [_mm(jnp.concatenate([x[lo:lo + CHUNK, DV:], o["qd"][lo:lo + CHUNK]], axis=0), s)
                     for x, o, lo, s in zip(xs, ops, los, states)]
            for i in range(n):
                vn_s[i, los[i]:los[i] + CHUNK, :] = xs[i][los[i]:los[i] + CHUNK, :DV] - ws_qs[i][:CHUNK]
            vns = [vn_s[i] for i in range(n)]
            o_cs = [wq[CHUNK:] + _mm(o["attn"][lo:lo + CHUNK, :], vn)
                    for wq, o, lo, vn in zip(ws_qs, ops, los, vns)]
            for i, (hh, _) in enumerate(chains):
                o_s[hh, pl.ds(ops[i]["r0"] + los[i], CHUNK), :] += o_cs[i]
            states = [s * o["egl"][lo:lo + 1, :]
                      + _mm(o["kdt"], jnp.where((rowi >= lo) & (rowi < lo + CHUNK), vn, 0.0))
                      for s, o, lo, vn in zip(states, ops, los, vns)]
        return tuple(states)

    if zero_init:
        states = tuple(jnp.zeros((DK, DV), jnp.float32) for _ in range(2 * hpb))
    else:
        states = tuple(s0_ref[0, d, hh] for hh in range(hpb) for d in range(2))

    chains = [(hh, d) for hh in range(hpb) for d in range(2)]

    def body(m, carry):
        return run_chains([m if d == 0 else n_sc - 1 - m for _, d in chains], carry)

    if n_sc == 1:
        states = body(0, states)
    else:
        states = lax.fori_loop(0, n_sc, body, states)

    for hh in range(hpb):
        for d in range(2):
            st_ref[0, d, hh] = states[2 * hh + d]
        o = o_s[hh]
        o = o * lax.rsqrt(jnp.mean(o * o, axis=-1, keepdims=True) + EPS) * og_ref[...]
        zz = z_ref[:, hh * DV:(hh + 1) * DV]
        o_ref[:, hh * DV:(hh + 1) * DV] = o * (zz * jax.nn.sigmoid(zz))


def _delta_call(qkv, z, ba, conv_w, a_log, dt_bias, onorm_g, s0, B, L, row_blk0):
    n_sc = L // SC
    t0 = row_blk0 * L
    bah = ba[t0:t0 + B * L, :4 * H_A].reshape(B, L, 4, H_A).transpose(0, 3, 1, 2)
    bar = bah.reshape(B, H_A, n_sc, SC, 4).transpose(0, 1, 2, 4, 3)
    scal = jnp.concatenate([a_log.reshape(-1), dt_bias.reshape(-1)]).astype(jnp.float32)
    hpb = max(1, min(H_A, DELTA_HEAD_ROWS // L))
    n_hb = H_A // hpb
    zero_init = s0 is None
    if zero_init:
        s0 = jnp.zeros((1, 2, hpb, DK, DV), jnp.float32)
        s0_spec = pl.BlockSpec((1, 2, hpb, DK, DV), lambda b, h, sc: (0, 0, 0, 0, 0))
    else:
        s0_spec = pl.BlockSpec((1, 2, hpb, DK, DV), lambda b, h, sc: (b, 0, h, 0, 0))

    def col(off):
        return pl.BlockSpec((L, hpb * DK), lambda b, h, sc: (row_blk0 + b, off * n_hb + h))

    def cw(off):
        return pl.BlockSpec((CONV_K, hpb * DK), lambda b, h, sc: (0, off * n_hb + h))

    kern = functools.partial(_delta_kernel, n_sc=n_sc, zero_init=zero_init, hpb=hpb)
    return pl.pallas_call(
        kern,
        grid_spec=pltpu.PrefetchScalarGridSpec(
            num_scalar_prefetch=1,
            grid=(B, n_hb),
            in_specs=[col(0), col(1), col(2),
                      pl.BlockSpec((L, hpb * DV), lambda b, h, sc: (row_blk0 + b, h)),
                      pl.BlockSpec((1, hpb, L, 4), lambda b, h, sc: (b, h, 0, 0)),
                      pl.BlockSpec((1, hpb, n_sc, 4, SC), lambda b, h, sc: (b, h, 0, 0, 0)),
                      cw(0), cw(1), cw(2),
                      pl.BlockSpec((1, DV), lambda b, h, sc: (0, 0)),
                      s0_spec],
            out_specs=[pl.BlockSpec((L, hpb * DV), lambda b, h, sc: (b, h)),
                       pl.BlockSpec((1, 2, hpb, DK, DV), lambda b, h, sc: (b, 0, h, 0, 0))],
            scratch_shapes=[pltpu.VMEM((hpb, L, DK), jnp.float32), pltpu.VMEM((hpb, L, DK), jnp.float32),
                            pltpu.VMEM((hpb, L, DV), jnp.float32), pltpu.VMEM((hpb, L, DV), jnp.float32),
                            pltpu.VMEM((2 * hpb, SC, DV), jnp.float32)]),
        out_shape=[jax.ShapeDtypeStruct((B * L, D_A), jnp.float32),
                   jax.ShapeDtypeStruct((B, 2, H_A, DK, DV), jnp.float32)],
        compiler_params=pltpu.CompilerParams(dimension_semantics=("arbitrary", "arbitrary"),
                                             vmem_limit_bytes=VMEM_LIMIT),
    )(scal, qkv, qkv, qkv, z, bah, bar, conv_w, conv_w, conv_w, onorm_g.reshape(1, DV), s0)


TR = 256
GSZ = N_EXPERTS // N_GROUPS
NEG = -jnp.inf
BM = 512
SUB = 128
SUBLANES = 8
FILL_PIECES = tuple(p for p in (BM >> s for s in range(1, BM.bit_length())) if p >= SUBLANES)


def _route_kernel(h_ref, rwh_ref, rwl_ref, rb_ref, idx_ref, rank_ref, w_ref, cnt_ref, cnt_s):
    i = pl.program_id(0)

    @pl.when(i == 0)
    def _():
        cnt_s[...] = jnp.zeros_like(cnt_s)

    h = h_ref[...]
    hh, hl = _split_bf16(h)
    logits = _bdot(hh, rwh_ref[...]) + (_bdot(hh, rwl_ref[...]) + _bdot(hl, rwh_ref[...]))
    scores = jax.nn.sigmoid(logits.T)
    sel = scores + rb_ref[...]
    erow = lax.broadcasted_iota(jnp.int32, sel.shape, 0)
    grow = lax.broadcasted_iota(jnp.int32, (GSZ, TR), 0)

    def first_argmax(v, rows):
        m = jnp.max(v, axis=0, keepdims=True)
        first = jnp.min(jnp.where(v == m, rows, N_EXPERTS), axis=0, keepdims=True)
        return m, first

    gs = []
    for g in range(N_GROUPS):
        vg = sel[g * GSZ:(g + 1) * GSZ, :]
        m1, i1 = first_argmax(vg, grow)
        m2 = jnp.max(jnp.where(grow == i1, NEG, vg), axis=0, keepdims=True)
        gs.append(m1 + m2)
    cand = []
    for g in range(N_GROUPS):
        beat = jnp.zeros(gs[g].shape, jnp.int32)
        for o in range(N_GROUPS):
            if o == g:
                continue
            wins = (gs[o] > gs[g]) | ((gs[o] == gs[g]) & (o < g))
            beat = beat + wins.astype(jnp.int32)
        cand.append(jnp.where(beat < TOPK_GROUP, sel[g * GSZ:(g + 1) * GSZ, :], NEG))
    cand = jnp.concatenate(cand, axis=0)
    chosen = []
    picked = jnp.zeros(sel.shape, jnp.bool_)
    for _ in range(TOP_K):
        _, ik = first_argmax(cand, erow)
        hit = erow == ik
        chosen.append((ik, hit))
        picked = picked | hit
        cand = jnp.where(hit, NEG, cand)
    wsum = jnp.sum(jnp.where(picked, scores, 0.0), axis=0, keepdims=True)

    ri = lax.broadcasted_iota(jnp.int32, (TR, TR), 0)
    ci = lax.broadcasted_iota(jnp.int32, (TR, TR), 1)
    earlier = (ri < ci).astype(jnp.bfloat16)
    rank_mat = _bdot(picked.astype(jnp.bfloat16), earlier) + cnt_s[...]
    cnt_s[...] = cnt_s[...] + jnp.sum(picked.astype(jnp.float32), axis=1, keepdims=True)
    cnt_ref[...] = cnt_s[...].astype(jnp.int32)

    for k, (ik, hit) in enumerate(chosen):
        idx_ref[0, k:k + 1, :] = ik
        rank_ref[0, k:k + 1, :] = jnp.sum(jnp.where(hit, rank_mat, 0.0), axis=0, keepdims=True).astype(jnp.int32)
        w_ref[0, k:k + 1, :] = jnp.sum(jnp.where(hit, scores, 0.0), axis=0, keepdims=True) / wsum * ROUTED_SCALE


def _route_call(hf, router_w, router_bias):
    T, D = hf.shape
    n_tiles = T // TR
    rwh, rwl = _split_bf16(router_w)
    row_spec = pl.BlockSpec((1, TOP_K, TR), lambda i: (i, 0, 0))
    return pl.pallas_call(
        _route_kernel,
        grid=(n_tiles,),
        in_specs=[pl.BlockSpec((TR, D), lambda i: (i, 0)),
                  pl.BlockSpec((D, N_EXPERTS), lambda i: (0, 0)),
                  pl.BlockSpec((D, N_EXPERTS), lambda i: (0, 0)),
                  pl.BlockSpec((N_EXPERTS, 1), lambda i: (0, 0))],
        out_specs=[row_spec, row_spec, row_spec, pl.BlockSpec((N_EXPERTS, 1), lambda i: (0, 0))],
        scratch_shapes=[pltpu.VMEM((N_EXPERTS, 1), jnp.float32)],
        out_shape=[jax.ShapeDtypeStruct((n_tiles, TOP_K, TR), jnp.int32),
                   jax.ShapeDtypeStruct((n_tiles, TOP_K, TR), jnp.int32),
                   jax.ShapeDtypeStruct((n_tiles, TOP_K, TR), jnp.float32),
                   jax.ShapeDtypeStruct((N_EXPERTS, 1), jnp.int32)],
        compiler_params=pltpu.CompilerParams(dimension_semantics=("arbitrary",)),
    )(hf, rwh, rwl, router_bias.reshape(N_EXPERTS, 1).astype(jnp.float32))


def _dispatch_kernel(fill_ref, idx_ref, rank_ref, pstart_ref, h_ref, pos_ref, xs_hbm, pos_v, pos_s, zbuf,
                     ssem, psem, zsem):
    n_blk = xs_hbm.shape[0] // BM

    @pl.when(pl.program_id(0) == 0)
    def _():
        zbuf[...] = jnp.zeros_like(zbuf)

        def pad_copies(e, act):
            n = fill_ref[N_EXPERTS + e]
            start = fill_ref[e]
            head = jnp.minimum((-start) & (SUBLANES - 1), n)
            for j in range(SUBLANES - 1):
                @pl.when(j < head)
                def _(j=j):
                    act(pltpu.make_async_copy(zbuf.at[0], xs_hbm.at[start + j], zsem))
            off = start + head
            rest = n - head
            for piece in FILL_PIECES:
                @pl.when((rest & piece) != 0)
                def _(off=off, piece=piece):
                    dst = xs_hbm.at[pl.ds(pl.multiple_of(off, SUBLANES), piece)]
                    act(pltpu.make_async_copy(zbuf.at[pl.ds(0, piece)], dst, zsem))
                off = off + (rest & piece)

        def tail_copy(b, act):
            act(pltpu.make_async_copy(zbuf, xs_hbm.at[pl.ds(pl.multiple_of(b * BM, BM), BM)], zsem))

        for act in (lambda c: c.start(), lambda c: c.wait()):
            def per_expert(e, carry, act=act):
                pad_copies(e, act)
                return carry

            def per_block(b, carry, act=act):
                tail_copy(b, act)
                return carry

            lax.fori_loop(0, N_EXPERTS, per_expert, 0)
            lax.fori_loop(fill_ref[2 * N_EXPERTS], n_blk, per_block, 0)

    erow = lax.broadcasted_iota(jnp.int32, (N_EXPERTS, TR), 0)
    pstart = pstart_ref[...]
    for k in range(TOP_K):
        hit = erow == idx_ref[0, k:k + 1, :]
        seg = jnp.sum(jnp.where(hit, pstart, 0), axis=0, keepdims=True)
        pos_v[k:k + 1, :] = seg + rank_ref[0, k:k + 1, :]
    pos_ref[0] = pos_v[...]
    cp = pltpu.make_async_copy(pos_v, pos_s, psem)
    cp.start()
    cp.wait()

    def body(t, carry):
        for k in range(TOP_K):
            pltpu.make_async_copy(h_ref.at[t], xs_hbm.at[pos_s[k, t]], ssem).start()
        return carry

    lax.fori_loop(0, TR, body, 0, unroll=8)
    n_rows = TR * TOP_K
    pltpu.make_async_copy(xs_hbm.at[pl.ds(0, n_rows)], xs_hbm.at[pl.ds(0, n_rows)], ssem).wait()


def _dispatch_call(hf, idx, rank, pad_start, fill_tab, n_pad):
    T, D = hf.shape
    n_tiles = T // TR
    row_spec = pl.BlockSpec((1, TOP_K, TR), lambda i, ft: (i, 0, 0))
    return pl.pallas_call(
        _dispatch_kernel,
        grid_spec=pltpu.PrefetchScalarGridSpec(
            num_scalar_prefetch=1,
            grid=(n_tiles,),
            in_specs=[row_spec, row_spec,
                      pl.BlockSpec((N_EXPERTS, 1), lambda i, ft: (0, 0)),
                      pl.BlockSpec((TR, D), lambda i, ft: (i, 0))],
            out_specs=[row_spec, pl.BlockSpec(memory_space=pl.ANY)],
            scratch_shapes=[pltpu.VMEM((TOP_K, TR), jnp.int32), pltpu.SMEM((TOP_K, TR), jnp.int32),
                            pltpu.VMEM((BM, D), hf.dtype),
                            pltpu.SemaphoreType.DMA, pltpu.SemaphoreType.DMA, pltpu.SemaphoreType.DMA]),
        out_shape=[jax.ShapeDtypeStruct((n_tiles, TOP_K, TR), jnp.int32),
                   jax.ShapeDtypeStruct((n_pad, D), hf.dtype)],
        compiler_params=pltpu.CompilerParams(dimension_semantics=("arbitrary",)),
    )(fill_tab, idx, rank, pad_start.reshape(N_EXPERTS, 1), hf)


def _expert_kernel(blk_e_ref, nvalid_ref, nused_ref, x_ref, wg_ref, wu_ref, wd_ref, y_ref, wg_s, wu_s, wd_s):
    i = pl.program_id(0)

    @pl.when(i < nused_ref[0])
    def _():
        e = blk_e_ref[i]
        prev = blk_e_ref[jnp.maximum(i - 1, 0)]

        @pl.when((i == 0) | (e != prev))
        def _():
            wg_s[...] = wg_ref[0].astype(jnp.bfloat16)
            wu_s[...] = wu_ref[0].astype(jnp.bfloat16)
            wd_s[...] = wd_ref[0].astype(jnp.bfloat16)

        n_valid = nvalid_ref[i]
        row = lax.broadcasted_iota(jnp.int32, (SUB, 1), 0)
        for sb in range(BM // SUB):
            rows = pl.ds(sb * SUB, SUB)

            @pl.when(sb * SUB < n_valid)
            def _(rows=rows, sb=sb):
                xa, xb = _unpack_rows(jnp.where(row + sb * SUB < n_valid, x_ref[rows, :], 0))
                xa = xa.astype(jnp.bfloat16)
                xb = xb.astype(jnp.bfloat16)
                half = xa.shape[1]
                g = _bdot(xa, wg_s[:half, :]) + _bdot(xb, wg_s[half:, :])
                u = _bdot(xa, wu_s[:half, :]) + _bdot(xb, wu_s[half:, :])
                a = (g * jax.nn.sigmoid(g)) * u
                y_ref[rows, :] = _pack_rows(_bdot(a.astype(jnp.bfloat16), wd_s[...]))

            @pl.when(sb * SUB >= n_valid)
            def _(rows=rows):
                y_ref[rows, :] = jnp.zeros((SUB, y_ref.shape[1]), y_ref.dtype)

    @pl.when(i >= nused_ref[0])
    def _():
        y_ref[...] = jnp.zeros_like(y_ref)


def _expert_call(x_sorted, blk_e, n_valid, n_used, w_gate, w_up, w_down):
    n_pad, DH = x_sorted.shape
    n_blk = n_pad // BM
    E, D, F = w_gate.shape

    def row_map(i, be, nv, nu):
        return (jnp.minimum(i, nu[0] - 1), 0)

    def w_map(i, be, nv, nu):
        return (be[jnp.minimum(i, nu[0] - 1)], 0, 0)

    return pl.pallas_call(
        _expert_kernel,
        grid_spec=pltpu.PrefetchScalarGridSpec(
            num_scalar_prefetch=3,
            grid=(n_blk,),
            in_specs=[pl.BlockSpec((BM, DH), row_map),
                      pl.BlockSpec((1, D, F), w_map),
                      pl.BlockSpec((1, D, F), w_map),
                      pl.BlockSpec((1, F, D), w_map)],
            out_specs=pl.BlockSpec((BM, DH), lambda i, be, nv, nu: (i, 0)),
            scratch_shapes=[pltpu.VMEM((D, F), jnp.bfloat16), pltpu.VMEM((D, F), jnp.bfloat16),
                            pltpu.VMEM((F, D), jnp.bfloat16)]),
        out_shape=jax.ShapeDtypeStruct((n_pad, DH), jnp.int32),
        compiler_params=pltpu.CompilerParams(dimension_semantics=("arbitrary",),
                                             vmem_limit_bytes=VMEM_LIMIT),
    )(blk_e, n_valid, n_used, x_sorted, w_gate, w_up, w_down)


TC = 128


def _combine_kernel(pos_hbm, y_hbm, w_ref, x1_ref, sh_ref, mod_ref, fg_ref, out_ref, ybuf, pos_s, gsem, psem):
    j = pl.program_id(0)
    last = pl.num_programs(0) - 1

    def pos_copy(b, slot):
        return pltpu.make_async_copy(pos_hbm.at[b], pos_s.at[slot], psem.at[slot])

    def start_gather(slot):
        for k in range(TOP_K):
            for t in range(TC):
                pltpu.make_async_copy(y_hbm.at[pos_s[slot, k * TC + t]], ybuf.at[slot, k, t],
                                      gsem.at[slot]).start()

    def wait_gather(slot):
        pltpu.make_async_copy(ybuf.at[slot], ybuf.at[slot], gsem.at[slot]).wait()

    @pl.when(j == 0)
    def _():
        pos_copy(0, 0).start()
        pos_copy(0, 0).wait()
        start_gather(0)
        pos_copy(jnp.minimum(1, last), 1).start()

    slot = j % 2
    nslot = 1 - slot
    pos_copy(0, nslot).wait()
    start_gather(nslot)
    pos_copy(jnp.minimum(j + 2, last), slot).start()
    wait_gather(slot)
    w = w_ref[...]
    acc_a = acc_b = None
    for k in range(TOP_K):
        ya, yb = _unpack_rows(ybuf[slot, k])
        acc_a = w[:, k:k + 1] * ya if k == 0 else acc_a + w[:, k:k + 1] * ya
        acc_b = w[:, k:k + 1] * yb if k == 0 else acc_b + w[:, k:k + 1] * yb
    acc = jnp.concatenate([acc_a, acc_b], axis=1)
    x2 = x1_ref[...] + mod_ref[0, 5:6, :] * (acc + sh_ref[...])
    out_ref[...] = x2 * lax.rsqrt(jnp.mean(x2 * x2, axis=-1, keepdims=True) + EPS) * fg_ref[...]

    @pl.when(j == last)
    def _():
        wait_gather(nslot)
        pos_copy(0, slot).wait()


def _combine_call(y_sorted, pos_t, wts, x1, shared, mod, final_g, n_ctx, lat_len):
    T, K = wts.shape
    DH = y_sorted.shape[1]
    D = 2 * DH
    n_tiles = T // TC
    row = functools.partial(_mod_row, tokens_per_tile=TC, n_ctx=n_ctx, lat_len=lat_len)
    return pl.pallas_call(
        _combine_kernel,
        grid=(n_tiles,),
        in_specs=[pl.BlockSpec(memory_space=pl.ANY),
                  pl.BlockSpec(memory_space=pl.ANY),
                  pl.BlockSpec((TC, K), lambda j: (j, 0)),
                  pl.BlockSpec((TC, D), lambda j: (j, 0)),
                  pl.BlockSpec((TC, D), lambda j: (j, 0)),
                  pl.BlockSpec((1, N_MOD, D), lambda j: (row(j), 0, 0)),
                  pl.BlockSpec((1, D), lambda j: (0, 0))],
        out_specs=pl.BlockSpec((TC, D), lambda j: (j, 0)),
        scratch_shapes=[pltpu.VMEM((2, K, TC, DH), jnp.uint32),
                        pltpu.SMEM((2, K * TC), jnp.int32),
                        pltpu.SemaphoreType.DMA((2,)), pltpu.SemaphoreType.DMA((2,))],
        out_shape=jax.ShapeDtypeStruct((T, D), jnp.float32),
        compiler_params=pltpu.CompilerParams(dimension_semantics=("arbitrary",)),
    )(pos_t, y_sorted, wts, x1, shared, mod, final_g.reshape(1, D))


SC_CORES = 2
SC_SUBCORES = 16
SC_CHUNK = 128


def _sc_gather_call(table, idx):
    n_idx = idx.shape[0]
    width = table.shape[1]
    n_workers = SC_CORES * SC_SUBCORES
    per_worker = n_idx // n_workers
    assert per_worker * n_workers == n_idx and per_worker % SC_CHUNK == 0
    mesh = plsc.VectorSubcoreMesh(core_axis_name="c", subcore_axis_name="s")

    def body(table_hbm, idx_hbm, out_hbm, idx_v, rows_v, sem):
        wid = lax.axis_index("s") * SC_CORES + lax.axis_index("c")
        base = wid * per_worker

        @pl.loop(0, per_worker // SC_CHUNK)
        def _(ch):
            off = base + ch * SC_CHUNK
            pltpu.sync_copy(idx_hbm.at[pl.ds(off, SC_CHUNK)], idx_v)
            pltpu.async_copy(table_hbm.at[idx_v], rows_v, sem).wait()
            pltpu.sync_copy(rows_v, out_hbm.at[pl.ds(off, SC_CHUNK)])

    return pl.kernel(
        body, out_type=jax.ShapeDtypeStruct((n_idx, width), table.dtype), mesh=mesh,
        scratch_types=[pltpu.VMEM((SC_CHUNK,), jnp.int32), pltpu.VMEM((SC_CHUNK, width), table.dtype),
                       pltpu.SemaphoreType.DMA],
    )(table, idx)


def _sc_scatter_call(src, pos, n_out):
    n_idx = pos.shape[0]
    width = src.shape[1]
    n_workers = SC_CORES * SC_SUBCORES
    per_worker = n_idx // n_workers
    assert per_worker * n_workers == n_idx and per_worker % SC_CHUNK == 0 and TR % SC_CHUNK == 0
    mesh = plsc.VectorSubcoreMesh(core_axis_name="c", subcore_axis_name="s")
    tile_pairs = TOP_K * TR

    def body(src_hbm, pos_hbm, out_hbm, idx_v, rows_v, sem):
        wid = lax.axis_index("s") * SC_CORES + lax.axis_index("c")
        base = wid * per_worker

        @pl.loop(0, per_worker // SC_CHUNK)
        def _(ch):
            off = base + ch * SC_CHUNK
            row0 = (off // tile_pairs) * TR + off % TR
            pltpu.sync_copy(pos_hbm.at[pl.ds(off, SC_CHUNK)], idx_v)
            pltpu.sync_copy(src_hbm.at[pl.ds(row0, SC_CHUNK)], rows_v)
            pltpu.async_copy(rows_v, out_hbm.at[idx_v], sem).wait()

    return pl.kernel(
        body, out_type=jax.ShapeDtypeStruct((n_out, width), src.dtype), mesh=mesh,
        scratch_types=[pltpu.VMEM((SC_CHUNK,), jnp.int32), pltpu.VMEM((SC_CHUNK, width), src.dtype),
                       pltpu.SemaphoreType.DMA],
    )(src, pos)


def _positions_kernel(idx_ref, rank_ref, pstart_ref, pos_ref):
    erow = lax.broadcasted_iota(jnp.int32, (N_EXPERTS, TR), 0)
    pstart = pstart_ref[...]
    for k in range(TOP_K):
        hit = erow == idx_ref[0, k:k + 1, :]
        seg = jnp.sum(jnp.where(hit, pstart, 0), axis=0, keepdims=True)
        pos_ref[0, k:k + 1, :] = seg + rank_ref[0, k:k + 1, :]


def _positions_call(idx, rank, pad_start):
    n_tiles = idx.shape[0]
    row_spec = pl.BlockSpec((1, TOP_K, TR), lambda i: (i, 0, 0))
    return pl.pallas_call(
        _positions_kernel,
        grid=(n_tiles,),
        in_specs=[row_spec, row_spec, pl.BlockSpec((N_EXPERTS, 1), lambda i: (0, 0))],
        out_specs=row_spec,
        out_shape=jax.ShapeDtypeStruct((n_tiles, TOP_K, TR), jnp.int32),
    )(idx, rank, pad_start.reshape(N_EXPERTS, 1))


def _combine_dense_kernel(g_ref, w_ref, x1_ref, sh_ref, mod_ref, fg_ref, out_ref):
    w = w_ref[...]
    acc_a = acc_b = None
    for k in range(TOP_K):
        ya, yb = _unpack_rows(g_ref[0, k])
        acc_a = w[:, k:k + 1] * ya if k == 0 else acc_a + w[:, k:k + 1] * ya
        acc_b = w[:, k:k + 1] * yb if k == 0 else acc_b + w[:, k:k + 1] * yb
    acc = jnp.concatenate([acc_a, acc_b], axis=1)
    x2 = x1_ref[...] + mod_ref[0, 5:6, :] * (acc + sh_ref[...])
    out_ref[...] = x2 * lax.rsqrt(jnp.mean(x2 * x2, axis=-1, keepdims=True) + EPS) * fg_ref[...]


def _combine_dense_call(gathered, wts, x1, shared, mod, final_g, n_ctx, lat_len):
    T, K = wts.shape
    DH = gathered.shape[-1]
    D = 2 * DH
    row = functools.partial(_mod_row, tokens_per_tile=TC, n_ctx=n_ctx, lat_len=lat_len)
    return pl.pallas_call(
        _combine_dense_kernel,
        grid=(T // TC,),
        in_specs=[pl.BlockSpec((1, K, TC, DH), lambda j: (j, 0, 0, 0)),
                  pl.BlockSpec((TC, K), lambda j: (j, 0)),
                  pl.BlockSpec((TC, D), lambda j: (j, 0)),
                  pl.BlockSpec((TC, D), lambda j: (j, 0)),
                  pl.BlockSpec((1, N_MOD, D), lambda j: (row(j), 0, 0)),
                  pl.BlockSpec((1, D), lambda j: (0, 0))],
        out_specs=pl.BlockSpec((TC, D), lambda j: (j, 0)),
        out_shape=jax.ShapeDtypeStruct((T, D), jnp.float32),
        compiler_params=pltpu.CompilerParams(dimension_semantics=("arbitrary",)),
    )(gathered, wts, x1, shared, mod, final_g.reshape(1, D))


def _moe_routed(h2, h2p, router_w, router_bias, w_gate, w_up, w_down):
    T, D = h2.shape
    idx, rank, w_rows, cnt = _route_call(h2, router_w, router_bias)
    wts = w_rows.transpose(0, 2, 1).reshape(T, TOP_K)
    counts = cnt[:, 0]
    padded = (counts + BM - 1) // BM * BM
    pad_end = jnp.cumsum(padded)
    pad_start = (pad_end - padded).astype(jnp.int32)
    n_pad = T * TOP_K + N_EXPERTS * BM
    n_blk = n_pad // BM
    n_used = (pad_end[-1] // BM).astype(jnp.int32).reshape(1)
    pos = _positions_call(idx, rank, pad_start)
    x_sorted = _sc_scatter_call(h2p, pos.reshape(-1), n_pad)
    blk_row0 = jnp.arange(n_blk, dtype=jnp.int32) * BM
    blk_e = jnp.minimum(jnp.sum((pad_end[None, :] <= blk_row0[:, None]).astype(jnp.int32), axis=1), N_EXPERTS - 1)
    own = blk_e[:, None] == jnp.arange(N_EXPERTS, dtype=jnp.int32)[None, :]
    seg_end = jnp.sum(jnp.where(own, (pad_start + counts)[None, :], 0), axis=1)
    n_valid = jnp.clip(seg_end - blk_row0, 0, BM).astype(jnp.int32)
    y = _expert_call(x_sorted, blk_e, n_valid, n_used, w_gate, w_up, w_down)
    pos_t = pos.reshape(T // TR, TOP_K, TR // TC, TC).transpose(0, 2, 1, 3).reshape(T // TC, TOP_K * TC)
    return y, pos_t, wts


def kernel(x_prompt, x_sample, state_delta, c, c_ctx, w_ada, b_ada, norm1_g, w_in, conv_w, a_log,
           dt_bias, onorm_g, pool_w, pool_scale, w_out, norm2_g, router_w, router_bias, exp_w_gate,
           exp_w_up, exp_w_down, sh_w_gate, sh_w_up, sh_w_down, final_g):
    Bc, Lc, D = x_prompt.shape
    Bl, Ll, _ = x_sample.shape
    n_ctx = Bc * Lc
    assert DEPTH == 1 and 1 + Bl <= MOD_ROWS and n_ctx % Ll == 0
    x_parts = (x_prompt.reshape(n_ctx, D), x_sample.reshape(Bl * Ll, D))
    cvec = jnp.concatenate([c_ctx[None], c, jnp.zeros((MOD_ROWS - 1 - Bl, D), c.dtype)], axis=0)
    l = 0
    mod = _ada_call(cvec, w_ada[l], b_ada[l]).reshape(MOD_ROWS, N_MOD, D)
    qkv, z, ba, u = _inproj_call(*x_parts, mod, norm1_g[l], w_in[l], Ll)
    dn = (conv_w[l], a_log[l], dt_bias[l], onorm_g[l])
    oa_c, st_ctx = _delta_call(qkv, z, ba, *dn, None, Bc, Lc, 0)
    oa_l, _ = _delta_call(qkv, z, ba, *dn, state_delta[:, l], Bl, Ll, n_ctx // Ll)
    op_c = _pool_call(u, pool_w[l], pool_scale[l], False, Bc, Lc, 0)
    op_l = _pool_call(u, pool_w[l], pool_scale[l], True, Bl, Ll, n_ctx // Ll)
    x1, h2, h2p, shared = _outproj_call(x_parts, (oa_c, oa_l), (op_c, op_l), mod, norm2_g[l], w_out[l],
                                        sh_w_gate[l], sh_w_up[l], sh_w_down[l], Ll)
    y, pos_t, wts = _moe_routed(h2, h2p, router_w[l], router_bias[l], exp_w_gate[l], exp_w_up[l],
                                exp_w_down[l])
    T = n_ctx + Bl * Ll
    gathered = _sc_gather_call(y, pos_t.reshape(-1))
    out = _combine_dense_call(gathered.reshape(T // TC, TOP_K, TC, D // 2), wts, x1, shared, mod, final_g,
                              n_ctx, Ll)
    y_prompt = out[:n_ctx].reshape(Bc, Lc, D)
    y_sample = out[n_ctx:].reshape(Bl, Ll, D)
    new_state_delta = st_ctx[:, None].astype(x_prompt.dtype)
    return (y_prompt, y_sample, new_state_delta)
```

```python
import functools
import jax, jax.numpy as jnp
from jax import lax
from jax.experimental import pallas as pl
from jax.experimental.pallas import tpu as pltpu
from jax.experimental.pallas import tpu_sc as plsc

D_MODEL = 1024
DEPTH = 1
GRID_W = 64
D_MIX = D_MODEL
D_A = D_MIX // 2
D_P = D_MIX - D_A
H_A = 4
DK = D_A // H_A
DV = D_A // H_A
CONV_K = 5
CHUNK = 64
POOL_WINDOWS = (2, 4, 8, 16)
N_PG = len(POOL_WINDOWS)
PG = D_P // N_PG
N_EXPERTS = 256
TOP_K = 8
N_GROUPS = 8
TOPK_GROUP = 4
ROUTED_SCALE = 2.5
EPS = 1e-6
VMEM_LIMIT = 48 * 1024 * 1024


def _split_bf16(a):
    hi = a.astype(jnp.bfloat16)
    return hi, (a - hi.astype(jnp.float32)).astype(jnp.bfloat16)


def _bdot(a, b):
    return jnp.dot(a, b, preferred_element_type=jnp.float32)


def _pack_rows(x):
    m = x.shape[1] // 2
    hi = lax.bitcast_convert_type(x[:, :m].astype(jnp.bfloat16).astype(jnp.float32), jnp.uint32)
    lo = lax.bitcast_convert_type(x[:, m:].astype(jnp.bfloat16).astype(jnp.float32), jnp.uint32)
    return lax.bitcast_convert_type(hi | (lo >> 16), jnp.int32)


def _unpack_rows(p):
    p = lax.bitcast_convert_type(p, jnp.uint32)
    hi = lax.bitcast_convert_type(p & jnp.uint32(0xFFFF0000), jnp.float32)
    lo = lax.bitcast_convert_type(p << 16, jnp.float32)
    return hi, lo


N_MOD = 6
MOD_ROWS = 8
TM = 512


def _ada_kernel(c_ref, w_ref, b_ref, o_ref):
    c = c_ref[...]
    s = c * jax.nn.sigmoid(c)
    sh, sl = _split_bf16(s)
    wh, wl = _split_bf16(w_ref[...])
    o_ref[...] = _bdot(sh, wh) + (_bdot(sh, wl) + _bdot(sl, wh)) + b_ref[...]


def _ada_call(cvec, w_ada, b_ada):
    R, D = cvec.shape
    N = w_ada.shape[1]
    tn = 1024
    return pl.pallas_call(
        _ada_kernel,
        grid=(N // tn,),
        in_specs=[pl.BlockSpec((R, D), lambda j: (0, 0)),
                  pl.BlockSpec((D, tn), lambda j: (0, j)),
                  pl.BlockSpec((1, tn), lambda j: (0, j))],
        out_specs=pl.BlockSpec((R, tn), lambda j: (0, j)),
        out_shape=jax.ShapeDtypeStruct((R, N), jnp.float32),
    )(cvec, w_ada, b_ada.reshape(1, N))


def _mod_row(tile, tokens_per_tile, n_ctx, lat_len):
    t0 = tile * tokens_per_tile
    return jnp.where(t0 < n_ctx, 0, 1 + (t0 - n_ctx) // lat_len)


def _two_part_specs(n_ctx_tiles, width):
    return (pl.BlockSpec((TM, width), lambda i: (jnp.minimum(i, n_ctx_tiles - 1), 0)),
            pl.BlockSpec((TM, width), lambda i: (jnp.maximum(i - n_ctx_tiles, 0), 0)))


def _pick(n_ctx_tiles, ctx_ref, lat_ref):
    return jnp.where(pl.program_id(0) < n_ctx_tiles, ctx_ref[...], lat_ref[...])


def _inproj_kernel(xc_ref, xl_ref, mod_ref, g_ref, wq_ref, wz_ref, wb_ref, wu_ref, q_ref, z_ref, b_ref, u_ref,
                   *, n_ctx_tiles):
    x = _pick(n_ctx_tiles, xc_ref, xl_ref)
    y = x * lax.rsqrt(jnp.mean(x * x, axis=-1, keepdims=True) + EPS) * g_ref[...]
    h = (y * (1.0 + mod_ref[0, 1:2, :]) + mod_ref[0, 0:1, :]).astype(jnp.bfloat16)
    q_ref[...] = _bdot(h, wq_ref[...])
    z_ref[...] = _bdot(h, wz_ref[...])
    b_ref[...] = _bdot(h, wb_ref[...])
    u_ref[...] = _bdot(h, wu_ref[...])


def _inproj_call(x_ctx, x_lat, mod, norm1_g, w_in, lat_len):
    n_ctx, D = x_ctx.shape
    T = n_ctx + x_lat.shape[0]
    bf = jnp.bfloat16
    nq, nz, nb = 3 * D_A, D_A, 4 * H_A
    wq = w_in[:, :nq].astype(bf)
    wz = w_in[:, nq:nq + nz].astype(bf)
    wb = jnp.pad(w_in[:, nq + nz:nq + nz + nb], ((0, 0), (0, 128 - nb))).astype(bf)
    wu = w_in[:, nq + nz + nb:].astype(bf)
    row = functools.partial(_mod_row, tokens_per_tile=TM, n_ctx=n_ctx, lat_len=lat_len)

    def full(a):
        return pl.BlockSpec(a.shape, lambda i: (0, 0))

    def rows(n):
        return pl.BlockSpec((TM, n), lambda i: (i, 0))

    return pl.pallas_call(
        functools.partial(_inproj_kernel, n_ctx_tiles=n_ctx // TM),
        grid=(T // TM,),
        in_specs=[*_two_part_specs(n_ctx // TM, D), pl.BlockSpec((1, N_MOD, D), lambda i: (row(i), 0, 0)),
                  pl.BlockSpec((1, D), lambda i: (0, 0)), full(wq), full(wz), full(wb), full(wu)],
        out_specs=[rows(nq), rows(nz), rows(128), rows(D_P)],
        out_shape=[jax.ShapeDtypeStruct((T, nq), jnp.float32), jax.ShapeDtypeStruct((T, nz), jnp.float32),
                   jax.ShapeDtypeStruct((T, 128), jnp.float32), jax.ShapeDtypeStruct((T, D_P), jnp.float32)],
        compiler_params=pltpu.CompilerParams(dimension_semantics=("arbitrary",),
                                             vmem_limit_bytes=VMEM_LIMIT),
    )(x_ctx, x_lat, mod, norm1_g.reshape(1, D), wq, wz, wb, wu)


PT = 256


def _window_bounds(pos, w, n):
    return jnp.maximum(pos - w // 2, 0), jnp.minimum(pos + w - w // 2, n)


def _band_sum(band, x):
    xh, xl = _split_bf16(x)
    return _bdot(band, xh) + _bdot(band, xl)


def _pool_seq_kernel(u_ref, pw_ref, ps_ref, o_ref):
    L = u_ref.shape[0]
    ti = lax.broadcasted_iota(jnp.int32, (L, L), 0)
    ji = lax.broadcasted_iota(jnp.int32, (L, L), 1)
    tcol = lax.broadcasted_iota(jnp.int32, (L, 1), 0)
    for i, w in enumerate(POOL_WINDOWS):
        lo, hi = _window_bounds(ti, w, L)
        band = ((ji >= lo) & (ji < hi)).astype(jnp.bfloat16)
        clo, chi = _window_bounds(tcol, w, L)
        ug = u_ref[:, i * PG:(i + 1) * PG]
        mean = _band_sum(band, ug) / (chi - clo).astype(jnp.float32)
        d = (mean - ug).astype(jnp.bfloat16)
        o_ref[:, i * PG:(i + 1) * PG] = _bdot(d, pw_ref[i]) * ps_ref[:, i * PG:(i + 1) * PG]


def _pool_grid_kernel(u_ref, pw_ref, ps_ref, o_ref, pad_s, r_s):
    L = u_ref.shape[0]
    rows = L // GRID_W
    halo = (max(POOL_WINDOWS) // 2) * GRID_W
    pad_s[0:halo, :] = jnp.zeros((halo, D_P), jnp.float32)
    pad_s[halo + L:, :] = jnp.zeros((halo, D_P), jnp.float32)
    pad_s[halo:halo + L, :] = u_ref[...]
    ti = lax.broadcasted_iota(jnp.int32, (PT, PT), 0)
    ji = lax.broadcasted_iota(jnp.int32, (PT, PT), 1)
    tcol = lax.broadcasted_iota(jnp.int32, (PT, 1), 0)
    for i, w in enumerate(POOL_WINDOWS):
        cs = slice(i * PG, (i + 1) * PG)
        acc = None
        for dr in range(-(w // 2), w - w // 2):
            part = pad_s[halo + dr * GRID_W:halo + dr * GRID_W + L, cs]
            acc = part if acc is None else acc + part
        r_s[...] = acc
        lo, hi = _window_bounds(ti % GRID_W, w, GRID_W)
        band = ((ji // GRID_W == ti // GRID_W) & (ji % GRID_W >= lo) & (ji % GRID_W < hi)).astype(jnp.bfloat16)
        clo, chi = _window_bounds(tcol % GRID_W, w, GRID_W)
        ccnt = (chi - clo).astype(jnp.float32)
        for tile in range(L // PT):
            ts = slice(tile * PT, (tile + 1) * PT)
            rlo, rhi = _window_bounds(tile * (PT // GRID_W) + tcol // GRID_W, w, rows)
            mean = _band_sum(band, r_s[ts, :]) / ((rhi - rlo).astype(jnp.float32) * ccnt)
            d = (mean - u_ref[ts, cs]).astype(jnp.bfloat16)
            o_ref[ts, cs] = _bdot(d, pw_ref[i]) * ps_ref[:, cs]


def _pool_call(u, pool_w, pool_scale, grid, B, L, row_blk0):
    pw = pool_w.astype(jnp.bfloat16)
    ps = pool_scale.reshape(1, D_P)
    specs = dict(
        grid=(B,),
        in_specs=[pl.BlockSpec((L, D_P), lambda b: (row_blk0 + b, 0)),
                  pl.BlockSpec((N_PG, PG, PG), lambda b: (0, 0, 0)),
                  pl.BlockSpec((1, D_P), lambda b: (0, 0))],
        out_specs=pl.BlockSpec((L, D_P), lambda b: (b, 0)),
        out_shape=jax.ShapeDtypeStruct((B * L, D_P), jnp.float32),
        compiler_params=pltpu.CompilerParams(dimension_semantics=("arbitrary",),
                                             vmem_limit_bytes=VMEM_LIMIT))
    if not grid:
        return pl.pallas_call(_pool_seq_kernel, **specs)(u, pw, ps)
    halo = (max(POOL_WINDOWS) // 2) * GRID_W
    return pl.pallas_call(
        _pool_grid_kernel,
        scratch_shapes=[pltpu.VMEM((L + 2 * halo, D_P), jnp.float32), pltpu.VMEM((L, PG), jnp.float32)],
        **specs)(u, pw, ps)


def _outproj_kernel(xc_ref, xl_ref, oac_ref, oal_ref, opc_ref, opl_ref, mod_ref, g2_ref, wo_ref, sg_ref, su_ref,
                    sd_ref, x1_ref, h2_ref, h2p_ref, sh_ref, *, n_ctx_tiles):
    o_a = _pick(n_ctx_tiles, oac_ref, oal_ref)
    o_p = _pick(n_ctx_tiles, opc_ref, opl_ref)
    mix = (_bdot(o_a.astype(jnp.bfloat16), wo_ref[:D_A, :])
           + _bdot(o_p.astype(jnp.bfloat16), wo_ref[D_A:, :]))
    x1 = _pick(n_ctx_tiles, xc_ref, xl_ref) + mod_ref[0, 2:3, :] * mix
    x1_ref[...] = x1
    y = x1 * lax.rsqrt(jnp.mean(x1 * x1, axis=-1, keepdims=True) + EPS) * g2_ref[...]
    h2 = y * (1.0 + mod_ref[0, 4:5, :]) + mod_ref[0, 3:4, :]
    h2_ref[...] = h2
    h2p_ref[...] = _pack_rows(h2)
    hb = h2.astype(jnp.bfloat16)
    g = _bdot(hb, sg_ref[...])
    a = (g * jax.nn.sigmoid(g)) * _bdot(hb, su_ref[...])
    sh_ref[...] = _bdot(a.astype(jnp.bfloat16), sd_ref[...])


def _outproj_call(x_parts, oa_parts, op_parts, mod, norm2_g, w_out, sh_gate, sh_up, sh_down, lat_len):
    n_ctx, D = x_parts[0].shape
    T = n_ctx + x_parts[1].shape[0]
    nct = n_ctx // TM
    bf = jnp.bfloat16
    row = functools.partial(_mod_row, tokens_per_tile=TM, n_ctx=n_ctx, lat_len=lat_len)
    ws = [w_out.astype(bf), sh_gate.astype(bf), sh_up.astype(bf), sh_down.astype(bf)]

    def rows(n):
        return pl.BlockSpec((TM, n), lambda i: (i, 0))

    return pl.pallas_call(
        functools.partial(_outproj_kernel, n_ctx_tiles=nct),
        grid=(T // TM,),
        in_specs=[*_two_part_specs(nct, D), *_two_part_specs(nct, D_A), *_two_part_specs(nct, D_P),
                  pl.BlockSpec((1, N_MOD, D), lambda i: (row(i), 0, 0)),
                  pl.BlockSpec((1, D), lambda i: (0, 0))] + [pl.BlockSpec(w.shape, lambda i: (0, 0)) for w in ws],
        out_specs=[rows(D), rows(D), rows(D // 2), rows(D)],
        out_shape=[jax.ShapeDtypeStruct((T, D), jnp.float32), jax.ShapeDtypeStruct((T, D), jnp.float32),
                   jax.ShapeDtypeStruct((T, D // 2), jnp.int32), jax.ShapeDtypeStruct((T, D), jnp.float32)],
        compiler_params=pltpu.CompilerParams(dimension_semantics=("arbitrary",),
                                             vmem_limit_bytes=VMEM_LIMIT),
    )(*x_parts, *oa_parts, *op_parts, mod, norm2_g.reshape(1, D), *ws)


SC = 256
CPS = SC // CHUNK
BASE = 16
DELTA_HEAD_ROWS = 4096


def _mm(a, b):
    return jnp.dot(a.astype(jnp.bfloat16), b.astype(jnp.bfloat16), preferred_element_type=jnp.float32)


def _mm_nt(a, b):
    return lax.dot_general(a.astype(jnp.bfloat16), b.astype(jnp.bfloat16), (((1,), (1,)), ((), ())),
                           preferred_element_type=jnp.float32)


def _softplus(x):
    return jnp.maximum(x, 0.0) + jnp.log(1.0 + jnp.exp(-jnp.abs(x)))


def _delta_kernel(sc_ref, xq_ref, xk_ref, xv_ref, z_ref, bac_ref, bar_ref, cwq_ref, cwk_ref, cwv_ref,
                  og_ref, s0_ref, o_ref, st_ref, q_s, k_s, v_s, o_s, vn_s, *, n_sc, zero_init, hpb):
    hb = pl.program_id(1)
    L = q_s.shape[1]

    def conv(x_ref, w_ref, cs):
        x = x_ref[:, cs]
        row = lax.broadcasted_iota(jnp.int32, x.shape, 0)
        acc = x * w_ref[CONV_K // 2:CONV_K // 2 + 1, cs]
        for j in range(CONV_K):
            d = j - CONV_K // 2
            if d == 0:
                continue
            xs = pltpu.roll(x, (-d) % L, 0)
            ok = (row + d >= 0) & (row + d < L)
            acc = acc + jnp.where(ok, xs, 0.0) * w_ref[j:j + 1, cs]
        return acc * jax.nn.sigmoid(acc)

    for hh in range(hpb):
        cs = slice(hh * DK, (hh + 1) * DK)
        q = conv(xq_ref, cwq_ref, cs)
        q_s[hh] = q * lax.rsqrt(jnp.sum(q * q, axis=-1, keepdims=True) + EPS) * (DK ** -0.5)
        k = conv(xk_ref, cwk_ref, cs)
        k_s[hh] = k * lax.rsqrt(jnp.sum(k * k, axis=-1, keepdims=True) + EPS)
        v_s[hh] = conv(xv_ref, cwv_ref, cs)
    o_s[...] = jnp.zeros_like(o_s)

    ri = lax.broadcasted_iota(jnp.int32, (SC, SC), 0)
    ci = lax.broadcasted_iota(jnp.int32, (SC, SC), 1)
    same = (ri // CHUNK) == (ci // CHUNK)
    same_base = (ri // BASE) == (ci // BASE)
    merge_masks = [(ri // w) == (ci // w) for w in (2 * BASE, CHUNK)]
    eye = (ri == ci).astype(jnp.float32)
    rowi = lax.broadcasted_iota(jnp.int32, (SC, DV), 0)

    def prep(m, d, hh):
        r0 = pl.multiple_of(m * SC, SC)
        h = hb * hpb + hh
        q = q_s[hh, pl.ds(r0, SC), :]
        k = k_s[hh, pl.ds(r0, SC), :]
        v = v_s[hh, pl.ds(r0, SC), :]
        bc = bac_ref[0, hh, pl.ds(r0, SC), :]
        br = bar_ref[0, hh, m]
        a_l = sc_ref[d * H_A + h]
        dtb = sc_ref[2 * H_A + d * H_A + h]
        neg_ea = -jnp.exp(jnp.full((1, 1), a_l, jnp.float32))
        beta = jax.nn.sigmoid(bc[:, d:d + 1])
        g_col = neg_ea * _softplus(bc[:, 2 + d:3 + d] + dtb)
        g_row = neg_ea * _softplus(br[2 + d:3 + d, :] + dtb)
        if d == 0:
            tri, strict = same & (ci <= ri), same & (ci < ri)
        else:
            tri, strict = same & (ci >= ri), same & (ci > ri)
        tri_t = same & (ri <= ci) if d == 0 else same & (ri >= ci)
        gc_col = jnp.sum(jnp.where(tri, g_row, 0.0), axis=1, keepdims=True)
        gc_row = jnp.sum(jnp.where(tri_t, g_col, 0.0), axis=0, keepdims=True)
        gl_col = jnp.sum(jnp.where(same, g_row, 0.0), axis=1, keepdims=True)
        decay = jnp.where(tri, jnp.exp(jnp.where(tri, gc_col - gc_row, 0.0)), 0.0)
        kb = k * beta
        a = jnp.where(strict, _mm_nt(kb, k) * decay, 0.0)
        attn = jnp.where(tri, _mm_nt(q, k) * decay, 0.0)
        eg = jnp.exp(gc_col)
        x = jnp.concatenate([v * beta, kb * eg], axis=1)
        qd = q * eg
        kdt = (k * jnp.exp(gl_col - gc_col)).T
        return dict(r0=r0, a=a, attn=attn, x=x, qd=qd, kdt=kdt, egl=jnp.exp(gl_col))

    def run_chains(ms, states):
        n = len(chains)
        ops = [prep(ms[i], d, hh) for i, (hh, d) in enumerate(chains)]
        ps = [jnp.where(same_base, o["a"], 0.0) for o in ops]
        ts = [eye - p for p in ps]
        for _ in range(BASE.bit_length() - 2):
            ps = [_mm(p, p) for p in ps]
            ts = [t + _mm(t, p) for t, p in zip(ts, ps)]
        inner = same_base
        for outer in merge_masks:
            lows = [_mm(jnp.where(outer & ~inner, o["a"], 0.0), t) for o, t in zip(ops, ts)]
            ts = [t - _mm(t, low) for t, low in zip(ts, lows)]
            inner = outer
        xs = [_mm(t, o["x"]) for t, o in zip(ts, ops)]
        for i in range(n):
            vn_s[i] = jnp.zeros((SC, DV), jnp.float32)
        states = list(states)
        for step in range(CPS):
            cs = [step if d == 0 else CPS - 1 - step for _, d in chains]
            los = [c * CHUNK for c in cs]
            ws_qs = [_mm(jnp.concatenate([x[lo:lo + CHUNK, DV:], o["qd"][lo:lo + CHUNK]], axis=0), s)
                     for x, o, lo, s in zip(xs, ops, los, states)]
            for i in range(n):
                vn_s[i, los[i]:los[i] + CHUNK, :] = xs[i][los[i]:los[i] + CHUNK, :DV] - ws_qs[i][:CHUNK]
            vns = [vn_s[i] for i in range(n)]
            o_cs = [wq[CHUNK:] + _mm(o["attn"][lo:lo + CHUNK, :], vn)
                    for wq, o, lo, vn in zip(ws_qs, ops, los, vns)]
            for i, (hh, _) in enumerate(chains):
                o_s[hh, pl.ds(ops[i]["r0"] + los[i], CHUNK), :] += o_cs[i]
            states = [s * o["egl"][lo:lo + 1, :]
                      + _mm(o["kdt"], jnp.where((rowi >= lo) & (rowi < lo + CHUNK), vn, 0.0))
                      for s, o, lo, vn in zip(states, ops, los, vns)]
        return tuple(states)

    if zero_init:
        states = tuple(jnp.zeros((DK, DV), jnp.float32) for _ in range(2 * hpb))
    else:
        states = tuple(s0_ref[0, d, hh] for hh in range(hpb) for d in range(2))

    chains = [(hh, d) for hh in range(hpb) for d in range(2)]

    def body(m, carry):
        return run_chains([m if d == 0 else n_sc - 1 - m for _, d in chains], carry)

    if n_sc == 1:
        states = body(0, states)
    else:
        states = lax.fori_loop(0, n_sc, body, states)

    for hh in range(hpb):
        for d in range(2):
            st_ref[0, d, hh] = states[2 * hh + d]
        o = o_s[hh]
        o = o * lax.rsqrt(jnp.mean(o * o, axis=-1, keepdims=True) + EPS) * og_ref[...]
        zz = z_ref[:, hh * DV:(hh + 1) * DV]
        o_ref[:, hh * DV:(hh + 1) * DV] = o * (zz * jax.nn.sigmoid(zz))


def _delta_call(qkv, z, ba, conv_w, a_log, dt_bias, onorm_g, s0, B, L, row_blk0):
    n_sc = L // SC
    t0 = row_blk0 * L
    bah = ba[t0:t0 + B * L, :4 * H_A].reshape(B, L, 4, H_A).transpose(0, 3, 1, 2)
    bar = bah.reshape(B, H_A, n_sc, SC, 4).transpose(0, 1, 2, 4, 3)
    scal = jnp.concatenate([a_log.reshape(-1), dt_bias.reshape(-1)]).astype(jnp.float32)
    hpb = max(1, min(H_A, DELTA_HEAD_ROWS // L))
    n_hb = H_A // hpb
    zero_init = s0 is None
    if zero_init:
        s0 = jnp.zeros((1, 2, hpb, DK, DV), jnp.float32)
        s0_spec = pl.BlockSpec((1, 2, hpb, DK, DV), lambda b, h, sc: (0, 0, 0, 0, 0))
    else:
        s0_spec = pl.BlockSpec((1, 2, hpb, DK, DV), lambda b, h, sc: (b, 0, h, 0, 0))

    def col(off):
        return pl.BlockSpec((L, hpb * DK), lambda b, h, sc: (row_blk0 + b, off * n_hb + h))

    def cw(off):
        return pl.BlockSpec((CONV_K, hpb * DK), lambda b, h, sc: (0, off * n_hb + h))

    kern = functools.partial(_delta_kernel, n_sc=n_sc, zero_init=zero_init, hpb=hpb)
    return pl.pallas_call(
        kern,
        grid_spec=pltpu.PrefetchScalarGridSpec(
            num_scalar_prefetch=1,
            grid=(B, n_hb),
            in_specs=[col(0), col(1), col(2),
                      pl.BlockSpec((L, hpb * DV), lambda b, h, sc: (row_blk0 + b, h)),
                      pl.BlockSpec((1, hpb, L, 4), lambda b, h, sc: (b, h, 0, 0)),
                      pl.BlockSpec((1, hpb, n_sc, 4, SC), lambda b, h, sc: (b, h, 0, 0, 0)),
                      cw(0), cw(1), cw(2),
                      pl.BlockSpec((1, DV), lambda b, h, sc: (0, 0)),
                      s0_spec],
            out_specs=[pl.BlockSpec((L, hpb * DV), lambda b, h, sc: (b, h)),
                       pl.BlockSpec((1, 2, hpb, DK, DV), lambda b, h, sc: (b, 0, h, 0, 0))],
            scratch_shapes=[pltpu.VMEM((hpb, L, DK), jnp.float32), pltpu.VMEM((hpb, L, DK), jnp.float32),
                            pltpu.VMEM((hpb, L, DV), jnp.float32), pltpu.VMEM((hpb, L, DV), jnp.float32),
                            pltpu.VMEM((2 * hpb, SC, DV), jnp.float32)]),
        out_shape=[jax.ShapeDtypeStruct((B * L, D_A), jnp.float32),
                   jax.ShapeDtypeStruct((B, 2, H_A, DK, DV), jnp.float32)],
        compiler_params=pltpu.CompilerParams(dimension_semantics=("arbitrary", "arbitrary"),
                                             vmem_limit_bytes=VMEM_LIMIT),
    )(scal, qkv, qkv, qkv, z, bah, bar, conv_w, conv_w, conv_w, onorm_g.reshape(1, DV), s0)


TR = 256
GSZ = N_EXPERTS // N_GROUPS
NEG = -jnp.inf
BM = 512
SUBLANES = 8
FILL_PIECES = tuple(p for p in (BM >> s for s in range(1, BM.bit_length())) if p >= SUBLANES)


def _route_kernel(h_ref, rwh_ref, rwl_ref, rb_ref, idx_ref, rank_ref, w_ref, cnt_ref, cnt_s):
    i = pl.program_id(0)

    @pl.when(i == 0)
    def _():
        cnt_s[...] = jnp.zeros_like(cnt_s)

    h = h_ref[...]
    hh, hl = _split_bf16(h)
    logits = _bdot(hh, rwh_ref[...]) + (_bdot(hh, rwl_ref[...]) + _bdot(hl, rwh_ref[...]))
    scores = jax.nn.sigmoid(logits.T)
    sel = scores + rb_ref[...]
    erow = lax.broadcasted_iota(jnp.int32, sel.shape, 0)
    grow = lax.broadcasted_iota(jnp.int32, (GSZ, TR), 0)

    def first_argmax(v, rows):
        m = jnp.max(v, axis=0, keepdims=True)
        first = jnp.min(jnp.where(v == m, rows, N_EXPERTS), axis=0, keepdims=True)
        return m, first

    gs = []
    for g in range(N_GROUPS):
        vg = sel[g * GSZ:(g + 1) * GSZ, :]
        m1, i1 = first_argmax(vg, grow)
        m2 = jnp.max(jnp.where(grow == i1, NEG, vg), axis=0, keepdims=True)
        gs.append(m1 + m2)
    cand = []
    for g in range(N_GROUPS):
        beat = jnp.zeros(gs[g].shape, jnp.int32)
        for o in range(N_GROUPS):
            if o == g:
                continue
            wins = (gs[o] > gs[g]) | ((gs[o] == gs[g]) & (o < g))
            beat = beat + wins.astype(jnp.int32)
        cand.append(jnp.where(beat < TOPK_GROUP, sel[g * GSZ:(g + 1) * GSZ, :], NEG))
    cand = jnp.concatenate(cand, axis=0)
    chosen = []
    picked = jnp.zeros(sel.shape, jnp.bool_)
    for _ in range(TOP_K):
        _, ik = first_argmax(cand, erow)
        hit = erow == ik
        chosen.append((ik, hit))
        picked = picked | hit
        cand = jnp.where(hit, NEG, cand)
    wsum = jnp.sum(jnp.where(picked, scores, 0.0), axis=0, keepdims=True)

    ri = lax.broadcasted_iota(jnp.int32, (TR, TR), 0)
    ci = lax.broadcasted_iota(jnp.int32, (TR, TR), 1)
    earlier = (ri < ci).astype(jnp.bfloat16)
    rank_mat = _bdot(picked.astype(jnp.bfloat16), earlier) + cnt_s[...]
    cnt_s[...] = cnt_s[...] + jnp.sum(picked.astype(jnp.float32), axis=1, keepdims=True)
    cnt_ref[...] = cnt_s[...].astype(jnp.int32)

    for k, (ik, hit) in enumerate(chosen):
        idx_ref[0, k:k + 1, :] = ik
        rank_ref[0, k:k + 1, :] = jnp.sum(jnp.where(hit, rank_mat, 0.0), axis=0, keepdims=True).astype(jnp.int32)
        w_ref[0, k:k + 1, :] = jnp.sum(jnp.where(hit, scores, 0.0), axis=0, keepdims=True) / wsum * ROUTED_SCALE


def _route_call(hf, router_w, router_bias):
    T, D = hf.shape
    n_tiles = T // TR
    rwh, rwl = _split_bf16(router_w)
    row_spec = pl.BlockSpec((1, TOP_K, TR), lambda i: (i, 0, 0))
    return pl.pallas_call(
        _route_kernel,
        grid=(n_tiles,),
        in_specs=[pl.BlockSpec((TR, D), lambda i: (i, 0)),
                  pl.BlockSpec((D, N_EXPERTS), lambda i: (0, 0)),
                  pl.BlockSpec((D, N_EXPERTS), lambda i: (0, 0)),
                  pl.BlockSpec((N_EXPERTS, 1), lambda i: (0, 0))],
        out_specs=[row_spec, row_spec, row_spec, pl.BlockSpec((N_EXPERTS, 1), lambda i: (0, 0))],
        scratch_shapes=[pltpu.VMEM((N_EXPERTS, 1), jnp.float32)],
        out_shape=[jax.ShapeDtypeStruct((n_tiles, TOP_K, TR), jnp.int32),
                   jax.ShapeDtypeStruct((n_tiles, TOP_K, TR), jnp.int32),
                   jax.ShapeDtypeStruct((n_tiles, TOP_K, TR), jnp.float32),
                   jax.ShapeDtypeStruct((N_EXPERTS, 1), jnp.int32)],
        compiler_params=pltpu.CompilerParams(dimension_semantics=("arbitrary",)),
    )(hf, rwh, rwl, router_bias.reshape(N_EXPERTS, 1).astype(jnp.float32))


def _dispatch_kernel(fill_ref, idx_ref, rank_ref, pstart_ref, h_ref, pos_ref, xs_hbm, pos_v, pos_s, zbuf,
                     ssem, psem, zsem):
    n_blk = xs_hbm.shape[0] // BM

    @pl.when(pl.program_id(0) == 0)
    def _():
        zbuf[...] = jnp.zeros_like(zbuf)

        def pad_copies(e, act):
            n = fill_ref[N_EXPERTS + e]
            start = fill_ref[e]
            head = jnp.minimum((-start) & (SUBLANES - 1), n)
            for j in range(SUBLANES - 1):
                @pl.when(j < head)
                def _(j=j):
                    act(pltpu.make_async_copy(zbuf.at[0], xs_hbm.at[start + j], zsem))
            off = start + head
            rest = n - head
            for piece in FILL_PIECES:
                @pl.when((rest & piece) != 0)
                def _(off=off, piece=piece):
                    dst = xs_hbm.at[pl.ds(pl.multiple_of(off, SUBLANES), piece)]
                    act(pltpu.make_async_copy(zbuf.at[pl.ds(0, piece)], dst, zsem))
                off = off + (rest & piece)

        def tail_copy(b, act):
            act(pltpu.make_async_copy(zbuf, xs_hbm.at[pl.ds(pl.multiple_of(b * BM, BM), BM)], zsem))

        for act in (lambda c: c.start(), lambda c: c.wait()):
            def per_expert(e, carry, act=act):
                pad_copies(e, act)
                return carry

            def per_block(b, carry, act=act):
                tail_copy(b, act)
                return carry

            lax.fori_loop(0, N_EXPERTS, per_expert, 0)
            lax.fori_loop(fill_ref[2 * N_EXPERTS], n_blk, per_block, 0)

    erow = lax.broadcasted_iota(jnp.int32, (N_EXPERTS, TR), 0)
    pstart = pstart_ref[...]
    for k in range(TOP_K):
        hit = erow == idx_ref[0, k:k + 1, :]
        seg = jnp.sum(jnp.where(hit, pstart, 0), axis=0, keepdims=True)
        pos_v[k:k + 1, :] = seg + rank_ref[0, k:k + 1, :]
    pos_ref[0] = pos_v[...]
    cp = pltpu.make_async_copy(pos_v, pos_s, psem)
    cp.start()
    cp.wait()

    def body(t, carry):
        for k in range(TOP_K):
            pltpu.make_async_copy(h_ref.at[t], xs_hbm.at[pos_s[k, t]], ssem).start()
        return carry

    lax.fori_loop(0, TR, body, 0, unroll=8)
    n_rows = TR * TOP_K
    pltpu.make_async_copy(xs_hbm.at[pl.ds(0, n_rows)], xs_hbm.at[pl.ds(0, n_rows)], ssem).wait()


def _dispatch_call(hf, idx, rank, pad_start, fill_tab, n_pad):
    T, D = hf.shape
    n_tiles = T // TR
    row_spec = pl.BlockSpec((1, TOP_K, TR), lambda i, ft: (i, 0, 0))
    return pl.pallas_call(
        _dispatch_kernel,
        grid_spec=pltpu.PrefetchScalarGridSpec(
            num_scalar_prefetch=1,
            grid=(n_tiles,),
            in_specs=[row_spec, row_spec,
                      pl.BlockSpec((N_EXPERTS, 1), lambda i, ft: (0, 0)),
                      pl.BlockSpec((TR, D), lambda i, ft: (i, 0))],
            out_specs=[row_spec, pl.BlockSpec(memory_space=pl.ANY)],
            scratch_shapes=[pltpu.VMEM((TOP_K, TR), jnp.int32), pltpu.SMEM((TOP_K, TR), jnp.int32),
                            pltpu.VMEM((BM, D), hf.dtype),
                            pltpu.SemaphoreType.DMA, pltpu.SemaphoreType.DMA, pltpu.SemaphoreType.DMA]),
        out_shape=[jax.ShapeDtypeStruct((n_tiles, TOP_K, TR), jnp.int32),
                   jax.ShapeDtypeStruct((n_pad, D), hf.dtype)],
        compiler_params=pltpu.CompilerParams(dimension_semantics=("arbitrary",)),
    )(fill_tab, idx, rank, pad_start.reshape(N_EXPERTS, 1), hf)


def _expert_kernel(blk_e_ref, nvalid_ref, nused_ref, x_ref, wg_ref, wu_ref, wd_ref, y_ref, wg_s, wu_s, wd_s):
    i = pl.program_id(0)

    @pl.when(i < nused_ref[0])
    def _():
        e = blk_e_ref[i]
        prev = blk_e_ref[jnp.maximum(i - 1, 0)]

        @pl.when((i == 0) | (e != prev))
        def _():
            wg_s[...] = wg_ref[0].astype(jnp.bfloat16)
            wu_s[...] = wu_ref[0].astype(jnp.bfloat16)
            wd_s[...] = wd_ref[0].astype(jnp.bfloat16)

        row = lax.broadcasted_iota(jnp.int32, (BM, 1), 0)
        xa, xb = _unpack_rows(jnp.where(row < nvalid_ref[i], x_ref[...], 0))
        xa = xa.astype(jnp.bfloat16)
        xb = xb.astype(jnp.bfloat16)
        half = xa.shape[1]
        g = _bdot(xa, wg_s[:half, :]) + _bdot(xb, wg_s[half:, :])
        u = _bdot(xa, wu_s[:half, :]) + _bdot(xb, wu_s[half:, :])
        a = (g * jax.nn.sigmoid(g)) * u
        y_ref[...] = _pack_rows(_bdot(a.astype(jnp.bfloat16), wd_s[...]))

    @pl.when(i >= nused_ref[0])
    def _():
        y_ref[...] = jnp.zeros_like(y_ref)


def _expert_call(x_sorted, blk_e, n_valid, n_used, w_gate, w_up, w_down):
    n_pad, DH = x_sorted.shape
    n_blk = n_pad // BM
    E, D, F = w_gate.shape

    def row_map(i, be, nv, nu):
        return (jnp.minimum(i, nu[0] - 1), 0)

    def w_map(i, be, nv, nu):
        return (be[jnp.minimum(i, nu[0] - 1)], 0, 0)

    return pl.pallas_call(
        _expert_kernel,
        grid_spec=pltpu.PrefetchScalarGridSpec(
            num_scalar_prefetch=3,
            grid=(n_blk,),
            in_specs=[pl.BlockSpec((BM, DH), row_map),
                      pl.BlockSpec((1, D, F), w_map),
                      pl.BlockSpec((1, D, F), w_map),
                      pl.BlockSpec((1, F, D), w_map)],
            out_specs=pl.BlockSpec((BM, DH), lambda i, be, nv, nu: (i, 0)),
            scratch_shapes=[pltpu.VMEM((D, F), jnp.bfloat16), pltpu.VMEM((D, F), jnp.bfloat16),
                            pltpu.VMEM((F, D), jnp.bfloat16)]),
        out_shape=jax.ShapeDtypeStruct((n_pad, DH), jnp.int32),
        compiler_params=pltpu.CompilerParams(dimension_semantics=("arbitrary",),
                                             vmem_limit_bytes=VMEM_LIMIT),
    )(blk_e, n_valid, n_used, x_sorted, w_gate, w_up, w_down)


TC = 128


def _combine_kernel(pos_hbm, y_hbm, w_ref, x1_ref, sh_ref, mod_ref, fg_ref, out_ref, ybuf, pos_s, gsem, psem):
    j = pl.program_id(0)
    last = pl.num_programs(0) - 1

    def pos_copy(b, slot):
        return pltpu.make_async_copy(pos_hbm.at[b], pos_s.at[slot], psem.at[slot])

    def start_gather(slot):
        for k in range(TOP_K):
            for t in range(TC):
                pltpu.make_async_copy(y_hbm.at[pos_s[slot, k * TC + t]], ybuf.at[slot, k, t],
                                      gsem.at[slot]).start()

    def wait_gather(slot):
        pltpu.make_async_copy(ybuf.at[slot], ybuf.at[slot], gsem.at[slot]).wait()

    @pl.when(j == 0)
    def _():
        pos_copy(0, 0).start()
        pos_copy(0, 0).wait()
        start_gather(0)
        pos_copy(jnp.minimum(1, last), 1).start()

    slot = j % 2
    nslot = 1 - slot
    pos_copy(0, nslot).wait()
    start_gather(nslot)
    pos_copy(jnp.minimum(j + 2, last), slot).start()
    wait_gather(slot)
    w = w_ref[...]
    acc_a = acc_b = None
    for k in range(TOP_K):
        ya, yb = _unpack_rows(ybuf[slot, k])
        acc_a = w[:, k:k + 1] * ya if k == 0 else acc_a + w[:, k:k + 1] * ya
        acc_b = w[:, k:k + 1] * yb if k == 0 else acc_b + w[:, k:k + 1] * yb
    acc = jnp.concatenate([acc_a, acc_b], axis=1)
    x2 = x1_ref[...] + mod_ref[0, 5:6, :] * (acc + sh_ref[...])
    out_ref[...] = x2 * lax.rsqrt(jnp.mean(x2 * x2, axis=-1, keepdims=True) + EPS) * fg_ref[...]

    @pl.when(j == last)
    def _():
        wait_gather(nslot)
        pos_copy(0, slot).wait()


def _combine_call(y_sorted, pos_t, wts, x1, shared, mod, final_g, n_ctx, lat_len):
    T, K = wts.shape
    DH = y_sorted.shape[1]
    D = 2 * DH
    n_tiles = T // TC
    row = functools.partial(_mod_row, tokens_per_tile=TC, n_ctx=n_ctx, lat_len=lat_len)
    return pl.pallas_call(
        _combine_kernel,
        grid=(n_tiles,),
        in_specs=[pl.BlockSpec(memory_space=pl.ANY),
                  pl.BlockSpec(memory_space=pl.ANY),
                  pl.BlockSpec((TC, K), lambda j: (j, 0)),
                  pl.BlockSpec((TC, D), lambda j: (j, 0)),
                  pl.BlockSpec((TC, D), lambda j: (j, 0)),
                  pl.BlockSpec((1, N_MOD, D), lambda j: (row(j), 0, 0)),
                  pl.BlockSpec((1, D), lambda j: (0, 0))],
        out_specs=pl.BlockSpec((TC, D), lambda j: (j, 0)),
        scratch_shapes=[pltpu.VMEM((2, K, TC, DH), jnp.uint32),
                        pltpu.SMEM((2, K * TC), jnp.int32),
                        pltpu.SemaphoreType.DMA((2,)), pltpu.SemaphoreType.DMA((2,))],
        out_shape=jax.ShapeDtypeStruct((T, D), jnp.float32),
        compiler_params=pltpu.CompilerParams(dimension_semantics=("arbitrary",)),
    )(pos_t, y_sorted, wts, x1, shared, mod, final_g.reshape(1, D))


SC_CORES = 2
SC_SUBCORES = 16
SC_CHUNK = 128


def _sc_gather_call(table, idx):
    n_idx = idx.shape[0]
    width = table.shape[1]
    n_workers = SC_CORES * SC_SUBCORES
    per_worker = n_idx // n_workers
    assert per_worker * n_workers == n_idx and per_worker % SC_CHUNK == 0
    mesh = plsc.VectorSubcoreMesh(core_axis_name="c", subcore_axis_name="s")

    def body(table_hbm, idx_hbm, out_hbm, idx_v, rows_v, sem):
        wid = lax.axis_index("s") * SC_CORES + lax.axis_index("c")
        base = wid * per_worker

        @pl.loop(0, per_worker // SC_CHUNK)
        def _(ch):
            off = base + ch * SC_CHUNK
            pltpu.sync_copy(idx_hbm.at[pl.ds(off, SC_CHUNK)], idx_v)
            pltpu.async_copy(table_hbm.at[idx_v], rows_v, sem).wait()
            pltpu.sync_copy(rows_v, out_hbm.at[pl.ds(off, SC_CHUNK)])

    return pl.kernel(
        body, out_type=jax.ShapeDtypeStruct((n_idx, width), table.dtype), mesh=mesh,
        scratch_types=[pltpu.VMEM((SC_CHUNK,), jnp.int32), pltpu.VMEM((SC_CHUNK, width), table.dtype),
                       pltpu.SemaphoreType.DMA],
    )(table, idx)


def _sc_scatter_call(src, pos, n_out):
    n_idx = pos.shape[0]
    width = src.shape[1]
    n_workers = SC_CORES * SC_SUBCORES
    per_worker = n_idx // n_workers
    assert per_worker * n_workers == n_idx and per_worker % SC_CHUNK == 0 and TR % SC_CHUNK == 0
    mesh = plsc.VectorSubcoreMesh(core_axis_name="c", subcore_axis_name="s")
    tile_pairs = TOP_K * TR

    def body(src_hbm, pos_hbm, out_hbm, idx_v, rows_v, sem):
        wid = lax.axis_index("s") * SC_CORES + lax.axis_index("c")
        base = wid * per_worker

        @pl.loop(0, per_worker // SC_CHUNK)
        def _(ch):
            off = base + ch * SC_CHUNK
            row0 = (off // tile_pairs) * TR + off % TR
            pltpu.sync_copy(pos_hbm.at[pl.ds(off, SC_CHUNK)], idx_v)
            pltpu.sync_copy(src_hbm.at[pl.ds(row0, SC_CHUNK)], rows_v)
            pltpu.async_copy(rows_v, out_hbm.at[idx_v], sem).wait()

    return pl.kernel(
        body, out_type=jax.ShapeDtypeStruct((n_out, width), src.dtype), mesh=mesh,
        scratch_types=[pltpu.VMEM((SC_CHUNK,), jnp.int32), pltpu.VMEM((SC_CHUNK, width), src.dtype),
                       pltpu.SemaphoreType.DMA],
    )(src, pos)


def _positions_kernel(idx_ref, rank_ref, pstart_ref, pos_ref):
    erow = lax.broadcasted_iota(jnp.int32, (N_EXPERTS, TR), 0)
    pstart = pstart_ref[...]
    for k in range(TOP_K):
        hit = erow == idx_ref[0, k:k + 1, :]
        seg = jnp.sum(jnp.where(hit, pstart, 0), axis=0, keepdims=True)
        pos_ref[0, k:k + 1, :] = seg + rank_ref[0, k:k + 1, :]


def _positions_call(idx, rank, pad_start):
    n_tiles = idx.shape[0]
    row_spec = pl.BlockSpec((1, TOP_K, TR), lambda i: (i, 0, 0))
    return pl.pallas_call(
        _positions_kernel,
        grid=(n_tiles,),
        in_specs=[row_spec, row_spec, pl.BlockSpec((N_EXPERTS, 1), lambda i: (0, 0))],
        out_specs=row_spec,
        out_shape=jax.ShapeDtypeStruct((n_tiles, TOP_K, TR), jnp.int32),
    )(idx, rank, pad_start.reshape(N_EXPERTS, 1))


def _combine_dense_kernel(g_ref, w_ref, x1_ref, sh_ref, mod_ref, fg_ref, out_ref):
    w = w_ref[...]
    acc_a = acc_b = None
    for k in range(TOP_K):
        ya, yb = _unpack_rows(g_ref[0, k])
        acc_a = w[:, k:k + 1] * ya if k == 0 else acc_a + w[:, k:k + 1] * ya
        acc_b = w[:, k:k + 1] * yb if k == 0 else acc_b + w[:, k:k + 1] * yb
    acc = jnp.concatenate([acc_a, acc_b], axis=1)
    x2 = x1_ref[...] + mod_ref[0, 5:6, :] * (acc + sh_ref[...])
    out_ref[...] = x2 * lax.rsqrt(jnp.mean(x2 * x2, axis=-1, keepdims=True) + EPS) * fg_ref[...]


def _combine_dense_call(gathered, wts, x1, shared, mod, final_g, n_ctx, lat_len):
    T, K = wts.shape
    DH = gathered.shape[-1]
    D = 2 * DH
    row = functools.partial(_mod_row, tokens_per_tile=TC, n_ctx=n_ctx, lat_len=lat_len)
    return pl.pallas_call(
        _combine_dense_kernel,
        grid=(T // TC,),
        in_specs=[pl.BlockSpec((1, K, TC, DH), lambda j: (j, 0, 0, 0)),
                  pl.BlockSpec((TC, K), lambda j: (j, 0)),
                  pl.BlockSpec((TC, D), lambda j: (j, 0)),
                  pl.BlockSpec((TC, D), lambda j: (j, 0)),
                  pl.BlockSpec((1, N_MOD, D), lambda j: (row(j), 0, 0)),
                  pl.BlockSpec((1, D), lambda j: (0, 0))],
        out_specs=pl.BlockSpec((TC, D), lambda j: (j, 0)),
        out_shape=jax.ShapeDtypeStruct((T, D), jnp.float32),
        compiler_params=pltpu.CompilerParams(dimension_semantics=("arbitrary",)),
    )(gathered, wts, x1, shared, mod, final_g.reshape(1, D))


def _moe_routed(h2, h2p, router_w, router_bias, w_gate, w_up, w_down):
    T, D = h2.shape
    idx, rank, w_rows, cnt = _route_call(h2, router_w, router_bias)
    wts = w_rows.transpose(0, 2, 1).reshape(T, TOP_K)
    counts = cnt[:, 0]
    padded = (counts + BM - 1) // BM * BM
    pad_end = jnp.cumsum(padded)
    pad_start = (pad_end - padded).astype(jnp.int32)
    n_pad = T * TOP_K + N_EXPERTS * BM
    n_blk = n_pad // BM
    n_used = (pad_end[-1] // BM).astype(jnp.int32).reshape(1)
    pos = _positions_call(idx, rank, pad_start)
    x_sorted = _sc_scatter_call(h2p, pos.reshape(-1), n_pad)
    blk_row0 = jnp.arange(n_blk, dtype=jnp.int32) * BM
    blk_e = jnp.minimum(jnp.sum((pad_end[None, :] <= blk_row0[:, None]).astype(jnp.int32), axis=1), N_EXPERTS - 1)
    own = blk_e[:, None] == jnp.arange(N_EXPERTS, dtype=jnp.int32)[None, :]
    seg_end = jnp.sum(jnp.where(own, (pad_start + counts)[None, :], 0), axis=1)
    n_valid = jnp.clip(seg_end - blk_row0, 0, BM).astype(jnp.int32)
    y = _expert_call(x_sorted, blk_e, n_valid, n_used, w_gate, w_up, w_down)
    pos_t = pos.reshape(T // TR, TOP_K, TR // TC, TC).transpose(0, 2, 1, 3).reshape(T // TC, TOP_K * TC)
    return y, pos_t, wts


def kernel(x_prompt, x_sample, state_delta, c, c_ctx, w_ada, b_ada, norm1_g, w_in, conv_w, a_log,
           dt_bias, onorm_g, pool_w, pool_scale, w_out, norm2_g, router_w, router_bias, exp_w_gate,
           exp_w_up, exp_w_down, sh_w_gate, sh_w_up, sh_w_down, final_g):
    Bc, Lc, D = x_prompt.shape
    Bl, Ll, _ = x_sample.shape
    n_ctx = Bc * Lc
    assert DEPTH == 1 and 1 + Bl <= MOD_ROWS and n_ctx % Ll == 0
    x_parts = (x_prompt.reshape(n_ctx, D), x_sample.reshape(Bl * Ll, D))
    cvec = jnp.concatenate([c_ctx[None], c, jnp.zeros((MOD_ROWS - 1 - Bl, D), c.dtype)], axis=0)
    l = 0
    mod = _ada_call(cvec, w_ada[l], b_ada[l]).reshape(MOD_ROWS, N_MOD, D)
    qkv, z, ba, u = _inproj_call(*x_parts, mod, norm1_g[l], w_in[l], Ll)
    dn = (conv_w[l], a_log[l], dt_bias[l], onorm_g[l])
    oa_c, st_ctx = _delta_call(qkv, z, ba, *dn, None, Bc, Lc, 0)
    oa_l, _ = _delta_call(qkv, z, ba, *dn, state_delta[:, l], Bl, Ll, n_ctx // Ll)
    op_c = _pool_call(u, pool_w[l], pool_scale[l], False, Bc, Lc, 0)
    op_l = _pool_call(u, pool_w[l], pool_scale[l], True, Bl, Ll, n_ctx // Ll)
    x1, h2, h2p, shared = _outproj_call(x_parts, (oa_c, oa_l), (op_c, op_l), mod, norm2_g[l], w_out[l],
                                        sh_w_gate[l], sh_w_up[l], sh_w_down[l], Ll)
    y, pos_t, wts = _moe_routed(h2, h2p, router_w[l], router_bias[l], exp_w_gate[l], exp_w_up[l],
                                exp_w_down[l])
    T = n_ctx + Bl * Ll
    gathered = _sc_gather_call(y, pos_t.reshape(-1))
    out = _combine_dense_call(gathered.reshape(T // TC, TOP_K, TC, D // 2), wts, x1, shared, mod, final_g,
                              n_ctx, Ll)
    y_prompt = out[:n_ctx].reshape(Bc, Lc, D)
    y_sample = out[n_ctx:].reshape(Bl, Ll, D)
    new_state_delta = st_ctx[:, None].astype(x_prompt.dtype)
    return (y_prompt, y_sample, new_state_delta)
```

```python
import functools
import jax, jax.numpy as jnp
from jax import lax
from jax.experimental import pallas as pl
from jax.experimental.pallas import tpu as pltpu
from jax.experimental.pallas import tpu_sc as plsc

D_MODEL = 1024
DEPTH = 1
GRID_W = 64
D_MIX = D_MODEL
D_A = D_MIX // 2
D_P = D_MIX - D_A
H_A = 4
DK = D_A // H_A
DV = D_A // H_A
CONV_K = 5
CHUNK = 64
POOL_WINDOWS = (2, 4, 8, 16)
N_PG = len(POOL_WINDOWS)
PG = D_P // N_PG
N_EXPERTS = 256
TOP_K = 8
N_GROUPS = 8
TOPK_GROUP = 4
ROUTED_SCALE = 2.5
EPS = 1e-6
VMEM_LIMIT = 48 * 1024 * 1024


def _split_bf16(a):
    hi = a.astype(jnp.bfloat16)
    return hi, (a - hi.astype(jnp.float32)).astype(jnp.bfloat16)


def _bdot(a, b):
    return jnp.dot(a, b, preferred_element_type=jnp.float32)


def _pack_rows(x):
    m = x.shape[1] // 2
    hi = lax.bitcast_convert_type(x[:, :m].astype(jnp.bfloat16).astype(jnp.float32), jnp.uint32)
    lo = lax.bitcast_convert_type(x[:, m:].astype(jnp.bfloat16).astype(jnp.float32), jnp.uint32)
    return lax.bitcast_convert_type(hi | (lo >> 16), jnp.int32)


def _unpack_rows(p):
    p = lax.bitcast_convert_type(p, jnp.uint32)
    hi = lax.bitcast_convert_type(p & jnp.uint32(0xFFFF0000), jnp.float32)
    lo = lax.bitcast_convert_type(p << 16, jnp.float32)
    return hi, lo


N_MOD = 6
MOD_ROWS = 8
TM = 512


def _ada_kernel(c_ref, w_ref, b_ref, o_ref):
    c = c_ref[...]
    s = c * jax.nn.sigmoid(c)
    sh, sl = _split_bf16(s)
    wh, wl = _split_bf16(w_ref[...])
    o_ref[...] = _bdot(sh, wh) + (_bdot(sh, wl) + _bdot(sl, wh)) + b_ref[...]


def _ada_call(cvec, w_ada, b_ada):
    R, D = cvec.shape
    N = w_ada.shape[1]
    tn = 1024
    return pl.pallas_call(
        _ada_kernel,
        grid=(N // tn,),
        in_specs=[pl.BlockSpec((R, D), lambda j: (0, 0)),
                  pl.BlockSpec((D, tn), lambda j: (0, j)),
                  pl.BlockSpec((1, tn), lambda j: (0, j))],
        out_specs=pl.BlockSpec((R, tn), lambda j: (0, j)),
        out_shape=jax.ShapeDtypeStruct((R, N), jnp.float32),
    )(cvec, w_ada, b_ada.reshape(1, N))


def _mod_row(tile, tokens_per_tile, n_ctx, lat_len):
    t0 = tile * tokens_per_tile
    return jnp.where(t0 < n_ctx, 0, 1 + (t0 - n_ctx) // lat_len)


def _two_part_specs(n_ctx_tiles, width):
    return (pl.BlockSpec((TM, width), lambda i: (jnp.minimum(i, n_ctx_tiles - 1), 0)),
            pl.BlockSpec((TM, width), lambda i: (jnp.maximum(i - n_ctx_tiles, 0), 0)))


def _pick(n_ctx_tiles, ctx_ref, lat_ref):
    return jnp.where(pl.program_id(0) < n_ctx_tiles, ctx_ref[...], lat_ref[...])


def _inproj_kernel(xc_ref, xl_ref, mod_ref, g_ref, wq_ref, wz_ref, wb_ref, wu_ref, q_ref, z_ref, b_ref, u_ref,
                   *, n_ctx_tiles):
    x = _pick(n_ctx_tiles, xc_ref, xl_ref)
    y = x * lax.rsqrt(jnp.mean(x * x, axis=-1, keepdims=True) + EPS) * g_ref[...]
    h = (y * (1.0 + mod_ref[0, 1:2, :]) + mod_ref[0, 0:1, :]).astype(jnp.bfloat16)
    q_ref[...] = _bdot(h, wq_ref[...])
    z_ref[...] = _bdot(h, wz_ref[...])
    b_ref[...] = _bdot(h, wb_ref[...])
    u_ref[...] = _bdot(h, wu_ref[...])


def _inproj_call(x_ctx, x_lat, mod, norm1_g, w_in, lat_len):
    n_ctx, D = x_ctx.shape
    T = n_ctx + x_lat.shape[0]
    bf = jnp.bfloat16
    nq, nz, nb = 3 * D_A, D_A, 4 * H_A
    wq = w_in[:, :nq].astype(bf)
    wz = w_in[:, nq:nq + nz].astype(bf)
    wb = jnp.pad(w_in[:, nq + nz:nq + nz + nb], ((0, 0), (0, 128 - nb))).astype(bf)
    wu = w_in[:, nq + nz + nb:].astype(bf)
    row = functools.partial(_mod_row, tokens_per_tile=TM, n_ctx=n_ctx, lat_len=lat_len)

    def full(a):
        return pl.BlockSpec(a.shape, lambda i: (0, 0))

    def rows(n):
        return pl.BlockSpec((TM, n), lambda i: (i, 0))

    return pl.pallas_call(
        functools.partial(_inproj_kernel, n_ctx_tiles=n_ctx // TM),
        grid=(T // TM,),
        in_specs=[*_two_part_specs(n_ctx // TM, D), pl.BlockSpec((1, N_MOD, D), lambda i: (row(i), 0, 0)),
                  pl.BlockSpec((1, D), lambda i: (0, 0)), full(wq), full(wz), full(wb), full(wu)],
        out_specs=[rows(nq), rows(nz), rows(128), rows(D_P)],
        out_shape=[jax.ShapeDtypeStruct((T, nq), jnp.float32), jax.ShapeDtypeStruct((T, nz), jnp.float32),
                   jax.ShapeDtypeStruct((T, 128), jnp.float32), jax.ShapeDtypeStruct((T, D_P), jnp.float32)],
        compiler_params=pltpu.CompilerParams(dimension_semantics=("arbitrary",),
                                             vmem_limit_bytes=VMEM_LIMIT),
    )(x_ctx, x_lat, mod, norm1_g.reshape(1, D), wq, wz, wb, wu)


PT = 256


def _window_bounds(pos, w, n):
    return jnp.maximum(pos - w // 2, 0), jnp.minimum(pos + w - w // 2, n)


def _band_sum(band, x):
    xh, xl = _split_bf16(x)
    return _bdot(band, xh) + _bdot(band, xl)


def _pool_seq_kernel(u_ref, pw_ref, ps_ref, o_ref):
    L = u_ref.shape[0]
    ti = lax.broadcasted_iota(jnp.int32, (L, L), 0)
    ji = lax.broadcasted_iota(jnp.int32, (L, L), 1)
    tcol = lax.broadcasted_iota(jnp.int32, (L, 1), 0)
    for i, w in enumerate(POOL_WINDOWS):
        lo, hi = _window_bounds(ti, w, L)
        band = ((ji >= lo) & (ji < hi)).astype(jnp.bfloat16)
        clo, chi = _window_bounds(tcol, w, L)
        ug = u_ref[:, i * PG:(i + 1) * PG]
        mean = _band_sum(band, ug) / (chi - clo).astype(jnp.float32)
        d = (mean - ug).astype(jnp.bfloat16)
        o_ref[:, i * PG:(i + 1) * PG] = _bdot(d, pw_ref[i]) * ps_ref[:, i * PG:(i + 1) * PG]


def _pool_grid_kernel(u_ref, pw_ref, ps_ref, o_ref, pad_s, r_s):
    L = u_ref.shape[0]
    rows = L // GRID_W
    halo = (max(POOL_WINDOWS) // 2) * GRID_W
    pad_s[0:halo, :] = jnp.zeros((halo, D_P), jnp.float32)
    pad_s[halo + L:, :] = jnp.zeros((halo, D_P), jnp.float32)
    pad_s[halo:halo + L, :] = u_ref[...]
    ti = lax.broadcasted_iota(jnp.int32, (PT, PT), 0)
    ji = lax.broadcasted_iota(jnp.int32, (PT, PT), 1)
    tcol = lax.broadcasted_iota(jnp.int32, (PT, 1), 0)
    for i, w in enumerate(POOL_WINDOWS):
        cs = slice(i * PG, (i + 1) * PG)
        acc = None
        for dr in range(-(w // 2), w - w // 2):
            part = pad_s[halo + dr * GRID_W:halo + dr * GRID_W + L, cs]
            acc = part if acc is None else acc + part
        r_s[...] = acc
        lo, hi = _window_bounds(ti % GRID_W, w, GRID_W)
        band = ((ji // GRID_W == ti // GRID_W) & (ji % GRID_W >= lo) & (ji % GRID_W < hi)).astype(jnp.bfloat16)
        clo, chi = _window_bounds(tcol % GRID_W, w, GRID_W)
        ccnt = (chi - clo).astype(jnp.float32)
        for tile in range(L // PT):
            ts = slice(tile * PT, (tile + 1) * PT)
            rlo, rhi = _window_bounds(tile * (PT // GRID_W) + tcol // GRID_W, w, rows)
            mean = _band_sum(band, r_s[ts, :]) / ((rhi - rlo).astype(jnp.float32) * ccnt)
            d = (mean - u_ref[ts, cs]).astype(jnp.bfloat16)
            o_ref[ts, cs] = _bdot(d, pw_ref[i]) * ps_ref[:, cs]


def _pool_call(u, pool_w, pool_scale, grid, B, L, row_blk0):
    pw = pool_w.astype(jnp.bfloat16)
    ps = pool_scale.reshape(1, D_P)
    specs = dict(
        grid=(B,),
        in_specs=[pl.BlockSpec((L, D_P), lambda b: (row_blk0 + b, 0)),
                  pl.BlockSpec((N_PG, PG, PG), lambda b: (0, 0, 0)),
                  pl.BlockSpec((1, D_P), lambda b: (0, 0))],
        out_specs=pl.BlockSpec((L, D_P), lambda b: (b, 0)),
        out_shape=jax.ShapeDtypeStruct((B * L, D_P), jnp.float32),
        compiler_params=pltpu.CompilerParams(dimension_semantics=("arbitrary",),
                                             vmem_limit_bytes=VMEM_LIMIT))
    if not grid:
        return pl.pallas_call(_pool_seq_kernel, **specs)(u, pw, ps)
    halo = (max(POOL_WINDOWS) // 2) * GRID_W
    return pl.pallas_call(
        _pool_grid_kernel,
        scratch_shapes=[pltpu.VMEM((L + 2 * halo, D_P), jnp.float32), pltpu.VMEM((L, PG), jnp.float32)],
        **specs)(u, pw, ps)


def _outproj_kernel(xc_ref, xl_ref, oac_ref, oal_ref, opc_ref, opl_ref, mod_ref, g2_ref, wo_ref, sg_ref, su_ref,
                    sd_ref, x1_ref, h2_ref, h2p_ref, sh_ref, *, n_ctx_tiles):
    o_a = _pick(n_ctx_tiles, oac_ref, oal_ref)
    o_p = _pick(n_ctx_tiles, opc_ref, opl_ref)
    mix = (_bdot(o_a.astype(jnp.bfloat16), wo_ref[:D_A, :])
           + _bdot(o_p.astype(jnp.bfloat16), wo_ref[D_A:, :]))
    x1 = _pick(n_ctx_tiles, xc_ref, xl_ref) + mod_ref[0, 2:3, :] * mix
    x1_ref[...] = x1
    y = x1 * lax.rsqrt(jnp.mean(x1 * x1, axis=-1, keepdims=True) + EPS) * g2_ref[...]
    h2 = y * (1.0 + mod_ref[0, 4:5, :]) + mod_ref[0, 3:4, :]
    h2_ref[...] = h2
    h2p_ref[...] = _pack_rows(h2)
    hb = h2.astype(jnp.bfloat16)
    g = _bdot(hb, sg_ref[...])
    a = (g * jax.nn.sigmoid(g)) * _bdot(hb, su_ref[...])
    sh_ref[...] = _bdot(a.astype(jnp.bfloat16), sd_ref[...])


def _outproj_call(x_parts, oa_parts, op_parts, mod, norm2_g, w_out, sh_gate, sh_up, sh_down, lat_len):
    n_ctx, D = x_parts[0].shape
    T = n_ctx + x_parts[1].shape[0]
    nct = n_ctx // TM
    bf = jnp.bfloat16
    row = functools.partial(_mod_row, tokens_per_tile=TM, n_ctx=n_ctx, lat_len=lat_len)
    ws = [w_out.astype(bf), sh_gate.astype(bf), sh_up.astype(bf), sh_down.astype(bf)]

    def rows(n):
        return pl.BlockSpec((TM, n), lambda i: (i, 0))

    return pl.pallas_call(
        functools.partial(_outproj_kernel, n_ctx_tiles=nct),
        grid=(T // TM,),
        in_specs=[*_two_part_specs(nct, D), *_two_part_specs(nct, D_A), *_two_part_specs(nct, D_P),
                  pl.BlockSpec((1, N_MOD, D), lambda i: (row(i), 0, 0)),
                  pl.BlockSpec((1, D), lambda i: (0, 0))] + [pl.BlockSpec(w.shape, lambda i: (0, 0)) for w in ws],
        out_specs=[rows(D), rows(D), rows(D // 2), rows(D)],
        out_shape=[jax.ShapeDtypeStruct((T, D), jnp.float32), jax.ShapeDtypeStruct((T, D), jnp.float32),
                   jax.ShapeDtypeStruct((T, D // 2), jnp.int32), jax.ShapeDtypeStruct((T, D), jnp.float32)],
        compiler_params=pltpu.CompilerParams(dimension_semantics=("arbitrary",),
                                             vmem_limit_bytes=VMEM_LIMIT),
    )(*x_parts, *oa_parts, *op_parts, mod, norm2_g.reshape(1, D), *ws)


SC = 256
CPS = SC // CHUNK
BASE = 16
DELTA_HEAD_ROWS = 4096


def _mm(a, b):
    return jnp.dot(a.astype(jnp.bfloat16), b.astype(jnp.bfloat16), preferred_element_type=jnp.float32)


def _mm_nt(a, b):
    return lax.dot_general(a.astype(jnp.bfloat16), b.astype(jnp.bfloat16), (((1,), (1,)), ((), ())),
                           preferred_element_type=jnp.float32)


def _softplus(x):
    return jnp.maximum(x, 0.0) + jnp.log(1.0 + jnp.exp(-jnp.abs(x)))


def _delta_kernel(sc_ref, xq_ref, xk_ref, xv_ref, z_ref, bac_ref, bar_ref, cwq_ref, cwk_ref, cwv_ref,
                  og_ref, s0_ref, o_ref, st_ref, q_s, k_s, v_s, o_s, vn_s, *, n_sc, zero_init, hpb):
    hb = pl.program_id(1)
    L = q_s.shape[1]

    def conv(x_ref, w_ref, cs):
        x = x_ref[:, cs]
        row = lax.broadcasted_iota(jnp.int32, x.shape, 0)
        acc = x * w_ref[CONV_K // 2:CONV_K // 2 + 1, cs]
        for j in range(CONV_K):
            d = j - CONV_K // 2
            if d == 0:
                continue
            xs = pltpu.roll(x, (-d) % L, 0)
            ok = (row + d >= 0) & (row + d < L)
            acc = acc + jnp.where(ok, xs, 0.0) * w_ref[j:j + 1, cs]
        return acc * jax.nn.sigmoid(acc)

    for hh in range(hpb):
        cs = slice(hh * DK, (hh + 1) * DK)
        q = conv(xq_ref, cwq_ref, cs)
        q_s[hh] = q * lax.rsqrt(jnp.sum(q * q, axis=-1, keepdims=True) + EPS) * (DK ** -0.5)
        k = conv(xk_ref, cwk_ref, cs)
        k_s[hh] = k * lax.rsqrt(jnp.sum(k * k, axis=-1, keepdims=True) + EPS)
        v_s[hh] = conv(xv_ref, cwv_ref, cs)
    o_s[...] = jnp.zeros_like(o_s)

    ri = lax.broadcasted_iota(jnp.int32, (SC, SC), 0)
    ci = lax.broadcasted_iota(jnp.int32, (SC, SC), 1)
    same = (ri // CHUNK) == (ci // CHUNK)
    same_base = (ri // BASE) == (ci // BASE)
    merge_masks = [(ri // w) == (ci // w) for w in (2 * BASE, CHUNK)]
    eye = (ri == ci).astype(jnp.float32)
    rowi = lax.broadcasted_iota(jnp.int32, (SC, DV), 0)

    def prep(m, d, hh):
        r0 = pl.multiple_of(m * SC, SC)
        h = hb * hpb + hh
        q = q_s[hh, pl.ds(r0, SC), :]
        k = k_s[hh, pl.ds(r0, SC), :]
        v = v_s[hh, pl.ds(r0, SC), :]
        bc = bac_ref[0, hh, pl.ds(r0, SC), :]
        br = bar_ref[0, hh, m]
        a_l = sc_ref[d * H_A + h]
        dtb = sc_ref[2 * H_A + d * H_A + h]
        neg_ea = -jnp.exp(jnp.full((1, 1), a_l, jnp.float32))
        beta = jax.nn.sigmoid(bc[:, d:d + 1])
        g_col = neg_ea * _softplus(bc[:, 2 + d:3 + d] + dtb)
        g_row = neg_ea * _softplus(br[2 + d:3 + d, :] + dtb)
        if d == 0:
            tri, strict = same & (ci <= ri), same & (ci < ri)
        else:
            tri, strict = same & (ci >= ri), same & (ci > ri)
        tri_t = same & (ri <= ci) if d == 0 else same & (ri >= ci)
        gc_col = jnp.sum(jnp.where(tri, g_row, 0.0), axis=1, keepdims=True)
        gc_row = jnp.sum(jnp.where(tri_t, g_col, 0.0), axis=0, keepdims=True)
        gl_col = jnp.sum(jnp.where(same, g_row, 0.0), axis=1, keepdims=True)
        decay = jnp.where(tri, jnp.exp(jnp.where(tri, gc_col - gc_row, 0.0)), 0.0)
        kb = k * beta
        a = jnp.where(strict, _mm_nt(kb, k) * decay, 0.0)
        attn = jnp.where(tri, _mm_nt(q, k) * decay, 0.0)
        eg = jnp.exp(gc_col)
        x = jnp.concatenate([v * beta, kb * eg], axis=1)
        qd = q * eg
        kdt = (k * jnp.exp(gl_col - gc_col)).T
        return dict(r0=r0, a=a, attn=attn, x=x, qd=qd, kdt=kdt, egl=jnp.exp(gl_col))

    def run_chains(ms, states):
        n = len(chains)
        ops = [prep(ms[i], d, hh) for i, (hh, d) in enumerate(chains)]
        ps = [jnp.where(same_base, o["a"], 0.0) for o in ops]
        ts = [eye - p for p in ps]
        for _ in range(BASE.bit_length() - 2):
            ps = [_mm(p, p) for p in ps]
            ts = [t + _mm(t, p) for t, p in zip(ts, ps)]
        inner = same_base
        for outer in merge_masks:
            lows = [_mm(jnp.where(outer & ~inner, o["a"], 0.0), t) for o, t in zip(ops, ts)]
            ts = [t - _mm(t, low) for t, low in zip(ts, lows)]
            inner = outer
        xs = [_mm(t, o["x"]) for t, o in zip(ts, ops)]
        for i in range(n):
            vn_s[i] = jnp.zeros((SC, DV), jnp.float32)
        states = list(states)
        for step in range(CPS):
            cs = [step if d == 0 else CPS - 1 - step for _, d in chains]
            los = [c * CHUNK for c in cs]
            ws_qs = [_mm(jnp.concatenate([x[lo:lo + CHUNK, DV:], o["qd"][lo:lo + CHUNK]], axis=0), s)
                     for x, o, lo, s in zip(xs, ops, los, states)]
            for i in range(n):
                vn_s[i, los[i]:los[i] + CHUNK, :] = xs[i][los[i]:los[i] + CHUNK, :DV] - ws_qs[i][:CHUNK]
            vns = [vn_s[i] for i in range(n)]
            o_cs = [wq[CHUNK:] + _mm(o["attn"][lo:lo + CHUNK, :], vn)
                    for wq, o, lo, vn in zip(ws_qs, ops, los, vns)]
            for i, (hh, _) in enumerate(chains):
                o_s[hh, pl.ds(ops[i]["r0"] + los[i], CHUNK), :] += o_cs[i]
            states = [s * o["egl"][lo:lo + 1, :]
                      + _mm(o["kdt"], jnp.where((rowi >= lo) & (rowi < lo + CHUNK), vn, 0.0))
                      for s, o, lo, vn in zip(states, ops, los, vns)]
        return tuple(states)

    if zero_init:
        states = tuple(jnp.zeros((DK, DV), jnp.float32) for _ in range(2 * hpb))
    else:
        states = tuple(s0_ref[0, d, hh] for hh in range(hpb) for d in range(2))

    chains = [(hh, d) for hh in range(hpb) for d in range(2)]

    def body(m, carry):
        return run_chains([m if d == 0 else n_sc - 1 - m for _, d in chains], carry)

    if n_sc == 1:
        states = body(0, states)
    else:
        states = lax.fori_loop(0, n_sc, body, states)

    for hh in range(hpb):
        for d in range(2):
            st_ref[0, d, hh] = states[2 * hh + d]
        o = o_s[hh]
        o = o * lax.rsqrt(jnp.mean(o * o, axis=-1, keepdims=True) + EPS) * og_ref[...]
        zz = z_ref[:, hh * DV:(hh + 1) * DV]
        o_ref[:, hh * DV:(hh + 1) * DV] = o * (zz * jax.nn.sigmoid(zz))


def _delta_call(qkv, z, ba, conv_w, a_log, dt_bias, onorm_g, s0, B, L, row_blk0):
    n_sc = L // SC
    t0 = row_blk0 * L
    bah = ba[t0:t0 + B * L, :4 * H_A].reshape(B, L, 4, H_A).transpose(0, 3, 1, 2)
    bar = bah.reshape(B, H_A, n_sc, SC, 4).transpose(0, 1, 2, 4, 3)
    scal = jnp.concatenate([a_log.reshape(-1), dt_bias.reshape(-1)]).astype(jnp.float32)
    hpb = max(1, min(H_A, DELTA_HEAD_ROWS // L))
    n_hb = H_A // hpb
    zero_init = s0 is None
    if zero_init:
        s0 = jnp.zeros((1, 2, hpb, DK, DV), jnp.float32)
        s0_spec = pl.BlockSpec((1, 2, hpb, DK, DV), lambda b, h, sc: (0, 0, 0, 0, 0))
    else:
        s0_spec = pl.BlockSpec((1, 2, hpb, DK, DV), lambda b, h, sc: (b, 0, h, 0, 0))

    def col(off):
        return pl.BlockSpec((L, hpb * DK), lambda b, h, sc: (row_blk0 + b, off * n_hb + h))

    def cw(off):
        return pl.BlockSpec((CONV_K, hpb * DK), lambda b, h, sc: (0, off * n_hb + h))

    kern = functools.partial(_delta_kernel, n_sc=n_sc, zero_init=zero_init, hpb=hpb)
    return pl.pallas_call(
        kern,
        grid_spec=pltpu.PrefetchScalarGridSpec(
            num_scalar_prefetch=1,
            grid=(B, n_hb),
            in_specs=[col(0), col(1), col(2),
                      pl.BlockSpec((L, hpb * DV), lambda b, h, sc: (row_blk0 + b, h)),
                      pl.BlockSpec((1, hpb, L, 4), lambda b, h, sc: (b, h, 0, 0)),
                      pl.BlockSpec((1, hpb, n_sc, 4, SC), lambda b, h, sc: (b, h, 0, 0, 0)),
                      cw(0), cw(1), cw(2),
                      pl.BlockSpec((1, DV), lambda b, h, sc: (0, 0)),
                      s0_spec],
            out_specs=[pl.BlockSpec((L, hpb * DV), lambda b, h, sc: (b, h)),
                       pl.BlockSpec((1, 2, hpb, DK, DV), lambda b, h, sc: (b, 0, h, 0, 0))],
            scratch_shapes=[pltpu.VMEM((hpb, L, DK), jnp.float32), pltpu.VMEM((hpb, L, DK), jnp.float32),
                            pltpu.VMEM((hpb, L, DV), jnp.float32), pltpu.VMEM((hpb, L, DV), jnp.float32),
                            pltpu.VMEM((2 * hpb, SC, DV), jnp.float32)]),
        out_shape=[jax.ShapeDtypeStruct((B * L, D_A), jnp.float32),
                   jax.ShapeDtypeStruct((B, 2, H_A, DK, DV), jnp.float32)],
        compiler_params=pltpu.CompilerParams(dimension_semantics=("arbitrary", "arbitrary"),
                                             vmem_limit_bytes=VMEM_LIMIT),
    )(scal, qkv, qkv, qkv, z, bah, bar, conv_w, conv_w, conv_w, onorm_g.reshape(1, DV), s0)


TR = 256
GSZ = N_EXPERTS // N_GROUPS
NEG = -jnp.inf
BM = 1024
SUBLANES = 8
FILL_PIECES = tuple(p for p in (BM >> s for s in range(1, BM.bit_length())) if p >= SUBLANES)


def _route_kernel(h_ref, rwh_ref, rwl_ref, rb_ref, idx_ref, rank_ref, w_ref, cnt_ref, cnt_s):
    i = pl.program_id(0)

    @pl.when(i == 0)
    def _():
        cnt_s[...] = jnp.zeros_like(cnt_s)

    h = h_ref[...]
    hh, hl = _split_bf16(h)
    logits = _bdot(hh, rwh_ref[...]) + (_bdot(hh, rwl_ref[...]) + _bdot(hl, rwh_ref[...]))
    scores = jax.nn.sigmoid(logits.T)
    sel = scores + rb_ref[...]
    erow = lax.broadcasted_iota(jnp.int32, sel.shape, 0)
    grow = lax.broadcasted_iota(jnp.int32, (GSZ, TR), 0)

    def first_argmax(v, rows):
        m = jnp.max(v, axis=0, keepdims=True)
        first = jnp.min(jnp.where(v == m, rows, N_EXPERTS), axis=0, keepdims=True)
        return m, first

    gs = []
    for g in range(N_GROUPS):
        vg = sel[g * GSZ:(g + 1) * GSZ, :]
        m1, i1 = first_argmax(vg, grow)
        m2 = jnp.max(jnp.where(grow == i1, NEG, vg), axis=0, keepdims=True)
        gs.append(m1 + m2)
    cand = []
    for g in range(N_GROUPS):
        beat = jnp.zeros(gs[g].shape, jnp.int32)
        for o in range(N_GROUPS):
            if o == g:
                continue
            wins = (gs[o] > gs[g]) | ((gs[o] == gs[g]) & (o < g))
            beat = beat + wins.astype(jnp.int32)
        cand.append(jnp.where(beat < TOPK_GROUP, sel[g * GSZ:(g + 1) * GSZ, :], NEG))
    cand = jnp.concatenate(cand, axis=0)
    chosen = []
    picked = jnp.zeros(sel.shape, jnp.bool_)
    for _ in range(TOP_K):
        _, ik = first_argmax(cand, erow)
        hit = erow == ik
        chosen.append((ik, hit))
        picked = picked | hit
        cand = jnp.where(hit, NEG, cand)
    wsum = jnp.sum(jnp.where(picked, scores, 0.0), axis=0, keepdims=True)

    ri = lax.broadcasted_iota(jnp.int32, (TR, TR), 0)
    ci = lax.broadcasted_iota(jnp.int32, (TR, TR), 1)
    earlier = (ri < ci).astype(jnp.bfloat16)
    rank_mat = _bdot(picked.astype(jnp.bfloat16), earlier) + cnt_s[...]
    cnt_s[...] = cnt_s[...] + jnp.sum(picked.astype(jnp.float32), axis=1, keepdims=True)
    cnt_ref[...] = cnt_s[...].astype(jnp.int32)

    for k, (ik, hit) in enumerate(chosen):
        idx_ref[0, k:k + 1, :] = ik
        rank_ref[0, k:k + 1, :] = jnp.sum(jnp.where(hit, rank_mat, 0.0), axis=0, keepdims=True).astype(jnp.int32)
        w_ref[0, k:k + 1, :] = jnp.sum(jnp.where(hit, scores, 0.0), axis=0, keepdims=True) / wsum * ROUTED_SCALE


def _route_call(hf, router_w, router_bias):
    T, D = hf.shape
    n_tiles = T // TR
    rwh, rwl = _split_bf16(router_w)
    row_spec = pl.BlockSpec((1, TOP_K, TR), lambda i: (i, 0, 0))
    return pl.pallas_call(
        _route_kernel,
        grid=(n_tiles,),
        in_specs=[pl.BlockSpec((TR, D), lambda i: (i, 0)),
                  pl.BlockSpec((D, N_EXPERTS), lambda i: (0, 0)),
                  pl.BlockSpec((D, N_EXPERTS), lambda i: (0, 0)),
                  pl.BlockSpec((N_EXPERTS, 1), lambda i: (0, 0))],
        out_specs=[row_spec, row_spec, row_spec, pl.BlockSpec((N_EXPERTS, 1), lambda i: (0, 0))],
        scratch_shapes=[pltpu.VMEM((N_EXPERTS, 1), jnp.float32)],
        out_shape=[jax.ShapeDtypeStruct((n_tiles, TOP_K, TR), jnp.int32),
                   jax.ShapeDtypeStruct((n_tiles, TOP_K, TR), jnp.int32),
                   jax.ShapeDtypeStruct((n_tiles, TOP_K, TR), jnp.float32),
                   jax.ShapeDtypeStruct((N_EXPERTS, 1), jnp.int32)],
        compiler_params=pltpu.CompilerParams(dimension_semantics=("arbitrary",)),
    )(hf, rwh, rwl, router_bias.reshape(N_EXPERTS, 1).astype(jnp.float32))


def _dispatch_kernel(fill_ref, idx_ref, rank_ref, pstart_ref, h_ref, pos_ref, xs_hbm, pos_v, pos_s, zbuf,
                     ssem, psem, zsem):
    n_blk = xs_hbm.shape[0] // BM

    @pl.when(pl.program_id(0) == 0)
    def _():
        zbuf[...] = jnp.zeros_like(zbuf)

        def pad_copies(e, act):
            n = fill_ref[N_EXPERTS + e]
            start = fill_ref[e]
            head = jnp.minimum((-start) & (SUBLANES - 1), n)
            for j in range(SUBLANES - 1):
                @pl.when(j < head)
                def _(j=j):
                    act(pltpu.make_async_copy(zbuf.at[0], xs_hbm.at[start + j], zsem))
            off = start + head
            rest = n - head
            for piece in FILL_PIECES:
                @pl.when((rest & piece) != 0)
                def _(off=off, piece=piece):
                    dst = xs_hbm.at[pl.ds(pl.multiple_of(off, SUBLANES), piece)]
                    act(pltpu.make_async_copy(zbuf.at[pl.ds(0, piece)], dst, zsem))
                off = off + (rest & piece)

        def tail_copy(b, act):
            act(pltpu.make_async_copy(zbuf, xs_hbm.at[pl.ds(pl.multiple_of(b * BM, BM), BM)], zsem))

        for act in (lambda c: c.start(), lambda c: c.wait()):
            def per_expert(e, carry, act=act):
                pad_copies(e, act)
                return carry

            def per_block(b, carry, act=act):
                tail_copy(b, act)
                return carry

            lax.fori_loop(0, N_EXPERTS, per_expert, 0)
            lax.fori_loop(fill_ref[2 * N_EXPERTS], n_blk, per_block, 0)

    erow = lax.broadcasted_iota(jnp.int32, (N_EXPERTS, TR), 0)
    pstart = pstart_ref[...]
    for k in range(TOP_K):
        hit = erow == idx_ref[0, k:k + 1, :]
        seg = jnp.sum(jnp.where(hit, pstart, 0), axis=0, keepdims=True)
        pos_v[k:k + 1, :] = seg + rank_ref[0, k:k + 1, :]
    pos_ref[0] = pos_v[...]
    cp = pltpu.make_async_copy(pos_v, pos_s, psem)
    cp.start()
    cp.wait()

    def body(t, carry):
        for k in range(TOP_K):
            pltpu.make_async_copy(h_ref.at[t], xs_hbm.at[pos_s[k, t]], ssem).start()
        return carry

    lax.fori_loop(0, TR, body, 0, unroll=8)
    n_rows = TR * TOP_K
    pltpu.make_async_copy(xs_hbm.at[pl.ds(0, n_rows)], xs_hbm.at[pl.ds(0, n_rows)], ssem).wait()


def _dispatch_call(hf, idx, rank, pad_start, fill_tab, n_pad):
    T, D = hf.shape
    n_tiles = T // TR
    row_spec = pl.BlockSpec((1, TOP_K, TR), lambda i, ft: (i, 0, 0))
    return pl.pallas_call(
        _dispatch_kernel,
        grid_spec=pltpu.PrefetchScalarGridSpec(
            num_scalar_prefetch=1,
            grid=(n_tiles,),
            in_specs=[row_spec, row_spec,
                      pl.BlockSpec((N_EXPERTS, 1), lambda i, ft: (0, 0)),
                      pl.BlockSpec((TR, D), lambda i, ft: (i, 0))],
            out_specs=[row_spec, pl.BlockSpec(memory_space=pl.ANY)],
            scratch_shapes=[pltpu.VMEM((TOP_K, TR), jnp.int32), pltpu.SMEM((TOP_K, TR), jnp.int32),
                            pltpu.VMEM((BM, D), hf.dtype),
                            pltpu.SemaphoreType.DMA, pltpu.SemaphoreType.DMA, pltpu.SemaphoreType.DMA]),
        out_shape=[jax.ShapeDtypeStruct((n_tiles, TOP_K, TR), jnp.int32),
                   jax.ShapeDtypeStruct((n_pad, D), hf.dtype)],
        compiler_params=pltpu.CompilerParams(dimension_semantics=("arbitrary",)),
    )(fill_tab, idx, rank, pad_start.reshape(N_EXPERTS, 1), hf)


def _expert_kernel(blk_e_ref, nvalid_ref, nused_ref, x_ref, wg_ref, wu_ref, wd_ref, y_ref, wg_s, wu_s, wd_s):
    i = pl.program_id(0)

    @pl.when(i < nused_ref[0])
    def _():
        e = blk_e_ref[i]
        prev = blk_e_ref[jnp.maximum(i - 1, 0)]

        @pl.when((i == 0) | (e != prev))
        def _():
            wg_s[...] = wg_ref[0].astype(jnp.bfloat16)
            wu_s[...] = wu_ref[0].astype(jnp.bfloat16)
            wd_s[...] = wd_ref[0].astype(jnp.bfloat16)

        row = lax.broadcasted_iota(jnp.int32, (BM, 1), 0)
        xa, xb = _unpack_rows(jnp.where(row < nvalid_ref[i], x_ref[...], 0))
        xa = xa.astype(jnp.bfloat16)
        xb = xb.astype(jnp.bfloat16)
        half = xa.shape[1]
        g = _bdot(xa, wg_s[:half, :]) + _bdot(xb, wg_s[half:, :])
        u = _bdot(xa, wu_s[:half, :]) + _bdot(xb, wu_s[half:, :])
        a = (g * jax.nn.sigmoid(g)) * u
        y_ref[...] = _pack_rows(_bdot(a.astype(jnp.bfloat16), wd_s[...]))

    @pl.when(i >= nused_ref[0])
    def _():
        y_ref[...] = jnp.zeros_like(y_ref)


def _expert_call(x_sorted, blk_e, n_valid, n_used, w_gate, w_up, w_down):
    n_pad, DH = x_sorted.shape
    n_blk = n_pad // BM
    E, D, F = w_gate.shape

    def row_map(i, be, nv, nu):
        return (jnp.minimum(i, nu[0] - 1), 0)

    def w_map(i, be, nv, nu):
        return (be[jnp.minimum(i, nu[0] - 1)], 0, 0)

    return pl.pallas_call(
        _expert_kernel,
        grid_spec=pltpu.PrefetchScalarGridSpec(
            num_scalar_prefetch=3,
            grid=(n_blk,),
            in_specs=[pl.BlockSpec((BM, DH), row_map),
                      pl.BlockSpec((1, D, F), w_map),
                      pl.BlockSpec((1, D, F), w_map),
                      pl.BlockSpec((1, F, D), w_map)],
            out_specs=pl.BlockSpec((BM, DH), lambda i, be, nv, nu: (i, 0)),
            scratch_shapes=[pltpu.VMEM((D, F), jnp.bfloat16), pltpu.VMEM((D, F), jnp.bfloat16),
                            pltpu.VMEM((F, D), jnp.bfloat16)]),
        out_shape=jax.ShapeDtypeStruct((n_pad, DH), jnp.int32),
        compiler_params=pltpu.CompilerParams(dimension_semantics=("arbitrary",),
                                             vmem_limit_bytes=VMEM_LIMIT),
    )(blk_e, n_valid, n_used, x_sorted, w_gate, w_up, w_down)


TC = 128


def _combine_kernel(pos_hbm, y_hbm, w_ref, x1_ref, sh_ref, mod_ref, fg_ref, out_ref, ybuf, pos_s, gsem, psem):
    j = pl.program_id(0)
    last = pl.num_programs(0) - 1

    def pos_copy(b, slot):
        return pltpu.make_async_copy(pos_hbm.at[b], pos_s.at[slot], psem.at[slot])

    def start_gather(slot):
        for k in range(TOP_K):
            for t in range(TC):
                pltpu.make_async_copy(y_hbm.at[pos_s[slot, k * TC + t]], ybuf.at[slot, k, t],
                                      gsem.at[slot]).start()

    def wait_gather(slot):
        pltpu.make_async_copy(ybuf.at[slot], ybuf.at[slot], gsem.at[slot]).wait()

    @pl.when(j == 0)
    def _():
        pos_copy(0, 0).start()
        pos_copy(0, 0).wait()
        start_gather(0)
        pos_copy(jnp.minimum(1, last), 1).start()

    slot = j % 2
    nslot = 1 - slot
    pos_copy(0, nslot).wait()
    start_gather(nslot)
    pos_copy(jnp.minimum(j + 2, last), slot).start()
    wait_gather(slot)
    w = w_ref[...]
    acc_a = acc_b = None
    for k in range(TOP_K):
        ya, yb = _unpack_rows(ybuf[slot, k])
        acc_a = w[:, k:k + 1] * ya if k == 0 else acc_a + w[:, k:k + 1] * ya
        acc_b = w[:, k:k + 1] * yb if k == 0 else acc_b + w[:, k:k + 1] * yb
    acc = jnp.concatenate([acc_a, acc_b], axis=1)
    x2 = x1_ref[...] + mod_ref[0, 5:6, :] * (acc + sh_ref[...])
    out_ref[...] = x2 * lax.rsqrt(jnp.mean(x2 * x2, axis=-1, keepdims=True) + EPS) * fg_ref[...]

    @pl.when(j == last)
    def _():
        wait_gather(nslot)
        pos_copy(0, slot).wait()


def _combine_call(y_sorted, pos_t, wts, x1, shared, mod, final_g, n_ctx, lat_len):
    T, K = wts.shape
    DH = y_sorted.shape[1]
    D = 2 * DH
    n_tiles = T // TC
    row = functools.partial(_mod_row, tokens_per_tile=TC, n_ctx=n_ctx, lat_len=lat_len)
    return pl.pallas_call(
        _combine_kernel,
        grid=(n_tiles,),
        in_specs=[pl.BlockSpec(memory_space=pl.ANY),
                  pl.BlockSpec(memory_space=pl.ANY),
                  pl.BlockSpec((TC, K), lambda j: (j, 0)),
                  pl.BlockSpec((TC, D), lambda j: (j, 0)),
                  pl.BlockSpec((TC, D), lambda j: (j, 0)),
                  pl.BlockSpec((1, N_MOD, D), lambda j: (row(j), 0, 0)),
                  pl.BlockSpec((1, D), lambda j: (0, 0))],
        out_specs=pl.BlockSpec((TC, D), lambda j: (j, 0)),
        scratch_shapes=[pltpu.VMEM((2, K, TC, DH), jnp.uint32),
                        pltpu.SMEM((2, K * TC), jnp.int32),
                        pltpu.SemaphoreType.DMA((2,)), pltpu.SemaphoreType.DMA((2,))],
        out_shape=jax.ShapeDtypeStruct((T, D), jnp.float32),
        compiler_params=pltpu.CompilerParams(dimension_semantics=("arbitrary",)),
    )(pos_t, y_sorted, wts, x1, shared, mod, final_g.reshape(1, D))


SC_CORES = 2
SC_SUBCORES = 16
SC_CHUNK = 128


def _sc_gather_call(table, idx):
    n_idx = idx.shape[0]
    width = table.shape[1]
    n_workers = SC_CORES * SC_SUBCORES
    per_worker = n_idx // n_workers
    assert per_worker * n_workers == n_idx and per_worker % SC_CHUNK == 0
    mesh = plsc.VectorSubcoreMesh(core_axis_name="c", subcore_axis_name="s")

    def body(table_hbm, idx_hbm, out_hbm, idx_v, rows_v, sem):
        wid = lax.axis_index("s") * SC_CORES + lax.axis_index("c")
        base = wid * per_worker

        @pl.loop(0, per_worker // SC_CHUNK)
        def _(ch):
            off = base + ch * SC_CHUNK
            pltpu.sync_copy(idx_hbm.at[pl.ds(off, SC_CHUNK)], idx_v)
            pltpu.async_copy(table_hbm.at[idx_v], rows_v, sem).wait()
            pltpu.sync_copy(rows_v, out_hbm.at[pl.ds(off, SC_CHUNK)])

    return pl.kernel(
        body, out_type=jax.ShapeDtypeStruct((n_idx, width), table.dtype), mesh=mesh,
        scratch_types=[pltpu.VMEM((SC_CHUNK,), jnp.int32), pltpu.VMEM((SC_CHUNK, width), table.dtype),
                       pltpu.SemaphoreType.DMA],
    )(table, idx)


def _sc_scatter_call(src, pos, n_out):
    n_idx = pos.shape[0]
    width = src.shape[1]
    n_workers = SC_CORES * SC_SUBCORES
    per_worker = n_idx // n_workers
    assert per_worker * n_workers == n_idx and per_worker % SC_CHUNK == 0 and TR % SC_CHUNK == 0
    mesh = plsc.VectorSubcoreMesh(core_axis_name="c", subcore_axis_name="s")
    tile_pairs = TOP_K * TR

    def body(src_hbm, pos_hbm, out_hbm, idx_v, rows_v, sem):
        wid = lax.axis_index("s") * SC_CORES + lax.axis_index("c")
        base = wid * per_worker

        @pl.loop(0, per_worker // SC_CHUNK)
        def _(ch):
            off = base + ch * SC_CHUNK
            row0 = (off // tile_pairs) * TR + off % TR
            pltpu.sync_copy(pos_hbm.at[pl.ds(off, SC_CHUNK)], idx_v)
            pltpu.sync_copy(src_hbm.at[pl.ds(row0, SC_CHUNK)], rows_v)
            pltpu.async_copy(rows_v, out_hbm.at[idx_v], sem).wait()

    return pl.kernel(
        body, out_type=jax.ShapeDtypeStruct((n_out, width), src.dtype), mesh=mesh,
        scratch_types=[pltpu.VMEM((SC_CHUNK,), jnp.int32), pltpu.VMEM((SC_CHUNK, width), src.dtype),
                       pltpu.SemaphoreType.DMA],
    )(src, pos)


def _positions_kernel(idx_ref, rank_ref, pstart_ref, pos_ref):
    erow = lax.broadcasted_iota(jnp.int32, (N_EXPERTS, TR), 0)
    pstart = pstart_ref[...]
    for k in range(TOP_K):
        hit = erow == idx_ref[0, k:k + 1, :]
        seg = jnp.sum(jnp.where(hit, pstart, 0), axis=0, keepdims=True)
        pos_ref[0, k:k + 1, :] = seg + rank_ref[0, k:k + 1, :]


def _positions_call(idx, rank, pad_start):
    n_tiles = idx.shape[0]
    row_spec = pl.BlockSpec((1, TOP_K, TR), lambda i: (i, 0, 0))
    return pl.pallas_call(
        _positions_kernel,
        grid=(n_tiles,),
        in_specs=[row_spec, row_spec, pl.BlockSpec((N_EXPERTS, 1), lambda i: (0, 0))],
        out_specs=row_spec,
        out_shape=jax.ShapeDtypeStruct((n_tiles, TOP_K, TR), jnp.int32),
    )(idx, rank, pad_start.reshape(N_EXPERTS, 1))


def _combine_dense_kernel(g_ref, w_ref, x1_ref, sh_ref, mod_ref, fg_ref, outc_ref, outl_ref, *, n_ctx_tiles):
    w = w_ref[...]
    acc_a = acc_b = None
    for k in range(TOP_K):
        ya, yb = _unpack_rows(g_ref[0, k])
        acc_a = w[:, k:k + 1] * ya if k == 0 else acc_a + w[:, k:k + 1] * ya
        acc_b = w[:, k:k + 1] * yb if k == 0 else acc_b + w[:, k:k + 1] * yb
    acc = jnp.concatenate([acc_a, acc_b], axis=1)
    x2 = x1_ref[...] + mod_ref[0, 5:6, :] * (acc + sh_ref[...])
    out = x2 * lax.rsqrt(jnp.mean(x2 * x2, axis=-1, keepdims=True) + EPS) * fg_ref[...]
    is_ctx = pl.program_id(0) < n_ctx_tiles

    @pl.when(is_ctx)
    def _():
        outc_ref[...] = out

    @pl.when(jnp.logical_not(is_ctx))
    def _():
        outl_ref[...] = out


def _combine_dense_call(gathered, wts, x1, shared, mod, final_g, n_ctx, lat_len):
    T, K = wts.shape
    DH = gathered.shape[-1]
    D = 2 * DH
    row = functools.partial(_mod_row, tokens_per_tile=TC, n_ctx=n_ctx, lat_len=lat_len)
    nct = n_ctx // TC
    return pl.pallas_call(
        functools.partial(_combine_dense_kernel, n_ctx_tiles=nct),
        grid=(T // TC,),
        in_specs=[pl.BlockSpec((1, K, TC, DH), lambda j: (j, 0, 0, 0)),
                  pl.BlockSpec((TC, K), lambda j: (j, 0)),
                  pl.BlockSpec((TC, D), lambda j: (j, 0)),
                  pl.BlockSpec((TC, D), lambda j: (j, 0)),
                  pl.BlockSpec((1, N_MOD, D), lambda j: (row(j), 0, 0)),
                  pl.BlockSpec((1, D), lambda j: (0, 0))],
        out_specs=[pl.BlockSpec((TC, D), lambda j: (jnp.minimum(j, nct - 1), 0)),
                   pl.BlockSpec((TC, D), lambda j: (jnp.maximum(j - nct, 0), 0))],
        out_shape=[jax.ShapeDtypeStruct((n_ctx, D), jnp.float32), jax.ShapeDtypeStruct((T - n_ctx, D), jnp.float32)],
        compiler_params=pltpu.CompilerParams(dimension_semantics=("arbitrary",)),
    )(gathered, wts, x1, shared, mod, final_g.reshape(1, D))


def _moe_routed(h2, h2p, router_w, router_bias, w_gate, w_up, w_down):
    T, D = h2.shape
    idx, rank, w_rows, cnt = _route_call(h2, router_w, router_bias)
    wts = w_rows.transpose(0, 2, 1).reshape(T, TOP_K)
    counts = cnt[:, 0]
    padded = (counts + BM - 1) // BM * BM
    pad_end = jnp.cumsum(padded)
    pad_start = (pad_end - padded).astype(jnp.int32)
    n_pad = T * TOP_K + N_EXPERTS * BM
    n_blk = n_pad // BM
    n_used = (pad_end[-1] // BM).astype(jnp.int32).reshape(1)
    pos = _positions_call(idx, rank, pad_start)
    x_sorted = _sc_scatter_call(h2p, pos.reshape(-1), n_pad)
    blk_row0 = jnp.arange(n_blk, dtype=jnp.int32) * BM
    blk_e = jnp.minimum(jnp.sum((pad_end[None, :] <= blk_row0[:, None]).astype(jnp.int32), axis=1), N_EXPERTS - 1)
    own = blk_e[:, None] == jnp.arange(N_EXPERTS, dtype=jnp.int32)[None, :]
    seg_end = jnp.sum(jnp.where(own, (pad_start + counts)[None, :], 0), axis=1)
    n_valid = jnp.clip(seg_end - blk_row0, 0, BM).astype(jnp.int32)
    y = _expert_call(x_sorted, blk_e, n_valid, n_used, w_gate, w_up, w_down)
    pos_t = pos.reshape(T // TR, TOP_K, TR // TC, TC).transpose(0, 2, 1, 3).reshape(T // TC, TOP_K * TC)
    return y, pos_t, wts


def kernel(x_prompt, x_sample, state_delta, c, c_ctx, w_ada, b_ada, norm1_g, w_in, conv_w, a_log,
           dt_bias, onorm_g, pool_w, pool_scale, w_out, norm2_g, router_w, router_bias, exp_w_gate,
           exp_w_up, exp_w_down, sh_w_gate, sh_w_up, sh_w_down, final_g):
    Bc, Lc, D = x_prompt.shape
    Bl, Ll, _ = x_sample.shape
    n_ctx = Bc * Lc
    assert DEPTH == 1 and 1 + Bl <= MOD_ROWS and n_ctx % Ll == 0
    x_parts = (x_prompt.reshape(n_ctx, D), x_sample.reshape(Bl * Ll, D))
    cvec = jnp.concatenate([c_ctx[None], c, jnp.zeros((MOD_ROWS - 1 - Bl, D), c.dtype)], axis=0)
    l = 0
    mod = _ada_call(cvec, w_ada[l], b_ada[l]).reshape(MOD_ROWS, N_MOD, D)
    qkv, z, ba, u = _inproj_call(*x_parts, mod, norm1_g[l], w_in[l], Ll)
    dn = (conv_w[l], a_log[l], dt_bias[l], onorm_g[l])
    oa_c, st_ctx = _delta_call(qkv, z, ba, *dn, None, Bc, Lc, 0)
    oa_l, _ = _delta_call(qkv, z, ba, *dn, state_delta[:, l], Bl, Ll, n_ctx // Ll)
    op_c = _pool_call(u, pool_w[l], pool_scale[l], False, Bc, Lc, 0)
    op_l = _pool_call(u, pool_w[l], pool_scale[l], True, Bl, Ll, n_ctx // Ll)
    x1, h2, h2p, shared = _outproj_call(x_parts, (oa_c, oa_l), (op_c, op_l), mod, norm2_g[l], w_out[l],
                                        sh_w_gate[l], sh_w_up[l], sh_w_down[l], Ll)
    y, pos_t, wts = _moe_routed(h2, h2p, router_w[l], router_bias[l], exp_w_gate[l], exp_w_up[l],
                                exp_w_down[l])
    T = n_ctx + Bl * Ll
    gathered = _sc_gather_call(y, pos_t.reshape(-1))
    out_c, out_l = _combine_dense_call(gathered.reshape(T // TC, TOP_K, TC, D // 2), wts, x1, shared, mod,
                                       final_g, n_ctx, Ll)
    y_prompt = out_c.reshape(Bc, Lc, D)
    y_sample = out_l.reshape(Bl, Ll, D)
    new_state_delta = st_ctx[:, None].astype(x_prompt.dtype)
    return (y_prompt, y_sample, new_state_delta)
```

```python
import functools
import jax, jax.numpy as jnp
from jax import lax
from jax.experimental import pallas as pl
from jax.experimental.pallas import tpu as pltpu
from jax.experimental.pallas import tpu_sc as plsc

D_MODEL = 1024
DEPTH = 1
GRID_W = 64
D_MIX = D_MODEL
D_A = D_MIX // 2
D_P = D_MIX - D_A
H_A = 4
DK = D_A // H_A
DV = D_A // H_A
CONV_K = 5
CHUNK = 64
POOL_WINDOWS = (2, 4, 8, 16)
N_PG = len(POOL_WINDOWS)
PG = D_P // N_PG
N_EXPERTS = 256
TOP_K = 8
N_GROUPS = 8
TOPK_GROUP = 4
ROUTED_SCALE = 2.5
EPS = 1e-6
VMEM_LIMIT = 48 * 1024 * 1024


def _split_bf16(a):
    hi = a.astype(jnp.bfloat16)
    return hi, (a - hi.astype(jnp.float32)).astype(jnp.bfloat16)


def _bdot(a, b):
    return jnp.dot(a, b, preferred_element_type=jnp.float32)


def _pack_rows(x):
    m = x.shape[1] // 2
    lo = lax.bitcast_convert_type(x[:, :m].astype(jnp.bfloat16).astype(jnp.float32), jnp.uint32)
    hi = lax.bitcast_convert_type(x[:, m:].astype(jnp.bfloat16).astype(jnp.float32), jnp.uint32)
    return lax.bitcast_convert_type(hi | (lo >> 16), jnp.int32)


def _pack_rows_native(x):
    m = x.shape[1] // 2
    return pltpu.pack_elementwise([x[:, :m], x[:, m:]], packed_dtype=jnp.bfloat16)


def _unpack_rows_native(p):
    return tuple(pltpu.unpack_elementwise(p, index=i, packed_dtype=jnp.bfloat16, unpacked_dtype=jnp.float32)
                 for i in range(2))


N_MOD = 6
MOD_ROWS = 8
TM = 512


def _ada_kernel(c_ref, w_ref, b_ref, o_ref):
    c = c_ref[...]
    s = c * jax.nn.sigmoid(c)
    sh, sl = _split_bf16(s)
    wh, wl = _split_bf16(w_ref[...])
    o_ref[...] = _bdot(sh, wh) + (_bdot(sh, wl) + _bdot(sl, wh)) + b_ref[...]


def _ada_call(cvec, w_ada, b_ada):
    R, D = cvec.shape
    N = w_ada.shape[1]
    tn = 1024
    return pl.pallas_call(
        _ada_kernel,
        grid=(N // tn,),
        in_specs=[pl.BlockSpec((R, D), lambda j: (0, 0)),
                  pl.BlockSpec((D, tn), lambda j: (0, j)),
                  pl.BlockSpec((1, tn), lambda j: (0, j))],
        out_specs=pl.BlockSpec((R, tn), lambda j: (0, j)),
        out_shape=jax.ShapeDtypeStruct((R, N), jnp.float32),
    )(cvec, w_ada, b_ada.reshape(1, N))


def _mod_row(tile, tokens_per_tile, n_ctx, lat_len):
    t0 = tile * tokens_per_tile
    return jnp.where(t0 < n_ctx, 0, 1 + (t0 - n_ctx) // lat_len)


def _two_part_specs(n_ctx_tiles, width):
    return (pl.BlockSpec((TM, width), lambda i: (jnp.minimum(i, n_ctx_tiles - 1), 0)),
            pl.BlockSpec((TM, width), lambda i: (jnp.maximum(i - n_ctx_tiles, 0), 0)))


def _pick(n_ctx_tiles, ctx_ref, lat_ref):
    return jnp.where(pl.program_id(0) < n_ctx_tiles, ctx_ref[...], lat_ref[...])


def _inproj_kernel(xc_ref, xl_ref, mod_ref, g_ref, wq_ref, wz_ref, wb_ref, wu_ref, q_ref, z_ref, b_ref, u_ref,
                   *, n_ctx_tiles):
    x = _pick(n_ctx_tiles, xc_ref, xl_ref)
    y = x * lax.rsqrt(jnp.mean(x * x, axis=-1, keepdims=True) + EPS) * g_ref[...]
    h = (y * (1.0 + mod_ref[0, 1:2, :]) + mod_ref[0, 0:1, :]).astype(jnp.bfloat16)
    q_ref[...] = _bdot(h, wq_ref[...])
    z_ref[...] = _bdot(h, wz_ref[...])
    b_ref[...] = _bdot(h, wb_ref[...])
    u_ref[...] = _bdot(h, wu_ref[...])


def _inproj_call(x_ctx, x_lat, mod, norm1_g, w_in, lat_len):
    n_ctx, D = x_ctx.shape
    T = n_ctx + x_lat.shape[0]
    bf = jnp.bfloat16
    nq, nz, nb = 3 * D_A, D_A, 4 * H_A
    wq = w_in[:, :nq].astype(bf)
    wz = w_in[:, nq:nq + nz].astype(bf)
    wb = jnp.pad(w_in[:, nq + nz:nq + nz + nb], ((0, 0), (0, 128 - nb))).astype(bf)
    wu = w_in[:, nq + nz + nb:].astype(bf)
    row = functools.partial(_mod_row, tokens_per_tile=TM, n_ctx=n_ctx, lat_len=lat_len)

    def full(a):
        return pl.BlockSpec(a.shape, lambda i: (0, 0))

    def rows(n):
        return pl.BlockSpec((TM, n), lambda i: (i, 0))

    return pl.pallas_call(
        functools.partial(_inproj_kernel, n_ctx_tiles=n_ctx // TM),
        grid=(T // TM,),
        in_specs=[*_two_part_specs(n_ctx // TM, D), pl.BlockSpec((1, N_MOD, D), lambda i: (row(i), 0, 0)),
                  pl.BlockSpec((1, D), lambda i: (0, 0)), full(wq), full(wz), full(wb), full(wu)],
        out_specs=[rows(nq), rows(nz), rows(128), rows(D_P)],
        out_shape=[jax.ShapeDtypeStruct((T, nq), jnp.float32), jax.ShapeDtypeStruct((T, nz), jnp.float32),
                   jax.ShapeDtypeStruct((T, 128), jnp.float32), jax.ShapeDtypeStruct((T, D_P), jnp.float32)],
        compiler_params=pltpu.CompilerParams(dimension_semantics=("arbitrary",),
                                             vmem_limit_bytes=VMEM_LIMIT),
    )(x_ctx, x_lat, mod, norm1_g.reshape(1, D), wq, wz, wb, wu)


PT = 256


def _window_bounds(pos, w, n):
    return jnp.maximum(pos - w // 2, 0), jnp.minimum(pos + w - w // 2, n)


def _band_sum(band, x):
    xh, xl = _split_bf16(x)
    return _bdot(band, xh) + _bdot(band, xl)


def _pool_seq_kernel(u_ref, pw_ref, ps_ref, o_ref):
    L = u_ref.shape[0]
    ti = lax.broadcasted_iota(jnp.int32, (L, L), 0)
    ji = lax.broadcasted_iota(jnp.int32, (L, L), 1)
    tcol = lax.broadcasted_iota(jnp.int32, (L, 1), 0)
    for i, w in enumerate(POOL_WINDOWS):
        lo, hi = _window_bounds(ti, w, L)
        band = ((ji >= lo) & (ji < hi)).astype(jnp.bfloat16)
        clo, chi = _window_bounds(tcol, w, L)
        ug = u_ref[:, i * PG:(i + 1) * PG]
        mean = _band_sum(band, ug) / (chi - clo).astype(jnp.float32)
        d = (mean - ug).astype(jnp.bfloat16)
        o_ref[:, i * PG:(i + 1) * PG] = _bdot(d, pw_ref[i]) * ps_ref[:, i * PG:(i + 1) * PG]


def _pool_grid_kernel(u_ref, pw_ref, ps_ref, o_ref, pad_s, r_s):
    L = u_ref.shape[0]
    rows = L // GRID_W
    halo = (max(POOL_WINDOWS) // 2) * GRID_W
    pad_s[0:halo, :] = jnp.zeros((halo, D_P), jnp.float32)
    pad_s[halo + L:, :] = jnp.zeros((halo, D_P), jnp.float32)
    pad_s[halo:halo + L, :] = u_ref[...]
    ti = lax.broadcasted_iota(jnp.int32, (PT, PT), 0)
    ji = lax.broadcasted_iota(jnp.int32, (PT, PT), 1)
    tcol = lax.broadcasted_iota(jnp.int32, (PT, 1), 0)
    for i, w in enumerate(POOL_WINDOWS):
        cs = slice(i * PG, (i + 1) * PG)
        acc = None
        for dr in range(-(w // 2), w - w // 2):
            part = pad_s[halo + dr * GRID_W:halo + dr * GRID_W + L, cs]
            acc = part if acc is None else acc + part
        r_s[...] = acc
        lo, hi = _window_bounds(ti % GRID_W, w, GRID_W)
        band = ((ji // GRID_W == ti // GRID_W) & (ji % GRID_W >= lo) & (ji % GRID_W < hi)).astype(jnp.bfloat16)
        clo, chi = _window_bounds(tcol % GRID_W, w, GRID_W)
        ccnt = (chi - clo).astype(jnp.float32)
        for tile in range(L // PT):
            ts = slice(tile * PT, (tile + 1) * PT)
            rlo, rhi = _window_bounds(tile * (PT // GRID_W) + tcol // GRID_W, w, rows)
            mean = _band_sum(band, r_s[ts, :]) / ((rhi - rlo).astype(jnp.float32) * ccnt)
            d = (mean - u_ref[ts, cs]).astype(jnp.bfloat16)
            o_ref[ts, cs] = _bdot(d, pw_ref[i]) * ps_ref[:, cs]


def _pool_call(u, pool_w, pool_scale, grid, B, L, row_blk0):
    pw = pool_w.astype(jnp.bfloat16)
    ps = pool_scale.reshape(1, D_P)
    specs = dict(
        grid=(B,),
        in_specs=[pl.BlockSpec((L, D_P), lambda b: (row_blk0 + b, 0)),
                  pl.BlockSpec((N_PG, PG, PG), lambda b: (0, 0, 0)),
                  pl.BlockSpec((1, D_P), lambda b: (0, 0))],
        out_specs=pl.BlockSpec((L, D_P), lambda b: (b, 0)),
        out_shape=jax.ShapeDtypeStruct((B * L, D_P), jnp.float32),
        compiler_params=pltpu.CompilerParams(dimension_semantics=("arbitrary",),
                                             vmem_limit_bytes=VMEM_LIMIT))
    if not grid:
        return pl.pallas_call(_pool_seq_kernel, **specs)(u, pw, ps)
    halo = (max(POOL_WINDOWS) // 2) * GRID_W
    return pl.pallas_call(
        _pool_grid_kernel,
        scratch_shapes=[pltpu.VMEM((L + 2 * halo, D_P), jnp.float32), pltpu.VMEM((L, PG), jnp.float32)],
        **specs)(u, pw, ps)


def _outproj_kernel(xc_ref, xl_ref, oac_ref, oal_ref, opc_ref, opl_ref, mod_ref, g2_ref, wo_ref, sg_ref, su_ref,
                    sd_ref, x1_ref, h2_ref, h2p_ref, sh_ref, *, n_ctx_tiles):
    o_a = _pick(n_ctx_tiles, oac_ref, oal_ref)
    o_p = _pick(n_ctx_tiles, opc_ref, opl_ref)
    mix = (_bdot(o_a.astype(jnp.bfloat16), wo_ref[:D_A, :])
           + _bdot(o_p.astype(jnp.bfloat16), wo_ref[D_A:, :]))
    x1 = _pick(n_ctx_tiles, xc_ref, xl_ref) + mod_ref[0, 2:3, :] * mix
    x1_ref[...] = x1
    y = x1 * lax.rsqrt(jnp.mean(x1 * x1, axis=-1, keepdims=True) + EPS) * g2_ref[...]
    h2 = y * (1.0 + mod_ref[0, 4:5, :]) + mod_ref[0, 3:4, :]
    h2_ref[...] = h2
    h2p_ref[...] = _pack_rows(h2)
    hb = h2.astype(jnp.bfloat16)
    g = _bdot(hb, sg_ref[...])
    a = (g * jax.nn.sigmoid(g)) * _bdot(hb, su_ref[...])
    sh_ref[...] = _bdot(a.astype(jnp.bfloat16), sd_ref[...])


def _outproj_call(x_parts, oa_parts, op_parts, mod, norm2_g, w_out, sh_gate, sh_up, sh_down, lat_len):
    n_ctx, D = x_parts[0].shape
    T = n_ctx + x_parts[1].shape[0]
    nct = n_ctx // TM
    bf = jnp.bfloat16
    row = functools.partial(_mod_row, tokens_per_tile=TM, n_ctx=n_ctx, lat_len=lat_len)
    ws = [w_out.astype(bf), sh_gate.astype(bf), sh_up.astype(bf), sh_down.astype(bf)]

    def rows(n):
        return pl.BlockSpec((TM, n), lambda i: (i, 0))

    return pl.pallas_call(
        functools.partial(_outproj_kernel, n_ctx_tiles=nct),
        grid=(T // TM,),
        in_specs=[*_two_part_specs(nct, D), *_two_part_specs(nct, D_A), *_two_part_specs(nct, D_P),
                  pl.BlockSpec((1, N_MOD, D), lambda i: (row(i), 0, 0)),
                  pl.BlockSpec((1, D), lambda i: (0, 0))] + [pl.BlockSpec(w.shape, lambda i: (0, 0)) for w in ws],
        out_specs=[rows(D), rows(D), rows(D // 2), rows(D)],
        out_shape=[jax.ShapeDtypeStruct((T, D), jnp.float32), jax.ShapeDtypeStruct((T, D), jnp.float32),
                   jax.ShapeDtypeStruct((T, D // 2), jnp.int32), jax.ShapeDtypeStruct((T, D), jnp.float32)],
        compiler_params=pltpu.CompilerParams(dimension_semantics=("arbitrary",),
                                             vmem_limit_bytes=VMEM_LIMIT),
    )(*x_parts, *oa_parts, *op_parts, mod, norm2_g.reshape(1, D), *ws)


SC = 256
CPS = SC // CHUNK
BASE = 16
DELTA_HEAD_ROWS = 4096


def _mm(a, b):
    return jnp.dot(a.astype(jnp.bfloat16), b.astype(jnp.bfloat16), preferred_element_type=jnp.float32)


def _mm_nt(a, b):
    return lax.dot_general(a.astype(jnp.bfloat16), b.astype(jnp.bfloat16), (((1,), (1,)), ((), ())),
                           preferred_element_type=jnp.float32)


def _softplus(x):
    return jnp.maximum(x, 0.0) + jnp.log(1.0 + jnp.exp(-jnp.abs(x)))


def _delta_kernel(sc_ref, xq_ref, xk_ref, xv_ref, z_ref, bac_ref, bar_ref, cwq_ref, cwk_ref, cwv_ref,
                  og_ref, s0_ref, o_ref, st_ref, q_s, k_s, v_s, o_s, vn_s, *, n_sc, zero_init, hpb):
    hb = pl.program_id(1)
    L = q_s.shape[1]

    def conv(x_ref, w_ref, cs):
        x = x_ref[:, cs]
        row = lax.broadcasted_iota(jnp.int32, x.shape, 0)
        acc = x * w_ref[CONV_K // 2:CONV_K // 2 + 1, cs]
        for j in range(CONV_K):
            d = j - CONV_K // 2
            if d == 0:
                continue
            xs = pltpu.roll(x, (-d) % L, 0)
            ok = (row + d >= 0) & (row + d < L)
            acc = acc + jnp.where(ok, xs, 0.0) * w_ref[j:j + 1, cs]
        return acc * jax.nn.sigmoid(acc)

    for hh in range(hpb):
        cs = slice(hh * DK, (hh + 1) * DK)
        q = conv(xq_ref, cwq_ref, cs)
        q_s[hh] = q * lax.rsqrt(jnp.sum(q * q, axis=-1, keepdims=True) + EPS) * (DK ** -0.5)
        k = conv(xk_ref, cwk_ref, cs)
        k_s[hh] = k * lax.rsqrt(jnp.sum(k * k, axis=-1, keepdims=True) + EPS)
        v_s[hh] = conv(xv_ref, cwv_ref, cs)
    o_s[...] = jnp.zeros_like(o_s)

    ri = lax.broadcasted_iota(jnp.int32, (SC, SC), 0)
    ci = lax.broadcasted_iota(jnp.int32, (SC, SC), 1)
    same = (ri // CHUNK) == (ci // CHUNK)
    same_base = (ri // BASE) == (ci // BASE)
    merge_masks = [(ri // w) == (ci // w) for w in (2 * BASE, CHUNK)]
    eye = (ri == ci).astype(jnp.float32)
    rowi = lax.broadcasted_iota(jnp.int32, (SC, DV), 0)

    def prep(m, d, hh):
        r0 = pl.multiple_of(m * SC, SC)
        h = hb * hpb + hh
        q = q_s[hh, pl.ds(r0, SC), :]
        k = k_s[hh, pl.ds(r0, SC), :]
        v = v_s[hh, pl.ds(r0, SC), :]
        bc = bac_ref[0, hh, pl.ds(r0, SC), :]
        br = bar_ref[0, hh, m]
        a_l = sc_ref[d * H_A + h]
        dtb = sc_ref[2 * H_A + d * H_A + h]
        neg_ea = -jnp.exp(jnp.full((1, 1), a_l, jnp.float32))
        beta = jax.nn.sigmoid(bc[:, d:d + 1])
        g_col = neg_ea * _softplus(bc[:, 2 + d:3 + d] + dtb)
        g_row = neg_ea * _softplus(br[2 + d:3 + d, :] + dtb)
        if d == 0:
            tri, strict = same & (ci <= ri), same & (ci < ri)
        else:
            tri, strict = same & (ci >= ri), same & (ci > ri)
        tri_t = same & (ri <= ci) if d == 0 else same & (ri >= ci)
        gc_col = jnp.sum(jnp.where(tri, g_row, 0.0), axis=1, keepdims=True)
        gc_row = jnp.sum(jnp.where(tri_t, g_col, 0.0), axis=0, keepdims=True)
        gl_col = jnp.sum(jnp.where(same, g_row, 0.0), axis=1, keepdims=True)
        decay = jnp.where(tri, jnp.exp(jnp.where(tri, gc_col - gc_row, 0.0)), 0.0)
        kb = k * beta
        a = jnp.where(strict, _mm_nt(kb, k) * decay, 0.0)
        attn = jnp.where(tri, _mm_nt(q, k) * decay, 0.0)
        eg = jnp.exp(gc_col)
        x = jnp.concatenate([v * beta, kb * eg], axis=1)
        qd = q * eg
        kdt = (k * jnp.exp(gl_col - gc_col)).T
        return dict(r0=r0, a=a, attn=attn, x=x, qd=qd, kdt=kdt, egl=jnp.exp(gl_col))

    def run_chains(ms, states):
        n = len(chains)
        ops = [prep(ms[i], d, hh) for i, (hh, d) in enumerate(chains)]
        ps = [jnp.where(same_base, o["a"], 0.0) for o in ops]
        ts = [eye - p for p in ps]
        for _ in range(BASE.bit_length() - 2):
            ps = [_mm(p, p) for p in ps]
            ts = [t + _mm(t, p) for t, p in zip(ts, ps)]
        inner = same_base
        for outer in merge_masks:
            lows = [_mm(jnp.where(outer & ~inner, o["a"], 0.0), t) for o, t in zip(ops, ts)]
            ts = [t - _mm(t, low) for t, low in zip(ts, lows)]
            inner = outer
        xs = [_mm(t, o["x"]) for t, o in zip(ts, ops)]
        for i in range(n):
            vn_s[i] = jnp.zeros((SC, DV), jnp.float32)
        states = list(states)
        for step in range(CPS):
            cs = [step if d == 0 else CPS - 1 - step for _, d in chains]
            los = [c * CHUNK for c in cs]
            ws_qs = [_mm(jnp.concatenate([x[lo:lo + CHUNK, DV:], o["qd"][lo:lo + CHUNK]], axis=0), s)
                     for x, o, lo, s in zip(xs, ops, los, states)]
            for i in range(n):
                vn_s[i, los[i]:los[i] + CHUNK, :] = xs[i][los[i]:los[i] + CHUNK, :DV] - ws_qs[i][:CHUNK]
            vns = [vn_s[i] for i in range(n)]
            o_cs = [wq[CHUNK:] + _mm(o["attn"][lo:lo + CHUNK, :], vn)
                    for wq, o, lo, vn in zip(ws_qs, ops, los, vns)]
            for i, (hh, _) in enumerate(chains):
                o_s[hh, pl.ds(ops[i]["r0"] + los[i], CHUNK), :] += o_cs[i]
            states = [s * o["egl"][lo:lo + 1, :]
                      + _mm(o["kdt"], jnp.where((rowi >= lo) & (rowi < lo + CHUNK), vn, 0.0))
                      for s, o, lo, vn in zip(states, ops, los, vns)]
        return tuple(states)

    if zero_init:
        states = tuple(jnp.zeros((DK, DV), jnp.float32) for _ in range(2 * hpb))
    else:
        states = tuple(s0_ref[0, d, hh] for hh in range(hpb) for d in range(2))

    chains = [(hh, d) for hh in range(hpb) for d in range(2)]

    def body(m, carry):
        return run_chains([m if d == 0 else n_sc - 1 - m for _, d in chains], carry)

    if n_sc == 1:
        states = body(0, states)
    else:
        states = lax.fori_loop(0, n_sc, body, states)

    for hh in range(hpb):
        for d in range(2):
            st_ref[0, d, hh] = states[2 * hh + d]
        o = o_s[hh]
        o = o * lax.rsqrt(jnp.mean(o * o, axis=-1, keepdims=True) + EPS) * og_ref[...]
        zz = z_ref[:, hh * DV:(hh + 1) * DV]
        o_ref[:, hh * DV:(hh + 1) * DV] = o * (zz * jax.nn.sigmoid(zz))


def _delta_call(qkv, z, ba, conv_w, a_log, dt_bias, onorm_g, s0, B, L, row_blk0):
    n_sc = L // SC
    t0 = row_blk0 * L
    bah = ba[t0:t0 + B * L, :4 * H_A].reshape(B, L, 4, H_A).transpose(0, 3, 1, 2)
    bar = bah.reshape(B, H_A, n_sc, SC, 4).transpose(0, 1, 2, 4, 3)
    scal = jnp.concatenate([a_log.reshape(-1), dt_bias.reshape(-1)]).astype(jnp.float32)
    hpb = max(1, min(H_A, DELTA_HEAD_ROWS // L))
    n_hb = H_A // hpb
    zero_init = s0 is None
    if zero_init:
        s0 = jnp.zeros((1, 2, hpb, DK, DV), jnp.float32)
        s0_spec = pl.BlockSpec((1, 2, hpb, DK, DV), lambda b, h, sc: (0, 0, 0, 0, 0))
    else:
        s0_spec = pl.BlockSpec((1, 2, hpb, DK, DV), lambda b, h, sc: (b, 0, h, 0, 0))

    def col(off):
        return pl.BlockSpec((L, hpb * DK), lambda b, h, sc: (row_blk0 + b, off * n_hb + h))

    def cw(off):
        return pl.BlockSpec((CONV_K, hpb * DK), lambda b, h, sc: (0, off * n_hb + h))

    kern = functools.partial(_delta_kernel, n_sc=n_sc, zero_init=zero_init, hpb=hpb)
    return pl.pallas_call(
        kern,
        grid_spec=pltpu.PrefetchScalarGridSpec(
            num_scalar_prefetch=1,
            grid=(B, n_hb),
            in_specs=[col(0), col(1), col(2),
                      pl.BlockSpec((L, hpb * DV), lambda b, h, sc: (row_blk0 + b, h)),
                      pl.BlockSpec((1, hpb, L, 4), lambda b, h, sc: (b, h, 0, 0)),
                      pl.BlockSpec((1, hpb, n_sc, 4, SC), lambda b, h, sc: (b, h, 0, 0, 0)),
                      cw(0), cw(1), cw(2),
                      pl.BlockSpec((1, DV), lambda b, h, sc: (0, 0)),
                      s0_spec],
            out_specs=[pl.BlockSpec((L, hpb * DV), lambda b, h, sc: (b, h)),
                       pl.BlockSpec((1, 2, hpb, DK, DV), lambda b, h, sc: (b, 0, h, 0, 0))],
            scratch_shapes=[pltpu.VMEM((hpb, L, DK), jnp.float32), pltpu.VMEM((hpb, L, DK), jnp.float32),
                            pltpu.VMEM((hpb, L, DV), jnp.float32), pltpu.VMEM((hpb, L, DV), jnp.float32),
                            pltpu.VMEM((2 * hpb, SC, DV), jnp.float32)]),
        out_shape=[jax.ShapeDtypeStruct((B * L, D_A), jnp.float32),
                   jax.ShapeDtypeStruct((B, 2, H_A, DK, DV), jnp.float32)],
        compiler_params=pltpu.CompilerParams(dimension_semantics=("arbitrary", "arbitrary"),
                                             vmem_limit_bytes=VMEM_LIMIT),
    )(scal, qkv, qkv, qkv, z, bah, bar, conv_w, conv_w, conv_w, onorm_g.reshape(1, DV), s0)


TR = 256
GSZ = N_EXPERTS // N_GROUPS
NEG = -jnp.inf
BM = 512


def _route_kernel(h_ref, rwh_ref, rwl_ref, rb_ref, idx_ref, rank_ref, w_ref, cnt_ref, cnt_s):
    i = pl.program_id(0)

    @pl.when(i == 0)
    def _():
        cnt_s[...] = jnp.zeros_like(cnt_s)

    h = h_ref[...]
    hh, hl = _split_bf16(h)
    logits = _bdot(hh, rwh_ref[...]) + (_bdot(hh, rwl_ref[...]) + _bdot(hl, rwh_ref[...]))
    scores = jax.nn.sigmoid(logits.T)
    sel = scores + rb_ref[...]
    erow = lax.broadcasted_iota(jnp.int32, sel.shape, 0)
    grow = lax.broadcasted_iota(jnp.int32, (GSZ, TR), 0)

    def first_argmax(v, rows):
        m = jnp.max(v, axis=0, keepdims=True)
        first = jnp.min(jnp.where(v == m, rows, N_EXPERTS), axis=0, keepdims=True)
        return m, first

    gs = []
    for g in range(N_GROUPS):
        vg = sel[g * GSZ:(g + 1) * GSZ, :]
        m1, i1 = first_argmax(vg, grow)
        m2 = jnp.max(jnp.where(grow == i1, NEG, vg), axis=0, keepdims=True)
        gs.append(m1 + m2)
    cand = []
    for g in range(N_GROUPS):
        beat = jnp.zeros(gs[g].shape, jnp.int32)
        for o in range(N_GROUPS):
            if o == g:
                continue
            wins = (gs[o] > gs[g]) | ((gs[o] == gs[g]) & (o < g))
            beat = beat + wins.astype(jnp.int32)
        cand.append(jnp.where(beat < TOPK_GROUP, sel[g * GSZ:(g + 1) * GSZ, :], NEG))
    cand = jnp.concatenate(cand, axis=0)
    chosen = []
    picked = jnp.zeros(sel.shape, jnp.bool_)
    for _ in range(TOP_K):
        _, ik = first_argmax(cand, erow)
        hit = erow == ik
        chosen.append((ik, hit))
        picked = picked | hit
        cand = jnp.where(hit, NEG, cand)
    wsum = jnp.sum(jnp.where(picked, scores, 0.0), axis=0, keepdims=True)

    ri = lax.broadcasted_iota(jnp.int32, (TR, TR), 0)
    ci = lax.broadcasted_iota(jnp.int32, (TR, TR), 1)
    earlier = (ri < ci).astype(jnp.bfloat16)
    rank_mat = _bdot(picked.astype(jnp.bfloat16), earlier) + cnt_s[...]
    cnt_s[...] = cnt_s[...] + jnp.sum(picked.astype(jnp.float32), axis=1, keepdims=True)
    cnt_ref[...] = cnt_s[...].astype(jnp.int32)

    for k, (ik, hit) in enumerate(chosen):
        idx_ref[0, k:k + 1, :] = ik
        rank_ref[0, k:k + 1, :] = jnp.sum(jnp.where(hit, rank_mat, 0.0), axis=0, keepdims=True).astype(jnp.int32)
        w_ref[0, k:k + 1, :] = jnp.sum(jnp.where(hit, scores, 0.0), axis=0, keepdims=True) / wsum * ROUTED_SCALE


def _route_call(hf, router_w, router_bias):
    T, D = hf.shape
    n_tiles = T // TR
    rwh, rwl = _split_bf16(router_w)
    row_spec = pl.BlockSpec((1, TOP_K, TR), lambda i: (i, 0, 0))
    return pl.pallas_call(
        _route_kernel,
        grid=(n_tiles,),
        in_specs=[pl.BlockSpec((TR, D), lambda i: (i, 0)),
                  pl.BlockSpec((D, N_EXPERTS), lambda i: (0, 0)),
                  pl.BlockSpec((D, N_EXPERTS), lambda i: (0, 0)),
                  pl.BlockSpec((N_EXPERTS, 1), lambda i: (0, 0))],
        out_specs=[row_spec, row_spec, row_spec, pl.BlockSpec((N_EXPERTS, 1), lambda i: (0, 0))],
        scratch_shapes=[pltpu.VMEM((N_EXPERTS, 1), jnp.float32)],
        out_shape=[jax.ShapeDtypeStruct((n_tiles, TOP_K, TR), jnp.int32),
                   jax.ShapeDtypeStruct((n_tiles, TOP_K, TR), jnp.int32),
                   jax.ShapeDtypeStruct((n_tiles, TOP_K, TR), jnp.float32),
                   jax.ShapeDtypeStruct((N_EXPERTS, 1), jnp.int32)],
        compiler_params=pltpu.CompilerParams(dimension_semantics=("arbitrary",)),
    )(hf, rwh, rwl, router_bias.reshape(N_EXPERTS, 1).astype(jnp.float32))


def _expert_kernel(blk_e_ref, nvalid_ref, nused_ref, x_ref, wg_ref, wu_ref, wd_ref, y_ref, wg_s, wu_s, wd_s):
    i = pl.program_id(0)

    @pl.when(i < nused_ref[0])
    def _():
        e = blk_e_ref[i]
        prev = blk_e_ref[jnp.maximum(i - 1, 0)]

        @pl.when((i == 0) | (e != prev))
        def _():
            wg_s[...] = wg_ref[0].astype(jnp.bfloat16)
            wu_s[...] = wu_ref[0].astype(jnp.bfloat16)
            wd_s[...] = wd_ref[0].astype(jnp.bfloat16)

        row = lax.broadcasted_iota(jnp.int32, (BM, 1), 0)
        xa, xb = _unpack_rows_native(jnp.where(row < nvalid_ref[i], x_ref[...], 0))
        xa = xa.astype(jnp.bfloat16)
        xb = xb.astype(jnp.bfloat16)
        half = xa.shape[1]
        g = _bdot(xa, wg_s[:half, :]) + _bdot(xb, wg_s[half:, :])
        u = _bdot(xa, wu_s[:half, :]) + _bdot(xb, wu_s[half:, :])
        a = (g * jax.nn.sigmoid(g)) * u
        y_ref[...] = _pack_rows_native(_bdot(a.astype(jnp.bfloat16), wd_s[...]))

    @pl.when(i >= nused_ref[0])
    def _():
        y_ref[...] = jnp.zeros_like(y_ref)


def _expert_call(x_sorted, blk_e, n_valid, n_used, w_gate, w_up, w_down):
    n_pad, DH = x_sorted.shape
    n_blk = n_pad // BM
    E, D, F = w_gate.shape

    def row_map(i, be, nv, nu):
        return (jnp.minimum(i, nu[0] - 1), 0)

    def w_map(i, be, nv, nu):
        return (be[jnp.minimum(i, nu[0] - 1)], 0, 0)

    return pl.pallas_call(
        _expert_kernel,
        grid_spec=pltpu.PrefetchScalarGridSpec(
            num_scalar_prefetch=3,
            grid=(n_blk,),
            in_specs=[pl.BlockSpec((BM, DH), row_map),
                      pl.BlockSpec((1, D, F), w_map),
                      pl.BlockSpec((1, D, F), w_map),
                      pl.BlockSpec((1, F, D), w_map)],
            out_specs=pl.BlockSpec((BM, DH), lambda i, be, nv, nu: (i, 0)),
            scratch_shapes=[pltpu.VMEM((D, F), jnp.bfloat16), pltpu.VMEM((D, F), jnp.bfloat16),
                            pltpu.VMEM((F, D), jnp.bfloat16)]),
        out_shape=jax.ShapeDtypeStruct((n_pad, DH), jnp.int32),
        compiler_params=pltpu.CompilerParams(dimension_semantics=("arbitrary",),
                                             vmem_limit_bytes=VMEM_LIMIT),
    )(blk_e, n_valid, n_used, x_sorted, w_gate, w_up, w_down)


TC = 128


SC_CORES = 2
SC_SUBCORES = 16
SC_CHUNK = 128


def _sc_gather_call(table, idx):
    n_idx = idx.shape[0]
    width = table.shape[1]
    n_workers = SC_CORES * SC_SUBCORES
    per_worker = n_idx // n_workers
    assert per_worker * n_workers == n_idx and per_worker % SC_CHUNK == 0
    mesh = plsc.VectorSubcoreMesh(core_axis_name="c", subcore_axis_name="s")

    def body(table_hbm, idx_hbm, out_hbm, idx_v, rows_v, sem):
        wid = lax.axis_index("s") * SC_CORES + lax.axis_index("c")
        base = wid * per_worker

        @pl.loop(0, per_worker // SC_CHUNK)
        def _(ch):
            off = base + ch * SC_CHUNK
            pltpu.sync_copy(idx_hbm.at[pl.ds(off, SC_CHUNK)], idx_v)
            pltpu.async_copy(table_hbm.at[idx_v], rows_v, sem).wait()
            pltpu.sync_copy(rows_v, out_hbm.at[pl.ds(off, SC_CHUNK)])

    return pl.kernel(
        body, out_type=jax.ShapeDtypeStruct((n_idx, width), table.dtype), mesh=mesh,
        scratch_types=[pltpu.VMEM((SC_CHUNK,), jnp.int32), pltpu.VMEM((SC_CHUNK, width), table.dtype),
                       pltpu.SemaphoreType.DMA],
    )(table, idx)


def _sc_scatter_call(src, pos, n_out):
    n_idx = pos.shape[0]
    width = src.shape[1]
    n_workers = SC_CORES * SC_SUBCORES
    per_worker = n_idx // n_workers
    assert per_worker * n_workers == n_idx and per_worker % SC_CHUNK == 0 and TR % SC_CHUNK == 0
    mesh = plsc.VectorSubcoreMesh(core_axis_name="c", subcore_axis_name="s")
    tile_pairs = TOP_K * TR

    def body(src_hbm, pos_hbm, out_hbm, idx_v, rows_v, sem):
        wid = lax.axis_index("s") * SC_CORES + lax.axis_index("c")
        base = wid * per_worker

        @pl.loop(0, per_worker // SC_CHUNK)
        def _(ch):
            off = base + ch * SC_CHUNK
            row0 = (off // tile_pairs) * TR + off % TR
            pltpu.sync_copy(pos_hbm.at[pl.ds(off, SC_CHUNK)], idx_v)
            pltpu.sync_copy(src_hbm.at[pl.ds(row0, SC_CHUNK)], rows_v)
            pltpu.async_copy(rows_v, out_hbm.at[idx_v], sem).wait()

    return pl.kernel(
        body, out_type=jax.ShapeDtypeStruct((n_out, width), src.dtype), mesh=mesh,
        scratch_types=[pltpu.VMEM((SC_CHUNK,), jnp.int32), pltpu.VMEM((SC_CHUNK, width), src.dtype),
                       pltpu.SemaphoreType.DMA],
    )(src, pos)


def _positions_kernel(idx_ref, rank_ref, pstart_ref, pos_ref):
    erow = lax.broadcasted_iota(jnp.int32, (N_EXPERTS, TR), 0)
    pstart = pstart_ref[...]
    for k in range(TOP_K):
        hit = erow == idx_ref[0, k:k + 1, :]
        seg = jnp.sum(jnp.where(hit, pstart, 0), axis=0, keepdims=True)
        pos_ref[0, k:k + 1, :] = seg + rank_ref[0, k:k + 1, :]


def _positions_call(idx, rank, pad_start):
    n_tiles = idx.shape[0]
    row_spec = pl.BlockSpec((1, TOP_K, TR), lambda i: (i, 0, 0))
    return pl.pallas_call(
        _positions_kernel,
        grid=(n_tiles,),
        in_specs=[row_spec, row_spec, pl.BlockSpec((N_EXPERTS, 1), lambda i: (0, 0))],
        out_specs=row_spec,
        out_shape=jax.ShapeDtypeStruct((n_tiles, TOP_K, TR), jnp.int32),
    )(idx, rank, pad_start.reshape(N_EXPERTS, 1))


def _combine_dense_kernel(g_ref, w_ref, x1_ref, sh_ref, mod_ref, fg_ref, outc_ref, outl_ref, *, n_ctx_tiles):
    w = w_ref[...]
    acc_a = acc_b = None
    for k in range(TOP_K):
        ya, yb = _unpack_rows_native(g_ref[0, k])
        acc_a = w[:, k:k + 1] * ya if k == 0 else acc_a + w[:, k:k + 1] * ya
        acc_b = w[:, k:k + 1] * yb if k == 0 else acc_b + w[:, k:k + 1] * yb
    acc = jnp.concatenate([acc_a, acc_b], axis=1)
    x2 = x1_ref[...] + mod_ref[0, 5:6, :] * (acc + sh_ref[...])
    out = x2 * lax.rsqrt(jnp.mean(x2 * x2, axis=-1, keepdims=True) + EPS) * fg_ref[...]
    is_ctx = pl.program_id(0) < n_ctx_tiles

    @pl.when(is_ctx)
    def _():
        outc_ref[...] = out

    @pl.when(jnp.logical_not(is_ctx))
    def _():
        outl_ref[...] = out


def _combine_dense_call(gathered, wts, x1, shared, mod, final_g, n_ctx, lat_len):
    T, K = wts.shape
    DH = gathered.shape[-1]
    D = 2 * DH
    row = functools.partial(_mod_row, tokens_per_tile=TC, n_ctx=n_ctx, lat_len=lat_len)
    nct = n_ctx // TC
    return pl.pallas_call(
        functools.partial(_combine_dense_kernel, n_ctx_tiles=nct),
        grid=(T // TC,),
        in_specs=[pl.BlockSpec((1, K, TC, DH), lambda j: (j, 0, 0, 0)),
                  pl.BlockSpec((TC, K), lambda j: (j, 0)),
                  pl.BlockSpec((TC, D), lambda j: (j, 0)),
                  pl.BlockSpec((TC, D), lambda j: (j, 0)),
                  pl.BlockSpec((1, N_MOD, D), lambda j: (row(j), 0, 0)),
                  pl.BlockSpec((1, D), lambda j: (0, 0))],
        out_specs=[pl.BlockSpec((TC, D), lambda j: (jnp.minimum(j, nct - 1), 0)),
                   pl.BlockSpec((TC, D), lambda j: (jnp.maximum(j - nct, 0), 0))],
        out_shape=[jax.ShapeDtypeStruct((n_ctx, D), jnp.float32), jax.ShapeDtypeStruct((T - n_ctx, D), jnp.float32)],
        compiler_params=pltpu.CompilerParams(dimension_semantics=("arbitrary",)),
    )(gathered, wts, x1, shared, mod, final_g.reshape(1, D))


def _moe_routed(h2, h2p, router_w, router_bias, w_gate, w_up, w_down):
    T, D = h2.shape
    idx, rank, w_rows, cnt = _route_call(h2, router_w, router_bias)
    wts = w_rows.transpose(0, 2, 1).reshape(T, TOP_K)
    counts = cnt[:, 0]
    padded = (counts + BM - 1) // BM * BM
    pad_end = jnp.cumsum(padded)
    pad_start = (pad_end - padded).astype(jnp.int32)
    n_pad = T * TOP_K + N_EXPERTS * BM
    n_blk = n_pad // BM
    n_used = (pad_end[-1] // BM).astype(jnp.int32).reshape(1)
    pos = _positions_call(idx, rank, pad_start)
    x_sorted = _sc_scatter_call(h2p, pos.reshape(-1), n_pad)
    blk_row0 = jnp.arange(n_blk, dtype=jnp.int32) * BM
    blk_e = jnp.minimum(jnp.sum((pad_end[None, :] <= blk_row0[:, None]).astype(jnp.int32), axis=1), N_EXPERTS - 1)
    own = blk_e[:, None] == jnp.arange(N_EXPERTS, dtype=jnp.int32)[None, :]
    seg_end = jnp.sum(jnp.where(own, (pad_start + counts)[None, :], 0), axis=1)
    n_valid = jnp.clip(seg_end - blk_row0, 0, BM).astype(jnp.int32)
    y = _expert_call(x_sorted, blk_e, n_valid, n_used, w_gate, w_up, w_down)
    pos_t = pos.reshape(T // TR, TOP_K, TR // TC, TC).transpose(0, 2, 1, 3).reshape(T // TC, TOP_K * TC)
    return y, pos_t, wts


def kernel(x_prompt, x_sample, state_delta, c, c_ctx, w_ada, b_ada, norm1_g, w_in, conv_w, a_log,
           dt_bias, onorm_g, pool_w, pool_scale, w_out, norm2_g, router_w, router_bias, exp_w_gate,
           exp_w_up, exp_w_down, sh_w_gate, sh_w_up, sh_w_down, final_g):
    Bc, Lc, D = x_prompt.shape
    Bl, Ll, _ = x_sample.shape
    n_ctx = Bc * Lc
    assert DEPTH == 1 and 1 + Bl <= MOD_ROWS and n_ctx % Ll == 0
    x_parts = (x_prompt.reshape(n_ctx, D), x_sample.reshape(Bl * Ll, D))
    cvec = jnp.concatenate([c_ctx[None], c, jnp.zeros((MOD_ROWS - 1 - Bl, D), c.dtype)], axis=0)
    l = 0
    mod = _ada_call(cvec, w_ada[l], b_ada[l]).reshape(MOD_ROWS, N_MOD, D)
    qkv, z, ba, u = _inproj_call(*x_parts, mod, norm1_g[l], w_in[l], Ll)
    dn = (conv_w[l], a_log[l], dt_bias[l], onorm_g[l])
    oa_c, st_ctx = _delta_call(qkv, z, ba, *dn, None, Bc, Lc, 0)
    oa_l, _ = _delta_call(qkv, z, ba, *dn, state_delta[:, l], Bl, Ll, n_ctx // Ll)
    op_c = _pool_call(u, pool_w[l], pool_scale[l], False, Bc, Lc, 0)
    op_l = _pool_call(u, pool_w[l], pool_scale[l], True, Bl, Ll, n_ctx // Ll)
    x1, h2, h2p, shared = _outproj_call(x_parts, (oa_c, oa_l), (op_c, op_l), mod, norm2_g[l], w_out[l],
                                        sh_w_gate[l], sh_w_up[l], sh_w_down[l], Ll)
    y, pos_t, wts = _moe_routed(h2, h2p, router_w[l], router_bias[l], exp_w_gate[l], exp_w_up[l],
                                exp_w_down[l])
    T = n_ctx + Bl * Ll
    gathered = _sc_gather_call(y, pos_t.reshape(-1))
    out_c, out_l = _combine_dense_call(gathered.reshape(T // TC, TOP_K, TC, D // 2), wts, x1, shared, mod,
                                       final_g, n_ctx, Ll)
    y_prompt = out_c.reshape(Bc, Lc, D)
    y_sample = out_l.reshape(Bl, Ll, D)
    new_state_delta = st_ctx[:, None].astype(x_prompt.dtype)
    return (y_prompt, y_sample, new_state_delta)
```

```python
import functools
import jax, jax.numpy as jnp
from jax import lax
from jax.experimental import pallas as pl
from jax.experimental.pallas import tpu as pltpu
from jax.experimental.pallas import tpu_sc as plsc

D_MODEL = 1024
DEPTH = 1
GRID_W = 64
D_MIX = D_MODEL
D_A = D_MIX // 2
D_P = D_MIX - D_A
H_A = 4
DK = D_A // H_A
DV = D_A // H_A
CONV_K = 5
CHUNK = 64
POOL_WINDOWS = (2, 4, 8, 16)
N_PG = len(POOL_WINDOWS)
PG = D_P // N_PG
N_EXPERTS = 256
TOP_K = 8
N_GROUPS = 8
TOPK_GROUP = 4
ROUTED_SCALE = 2.5
EPS = 1e-6
VMEM_LIMIT = 48 * 1024 * 1024


def _split_bf16(a):
    hi = a.astype(jnp.bfloat16)
    return hi, (a - hi.astype(jnp.float32)).astype(jnp.bfloat16)


def _bdot(a, b):
    return jnp.dot(a, b, preferred_element_type=jnp.float32)


def _pack_rows(x):
    m = x.shape[1] // 2
    lo = lax.bitcast_convert_type(x[:, :m].astype(jnp.bfloat16).astype(jnp.float32), jnp.uint32)
    hi = lax.bitcast_convert_type(x[:, m:].astype(jnp.bfloat16).astype(jnp.float32), jnp.uint32)
    return lax.bitcast_convert_type(hi | (lo >> 16), jnp.int32)


def _pack_rows_native(x):
    m = x.shape[1] // 2
    return pltpu.pack_elementwise([x[:, :m], x[:, m:]], packed_dtype=jnp.bfloat16)


def _unpack_rows_native(p):
    return tuple(pltpu.unpack_elementwise(p, index=i, packed_dtype=jnp.bfloat16, unpacked_dtype=jnp.float32)
                 for i in range(2))


N_MOD = 6
MOD_ROWS = 8
TM = 512


def _ada_kernel(c_ref, w_ref, b_ref, o_ref):
    c = c_ref[...]
    s = c * jax.nn.sigmoid(c)
    sh, sl = _split_bf16(s)
    wh, wl = _split_bf16(w_ref[...])
    o_ref[...] = _bdot(sh, wh) + (_bdot(sh, wl) + _bdot(sl, wh)) + b_ref[...]


def _ada_call(cvec, w_ada, b_ada):
    R, D = cvec.shape
    N = w_ada.shape[1]
    tn = 1024
    return pl.pallas_call(
        _ada_kernel,
        grid=(N // tn,),
        in_specs=[pl.BlockSpec((R, D), lambda j: (0, 0)),
                  pl.BlockSpec((D, tn), lambda j: (0, j)),
                  pl.BlockSpec((1, tn), lambda j: (0, j))],
        out_specs=pl.BlockSpec((R, tn), lambda j: (0, j)),
        out_shape=jax.ShapeDtypeStruct((R, N), jnp.float32),
    )(cvec, w_ada, b_ada.reshape(1, N))


def _mod_row(tile, tokens_per_tile, n_ctx, lat_len):
    t0 = tile * tokens_per_tile
    return jnp.where(t0 < n_ctx, 0, 1 + (t0 - n_ctx) // lat_len)


def _two_part_specs(n_ctx_tiles, width):
    return (pl.BlockSpec((TM, width), lambda i: (jnp.minimum(i, n_ctx_tiles - 1), 0)),
            pl.BlockSpec((TM, width), lambda i: (jnp.maximum(i - n_ctx_tiles, 0), 0)))


def _pick(n_ctx_tiles, ctx_ref, lat_ref):
    return jnp.where(pl.program_id(0) < n_ctx_tiles, ctx_ref[...], lat_ref[...])


def _inproj_kernel(xc_ref, xl_ref, mod_ref, g_ref, wq_ref, wz_ref, wb_ref, wu_ref, q_ref, z_ref, b_ref, u_ref,
                   *, n_ctx_tiles):
    x = _pick(n_ctx_tiles, xc_ref, xl_ref)
    y = x * lax.rsqrt(jnp.mean(x * x, axis=-1, keepdims=True) + EPS) * g_ref[...]
    h = (y * (1.0 + mod_ref[0, 1:2, :]) + mod_ref[0, 0:1, :]).astype(jnp.bfloat16)
    q_ref[...] = _bdot(h, wq_ref[...])
    z_ref[...] = _bdot(h, wz_ref[...])
    b_ref[...] = _bdot(h, wb_ref[...])
    u_ref[...] = _bdot(h, wu_ref[...])


def _inproj_call(x_ctx, x_lat, mod, norm1_g, w_in, lat_len):
    n_ctx, D = x_ctx.shape
    T = n_ctx + x_lat.shape[0]
    bf = jnp.bfloat16
    nq, nz, nb = 3 * D_A, D_A, 4 * H_A
    wq = w_in[:, :nq].astype(bf)
    wz = w_in[:, nq:nq + nz].astype(bf)
    wb = jnp.pad(w_in[:, nq + nz:nq + nz + nb], ((0, 0), (0, 128 - nb))).astype(bf)
    wu = w_in[:, nq + nz + nb:].astype(bf)
    row = functools.partial(_mod_row, tokens_per_tile=TM, n_ctx=n_ctx, lat_len=lat_len)

    def full(a):
        return pl.BlockSpec(a.shape, lambda i: (0, 0))

    def rows(n):
        return pl.BlockSpec((TM, n), lambda i: (i, 0))

    return pl.pallas_call(
        functools.partial(_inproj_kernel, n_ctx_tiles=n_ctx // TM),
        grid=(T // TM,),
        in_specs=[*_two_part_specs(n_ctx // TM, D), pl.BlockSpec((1, N_MOD, D), lambda i: (row(i), 0, 0)),
                  pl.BlockSpec((1, D), lambda i: (0, 0)), full(wq), full(wz), full(wb), full(wu)],
        out_specs=[rows(nq), rows(nz), rows(128), rows(D_P)],
        out_shape=[jax.ShapeDtypeStruct((T, nq), jnp.float32), jax.ShapeDtypeStruct((T, nz), jnp.float32),
                   jax.ShapeDtypeStruct((T, 128), jnp.float32), jax.ShapeDtypeStruct((T, D_P), jnp.float32)],
        compiler_params=pltpu.CompilerParams(dimension_semantics=("arbitrary",),
                                             vmem_limit_bytes=VMEM_LIMIT),
    )(x_ctx, x_lat, mod, norm1_g.reshape(1, D), wq, wz, wb, wu)


PT = 256


def _window_bounds(pos, w, n):
    return jnp.maximum(pos - w // 2, 0), jnp.minimum(pos + w - w // 2, n)


def _band_sum(band, x):
    xh, xl = _split_bf16(x)
    return _bdot(band, xh) + _bdot(band, xl)


def _pool_seq_kernel(u_ref, pw_ref, ps_ref, o_ref):
    L = u_ref.shape[0]
    ti = lax.broadcasted_iota(jnp.int32, (L, L), 0)
    ji = lax.broadcasted_iota(jnp.int32, (L, L), 1)
    tcol = lax.broadcasted_iota(jnp.int32, (L, 1), 0)
    for i, w in enumerate(POOL_WINDOWS):
        lo, hi = _window_bounds(ti, w, L)
        band = ((ji >= lo) & (ji < hi)).astype(jnp.bfloat16)
        clo, chi = _window_bounds(tcol, w, L)
        ug = u_ref[:, i * PG:(i + 1) * PG]
        mean = _band_sum(band, ug) / (chi - clo).astype(jnp.float32)
        d = (mean - ug).astype(jnp.bfloat16)
        o_ref[:, i * PG:(i + 1) * PG] = _bdot(d, pw_ref[i]) * ps_ref[:, i * PG:(i + 1) * PG]


def _pool_grid_kernel(u_ref, pw_ref, ps_ref, o_ref, pad_s, r_s):
    L = u_ref.shape[0]
    rows = L // GRID_W
    halo = (max(POOL_WINDOWS) // 2) * GRID_W
    pad_s[0:halo, :] = jnp.zeros((halo, D_P), jnp.float32)
    pad_s[halo + L:, :] = jnp.zeros((halo, D_P), jnp.float32)
    pad_s[halo:halo + L, :] = u_ref[...]
    ti = lax.broadcasted_iota(jnp.int32, (PT, PT), 0)
    ji = lax.broadcasted_iota(jnp.int32, (PT, PT), 1)
    tcol = lax.broadcasted_iota(jnp.int32, (PT, 1), 0)
    for i, w in enumerate(POOL_WINDOWS):
        cs = slice(i * PG, (i + 1) * PG)
        acc = None
        for dr in range(-(w // 2), w - w // 2):
            part = pad_s[halo + dr * GRID_W:halo + dr * GRID_W + L, cs]
            acc = part if acc is None else acc + part
        r_s[...] = acc
        lo, hi = _window_bounds(ti % GRID_W, w, GRID_W)
        band = ((ji // GRID_W == ti // GRID_W) & (ji % GRID_W >= lo) & (ji % GRID_W < hi)).astype(jnp.bfloat16)
        clo, chi = _window_bounds(tcol % GRID_W, w, GRID_W)
        ccnt = (chi - clo).astype(jnp.float32)
        for tile in range(L // PT):
            ts = slice(tile * PT, (tile + 1) * PT)
            rlo, rhi = _window_bounds(tile * (PT // GRID_W) + tcol // GRID_W, w, rows)
            mean = _band_sum(band, r_s[ts, :]) / ((rhi - rlo).astype(jnp.float32) * ccnt)
            d = (mean - u_ref[ts, cs]).astype(jnp.bfloat16)
            o_ref[ts, cs] = _bdot(d, pw_ref[i]) * ps_ref[:, cs]


def _pool_call(u, pool_w, pool_scale, grid, B, L, row_blk0):
    pw = pool_w.astype(jnp.bfloat16)
    ps = pool_scale.reshape(1, D_P)
    specs = dict(
        grid=(B,),
        in_specs=[pl.BlockSpec((L, D_P), lambda b: (row_blk0 + b, 0)),
                  pl.BlockSpec((N_PG, PG, PG), lambda b: (0, 0, 0)),
                  pl.BlockSpec((1, D_P), lambda b: (0, 0))],
        out_specs=pl.BlockSpec((L, D_P), lambda b: (b, 0)),
        out_shape=jax.ShapeDtypeStruct((B * L, D_P), jnp.float32),
        compiler_params=pltpu.CompilerParams(dimension_semantics=("arbitrary",),
                                             vmem_limit_bytes=VMEM_LIMIT))
    if not grid:
        return pl.pallas_call(_pool_seq_kernel, **specs)(u, pw, ps)
    halo = (max(POOL_WINDOWS) // 2) * GRID_W
    return pl.pallas_call(
        _pool_grid_kernel,
        scratch_shapes=[pltpu.VMEM((L + 2 * halo, D_P), jnp.float32), pltpu.VMEM((L, PG), jnp.float32)],
        **specs)(u, pw, ps)


def _outproj_kernel(xc_ref, xl_ref, oac_ref, oal_ref, opc_ref, opl_ref, mod_ref, g2_ref, wo_ref, sg_ref, su_ref,
                    sd_ref, x1_ref, h2_ref, h2p_ref, sh_ref, *, n_ctx_tiles):
    o_a = _pick(n_ctx_tiles, oac_ref, oal_ref)
    o_p = _pick(n_ctx_tiles, opc_ref, opl_ref)
    mix = (_bdot(o_a.astype(jnp.bfloat16), wo_ref[:D_A, :])
           + _bdot(o_p.astype(jnp.bfloat16), wo_ref[D_A:, :]))
    x1 = _pick(n_ctx_tiles, xc_ref, xl_ref) + mod_ref[0, 2:3, :] * mix
    x1_ref[...] = x1
    y = x1 * lax.rsqrt(jnp.mean(x1 * x1, axis=-1, keepdims=True) + EPS) * g2_ref[...]
    h2 = y * (1.0 + mod_ref[0, 4:5, :]) + mod_ref[0, 3:4, :]
    h2_ref[...] = h2
    h2p_ref[...] = _pack_rows(h2)
    hb = h2.astype(jnp.bfloat16)
    g = _bdot(hb, sg_ref[...])
    a = (g * jax.nn.sigmoid(g)) * _bdot(hb, su_ref[...])
    sh_ref[...] = _bdot(a.astype(jnp.bfloat16), sd_ref[...])


def _outproj_call(x_parts, oa_parts, op_parts, mod, norm2_g, w_out, sh_gate, sh_up, sh_down, lat_len):
    n_ctx, D = x_parts[0].shape
    T = n_ctx + x_parts[1].shape[0]
    nct = n_ctx // TM
    bf = jnp.bfloat16
    row = functools.partial(_mod_row, tokens_per_tile=TM, n_ctx=n_ctx, lat_len=lat_len)
    ws = [w_out.astype(bf), sh_gate.astype(bf), sh_up.astype(bf), sh_down.astype(bf)]

    def rows(n):
        return pl.BlockSpec((TM, n), lambda i: (i, 0))

    return pl.pallas_call(
        functools.partial(_outproj_kernel, n_ctx_tiles=nct),
        grid=(T // TM,),
        in_specs=[*_two_part_specs(nct, D), *_two_part_specs(nct, D_A), *_two_part_specs(nct, D_P),
                  pl.BlockSpec((1, N_MOD, D), lambda i: (row(i), 0, 0)),
                  pl.BlockSpec((1, D), lambda i: (0, 0))] + [pl.BlockSpec(w.shape, lambda i: (0, 0)) for w in ws],
        out_specs=[rows(D), rows(D), rows(D // 2), rows(D)],
        out_shape=[jax.ShapeDtypeStruct((T, D), jnp.float32), jax.ShapeDtypeStruct((T, D), jnp.float32),
                   jax.ShapeDtypeStruct((T, D // 2), jnp.int32), jax.ShapeDtypeStruct((T, D), jnp.float32)],
        compiler_params=pltpu.CompilerParams(dimension_semantics=("arbitrary",),
                                             vmem_limit_bytes=VMEM_LIMIT),
    )(*x_parts, *oa_parts, *op_parts, mod, norm2_g.reshape(1, D), *ws)


SC = 256
CPS = SC // CHUNK
BASE = 16
DELTA_HEAD_ROWS = 4096
M_TRI, M_SAME, M_EYE, M_BASE, M_MERGE, N_MASKS = 0, 2, 3, 4, 5, 7


def _mm(a, b):
    return jnp.dot(a.astype(jnp.bfloat16), b.astype(jnp.bfloat16), preferred_element_type=jnp.float32)


def _mm_nt(a, b):
    return lax.dot_general(a.astype(jnp.bfloat16), b.astype(jnp.bfloat16), (((1,), (1,)), ((), ())),
                           preferred_element_type=jnp.float32)


def _softplus(x):
    return jnp.maximum(x, 0.0) + jnp.log(1.0 + jnp.exp(-jnp.abs(x)))


def _delta_kernel(sc_ref, xq_ref, xk_ref, xv_ref, z_ref, bac_ref, bar_ref, cwq_ref, cwk_ref, cwv_ref,
                  og_ref, s0_ref, o_ref, st_ref, q_s, k_s, v_s, o_s, vn_s, mask_s, *, n_sc, zero_init, hpb):
    hb = pl.program_id(1)
    L = q_s.shape[1]

    def conv(x_ref, w_ref, cs):
        x = x_ref[:, cs]
        row = lax.broadcasted_iota(jnp.int32, x.shape, 0)
        acc = x * w_ref[CONV_K // 2:CONV_K // 2 + 1, cs]
        for j in range(CONV_K):
            d = j - CONV_K // 2
            if d == 0:
                continue
            xs = pltpu.roll(x, (-d) % L, 0)
            ok = (row + d >= 0) & (row + d < L)
            acc = acc + jnp.where(ok, xs, 0.0) * w_ref[j:j + 1, cs]
        return acc * jax.nn.sigmoid(acc)

    for hh in range(hpb):
        cs = slice(hh * DK, (hh + 1) * DK)
        q = conv(xq_ref, cwq_ref, cs)
        q_s[hh] = q * lax.rsqrt(jnp.sum(q * q, axis=-1, keepdims=True) + EPS) * (DK ** -0.5)
        k = conv(xk_ref, cwk_ref, cs)
        k_s[hh] = k * lax.rsqrt(jnp.sum(k * k, axis=-1, keepdims=True) + EPS)
        v_s[hh] = conv(xv_ref, cwv_ref, cs)
    o_s[...] = jnp.zeros_like(o_s)

    @pl.when((pl.program_id(0) == 0) & (hb == 0))
    def _():
        ri = lax.broadcasted_iota(jnp.int32, (SC, SC), 0)
        ci = lax.broadcasted_iota(jnp.int32, (SC, SC), 1)
        blocks = [(ri // w) == (ci // w) for w in (BASE, 2 * BASE, CHUNK)]
        f32 = jnp.float32
        mask_s[M_TRI] = (blocks[2] & (ci <= ri)).astype(f32)
        mask_s[M_TRI + 1] = (blocks[2] & (ci >= ri)).astype(f32)
        mask_s[M_SAME] = blocks[2].astype(f32)
        mask_s[M_EYE] = (ri == ci).astype(f32)
        mask_s[M_BASE] = blocks[0].astype(f32)
        for lvl in range(2):
            mask_s[M_MERGE + lvl] = (blocks[lvl + 1] & ~blocks[lvl]).astype(f32)

    rowi = lax.broadcasted_iota(jnp.int32, (SC, DV), 0)

    def prep(m, d, hh):
        r0 = pl.multiple_of(m * SC, SC)
        h = hb * hpb + hh
        q = q_s[hh, pl.ds(r0, SC), :]
        k = k_s[hh, pl.ds(r0, SC), :]
        v = v_s[hh, pl.ds(r0, SC), :]
        bc = bac_ref[0, hh, pl.ds(r0, SC), :]
        br = bar_ref[0, hh, m]
        a_l = sc_ref[d * H_A + h]
        dtb = sc_ref[2 * H_A + d * H_A + h]
        neg_ea = -jnp.exp(jnp.full((1, 1), a_l, jnp.float32))
        beta = jax.nn.sigmoid(bc[:, d:d + 1])
        g_col = neg_ea * _softplus(bc[:, 2 + d:3 + d] + dtb)
        g_row = neg_ea * _softplus(br[2 + d:3 + d, :] + dtb)
        tri = mask_s[M_TRI + d]
        tri_t = mask_s[M_TRI + 1 - d]
        gc_col = jnp.sum(tri * g_row, axis=1, keepdims=True)
        gc_row = jnp.sum(tri_t * g_col, axis=0, keepdims=True)
        gl_col = jnp.sum(mask_s[M_SAME] * g_row, axis=1, keepdims=True)
        decay = jnp.exp((gc_col - gc_row) * tri) * tri
        kb = k * beta
        a = _mm_nt(kb, k) * (decay - mask_s[M_EYE])
        attn = _mm_nt(q, k) * decay
        eg = jnp.exp(gc_col)
        x = jnp.concatenate([v * beta, kb * eg], axis=1)
        qd = q * eg
        kdt = (k * jnp.exp(gl_col - gc_col)).T
        return dict(r0=r0, a=a, attn=attn, x=x, qd=qd, kdt=kdt, egl=jnp.exp(gl_col))

    def run_chains(ms, states):
        n = len(chains)
        ops = [prep(ms[i], d, hh) for i, (hh, d) in enumerate(chains)]
        ps = [o["a"] * mask_s[M_BASE] for o in ops]
        ts = [mask_s[M_EYE] - p for p in ps]
        for _ in range(BASE.bit_length() - 2):
            ps = [_mm(p, p) for p in ps]
            ts = [t + _mm(t, p) for t, p in zip(ts, ps)]
        for lvl in range(2):
            lows = [_mm(o["a"] * mask_s[M_MERGE + lvl], t) for o, t in zip(ops, ts)]
            ts = [t - _mm(t, low) for t, low in zip(ts, lows)]
        xs = [_mm(t, o["x"]) for t, o in zip(ts, ops)]
        for i in range(n):
            vn_s[i] = jnp.zeros((SC, DV), jnp.float32)
        states = list(states)
        for step in range(CPS):
            cs = [step if d == 0 else CPS - 1 - step for _, d in chains]
            los = [c * CHUNK for c in cs]
            ws_qs = [_mm(jnp.concatenate([x[lo:lo + CHUNK, DV:], o["qd"][lo:lo + CHUNK]], axis=0), s)
                     for x, o, lo, s in zip(xs, ops, los, states)]
            for i in range(n):
                vn_s[i, los[i]:los[i] + CHUNK, :] = xs[i][los[i]:los[i] + CHUNK, :DV] - ws_qs[i][:CHUNK]
            vns = [vn_s[i] for i in range(n)]
            o_cs = [wq[CHUNK:] + _mm(o["attn"][lo:lo + CHUNK, :], vn)
                    for wq, o, lo, vn in zip(ws_qs, ops, los, vns)]
            for i, (hh, _) in enumerate(chains):
                o_s[hh, pl.ds(ops[i]["r0"] + los[i], CHUNK), :] += o_cs[i]
            states = [s * o["egl"][lo:lo + 1, :]
                      + _mm(o["kdt"], jnp.where((rowi >= lo) & (rowi < lo + CHUNK), vn, 0.0))
                      for s, o, lo, vn in zip(states, ops, los, vns)]
        return tuple(states)

    if zero_init:
        states = tuple(jnp.zeros((DK, DV), jnp.float32) for _ in range(2 * hpb))
    else:
        states = tuple(s0_ref[0, d, hh] for hh in range(hpb) for d in range(2))

    chains = [(hh, d) for hh in range(hpb) for d in range(2)]

    def body(m, carry):
        return run_chains([m if d == 0 else n_sc - 1 - m for _, d in chains], carry)

    if n_sc == 1:
        states = body(0, states)
    else:
        states = lax.fori_loop(0, n_sc, body, states)

    for hh in range(hpb):
        for d in range(2):
            st_ref[0, d, hh] = states[2 * hh + d]
        o = o_s[hh]
        o = o * lax.rsqrt(jnp.mean(o * o, axis=-1, keepdims=True) + EPS) * og_ref[...]
        zz = z_ref[:, hh * DV:(hh + 1) * DV]
        o_ref[:, hh * DV:(hh + 1) * DV] = o * (zz * jax.nn.sigmoid(zz))


def _delta_call(qkv, z, ba, conv_w, a_log, dt_bias, onorm_g, s0, B, L, row_blk0):
    n_sc = L // SC
    t0 = row_blk0 * L
    bah = ba[t0:t0 + B * L, :4 * H_A].reshape(B, L, 4, H_A).transpose(0, 3, 1, 2)
    bar = bah.reshape(B, H_A, n_sc, SC, 4).transpose(0, 1, 2, 4, 3)
    scal = jnp.concatenate([a_log.reshape(-1), dt_bias.reshape(-1)]).astype(jnp.float32)
    hpb = max(1, min(H_A, DELTA_HEAD_ROWS // L))
    n_hb = H_A // hpb
    zero_init = s0 is None
    if zero_init:
        s0 = jnp.zeros((1, 2, hpb, DK, DV), jnp.float32)
        s0_spec = pl.BlockSpec((1, 2, hpb, DK, DV), lambda b, h, sc: (0, 0, 0, 0, 0))
    else:
        s0_spec = pl.BlockSpec((1, 2, hpb, DK, DV), lambda b, h, sc: (b, 0, h, 0, 0))

    def col(off):
        return pl.BlockSpec((L, hpb * DK), lambda b, h, sc: (row_blk0 + b, off * n_hb + h))

    def cw(off):
        return pl.BlockSpec((CONV_K, hpb * DK), lambda b, h, sc: (0, off * n_hb + h))

    kern = functools.partial(_delta_kernel, n_sc=n_sc, zero_init=zero_init, hpb=hpb)
    return pl.pallas_call(
        kern,
        grid_spec=pltpu.PrefetchScalarGridSpec(
            num_scalar_prefetch=1,
            grid=(B, n_hb),
            in_specs=[col(0), col(1), col(2),
                      pl.BlockSpec((L, hpb * DV), lambda b, h, sc: (row_blk0 + b, h)),
                      pl.BlockSpec((1, hpb, L, 4), lambda b, h, sc: (b, h, 0, 0)),
                      pl.BlockSpec((1, hpb, n_sc, 4, SC), lambda b, h, sc: (b, h, 0, 0, 0)),
                      cw(0), cw(1), cw(2),
                      pl.BlockSpec((1, DV), lambda b, h, sc: (0, 0)),
                      s0_spec],
            out_specs=[pl.BlockSpec((L, hpb * DV), lambda b, h, sc: (b, h)),
                       pl.BlockSpec((1, 2, hpb, DK, DV), lambda b, h, sc: (b, 0, h, 0, 0))],
            scratch_shapes=[pltpu.VMEM((hpb, L, DK), jnp.float32), pltpu.VMEM((hpb, L, DK), jnp.float32),
                            pltpu.VMEM((hpb, L, DV), jnp.float32), pltpu.VMEM((hpb, L, DV), jnp.float32),
                            pltpu.VMEM((2 * hpb, SC, DV), jnp.float32),
                            pltpu.VMEM((N_MASKS, SC, SC), jnp.float32)]),
        out_shape=[jax.ShapeDtypeStruct((B * L, D_A), jnp.float32),
                   jax.ShapeDtypeStruct((B, 2, H_A, DK, DV), jnp.float32)],
        compiler_params=pltpu.CompilerParams(dimension_semantics=("arbitrary", "arbitrary"),
                                             vmem_limit_bytes=VMEM_LIMIT),
    )(scal, qkv, qkv, qkv, z, bah, bar, conv_w, conv_w, conv_w, onorm_g.reshape(1, DV), s0)


TR = 256
GSZ = N_EXPERTS // N_GROUPS
NEG = -jnp.inf
BM = 512


def _route_kernel(h_ref, rwh_ref, rwl_ref, rb_ref, idx_ref, rank_ref, w_ref, cnt_ref, cnt_s):
    i = pl.program_id(0)

    @pl.when(i == 0)
    def _():
        cnt_s[...] = jnp.zeros_like(cnt_s)

    h = h_ref[...]
    hh, hl = _split_bf16(h)
    logits = _bdot(hh, rwh_ref[...]) + (_bdot(hh, rwl_ref[...]) + _bdot(hl, rwh_ref[...]))
    scores = jax.nn.sigmoid(logits.T)
    sel = scores + rb_ref[...]
    erow = lax.broadcasted_iota(jnp.int32, sel.shape, 0)
    grow = lax.broadcasted_iota(jnp.int32, (GSZ, TR), 0)

    def first_argmax(v, rows):
        m = jnp.max(v, axis=0, keepdims=True)
        first = jnp.min(jnp.where(v == m, rows, N_EXPERTS), axis=0, keepdims=True)
        return m, first

    gs = []
    for g in range(N_GROUPS):
        vg = sel[g * GSZ:(g + 1) * GSZ, :]
        m1, i1 = first_argmax(vg, grow)
        m2 = jnp.max(jnp.where(grow == i1, NEG, vg), axis=0, keepdims=True)
        gs.append(m1 + m2)
    cand = []
    for g in range(N_GROUPS):
        beat = jnp.zeros(gs[g].shape, jnp.int32)
        for o in range(N_GROUPS):
            if o == g:
                continue
            wins = (gs[o] > gs[g]) | ((gs[o] == gs[g]) & (o < g))
            beat = beat + wins.astype(jnp.int32)
        cand.append(jnp.where(beat < TOPK_GROUP, sel[g * GSZ:(g + 1) * GSZ, :], NEG))
    cand = jnp.concatenate(cand, axis=0)
    chosen = []
    picked = jnp.zeros(sel.shape, jnp.bool_)
    for _ in range(TOP_K):
        _, ik = first_argmax(cand, erow)
        hit = erow == ik
        chosen.append((ik, hit))
        picked = picked | hit
        cand = jnp.where(hit, NEG, cand)
    wsum = jnp.sum(jnp.where(picked, scores, 0.0), axis=0, keepdims=True)

    ri = lax.broadcasted_iota(jnp.int32, (TR, TR), 0)
    ci = lax.broadcasted_iota(jnp.int32, (TR, TR), 1)
    earlier = (ri < ci).astype(jnp.bfloat16)
    rank_mat = _bdot(picked.astype(jnp.bfloat16), earlier) + cnt_s[...]
    cnt_s[...] = cnt_s[...] + jnp.sum(picked.astype(jnp.float32), axis=1, keepdims=True)
    cnt_ref[...] = cnt_s[...].astype(jnp.int32)

    for k, (ik, hit) in enumerate(chosen):
        idx_ref[0, k:k + 1, :] = ik
        rank_ref[0, k:k + 1, :] = jnp.sum(jnp.where(hit, rank_mat, 0.0), axis=0, keepdims=True).astype(jnp.int32)
        w_ref[0, k:k + 1, :] = jnp.sum(jnp.where(hit, scores, 0.0), axis=0, keepdims=True) / wsum * ROUTED_SCALE


def _route_call(hf, router_w, router_bias):
    T, D = hf.shape
    n_tiles = T // TR
    rwh, rwl = _split_bf16(router_w)
    row_spec = pl.BlockSpec((1, TOP_K, TR), lambda i: (i, 0, 0))
    return pl.pallas_call(
        _route_kernel,
        grid=(n_tiles,),
        in_specs=[pl.BlockSpec((TR, D), lambda i: (i, 0)),
                  pl.BlockSpec((D, N_EXPERTS), lambda i: (0, 0)),
                  pl.BlockSpec((D, N_EXPERTS), lambda i: (0, 0)),
                  pl.BlockSpec((N_EXPERTS, 1), lambda i: (0, 0))],
        out_specs=[row_spec, row_spec, row_spec, pl.BlockSpec((N_EXPERTS, 1), lambda i: (0, 0))],
        scratch_shapes=[pltpu.VMEM((N_EXPERTS, 1), jnp.float32)],
        out_shape=[jax.ShapeDtypeStruct((n_tiles, TOP_K, TR), jnp.int32),
                   jax.ShapeDtypeStruct((n_tiles, TOP_K, TR), jnp.int32),
                   jax.ShapeDtypeStruct((n_tiles, TOP_K, TR), jnp.float32),
                   jax.ShapeDtypeStruct((N_EXPERTS, 1), jnp.int32)],
        compiler_params=pltpu.CompilerParams(dimension_semantics=("arbitrary",)),
    )(hf, rwh, rwl, router_bias.reshape(N_EXPERTS, 1).astype(jnp.float32))


def _expert_kernel(blk_e_ref, nvalid_ref, nused_ref, x_ref, wg_ref, wu_ref, wd_ref, y_ref, wg_s, wu_s, wd_s):
    i = pl.program_id(0)

    @pl.when(i < nused_ref[0])
    def _():
        e = blk_e_ref[i]
        prev = blk_e_ref[jnp.maximum(i - 1, 0)]

        @pl.when((i == 0) | (e != prev))
        def _():
            wg_s[...] = wg_ref[0].astype(jnp.bfloat16)
            wu_s[...] = wu_ref[0].astype(jnp.bfloat16)
            wd_s[...] = wd_ref[0].astype(jnp.bfloat16)

        row = lax.broadcasted_iota(jnp.int32, (BM, 1), 0)
        xa, xb = _unpack_rows_native(jnp.where(row < nvalid_ref[i], x_ref[...], 0))
        xa = xa.astype(jnp.bfloat16)
        xb = xb.astype(jnp.bfloat16)
        half = xa.shape[1]
        g = _bdot(xa, wg_s[:half, :]) + _bdot(xb, wg_s[half:, :])
        u = _bdot(xa, wu_s[:half, :]) + _bdot(xb, wu_s[half:, :])
        a = (g * jax.nn.sigmoid(g)) * u
        y_ref[...] = _pack_rows_native(_bdot(a.astype(jnp.bfloat16), wd_s[...]))

    @pl.when(i >= nused_ref[0])
    def _():
        y_ref[...] = jnp.zeros_like(y_ref)


def _expert_call(x_sorted, blk_e, n_valid, n_used, w_gate, w_up, w_down):
    n_pad, DH = x_sorted.shape
    n_blk = n_pad // BM
    E, D, F = w_gate.shape

    def row_map(i, be, nv, nu):
        return (jnp.minimum(i, nu[0] - 1), 0)

    def w_map(i, be, nv, nu):
        return (be[jnp.minimum(i, nu[0] - 1)], 0, 0)

    return pl.pallas_call(
        _expert_kernel,
        grid_spec=pltpu.PrefetchScalarGridSpec(
            num_scalar_prefetch=3,
            grid=(n_blk,),
            in_specs=[pl.BlockSpec((BM, DH), row_map),
                      pl.BlockSpec((1, D, F), w_map),
                      pl.BlockSpec((1, D, F), w_map),
                      pl.BlockSpec((1, F, D), w_map)],
            out_specs=pl.BlockSpec((BM, DH), lambda i, be, nv, nu: (i, 0)),
            scratch_shapes=[pltpu.VMEM((D, F), jnp.bfloat16), pltpu.VMEM((D, F), jnp.bfloat16),
                            pltpu.VMEM((F, D), jnp.bfloat16)]),
        out_shape=jax.ShapeDtypeStruct((n_pad, DH), jnp.int32),
        compiler_params=pltpu.CompilerParams(dimension_semantics=("arbitrary",),
                                             vmem_limit_bytes=VMEM_LIMIT),
    )(blk_e, n_valid, n_used, x_sorted, w_gate, w_up, w_down)


TC = 128


SC_CORES = 2
SC_SUBCORES = 16
SC_CHUNK = 128


def _sc_gather_call(table, idx):
    n_idx = idx.shape[0]
    width = table.shape[1]
    n_workers = SC_CORES * SC_SUBCORES
    per_worker = n_idx // n_workers
    assert per_worker * n_workers == n_idx and per_worker % SC_CHUNK == 0
    mesh = plsc.VectorSubcoreMesh(core_axis_name="c", subcore_axis_name="s")

    def body(table_hbm, idx_hbm, out_hbm, idx_v, rows_v, sem):
        wid = lax.axis_index("s") * SC_CORES + lax.axis_index("c")
        base = wid * per_worker

        @pl.loop(0, per_worker // SC_CHUNK)
        def _(ch):
            off = base + ch * SC_CHUNK
            pltpu.sync_copy(idx_hbm.at[pl.ds(off, SC_CHUNK)], idx_v)
            pltpu.async_copy(table_hbm.at[idx_v], rows_v, sem).wait()
            pltpu.sync_copy(rows_v, out_hbm.at[pl.ds(off, SC_CHUNK)])

    return pl.kernel(
        body, out_type=jax.ShapeDtypeStruct((n_idx, width), table.dtype), mesh=mesh,
        scratch_types=[pltpu.VMEM((SC_CHUNK,), jnp.int32), pltpu.VMEM((SC_CHUNK, width), table.dtype),
                       pltpu.SemaphoreType.DMA],
    )(table, idx)


def _sc_scatter_call(src, pos, n_out):
    n_idx = pos.shape[0]
    width = src.shape[1]
    n_workers = SC_CORES * SC_SUBCORES
    per_worker = n_idx // n_workers
    assert per_worker * n_workers == n_idx and per_worker % SC_CHUNK == 0 and TR % SC_CHUNK == 0
    mesh = plsc.VectorSubcoreMesh(core_axis_name="c", subcore_axis_name="s")
    tile_pairs = TOP_K * TR

    def body(src_hbm, pos_hbm, out_hbm, idx_v, rows_v, sem):
        wid = lax.axis_index("s") * SC_CORES + lax.axis_index("c")
        base = wid * per_worker

        @pl.loop(0, per_worker // SC_CHUNK)
        def _(ch):
            off = base + ch * SC_CHUNK
            row0 = (off // tile_pairs) * TR + off % TR
            pltpu.sync_copy(pos_hbm.at[pl.ds(off, SC_CHUNK)], idx_v)
            pltpu.sync_copy(src_hbm.at[pl.ds(row0, SC_CHUNK)], rows_v)
            pltpu.async_copy(rows_v, out_hbm.at[idx_v], sem).wait()

    return pl.kernel(
        body, out_type=jax.ShapeDtypeStruct((n_out, width), src.dtype), mesh=mesh,
        scratch_types=[pltpu.VMEM((SC_CHUNK,), jnp.int32), pltpu.VMEM((SC_CHUNK, width), src.dtype),
                       pltpu.SemaphoreType.DMA],
    )(src, pos)


def _positions_kernel(idx_ref, rank_ref, pstart_ref, pos_ref):
    erow = lax.broadcasted_iota(jnp.int32, (N_EXPERTS, TR), 0)
    pstart = pstart_ref[...]
    for k in range(TOP_K):
        hit = erow == idx_ref[0, k:k + 1, :]
        seg = jnp.sum(jnp.where(hit, pstart, 0), axis=0, keepdims=True)
        pos_ref[0, k:k + 1, :] = seg + rank_ref[0, k:k + 1, :]


def _positions_call(idx, rank, pad_start):
    n_tiles = idx.shape[0]
    row_spec = pl.BlockSpec((1, TOP_K, TR), lambda i: (i, 0, 0))
    return pl.pallas_call(
        _positions_kernel,
        grid=(n_tiles,),
        in_specs=[row_spec, row_spec, pl.BlockSpec((N_EXPERTS, 1), lambda i: (0, 0))],
        out_specs=row_spec,
        out_shape=jax.ShapeDtypeStruct((n_tiles, TOP_K, TR), jnp.int32),
    )(idx, rank, pad_start.reshape(N_EXPERTS, 1))


def _combine_dense_kernel(g_ref, w_ref, x1_ref, sh_ref, mod_ref, fg_ref, outc_ref, outl_ref, *, n_ctx_tiles):
    w = w_ref[...]
    acc_a = acc_b = None
    for k in range(TOP_K):
        ya, yb = _unpack_rows_native(g_ref[0, k])
        acc_a = w[:, k:k + 1] * ya if k == 0 else acc_a + w[:, k:k + 1] * ya
        acc_b = w[:, k:k + 1] * yb if k == 0 else acc_b + w[:, k:k + 1] * yb
    acc = jnp.concatenate([acc_a, acc_b], axis=1)
    x2 = x1_ref[...] + mod_ref[0, 5:6, :] * (acc + sh_ref[...])
    out = x2 * lax.rsqrt(jnp.mean(x2 * x2, axis=-1, keepdims=True) + EPS) * fg_ref[...]
    is_ctx = pl.program_id(0) < n_ctx_tiles

    @pl.when(is_ctx)
    def _():
        outc_ref[...] = out

    @pl.when(jnp.logical_not(is_ctx))
    def _():
        outl_ref[...] = out


def _combine_dense_call(gathered, wts, x1, shared, mod, final_g, n_ctx, lat_len):
    T, K = wts.shape
    DH = gathered.shape[-1]
    D = 2 * DH
    row = functools.partial(_mod_row, tokens_per_tile=TC, n_ctx=n_ctx, lat_len=lat_len)
    nct = n_ctx // TC
    return pl.pallas_call(
        functools.partial(_combine_dense_kernel, n_ctx_tiles=nct),
        grid=(T // TC,),
        in_specs=[pl.BlockSpec((1, K, TC, DH), lambda j: (j, 0, 0, 0)),
                  pl.BlockSpec((TC, K), lambda j: (j, 0)),
                  pl.BlockSpec((TC, D), lambda j: (j, 0)),
                  pl.BlockSpec((TC, D), lambda j: (j, 0)),
                  pl.BlockSpec((1, N_MOD, D), lambda j: (row(j), 0, 0)),
                  pl.BlockSpec((1, D), lambda j: (0, 0))],
        out_specs=[pl.BlockSpec((TC, D), lambda j: (jnp.minimum(j, nct - 1), 0)),
                   pl.BlockSpec((TC, D), lambda j: (jnp.maximum(j - nct, 0), 0))],
        out_shape=[jax.ShapeDtypeStruct((n_ctx, D), jnp.float32), jax.ShapeDtypeStruct((T - n_ctx, D), jnp.float32)],
        compiler_params=pltpu.CompilerParams(dimension_semantics=("arbitrary",)),
    )(gathered, wts, x1, shared, mod, final_g.reshape(1, D))


def _moe_routed(h2, h2p, router_w, router_bias, w_gate, w_up, w_down):
    T, D = h2.shape
    idx, rank, w_rows, cnt = _route_call(h2, router_w, router_bias)
    wts = w_rows.transpose(0, 2, 1).reshape(T, TOP_K)
    counts = cnt[:, 0]
    padded = (counts + BM - 1) // BM * BM
    pad_end = jnp.cumsum(padded)
    pad_start = (pad_end - padded).astype(jnp.int32)
    n_pad = T * TOP_K + N_EXPERTS * BM
    n_blk = n_pad // BM
    n_used = (pad_end[-1] // BM).astype(jnp.int32).reshape(1)
    pos = _positions_call(idx, rank, pad_start)
    x_sorted = _sc_scatter_call(h2p, pos.reshape(-1), n_pad)
    blk_row0 = jnp.arange(n_blk, dtype=jnp.int32) * BM
    blk_e = jnp.minimum(jnp.sum((pad_end[None, :] <= blk_row0[:, None]).astype(jnp.int32), axis=1), N_EXPERTS - 1)
    own = blk_e[:, None] == jnp.arange(N_EXPERTS, dtype=jnp.int32)[None, :]
    seg_end = jnp.sum(jnp.where(own, (pad_start + counts)[None, :], 0), axis=1)
    n_valid = jnp.clip(seg_end - blk_row0, 0, BM).astype(jnp.int32)
    y = _expert_call(x_sorted, blk_e, n_valid, n_used, w_gate, w_up, w_down)
    pos_t = pos.reshape(T // TR, TOP_K, TR // TC, TC).transpose(0, 2, 1, 3).reshape(T // TC, TOP_K * TC)
    return y, pos_t, wts


def kernel(x_prompt, x_sample, state_delta, c, c_ctx, w_ada, b_ada, norm1_g, w_in, conv_w, a_log,
           dt_bias, onorm_g, pool_w, pool_scale, w_out, norm2_g, router_w, router_bias, exp_w_gate,
           exp_w_up, exp_w_down, sh_w_gate, sh_w_up, sh_w_down, final_g):
    Bc, Lc, D = x_prompt.shape
    Bl, Ll, _ = x_sample.shape
    n_ctx = Bc * Lc
    assert DEPTH == 1 and 1 + Bl <= MOD_ROWS and n_ctx % Ll == 0
    x_parts = (x_prompt.reshape(n_ctx, D), x_sample.reshape(Bl * Ll, D))
    cvec = jnp.concatenate([c_ctx[None], c, jnp.zeros((MOD_ROWS - 1 - Bl, D), c.dtype)], axis=0)
    l = 0
    mod = _ada_call(cvec, w_ada[l], b_ada[l]).reshape(MOD_ROWS, N_MOD, D)
    qkv, z, ba, u = _inproj_call(*x_parts, mod, norm1_g[l], w_in[l], Ll)
    dn = (conv_w[l], a_log[l], dt_bias[l], onorm_g[l])
    oa_c, st_ctx = _delta_call(qkv, z, ba, *dn, None, Bc, Lc, 0)
    oa_l, _ = _delta_call(qkv, z, ba, *dn, state_delta[:, l], Bl, Ll, n_ctx // Ll)
    op_c = _pool_call(u, pool_w[l], pool_scale[l], False, Bc, Lc, 0)
    op_l = _pool_call(u, pool_w[l], pool_scale[l], True, Bl, Ll, n_ctx // Ll)
    x1, h2, h2p, shared = _outproj_call(x_parts, (oa_c, oa_l), (op_c, op_l), mod, norm2_g[l], w_out[l],
                                        sh_w_gate[l], sh_w_up[l], sh_w_down[l], Ll)
    y, pos_t, wts = _moe_routed(h2, h2p, router_w[l], router_bias[l], exp_w_gate[l], exp_w_up[l],
                                exp_w_down[l])
    T = n_ctx + Bl * Ll
    gathered = _sc_gather_call(y, pos_t.reshape(-1))
    out_c, out_l = _combine_dense_call(gathered.reshape(T // TC, TOP_K, TC, D // 2), wts, x1, shared, mod,
                                       final_g, n_ctx, Ll)
    y_prompt = out_c.reshape(Bc, Lc, D)
    y_sample = out_l.reshape(Bl, Ll, D)
    new_state_delta = st_ctx[:, None].astype(x_prompt.dtype)
    return (y_prompt, y_sample, new_state_delta)
```

```python
import functools
import jax, jax.numpy as jnp
from jax import lax
from jax.experimental import pallas as pl
from jax.experimental.pallas import tpu as pltpu
from jax.experimental.pallas import tpu_sc as plsc

D_MODEL = 1024
DEPTH = 1
GRID_W = 64
D_MIX = D_MODEL
D_A = D_MIX // 2
D_P = D_MIX - D_A
H_A = 4
DK = D_A // H_A
DV = D_A // H_A
CONV_K = 5
CHUNK = 64
POOL_WINDOWS = (2, 4, 8, 16)
N_PG = len(POOL_WINDOWS)
PG = D_P // N_PG
N_EXPERTS = 256
TOP_K = 8
N_GROUPS = 8
TOPK_GROUP = 4
ROUTED_SCALE = 2.5
EPS = 1e-6
VMEM_LIMIT = 48 * 1024 * 1024


def _split_bf16(a):
    hi = a.astype(jnp.bfloat16)
    return hi, (a - hi.astype(jnp.float32)).astype(jnp.bfloat16)


def _bdot(a, b):
    return jnp.dot(a, b, preferred_element_type=jnp.float32)


def _pack_rows(x):
    m = x.shape[1] // 2
    lo = lax.bitcast_convert_type(x[:, :m].astype(jnp.bfloat16).astype(jnp.float32), jnp.uint32)
    hi = lax.bitcast_convert_type(x[:, m:].astype(jnp.bfloat16).astype(jnp.float32), jnp.uint32)
    return lax.bitcast_convert_type(hi | (lo >> 16), jnp.int32)


def _pack_rows_native(x):
    m = x.shape[1] // 2
    return pltpu.pack_elementwise([x[:, :m], x[:, m:]], packed_dtype=jnp.bfloat16)


def _unpack_rows_native(p):
    return tuple(pltpu.unpack_elementwise(p, index=i, packed_dtype=jnp.bfloat16, unpacked_dtype=jnp.float32)
                 for i in range(2))


N_MOD = 6
MOD_ROWS = 8
TM = 512


def _ada_kernel(c_ref, w_ref, b_ref, o_ref):
    c = c_ref[...]
    s = c * jax.nn.sigmoid(c)
    sh, sl = _split_bf16(s)
    wh, wl = _split_bf16(w_ref[...])
    o_ref[...] = _bdot(sh, wh) + (_bdot(sh, wl) + _bdot(sl, wh)) + b_ref[...]


def _ada_call(cvec, w_ada, b_ada):
    R, D = cvec.shape
    N = w_ada.shape[1]
    tn = 1024
    return pl.pallas_call(
        _ada_kernel,
        grid=(N // tn,),
        in_specs=[pl.BlockSpec((R, D), lambda j: (0, 0)),
                  pl.BlockSpec((D, tn), lambda j: (0, j)),
                  pl.BlockSpec((1, tn), lambda j: (0, j))],
        out_specs=pl.BlockSpec((R, tn), lambda j: (0, j)),
        out_shape=jax.ShapeDtypeStruct((R, N), jnp.float32),
    )(cvec, w_ada, b_ada.reshape(1, N))


def _mod_row(tile, tokens_per_tile, n_ctx, lat_len):
    t0 = tile * tokens_per_tile
    return jnp.where(t0 < n_ctx, 0, 1 + (t0 - n_ctx) // lat_len)


def _two_part_specs(n_ctx_tiles, width):
    return (pl.BlockSpec((TM, width), lambda i: (jnp.minimum(i, n_ctx_tiles - 1), 0)),
            pl.BlockSpec((TM, width), lambda i: (jnp.maximum(i - n_ctx_tiles, 0), 0)))


def _pick(n_ctx_tiles, ctx_ref, lat_ref):
    return jnp.where(pl.program_id(0) < n_ctx_tiles, ctx_ref[...], lat_ref[...])


def _inproj_kernel(xc_ref, xl_ref, mod_ref, g_ref, wq_ref, wz_ref, wb_ref, wu_ref, q_ref, z_ref, b_ref, u_ref,
                   *, n_ctx_tiles):
    x = _pick(n_ctx_tiles, xc_ref, xl_ref)
    y = x * lax.rsqrt(jnp.mean(x * x, axis=-1, keepdims=True) + EPS) * g_ref[...]
    h = (y * (1.0 + mod_ref[0, 1:2, :]) + mod_ref[0, 0:1, :]).astype(jnp.bfloat16)
    q_ref[...] = _bdot(h, wq_ref[...])
    z_ref[...] = _bdot(h, wz_ref[...])
    b_ref[...] = _bdot(h, wb_ref[...])
    u_ref[...] = _bdot(h, wu_ref[...])


def _inproj_call(x_ctx, x_lat, mod, norm1_g, w_in, lat_len):
    n_ctx, D = x_ctx.shape
    T = n_ctx + x_lat.shape[0]
    bf = jnp.bfloat16
    nq, nz, nb = 3 * D_A, D_A, 4 * H_A
    wq = w_in[:, :nq].astype(bf)
    wz = w_in[:, nq:nq + nz].astype(bf)
    wb = jnp.pad(w_in[:, nq + nz:nq + nz + nb], ((0, 0), (0, 128 - nb))).astype(bf)
    wu = w_in[:, nq + nz + nb:].astype(bf)
    row = functools.partial(_mod_row, tokens_per_tile=TM, n_ctx=n_ctx, lat_len=lat_len)

    def full(a):
        return pl.BlockSpec(a.shape, lambda i: (0, 0))

    def rows(n):
        return pl.BlockSpec((TM, n), lambda i: (i, 0))

    return pl.pallas_call(
        functools.partial(_inproj_kernel, n_ctx_tiles=n_ctx // TM),
        grid=(T // TM,),
        in_specs=[*_two_part_specs(n_ctx // TM, D), pl.BlockSpec((1, N_MOD, D), lambda i: (row(i), 0, 0)),
                  pl.BlockSpec((1, D), lambda i: (0, 0)), full(wq), full(wz), full(wb), full(wu)],
        out_specs=[rows(nq), rows(nz), rows(128), rows(D_P)],
        out_shape=[jax.ShapeDtypeStruct((T, nq), jnp.float32), jax.ShapeDtypeStruct((T, nz), jnp.float32),
                   jax.ShapeDtypeStruct((T, 128), jnp.float32), jax.ShapeDtypeStruct((T, D_P), jnp.float32)],
        compiler_params=pltpu.CompilerParams(dimension_semantics=("arbitrary",),
                                             vmem_limit_bytes=VMEM_LIMIT),
    )(x_ctx, x_lat, mod, norm1_g.reshape(1, D), wq, wz, wb, wu)


PT = 256


def _window_bounds(pos, w, n):
    return jnp.maximum(pos - w // 2, 0), jnp.minimum(pos + w - w // 2, n)


def _band_sum(band, x):
    xh, xl = _split_bf16(x)
    return _bdot(band, xh) + _bdot(band, xl)


def _pool_seq_kernel(u_ref, pw_ref, ps_ref, o_ref):
    L = u_ref.shape[0]
    ti = lax.broadcasted_iota(jnp.int32, (L, L), 0)
    ji = lax.broadcasted_iota(jnp.int32, (L, L), 1)
    tcol = lax.broadcasted_iota(jnp.int32, (L, 1), 0)
    for i, w in enumerate(POOL_WINDOWS):
        lo, hi = _window_bounds(ti, w, L)
        band = ((ji >= lo) & (ji < hi)).astype(jnp.bfloat16)
        clo, chi = _window_bounds(tcol, w, L)
        ug = u_ref[:, i * PG:(i + 1) * PG]
        mean = _band_sum(band, ug) / (chi - clo).astype(jnp.float32)
        d = (mean - ug).astype(jnp.bfloat16)
        o_ref[:, i * PG:(i + 1) * PG] = _bdot(d, pw_ref[i]) * ps_ref[:, i * PG:(i + 1) * PG]


def _pool_grid_kernel(u_ref, pw_ref, ps_ref, o_ref, pad_s, r_s):
    L = u_ref.shape[0]
    rows = L // GRID_W
    halo = (max(POOL_WINDOWS) // 2) * GRID_W
    pad_s[0:halo, :] = jnp.zeros((halo, D_P), jnp.float32)
    pad_s[halo + L:, :] = jnp.zeros((halo, D_P), jnp.float32)
    pad_s[halo:halo + L, :] = u_ref[...]
    ti = lax.broadcasted_iota(jnp.int32, (PT, PT), 0)
    ji = lax.broadcasted_iota(jnp.int32, (PT, PT), 1)
    tcol = lax.broadcasted_iota(jnp.int32, (PT, 1), 0)
    for i, w in enumerate(POOL_WINDOWS):
        cs = slice(i * PG, (i + 1) * PG)
        acc = None
        for dr in range(-(w // 2), w - w // 2):
            part = pad_s[halo + dr * GRID_W:halo + dr * GRID_W + L, cs]
            acc = part if acc is None else acc + part
        r_s[...] = acc
        lo, hi = _window_bounds(ti % GRID_W, w, GRID_W)
        band = ((ji // GRID_W == ti // GRID_W) & (ji % GRID_W >= lo) & (ji % GRID_W < hi)).astype(jnp.bfloat16)
        clo, chi = _window_bounds(tcol % GRID_W, w, GRID_W)
        ccnt = (chi - clo).astype(jnp.float32)
        for tile in range(L // PT):
            ts = slice(tile * PT, (tile + 1) * PT)
            rlo, rhi = _window_bounds(tile * (PT // GRID_W) + tcol // GRID_W, w, rows)
            mean = _band_sum(band, r_s[ts, :]) / ((rhi - rlo).astype(jnp.float32) * ccnt)
            d = (mean - u_ref[ts, cs]).astype(jnp.bfloat16)
            o_ref[ts, cs] = _bdot(d, pw_ref[i]) * ps_ref[:, cs]


def _pool_call(u, pool_w, pool_scale, grid, B, L, row_blk0):
    pw = pool_w.astype(jnp.bfloat16)
    ps = pool_scale.reshape(1, D_P)
    specs = dict(
        grid=(B,),
        in_specs=[pl.BlockSpec((L, D_P), lambda b: (row_blk0 + b, 0)),
                  pl.BlockSpec((N_PG, PG, PG), lambda b: (0, 0, 0)),
                  pl.BlockSpec((1, D_P), lambda b: (0, 0))],
        out_specs=pl.BlockSpec((L, D_P), lambda b: (b, 0)),
        out_shape=jax.ShapeDtypeStruct((B * L, D_P), jnp.float32),
        compiler_params=pltpu.CompilerParams(dimension_semantics=("arbitrary",),
                                             vmem_limit_bytes=VMEM_LIMIT))
    if not grid:
        return pl.pallas_call(_pool_seq_kernel, **specs)(u, pw, ps)
    halo = (max(POOL_WINDOWS) // 2) * GRID_W
    return pl.pallas_call(
        _pool_grid_kernel,
        scratch_shapes=[pltpu.VMEM((L + 2 * halo, D_P), jnp.float32), pltpu.VMEM((L, PG), jnp.float32)],
        **specs)(u, pw, ps)


def _outproj_kernel(xc_ref, xl_ref, oac_ref, oal_ref, opc_ref, opl_ref, mod_ref, g2_ref, wo_ref, sg_ref, su_ref,
                    sd_ref, x1_ref, h2_ref, h2p_ref, sh_ref, *, n_ctx_tiles):
    o_a = _pick(n_ctx_tiles, oac_ref, oal_ref)
    o_p = _pick(n_ctx_tiles, opc_ref, opl_ref)
    mix = (_bdot(o_a.astype(jnp.bfloat16), wo_ref[:D_A, :])
           + _bdot(o_p.astype(jnp.bfloat16), wo_ref[D_A:, :]))
    x1 = _pick(n_ctx_tiles, xc_ref, xl_ref) + mod_ref[0, 2:3, :] * mix
    x1_ref[...] = x1
    y = x1 * lax.rsqrt(jnp.mean(x1 * x1, axis=-1, keepdims=True) + EPS) * g2_ref[...]
    h2 = y * (1.0 + mod_ref[0, 4:5, :]) + mod_ref[0, 3:4, :]
    h2_ref[...] = h2
    h2p_ref[...] = _pack_rows(h2)
    hb = h2.astype(jnp.bfloat16)
    g = _bdot(hb, sg_ref[...])
    a = (g * jax.nn.sigmoid(g)) * _bdot(hb, su_ref[...])
    sh_ref[...] = _bdot(a.astype(jnp.bfloat16), sd_ref[...])


def _outproj_call(x_parts, oa_parts, op_parts, mod, norm2_g, w_out, sh_gate, sh_up, sh_down, lat_len):
    n_ctx, D = x_parts[0].shape
    T = n_ctx + x_parts[1].shape[0]
    nct = n_ctx // TM
    bf = jnp.bfloat16
    row = functools.partial(_mod_row, tokens_per_tile=TM, n_ctx=n_ctx, lat_len=lat_len)
    ws = [w_out.astype(bf), sh_gate.astype(bf), sh_up.astype(bf), sh_down.astype(bf)]

    def rows(n):
        return pl.BlockSpec((TM, n), lambda i: (i, 0))

    return pl.pallas_call(
        functools.partial(_outproj_kernel, n_ctx_tiles=nct),
        grid=(T // TM,),
        in_specs=[*_two_part_specs(nct, D), *_two_part_specs(nct, D_A), *_two_part_specs(nct, D_P),
                  pl.BlockSpec((1, N_MOD, D), lambda i: (row(i), 0, 0)),
                  pl.BlockSpec((1, D), lambda i: (0, 0))] + [pl.BlockSpec(w.shape, lambda i: (0, 0)) for w in ws],
        out_specs=[rows(D), rows(D), rows(D // 2), rows(D)],
        out_shape=[jax.ShapeDtypeStruct((T, D), jnp.float32), jax.ShapeDtypeStruct((T, D), jnp.float32),
                   jax.ShapeDtypeStruct((T, D // 2), jnp.int32), jax.ShapeDtypeStruct((T, D), jnp.float32)],
        compiler_params=pltpu.CompilerParams(dimension_semantics=("arbitrary",),
                                             vmem_limit_bytes=VMEM_LIMIT),
    )(*x_parts, *oa_parts, *op_parts, mod, norm2_g.reshape(1, D), *ws)


SC = 256
CPS = SC // CHUNK
BASE = 16
DELTA_HEAD_ROWS = 4096


def _mm(a, b):
    return jnp.dot(a.astype(jnp.bfloat16), b.astype(jnp.bfloat16), preferred_element_type=jnp.float32)


def _mm_nt(a, b):
    return lax.dot_general(a.astype(jnp.bfloat16), b.astype(jnp.bfloat16), (((1,), (1,)), ((), ())),
                           preferred_element_type=jnp.float32)


def _softplus(x):
    return jnp.maximum(x, 0.0) + jnp.log(1.0 + jnp.exp(-jnp.abs(x)))


def _delta_kernel(sc_ref, xq_ref, xk_ref, xv_ref, z_ref, bac_ref, bar_ref, cwq_ref, cwk_ref, cwv_ref,
                  og_ref, s0_ref, o_ref, st_ref, q_s, k_s, v_s, o_s, vn_s, *, n_sc, zero_init, hpb):
    hb = pl.program_id(1)
    L = q_s.shape[1]

    def conv(x_ref, w_ref, cs):
        x = x_ref[:, cs]
        row = lax.broadcasted_iota(jnp.int32, x.shape, 0)
        acc = x * w_ref[CONV_K // 2:CONV_K // 2 + 1, cs]
        for j in range(CONV_K):
            d = j - CONV_K // 2
            if d == 0:
                continue
            xs = pltpu.roll(x, (-d) % L, 0)
            ok = (row + d >= 0) & (row + d < L)
            acc = acc + jnp.where(ok, xs, 0.0) * w_ref[j:j + 1, cs]
        return acc * jax.nn.sigmoid(acc)

    for hh in range(hpb):
        cs = slice(hh * DK, (hh + 1) * DK)
        q = conv(xq_ref, cwq_ref, cs)
        q_s[hh] = q * lax.rsqrt(jnp.sum(q * q, axis=-1, keepdims=True) + EPS) * (DK ** -0.5)
        k = conv(xk_ref, cwk_ref, cs)
        k_s[hh] = k * lax.rsqrt(jnp.sum(k * k, axis=-1, keepdims=True) + EPS)
        v_s[hh] = conv(xv_ref, cwv_ref, cs)
    o_s[...] = jnp.zeros_like(o_s)

    ri = lax.broadcasted_iota(jnp.int32, (SC, SC), 0)
    ci = lax.broadcasted_iota(jnp.int32, (SC, SC), 1)
    same = (ri // CHUNK) == (ci // CHUNK)
    same_base = (ri // BASE) == (ci // BASE)
    merge_masks = [(ri // w) == (ci // w) for w in (2 * BASE, CHUNK)]
    eye = (ri == ci).astype(jnp.float32)
    rowi = lax.broadcasted_iota(jnp.int32, (SC, DV), 0)

    def prep(m, d, hh):
        r0 = pl.multiple_of(m * SC, SC)
        h = hb * hpb + hh
        q = q_s[hh, pl.ds(r0, SC), :]
        k = k_s[hh, pl.ds(r0, SC), :]
        v = v_s[hh, pl.ds(r0, SC), :]
        bc = bac_ref[0, hh, pl.ds(r0, SC), :]
        br = bar_ref[0, hh, m]
        a_l = sc_ref[d * H_A + h]
        dtb = sc_ref[2 * H_A + d * H_A + h]
        neg_ea = -jnp.exp(jnp.full((1, 1), a_l, jnp.float32))
        beta = jax.nn.sigmoid(bc[:, d:d + 1])
        g_col = neg_ea * _softplus(bc[:, 2 + d:3 + d] + dtb)
        g_row = neg_ea * _softplus(br[2 + d:3 + d, :] + dtb)
        if d == 0:
            tri, strict = same & (ci <= ri), same & (ci < ri)
        else:
            tri, strict = same & (ci >= ri), same & (ci > ri)
        tri_t = same & (ri <= ci) if d == 0 else same & (ri >= ci)
        gc_col = jnp.sum(jnp.where(tri, g_row, 0.0), axis=1, keepdims=True)
        gc_row = jnp.sum(jnp.where(tri_t, g_col, 0.0), axis=0, keepdims=True)
        gl_col = jnp.sum(jnp.where(same, g_row, 0.0), axis=1, keepdims=True)
        decay = jnp.where(tri, jnp.exp(jnp.where(tri, gc_col - gc_row, 0.0)), 0.0)
        kb = k * beta
        a = jnp.where(strict, _mm_nt(kb, k) * decay, 0.0)
        attn = jnp.where(tri, _mm_nt(q, k) * decay, 0.0)
        eg = jnp.exp(gc_col)
        x = jnp.concatenate([v * beta, kb * eg], axis=1)
        qd = q * eg
        kdt = (k * jnp.exp(gl_col - gc_col)).T
        return dict(r0=r0, a=a, attn=attn, x=x, qd=qd, kdt=kdt, egl=jnp.exp(gl_col))

    def run_chains(ms, states):
        n = len(chains)
        ops = [prep(ms[i], d, hh) for i, (hh, d) in enumerate(chains)]
        ps = [jnp.where(same_base, o["a"], 0.0) for o in ops]
        ts = [eye - p for p in ps]
        for _ in range(BASE.bit_length() - 2):
            ps = [_mm(p, p) for p in ps]
            ts = [t + _mm(t, p) for t, p in zip(ts, ps)]
        inner = same_base
        for outer in merge_masks:
            lows = [_mm(jnp.where(outer & ~inner, o["a"], 0.0), t) for o, t in zip(ops, ts)]
            ts = [t - _mm(t, low) for t, low in zip(ts, lows)]
            inner = outer
        xs = [_mm(t, o["x"]) for t, o in zip(ts, ops)]
        for i in range(n):
            vn_s[i] = jnp.zeros((SC, DV), jnp.float32)
        states = list(states)
        for step in range(CPS):
            cs = [step if d == 0 else CPS - 1 - step for _, d in chains]
            los = [c * CHUNK for c in cs]
            ws_qs = [_mm(jnp.concatenate([x[lo:lo + CHUNK, DV:], o["qd"][lo:lo + CHUNK]], axis=0), s)
                     for x, o, lo, s in zip(xs, ops, los, states)]
            for i in range(n):
                vn_s[i, los[i]:los[i] + CHUNK, :] = xs[i][los[i]:los[i] + CHUNK, :DV] - ws_qs[i][:CHUNK]
            vns = [vn_s[i] for i in range(n)]
            o_cs = [wq[CHUNK:] + _mm(o["attn"][lo:lo + CHUNK, :], vn)
                    for wq, o, lo, vn in zip(ws_qs, ops, los, vns)]
            for i, (hh, _) in enumerate(chains):
                o_s[hh, pl.ds(ops[i]["r0"] + los[i], CHUNK), :] += o_cs[i]
            states = [s * o["egl"][lo:lo + 1, :]
                      + _mm(o["kdt"], jnp.where((rowi >= lo) & (rowi < lo + CHUNK), vn, 0.0))
                      for s, o, lo, vn in zip(states, ops, los, vns)]
        return tuple(states)

    if zero_init:
        states = tuple(jnp.zeros((DK, DV), jnp.float32) for _ in range(2 * hpb))
    else:
        states = tuple(s0_ref[0, d, hh] for hh in range(hpb) for d in range(2))

    chains = [(hh, d) for hh in range(hpb) for d in range(2)]

    def body(m, carry):
        return run_chains([m if d == 0 else n_sc - 1 - m for _, d in chains], carry)

    if n_sc == 1:
        states = body(0, states)
    else:
        states = lax.fori_loop(0, n_sc, body, states)

    for hh in range(hpb):
        for d in range(2):
            st_ref[0, d, hh] = states[2 * hh + d]
        o = o_s[hh]
        o = o * lax.rsqrt(jnp.mean(o * o, axis=-1, keepdims=True) + EPS) * og_ref[...]
        zz = z_ref[:, hh * DV:(hh + 1) * DV]
        o_ref[:, hh * DV:(hh + 1) * DV] = o * (zz * jax.nn.sigmoid(zz))


def _delta_call(qkv, z, ba, conv_w, a_log, dt_bias, onorm_g, s0, B, L, row_blk0):
    n_sc = L // SC
    t0 = row_blk0 * L
    bah = ba[t0:t0 + B * L, :4 * H_A].reshape(B, L, 4, H_A).transpose(0, 3, 1, 2)
    bar = bah.reshape(B, H_A, n_sc, SC, 4).transpose(0, 1, 2, 4, 3)
    scal = jnp.concatenate([a_log.reshape(-1), dt_bias.reshape(-1)]).astype(jnp.float32)
    hpb = max(1, min(H_A, DELTA_HEAD_ROWS // L))
    n_hb = H_A // hpb
    zero_init = s0 is None
    if zero_init:
        s0 = jnp.zeros((1, 2, hpb, DK, DV), jnp.float32)
        s0_spec = pl.BlockSpec((1, 2, hpb, DK, DV), lambda b, h, sc: (0, 0, 0, 0, 0))
    else:
        s0_spec = pl.BlockSpec((1, 2, hpb, DK, DV), lambda b, h, sc: (b, 0, h, 0, 0))

    def col(off):
        return pl.BlockSpec((L, hpb * DK), lambda b, h, sc: (row_blk0 + b, off * n_hb + h))

    def cw(off):
        return pl.BlockSpec((CONV_K, hpb * DK), lambda b, h, sc: (0, off * n_hb + h))

    kern = functools.partial(_delta_kernel, n_sc=n_sc, zero_init=zero_init, hpb=hpb)
    return pl.pallas_call(
        kern,
        grid_spec=pltpu.PrefetchScalarGridSpec(
            num_scalar_prefetch=1,
            grid=(B, n_hb),
            in_specs=[col(0), col(1), col(2),
                      pl.BlockSpec((L, hpb * DV), lambda b, h, sc: (row_blk0 + b, h)),
                      pl.BlockSpec((1, hpb, L, 4), lambda b, h, sc: (b, h, 0, 0)),
                      pl.BlockSpec((1, hpb, n_sc, 4, SC), lambda b, h, sc: (b, h, 0, 0, 0)),
                      cw(0), cw(1), cw(2),
                      pl.BlockSpec((1, DV), lambda b, h, sc: (0, 0)),
                      s0_spec],
            out_specs=[pl.BlockSpec((L, hpb * DV), lambda b, h, sc: (b, h)),
                       pl.BlockSpec((1, 2, hpb, DK, DV), lambda b, h, sc: (b, 0, h, 0, 0))],
            scratch_shapes=[pltpu.VMEM((hpb, L, DK), jnp.float32), pltpu.VMEM((hpb, L, DK), jnp.float32),
                            pltpu.VMEM((hpb, L, DV), jnp.float32), pltpu.VMEM((hpb, L, DV), jnp.float32),
                            pltpu.VMEM((2 * hpb, SC, DV), jnp.float32)]),
        out_shape=[jax.ShapeDtypeStruct((B * L, D_A), jnp.float32),
                   jax.ShapeDtypeStruct((B, 2, H_A, DK, DV), jnp.float32)],
        compiler_params=pltpu.CompilerParams(dimension_semantics=("arbitrary", "arbitrary"),
                                             vmem_limit_bytes=VMEM_LIMIT),
    )(scal, qkv, qkv, qkv, z, bah, bar, conv_w, conv_w, conv_w, onorm_g.reshape(1, DV), s0)


TR = 256
GSZ = N_EXPERTS // N_GROUPS
NEG = -jnp.inf
BM = 512


def _route_kernel(h_ref, rwh_ref, rwl_ref, rb_ref, idx_ref, rank_ref, w_ref, cnt_ref, cnt_s):
    i = pl.program_id(0)

    @pl.when(i == 0)
    def _():
        cnt_s[...] = jnp.zeros_like(cnt_s)

    h = h_ref[...]
    hh, hl = _split_bf16(h)
    logits = _bdot(hh, rwh_ref[...]) + (_bdot(hh, rwl_ref[...]) + _bdot(hl, rwh_ref[...]))
    scores = jax.nn.sigmoid(logits.T)
    sel = scores + rb_ref[...]
    erow = lax.broadcasted_iota(jnp.int32, sel.shape, 0)
    grow = lax.broadcasted_iota(jnp.int32, (GSZ, TR), 0)

    def first_argmax(v, rows):
        m = jnp.max(v, axis=0, keepdims=True)
        first = jnp.min(jnp.where(v == m, rows, N_EXPERTS), axis=0, keepdims=True)
        return m, first

    gs = []
    for g in range(N_GROUPS):
        vg = sel[g * GSZ:(g + 1) * GSZ, :]
        m1, i1 = first_argmax(vg, grow)
        m2 = jnp.max(jnp.where(grow == i1, NEG, vg), axis=0, keepdims=True)
        gs.append(m1 + m2)
    cand = []
    for g in range(N_GROUPS):
        beat = jnp.zeros(gs[g].shape, jnp.int32)
        for o in range(N_GROUPS):
            if o == g:
                continue
            wins = (gs[o] > gs[g]) | ((gs[o] == gs[g]) & (o < g))
            beat = beat + wins.astype(jnp.int32)
        cand.append(jnp.where(beat < TOPK_GROUP, sel[g * GSZ:(g + 1) * GSZ, :], NEG))
    cand = jnp.concatenate(cand, axis=0)
    chosen = []
    picked = jnp.zeros(sel.shape, jnp.bool_)
    for _ in range(TOP_K):
        _, ik = first_argmax(cand, erow)
        hit = erow == ik
        chosen.append((ik, hit))
        picked = picked | hit
        cand = jnp.where(hit, NEG, cand)
    wsum = jnp.sum(jnp.where(picked, scores, 0.0), axis=0, keepdims=True)

    ri = lax.broadcasted_iota(jnp.int32, (TR, TR), 0)
    ci = lax.broadcasted_iota(jnp.int32, (TR, TR), 1)
    earlier = (ri < ci).astype(jnp.bfloat16)
    rank_mat = _bdot(picked.astype(jnp.bfloat16), earlier) + cnt_s[...]
    cnt_s[...] = cnt_s[...] + jnp.sum(picked.astype(jnp.float32), axis=1, keepdims=True)
    cnt_ref[...] = cnt_s[...].astype(jnp.int32)

    for k, (ik, hit) in enumerate(chosen):
        idx_ref[0, k:k + 1, :] = ik
        rank_ref[0, k:k + 1, :] = jnp.sum(jnp.where(hit, rank_mat, 0.0), axis=0, keepdims=True).astype(jnp.int32)
        w_ref[0, k:k + 1, :] = jnp.sum(jnp.where(hit, scores, 0.0), axis=0, keepdims=True) / wsum * ROUTED_SCALE


def _route_call(hf, router_w, router_bias):
    T, D = hf.shape
    n_tiles = T // TR
    rwh, rwl = _split_bf16(router_w)
    row_spec = pl.BlockSpec((1, TOP_K, TR), lambda i: (i, 0, 0))
    return pl.pallas_call(
        _route_kernel,
        grid=(n_tiles,),
        in_specs=[pl.BlockSpec((TR, D), lambda i: (i, 0)),
                  pl.BlockSpec((D, N_EXPERTS), lambda i: (0, 0)),
                  pl.BlockSpec((D, N_EXPERTS), lambda i: (0, 0)),
                  pl.BlockSpec((N_EXPERTS, 1), lambda i: (0, 0))],
        out_specs=[row_spec, row_spec, row_spec, pl.BlockSpec((N_EXPERTS, 1), lambda i: (0, 0))],
        scratch_shapes=[pltpu.VMEM((N_EXPERTS, 1), jnp.float32)],
        out_shape=[jax.ShapeDtypeStruct((n_tiles, TOP_K, TR), jnp.int32),
                   jax.ShapeDtypeStruct((n_tiles, TOP_K, TR), jnp.int32),
                   jax.ShapeDtypeStruct((n_tiles, TOP_K, TR), jnp.float32),
                   jax.ShapeDtypeStruct((N_EXPERTS, 1), jnp.int32)],
        compiler_params=pltpu.CompilerParams(dimension_semantics=("arbitrary",)),
    )(hf, rwh, rwl, router_bias.reshape(N_EXPERTS, 1).astype(jnp.float32))


def _expert_kernel(blk_e_ref, nvalid_ref, nused_ref, x_ref, wg_ref, wu_ref, wd_ref, y_ref, wg_s, wu_s, wd_s):
    i = pl.program_id(0)

    @pl.when(i < nused_ref[0])
    def _():
        e = blk_e_ref[i]
        prev = blk_e_ref[jnp.maximum(i - 1, 0)]

        @pl.when((i == 0) | (e != prev))
        def _():
            wg_s[...] = wg_ref[0].astype(jnp.bfloat16)
            wu_s[...] = wu_ref[0].astype(jnp.bfloat16)
            wd_s[...] = wd_ref[0].astype(jnp.bfloat16)

        row = lax.broadcasted_iota(jnp.int32, (BM, 1), 0)
        xa, xb = _unpack_rows_native(jnp.where(row < nvalid_ref[i], x_ref[...], 0))
        xa = xa.astype(jnp.bfloat16)
        xb = xb.astype(jnp.bfloat16)
        half = xa.shape[1]
        g = _bdot(xa, wg_s[:half, :]) + _bdot(xb, wg_s[half:, :])
        u = _bdot(xa, wu_s[:half, :]) + _bdot(xb, wu_s[half:, :])
        a = (g * jax.nn.sigmoid(g)) * u
        y_ref[...] = _pack_rows_native(_bdot(a.astype(jnp.bfloat16), wd_s[...]))


def _expert_call(x_sorted, blk_e, n_valid, n_used, w_gate, w_up, w_down):
    n_pad, DH = x_sorted.shape
    n_blk = n_pad // BM
    E, D, F = w_gate.shape

    def row_map(i, be, nv, nu):
        return (jnp.minimum(i, nu[0] - 1), 0)

    def w_map(i, be, nv, nu):
        return (be[jnp.minimum(i, nu[0] - 1)], 0, 0)

    return pl.pallas_call(
        _expert_kernel,
        grid_spec=pltpu.PrefetchScalarGridSpec(
            num_scalar_prefetch=3,
            grid=(n_blk,),
            in_specs=[pl.BlockSpec((BM, DH), row_map),
                      pl.BlockSpec((1, D, F), w_map),
                      pl.BlockSpec((1, D, F), w_map),
                      pl.BlockSpec((1, F, D), w_map)],
            out_specs=pl.BlockSpec((BM, DH), row_map),
            scratch_shapes=[pltpu.VMEM((D, F), jnp.bfloat16), pltpu.VMEM((D, F), jnp.bfloat16),
                            pltpu.VMEM((F, D), jnp.bfloat16)]),
        out_shape=jax.ShapeDtypeStruct((n_pad, DH), jnp.int32),
        compiler_params=pltpu.CompilerParams(dimension_semantics=("arbitrary",),
                                             vmem_limit_bytes=VMEM_LIMIT),
    )(blk_e, n_valid, n_used, x_sorted, w_gate, w_up, w_down)


TC = 128


SC_CORES = 2
SC_SUBCORES = 16
SC_CHUNK = 128


def _sc_gather_call(table, idx):
    n_idx = idx.shape[0]
    width = table.shape[1]
    n_workers = SC_CORES * SC_SUBCORES
    per_worker = n_idx // n_workers
    assert per_worker * n_workers == n_idx and per_worker % SC_CHUNK == 0
    mesh = plsc.VectorSubcoreMesh(core_axis_name="c", subcore_axis_name="s")

    def body(table_hbm, idx_hbm, out_hbm, idx_v, rows_v, sem):
        wid = lax.axis_index("s") * SC_CORES + lax.axis_index("c")
        base = wid * per_worker

        @pl.loop(0, per_worker // SC_CHUNK)
        def _(ch):
            off = base + ch * SC_CHUNK
            pltpu.sync_copy(idx_hbm.at[pl.ds(off, SC_CHUNK)], idx_v)
            pltpu.async_copy(table_hbm.at[idx_v], rows_v, sem).wait()
            pltpu.sync_copy(rows_v, out_hbm.at[pl.ds(off, SC_CHUNK)])

    return pl.kernel(
        body, out_type=jax.ShapeDtypeStruct((n_idx, width), table.dtype), mesh=mesh,
        scratch_types=[pltpu.VMEM((SC_CHUNK,), jnp.int32), pltpu.VMEM((SC_CHUNK, width), table.dtype),
                       pltpu.SemaphoreType.DMA],
    )(table, idx)


def _sc_scatter_call(src, pos, n_out):
    n_idx = pos.shape[0]
    width = src.shape[1]
    n_workers = SC_CORES * SC_SUBCORES
    per_worker = n_idx // n_workers
    assert per_worker * n_workers == n_idx and per_worker % SC_CHUNK == 0 and TR % SC_CHUNK == 0
    mesh = plsc.VectorSubcoreMesh(core_axis_name="c", subcore_axis_name="s")
    tile_pairs = TOP_K * TR

    def body(src_hbm, pos_hbm, out_hbm, idx_v, rows_v, sem):
        wid = lax.axis_index("s") * SC_CORES + lax.axis_index("c")
        base = wid * per_worker

        @pl.loop(0, per_worker // SC_CHUNK)
        def _(ch):
            off = base + ch * SC_CHUNK
            row0 = (off // tile_pairs) * TR + off % TR
            pltpu.sync_copy(pos_hbm.at[pl.ds(off, SC_CHUNK)], idx_v)
            pltpu.sync_copy(src_hbm.at[pl.ds(row0, SC_CHUNK)], rows_v)
            pltpu.async_copy(rows_v, out_hbm.at[idx_v], sem).wait()

    return pl.kernel(
        body, out_type=jax.ShapeDtypeStruct((n_out, width), src.dtype), mesh=mesh,
        scratch_types=[pltpu.VMEM((SC_CHUNK,), jnp.int32), pltpu.VMEM((SC_CHUNK, width), src.dtype),
                       pltpu.SemaphoreType.DMA],
    )(src, pos)


def _positions_kernel(idx_ref, rank_ref, pstart_ref, pos_ref):
    erow = lax.broadcasted_iota(jnp.int32, (N_EXPERTS, TR), 0)
    pstart = pstart_ref[...]
    for k in range(TOP_K):
        hit = erow == idx_ref[0, k:k + 1, :]
        seg = jnp.sum(jnp.where(hit, pstart, 0), axis=0, keepdims=True)
        pos_ref[0, k:k + 1, :] = seg + rank_ref[0, k:k + 1, :]


def _positions_call(idx, rank, pad_start):
    n_tiles = idx.shape[0]
    row_spec = pl.BlockSpec((1, TOP_K, TR), lambda i: (i, 0, 0))
    return pl.pallas_call(
        _positions_kernel,
        grid=(n_tiles,),
        in_specs=[row_spec, row_spec, pl.BlockSpec((N_EXPERTS, 1), lambda i: (0, 0))],
        out_specs=row_spec,
        out_shape=jax.ShapeDtypeStruct((n_tiles, TOP_K, TR), jnp.int32),
    )(idx, rank, pad_start.reshape(N_EXPERTS, 1))


def _combine_dense_kernel(g_ref, w_ref, x1_ref, sh_ref, mod_ref, fg_ref, outc_ref, outl_ref, *, n_ctx_tiles):
    w = w_ref[...]
    acc_a = acc_b = None
    for k in range(TOP_K):
        ya, yb = _unpack_rows_native(g_ref[0, k])
        acc_a = w[:, k:k + 1] * ya if k == 0 else acc_a + w[:, k:k + 1] * ya
        acc_b = w[:, k:k + 1] * yb if k == 0 else acc_b + w[:, k:k + 1] * yb
    acc = jnp.concatenate([acc_a, acc_b], axis=1)
    x2 = x1_ref[...] + mod_ref[0, 5:6, :] * (acc + sh_ref[...])
    out = x2 * lax.rsqrt(jnp.mean(x2 * x2, axis=-1, keepdims=True) + EPS) * fg_ref[...]
    is_ctx = pl.program_id(0) < n_ctx_tiles

    @pl.when(is_ctx)
    def _():
        outc_ref[...] = out

    @pl.when(jnp.logical_not(is_ctx))
    def _():
        outl_ref[...] = out


def _combine_dense_call(gathered, wts, x1, shared, mod, final_g, n_ctx, lat_len):
    T, K = wts.shape
    DH = gathered.shape[-1]
    D = 2 * DH
    row = functools.partial(_mod_row, tokens_per_tile=TC, n_ctx=n_ctx, lat_len=lat_len)
    nct = n_ctx // TC
    return pl.pallas_call(
        functools.partial(_combine_dense_kernel, n_ctx_tiles=nct),
        grid=(T // TC,),
        in_specs=[pl.BlockSpec((1, K, TC, DH), lambda j: (j, 0, 0, 0)),
                  pl.BlockSpec((TC, K), lambda j: (j, 0)),
                  pl.BlockSpec((TC, D), lambda j: (j, 0)),
                  pl.BlockSpec((TC, D), lambda j: (j, 0)),
                  pl.BlockSpec((1, N_MOD, D), lambda j: (row(j), 0, 0)),
                  pl.BlockSpec((1, D), lambda j: (0, 0))],
        out_specs=[pl.BlockSpec((TC, D), lambda j: (jnp.minimum(j, nct - 1), 0)),
                   pl.BlockSpec((TC, D), lambda j: (jnp.maximum(j - nct, 0), 0))],
        out_shape=[jax.ShapeDtypeStruct((n_ctx, D), jnp.float32), jax.ShapeDtypeStruct((T - n_ctx, D), jnp.float32)],
        compiler_params=pltpu.CompilerParams(dimension_semantics=("arbitrary",)),
    )(gathered, wts, x1, shared, mod, final_g.reshape(1, D))


def _moe_routed(h2, h2p, router_w, router_bias, w_gate, w_up, w_down):
    T, D = h2.shape
    idx, rank, w_rows, cnt = _route_call(h2, router_w, router_bias)
    wts = w_rows.transpose(0, 2, 1).reshape(T, TOP_K)
    counts = cnt[:, 0]
    padded = (counts + BM - 1) // BM * BM
    pad_end = jnp.cumsum(padded)
    pad_start = (pad_end - padded).astype(jnp.int32)
    n_pad = T * TOP_K + N_EXPERTS * BM
    n_blk = n_pad // BM
    n_used = (pad_end[-1] // BM).astype(jnp.int32).reshape(1)
    pos = _positions_call(idx, rank, pad_start)
    x_sorted = _sc_scatter_call(h2p, pos.reshape(-1), n_pad)
    blk_row0 = jnp.arange(n_blk, dtype=jnp.int32) * BM
    blk_e = jnp.minimum(jnp.sum((pad_end[None, :] <= blk_row0[:, None]).astype(jnp.int32), axis=1), N_EXPERTS - 1)
    own = blk_e[:, None] == jnp.arange(N_EXPERTS, dtype=jnp.int32)[None, :]
    seg_end = jnp.sum(jnp.where(own, (pad_start + counts)[None, :], 0), axis=1)
    n_valid = jnp.clip(seg_end - blk_row0, 0, BM).astype(jnp.int32)
    y = _expert_call(x_sorted, blk_e, n_valid, n_used, w_gate, w_up, w_down)
    pos_t = pos.reshape(T // TR, TOP_K, TR // TC, TC).transpose(0, 2, 1, 3).reshape(T // TC, TOP_K * TC)
    return y, pos_t, wts


def kernel(x_prompt, x_sample, state_delta, c, c_ctx, w_ada, b_ada, norm1_g, w_in, conv_w, a_log,
           dt_bias, onorm_g, pool_w, pool_scale, w_out, norm2_g, router_w, router_bias, exp_w_gate,
           exp_w_up, exp_w_down, sh_w_gate, sh_w_up, sh_w_down, final_g):
    Bc, Lc, D = x_prompt.shape
    Bl, Ll, _ = x_sample.shape
    n_ctx = Bc * Lc
    assert DEPTH == 1 and 1 + Bl <= MOD_ROWS and n_ctx % Ll == 0
    x_parts = (x_prompt.reshape(n_ctx, D), x_sample.reshape(Bl * Ll, D))
    cvec = jnp.concatenate([c_ctx[None], c, jnp.zeros((MOD_ROWS - 1 - Bl, D), c.dtype)], axis=0)
    l = 0
    mod = _ada_call(cvec, w_ada[l], b_ada[l]).reshape(MOD_ROWS, N_MOD, D)
    qkv, z, ba, u = _inproj_call(*x_parts, mod, norm1_g[l], w_in[l], Ll)
    dn = (conv_w[l], a_log[l], dt_bias[l], onorm_g[l])
    oa_c, st_ctx = _delta_call(qkv, z, ba, *dn, None, Bc, Lc, 0)
    oa_l, _ = _delta_call(qkv, z, ba, *dn, state_delta[:, l], Bl, Ll, n_ctx // Ll)
    op_c = _pool_call(u, pool_w[l], pool_scale[l], False, Bc, Lc, 0)
    op_l = _pool_call(u, pool_w[l], pool_scale[l], True, Bl, Ll, n_ctx // Ll)
    x1, h2, h2p, shared = _outproj_call(x_parts, (oa_c, oa_l), (op_c, op_l), mod, norm2_g[l], w_out[l],
                                        sh_w_gate[l], sh_w_up[l], sh_w_down[l], Ll)
    y, pos_t, wts = _moe_routed(h2, h2p, router_w[l], router_bias[l], exp_w_gate[l], exp_w_up[l],
                                exp_w_down[l])
    T = n_ctx + Bl * Ll
    gathered = _sc_gather_call(y, pos_t.reshape(-1))
    out_c, out_l = _combine_dense_call(gathered.reshape(T // TC, TOP_K, TC, D // 2), wts, x1, shared, mod,
                                       final_g, n_ctx, Ll)
    y_prompt = out_c.reshape(Bc, Lc, D)
    y_sample = out_l.reshape(Bl, Ll, D)
    new_state_delta = st_ctx[:, None].astype(x_prompt.dtype)
    return (y_prompt, y_sample, new_state_delta)
```

```python
import functools
import jax, jax.numpy as jnp
from jax import lax
from jax.experimental import pallas as pl
from jax.experimental.pallas import tpu as pltpu
from jax.experimental.pallas import tpu_sc as plsc

D_MODEL = 1024
DEPTH = 1
GRID_W = 64
D_MIX = D_MODEL
D_A = D_MIX // 2
D_P = D_MIX - D_A
H_A = 4
DK = D_A // H_A
DV = D_A // H_A
CONV_K = 5
CHUNK = 64
POOL_WINDOWS = (2, 4, 8, 16)
N_PG = len(POOL_WINDOWS)
PG = D_P // N_PG
N_EXPERTS = 256
TOP_K = 8
N_GROUPS = 8
TOPK_GROUP = 4
ROUTED_SCALE = 2.5
EPS = 1e-6
VMEM_LIMIT = 48 * 1024 * 1024


def _split_bf16(a):
    hi = a.astype(jnp.bfloat16)
    return hi, (a - hi.astype(jnp.float32)).astype(jnp.bfloat16)


def _bdot(a, b):
    return jnp.dot(a, b, preferred_element_type=jnp.float32)


def _pack_rows(x):
    m = x.shape[1] // 2
    lo = lax.bitcast_convert_type(x[:, :m].astype(jnp.bfloat16).astype(jnp.float32), jnp.uint32)
    hi = lax.bitcast_convert_type(x[:, m:].astype(jnp.bfloat16).astype(jnp.float32), jnp.uint32)
    return lax.bitcast_convert_type(hi | (lo >> 16), jnp.int32)


def _pack_rows_native(x):
    m = x.shape[1] // 2
    return pltpu.pack_elementwise([x[:, :m], x[:, m:]], packed_dtype=jnp.bfloat16)


def _unpack_rows_native(p):
    return tuple(pltpu.unpack_elementwise(p, index=i, packed_dtype=jnp.bfloat16, unpacked_dtype=jnp.float32)
                 for i in range(2))


N_MOD = 6
MOD_ROWS = 8
TM = 512


def _ada_kernel(c_ref, w_ref, b_ref, o_ref):
    c = c_ref[...]
    s = c * jax.nn.sigmoid(c)
    sh, sl = _split_bf16(s)
    wh, wl = _split_bf16(w_ref[...])
    o_ref[...] = _bdot(sh, wh) + (_bdot(sh, wl) + _bdot(sl, wh)) + b_ref[...]


def _ada_call(cvec, w_ada, b_ada):
    R, D = cvec.shape
    N = w_ada.shape[1]
    tn = 1024
    return pl.pallas_call(
        _ada_kernel,
        grid=(N // tn,),
        in_specs=[pl.BlockSpec((R, D), lambda j: (0, 0)),
                  pl.BlockSpec((D, tn), lambda j: (0, j)),
                  pl.BlockSpec((1, tn), lambda j: (0, j))],
        out_specs=pl.BlockSpec((R, tn), lambda j: (0, j)),
        out_shape=jax.ShapeDtypeStruct((R, N), jnp.float32),
    )(cvec, w_ada, b_ada.reshape(1, N))


def _mod_row(tile, tokens_per_tile, n_ctx, lat_len):
    t0 = tile * tokens_per_tile
    return jnp.where(t0 < n_ctx, 0, 1 + (t0 - n_ctx) // lat_len)


def _two_part_specs(n_ctx_tiles, width):
    return (pl.BlockSpec((TM, width), lambda i: (jnp.minimum(i, n_ctx_tiles - 1), 0)),
            pl.BlockSpec((TM, width), lambda i: (jnp.maximum(i - n_ctx_tiles, 0), 0)))


def _pick(n_ctx_tiles, ctx_ref, lat_ref):
    return jnp.where(pl.program_id(0) < n_ctx_tiles, ctx_ref[...], lat_ref[...])


def _inproj_kernel(xc_ref, xl_ref, mod_ref, g_ref, wq_ref, wz_ref, wb_ref, wu_ref, q_ref, z_ref, b_ref, u_ref,
                   *, n_ctx_tiles):
    x = _pick(n_ctx_tiles, xc_ref, xl_ref)
    y = x * lax.rsqrt(jnp.mean(x * x, axis=-1, keepdims=True) + EPS) * g_ref[...]
    h = (y * (1.0 + mod_ref[0, 1:2, :]) + mod_ref[0, 0:1, :]).astype(jnp.bfloat16)
    q_ref[...] = _bdot(h, wq_ref[...])
    z_ref[...] = _bdot(h, wz_ref[...])
    b_ref[...] = _bdot(h, wb_ref[...])
    u_ref[...] = _bdot(h, wu_ref[...])


def _inproj_call(x_ctx, x_lat, mod, norm1_g, w_in, lat_len):
    n_ctx, D = x_ctx.shape
    T = n_ctx + x_lat.shape[0]
    bf = jnp.bfloat16
    nq, nz, nb = 3 * D_A, D_A, 4 * H_A
    wq = w_in[:, :nq].astype(bf)
    wz = w_in[:, nq:nq + nz].astype(bf)
    wb = jnp.pad(w_in[:, nq + nz:nq + nz + nb], ((0, 0), (0, 128 - nb))).astype(bf)
    wu = w_in[:, nq + nz + nb:].astype(bf)
    row = functools.partial(_mod_row, tokens_per_tile=TM, n_ctx=n_ctx, lat_len=lat_len)

    def full(a):
        return pl.BlockSpec(a.shape, lambda i: (0, 0))

    def rows(n):
        return pl.BlockSpec((TM, n), lambda i: (i, 0))

    return pl.pallas_call(
        functools.partial(_inproj_kernel, n_ctx_tiles=n_ctx // TM),
        grid=(T // TM,),
        in_specs=[*_two_part_specs(n_ctx // TM, D), pl.BlockSpec((1, N_MOD, D), lambda i: (row(i), 0, 0)),
                  pl.BlockSpec((1, D), lambda i: (0, 0)), full(wq), full(wz), full(wb), full(wu)],
        out_specs=[rows(nq), rows(nz), rows(128), rows(D_P)],
        out_shape=[jax.ShapeDtypeStruct((T, nq), jnp.float32), jax.ShapeDtypeStruct((T, nz), jnp.float32),
                   jax.ShapeDtypeStruct((T, 128), jnp.float32), jax.ShapeDtypeStruct((T, D_P), jnp.float32)],
        compiler_params=pltpu.CompilerParams(dimension_semantics=("arbitrary",),
                                             vmem_limit_bytes=VMEM_LIMIT),
    )(x_ctx, x_lat, mod, norm1_g.reshape(1, D), wq, wz, wb, wu)


PT = 256


def _window_bounds(pos, w, n):
    return jnp.maximum(pos - w // 2, 0), jnp.minimum(pos + w - w // 2, n)


def _band_sum(band, x):
    xh, xl = _split_bf16(x)
    return _bdot(band, xh) + _bdot(band, xl)


def _pool_seq_kernel(u_ref, pw_ref, ps_ref, o_ref):
    L = u_ref.shape[0]
    ti = lax.broadcasted_iota(jnp.int32, (L, L), 0)
    ji = lax.broadcasted_iota(jnp.int32, (L, L), 1)
    tcol = lax.broadcasted_iota(jnp.int32, (L, 1), 0)
    for i, w in enumerate(POOL_WINDOWS):
        lo, hi = _window_bounds(ti, w, L)
        band = ((ji >= lo) & (ji < hi)).astype(jnp.bfloat16)
        clo, chi = _window_bounds(tcol, w, L)
        ug = u_ref[:, i * PG:(i + 1) * PG]
        mean = _band_sum(band, ug) / (chi - clo).astype(jnp.float32)
        d = (mean - ug).astype(jnp.bfloat16)
        o_ref[:, i * PG:(i + 1) * PG] = _bdot(d, pw_ref[i]) * ps_ref[:, i * PG:(i + 1) * PG]


def _pool_grid_kernel(u_ref, pw_ref, ps_ref, o_ref, pad_s, r_s):
    L = u_ref.shape[0]
    rows = L // GRID_W
    halo = (max(POOL_WINDOWS) // 2) * GRID_W
    pad_s[0:halo, :] = jnp.zeros((halo, D_P), jnp.float32)
    pad_s[halo + L:, :] = jnp.zeros((halo, D_P), jnp.float32)
    pad_s[halo:halo + L, :] = u_ref[...]
    ti = lax.broadcasted_iota(jnp.int32, (PT, PT), 0)
    ji = lax.broadcasted_iota(jnp.int32, (PT, PT), 1)
    tcol = lax.broadcasted_iota(jnp.int32, (PT, 1), 0)
    for i, w in enumerate(POOL_WINDOWS):
        cs = slice(i * PG, (i + 1) * PG)
        acc = None
        for dr in range(-(w // 2), w - w // 2):
            part = pad_s[halo + dr * GRID_W:halo + dr * GRID_W + L, cs]
            acc = part if acc is None else acc + part
        r_s[...] = acc
        lo, hi = _window_bounds(ti % GRID_W, w, GRID_W)
        band = ((ji // GRID_W == ti // GRID_W) & (ji % GRID_W >= lo) & (ji % GRID_W < hi)).astype(jnp.bfloat16)
        clo, chi = _window_bounds(tcol % GRID_W, w, GRID_W)
        ccnt = (chi - clo).astype(jnp.float32)
        for tile in range(L // PT):
            ts = slice(tile * PT, (tile + 1) * PT)
            rlo, rhi = _window_bounds(tile * (PT // GRID_W) + tcol // GRID_W, w, rows)
            mean = _band_sum(band, r_s[ts, :]) / ((rhi - rlo).astype(jnp.float32) * ccnt)
            d = (mean - u_ref[ts, cs]).astype(jnp.bfloat16)
            o_ref[ts, cs] = _bdot(d, pw_ref[i]) * ps_ref[:, cs]


def _pool_call(u, pool_w, pool_scale, grid, B, L, row_blk0):
    pw = pool_w.astype(jnp.bfloat16)
    ps = pool_scale.reshape(1, D_P)
    specs = dict(
        grid=(B,),
        in_specs=[pl.BlockSpec((L, D_P), lambda b: (row_blk0 + b, 0)),
                  pl.BlockSpec((N_PG, PG, PG), lambda b: (0, 0, 0)),
                  pl.BlockSpec((1, D_P), lambda b: (0, 0))],
        out_specs=pl.BlockSpec((L, D_P), lambda b: (b, 0)),
        out_shape=jax.ShapeDtypeStruct((B * L, D_P), jnp.float32),
        compiler_params=pltpu.CompilerParams(dimension_semantics=("arbitrary",),
                                             vmem_limit_bytes=VMEM_LIMIT))
    if not grid:
        return pl.pallas_call(_pool_seq_kernel, **specs)(u, pw, ps)
    halo = (max(POOL_WINDOWS) // 2) * GRID_W
    return pl.pallas_call(
        _pool_grid_kernel,
        scratch_shapes=[pltpu.VMEM((L + 2 * halo, D_P), jnp.float32), pltpu.VMEM((L, PG), jnp.float32)],
        **specs)(u, pw, ps)


def _outproj_kernel(xc_ref, xl_ref, oac_ref, oal_ref, opc_ref, opl_ref, mod_ref, g2_ref, wo_ref, sg_ref, su_ref,
                    sd_ref, x1_ref, h2_ref, h2p_ref, sh_ref, *, n_ctx_tiles):
    o_a = _pick(n_ctx_tiles, oac_ref, oal_ref)
    o_p = _pick(n_ctx_tiles, opc_ref, opl_ref)
    mix = (_bdot(o_a.astype(jnp.bfloat16), wo_ref[:D_A, :])
           + _bdot(o_p.astype(jnp.bfloat16), wo_ref[D_A:, :]))
    x1 = _pick(n_ctx_tiles, xc_ref, xl_ref) + mod_ref[0, 2:3, :] * mix
    x1_ref[...] = x1
    y = x1 * lax.rsqrt(jnp.mean(x1 * x1, axis=-1, keepdims=True) + EPS) * g2_ref[...]
    h2 = y * (1.0 + mod_ref[0, 4:5, :]) + mod_ref[0, 3:4, :]
    h2_ref[...] = h2
    h2p_ref[...] = _pack_rows(h2)
    hb = h2.astype(jnp.bfloat16)
    g = _bdot(hb, sg_ref[...])
    a = (g * jax.nn.sigmoid(g)) * _bdot(hb, su_ref[...])
    sh_ref[...] = _bdot(a.astype(jnp.bfloat16), sd_ref[...])


def _outproj_call(x_parts, oa_parts, op_parts, mod, norm2_g, w_out, sh_gate, sh_up, sh_down, lat_len):
    n_ctx, D = x_parts[0].shape
    T = n_ctx + x_parts[1].shape[0]
    nct = n_ctx // TM
    bf = jnp.bfloat16
    row = functools.partial(_mod_row, tokens_per_tile=TM, n_ctx=n_ctx, lat_len=lat_len)
    ws = [w_out.astype(bf), sh_gate.astype(bf), sh_up.astype(bf), sh_down.astype(bf)]

    def rows(n):
        return pl.BlockSpec((TM, n), lambda i: (i, 0))

    return pl.pallas_call(
        functools.partial(_outproj_kernel, n_ctx_tiles=nct),
        grid=(T // TM,),
        in_specs=[*_two_part_specs(nct, D), *_two_part_specs(nct, D_A), *_two_part_specs(nct, D_P),
                  pl.BlockSpec((1, N_MOD, D), lambda i: (row(i), 0, 0)),
                  pl.BlockSpec((1, D), lambda i: (0, 0))] + [pl.BlockSpec(w.shape, lambda i: (0, 0)) for w in ws],
        out_specs=[rows(D), rows(D), rows(D // 2), rows(D)],
        out_shape=[jax.ShapeDtypeStruct((T, D), jnp.float32), jax.ShapeDtypeStruct((T, D), jnp.float32),
                   jax.ShapeDtypeStruct((T, D // 2), jnp.int32), jax.ShapeDtypeStruct((T, D), jnp.float32)],
        compiler_params=pltpu.CompilerParams(dimension_semantics=("arbitrary",),
                                             vmem_limit_bytes=VMEM_LIMIT),
    )(*x_parts, *oa_parts, *op_parts, mod, norm2_g.reshape(1, D), *ws)


SC = 256
CPS = SC // CHUNK
BASE = 16
DELTA_HEAD_ROWS = 4096


def _mm(a, b):
    return jnp.dot(a.astype(jnp.bfloat16), b.astype(jnp.bfloat16), preferred_element_type=jnp.float32)


def _mm_nt(a, b):
    return lax.dot_general(a.astype(jnp.bfloat16), b.astype(jnp.bfloat16), (((1,), (1,)), ((), ())),
                           preferred_element_type=jnp.float32)


def _softplus(x):
    return jnp.maximum(x, 0.0) + jnp.log(1.0 + jnp.exp(-jnp.abs(x)))


def _delta_kernel(sc_ref, xq_ref, xk_ref, xv_ref, z_ref, bac_ref, bar_ref, cwq_ref, cwk_ref, cwv_ref,
                  og_ref, s0_ref, o_ref, st_ref, q_s, k_s, v_s, o_s, vn_s, *, n_sc, zero_init, hpb):
    hb = pl.program_id(1)
    L = q_s.shape[1]

    def conv(x_ref, w_ref, cs):
        x = x_ref[:, cs]
        row = lax.broadcasted_iota(jnp.int32, x.shape, 0)
        acc = x * w_ref[CONV_K // 2:CONV_K // 2 + 1, cs]
        for j in range(CONV_K):
            d = j - CONV_K // 2
            if d == 0:
                continue
            xs = pltpu.roll(x, (-d) % L, 0)
            ok = (row + d >= 0) & (row + d < L)
            acc = acc + jnp.where(ok, xs, 0.0) * w_ref[j:j + 1, cs]
        return acc * jax.nn.sigmoid(acc)

    for hh in range(hpb):
        cs = slice(hh * DK, (hh + 1) * DK)
        q = conv(xq_ref, cwq_ref, cs)
        q_s[hh] = q * lax.rsqrt(jnp.sum(q * q, axis=-1, keepdims=True) + EPS) * (DK ** -0.5)
        k = conv(xk_ref, cwk_ref, cs)
        k_s[hh] = k * lax.rsqrt(jnp.sum(k * k, axis=-1, keepdims=True) + EPS)
        v_s[hh] = conv(xv_ref, cwv_ref, cs)
    o_s[...] = jnp.zeros_like(o_s)

    ri = lax.broadcasted_iota(jnp.int32, (SC, SC), 0)
    ci = lax.broadcasted_iota(jnp.int32, (SC, SC), 1)
    same = (ri // CHUNK) == (ci // CHUNK)
    same_base = (ri // BASE) == (ci // BASE)
    merge_masks = [(ri // w) == (ci // w) for w in (2 * BASE, CHUNK)]
    eye = (ri == ci).astype(jnp.float32)
    rowi = lax.broadcasted_iota(jnp.int32, (SC, DV), 0)

    def prep(m, d, hh):
        r0 = pl.multiple_of(m * SC, SC)
        h = hb * hpb + hh
        q = q_s[hh, pl.ds(r0, SC), :]
        k = k_s[hh, pl.ds(r0, SC), :]
        v = v_s[hh, pl.ds(r0, SC), :]
        bc = bac_ref[0, hh, pl.ds(r0, SC), :]
        br = bar_ref[0, hh, m]
        a_l = sc_ref[d * H_A + h]
        dtb = sc_ref[2 * H_A + d * H_A + h]
        neg_ea = -jnp.exp(jnp.full((1, 1), a_l, jnp.float32))
        beta = jax.nn.sigmoid(bc[:, d:d + 1])
        g_col = neg_ea * _softplus(bc[:, 2 + d:3 + d] + dtb)
        g_row = neg_ea * _softplus(br[2 + d:3 + d, :] + dtb)
        if d == 0:
            tri, strict = same & (ci <= ri), same & (ci < ri)
        else:
            tri, strict = same & (ci >= ri), same & (ci > ri)
        tri_t = same & (ri <= ci) if d == 0 else same & (ri >= ci)
        gc_col = jnp.sum(jnp.where(tri, g_row, 0.0), axis=1, keepdims=True)
        gc_row = jnp.sum(jnp.where(tri_t, g_col, 0.0), axis=0, keepdims=True)
        gl_col = jnp.sum(jnp.where(same, g_row, 0.0), axis=1, keepdims=True)
        decay = jnp.where(tri, jnp.exp(jnp.where(tri, gc_col - gc_row, 0.0)), 0.0)
        kb = k * beta
        a = jnp.where(strict, _mm_nt(kb, k) * decay, 0.0)
        attn = jnp.where(tri, _mm_nt(q, k) * decay, 0.0)
        eg = jnp.exp(gc_col)
        x = jnp.concatenate([v * beta, kb * eg], axis=1)
        qd = q * eg
        kdt = (k * jnp.exp(gl_col - gc_col)).T
        return dict(r0=r0, a=a, attn=attn, x=x, qd=qd, kdt=kdt, egl=jnp.exp(gl_col))

    def run_chains(ms, states):
        n = len(chains)
        ops = [prep(ms[i], d, hh) for i, (hh, d) in enumerate(chains)]
        ps = [jnp.where(same_base, o["a"], 0.0) for o in ops]
        ts = [eye - p for p in ps]
        for _ in range(BASE.bit_length() - 2):
            ps = [_mm(p, p) for p in ps]
            ts = [t + _mm(t, p) for t, p in zip(ts, ps)]
        inner = same_base
        for outer in merge_masks:
            lows = [_mm(jnp.where(outer & ~inner, o["a"], 0.0), t) for o, t in zip(ops, ts)]
            ts = [t - _mm(t, low) for t, low in zip(ts, lows)]
            inner = outer
        xs = [_mm(t, o["x"]) for t, o in zip(ts, ops)]
        for i in range(n):
            vn_s[i] = jnp.zeros((SC, DV), jnp.float32)
        states = list(states)
        for step in range(CPS):
            cs = [step if d == 0 else CPS - 1 - step for _, d in chains]
            los = [c * CHUNK for c in cs]
            ws_qs = [_mm(jnp.concatenate([x[lo:lo + CHUNK, DV:], o["qd"][lo:lo + CHUNK]], axis=0), s)
                     for x, o, lo, s in zip(xs, ops, los, states)]
            for i in range(n):
                vn_s[i, los[i]:los[i] + CHUNK, :] = xs[i][los[i]:los[i] + CHUNK, :DV] - ws_qs[i][:CHUNK]
            vns = [vn_s[i] for i in range(n)]
            o_cs = [wq[CHUNK:] + _mm(o["attn"][lo:lo + CHUNK, :], vn)
                    for wq, o, lo, vn in zip(ws_qs, ops, los, vns)]
            for i, (hh, _) in enumerate(chains):
                o_s[hh, pl.ds(ops[i]["r0"] + los[i], CHUNK), :] += o_cs[i]
            states = [s * o["egl"][lo:lo + 1, :]
                      + _mm(o["kdt"], jnp.where((rowi >= lo) & (rowi < lo + CHUNK), vn, 0.0))
                      for s, o, lo, vn in zip(states, ops, los, vns)]
        return tuple(states)

    if zero_init:
        states = tuple(jnp.zeros((DK, DV), jnp.float32) for _ in range(2 * hpb))
    else:
        states = tuple(s0_ref[0, d, hh] for hh in range(hpb) for d in range(2))

    chains = [(hh, d) for hh in range(hpb) for d in range(2)]

    def body(m, carry):
        return run_chains([m if d == 0 else n_sc - 1 - m for _, d in chains], carry)

    if n_sc == 1:
        states = body(0, states)
    else:
        states = lax.fori_loop(0, n_sc, body, states)

    for hh in range(hpb):
        for d in range(2):
            st_ref[0, d, hh] = states[2 * hh + d]
        o = o_s[hh]
        o = o * lax.rsqrt(jnp.mean(o * o, axis=-1, keepdims=True) + EPS) * og_ref[...]
        zz = z_ref[:, hh * DV:(hh + 1) * DV]
        o_ref[:, hh * DV:(hh + 1) * DV] = o * (zz * jax.nn.sigmoid(zz))


def _delta_call(qkv, z, ba, conv_w, a_log, dt_bias, onorm_g, s0, B, L, row_blk0):
    n_sc = L // SC
    t0 = row_blk0 * L
    bah = ba[t0:t0 + B * L, :4 * H_A].reshape(B, L, 4, H_A).transpose(0, 3, 1, 2)
    bar = bah.reshape(B, H_A, n_sc, SC, 4).transpose(0, 1, 2, 4, 3)
    scal = jnp.concatenate([a_log.reshape(-1), dt_bias.reshape(-1)]).astype(jnp.float32)
    hpb = max(1, min(H_A, DELTA_HEAD_ROWS // L))
    n_hb = H_A // hpb
    zero_init = s0 is None
    if zero_init:
        s0 = jnp.zeros((1, 2, hpb, DK, DV), jnp.float32)
        s0_spec = pl.BlockSpec((1, 2, hpb, DK, DV), lambda b, h, sc: (0, 0, 0, 0, 0))
    else:
        s0_spec = pl.BlockSpec((1, 2, hpb, DK, DV), lambda b, h, sc: (b, 0, h, 0, 0))

    def col(off):
        return pl.BlockSpec((L, hpb * DK), lambda b, h, sc: (row_blk0 + b, off * n_hb + h))

    def cw(off):
        return pl.BlockSpec((CONV_K, hpb * DK), lambda b, h, sc: (0, off * n_hb + h))

    kern = functools.partial(_delta_kernel, n_sc=n_sc, zero_init=zero_init, hpb=hpb)
    return pl.pallas_call(
        kern,
        grid_spec=pltpu.PrefetchScalarGridSpec(
            num_scalar_prefetch=1,
            grid=(B, n_hb),
            in_specs=[col(0), col(1), col(2),
                      pl.BlockSpec((L, hpb * DV), lambda b, h, sc: (row_blk0 + b, h)),
                      pl.BlockSpec((1, hpb, L, 4), lambda b, h, sc: (b, h, 0, 0)),
                      pl.BlockSpec((1, hpb, n_sc, 4, SC), lambda b, h, sc: (b, h, 0, 0, 0)),
                      cw(0), cw(1), cw(2),
                      pl.BlockSpec((1, DV), lambda b, h, sc: (0, 0)),
                      s0_spec],
            out_specs=[pl.BlockSpec((L, hpb * DV), lambda b, h, sc: (b, h)),
                       pl.BlockSpec((1, 2, hpb, DK, DV), lambda b, h, sc: (b, 0, h, 0, 0))],
            scratch_shapes=[pltpu.VMEM((hpb, L, DK), jnp.float32), pltpu.VMEM((hpb, L, DK), jnp.float32),
                            pltpu.VMEM((hpb, L, DV), jnp.float32), pltpu.VMEM((hpb, L, DV), jnp.float32),
                            pltpu.VMEM((2 * hpb, SC, DV), jnp.float32)]),
        out_shape=[jax.ShapeDtypeStruct((B * L, D_A), jnp.float32),
                   jax.ShapeDtypeStruct((B, 2, H_A, DK, DV), jnp.float32)],
        compiler_params=pltpu.CompilerParams(dimension_semantics=("arbitrary", "arbitrary"),
                                             vmem_limit_bytes=VMEM_LIMIT),
    )(scal, qkv, qkv, qkv, z, bah, bar, conv_w, conv_w, conv_w, onorm_g.reshape(1, DV), s0)


TR = 256
GSZ = N_EXPERTS // N_GROUPS
NEG = -jnp.inf
BM = 768


def _route_kernel(h_ref, rwh_ref, rwl_ref, rb_ref, idx_ref, rank_ref, w_ref, cnt_ref, cnt_s):
    i = pl.program_id(0)

    @pl.when(i == 0)
    def _():
        cnt_s[...] = jnp.zeros_like(cnt_s)

    h = h_ref[...]
    hh, hl = _split_bf16(h)
    logits = _bdot(hh, rwh_ref[...]) + (_bdot(hh, rwl_ref[...]) + _bdot(hl, rwh_ref[...]))
    scores = jax.nn.sigmoid(logits.T)
    sel = scores + rb_ref[...]
    erow = lax.broadcasted_iota(jnp.int32, sel.shape, 0)
    grow = lax.broadcasted_iota(jnp.int32, (GSZ, TR), 0)

    def first_argmax(v, rows):
        m = jnp.max(v, axis=0, keepdims=True)
        first = jnp.min(jnp.where(v == m, rows, N_EXPERTS), axis=0, keepdims=True)
        return m, first

    gs = []
    for g in range(N_GROUPS):
        vg = sel[g * GSZ:(g + 1) * GSZ, :]
        m1, i1 = first_argmax(vg, grow)
        m2 = jnp.max(jnp.where(grow == i1, NEG, vg), axis=0, keepdims=True)
        gs.append(m1 + m2)
    cand = []
    for g in range(N_GROUPS):
        beat = jnp.zeros(gs[g].shape, jnp.int32)
        for o in range(N_GROUPS):
            if o == g:
                continue
            wins = (gs[o] > gs[g]) | ((gs[o] == gs[g]) & (o < g))
            beat = beat + wins.astype(jnp.int32)
        cand.append(jnp.where(beat < TOPK_GROUP, sel[g * GSZ:(g + 1) * GSZ, :], NEG))
    cand = jnp.concatenate(cand, axis=0)
    chosen = []
    picked = jnp.zeros(sel.shape, jnp.bool_)
    for _ in range(TOP_K):
        _, ik = first_argmax(cand, erow)
        hit = erow == ik
        chosen.append((ik, hit))
        picked = picked | hit
        cand = jnp.where(hit, NEG, cand)
    wsum = jnp.sum(jnp.where(picked, scores, 0.0), axis=0, keepdims=True)

    ri = lax.broadcasted_iota(jnp.int32, (TR, TR), 0)
    ci = lax.broadcasted_iota(jnp.int32, (TR, TR), 1)
    earlier = (ri < ci).astype(jnp.bfloat16)
    rank_mat = _bdot(picked.astype(jnp.bfloat16), earlier) + cnt_s[...]
    cnt_s[...] = cnt_s[...] + jnp.sum(picked.astype(jnp.float32), axis=1, keepdims=True)
    cnt_ref[...] = cnt_s[...].astype(jnp.int32)

    for k, (ik, hit) in enumerate(chosen):
        idx_ref[0, k:k + 1, :] = ik
        rank_ref[0, k:k + 1, :] = jnp.sum(jnp.where(hit, rank_mat, 0.0), axis=0, keepdims=True).astype(jnp.int32)
        w_ref[0, k:k + 1, :] = jnp.sum(jnp.where(hit, scores, 0.0), axis=0, keepdims=True) / wsum * ROUTED_SCALE


def _route_call(hf, router_w, router_bias):
    T, D = hf.shape
    n_tiles = T // TR
    rwh, rwl = _split_bf16(router_w)
    row_spec = pl.BlockSpec((1, TOP_K, TR), lambda i: (i, 0, 0))
    return pl.pallas_call(
        _route_kernel,
        grid=(n_tiles,),
        in_specs=[pl.BlockSpec((TR, D), lambda i: (i, 0)),
                  pl.BlockSpec((D, N_EXPERTS), lambda i: (0, 0)),
                  pl.BlockSpec((D, N_EXPERTS), lambda i: (0, 0)),
                  pl.BlockSpec((N_EXPERTS, 1), lambda i: (0, 0))],
        out_specs=[row_spec, row_spec, row_spec, pl.BlockSpec((N_EXPERTS, 1), lambda i: (0, 0))],
        scratch_shapes=[pltpu.VMEM((N_EXPERTS, 1), jnp.float32)],
        out_shape=[jax.ShapeDtypeStruct((n_tiles, TOP_K, TR), jnp.int32),
                   jax.ShapeDtypeStruct((n_tiles, TOP_K, TR), jnp.int32),
                   jax.ShapeDtypeStruct((n_tiles, TOP_K, TR), jnp.float32),
                   jax.ShapeDtypeStruct((N_EXPERTS, 1), jnp.int32)],
        compiler_params=pltpu.CompilerParams(dimension_semantics=("arbitrary",)),
    )(hf, rwh, rwl, router_bias.reshape(N_EXPERTS, 1).astype(jnp.float32))


def _expert_kernel(blk_e_ref, nvalid_ref, nused_ref, x_ref, wg_ref, wu_ref, wd_ref, y_ref, wg_s, wu_s, wd_s):
    i = pl.program_id(0)

    @pl.when(i < nused_ref[0])
    def _():
        e = blk_e_ref[i]
        prev = blk_e_ref[jnp.maximum(i - 1, 0)]

        @pl.when((i == 0) | (e != prev))
        def _():
            wg_s[...] = wg_ref[0].astype(jnp.bfloat16)
            wu_s[...] = wu_ref[0].astype(jnp.bfloat16)
            wd_s[...] = wd_ref[0].astype(jnp.bfloat16)

        row = lax.broadcasted_iota(jnp.int32, (BM, 1), 0)
        xa, xb = _unpack_rows_native(jnp.where(row < nvalid_ref[i], x_ref[...], 0))
        xa = xa.astype(jnp.bfloat16)
        xb = xb.astype(jnp.bfloat16)
        half = xa.shape[1]
        g = _bdot(xa, wg_s[:half, :]) + _bdot(xb, wg_s[half:, :])
        u = _bdot(xa, wu_s[:half, :]) + _bdot(xb, wu_s[half:, :])
        a = (g * jax.nn.sigmoid(g)) * u
        y_ref[...] = _pack_rows_native(_bdot(a.astype(jnp.bfloat16), wd_s[...]))


def _expert_call(x_sorted, blk_e, n_valid, n_used, w_gate, w_up, w_down):
    n_pad, DH = x_sorted.shape
    n_blk = n_pad // BM
    E, D, F = w_gate.shape

    def row_map(i, be, nv, nu):
        return (jnp.minimum(i, nu[0] - 1), 0)

    def w_map(i, be, nv, nu):
        return (be[jnp.minimum(i, nu[0] - 1)], 0, 0)

    return pl.pallas_call(
        _expert_kernel,
        grid_spec=pltpu.PrefetchScalarGridSpec(
            num_scalar_prefetch=3,
            grid=(n_blk,),
            in_specs=[pl.BlockSpec((BM, DH), row_map),
                      pl.BlockSpec((1, D, F), w_map),
                      pl.BlockSpec((1, D, F), w_map),
                      pl.BlockSpec((1, F, D), w_map)],
            out_specs=pl.BlockSpec((BM, DH), row_map),
            scratch_shapes=[pltpu.VMEM((D, F), jnp.bfloat16), pltpu.VMEM((D, F), jnp.bfloat16),
                            pltpu.VMEM((F, D), jnp.bfloat16)]),
        out_shape=jax.ShapeDtypeStruct((n_pad, DH), jnp.int32),
        compiler_params=pltpu.CompilerParams(dimension_semantics=("arbitrary",),
                                             vmem_limit_bytes=VMEM_LIMIT),
    )(blk_e, n_valid, n_used, x_sorted, w_gate, w_up, w_down)


TC = 128


SC_CORES = 2
SC_SUBCORES = 16
SC_CHUNK = 128


def _sc_gather_call(table, idx):
    n_idx = idx.shape[0]
    width = table.shape[1]
    n_workers = SC_CORES * SC_SUBCORES
    per_worker = n_idx // n_workers
    assert per_worker * n_workers == n_idx and per_worker % SC_CHUNK == 0
    mesh = plsc.VectorSubcoreMesh(core_axis_name="c", subcore_axis_name="s")

    def body(table_hbm, idx_hbm, out_hbm, idx_v, rows_v, sem):
        wid = lax.axis_index("s") * SC_CORES + lax.axis_index("c")
        base = wid * per_worker

        @pl.loop(0, per_worker // SC_CHUNK)
        def _(ch):
            off = base + ch * SC_CHUNK
            pltpu.sync_copy(idx_hbm.at[pl.ds(off, SC_CHUNK)], idx_v)
            pltpu.async_copy(table_hbm.at[idx_v], rows_v, sem).wait()
            pltpu.sync_copy(rows_v, out_hbm.at[pl.ds(off, SC_CHUNK)])

    return pl.kernel(
        body, out_type=jax.ShapeDtypeStruct((n_idx, width), table.dtype), mesh=mesh,
        scratch_types=[pltpu.VMEM((SC_CHUNK,), jnp.int32), pltpu.VMEM((SC_CHUNK, width), table.dtype),
                       pltpu.SemaphoreType.DMA],
    )(table, idx)


def _sc_scatter_call(src, pos, n_out):
    n_idx = pos.shape[0]
    width = src.shape[1]
    n_workers = SC_CORES * SC_SUBCORES
    per_worker = n_idx // n_workers
    assert per_worker * n_workers == n_idx and per_worker % SC_CHUNK == 0 and TR % SC_CHUNK == 0
    mesh = plsc.VectorSubcoreMesh(core_axis_name="c", subcore_axis_name="s")
    tile_pairs = TOP_K * TR

    def body(src_hbm, pos_hbm, out_hbm, idx_v, rows_v, sem):
        wid = lax.axis_index("s") * SC_CORES + lax.axis_index("c")
        base = wid * per_worker

        @pl.loop(0, per_worker // SC_CHUNK)
        def _(ch):
            off = base + ch * SC_CHUNK
            row0 = (off // tile_pairs) * TR + off % TR
            pltpu.sync_copy(pos_hbm.at[pl.ds(off, SC_CHUNK)], idx_v)
            pltpu.sync_copy(src_hbm.at[pl.ds(row0, SC_CHUNK)], rows_v)
            pltpu.async_copy(rows_v, out_hbm.at[idx_v], sem).wait()

    return pl.kernel(
        body, out_type=jax.ShapeDtypeStruct((n_out, width), src.dtype), mesh=mesh,
        scratch_types=[pltpu.VMEM((SC_CHUNK,), jnp.int32), pltpu.VMEM((SC_CHUNK, width), src.dtype),
                       pltpu.SemaphoreType.DMA],
    )(src, pos)


def _positions_kernel(idx_ref, rank_ref, pstart_ref, pos_ref):
    erow = lax.broadcasted_iota(jnp.int32, (N_EXPERTS, TR), 0)
    pstart = pstart_ref[...]
    for k in range(TOP_K):
        hit = erow == idx_ref[0, k:k + 1, :]
        seg = jnp.sum(jnp.where(hit, pstart, 0), axis=0, keepdims=True)
        pos_ref[0, k:k + 1, :] = seg + rank_ref[0, k:k + 1, :]


def _positions_call(idx, rank, pad_start):
    n_tiles = idx.shape[0]
    row_spec = pl.BlockSpec((1, TOP_K, TR), lambda i: (i, 0, 0))
    return pl.pallas_call(
        _positions_kernel,
        grid=(n_tiles,),
        in_specs=[row_spec, row_spec, pl.BlockSpec((N_EXPERTS, 1), lambda i: (0, 0))],
        out_specs=row_spec,
        out_shape=jax.ShapeDtypeStruct((n_tiles, TOP_K, TR), jnp.int32),
    )(idx, rank, pad_start.reshape(N_EXPERTS, 1))


def _combine_dense_kernel(g_ref, w_ref, x1_ref, sh_ref, mod_ref, fg_ref, outc_ref, outl_ref, *, n_ctx_tiles):
    w = w_ref[...]
    acc_a = acc_b = None
    for k in range(TOP_K):
        ya, yb = _unpack_rows_native(g_ref[0, k])
        acc_a = w[:, k:k + 1] * ya if k == 0 else acc_a + w[:, k:k + 1] * ya
        acc_b = w[:, k:k + 1] * yb if k == 0 else acc_b + w[:, k:k + 1] * yb
    acc = jnp.concatenate([acc_a, acc_b], axis=1)
    x2 = x1_ref[...] + mod_ref[0, 5:6, :] * (acc + sh_ref[...])
    out = x2 * lax.rsqrt(jnp.mean(x2 * x2, axis=-1, keepdims=True) + EPS) * fg_ref[...]
    is_ctx = pl.program_id(0) < n_ctx_tiles

    @pl.when(is_ctx)
    def _():
        outc_ref[...] = out

    @pl.when(jnp.logical_not(is_ctx))
    def _():
        outl_ref[...] = out


def _combine_dense_call(gathered, wts, x1, shared, mod, final_g, n_ctx, lat_len):
    T, K = wts.shape
    DH = gathered.shape[-1]
    D = 2 * DH
    row = functools.partial(_mod_row, tokens_per_tile=TC, n_ctx=n_ctx, lat_len=lat_len)
    nct = n_ctx // TC
    return pl.pallas_call(
        functools.partial(_combine_dense_kernel, n_ctx_tiles=nct),
        grid=(T // TC,),
        in_specs=[pl.BlockSpec((1, K, TC, DH), lambda j: (j, 0, 0, 0)),
                  pl.BlockSpec((TC, K), lambda j: (j, 0)),
                  pl.BlockSpec((TC, D), lambda j: (j, 0)),
                  pl.BlockSpec((TC, D), lambda j: (j, 0)),
                  pl.BlockSpec((1, N_MOD, D), lambda j: (row(j), 0, 0)),
                  pl.BlockSpec((1, D), lambda j: (0, 0))],
        out_specs=[pl.BlockSpec((TC, D), lambda j: (jnp.minimum(j, nct - 1), 0)),
                   pl.BlockSpec((TC, D), lambda j: (jnp.maximum(j - nct, 0), 0))],
        out_shape=[jax.ShapeDtypeStruct((n_ctx, D), jnp.float32), jax.ShapeDtypeStruct((T - n_ctx, D), jnp.float32)],
        compiler_params=pltpu.CompilerParams(dimension_semantics=("arbitrary",)),
    )(gathered, wts, x1, shared, mod, final_g.reshape(1, D))


def _moe_routed(h2, h2p, router_w, router_bias, w_gate, w_up, w_down):
    T, D = h2.shape
    idx, rank, w_rows, cnt = _route_call(h2, router_w, router_bias)
    wts = w_rows.transpose(0, 2, 1).reshape(T, TOP_K)
    counts = cnt[:, 0]
    padded = (counts + BM - 1) // BM * BM
    pad_end = jnp.cumsum(padded)
    pad_start = (pad_end - padded).astype(jnp.int32)
    n_blk = -(-(T * TOP_K) // BM) + N_EXPERTS
    n_pad = n_blk * BM
    n_used = (pad_end[-1] // BM).astype(jnp.int32).reshape(1)
    pos = _positions_call(idx, rank, pad_start)
    x_sorted = _sc_scatter_call(h2p, pos.reshape(-1), n_pad)
    blk_row0 = jnp.arange(n_blk, dtype=jnp.int32) * BM
    blk_e = jnp.minimum(jnp.sum((pad_end[None, :] <= blk_row0[:, None]).astype(jnp.int32), axis=1), N_EXPERTS - 1)
    own = blk_e[:, None] == jnp.arange(N_EXPERTS, dtype=jnp.int32)[None, :]
    seg_end = jnp.sum(jnp.where(own, (pad_start + counts)[None, :], 0), axis=1)
    n_valid = jnp.clip(seg_end - blk_row0, 0, BM).astype(jnp.int32)
    y = _expert_call(x_sorted, blk_e, n_valid, n_used, w_gate, w_up, w_down)
    pos_t = pos.reshape(T // TR, TOP_K, TR // TC, TC).transpose(0, 2, 1, 3).reshape(T // TC, TOP_K * TC)
    return y, pos_t, wts


def kernel(x_prompt, x_sample, state_delta, c, c_ctx, w_ada, b_ada, norm1_g, w_in, conv_w, a_log,
           dt_bias, onorm_g, pool_w, pool_scale, w_out, norm2_g, router_w, router_bias, exp_w_gate,
           exp_w_up, exp_w_down, sh_w_gate, sh_w_up, sh_w_down, final_g):
    Bc, Lc, D = x_prompt.shape
    Bl, Ll, _ = x_sample.shape
    n_ctx = Bc * Lc
    assert DEPTH == 1 and 1 + Bl <= MOD_ROWS and n_ctx % Ll == 0
    x_parts = (x_prompt.reshape(n_ctx, D), x_sample.reshape(Bl * Ll, D))
    cvec = jnp.concatenate([c_ctx[None], c, jnp.zeros((MOD_ROWS - 1 - Bl, D), c.dtype)], axis=0)
    l = 0
    mod = _ada_call(cvec, w_ada[l], b_ada[l]).reshape(MOD_ROWS, N_MOD, D)
    qkv, z, ba, u = _inproj_call(*x_parts, mod, norm1_g[l], w_in[l], Ll)
    dn = (conv_w[l], a_log[l], dt_bias[l], onorm_g[l])
    oa_c, st_ctx = _delta_call(qkv, z, ba, *dn, None, Bc, Lc, 0)
    oa_l, _ = _delta_call(qkv, z, ba, *dn, state_delta[:, l], Bl, Ll, n_ctx // Ll)
    op_c = _pool_call(u, pool_w[l], pool_scale[l], False, Bc, Lc, 0)
    op_l = _pool_call(u, pool_w[l], pool_scale[l], True, Bl, Ll, n_ctx // Ll)
    x1, h2, h2p, shared = _outproj_call(x_parts, (oa_c, oa_l), (op_c, op_l), mod, norm2_g[l], w_out[l],
                                        sh_w_gate[l], sh_w_up[l], sh_w_down[l], Ll)
    y, pos_t, wts = _moe_routed(h2, h2p, router_w[l], router_bias[l], exp_w_gate[l], exp_w_up[l],
                                exp_w_down[l])
    T = n_ctx + Bl * Ll
    gathered = _sc_gather_call(y, pos_t.reshape(-1))
    out_c, out_l = _combine_dense_call(gathered.reshape(T // TC, TOP_K, TC, D // 2), wts, x1, shared, mod,
                                       final_g, n_ctx, Ll)
    y_prompt = out_c.reshape(Bc, Lc, D)
    y_sample = out_l.reshape(Bl, Ll, D)
    new_state_delta = st_ctx[:, None].astype(x_prompt.dtype)
    return (y_prompt, y_sample, new_state_delta)
```

```python
import functools
import jax, jax.numpy as jnp
from jax import lax
from jax.experimental import pallas as pl
from jax.experimental.pallas import tpu as pltpu
from jax.experimental.pallas import tpu_sc as plsc

D_MODEL = 1024
DEPTH = 1
GRID_W = 64
D_MIX = D_MODEL
D_A = D_MIX // 2
D_P = D_MIX - D_A
H_A = 4
DK = D_A // H_A
DV = D_A // H_A
CONV_K = 5
CHUNK = 64
POOL_WINDOWS = (2, 4, 8, 16)
N_PG = len(POOL_WINDOWS)
PG = D_P // N_PG
N_EXPERTS = 256
TOP_K = 8
N_GROUPS = 8
TOPK_GROUP = 4
ROUTED_SCALE = 2.5
EPS = 1e-6
VMEM_LIMIT = 48 * 1024 * 1024


def _split_bf16(a):
    hi = a.astype(jnp.bfloat16)
    return hi, (a - hi.astype(jnp.float32)).astype(jnp.bfloat16)


def _bdot(a, b):
    return jnp.dot(a, b, preferred_element_type=jnp.float32)


def _pack_rows(x):
    m = x.shape[1] // 2
    lo = lax.bitcast_convert_type(x[:, :m].astype(jnp.bfloat16).astype(jnp.float32), jnp.uint32)
    hi = lax.bitcast_convert_type(x[:, m:].astype(jnp.bfloat16).astype(jnp.float32), jnp.uint32)
    return lax.bitcast_convert_type(hi | (lo >> 16), jnp.int32)


def _pack_rows_native(x):
    m = x.shape[1] // 2
    return pltpu.pack_elementwise([x[:, :m], x[:, m:]], packed_dtype=jnp.bfloat16)


def _unpack_rows_native(p):
    return tuple(pltpu.unpack_elementwise(p, index=i, packed_dtype=jnp.bfloat16, unpacked_dtype=jnp.float32)
                 for i in range(2))


N_MOD = 6
MOD_ROWS = 8
TM = 512


def _ada_kernel(c_ref, w_ref, b_ref, o_ref):
    c = c_ref[...]
    s = c * jax.nn.sigmoid(c)
    sh, sl = _split_bf16(s)
    wh, wl = _split_bf16(w_ref[...])
    o_ref[...] = _bdot(sh, wh) + (_bdot(sh, wl) + _bdot(sl, wh)) + b_ref[...]


def _ada_call(cvec, w_ada, b_ada):
    R, D = cvec.shape
    N = w_ada.shape[1]
    tn = 1024
    return pl.pallas_call(
        _ada_kernel,
        grid=(N // tn,),
        in_specs=[pl.BlockSpec((R, D), lambda j: (0, 0)),
                  pl.BlockSpec((D, tn), lambda j: (0, j)),
                  pl.BlockSpec((1, tn), lambda j: (0, j))],
        out_specs=pl.BlockSpec((R, tn), lambda j: (0, j)),
        out_shape=jax.ShapeDtypeStruct((R, N), jnp.float32),
    )(cvec, w_ada, b_ada.reshape(1, N))


def _mod_row(tile, tokens_per_tile, n_ctx, lat_len):
    t0 = tile * tokens_per_tile
    return jnp.where(t0 < n_ctx, 0, 1 + (t0 - n_ctx) // lat_len)


def _two_part_specs(n_ctx_tiles, width):
    return (pl.BlockSpec((TM, width), lambda i: (jnp.minimum(i, n_ctx_tiles - 1), 0)),
            pl.BlockSpec((TM, width), lambda i: (jnp.maximum(i - n_ctx_tiles, 0), 0)))


def _pick(n_ctx_tiles, ctx_ref, lat_ref):
    return jnp.where(pl.program_id(0) < n_ctx_tiles, ctx_ref[...], lat_ref[...])


def _inproj_kernel(xc_ref, xl_ref, mod_ref, g_ref, wq_ref, wz_ref, wb_ref, wu_ref, q_ref, z_ref, b_ref, u_ref,
                   *, n_ctx_tiles):
    x = _pick(n_ctx_tiles, xc_ref, xl_ref)
    y = x * lax.rsqrt(jnp.mean(x * x, axis=-1, keepdims=True) + EPS) * g_ref[...]
    h = (y * (1.0 + mod_ref[0, 1:2, :]) + mod_ref[0, 0:1, :]).astype(jnp.bfloat16)
    q_ref[...] = _bdot(h, wq_ref[...])
    z_ref[...] = _bdot(h, wz_ref[...])
    b_ref[...] = _bdot(h, wb_ref[...])
    u_ref[...] = _bdot(h, wu_ref[...])


def _inproj_call(x_ctx, x_lat, mod, norm1_g, w_in, lat_len):
    n_ctx, D = x_ctx.shape
    T = n_ctx + x_lat.shape[0]
    bf = jnp.bfloat16
    nq, nz, nb = 3 * D_A, D_A, 4 * H_A
    wq = w_in[:, :nq].astype(bf)
    wz = w_in[:, nq:nq + nz].astype(bf)
    wb = jnp.pad(w_in[:, nq + nz:nq + nz + nb], ((0, 0), (0, 128 - nb))).astype(bf)
    wu = w_in[:, nq + nz + nb:].astype(bf)
    row = functools.partial(_mod_row, tokens_per_tile=TM, n_ctx=n_ctx, lat_len=lat_len)

    def full(a):
        return pl.BlockSpec(a.shape, lambda i: (0, 0))

    def rows(n):
        return pl.BlockSpec((TM, n), lambda i: (i, 0))

    return pl.pallas_call(
        functools.partial(_inproj_kernel, n_ctx_tiles=n_ctx // TM),
        grid=(T // TM,),
        in_specs=[*_two_part_specs(n_ctx // TM, D), pl.BlockSpec((1, N_MOD, D), lambda i: (row(i), 0, 0)),
                  pl.BlockSpec((1, D), lambda i: (0, 0)), full(wq), full(wz), full(wb), full(wu)],
        out_specs=[rows(nq), rows(nz), rows(128), rows(D_P)],
        out_shape=[jax.ShapeDtypeStruct((T, nq), jnp.float32), jax.ShapeDtypeStruct((T, nz), jnp.float32),
                   jax.ShapeDtypeStruct((T, 128), jnp.float32), jax.ShapeDtypeStruct((T, D_P), jnp.float32)],
        compiler_params=pltpu.CompilerParams(dimension_semantics=("arbitrary",),
                                             vmem_limit_bytes=VMEM_LIMIT),
    )(x_ctx, x_lat, mod, norm1_g.reshape(1, D), wq, wz, wb, wu)


PT = 256


def _window_bounds(pos, w, n):
    return jnp.maximum(pos - w // 2, 0), jnp.minimum(pos + w - w // 2, n)


def _band_sum(band, x):
    xh, xl = _split_bf16(x)
    return _bdot(band, xh) + _bdot(band, xl)


def _pool_seq_kernel(u_ref, pw_ref, ps_ref, o_ref):
    L = u_ref.shape[0]
    ti = lax.broadcasted_iota(jnp.int32, (L, L), 0)
    ji = lax.broadcasted_iota(jnp.int32, (L, L), 1)
    tcol = lax.broadcasted_iota(jnp.int32, (L, 1), 0)
    for i, w in enumerate(POOL_WINDOWS):
        lo, hi = _window_bounds(ti, w, L)
        band = ((ji >= lo) & (ji < hi)).astype(jnp.bfloat16)
        clo, chi = _window_bounds(tcol, w, L)
        ug = u_ref[:, i * PG:(i + 1) * PG]
        mean = _band_sum(band, ug) / (chi - clo).astype(jnp.float32)
        d = (mean - ug).astype(jnp.bfloat16)
        o_ref[:, i * PG:(i + 1) * PG] = _bdot(d, pw_ref[i]) * ps_ref[:, i * PG:(i + 1) * PG]


def _pool_grid_kernel(u_ref, pw_ref, ps_ref, o_ref, pad_s, r_s):
    L = u_ref.shape[0]
    rows = L // GRID_W
    halo = (max(POOL_WINDOWS) // 2) * GRID_W
    pad_s[0:halo, :] = jnp.zeros((halo, D_P), jnp.float32)
    pad_s[halo + L:, :] = jnp.zeros((halo, D_P), jnp.float32)
    pad_s[halo:halo + L, :] = u_ref[...]
    ti = lax.broadcasted_iota(jnp.int32, (PT, PT), 0)
    ji = lax.broadcasted_iota(jnp.int32, (PT, PT), 1)
    tcol = lax.broadcasted_iota(jnp.int32, (PT, 1), 0)
    for i, w in enumerate(POOL_WINDOWS):
        cs = slice(i * PG, (i + 1) * PG)
        acc = None
        for dr in range(-(w // 2), w - w // 2):
            part = pad_s[halo + dr * GRID_W:halo + dr * GRID_W + L, cs]
            acc = part if acc is None else acc + part
        r_s[...] = acc
        lo, hi = _window_bounds(ti % GRID_W, w, GRID_W)
        band = ((ji // GRID_W == ti // GRID_W) & (ji % GRID_W >= lo) & (ji % GRID_W < hi)).astype(jnp.bfloat16)
        clo, chi = _window_bounds(tcol % GRID_W, w, GRID_W)
        ccnt = (chi - clo).astype(jnp.float32)
        for tile in range(L // PT):
            ts = slice(tile * PT, (tile + 1) * PT)
            rlo, rhi = _window_bounds(tile * (PT // GRID_W) + tcol // GRID_W, w, rows)
            mean = _band_sum(band, r_s[ts, :]) / ((rhi - rlo).astype(jnp.float32) * ccnt)
            d = (mean - u_ref[ts, cs]).astype(jnp.bfloat16)
            o_ref[ts, cs] = _bdot(d, pw_ref[i]) * ps_ref[:, cs]


def _pool_call(u, pool_w, pool_scale, grid, B, L, row_blk0):
    pw = pool_w.astype(jnp.bfloat16)
    ps = pool_scale.reshape(1, D_P)
    specs = dict(
        grid=(B,),
        in_specs=[pl.BlockSpec((L, D_P), lambda b: (row_blk0 + b, 0)),
                  pl.BlockSpec((N_PG, PG, PG), lambda b: (0, 0, 0)),
                  pl.BlockSpec((1, D_P), lambda b: (0, 0))],
        out_specs=pl.BlockSpec((L, D_P), lambda b: (b, 0)),
        out_shape=jax.ShapeDtypeStruct((B * L, D_P), jnp.float32),
        compiler_params=pltpu.CompilerParams(dimension_semantics=("arbitrary",),
                                             vmem_limit_bytes=VMEM_LIMIT))
    if not grid:
        return pl.pallas_call(_pool_seq_kernel, **specs)(u, pw, ps)
    halo = (max(POOL_WINDOWS) // 2) * GRID_W
    return pl.pallas_call(
        _pool_grid_kernel,
        scratch_shapes=[pltpu.VMEM((L + 2 * halo, D_P), jnp.float32), pltpu.VMEM((L, PG), jnp.float32)],
        **specs)(u, pw, ps)


def _outproj_kernel(xc_ref, xl_ref, oac_ref, oal_ref, opc_ref, opl_ref, mod_ref, g2_ref, wo_ref, sg_ref, su_ref,
                    sd_ref, x1_ref, h2_ref, h2p_ref, sh_ref, *, n_ctx_tiles):
    o_a = _pick(n_ctx_tiles, oac_ref, oal_ref)
    o_p = _pick(n_ctx_tiles, opc_ref, opl_ref)
    mix = (_bdot(o_a.astype(jnp.bfloat16), wo_ref[:D_A, :])
           + _bdot(o_p.astype(jnp.bfloat16), wo_ref[D_A:, :]))
    x1 = _pick(n_ctx_tiles, xc_ref, xl_ref) + mod_ref[0, 2:3, :] * mix
    x1_ref[...] = x1
    y = x1 * lax.rsqrt(jnp.mean(x1 * x1, axis=-1, keepdims=True) + EPS) * g2_ref[...]
    h2 = y * (1.0 + mod_ref[0, 4:5, :]) + mod_ref[0, 3:4, :]
    h2_ref[...] = h2
    h2p_ref[...] = _pack_rows(h2)
    hb = h2.astype(jnp.bfloat16)
    g = _bdot(hb, sg_ref[...])
    a = (g * jax.nn.sigmoid(g)) * _bdot(hb, su_ref[...])
    sh_ref[...] = _bdot(a.astype(jnp.bfloat16), sd_ref[...])


def _outproj_call(x_parts, oa_parts, op_parts, mod, norm2_g, w_out, sh_gate, sh_up, sh_down, lat_len):
    n_ctx, D = x_parts[0].shape
    T = n_ctx + x_parts[1].shape[0]
    nct = n_ctx // TM
    bf = jnp.bfloat16
    row = functools.partial(_mod_row, tokens_per_tile=TM, n_ctx=n_ctx, lat_len=lat_len)
    ws = [w_out.astype(bf), sh_gate.astype(bf), sh_up.astype(bf), sh_down.astype(bf)]

    def rows(n):
        return pl.BlockSpec((TM, n), lambda i: (i, 0))

    return pl.pallas_call(
        functools.partial(_outproj_kernel, n_ctx_tiles=nct),
        grid=(T // TM,),
        in_specs=[*_two_part_specs(nct, D), *_two_part_specs(nct, D_A), *_two_part_specs(nct, D_P),
                  pl.BlockSpec((1, N_MOD, D), lambda i: (row(i), 0, 0)),
                  pl.BlockSpec((1, D), lambda i: (0, 0))] + [pl.BlockSpec(w.shape, lambda i: (0, 0)) for w in ws],
        out_specs=[rows(D), rows(D), rows(D // 2), rows(D)],
        out_shape=[jax.ShapeDtypeStruct((T, D), jnp.float32), jax.ShapeDtypeStruct((T, D), jnp.float32),
                   jax.ShapeDtypeStruct((T, D // 2), jnp.int32), jax.ShapeDtypeStruct((T, D), jnp.float32)],
        compiler_params=pltpu.CompilerParams(dimension_semantics=("arbitrary",),
                                             vmem_limit_bytes=VMEM_LIMIT),
    )(*x_parts, *oa_parts, *op_parts, mod, norm2_g.reshape(1, D), *ws)


SC = 256
CPS = SC // CHUNK
BASE = 16
DELTA_HEAD_ROWS = 4096


def _mm(a, b):
    return jnp.dot(a.astype(jnp.bfloat16), b.astype(jnp.bfloat16), preferred_element_type=jnp.float32)


def _mm_nt(a, b):
    return lax.dot_general(a.astype(jnp.bfloat16), b.astype(jnp.bfloat16), (((1,), (1,)), ((), ())),
                           preferred_element_type=jnp.float32)


def _softplus(x):
    return jnp.maximum(x, 0.0) + jnp.log(1.0 + jnp.exp(-jnp.abs(x)))


def _delta_kernel(sc_ref, xq_ref, xk_ref, xv_ref, z_ref, bac_ref, bar_ref, cwq_ref, cwk_ref, cwv_ref,
                  og_ref, s0_ref, o_ref, st_ref, q_s, k_s, v_s, o_s, vn_s, *, n_sc, zero_init, hpb):
    hb = pl.program_id(1)
    L = q_s.shape[1]

    def conv(x_ref, w_ref, cs):
        x = x_ref[:, cs]
        row = lax.broadcasted_iota(jnp.int32, x.shape, 0)
        acc = x * w_ref[CONV_K // 2:CONV_K // 2 + 1, cs]
        for j in range(CONV_K):
            d = j - CONV_K // 2
            if d == 0:
                continue
            xs = pltpu.roll(x, (-d) % L, 0)
            ok = (row + d >= 0) & (row + d < L)
            acc = acc + jnp.where(ok, xs, 0.0) * w_ref[j:j + 1, cs]
        return acc * jax.nn.sigmoid(acc)

    for hh in range(hpb):
        cs = slice(hh * DK, (hh + 1) * DK)
        q = conv(xq_ref, cwq_ref, cs)
        q_s[hh] = q * lax.rsqrt(jnp.sum(q * q, axis=-1, keepdims=True) + EPS) * (DK ** -0.5)
        k = conv(xk_ref, cwk_ref, cs)
        k_s[hh] = k * lax.rsqrt(jnp.sum(k * k, axis=-1, keepdims=True) + EPS)
        v_s[hh] = conv(xv_ref, cwv_ref, cs)
    o_s[...] = jnp.zeros_like(o_s)

    ri = lax.broadcasted_iota(jnp.int32, (SC, SC), 0)
    ci = lax.broadcasted_iota(jnp.int32, (SC, SC), 1)
    same = (ri // CHUNK) == (ci // CHUNK)
    same_base = (ri // BASE) == (ci // BASE)
    merge_masks = [(ri // w) == (ci // w) for w in (2 * BASE, CHUNK)]
    eye = (ri == ci).astype(jnp.float32)
    rowi = lax.broadcasted_iota(jnp.int32, (SC, DV), 0)

    def prep(m, d, hh):
        r0 = pl.multiple_of(m * SC, SC)
        h = hb * hpb + hh
        q = q_s[hh, pl.ds(r0, SC), :]
        k = k_s[hh, pl.ds(r0, SC), :]
        v = v_s[hh, pl.ds(r0, SC), :]
        bc = bac_ref[0, hh, pl.ds(r0, SC), :]
        br = bar_ref[0, hh, m]
        a_l = sc_ref[d * H_A + h]
        dtb = sc_ref[2 * H_A + d * H_A + h]
        neg_ea = -jnp.exp(jnp.full((1, 1), a_l, jnp.float32))
        beta = jax.nn.sigmoid(bc[:, d:d + 1])
        g_col = neg_ea * _softplus(bc[:, 2 + d:3 + d] + dtb)
        g_row = neg_ea * _softplus(br[2 + d:3 + d, :] + dtb)
        if d == 0:
            tri, strict = same & (ci <= ri), same & (ci < ri)
        else:
            tri, strict = same & (ci >= ri), same & (ci > ri)
        tri_t = same & (ri <= ci) if d == 0 else same & (ri >= ci)
        gc_col = jnp.sum(jnp.where(tri, g_row, 0.0), axis=1, keepdims=True)
        gc_row = jnp.sum(jnp.where(tri_t, g_col, 0.0), axis=0, keepdims=True)
        gl_col = jnp.sum(jnp.where(same, g_row, 0.0), axis=1, keepdims=True)
        decay = jnp.where(tri, jnp.exp(jnp.where(tri, gc_col - gc_row, 0.0)), 0.0)
        kb = k * beta
        a = jnp.where(strict, _mm_nt(kb, k) * decay, 0.0)
        attn = jnp.where(tri, _mm_nt(q, k) * decay, 0.0)
        eg = jnp.exp(gc_col)
        x = jnp.concatenate([v * beta, kb * eg], axis=1)
        qd = q * eg
        kdt = (k * jnp.exp(gl_col - gc_col)).T
        return dict(r0=r0, a=a, attn=attn, x=x, qd=qd, kdt=kdt, egl=jnp.exp(gl_col))

    def run_chains(ms, states):
        n = len(chains)
        ops = [prep(ms[i], d, hh) for i, (hh, d) in enumerate(chains)]
        ps = [jnp.where(same_base, o["a"], 0.0) for o in ops]
        ts = [eye - p for p in ps]
        for _ in range(BASE.bit_length() - 2):
            ps = [_mm(p, p) for p in ps]
            ts = [t + _mm(t, p) for t, p in zip(ts, ps)]
        inner = same_base
        for outer in merge_masks:
            lows = [_mm(jnp.where(outer & ~inner, o["a"], 0.0), t) for o, t in zip(ops, ts)]
            ts = [t - _mm(t, low) for t, low in zip(ts, lows)]
            inner = outer
        xs = [_mm(t, o["x"]) for t, o in zip(ts, ops)]
        for i in range(n):
            vn_s[i] = jnp.zeros((SC, DV), jnp.float32)
        states = list(states)
        for step in range(CPS):
            cs = [step if d == 0 else CPS - 1 - step for _, d in chains]
            los = [c * CHUNK for c in cs]
            ws_qs = [_mm(jnp.concatenate([x[lo:lo + CHUNK, DV:], o["qd"][lo:lo + CHUNK]], axis=0), s)
                     for x, o, lo, s in zip(xs, ops, los, states)]
            for i in range(n):
                vn_s[i, los[i]:los[i] + CHUNK, :] = xs[i][los[i]:los[i] + CHUNK, :DV] - ws_qs[i][:CHUNK]
            vns = [vn_s[i] for i in range(n)]
            o_cs = [wq[CHUNK:] + _mm(o["attn"][lo:lo + CHUNK, :], vn)
                    for wq, o, lo, vn in zip(ws_qs, ops, los, vns)]
            for i, (hh, _) in enumerate(chains):
                o_s[hh, pl.ds(ops[i]["r0"] + los[i], CHUNK), :] += o_cs[i]
            states = [s * o["egl"][lo:lo + 1, :]
                      + _mm(o["kdt"], jnp.where((rowi >= lo) & (rowi < lo + CHUNK), vn, 0.0))
                      for s, o, lo, vn in zip(states, ops, los, vns)]
        return tuple(states)

    if zero_init:
        states = tuple(jnp.zeros((DK, DV), jnp.float32) for _ in range(2 * hpb))
    else:
        states = tuple(s0_ref[0, d, hh] for hh in range(hpb) for d in range(2))

    chains = [(hh, d) for hh in range(hpb) for d in range(2)]

    def body(m, carry):
        return run_chains([m if d == 0 else n_sc - 1 - m for _, d in chains], carry)

    if n_sc == 1:
        states = body(0, states)
    else:
        states = lax.fori_loop(0, n_sc, body, states)

    for hh in range(hpb):
        for d in range(2):
            st_ref[0, d, hh] = states[2 * hh + d]
        o = o_s[hh]
        o = o * lax.rsqrt(jnp.mean(o * o, axis=-1, keepdims=True) + EPS) * og_ref[...]
        zz = z_ref[:, hh * DV:(hh + 1) * DV]
        o_ref[:, hh * DV:(hh + 1) * DV] = o * (zz * jax.nn.sigmoid(zz))


def _delta_call(qkv, z, ba, conv_w, a_log, dt_bias, onorm_g, s0, B, L, row_blk0):
    n_sc = L // SC
    t0 = row_blk0 * L
    bah = ba[t0:t0 + B * L, :4 * H_A].reshape(B, L, 4, H_A).transpose(0, 3, 1, 2)
    bar = bah.reshape(B, H_A, n_sc, SC, 4).transpose(0, 1, 2, 4, 3)
    scal = jnp.concatenate([a_log.reshape(-1), dt_bias.reshape(-1)]).astype(jnp.float32)
    hpb = max(1, min(H_A, DELTA_HEAD_ROWS // L))
    n_hb = H_A // hpb
    zero_init = s0 is None
    if zero_init:
        s0 = jnp.zeros((1, 2, hpb, DK, DV), jnp.float32)
        s0_spec = pl.BlockSpec((1, 2, hpb, DK, DV), lambda b, h, sc: (0, 0, 0, 0, 0))
    else:
        s0_spec = pl.BlockSpec((1, 2, hpb, DK, DV), lambda b, h, sc: (b, 0, h, 0, 0))

    def col(off):
        return pl.BlockSpec((L, hpb * DK), lambda b, h, sc: (row_blk0 + b, off * n_hb + h))

    def cw(off):
        return pl.BlockSpec((CONV_K, hpb * DK), lambda b, h, sc: (0, off * n_hb + h))

    kern = functools.partial(_delta_kernel, n_sc=n_sc, zero_init=zero_init, hpb=hpb)
    return pl.pallas_call(
        kern,
        grid_spec=pltpu.PrefetchScalarGridSpec(
            num_scalar_prefetch=1,
            grid=(B, n_hb),
            in_specs=[col(0), col(1), col(2),
                      pl.BlockSpec((L, hpb * DV), lambda b, h, sc: (row_blk0 + b, h)),
                      pl.BlockSpec((1, hpb, L, 4), lambda b, h, sc: (b, h, 0, 0)),
                      pl.BlockSpec((1, hpb, n_sc, 4, SC), lambda b, h, sc: (b, h, 0, 0, 0)),
                      cw(0), cw(1), cw(2),
                      pl.BlockSpec((1, DV), lambda b, h, sc: (0, 0)),
                      s0_spec],
            out_specs=[pl.BlockSpec((L, hpb * DV), lambda b, h, sc: (b, h)),
                       pl.BlockSpec((1, 2, hpb, DK, DV), lambda b, h, sc: (b, 0, h, 0, 0))],
            scratch_shapes=[pltpu.VMEM((hpb, L, DK), jnp.float32), pltpu.VMEM((hpb, L, DK), jnp.float32),
                            pltpu.VMEM((hpb, L, DV), jnp.float32), pltpu.VMEM((hpb, L, DV), jnp.float32),
                            pltpu.VMEM((2 * hpb, SC, DV), jnp.float32)]),
        out_shape=[jax.ShapeDtypeStruct((B * L, D_A), jnp.float32),
                   jax.ShapeDtypeStruct((B, 2, H_A, DK, DV), jnp.float32)],
        compiler_params=pltpu.CompilerParams(dimension_semantics=("arbitrary", "arbitrary"),
                                             vmem_limit_bytes=VMEM_LIMIT),
    )(scal, qkv, qkv, qkv, z, bah, bar, conv_w, conv_w, conv_w, onorm_g.reshape(1, DV), s0)


TR = 256
GSZ = N_EXPERTS // N_GROUPS
NEG = -jnp.inf
BM = 640


def _route_kernel(h_ref, rwh_ref, rwl_ref, rb_ref, idx_ref, rank_ref, w_ref, cnt_ref, cnt_s):
    i = pl.program_id(0)

    @pl.when(i == 0)
    def _():
        cnt_s[...] = jnp.zeros_like(cnt_s)

    h = h_ref[...]
    hh, hl = _split_bf16(h)
    logits = _bdot(hh, rwh_ref[...]) + (_bdot(hh, rwl_ref[...]) + _bdot(hl, rwh_ref[...]))
    scores = jax.nn.sigmoid(logits.T)
    sel = scores + rb_ref[...]
    erow = lax.broadcasted_iota(jnp.int32, sel.shape, 0)
    grow = lax.broadcasted_iota(jnp.int32, (GSZ, TR), 0)

    def first_argmax(v, rows):
        m = jnp.max(v, axis=0, keepdims=True)
        first = jnp.min(jnp.where(v == m, rows, N_EXPERTS), axis=0, keepdims=True)
        return m, first

    gs = []
    for g in range(N_GROUPS):
        vg = sel[g * GSZ:(g + 1) * GSZ, :]
        m1, i1 = first_argmax(vg, grow)
        m2 = jnp.max(jnp.where(grow == i1, NEG, vg), axis=0, keepdims=True)
        gs.append(m1 + m2)
    cand = []
    for g in range(N_GROUPS):
        beat = jnp.zeros(gs[g].shape, jnp.int32)
        for o in range(N_GROUPS):
            if o == g:
                continue
            wins = (gs[o] > gs[g]) | ((gs[o] == gs[g]) & (o < g))
            beat = beat + wins.astype(jnp.int32)
        cand.append(jnp.where(beat < TOPK_GROUP, sel[g * GSZ:(g + 1) * GSZ, :], NEG))
    cand = jnp.concatenate(cand, axis=0)
    chosen = []
    picked = jnp.zeros(sel.shape, jnp.bool_)
    for _ in range(TOP_K):
        _, ik = first_argmax(cand, erow)
        hit = erow == ik
        chosen.append((ik, hit))
        picked = picked | hit
        cand = jnp.where(hit, NEG, cand)
    wsum = jnp.sum(jnp.where(picked, scores, 0.0), axis=0, keepdims=True)

    ri = lax.broadcasted_iota(jnp.int32, (TR, TR), 0)
    ci = lax.broadcasted_iota(jnp.int32, (TR, TR), 1)
    earlier = (ri < ci).astype(jnp.bfloat16)
    rank_mat = _bdot(picked.astype(jnp.bfloat16), earlier) + cnt_s[...]
    cnt_s[...] = cnt_s[...] + jnp.sum(picked.astype(jnp.float32), axis=1, keepdims=True)
    cnt_ref[...] = cnt_s[...].astype(jnp.int32)

    for k, (ik, hit) in enumerate(chosen):
        idx_ref[0, k:k + 1, :] = ik
        rank_ref[0, k:k + 1, :] = jnp.sum(jnp.where(hit, rank_mat, 0.0), axis=0, keepdims=True).astype(jnp.int32)
        w_ref[0, k:k + 1, :] = jnp.sum(jnp.where(hit, scores, 0.0), axis=0, keepdims=True) / wsum * ROUTED_SCALE


def _route_call(hf, router_w, router_bias):
    T, D = hf.shape
    n_tiles = T // TR
    rwh, rwl = _split_bf16(router_w)
    row_spec = pl.BlockSpec((1, TOP_K, TR), lambda i: (i, 0, 0))
    return pl.pallas_call(
        _route_kernel,
        grid=(n_tiles,),
        in_specs=[pl.BlockSpec((TR, D), lambda i: (i, 0)),
                  pl.BlockSpec((D, N_EXPERTS), lambda i: (0, 0)),
                  pl.BlockSpec((D, N_EXPERTS), lambda i: (0, 0)),
                  pl.BlockSpec((N_EXPERTS, 1), lambda i: (0, 0))],
        out_specs=[row_spec, row_spec, row_spec, pl.BlockSpec((N_EXPERTS, 1), lambda i: (0, 0))],
        scratch_shapes=[pltpu.VMEM((N_EXPERTS, 1), jnp.float32)],
        out_shape=[jax.ShapeDtypeStruct((n_tiles, TOP_K, TR), jnp.int32),
                   jax.ShapeDtypeStruct((n_tiles, TOP_K, TR), jnp.int32),
                   jax.ShapeDtypeStruct((n_tiles, TOP_K, TR), jnp.float32),
                   jax.ShapeDtypeStruct((N_EXPERTS, 1), jnp.int32)],
        compiler_params=pltpu.CompilerParams(dimension_semantics=("arbitrary",)),
    )(hf, rwh, rwl, router_bias.reshape(N_EXPERTS, 1).astype(jnp.float32))


def _expert_kernel(blk_e_ref, nvalid_ref, nused_ref, x_ref, wg_ref, wu_ref, wd_ref, y_ref, wg_s, wu_s, wd_s):
    i = pl.program_id(0)

    @pl.when(i < nused_ref[0])
    def _():
        e = blk_e_ref[i]
        prev = blk_e_ref[jnp.maximum(i - 1, 0)]

        @pl.when((i == 0) | (e != prev))
        def _():
            wg_s[...] = wg_ref[0].astype(jnp.bfloat16)
            wu_s[...] = wu_ref[0].astype(jnp.bfloat16)
            wd_s[...] = wd_ref[0].astype(jnp.bfloat16)

        row = lax.broadcasted_iota(jnp.int32, (BM, 1), 0)
        xa, xb = _unpack_rows_native(jnp.where(row < nvalid_ref[i], x_ref[...], 0))
        xa = xa.astype(jnp.bfloat16)
        xb = xb.astype(jnp.bfloat16)
        half = xa.shape[1]
        g = _bdot(xa, wg_s[:half, :]) + _bdot(xb, wg_s[half:, :])
        u = _bdot(xa, wu_s[:half, :]) + _bdot(xb, wu_s[half:, :])
        a = (g * jax.nn.sigmoid(g)) * u
        y_ref[...] = _pack_rows_native(_bdot(a.astype(jnp.bfloat16), wd_s[...]))


def _expert_call(x_sorted, blk_e, n_valid, n_used, w_gate, w_up, w_down):
    n_pad, DH = x_sorted.shape
    n_blk = n_pad // BM
    E, D, F = w_gate.shape

    def row_map(i, be, nv, nu):
        return (jnp.minimum(i, nu[0] - 1), 0)

    def w_map(i, be, nv, nu):
        return (be[jnp.minimum(i, nu[0] - 1)], 0, 0)

    return pl.pallas_call(
        _expert_kernel,
        grid_spec=pltpu.PrefetchScalarGridSpec(
            num_scalar_prefetch=3,
            grid=(n_blk,),
            in_specs=[pl.BlockSpec((BM, DH), row_map),
                      pl.BlockSpec((1, D, F), w_map),
                      pl.BlockSpec((1, D, F), w_map),
                      pl.BlockSpec((1, F, D), w_map)],
            out_specs=pl.BlockSpec((BM, DH), row_map),
            scratch_shapes=[pltpu.VMEM((D, F), jnp.bfloat16), pltpu.VMEM((D, F), jnp.bfloat16),
                            pltpu.VMEM((F, D), jnp.bfloat16)]),
        out_shape=jax.ShapeDtypeStruct((n_pad, DH), jnp.int32),
        compiler_params=pltpu.CompilerParams(dimension_semantics=("arbitrary",),
                                             vmem_limit_bytes=VMEM_LIMIT),
    )(blk_e, n_valid, n_used, x_sorted, w_gate, w_up, w_down)


TC = 128


SC_CORES = 2
SC_SUBCORES = 16
SC_CHUNK = 128


def _sc_gather_call(table, idx):
    n_idx = idx.shape[0]
    width = table.shape[1]
    n_workers = SC_CORES * SC_SUBCORES
    per_worker = n_idx // n_workers
    assert per_worker * n_workers == n_idx and per_worker % SC_CHUNK == 0
    mesh = plsc.VectorSubcoreMesh(core_axis_name="c", subcore_axis_name="s")

    def body(table_hbm, idx_hbm, out_hbm, idx_v, rows_v, sem):
        wid = lax.axis_index("s") * SC_CORES + lax.axis_index("c")
        base = wid * per_worker

        @pl.loop(0, per_worker // SC_CHUNK)
        def _(ch):
            off = base + ch * SC_CHUNK
            pltpu.sync_copy(idx_hbm.at[pl.ds(off, SC_CHUNK)], idx_v)
            pltpu.async_copy(table_hbm.at[idx_v], rows_v, sem).wait()
            pltpu.sync_copy(rows_v, out_hbm.at[pl.ds(off, SC_CHUNK)])

    return pl.kernel(
        body, out_type=jax.ShapeDtypeStruct((n_idx, width), table.dtype), mesh=mesh,
        scratch_types=[pltpu.VMEM((SC_CHUNK,), jnp.int32), pltpu.VMEM((SC_CHUNK, width), table.dtype),
                       pltpu.SemaphoreType.DMA],
    )(table, idx)


def _sc_scatter_call(src, pos, n_out):
    n_idx = pos.shape[0]
    width = src.shape[1]
    n_workers = SC_CORES * SC_SUBCORES
    per_worker = n_idx // n_workers
    assert per_worker * n_workers == n_idx and per_worker % SC_CHUNK == 0 and TR % SC_CHUNK == 0
    mesh = plsc.VectorSubcoreMesh(core_axis_name="c", subcore_axis_name="s")
    tile_pairs = TOP_K * TR

    def body(src_hbm, pos_hbm, out_hbm, idx_v, rows_v, sem):
        wid = lax.axis_index("s") * SC_CORES + lax.axis_index("c")
        base = wid * per_worker

        @pl.loop(0, per_worker // SC_CHUNK)
        def _(ch):
            off = base + ch * SC_CHUNK
            row0 = (off // tile_pairs) * TR + off % TR
            pltpu.sync_copy(pos_hbm.at[pl.ds(off, SC_CHUNK)], idx_v)
            pltpu.sync_copy(src_hbm.at[pl.ds(row0, SC_CHUNK)], rows_v)
            pltpu.async_copy(rows_v, out_hbm.at[idx_v], sem).wait()

    return pl.kernel(
        body, out_type=jax.ShapeDtypeStruct((n_out, width), src.dtype), mesh=mesh,
        scratch_types=[pltpu.VMEM((SC_CHUNK,), jnp.int32), pltpu.VMEM((SC_CHUNK, width), src.dtype),
                       pltpu.SemaphoreType.DMA],
    )(src, pos)


def _positions_kernel(idx_ref, rank_ref, pstart_ref, pos_ref):
    erow = lax.broadcasted_iota(jnp.int32, (N_EXPERTS, TR), 0)
    pstart = pstart_ref[...]
    for k in range(TOP_K):
        hit = erow == idx_ref[0, k:k + 1, :]
        seg = jnp.sum(jnp.where(hit, pstart, 0), axis=0, keepdims=True)
        pos_ref[0, k:k + 1, :] = seg + rank_ref[0, k:k + 1, :]


def _positions_call(idx, rank, pad_start):
    n_tiles = idx.shape[0]
    row_spec = pl.BlockSpec((1, TOP_K, TR), lambda i: (i, 0, 0))
    return pl.pallas_call(
        _positions_kernel,
        grid=(n_tiles,),
        in_specs=[row_spec, row_spec, pl.BlockSpec((N_EXPERTS, 1), lambda i: (0, 0))],
        out_specs=row_spec,
        out_shape=jax.ShapeDtypeStruct((n_tiles, TOP_K, TR), jnp.int32),
    )(idx, rank, pad_start.reshape(N_EXPERTS, 1))


def _combine_dense_kernel(g_ref, w_ref, x1_ref, sh_ref, mod_ref, fg_ref, outc_ref, outl_ref, *, n_ctx_tiles):
    w = w_ref[...]
    acc_a = acc_b = None
    for k in range(TOP_K):
        ya, yb = _unpack_rows_native(g_ref[0, k])
        acc_a = w[:, k:k + 1] * ya if k == 0 else acc_a + w[:, k:k + 1] * ya
        acc_b = w[:, k:k + 1] * yb if k == 0 else acc_b + w[:, k:k + 1] * yb
    acc = jnp.concatenate([acc_a, acc_b], axis=1)
    x2 = x1_ref[...] + mod_ref[0, 5:6, :] * (acc + sh_ref[...])
    out = x2 * lax.rsqrt(jnp.mean(x2 * x2, axis=-1, keepdims=True) + EPS) * fg_ref[...]
    is_ctx = pl.program_id(0) < n_ctx_tiles

    @pl.when(is_ctx)
    def _():
        outc_ref[...] = out

    @pl.when(jnp.logical_not(is_ctx))
    def _():
        outl_ref[...] = out


def _combine_dense_call(gathered, wts, x1, shared, mod, final_g, n_ctx, lat_len):
    T, K = wts.shape
    DH = gathered.shape[-1]
    D = 2 * DH
    row = functools.partial(_mod_row, tokens_per_tile=TC, n_ctx=n_ctx, lat_len=lat_len)
    nct = n_ctx // TC
    return pl.pallas_call(
        functools.partial(_combine_dense_kernel, n_ctx_tiles=nct),
        grid=(T // TC,),
        in_specs=[pl.BlockSpec((1, K, TC, DH), lambda j: (j, 0, 0, 0)),
                  pl.BlockSpec((TC, K), lambda j: (j, 0)),
                  pl.BlockSpec((TC, D), lambda j: (j, 0)),
                  pl.BlockSpec((TC, D), lambda j: (j, 0)),
                  pl.BlockSpec((1, N_MOD, D), lambda j: (row(j), 0, 0)),
                  pl.BlockSpec((1, D), lambda j: (0, 0))],
        out_specs=[pl.BlockSpec((TC, D), lambda j: (jnp.minimum(j, nct - 1), 0)),
                   pl.BlockSpec((TC, D), lambda j: (jnp.maximum(j - nct, 0), 0))],
        out_shape=[jax.ShapeDtypeStruct((n_ctx, D), jnp.float32), jax.ShapeDtypeStruct((T - n_ctx, D), jnp.float32)],
        compiler_params=pltpu.CompilerParams(dimension_semantics=("arbitrary",)),
    )(gathered, wts, x1, shared, mod, final_g.reshape(1, D))


def _moe_routed(h2, h2p, router_w, router_bias, w_gate, w_up, w_down):
    T, D = h2.shape
    idx, rank, w_rows, cnt = _route_call(h2, router_w, router_bias)
    wts = w_rows.transpose(0, 2, 1).reshape(T, TOP_K)
    counts = cnt[:, 0]
    padded = (counts + BM - 1) // BM * BM
    pad_end = jnp.cumsum(padded)
    pad_start = (pad_end - padded).astype(jnp.int32)
    n_blk = -(-(T * TOP_K) // BM) + N_EXPERTS
    n_pad = n_blk * BM
    n_used = (pad_end[-1] // BM).astype(jnp.int32).reshape(1)
    pos = _positions_call(idx, rank, pad_start)
    x_sorted = _sc_scatter_call(h2p, pos.reshape(-1), n_pad)
    blk_row0 = jnp.arange(n_blk, dtype=jnp.int32) * BM
    blk_e = jnp.minimum(jnp.sum((pad_end[None, :] <= blk_row0[:, None]).astype(jnp.int32), axis=1), N_EXPERTS - 1)
    own = blk_e[:, None] == jnp.arange(N_EXPERTS, dtype=jnp.int32)[None, :]
    seg_end = jnp.sum(jnp.where(own, (pad_start + counts)[None, :], 0), axis=1)
    n_valid = jnp.clip(seg_end - blk_row0, 0, BM).astype(jnp.int32)
    y = _expert_call(x_sorted, blk_e, n_valid, n_used, w_gate, w_up, w_down)
    pos_t = pos.reshape(T // TR, TOP_K, TR // TC, TC).transpose(0, 2, 1, 3).reshape(T // TC, TOP_K * TC)
    return y, pos_t, wts


def kernel(x_prompt, x_sample, state_delta, c, c_ctx, w_ada, b_ada, norm1_g, w_in, conv_w, a_log,
           dt_bias, onorm_g, pool_w, pool_scale, w_out, norm2_g, router_w, router_bias, exp_w_gate,
           exp_w_up, exp_w_down, sh_w_gate, sh_w_up, sh_w_down, final_g):
    Bc, Lc, D = x_prompt.shape
    Bl, Ll, _ = x_sample.shape
    n_ctx = Bc * Lc
    assert DEPTH == 1 and 1 + Bl <= MOD_ROWS and n_ctx % Ll == 0
    x_parts = (x_prompt.reshape(n_ctx, D), x_sample.reshape(Bl * Ll, D))
    cvec = jnp.concatenate([c_ctx[None], c, jnp.zeros((MOD_ROWS - 1 - Bl, D), c.dtype)], axis=0)
    l = 0
    mod = _ada_call(cvec, w_ada[l], b_ada[l]).reshape(MOD_ROWS, N_MOD, D)
    qkv, z, ba, u = _inproj_call(*x_parts, mod, norm1_g[l], w_in[l], Ll)
    dn = (conv_w[l], a_log[l], dt_bias[l], onorm_g[l])
    oa_c, st_ctx = _delta_call(qkv, z, ba, *dn, None, Bc, Lc, 0)
    oa_l, _ = _delta_call(qkv, z, ba, *dn, state_delta[:, l], Bl, Ll, n_ctx // Ll)
    op_c = _pool_call(u, pool_w[l], pool_scale[l], False, Bc, Lc, 0)
    op_l = _pool_call(u, pool_w[l], pool_scale[l], True, Bl, Ll, n_ctx // Ll)
    x1, h2, h2p, shared = _outproj_call(x_parts, (oa_c, oa_l), (op_c, op_l), mod, norm2_g[l], w_out[l],
                                        sh_w_gate[l], sh_w_up[l], sh_w_down[l], Ll)
    y, pos_t, wts = _moe_routed(h2, h2p, router_w[l], router_bias[l], exp_w_gate[l], exp_w_up[l],
                                exp_w_down[l])
    T = n_ctx + Bl * Ll
    gathered = _sc_gather_call(y, pos_t.reshape(-1))
    out_c, out_l = _combine_dense_call(gathered.reshape(T // TC, TOP_K, TC, D // 2), wts, x1, shared, mod,
                                       final_g, n_ctx, Ll)
    y_prompt = out_c.reshape(Bc, Lc, D)
    y_sample = out_l.reshape(Bl, Ll, D)
    new_state_delta = st_ctx[:, None].astype(x_prompt.dtype)
    return (y_prompt, y_sample, new_state_delta)
```

```python
import functools
import jax, jax.numpy as jnp
from jax import lax
from jax.experimental import pallas as pl
from jax.experimental.pallas import tpu as pltpu
from jax.experimental.pallas import tpu_sc as plsc

D_MODEL = 1024
DEPTH = 1
GRID_W = 64
D_MIX = D_MODEL
D_A = D_MIX // 2
D_P = D_MIX - D_A
H_A = 4
DK = D_A // H_A
DV = D_A // H_A
CONV_K = 5
CHUNK = 64
POOL_WINDOWS = (2, 4, 8, 16)
N_PG = len(POOL_WINDOWS)
PG = D_P // N_PG
N_EXPERTS = 256
TOP_K = 8
N_GROUPS = 8
TOPK_GROUP = 4
ROUTED_SCALE = 2.5
EPS = 1e-6
VMEM_LIMIT = 48 * 1024 * 1024


def _split_bf16(a):
    hi = a.astype(jnp.bfloat16)
    return hi, (a - hi.astype(jnp.float32)).astype(jnp.bfloat16)


def _bdot(a, b):
    return jnp.dot(a, b, preferred_element_type=jnp.float32)


def _pack_rows(x):
    m = x.shape[1] // 2
    lo = lax.bitcast_convert_type(x[:, :m].astype(jnp.bfloat16).astype(jnp.float32), jnp.uint32)
    hi = lax.bitcast_convert_type(x[:, m:].astype(jnp.bfloat16).astype(jnp.float32), jnp.uint32)
    return lax.bitcast_convert_type(hi | (lo >> 16), jnp.int32)


def _pack_rows_native(x):
    m = x.shape[1] // 2
    return pltpu.pack_elementwise([x[:, :m], x[:, m:]], packed_dtype=jnp.bfloat16)


def _unpack_rows_native(p):
    return tuple(pltpu.unpack_elementwise(p, index=i, packed_dtype=jnp.bfloat16, unpacked_dtype=jnp.float32)
                 for i in range(2))


N_MOD = 6
MOD_ROWS = 8
TM = 512


def _ada_kernel(c_ref, w_ref, b_ref, o_ref):
    c = c_ref[...]
    s = c * jax.nn.sigmoid(c)
    sh, sl = _split_bf16(s)
    wh, wl = _split_bf16(w_ref[...])
    o_ref[...] = _bdot(sh, wh) + (_bdot(sh, wl) + _bdot(sl, wh)) + b_ref[...]


def _ada_call(cvec, w_ada, b_ada):
    R, D = cvec.shape
    N = w_ada.shape[1]
    tn = 1024
    return pl.pallas_call(
        _ada_kernel,
        grid=(N // tn,),
        in_specs=[pl.BlockSpec((R, D), lambda j: (0, 0)),
                  pl.BlockSpec((D, tn), lambda j: (0, j)),
                  pl.BlockSpec((1, tn), lambda j: (0, j))],
        out_specs=pl.BlockSpec((R, tn), lambda j: (0, j)),
        out_shape=jax.ShapeDtypeStruct((R, N), jnp.float32),
    )(cvec, w_ada, b_ada.reshape(1, N))


def _mod_row(tile, tokens_per_tile, n_ctx, lat_len):
    t0 = tile * tokens_per_tile
    return jnp.where(t0 < n_ctx, 0, 1 + (t0 - n_ctx) // lat_len)


def _two_part_specs(n_ctx_tiles, width):
    return (pl.BlockSpec((TM, width), lambda i: (jnp.minimum(i, n_ctx_tiles - 1), 0)),
            pl.BlockSpec((TM, width), lambda i: (jnp.maximum(i - n_ctx_tiles, 0), 0)))


def _pick(n_ctx_tiles, ctx_ref, lat_ref):
    return jnp.where(pl.program_id(0) < n_ctx_tiles, ctx_ref[...], lat_ref[...])


def _inproj_kernel(xc_ref, xl_ref, mod_ref, g_ref, wq_ref, wz_ref, wb_ref, wu_ref, q_ref, z_ref, b_ref, u_ref,
                   *, n_ctx_tiles):
    x = _pick(n_ctx_tiles, xc_ref, xl_ref)
    y = x * lax.rsqrt(jnp.mean(x * x, axis=-1, keepdims=True) + EPS) * g_ref[...]
    h = (y * (1.0 + mod_ref[0, 1:2, :]) + mod_ref[0, 0:1, :]).astype(jnp.bfloat16)
    q_ref[...] = _bdot(h, wq_ref[...])
    z_ref[...] = _bdot(h, wz_ref[...])
    b_ref[...] = _bdot(h, wb_ref[...])
    u_ref[...] = _bdot(h, wu_ref[...])


def _inproj_call(x_ctx, x_lat, mod, norm1_g, w_in, lat_len):
    n_ctx, D = x_ctx.shape
    T = n_ctx + x_lat.shape[0]
    bf = jnp.bfloat16
    nq, nz, nb = 3 * D_A, D_A, 4 * H_A
    wq = w_in[:, :nq].astype(bf)
    wz = w_in[:, nq:nq + nz].astype(bf)
    wb = jnp.pad(w_in[:, nq + nz:nq + nz + nb], ((0, 0), (0, 128 - nb))).astype(bf)
    wu = w_in[:, nq + nz + nb:].astype(bf)
    row = functools.partial(_mod_row, tokens_per_tile=TM, n_ctx=n_ctx, lat_len=lat_len)

    def full(a):
        return pl.BlockSpec(a.shape, lambda i: (0, 0))

    def rows(n):
        return pl.BlockSpec((TM, n), lambda i: (i, 0))

    return pl.pallas_call(
        functools.partial(_inproj_kernel, n_ctx_tiles=n_ctx // TM),
        grid=(T // TM,),
        in_specs=[*_two_part_specs(n_ctx // TM, D), pl.BlockSpec((1, N_MOD, D), lambda i: (row(i), 0, 0)),
                  pl.BlockSpec((1, D), lambda i: (0, 0)), full(wq), full(wz), full(wb), full(wu)],
        out_specs=[rows(nq), rows(nz), rows(128), rows(D_P)],
        out_shape=[jax.ShapeDtypeStruct((T, nq), jnp.float32), jax.ShapeDtypeStruct((T, nz), jnp.float32),
                   jax.ShapeDtypeStruct((T, 128), jnp.float32), jax.ShapeDtypeStruct((T, D_P), jnp.float32)],
        compiler_params=pltpu.CompilerParams(dimension_semantics=("arbitrary",),
                                             vmem_limit_bytes=VMEM_LIMIT),
    )(x_ctx, x_lat, mod, norm1_g.reshape(1, D), wq, wz, wb, wu)


PT = 256


def _window_bounds(pos, w, n):
    return jnp.maximum(pos - w // 2, 0), jnp.minimum(pos + w - w // 2, n)


def _band_sum(band, x):
    xh, xl = _split_bf16(x)
    return _bdot(band, xh) + _bdot(band, xl)


def _pool_seq_kernel(u_ref, pw_ref, ps_ref, o_ref):
    L = u_ref.shape[0]
    ti = lax.broadcasted_iota(jnp.int32, (L, L), 0)
    ji = lax.broadcasted_iota(jnp.int32, (L, L), 1)
    tcol = lax.broadcasted_iota(jnp.int32, (L, 1), 0)
    for i, w in enumerate(POOL_WINDOWS):
        lo, hi = _window_bounds(ti, w, L)
        band = ((ji >= lo) & (ji < hi)).astype(jnp.bfloat16)
        clo, chi = _window_bounds(tcol, w, L)
        ug = u_ref[:, i * PG:(i + 1) * PG]
        mean = _band_sum(band, ug) / (chi - clo).astype(jnp.float32)
        d = (mean - ug).astype(jnp.bfloat16)
        o_ref[:, i * PG:(i + 1) * PG] = _bdot(d, pw_ref[i]) * ps_ref[:, i * PG:(i + 1) * PG]


def _pool_grid_kernel(u_ref, pw_ref, ps_ref, o_ref, pad_s, r_s):
    L = u_ref.shape[0]
    rows = L // GRID_W
    halo = (max(POOL_WINDOWS) // 2) * GRID_W
    pad_s[0:halo, :] = jnp.zeros((halo, D_P), jnp.float32)
    pad_s[halo + L:, :] = jnp.zeros((halo, D_P), jnp.float32)
    pad_s[halo:halo + L, :] = u_ref[...]
    ti = lax.broadcasted_iota(jnp.int32, (PT, PT), 0)
    ji = lax.broadcasted_iota(jnp.int32, (PT, PT), 1)
    tcol = lax.broadcasted_iota(jnp.int32, (PT, 1), 0)
    for i, w in enumerate(POOL_WINDOWS):
        cs = slice(i * PG, (i + 1) * PG)
        acc = None
        for dr in range(-(w // 2), w - w // 2):
            part = pad_s[halo + dr * GRID_W:halo + dr * GRID_W + L, cs]
            acc = part if acc is None else acc + part
        r_s[...] = acc
        lo, hi = _window_bounds(ti % GRID_W, w, GRID_W)
        band = ((ji // GRID_W == ti // GRID_W) & (ji % GRID_W >= lo) & (ji % GRID_W < hi)).astype(jnp.bfloat16)
        clo, chi = _window_bounds(tcol % GRID_W, w, GRID_W)
        ccnt = (chi - clo).astype(jnp.float32)
        for tile in range(L // PT):
            ts = slice(tile * PT, (tile + 1) * PT)
            rlo, rhi = _window_bounds(tile * (PT // GRID_W) + tcol // GRID_W, w, rows)
            mean = _band_sum(band, r_s[ts, :]) / ((rhi - rlo).astype(jnp.float32) * ccnt)
            d = (mean - u_ref[ts, cs]).astype(jnp.bfloat16)
            o_ref[ts, cs] = _bdot(d, pw_ref[i]) * ps_ref[:, cs]


def _pool_call(u, pool_w, pool_scale, grid, B, L, row_blk0):
    pw = pool_w.astype(jnp.bfloat16)
    ps = pool_scale.reshape(1, D_P)
    specs = dict(
        grid=(B,),
        in_specs=[pl.BlockSpec((L, D_P), lambda b: (row_blk0 + b, 0)),
                  pl.BlockSpec((N_PG, PG, PG), lambda b: (0, 0, 0)),
                  pl.BlockSpec((1, D_P), lambda b: (0, 0))],
        out_specs=pl.BlockSpec((L, D_P), lambda b: (b, 0)),
        out_shape=jax.ShapeDtypeStruct((B * L, D_P), jnp.float32),
        compiler_params=pltpu.CompilerParams(dimension_semantics=("arbitrary",),
                                             vmem_limit_bytes=VMEM_LIMIT))
    if not grid:
        return pl.pallas_call(_pool_seq_kernel, **specs)(u, pw, ps)
    halo = (max(POOL_WINDOWS) // 2) * GRID_W
    return pl.pallas_call(
        _pool_grid_kernel,
        scratch_shapes=[pltpu.VMEM((L + 2 * halo, D_P), jnp.float32), pltpu.VMEM((L, PG), jnp.float32)],
        **specs)(u, pw, ps)


def _outproj_kernel(xc_ref, xl_ref, oac_ref, oal_ref, opc_ref, opl_ref, mod_ref, g2_ref, wo_ref, sg_ref, su_ref,
                    sd_ref, rwh_ref, rwl_ref, x1_ref, lg_ref, h2p_ref, sh_ref, *, n_ctx_tiles):
    o_a = _pick(n_ctx_tiles, oac_ref, oal_ref)
    o_p = _pick(n_ctx_tiles, opc_ref, opl_ref)
    mix = (_bdot(o_a.astype(jnp.bfloat16), wo_ref[:D_A, :])
           + _bdot(o_p.astype(jnp.bfloat16), wo_ref[D_A:, :]))
    x1 = _pick(n_ctx_tiles, xc_ref, xl_ref) + mod_ref[0, 2:3, :] * mix
    x1_ref[...] = x1
    y = x1 * lax.rsqrt(jnp.mean(x1 * x1, axis=-1, keepdims=True) + EPS) * g2_ref[...]
    h2 = y * (1.0 + mod_ref[0, 4:5, :]) + mod_ref[0, 3:4, :]
    h2p_ref[...] = _pack_rows(h2)
    hb, hl = _split_bf16(h2)
    lg_ref[...] = _bdot(hb, rwh_ref[...]) + (_bdot(hb, rwl_ref[...]) + _bdot(hl, rwh_ref[...]))
    g = _bdot(hb, sg_ref[...])
    a = (g * jax.nn.sigmoid(g)) * _bdot(hb, su_ref[...])
    sh_ref[...] = _pack_rows(_bdot(a.astype(jnp.bfloat16), sd_ref[...]))


def _outproj_call(x_parts, oa_parts, op_parts, mod, norm2_g, w_out, sh_gate, sh_up, sh_down, router_w, lat_len):
    n_ctx, D = x_parts[0].shape
    T = n_ctx + x_parts[1].shape[0]
    nct = n_ctx // TM
    bf = jnp.bfloat16
    row = functools.partial(_mod_row, tokens_per_tile=TM, n_ctx=n_ctx, lat_len=lat_len)
    ws = [w_out.astype(bf), sh_gate.astype(bf), sh_up.astype(bf), sh_down.astype(bf), *_split_bf16(router_w)]

    def rows(n):
        return pl.BlockSpec((TM, n), lambda i: (i, 0))

    return pl.pallas_call(
        functools.partial(_outproj_kernel, n_ctx_tiles=nct),
        grid=(T // TM,),
        in_specs=[*_two_part_specs(nct, D), *_two_part_specs(nct, D_A), *_two_part_specs(nct, D_P),
                  pl.BlockSpec((1, N_MOD, D), lambda i: (row(i), 0, 0)),
                  pl.BlockSpec((1, D), lambda i: (0, 0))] + [pl.BlockSpec(w.shape, lambda i: (0, 0)) for w in ws],
        out_specs=[rows(D), rows(N_EXPERTS), rows(D // 2), rows(D // 2)],
        out_shape=[jax.ShapeDtypeStruct((T, D), jnp.float32), jax.ShapeDtypeStruct((T, N_EXPERTS), jnp.float32),
                   jax.ShapeDtypeStruct((T, D // 2), jnp.int32), jax.ShapeDtypeStruct((T, D // 2), jnp.int32)],
        compiler_params=pltpu.CompilerParams(dimension_semantics=("arbitrary",),
                                             vmem_limit_bytes=VMEM_LIMIT),
    )(*x_parts, *oa_parts, *op_parts, mod, norm2_g.reshape(1, D), *ws)


SC = 256
CPS = SC // CHUNK
BASE = 16
DELTA_HEAD_ROWS = 4096


def _mm(a, b):
    return jnp.dot(a.astype(jnp.bfloat16), b.astype(jnp.bfloat16), preferred_element_type=jnp.float32)


def _mm_nt(a, b):
    return lax.dot_general(a.astype(jnp.bfloat16), b.astype(jnp.bfloat16), (((1,), (1,)), ((), ())),
                           preferred_element_type=jnp.float32)


def _softplus(x):
    return jnp.maximum(x, 0.0) + jnp.log(1.0 + jnp.exp(-jnp.abs(x)))


def _delta_kernel(sc_ref, xq_ref, xk_ref, xv_ref, z_ref, bac_ref, bar_ref, cwq_ref, cwk_ref, cwv_ref,
                  og_ref, s0_ref, o_ref, st_ref, q_s, k_s, v_s, o_s, vn_s, *, n_sc, zero_init, hpb):
    hb = pl.program_id(1)
    L = q_s.shape[1]

    def conv(x_ref, w_ref, cs):
        x = x_ref[:, cs]
        row = lax.broadcasted_iota(jnp.int32, x.shape, 0)
        acc = x * w_ref[CONV_K // 2:CONV_K // 2 + 1, cs]
        for j in range(CONV_K):
            d = j - CONV_K // 2
            if d == 0:
                continue
            xs = pltpu.roll(x, (-d) % L, 0)
            ok = (row + d >= 0) & (row + d < L)
            acc = acc + jnp.where(ok, xs, 0.0) * w_ref[j:j + 1, cs]
        return acc * jax.nn.sigmoid(acc)

    for hh in range(hpb):
        cs = slice(hh * DK, (hh + 1) * DK)
        q = conv(xq_ref, cwq_ref, cs)
        q_s[hh] = q * lax.rsqrt(jnp.sum(q * q, axis=-1, keepdims=True) + EPS) * (DK ** -0.5)
        k = conv(xk_ref, cwk_ref, cs)
        k_s[hh] = k * lax.rsqrt(jnp.sum(k * k, axis=-1, keepdims=True) + EPS)
        v_s[hh] = conv(xv_ref, cwv_ref, cs)
    o_s[...] = jnp.zeros_like(o_s)

    ri = lax.broadcasted_iota(jnp.int32, (SC, SC), 0)
    ci = lax.broadcasted_iota(jnp.int32, (SC, SC), 1)
    same = (ri // CHUNK) == (ci // CHUNK)
    same_base = (ri // BASE) == (ci // BASE)
    merge_masks = [(ri // w) == (ci // w) for w in (2 * BASE, CHUNK)]
    eye = (ri == ci).astype(jnp.float32)
    rowi = lax.broadcasted_iota(jnp.int32, (SC, DV), 0)

    def prep(m, d, hh):
        r0 = pl.multiple_of(m * SC, SC)
        h = hb * hpb + hh
        q = q_s[hh, pl.ds(r0, SC), :]
        k = k_s[hh, pl.ds(r0, SC), :]
        v = v_s[hh, pl.ds(r0, SC), :]
        bc = bac_ref[0, hh, pl.ds(r0, SC), :]
        br = bar_ref[0, hh, m]
        a_l = sc_ref[d * H_A + h]
        dtb = sc_ref[2 * H_A + d * H_A + h]
        neg_ea = -jnp.exp(jnp.full((1, 1), a_l, jnp.float32))
        beta = jax.nn.sigmoid(bc[:, d:d + 1])
        g_col = neg_ea * _softplus(bc[:, 2 + d:3 + d] + dtb)
        g_row = neg_ea * _softplus(br[2 + d:3 + d, :] + dtb)
        if d == 0:
            tri, strict = same & (ci <= ri), same & (ci < ri)
        else:
            tri, strict = same & (ci >= ri), same & (ci > ri)
        tri_t = same & (ri <= ci) if d == 0 else same & (ri >= ci)
        gc_col = jnp.sum(jnp.where(tri, g_row, 0.0), axis=1, keepdims=True)
        gc_row = jnp.sum(jnp.where(tri_t, g_col, 0.0), axis=0, keepdims=True)
        gl_col = jnp.sum(jnp.where(same, g_row, 0.0), axis=1, keepdims=True)
        decay = jnp.where(tri, jnp.exp(jnp.where(tri, gc_col - gc_row, 0.0)), 0.0)
        kb = k * beta
        a = jnp.where(strict, _mm_nt(kb, k) * decay, 0.0)
        attn = jnp.where(tri, _mm_nt(q, k) * decay, 0.0)
        eg = jnp.exp(gc_col)
        x = jnp.concatenate([v * beta, kb * eg], axis=1)
        qd = q * eg
        kdt = (k * jnp.exp(gl_col - gc_col)).T
        return dict(r0=r0, a=a, attn=attn, x=x, qd=qd, kdt=kdt, egl=jnp.exp(gl_col))

    def run_chains(ms, states):
        n = len(chains)
        ops = [prep(ms[i], d, hh) for i, (hh, d) in enumerate(chains)]
        ps = [jnp.where(same_base, o["a"], 0.0) for o in ops]
        ts = [eye - p for p in ps]
        for _ in range(BASE.bit_length() - 2):
            ps = [_mm(p, p) for p in ps]
            ts = [t + _mm(t, p) for t, p in zip(ts, ps)]
        inner = same_base
        for outer in merge_masks:
            lows = [_mm(jnp.where(outer & ~inner, o["a"], 0.0), t) for o, t in zip(ops, ts)]
            ts = [t - _mm(t, low) for t, low in zip(ts, lows)]
            inner = outer
        xs = [_mm(t, o["x"]) for t, o in zip(ts, ops)]
        for i in range(n):
            vn_s[i] = jnp.zeros((SC, DV), jnp.float32)
        states = list(states)
        for step in range(CPS):
            cs = [step if d == 0 else CPS - 1 - step for _, d in chains]
            los = [c * CHUNK for c in cs]
            ws_qs = [_mm(jnp.concatenate([x[lo:lo + CHUNK, DV:], o["qd"][lo:lo + CHUNK]], axis=0), s)
                     for x, o, lo, s in zip(xs, ops, los, states)]
            for i in range(n):
                vn_s[i, los[i]:los[i] + CHUNK, :] = xs[i][los[i]:los[i] + CHUNK, :DV] - ws_qs[i][:CHUNK]
            vns = [vn_s[i] for i in range(n)]
            o_cs = [wq[CHUNK:] + _mm(o["attn"][lo:lo + CHUNK, :], vn)
                    for wq, o, lo, vn in zip(ws_qs, ops, los, vns)]
            for i, (hh, _) in enumerate(chains):
                o_s[hh, pl.ds(ops[i]["r0"] + los[i], CHUNK), :] += o_cs[i]
            states = [s * o["egl"][lo:lo + 1, :]
                      + _mm(o["kdt"], jnp.where((rowi >= lo) & (rowi < lo + CHUNK), vn, 0.0))
                      for s, o, lo, vn in zip(states, ops, los, vns)]
        return tuple(states)

    if zero_init:
        states = tuple(jnp.zeros((DK, DV), jnp.float32) for _ in range(2 * hpb))
    else:
        states = tuple(s0_ref[0, d, hh] for hh in range(hpb) for d in range(2))

    chains = [(hh, d) for hh in range(hpb) for d in range(2)]

    def body(m, carry):
        return run_chains([m if d == 0 else n_sc - 1 - m for _, d in chains], carry)

    if n_sc == 1:
        states = body(0, states)
    else:
        states = lax.fori_loop(0, n_sc, body, states)

    for hh in range(hpb):
        for d in range(2):
            st_ref[0, d, hh] = states[2 * hh + d]
        o = o_s[hh]
        o = o * lax.rsqrt(jnp.mean(o * o, axis=-1, keepdims=True) + EPS) * og_ref[...]
        zz = z_ref[:, hh * DV:(hh + 1) * DV]
        o_ref[:, hh * DV:(hh + 1) * DV] = o * (zz * jax.nn.sigmoid(zz))


def _delta_call(qkv, z, ba, conv_w, a_log, dt_bias, onorm_g, s0, B, L, row_blk0):
    n_sc = L // SC
    t0 = row_blk0 * L
    bah = ba[t0:t0 + B * L, :4 * H_A].reshape(B, L, 4, H_A).transpose(0, 3, 1, 2)
    bar = bah.reshape(B, H_A, n_sc, SC, 4).transpose(0, 1, 2, 4, 3)
    scal = jnp.concatenate([a_log.reshape(-1), dt_bias.reshape(-1)]).astype(jnp.float32)
    hpb = max(1, min(H_A, DELTA_HEAD_ROWS // L))
    n_hb = H_A // hpb
    zero_init = s0 is None
    if zero_init:
        s0 = jnp.zeros((1, 2, hpb, DK, DV), jnp.float32)
        s0_spec = pl.BlockSpec((1, 2, hpb, DK, DV), lambda b, h, sc: (0, 0, 0, 0, 0))
    else:
        s0_spec = pl.BlockSpec((1, 2, hpb, DK, DV), lambda b, h, sc: (b, 0, h, 0, 0))

    def col(off):
        return pl.BlockSpec((L, hpb * DK), lambda b, h, sc: (row_blk0 + b, off * n_hb + h))

    def cw(off):
        return pl.BlockSpec((CONV_K, hpb * DK), lambda b, h, sc: (0, off * n_hb + h))

    kern = functools.partial(_delta_kernel, n_sc=n_sc, zero_init=zero_init, hpb=hpb)
    return pl.pallas_call(
        kern,
        grid_spec=pltpu.PrefetchScalarGridSpec(
            num_scalar_prefetch=1,
            grid=(B, n_hb),
            in_specs=[col(0), col(1), col(2),
                      pl.BlockSpec((L, hpb * DV), lambda b, h, sc: (row_blk0 + b, h)),
                      pl.BlockSpec((1, hpb, L, 4), lambda b, h, sc: (b, h, 0, 0)),
                      pl.BlockSpec((1, hpb, n_sc, 4, SC), lambda b, h, sc: (b, h, 0, 0, 0)),
                      cw(0), cw(1), cw(2),
                      pl.BlockSpec((1, DV), lambda b, h, sc: (0, 0)),
                      s0_spec],
            out_specs=[pl.BlockSpec((L, hpb * DV), lambda b, h, sc: (b, h)),
                       pl.BlockSpec((1, 2, hpb, DK, DV), lambda b, h, sc: (b, 0, h, 0, 0))],
            scratch_shapes=[pltpu.VMEM((hpb, L, DK), jnp.float32), pltpu.VMEM((hpb, L, DK), jnp.float32),
                            pltpu.VMEM((hpb, L, DV), jnp.float32), pltpu.VMEM((hpb, L, DV), jnp.float32),
                            pltpu.VMEM((2 * hpb, SC, DV), jnp.float32)]),
        out_shape=[jax.ShapeDtypeStruct((B * L, D_A), jnp.float32),
                   jax.ShapeDtypeStruct((B, 2, H_A, DK, DV), jnp.float32)],
        compiler_params=pltpu.CompilerParams(dimension_semantics=("arbitrary", "arbitrary"),
                                             vmem_limit_bytes=VMEM_LIMIT),
    )(scal, qkv, qkv, qkv, z, bah, bar, conv_w, conv_w, conv_w, onorm_g.reshape(1, DV), s0)


TR = 256
GSZ = N_EXPERTS // N_GROUPS
NEG = -jnp.inf
BM = 640


def _route_kernel(lg_ref, rb_ref, idx_ref, rank_ref, w_ref, cnt_ref, cnt_s):
    i = pl.program_id(0)

    @pl.when(i == 0)
    def _():
        cnt_s[...] = jnp.zeros_like(cnt_s)

    scores = jax.nn.sigmoid(lg_ref[...].T)
    sel = scores + rb_ref[...]
    erow = lax.broadcasted_iota(jnp.int32, sel.shape, 0)
    grow = lax.broadcasted_iota(jnp.int32, (GSZ, TR), 0)

    def first_argmax(v, rows):
        m = jnp.max(v, axis=0, keepdims=True)
        first = jnp.min(jnp.where(v == m, rows, N_EXPERTS), axis=0, keepdims=True)
        return m, first

    gs = []
    for g in range(N_GROUPS):
        vg = sel[g * GSZ:(g + 1) * GSZ, :]
        m1, i1 = first_argmax(vg, grow)
        m2 = jnp.max(jnp.where(grow == i1, NEG, vg), axis=0, keepdims=True)
        gs.append(m1 + m2)
    cand = []
    for g in range(N_GROUPS):
        beat = jnp.zeros(gs[g].shape, jnp.int32)
        for o in range(N_GROUPS):
            if o == g:
                continue
            wins = (gs[o] > gs[g]) | ((gs[o] == gs[g]) & (o < g))
            beat = beat + wins.astype(jnp.int32)
        cand.append(jnp.where(beat < TOPK_GROUP, sel[g * GSZ:(g + 1) * GSZ, :], NEG))
    cand = jnp.concatenate(cand, axis=0)
    chosen = []
    picked = jnp.zeros(sel.shape, jnp.bool_)
    for _ in range(TOP_K):
        _, ik = first_argmax(cand, erow)
        hit = erow == ik
        chosen.append((ik, hit))
        picked = picked | hit
        cand = jnp.where(hit, NEG, cand)
    wsum = jnp.sum(jnp.where(picked, scores, 0.0), axis=0, keepdims=True)

    ri = lax.broadcasted_iota(jnp.int32, (TR, TR), 0)
    ci = lax.broadcasted_iota(jnp.int32, (TR, TR), 1)
    earlier = (ri < ci).astype(jnp.bfloat16)
    rank_mat = _bdot(picked.astype(jnp.bfloat16), earlier) + cnt_s[...]
    cnt_s[...] = cnt_s[...] + jnp.sum(picked.astype(jnp.float32), axis=1, keepdims=True)
    cnt_ref[...] = cnt_s[...].astype(jnp.int32)

    for k, (ik, hit) in enumerate(chosen):
        idx_ref[0, k:k + 1, :] = ik
        rank_ref[0, k:k + 1, :] = jnp.sum(jnp.where(hit, rank_mat, 0.0), axis=0, keepdims=True).astype(jnp.int32)
        w_ref[0, k:k + 1, :] = jnp.sum(jnp.where(hit, scores, 0.0), axis=0, keepdims=True) / wsum * ROUTED_SCALE


def _route_call(logits, router_bias):
    T = logits.shape[0]
    n_tiles = T // TR
    row_spec = pl.BlockSpec((1, TOP_K, TR), lambda i: (i, 0, 0))
    return pl.pallas_call(
        _route_kernel,
        grid=(n_tiles,),
        in_specs=[pl.BlockSpec((TR, N_EXPERTS), lambda i: (i, 0)),
                  pl.BlockSpec((N_EXPERTS, 1), lambda i: (0, 0))],
        out_specs=[row_spec, row_spec, row_spec, pl.BlockSpec((N_EXPERTS, 1), lambda i: (0, 0))],
        scratch_shapes=[pltpu.VMEM((N_EXPERTS, 1), jnp.float32)],
        out_shape=[jax.ShapeDtypeStruct((n_tiles, TOP_K, TR), jnp.int32),
                   jax.ShapeDtypeStruct((n_tiles, TOP_K, TR), jnp.int32),
                   jax.ShapeDtypeStruct((n_tiles, TOP_K, TR), jnp.float32),
                   jax.ShapeDtypeStruct((N_EXPERTS, 1), jnp.int32)],
        compiler_params=pltpu.CompilerParams(dimension_semantics=("arbitrary",)),
    )(logits, router_bias.reshape(N_EXPERTS, 1).astype(jnp.float32))


def _expert_kernel(blk_e_ref, nvalid_ref, nused_ref, x_ref, wg_ref, wu_ref, wd_ref, y_ref, wg_s, wu_s, wd_s):
    i = pl.program_id(0)

    @pl.when(i < nused_ref[0])
    def _():
        e = blk_e_ref[i]
        prev = blk_e_ref[jnp.maximum(i - 1, 0)]

        @pl.when((i == 0) | (e != prev))
        def _():
            wg_s[...] = wg_ref[0].astype(jnp.bfloat16)
            wu_s[...] = wu_ref[0].astype(jnp.bfloat16)
            wd_s[...] = wd_ref[0].astype(jnp.bfloat16)

        row = lax.broadcasted_iota(jnp.int32, (BM, 1), 0)
        xa, xb = _unpack_rows_native(jnp.where(row < nvalid_ref[i], x_ref[...], 0))
        xa = xa.astype(jnp.bfloat16)
        xb = xb.astype(jnp.bfloat16)
        half = xa.shape[1]
        g = _bdot(xa, wg_s[:half, :]) + _bdot(xb, wg_s[half:, :])
        u = _bdot(xa, wu_s[:half, :]) + _bdot(xb, wu_s[half:, :])
        a = (g * jax.nn.sigmoid(g)) * u
        y_ref[...] = _pack_rows_native(_bdot(a.astype(jnp.bfloat16), wd_s[...]))


def _expert_call(x_sorted, blk_e, n_valid, n_used, w_gate, w_up, w_down):
    n_pad, DH = x_sorted.shape
    n_blk = n_pad // BM
    E, D, F = w_gate.shape

    def row_map(i, be, nv, nu):
        return (jnp.minimum(i, nu[0] - 1), 0)

    def w_map(i, be, nv, nu):
        return (be[jnp.minimum(i, nu[0] - 1)], 0, 0)

    return pl.pallas_call(
        _expert_kernel,
        grid_spec=pltpu.PrefetchScalarGridSpec(
            num_scalar_prefetch=3,
            grid=(n_blk,),
            in_specs=[pl.BlockSpec((BM, DH), row_map),
                      pl.BlockSpec((1, D, F), w_map),
                      pl.BlockSpec((1, D, F), w_map),
                      pl.BlockSpec((1, F, D), w_map)],
            out_specs=pl.BlockSpec((BM, DH), row_map),
            scratch_shapes=[pltpu.VMEM((D, F), jnp.bfloat16), pltpu.VMEM((D, F), jnp.bfloat16),
                            pltpu.VMEM((F, D), jnp.bfloat16)]),
        out_shape=jax.ShapeDtypeStruct((n_pad, DH), jnp.int32),
        compiler_params=pltpu.CompilerParams(dimension_semantics=("arbitrary",),
                                             vmem_limit_bytes=VMEM_LIMIT),
    )(blk_e, n_valid, n_used, x_sorted, w_gate, w_up, w_down)


TC = 128


SC_CORES = 2
SC_SUBCORES = 16
SC_CHUNK = 128


def _sc_gather_call(table, idx):
    n_idx = idx.shape[0]
    width = table.shape[1]
    n_workers = SC_CORES * SC_SUBCORES
    per_worker = n_idx // n_workers
    assert per_worker * n_workers == n_idx and per_worker % SC_CHUNK == 0
    mesh = plsc.VectorSubcoreMesh(core_axis_name="c", subcore_axis_name="s")

    def body(table_hbm, idx_hbm, out_hbm, idx_v, rows_v, sem):
        wid = lax.axis_index("s") * SC_CORES + lax.axis_index("c")
        base = wid * per_worker

        @pl.loop(0, per_worker // SC_CHUNK)
        def _(ch):
            off = base + ch * SC_CHUNK
            pltpu.sync_copy(idx_hbm.at[pl.ds(off, SC_CHUNK)], idx_v)
            pltpu.async_copy(table_hbm.at[idx_v], rows_v, sem).wait()
            pltpu.sync_copy(rows_v, out_hbm.at[pl.ds(off, SC_CHUNK)])

    return pl.kernel(
        body, out_type=jax.ShapeDtypeStruct((n_idx, width), table.dtype), mesh=mesh,
        scratch_types=[pltpu.VMEM((SC_CHUNK,), jnp.int32), pltpu.VMEM((SC_CHUNK, width), table.dtype),
                       pltpu.SemaphoreType.DMA],
    )(table, idx)


def _sc_scatter_call(src, pos, n_out):
    n_idx = pos.shape[0]
    width = src.shape[1]
    n_workers = SC_CORES * SC_SUBCORES
    per_worker = n_idx // n_workers
    assert per_worker * n_workers == n_idx and per_worker % SC_CHUNK == 0 and TR % SC_CHUNK == 0
    mesh = plsc.VectorSubcoreMesh(core_axis_name="c", subcore_axis_name="s")
    tile_pairs = TOP_K * TR

    def body(src_hbm, pos_hbm, out_hbm, idx_v, rows_v, sem):
        wid = lax.axis_index("s") * SC_CORES + lax.axis_index("c")
        base = wid * per_worker

        @pl.loop(0, per_worker // SC_CHUNK)
        def _(ch):
            off = base + ch * SC_CHUNK
            row0 = (off // tile_pairs) * TR + off % TR
            pltpu.sync_copy(pos_hbm.at[pl.ds(off, SC_CHUNK)], idx_v)
            pltpu.sync_copy(src_hbm.at[pl.ds(row0, SC_CHUNK)], rows_v)
            pltpu.async_copy(rows_v, out_hbm.at[idx_v], sem).wait()

    return pl.kernel(
        body, out_type=jax.ShapeDtypeStruct((n_out, width), src.dtype), mesh=mesh,
        scratch_types=[pltpu.VMEM((SC_CHUNK,), jnp.int32), pltpu.VMEM((SC_CHUNK, width), src.dtype),
                       pltpu.SemaphoreType.DMA],
    )(src, pos)


def _positions_kernel(idx_ref, rank_ref, pstart_ref, pos_ref):
    erow = lax.broadcasted_iota(jnp.int32, (N_EXPERTS, TR), 0)
    pstart = pstart_ref[...]
    for k in range(TOP_K):
        hit = erow == idx_ref[0, k:k + 1, :]
        seg = jnp.sum(jnp.where(hit, pstart, 0), axis=0, keepdims=True)
        pos_ref[0, k:k + 1, :] = seg + rank_ref[0, k:k + 1, :]


def _positions_call(idx, rank, pad_start):
    n_tiles = idx.shape[0]
    row_spec = pl.BlockSpec((1, TOP_K, TR), lambda i: (i, 0, 0))
    return pl.pallas_call(
        _positions_kernel,
        grid=(n_tiles,),
        in_specs=[row_spec, row_spec, pl.BlockSpec((N_EXPERTS, 1), lambda i: (0, 0))],
        out_specs=row_spec,
        out_shape=jax.ShapeDtypeStruct((n_tiles, TOP_K, TR), jnp.int32),
    )(idx, rank, pad_start.reshape(N_EXPERTS, 1))


def _combine_dense_kernel(g_ref, w_ref, x1_ref, sh_ref, mod_ref, fg_ref, outc_ref, outl_ref, *, n_ctx_tiles):
    w = w_ref[...]
    acc_a = acc_b = None
    for k in range(TOP_K):
        ya, yb = _unpack_rows_native(g_ref[0, k])
        acc_a = w[:, k:k + 1] * ya if k == 0 else acc_a + w[:, k:k + 1] * ya
        acc_b = w[:, k:k + 1] * yb if k == 0 else acc_b + w[:, k:k + 1] * yb
    sa, sb = _unpack_rows_native(sh_ref[...])
    moe = jnp.concatenate([acc_a + sa, acc_b + sb], axis=1)
    x2 = x1_ref[...] + mod_ref[0, 5:6, :] * moe
    out = x2 * lax.rsqrt(jnp.mean(x2 * x2, axis=-1, keepdims=True) + EPS) * fg_ref[...]
    is_ctx = pl.program_id(0) < n_ctx_tiles

    @pl.when(is_ctx)
    def _():
        outc_ref[...] = out

    @pl.when(jnp.logical_not(is_ctx))
    def _():
        outl_ref[...] = out


def _combine_dense_call(gathered, wts, x1, shared, mod, final_g, n_ctx, lat_len):
    T, K = wts.shape
    DH = gathered.shape[-1]
    D = 2 * DH
    row = functools.partial(_mod_row, tokens_per_tile=TC, n_ctx=n_ctx, lat_len=lat_len)
    nct = n_ctx // TC
    return pl.pallas_call(
        functools.partial(_combine_dense_kernel, n_ctx_tiles=nct),
        grid=(T // TC,),
        in_specs=[pl.BlockSpec((1, K, TC, DH), lambda j: (j, 0, 0, 0)),
                  pl.BlockSpec((TC, K), lambda j: (j, 0)),
                  pl.BlockSpec((TC, D), lambda j: (j, 0)),
                  pl.BlockSpec((TC, DH), lambda j: (j, 0)),
                  pl.BlockSpec((1, N_MOD, D), lambda j: (row(j), 0, 0)),
                  pl.BlockSpec((1, D), lambda j: (0, 0))],
        out_specs=[pl.BlockSpec((TC, D), lambda j: (jnp.minimum(j, nct - 1), 0)),
                   pl.BlockSpec((TC, D), lambda j: (jnp.maximum(j - nct, 0), 0))],
        out_shape=[jax.ShapeDtypeStruct((n_ctx, D), jnp.float32), jax.ShapeDtypeStruct((T - n_ctx, D), jnp.float32)],
        compiler_params=pltpu.CompilerParams(dimension_semantics=("arbitrary",)),
    )(gathered, wts, x1, shared, mod, final_g.reshape(1, D))


def _moe_routed(logits, h2p, router_bias, w_gate, w_up, w_down):
    T = logits.shape[0]
    idx, rank, w_rows, cnt = _route_call(logits, router_bias)
    wts = w_rows.transpose(0, 2, 1).reshape(T, TOP_K)
    counts = cnt[:, 0]
    padded = (counts + BM - 1) // BM * BM
    pad_end = jnp.cumsum(padded)
    pad_start = (pad_end - padded).astype(jnp.int32)
    n_blk = -(-(T * TOP_K) // BM) + N_EXPERTS
    n_pad = n_blk * BM
    n_used = (pad_end[-1] // BM).astype(jnp.int32).reshape(1)
    pos = _positions_call(idx, rank, pad_start)
    x_sorted = _sc_scatter_call(h2p, pos.reshape(-1), n_pad)
    blk_row0 = jnp.arange(n_blk, dtype=jnp.int32) * BM
    blk_e = jnp.minimum(jnp.sum((pad_end[None, :] <= blk_row0[:, None]).astype(jnp.int32), axis=1), N_EXPERTS - 1)
    own = blk_e[:, None] == jnp.arange(N_EXPERTS, dtype=jnp.int32)[None, :]
    seg_end = jnp.sum(jnp.where(own, (pad_start + counts)[None, :], 0), axis=1)
    n_valid = jnp.clip(seg_end - blk_row0, 0, BM).astype(jnp.int32)
    y = _expert_call(x_sorted, blk_e, n_valid, n_used, w_gate, w_up, w_down)
    pos_t = pos.reshape(T // TR, TOP_K, TR // TC, TC).transpose(0, 2, 1, 3).reshape(T // TC, TOP_K * TC)
    return y, pos_t, wts


def kernel(x_prompt, x_sample, state_delta, c, c_ctx, w_ada, b_ada, norm1_g, w_in, conv_w, a_log,
           dt_bias, onorm_g, pool_w, pool_scale, w_out, norm2_g, router_w, router_bias, exp_w_gate,
           exp_w_up, exp_w_down, sh_w_gate, sh_w_up, sh_w_down, final_g):
    Bc, Lc, D = x_prompt.shape
    Bl, Ll, _ = x_sample.shape
    n_ctx = Bc * Lc
    assert DEPTH == 1 and 1 + Bl <= MOD_ROWS and n_ctx % Ll == 0
    x_parts = (x_prompt.reshape(n_ctx, D), x_sample.reshape(Bl * Ll, D))
    cvec = jnp.concatenate([c_ctx[None], c, jnp.zeros((MOD_ROWS - 1 - Bl, D), c.dtype)], axis=0)
    l = 0
    mod = _ada_call(cvec, w_ada[l], b_ada[l]).reshape(MOD_ROWS, N_MOD, D)
    qkv, z, ba, u = _inproj_call(*x_parts, mod, norm1_g[l], w_in[l], Ll)
    dn = (conv_w[l], a_log[l], dt_bias[l], onorm_g[l])
    oa_c, st_ctx = _delta_call(qkv, z, ba, *dn, None, Bc, Lc, 0)
    oa_l, _ = _delta_call(qkv, z, ba, *dn, state_delta[:, l], Bl, Ll, n_ctx // Ll)
    op_c = _pool_call(u, pool_w[l], pool_scale[l], False, Bc, Lc, 0)
    op_l = _pool_call(u, pool_w[l], pool_scale[l], True, Bl, Ll, n_ctx // Ll)
    x1, logits, h2p, shared = _outproj_call(x_parts, (oa_c, oa_l), (op_c, op_l), mod, norm2_g[l], w_out[l],
                                            sh_w_gate[l], sh_w_up[l], sh_w_down[l], router_w[l], Ll)
    y, pos_t, wts = _moe_routed(logits, h2p, router_bias[l], exp_w_gate[l], exp_w_up[l], exp_w_down[l])
    T = n_ctx + Bl * Ll
    gathered = _sc_gather_call(y, pos_t.reshape(-1))
    out_c, out_l = _combine_dense_call(gathered.reshape(T // TC, TOP_K, TC, D // 2), wts, x1, shared, mod,
                                       final_g, n_ctx, Ll)
    y_prompt = out_c.reshape(Bc, Lc, D)
    y_sample = out_l.reshape(Bl, Ll, D)
    new_state_delta = st_ctx[:, None].astype(x_prompt.dtype)
    return (y_prompt, y_sample, new_state_delta)
```

```python
import functools
import jax, jax.numpy as jnp
from jax import lax
from jax.experimental import pallas as pl
from jax.experimental.pallas import tpu as pltpu
from jax.experimental.pallas import tpu_sc as plsc

D_MODEL = 1024
DEPTH = 1
GRID_W = 64
D_MIX = D_MODEL
D_A = D_MIX // 2
D_P = D_MIX - D_A
H_A = 4
DK = D_A // H_A
DV = D_A // H_A
CONV_K = 5
CHUNK = 64
POOL_WINDOWS = (2, 4, 8, 16)
N_PG = len(POOL_WINDOWS)
PG = D_P // N_PG
N_EXPERTS = 256
TOP_K = 8
N_GROUPS = 8
TOPK_GROUP = 4
ROUTED_SCALE = 2.5
EPS = 1e-6
VMEM_LIMIT = 48 * 1024 * 1024


def _split_bf16(a):
    hi = a.astype(jnp.bfloat16)
    return hi, (a - hi.astype(jnp.float32)).astype(jnp.bfloat16)


def _bdot(a, b):
    return jnp.dot(a, b, preferred_element_type=jnp.float32)


def _pack_rows(x):
    m = x.shape[1] // 2
    lo = lax.bitcast_convert_type(x[:, :m].astype(jnp.bfloat16).astype(jnp.float32), jnp.uint32)
    hi = lax.bitcast_convert_type(x[:, m:].astype(jnp.bfloat16).astype(jnp.float32), jnp.uint32)
    return lax.bitcast_convert_type(hi | (lo >> 16), jnp.int32)


def _pack_rows_native(x):
    m = x.shape[1] // 2
    return pltpu.pack_elementwise([x[:, :m], x[:, m:]], packed_dtype=jnp.bfloat16)


def _unpack_rows_native(p):
    return tuple(pltpu.unpack_elementwise(p, index=i, packed_dtype=jnp.bfloat16, unpacked_dtype=jnp.float32)
                 for i in range(2))


N_MOD = 6
MOD_ROWS = 8
TM = 512


def _ada_kernel(c_ref, w_ref, b_ref, o_ref):
    c = c_ref[...]
    s = c * jax.nn.sigmoid(c)
    sh, sl = _split_bf16(s)
    wh, wl = _split_bf16(w_ref[...])
    o_ref[...] = _bdot(sh, wh) + (_bdot(sh, wl) + _bdot(sl, wh)) + b_ref[...]


def _ada_call(cvec, w_ada, b_ada):
    R, D = cvec.shape
    N = w_ada.shape[1]
    tn = 1024
    return pl.pallas_call(
        _ada_kernel,
        grid=(N // tn,),
        in_specs=[pl.BlockSpec((R, D), lambda j: (0, 0)),
                  pl.BlockSpec((D, tn), lambda j: (0, j)),
                  pl.BlockSpec((1, tn), lambda j: (0, j))],
        out_specs=pl.BlockSpec((R, tn), lambda j: (0, j)),
        out_shape=jax.ShapeDtypeStruct((R, N), jnp.float32),
    )(cvec, w_ada, b_ada.reshape(1, N))


def _mod_row(tile, tokens_per_tile, n_ctx, lat_len):
    t0 = tile * tokens_per_tile
    return jnp.where(t0 < n_ctx, 0, 1 + (t0 - n_ctx) // lat_len)


def _two_part_specs(n_ctx_tiles, width):
    return (pl.BlockSpec((TM, width), lambda i: (jnp.minimum(i, n_ctx_tiles - 1), 0)),
            pl.BlockSpec((TM, width), lambda i: (jnp.maximum(i - n_ctx_tiles, 0), 0)))


def _pick(n_ctx_tiles, ctx_ref, lat_ref):
    return jnp.where(pl.program_id(0) < n_ctx_tiles, ctx_ref[...], lat_ref[...])


def _inproj_kernel(xc_ref, xl_ref, mod_ref, g_ref, wq_ref, wz_ref, wb_ref, wu_ref, q_ref, z_ref, b_ref, u_ref,
                   *, n_ctx_tiles):
    x = _pick(n_ctx_tiles, xc_ref, xl_ref)
    y = x * lax.rsqrt(jnp.mean(x * x, axis=-1, keepdims=True) + EPS) * g_ref[...]
    h = (y * (1.0 + mod_ref[0, 1:2, :]) + mod_ref[0, 0:1, :]).astype(jnp.bfloat16)
    q_ref[...] = _bdot(h, wq_ref[...])
    z_ref[...] = _bdot(h, wz_ref[...])
    b_ref[...] = _bdot(h, wb_ref[...])
    u_ref[...] = _bdot(h, wu_ref[...])


def _inproj_call(x_ctx, x_lat, mod, norm1_g, w_in, lat_len):
    n_ctx, D = x_ctx.shape
    T = n_ctx + x_lat.shape[0]
    bf = jnp.bfloat16
    nq, nz, nb = 3 * D_A, D_A, 4 * H_A
    wq = w_in[:, :nq].astype(bf)
    wz = w_in[:, nq:nq + nz].astype(bf)
    wb = jnp.pad(w_in[:, nq + nz:nq + nz + nb], ((0, 0), (0, 128 - nb))).astype(bf)
    wu = w_in[:, nq + nz + nb:].astype(bf)
    row = functools.partial(_mod_row, tokens_per_tile=TM, n_ctx=n_ctx, lat_len=lat_len)

    def full(a):
        return pl.BlockSpec(a.shape, lambda i: (0, 0))

    def rows(n):
        return pl.BlockSpec((TM, n), lambda i: (i, 0))

    return pl.pallas_call(
        functools.partial(_inproj_kernel, n_ctx_tiles=n_ctx // TM),
        grid=(T // TM,),
        in_specs=[*_two_part_specs(n_ctx // TM, D), pl.BlockSpec((1, N_MOD, D), lambda i: (row(i), 0, 0)),
                  pl.BlockSpec((1, D), lambda i: (0, 0)), full(wq), full(wz), full(wb), full(wu)],
        out_specs=[rows(nq), rows(nz), rows(128), rows(D_P)],
        out_shape=[jax.ShapeDtypeStruct((T, nq), jnp.float32), jax.ShapeDtypeStruct((T, nz), jnp.float32),
                   jax.ShapeDtypeStruct((T, 128), jnp.float32), jax.ShapeDtypeStruct((T, D_P), jnp.float32)],
        compiler_params=pltpu.CompilerParams(dimension_semantics=("arbitrary",),
                                             vmem_limit_bytes=VMEM_LIMIT),
    )(x_ctx, x_lat, mod, norm1_g.reshape(1, D), wq, wz, wb, wu)


PT = 256


def _window_bounds(pos, w, n):
    return jnp.maximum(pos - w // 2, 0), jnp.minimum(pos + w - w // 2, n)


def _band_sum(band, x):
    xh, xl = _split_bf16(x)
    return _bdot(band, xh) + _bdot(band, xl)


def _pool_seq_kernel(u_ref, pw_ref, ps_ref, o_ref):
    L = u_ref.shape[0]
    ti = lax.broadcasted_iota(jnp.int32, (L, L), 0)
    ji = lax.broadcasted_iota(jnp.int32, (L, L), 1)
    tcol = lax.broadcasted_iota(jnp.int32, (L, 1), 0)
    for i, w in enumerate(POOL_WINDOWS):
        lo, hi = _window_bounds(ti, w, L)
        band = ((ji >= lo) & (ji < hi)).astype(jnp.bfloat16)
        clo, chi = _window_bounds(tcol, w, L)
        ug = u_ref[:, i * PG:(i + 1) * PG]
        mean = _band_sum(band, ug) / (chi - clo).astype(jnp.float32)
        d = (mean - ug).astype(jnp.bfloat16)
        o_ref[:, i * PG:(i + 1) * PG] = _bdot(d, pw_ref[i]) * ps_ref[:, i * PG:(i + 1) * PG]


def _pool_grid_kernel(u_ref, pw_ref, ps_ref, o_ref, pad_s, r_s):
    L = u_ref.shape[0]
    rows = L // GRID_W
    halo = (max(POOL_WINDOWS) // 2) * GRID_W
    pad_s[0:halo, :] = jnp.zeros((halo, D_P), jnp.float32)
    pad_s[halo + L:, :] = jnp.zeros((halo, D_P), jnp.float32)
    pad_s[halo:halo + L, :] = u_ref[...]
    ti = lax.broadcasted_iota(jnp.int32, (PT, PT), 0)
    ji = lax.broadcasted_iota(jnp.int32, (PT, PT), 1)
    tcol = lax.broadcasted_iota(jnp.int32, (PT, 1), 0)
    for i, w in enumerate(POOL_WINDOWS):
        cs = slice(i * PG, (i + 1) * PG)
        acc = None
        for dr in range(-(w // 2), w - w // 2):
            part = pad_s[halo + dr * GRID_W:halo + dr * GRID_W + L, cs]
            acc = part if acc is None else acc + part
        r_s[...] = acc
        lo, hi = _window_bounds(ti % GRID_W, w, GRID_W)
        band = ((ji // GRID_W == ti // GRID_W) & (ji % GRID_W >= lo) & (ji % GRID_W < hi)).astype(jnp.bfloat16)
        clo, chi = _window_bounds(tcol % GRID_W, w, GRID_W)
        ccnt = (chi - clo).astype(jnp.float32)
        for tile in range(L // PT):
            ts = slice(tile * PT, (tile + 1) * PT)
            rlo, rhi = _window_bounds(tile * (PT // GRID_W) + tcol // GRID_W, w, rows)
            mean = _band_sum(band, r_s[ts, :]) / ((rhi - rlo).astype(jnp.float32) * ccnt)
            d = (mean - u_ref[ts, cs]).astype(jnp.bfloat16)
            o_ref[ts, cs] = _bdot(d, pw_ref[i]) * ps_ref[:, cs]


def _pool_call(u, pool_w, pool_scale, grid, B, L, row_blk0):
    pw = pool_w.astype(jnp.bfloat16)
    ps = pool_scale.reshape(1, D_P)
    specs = dict(
        grid=(B,),
        in_specs=[pl.BlockSpec((L, D_P), lambda b: (row_blk0 + b, 0)),
                  pl.BlockSpec((N_PG, PG, PG), lambda b: (0, 0, 0)),
                  pl.BlockSpec((1, D_P), lambda b: (0, 0))],
        out_specs=pl.BlockSpec((L, D_P), lambda b: (b, 0)),
        out_shape=jax.ShapeDtypeStruct((B * L, D_P), jnp.float32),
        compiler_params=pltpu.CompilerParams(dimension_semantics=("arbitrary",),
                                             vmem_limit_bytes=VMEM_LIMIT))
    if not grid:
        return pl.pallas_call(_pool_seq_kernel, **specs)(u, pw, ps)
    halo = (max(POOL_WINDOWS) // 2) * GRID_W
    return pl.pallas_call(
        _pool_grid_kernel,
        scratch_shapes=[pltpu.VMEM((L + 2 * halo, D_P), jnp.float32), pltpu.VMEM((L, PG), jnp.float32)],
        **specs)(u, pw, ps)


def _outproj_kernel(xc_ref, xl_ref, oac_ref, oal_ref, opc_ref, opl_ref, mod_ref, g2_ref, wo_ref, sg_ref, su_ref,
                    sd_ref, rwh_ref, rwl_ref, x1_ref, lg_ref, h2p_ref, sh_ref, *, n_ctx_tiles):
    o_a = _pick(n_ctx_tiles, oac_ref, oal_ref)
    o_p = _pick(n_ctx_tiles, opc_ref, opl_ref)
    mix = (_bdot(o_a.astype(jnp.bfloat16), wo_ref[:D_A, :])
           + _bdot(o_p.astype(jnp.bfloat16), wo_ref[D_A:, :]))
    x1 = _pick(n_ctx_tiles, xc_ref, xl_ref) + mod_ref[0, 2:3, :] * mix
    x1_ref[...] = x1
    y = x1 * lax.rsqrt(jnp.mean(x1 * x1, axis=-1, keepdims=True) + EPS) * g2_ref[...]
    h2 = y * (1.0 + mod_ref[0, 4:5, :]) + mod_ref[0, 3:4, :]
    h2p_ref[...] = _pack_rows(h2)
    hb, hl = _split_bf16(h2)
    lg_ref[...] = _bdot(hb, rwh_ref[...]) + (_bdot(hb, rwl_ref[...]) + _bdot(hl, rwh_ref[...]))
    g = _bdot(hb, sg_ref[...])
    a = (g * jax.nn.sigmoid(g)) * _bdot(hb, su_ref[...])
    sh_ref[...] = _pack_rows(_bdot(a.astype(jnp.bfloat16), sd_ref[...]))


def _outproj_call(x_parts, oa_parts, op_parts, mod, norm2_g, w_out, sh_gate, sh_up, sh_down, router_w, lat_len):
    n_ctx, D = x_parts[0].shape
    T = n_ctx + x_parts[1].shape[0]
    nct = n_ctx // TM
    bf = jnp.bfloat16
    row = functools.partial(_mod_row, tokens_per_tile=TM, n_ctx=n_ctx, lat_len=lat_len)
    ws = [w_out.astype(bf), sh_gate.astype(bf), sh_up.astype(bf), sh_down.astype(bf), *_split_bf16(router_w)]

    def rows(n):
        return pl.BlockSpec((TM, n), lambda i: (i, 0))

    return pl.pallas_call(
        functools.partial(_outproj_kernel, n_ctx_tiles=nct),
        grid=(T // TM,),
        in_specs=[*_two_part_specs(nct, D), *_two_part_specs(nct, D_A), *_two_part_specs(nct, D_P),
                  pl.BlockSpec((1, N_MOD, D), lambda i: (row(i), 0, 0)),
                  pl.BlockSpec((1, D), lambda i: (0, 0))] + [pl.BlockSpec(w.shape, lambda i: (0, 0)) for w in ws],
        out_specs=[rows(D), rows(N_EXPERTS), rows(D // 2), rows(D // 2)],
        out_shape=[jax.ShapeDtypeStruct((T, D), jnp.float32), jax.ShapeDtypeStruct((T, N_EXPERTS), jnp.float32),
                   jax.ShapeDtypeStruct((T, D // 2), jnp.int32), jax.ShapeDtypeStruct((T, D // 2), jnp.int32)],
        compiler_params=pltpu.CompilerParams(dimension_semantics=("arbitrary",),
                                             vmem_limit_bytes=VMEM_LIMIT),
    )(*x_parts, *oa_parts, *op_parts, mod, norm2_g.reshape(1, D), *ws)


SC = 256
CPS = SC // CHUNK
BASE = 16
DELTA_HEAD_ROWS = 4096


def _mm(a, b):
    return jnp.dot(a.astype(jnp.bfloat16), b.astype(jnp.bfloat16), preferred_element_type=jnp.float32)


def _mm_nt(a, b):
    return lax.dot_general(a.astype(jnp.bfloat16), b.astype(jnp.bfloat16), (((1,), (1,)), ((), ())),
                           preferred_element_type=jnp.float32)


def _softplus(x):
    return jnp.maximum(x, 0.0) + jnp.log(1.0 + jnp.exp(-jnp.abs(x)))


def _delta_kernel(sc_ref, xq_ref, xk_ref, xv_ref, z_ref, bac_ref, bar_ref, cwq_ref, cwk_ref, cwv_ref,
                  og_ref, s0_ref, o_ref, st_ref, q_s, k_s, v_s, o_s, vn_s, *, n_sc, zero_init, hpb):
    hb = pl.program_id(1)
    L = q_s.shape[1]

    def conv(x_ref, w_ref, cs):
        x = x_ref[:, cs]
        row = lax.broadcasted_iota(jnp.int32, x.shape, 0)
        acc = x * w_ref[CONV_K // 2:CONV_K // 2 + 1, cs]
        for j in range(CONV_K):
            d = j - CONV_K // 2
            if d == 0:
                continue
            xs = pltpu.roll(x, (-d) % L, 0)
            ok = (row + d >= 0) & (row + d < L)
            acc = acc + jnp.where(ok, xs, 0.0) * w_ref[j:j + 1, cs]
        return acc * jax.nn.sigmoid(acc)

    for hh in range(hpb):
        cs = slice(hh * DK, (hh + 1) * DK)
        q = conv(xq_ref, cwq_ref, cs)
        q_s[hh] = q * lax.rsqrt(jnp.sum(q * q, axis=-1, keepdims=True) + EPS) * (DK ** -0.5)
        k = conv(xk_ref, cwk_ref, cs)
        k_s[hh] = k * lax.rsqrt(jnp.sum(k * k, axis=-1, keepdims=True) + EPS)
        v_s[hh] = conv(xv_ref, cwv_ref, cs)
    o_s[...] = jnp.zeros_like(o_s)

    ri = lax.broadcasted_iota(jnp.int32, (SC, SC), 0)
    ci = lax.broadcasted_iota(jnp.int32, (SC, SC), 1)
    same = (ri // CHUNK) == (ci // CHUNK)
    same_base = (ri // BASE) == (ci // BASE)
    merge_masks = [(ri // w) == (ci // w) for w in (2 * BASE, CHUNK)]
    eye = (ri == ci).astype(jnp.float32)
    rowi = lax.broadcasted_iota(jnp.int32, (SC, DV), 0)

    def prep(m, d, hh):
        r0 = pl.multiple_of(m * SC, SC)
        h = hb * hpb + hh
        q = q_s[hh, pl.ds(r0, SC), :]
        k = k_s[hh, pl.ds(r0, SC), :]
        v = v_s[hh, pl.ds(r0, SC), :]
        bc = bac_ref[0, hh, pl.ds(r0, SC), :]
        br = bar_ref[0, hh, m]
        a_l = sc_ref[d * H_A + h]
        dtb = sc_ref[2 * H_A + d * H_A + h]
        neg_ea = -jnp.exp(jnp.full((1, 1), a_l, jnp.float32))
        beta = jax.nn.sigmoid(bc[:, d:d + 1])
        g_col = neg_ea * _softplus(bc[:, 2 + d:3 + d] + dtb)
        g_row = neg_ea * _softplus(br[2 + d:3 + d, :] + dtb)
        if d == 0:
            tri, strict = same & (ci <= ri), same & (ci < ri)
        else:
            tri, strict = same & (ci >= ri), same & (ci > ri)
        tri_t = same & (ri <= ci) if d == 0 else same & (ri >= ci)
        gc_col = jnp.sum(jnp.where(tri, g_row, 0.0), axis=1, keepdims=True)
        gc_row = jnp.sum(jnp.where(tri_t, g_col, 0.0), axis=0, keepdims=True)
        gl_col = jnp.sum(jnp.where(same, g_row, 0.0), axis=1, keepdims=True)
        decay = jnp.where(tri, jnp.exp(jnp.where(tri, gc_col - gc_row, 0.0)), 0.0)
        kb = k * beta
        a = jnp.where(strict, _mm_nt(kb, k) * decay, 0.0)
        attn = jnp.where(tri, _mm_nt(q, k) * decay, 0.0)
        eg = jnp.exp(gc_col)
        x = jnp.concatenate([v * beta, kb * eg], axis=1)
        qd = q * eg
        kdt = (k * jnp.exp(gl_col - gc_col)).T
        return dict(r0=r0, a=a, attn=attn, x=x, qd=qd, kdt=kdt, egl=jnp.exp(gl_col))

    def run_chains(ms, states):
        n = len(chains)
        ops = [prep(ms[i], d, hh) for i, (hh, d) in enumerate(chains)]
        ps = [jnp.where(same_base, o["a"], 0.0) for o in ops]
        ts = [eye - p for p in ps]
        for _ in range(BASE.bit_length() - 2):
            ps = [_mm(p, p) for p in ps]
            ts = [t + _mm(t, p) for t, p in zip(ts, ps)]
        inner = same_base
        for outer in merge_masks:
            lows = [_mm(jnp.where(outer & ~inner, o["a"], 0.0), t) for o, t in zip(ops, ts)]
            ts = [t - _mm(t, low) for t, low in zip(ts, lows)]
            inner = outer
        xs = [_mm(t, o["x"]) for t, o in zip(ts, ops)]
        for i in range(n):
            vn_s[i] = jnp.zeros((SC, DV), jnp.float32)
        states = list(states)
        for step in range(CPS):
            cs = [step if d == 0 else CPS - 1 - step for _, d in chains]
            los = [c * CHUNK for c in cs]
            ws_qs = [_mm(jnp.concatenate([x[lo:lo + CHUNK, DV:], o["qd"][lo:lo + CHUNK]], axis=0), s)
                     for x, o, lo, s in zip(xs, ops, los, states)]
            for i in range(n):
                vn_s[i, los[i]:los[i] + CHUNK, :] = xs[i][los[i]:los[i] + CHUNK, :DV] - ws_qs[i][:CHUNK]
            vns = [vn_s[i] for i in range(n)]
            o_cs = [wq[CHUNK:] + _mm(o["attn"][lo:lo + CHUNK, :], vn)
                    for wq, o, lo, vn in zip(ws_qs, ops, los, vns)]
            for i, (hh, _) in enumerate(chains):
                o_s[hh, pl.ds(ops[i]["r0"] + los[i], CHUNK), :] += o_cs[i]
            states = [s * o["egl"][lo:lo + 1, :]
                      + _mm(o["kdt"], jnp.where((rowi >= lo) & (rowi < lo + CHUNK), vn, 0.0))
                      for s, o, lo, vn in zip(states, ops, los, vns)]
        return tuple(states)

    if zero_init:
        states = tuple(jnp.zeros((DK, DV), jnp.float32) for _ in range(2 * hpb))
    else:
        states = tuple(s0_ref[0, d, hh] for hh in range(hpb) for d in range(2))

    chains = [(hh, d) for hh in range(hpb) for d in range(2)]

    def body(m, carry):
        return run_chains([m if d == 0 else n_sc - 1 - m for _, d in chains], carry)

    if n_sc == 1:
        states = body(0, states)
    else:
        states = lax.fori_loop(0, n_sc, body, states)

    for hh in range(hpb):
        for d in range(2):
            st_ref[0, d, hh] = states[2 * hh + d]
        o = o_s[hh]
        o = o * lax.rsqrt(jnp.mean(o * o, axis=-1, keepdims=True) + EPS) * og_ref[...]
        zz = z_ref[:, hh * DV:(hh + 1) * DV]
        o_ref[:, hh * DV:(hh + 1) * DV] = o * (zz * jax.nn.sigmoid(zz))


def _delta_call(qkv, z, ba, conv_w, a_log, dt_bias, onorm_g, s0, B, L, row_blk0):
    n_sc = L // SC
    t0 = row_blk0 * L
    bah = ba[t0:t0 + B * L, :4 * H_A].reshape(B, L, 4, H_A).transpose(0, 3, 1, 2)
    bar = bah.reshape(B, H_A, n_sc, SC, 4).transpose(0, 1, 2, 4, 3)
    scal = jnp.concatenate([a_log.reshape(-1), dt_bias.reshape(-1)]).astype(jnp.float32)
    hpb = max(1, min(H_A, DELTA_HEAD_ROWS // L))
    n_hb = H_A // hpb
    zero_init = s0 is None
    if zero_init:
        s0 = jnp.zeros((1, 2, hpb, DK, DV), jnp.float32)
        s0_spec = pl.BlockSpec((1, 2, hpb, DK, DV), lambda b, h, sc: (0, 0, 0, 0, 0))
    else:
        s0_spec = pl.BlockSpec((1, 2, hpb, DK, DV), lambda b, h, sc: (b, 0, h, 0, 0))

    def col(off):
        return pl.BlockSpec((L, hpb * DK), lambda b, h, sc: (row_blk0 + b, off * n_hb + h))

    def cw(off):
        return pl.BlockSpec((CONV_K, hpb * DK), lambda b, h, sc: (0, off * n_hb + h))

    kern = functools.partial(_delta_kernel, n_sc=n_sc, zero_init=zero_init, hpb=hpb)
    return pl.pallas_call(
        kern,
        grid_spec=pltpu.PrefetchScalarGridSpec(
            num_scalar_prefetch=1,
            grid=(B, n_hb),
            in_specs=[col(0), col(1), col(2),
                      pl.BlockSpec((L, hpb * DV), lambda b, h, sc: (row_blk0 + b, h)),
                      pl.BlockSpec((1, hpb, L, 4), lambda b, h, sc: (b, h, 0, 0)),
                      pl.BlockSpec((1, hpb, n_sc, 4, SC), lambda b, h, sc: (b, h, 0, 0, 0)),
                      cw(0), cw(1), cw(2),
                      pl.BlockSpec((1, DV), lambda b, h, sc: (0, 0)),
                      s0_spec],
            out_specs=[pl.BlockSpec((L, hpb * DV), lambda b, h, sc: (b, h)),
                       pl.BlockSpec((1, 2, hpb, DK, DV), lambda b, h, sc: (b, 0, h, 0, 0))],
            scratch_shapes=[pltpu.VMEM((hpb, L, DK), jnp.float32), pltpu.VMEM((hpb, L, DK), jnp.float32),
                            pltpu.VMEM((hpb, L, DV), jnp.float32), pltpu.VMEM((hpb, L, DV), jnp.float32),
                            pltpu.VMEM((2 * hpb, SC, DV), jnp.float32)]),
        out_shape=[jax.ShapeDtypeStruct((B * L, D_A), jnp.float32),
                   jax.ShapeDtypeStruct((B, 2, H_A, DK, DV), jnp.float32)],
        compiler_params=pltpu.CompilerParams(dimension_semantics=("arbitrary", "arbitrary"),
                                             vmem_limit_bytes=VMEM_LIMIT),
    )(scal, qkv, qkv, qkv, z, bah, bar, conv_w, conv_w, conv_w, onorm_g.reshape(1, DV), s0)


TR = 256
GSZ = N_EXPERTS // N_GROUPS
NEG = -jnp.inf
BM = 640


def _route_kernel(lg_ref, rb_ref, idx_ref, rank_ref, w_ref, cnt_ref, cnt_s):
    i = pl.program_id(0)

    @pl.when(i == 0)
    def _():
        cnt_s[...] = jnp.zeros_like(cnt_s)

    scores = jax.nn.sigmoid(lg_ref[...].T)
    sel = scores + rb_ref[...]
    erow = lax.broadcasted_iota(jnp.int32, sel.shape, 0)
    grow = lax.broadcasted_iota(jnp.int32, (GSZ, TR), 0)

    def first_argmax(v, rows):
        m = jnp.max(v, axis=0, keepdims=True)
        first = jnp.min(jnp.where(v == m, rows, N_EXPERTS), axis=0, keepdims=True)
        return m, first

    gs = []
    for g in range(N_GROUPS):
        vg = sel[g * GSZ:(g + 1) * GSZ, :]
        m1, i1 = first_argmax(vg, grow)
        m2 = jnp.max(jnp.where(grow == i1, NEG, vg), axis=0, keepdims=True)
        gs.append(m1 + m2)
    cand = []
    for g in range(N_GROUPS):
        beat = jnp.zeros(gs[g].shape, jnp.int32)
        for o in range(N_GROUPS):
            if o == g:
                continue
            wins = (gs[o] > gs[g]) | ((gs[o] == gs[g]) & (o < g))
            beat = beat + wins.astype(jnp.int32)
        cand.append(jnp.where(beat < TOPK_GROUP, sel[g * GSZ:(g + 1) * GSZ, :], NEG))
    cand = jnp.concatenate(cand, axis=0)
    chosen = []
    picked = jnp.zeros(sel.shape, jnp.bool_)
    for _ in range(TOP_K):
        _, ik = first_argmax(cand, erow)
        hit = erow == ik
        chosen.append((ik, hit))
        picked = picked | hit
        cand = jnp.where(hit, NEG, cand)
    wsum = jnp.sum(jnp.where(picked, scores, 0.0), axis=0, keepdims=True)

    ri = lax.broadcasted_iota(jnp.int32, (TR, TR), 0)
    ci = lax.broadcasted_iota(jnp.int32, (TR, TR), 1)
    earlier = (ri < ci).astype(jnp.bfloat16)
    rank_mat = _bdot(picked.astype(jnp.bfloat16), earlier) + cnt_s[...]
    cnt_s[...] = cnt_s[...] + jnp.sum(picked.astype(jnp.float32), axis=1, keepdims=True)
    cnt_ref[...] = cnt_s[...].astype(jnp.int32)

    for k, (ik, hit) in enumerate(chosen):
        idx_ref[0, k:k + 1, :] = ik
        rank_ref[0, k:k + 1, :] = jnp.sum(jnp.where(hit, rank_mat, 0.0), axis=0, keepdims=True).astype(jnp.int32)
        w_ref[0, k:k + 1, :] = jnp.sum(jnp.where(hit, scores, 0.0), axis=0, keepdims=True) / wsum * ROUTED_SCALE


def _route_call(logits, router_bias):
    T = logits.shape[0]
    n_tiles = T // TR
    row_spec = pl.BlockSpec((1, TOP_K, TR), lambda i: (i, 0, 0))
    return pl.pallas_call(
        _route_kernel,
        grid=(n_tiles,),
        in_specs=[pl.BlockSpec((TR, N_EXPERTS), lambda i: (i, 0)),
                  pl.BlockSpec((N_EXPERTS, 1), lambda i: (0, 0))],
        out_specs=[row_spec, row_spec, row_spec, pl.BlockSpec((N_EXPERTS, 1), lambda i: (0, 0))],
        scratch_shapes=[pltpu.VMEM((N_EXPERTS, 1), jnp.float32)],
        out_shape=[jax.ShapeDtypeStruct((n_tiles, TOP_K, TR), jnp.int32),
                   jax.ShapeDtypeStruct((n_tiles, TOP_K, TR), jnp.int32),
                   jax.ShapeDtypeStruct((n_tiles, TOP_K, TR), jnp.float32),
                   jax.ShapeDtypeStruct((N_EXPERTS, 1), jnp.int32)],
        compiler_params=pltpu.CompilerParams(dimension_semantics=("arbitrary",)),
    )(logits, router_bias.reshape(N_EXPERTS, 1).astype(jnp.float32))


def _expert_kernel(blk_e_ref, nvalid_ref, nused_ref, x_ref, wg_ref, wu_ref, wd_ref, y_ref, wg_s, wu_s, wd_s):
    i = pl.program_id(0)

    @pl.when(i < nused_ref[0])
    def _():
        e = blk_e_ref[i]
        prev = blk_e_ref[jnp.maximum(i - 1, 0)]

        @pl.when((i == 0) | (e != prev))
        def _():
            wg_s[...] = wg_ref[0].astype(jnp.bfloat16)
            wu_s[...] = wu_ref[0].astype(jnp.bfloat16)
            wd_s[...] = wd_ref[0].astype(jnp.bfloat16)

        row = lax.broadcasted_iota(jnp.int32, (BM, 1), 0)
        xa, xb = _unpack_rows_native(jnp.where(row < nvalid_ref[i], x_ref[...], 0))
        xa = xa.astype(jnp.bfloat16)
        xb = xb.astype(jnp.bfloat16)
        half = xa.shape[1]
        g = _bdot(xa, wg_s[:half, :]) + _bdot(xb, wg_s[half:, :])
        u = _bdot(xa, wu_s[:half, :]) + _bdot(xb, wu_s[half:, :])
        a = (g * jax.nn.sigmoid(g)) * u
        y_ref[...] = _pack_rows_native(_bdot(a.astype(jnp.bfloat16), wd_s[...]))


def _expert_call(x_sorted, blk_e, n_valid, n_used, w_gate, w_up, w_down):
    n_pad, DH = x_sorted.shape
    n_blk = n_pad // BM
    E, D, F = w_gate.shape

    def row_map(i, be, nv, nu):
        return (jnp.minimum(i, nu[0] - 1), 0)

    def w_map(i, be, nv, nu):
        return (be[jnp.minimum(i, nu[0] - 1)], 0, 0)

    return pl.pallas_call(
        _expert_kernel,
        grid_spec=pltpu.PrefetchScalarGridSpec(
            num_scalar_prefetch=3,
            grid=(n_blk,),
            in_specs=[pl.BlockSpec((BM, DH), row_map),
                      pl.BlockSpec((1, D, F), w_map),
                      pl.BlockSpec((1, D, F), w_map),
                      pl.BlockSpec((1, F, D), w_map)],
            out_specs=pl.BlockSpec((BM, DH), row_map),
            scratch_shapes=[pltpu.VMEM((D, F), jnp.bfloat16), pltpu.VMEM((D, F), jnp.bfloat16),
                            pltpu.VMEM((F, D), jnp.bfloat16)]),
        out_shape=jax.ShapeDtypeStruct((n_pad, DH), jnp.int32),
        compiler_params=pltpu.CompilerParams(dimension_semantics=("arbitrary",),
                                             vmem_limit_bytes=VMEM_LIMIT),
    )(blk_e, n_valid, n_used, x_sorted, w_gate, w_up, w_down)


TC = 128


SC_CORES = 2
SC_SUBCORES = 16
SC_CHUNK = 128
COMBINE_PARTS = 2


def _sc_gather_call(table, idx):
    n_idx = idx.shape[0]
    width = table.shape[1]
    n_workers = SC_CORES * SC_SUBCORES
    per_worker = n_idx // n_workers
    assert per_worker * n_workers == n_idx and per_worker % SC_CHUNK == 0
    mesh = plsc.VectorSubcoreMesh(core_axis_name="c", subcore_axis_name="s")

    def body(table_hbm, idx_hbm, out_hbm, idx_v, rows_v, sem):
        wid = lax.axis_index("s") * SC_CORES + lax.axis_index("c")
        base = wid * per_worker

        @pl.loop(0, per_worker // SC_CHUNK)
        def _(ch):
            off = base + ch * SC_CHUNK
            pltpu.sync_copy(idx_hbm.at[pl.ds(off, SC_CHUNK)], idx_v)
            pltpu.async_copy(table_hbm.at[idx_v], rows_v, sem).wait()
            pltpu.sync_copy(rows_v, out_hbm.at[pl.ds(off, SC_CHUNK)])

    return pl.kernel(
        body, out_type=jax.ShapeDtypeStruct((n_idx, width), table.dtype), mesh=mesh,
        scratch_types=[pltpu.VMEM((SC_CHUNK,), jnp.int32), pltpu.VMEM((SC_CHUNK, width), table.dtype),
                       pltpu.SemaphoreType.DMA],
    )(table, idx)


def _sc_scatter_call(src, pos, n_out):
    n_idx = pos.shape[0]
    width = src.shape[1]
    n_workers = SC_CORES * SC_SUBCORES
    per_worker = n_idx // n_workers
    assert per_worker * n_workers == n_idx and per_worker % SC_CHUNK == 0 and TR % SC_CHUNK == 0
    mesh = plsc.VectorSubcoreMesh(core_axis_name="c", subcore_axis_name="s")
    tile_pairs = TOP_K * TR

    def body(src_hbm, pos_hbm, out_hbm, idx_v, rows_v, sem):
        wid = lax.axis_index("s") * SC_CORES + lax.axis_index("c")
        base = wid * per_worker

        @pl.loop(0, per_worker // SC_CHUNK)
        def _(ch):
            off = base + ch * SC_CHUNK
            row0 = (off // tile_pairs) * TR + off % TR
            pltpu.sync_copy(pos_hbm.at[pl.ds(off, SC_CHUNK)], idx_v)
            pltpu.sync_copy(src_hbm.at[pl.ds(row0, SC_CHUNK)], rows_v)
            pltpu.async_copy(rows_v, out_hbm.at[idx_v], sem).wait()

    return pl.kernel(
        body, out_type=jax.ShapeDtypeStruct((n_out, width), src.dtype), mesh=mesh,
        scratch_types=[pltpu.VMEM((SC_CHUNK,), jnp.int32), pltpu.VMEM((SC_CHUNK, width), src.dtype),
                       pltpu.SemaphoreType.DMA],
    )(src, pos)


def _positions_kernel(idx_ref, rank_ref, pstart_ref, pos_ref):
    erow = lax.broadcasted_iota(jnp.int32, (N_EXPERTS, TR), 0)
    pstart = pstart_ref[...]
    for k in range(TOP_K):
        hit = erow == idx_ref[0, k:k + 1, :]
        seg = jnp.sum(jnp.where(hit, pstart, 0), axis=0, keepdims=True)
        pos_ref[0, k:k + 1, :] = seg + rank_ref[0, k:k + 1, :]


def _positions_call(idx, rank, pad_start):
    n_tiles = idx.shape[0]
    row_spec = pl.BlockSpec((1, TOP_K, TR), lambda i: (i, 0, 0))
    return pl.pallas_call(
        _positions_kernel,
        grid=(n_tiles,),
        in_specs=[row_spec, row_spec, pl.BlockSpec((N_EXPERTS, 1), lambda i: (0, 0))],
        out_specs=row_spec,
        out_shape=jax.ShapeDtypeStruct((n_tiles, TOP_K, TR), jnp.int32),
    )(idx, rank, pad_start.reshape(N_EXPERTS, 1))


def _combine_dense_kernel(g_ref, w_ref, x1_ref, sh_ref, mod_ref, fg_ref, *rest):
    out_ref = rest[-1]
    w = w_ref[...]
    acc_a = acc_b = None
    for k in range(TOP_K):
        ya, yb = _unpack_rows_native(g_ref[0, k])
        acc_a = w[:, k:k + 1] * ya if k == 0 else acc_a + w[:, k:k + 1] * ya
        acc_b = w[:, k:k + 1] * yb if k == 0 else acc_b + w[:, k:k + 1] * yb
    sa, sb = _unpack_rows_native(sh_ref[...])
    moe = jnp.concatenate([acc_a + sa, acc_b + sb], axis=1)
    x2 = x1_ref[...] + mod_ref[0, 5:6, :] * moe
    out_ref[...] = x2 * lax.rsqrt(jnp.mean(x2 * x2, axis=-1, keepdims=True) + EPS) * fg_ref[...]


def _combine_dense_call(gathered, wts, x1, shared, mod, final_g, n_ctx, lat_len, tile0, out_rows, out_tile0,
                        prev_out=None):
    n_tiles, K, _, DH = gathered.shape
    D = 2 * DH
    row = functools.partial(_mod_row, tokens_per_tile=TC, n_ctx=n_ctx, lat_len=lat_len)
    operands = [gathered, wts, x1, shared, mod, final_g.reshape(1, D)]
    in_specs = [pl.BlockSpec((1, K, TC, DH), lambda j: (j, 0, 0, 0)),
                pl.BlockSpec((TC, K), lambda j: (tile0 + j, 0)),
                pl.BlockSpec((TC, D), lambda j: (tile0 + j, 0)),
                pl.BlockSpec((TC, DH), lambda j: (tile0 + j, 0)),
                pl.BlockSpec((1, N_MOD, D), lambda j: (row(tile0 + j), 0, 0)),
                pl.BlockSpec((1, D), lambda j: (0, 0))]
    aliases = {}
    if prev_out is not None:
        aliases = {len(operands): 0}
        operands.append(prev_out)
        in_specs.append(pl.BlockSpec(memory_space=pl.ANY))
    return pl.pallas_call(
        _combine_dense_kernel,
        grid=(n_tiles,),
        in_specs=in_specs,
        out_specs=pl.BlockSpec((TC, D), lambda j: (out_tile0 + j, 0)),
        out_shape=jax.ShapeDtypeStruct((out_rows, D), jnp.float32),
        input_output_aliases=aliases,
        compiler_params=pltpu.CompilerParams(dimension_semantics=("arbitrary",)),
    )(*operands)


def _moe_routed(logits, h2p, router_bias, w_gate, w_up, w_down):
    T = logits.shape[0]
    idx, rank, w_rows, cnt = _route_call(logits, router_bias)
    wts = w_rows.transpose(0, 2, 1).reshape(T, TOP_K)
    counts = cnt[:, 0]
    padded = (counts + BM - 1) // BM * BM
    pad_end = jnp.cumsum(padded)
    pad_start = (pad_end - padded).astype(jnp.int32)
    n_blk = -(-(T * TOP_K) // BM) + N_EXPERTS
    n_pad = n_blk * BM
    n_used = (pad_end[-1] // BM).astype(jnp.int32).reshape(1)
    pos = _positions_call(idx, rank, pad_start)
    x_sorted = _sc_scatter_call(h2p, pos.reshape(-1), n_pad)
    blk_row0 = jnp.arange(n_blk, dtype=jnp.int32) * BM
    blk_e = jnp.minimum(jnp.sum((pad_end[None, :] <= blk_row0[:, None]).astype(jnp.int32), axis=1), N_EXPERTS - 1)
    own = blk_e[:, None] == jnp.arange(N_EXPERTS, dtype=jnp.int32)[None, :]
    seg_end = jnp.sum(jnp.where(own, (pad_start + counts)[None, :], 0), axis=1)
    n_valid = jnp.clip(seg_end - blk_row0, 0, BM).astype(jnp.int32)
    y = _expert_call(x_sorted, blk_e, n_valid, n_used, w_gate, w_up, w_down)
    pos_t = pos.reshape(T // TR, TOP_K, TR // TC, TC).transpose(0, 2, 1, 3).reshape(T // TC, TOP_K * TC)
    return y, pos_t, wts


def kernel(x_prompt, x_sample, state_delta, c, c_ctx, w_ada, b_ada, norm1_g, w_in, conv_w, a_log,
           dt_bias, onorm_g, pool_w, pool_scale, w_out, norm2_g, router_w, router_bias, exp_w_gate,
           exp_w_up, exp_w_down, sh_w_gate, sh_w_up, sh_w_down, final_g):
    Bc, Lc, D = x_prompt.shape
    Bl, Ll, _ = x_sample.shape
    n_ctx = Bc * Lc
    assert DEPTH == 1 and 1 + Bl <= MOD_ROWS and n_ctx % Ll == 0
    x_parts = (x_prompt.reshape(n_ctx, D), x_sample.reshape(Bl * Ll, D))
    cvec = jnp.concatenate([c_ctx[None], c, jnp.zeros((MOD_ROWS - 1 - Bl, D), c.dtype)], axis=0)
    l = 0
    mod = _ada_call(cvec, w_ada[l], b_ada[l]).reshape(MOD_ROWS, N_MOD, D)
    qkv, z, ba, u = _inproj_call(*x_parts, mod, norm1_g[l], w_in[l], Ll)
    dn = (conv_w[l], a_log[l], dt_bias[l], onorm_g[l])
    oa_c, st_ctx = _delta_call(qkv, z, ba, *dn, None, Bc, Lc, 0)
    oa_l, _ = _delta_call(qkv, z, ba, *dn, state_delta[:, l], Bl, Ll, n_ctx // Ll)
    op_c = _pool_call(u, pool_w[l], pool_scale[l], False, Bc, Lc, 0)
    op_l = _pool_call(u, pool_w[l], pool_scale[l], True, Bl, Ll, n_ctx // Ll)
    x1, logits, h2p, shared = _outproj_call(x_parts, (oa_c, oa_l), (op_c, op_l), mod, norm2_g[l], w_out[l],
                                            sh_w_gate[l], sh_w_up[l], sh_w_down[l], router_w[l], Ll)
    y, pos_t, wts = _moe_routed(logits, h2p, router_bias[l], exp_w_gate[l], exp_w_up[l], exp_w_down[l])
    nct = n_ctx // TC
    outs = []
    for lo, hi in ((0, nct), (nct, pos_t.shape[0])):
        step = (hi - lo) // COMBINE_PARTS
        out = None
        for s in range(COMBINE_PARTS):
            a = lo + s * step
            gathered = _sc_gather_call(y, pos_t[a:a + step].reshape(-1))
            out = _combine_dense_call(gathered.reshape(step, TOP_K, TC, D // 2), wts, x1, shared, mod, final_g,
                                      n_ctx, Ll, a, (hi - lo) * TC, s * step, out)
        outs.append(out)
    y_prompt = outs[0].reshape(Bc, Lc, D)
    y_sample = outs[1].reshape(Bl, Ll, D)
    new_state_delta = st_ctx[:, None].astype(x_prompt.dtype)
    return (y_prompt, y_sample, new_state_delta)
```
